```python
import jax, jax.numpy as jnp
from jax import lax
import numpy as np

D_MODEL = 1024
BATCH = 8
SEQ = 8192
DEPTH = 2

N_META = 16
CHUNK = 64
PAD = CHUNK - N_META
D_FF = 2816
EPS = 1e-6
N_MIXERS = 2
N_RET = (DEPTH + 1) // 2
N_GLA = DEPTH // 2

RET_HEADS = 4
RET_DK = D_MODEL // RET_HEADS
RET_DV = 2 * D_MODEL // RET_HEADS
RET_IN = 2 * D_MODEL + 4 * D_MODEL
ROPE_BASE = 10000.0

GLA_HEADS = 4
GLA_DK = D_MODEL // 2 // GLA_HEADS
GLA_DV = D_MODEL // GLA_HEADS
GLA_RANK = 16
GLA_TAU = 16.0
GLA_HK = GLA_HEADS * GLA_DK
GLA_HV = GLA_HEADS * GLA_DV
GLA_IN = 2 * GLA_HK + 2 * GLA_HV + GLA_RANK

kernel_name = "hybrid_retnet_gla_macaron_meta"


def rmsnorm(x, g):
    xf = x.astype(jnp.float32)
    y = xf * lax.rsqrt(jnp.mean(xf * xf, axis=-1, keepdims=True) + EPS)
    return (y * g).astype(x.dtype)


def swiglu(x, w_in, w_out):
    gate, up = jnp.split(x @ w_in, 2, axis=-1)
    return (jax.nn.silu(gate) * up) @ w_out


def to_chunks(t):
    b, T, h, d = t.shape
    n = (T + PAD) // CHUNK
    t = jnp.pad(t, ((0, 0), (PAD, 0), (0, 0), (0, 0)))
    return t.reshape(b, n, CHUNK, h, d).transpose(1, 0, 3, 2, 4)


def from_chunks(t):
    n, b, h, c, d = t.shape
    return t.transpose(1, 0, 3, 2, 4).reshape(b, n * c, h, d)[:, PAD:]


def rotary(t, pos):
    half = t.shape[-1] // 2
    inv = 1.0 / (ROPE_BASE ** jnp.linspace(0.0, 1.0, half, dtype=jnp.float32))
    ang = pos.astype(jnp.float32)[:, None] * inv[None, :]
    cos = jnp.cos(ang)[None, :, None, :].astype(t.dtype)
    sin = jnp.sin(ang)[None, :, None, :].astype(t.dtype)
    t1, t2 = t[..., :half], t[..., half:]
    return jnp.concatenate([t1 * cos - t2 * sin, t1 * sin + t2 * cos], axis=-1)


def retention(h, w_in, head_norm, w_out):
    b, T, _ = h.shape
    q, k, v, g = jnp.split(h @ w_in, [D_MODEL, 2 * D_MODEL, 4 * D_MODEL], axis=-1)
    pos = jnp.arange(T)
    q = rotary(q.reshape(b, T, RET_HEADS, RET_DK), pos)
    k = rotary(k.reshape(b, T, RET_HEADS, RET_DK), pos) * (RET_DK ** -0.5)
    v = v.reshape(b, T, RET_HEADS, RET_DV)

    log_gamma = jnp.log1p(-2.0 ** (-5.0 - jnp.arange(RET_HEADS, dtype=jnp.float32)))
    idx = jnp.arange(CHUNK, dtype=jnp.float32)
    rel = idx[:, None] - idx[None, :]
    decay_intra = jnp.where(rel >= 0, jnp.exp(log_gamma[:, None, None] * jnp.maximum(rel, 0.0)), 0.0)
    decay_q = jnp.exp(log_gamma[:, None] * (idx + 1.0))[..., None]
    decay_k = jnp.exp(log_gamma[:, None] * (CHUNK - 1.0 - idx))[..., None]
    decay_chunk = jnp.exp(log_gamma * CHUNK)[:, None, None]

    def step(S, inp):
        qi, ki, vi = inp
        scores = jnp.einsum('bhid,bhjd->bhij', qi, ki) * decay_intra
        o = (jnp.einsum('bhij,bhjv->bhiv', scores, vi)
             + jnp.einsum('bhid,bhdv->bhiv', qi * decay_q, S))
        S = S * decay_chunk + jnp.einsum('bhjd,bhjv->bhdv', ki * decay_k, vi)
        return S, o

    S0 = jnp.zeros((b, RET_HEADS, RET_DK, RET_DV), jnp.float32)
    _, o = lax.scan(step, S0, (to_chunks(q), to_chunks(k), to_chunks(v)))
    o = rmsnorm(from_chunks(o), head_norm)
    o = o.reshape(b, T, RET_HEADS * RET_DV) * jax.nn.silu(g)
    return o @ w_out


def gla(h, w_in, w_gate, b_gate, head_norm, w_out):
    b, T, _ = h.shape
    q, k, v, g, z = jnp.split(h @ w_in, [GLA_HK, 2 * GLA_HK, 2 * GLA_HK + GLA_HV, 2 * GLA_HK + 2 * GLA_HV], axis=-1)
    q = q.reshape(b, T, GLA_HEADS, GLA_DK) * (GLA_DK ** -0.5)
    k = k.reshape(b, T, GLA_HEADS, GLA_DK)
    v = v.reshape(b, T, GLA_HEADS, GLA_DV)
    log_a = jax.nn.log_sigmoid((z @ w_gate + b_gate).astype(jnp.float32)) / GLA_TAU
    log_a = log_a.reshape(b, T, GLA_HEADS, GLA_DK)
    causal = jnp.tril(jnp.ones((CHUNK, CHUNK), dtype=bool))[:, :, None]

    def step(S, inp):
        qi, ki, vi, ai = inp
        bcum = jnp.cumsum(ai, axis=2)
        diff = bcum[:, :, :, None, :] - bcum[:, :, None, :, :]
        dec = jnp.exp(jnp.where(causal, diff, -jnp.inf))
        scores = jnp.einsum('bhid,bhijd,bhjd->bhij', qi, dec, ki)
        o = (jnp.einsum('bhij,bhjv->bhiv', scores, vi)
             + jnp.einsum('bhid,bhdv->bhiv', qi * jnp.exp(bcum), S))
        btot = bcum[:, :, -1:, :]
        S = (S * jnp.exp(btot)[:, :, 0, :, None]
             + jnp.einsum('bhjd,bhjv->bhdv', ki * jnp.exp(btot - bcum), vi))
        return S, o

    S0 = jnp.zeros((b, GLA_HEADS, GLA_DK, GLA_DV), jnp.float32)
    _, o = lax.scan(step, S0, (to_chunks(q), to_chunks(k), to_chunks(v), to_chunks(log_a)))
    o = rmsnorm(from_chunks(o), head_norm)
    o = o.reshape(b, T, GLA_HV) * jax.nn.silu(g)
    return o @ w_out


def _fwd_setup_inputs(seed: int = 0) -> dict:
    key = jax.random.key(seed)
    ks = jax.random.split(key, 20)
    nrm = lambda k, shape, fan_in: jax.random.normal(k, shape, jnp.float32) * (fan_in ** -0.5)
    gain = lambda k, shape: 1.0 + 0.01 * jax.random.normal(k, shape, jnp.float32)
    return {
        "x": jax.random.normal(ks[0], (BATCH, SEQ, D_MODEL), jnp.float32),
        "meta_tokens": jax.random.normal(ks[1], (N_META, D_MODEL), jnp.float32),
        "norm_ffn1": gain(ks[2], (DEPTH, D_MODEL)),
        "ffn1_w_in": nrm(ks[3], (DEPTH, D_MODEL, 2 * D_FF), D_MODEL),
        "ffn1_w_out": nrm(ks[4], (DEPTH, D_FF, D_MODEL), D_FF),
        "norm_mix": gain(ks[5], (DEPTH, D_MODEL)),
        "norm_ffn2": gain(ks[6], (DEPTH, D_MODEL)),
        "ffn2_w_in": nrm(ks[7], (DEPTH, D_MODEL, 2 * D_FF), D_MODEL),
        "ffn2_w_out": nrm(ks[8], (DEPTH, D_FF, D_MODEL), D_FF),
        "ret_w_in": nrm(ks[9], (N_RET, D_MODEL, RET_IN), D_MODEL),
        "ret_head_norm": gain(ks[10], (N_RET, RET_HEADS, RET_DV)),
        "ret_w_out": nrm(ks[11], (N_RET, RET_HEADS * RET_DV, D_MODEL), RET_HEADS * RET_DV),
        "gla_w_in": nrm(ks[12], (N_GLA, D_MODEL, GLA_IN), D_MODEL),
        "gla_w_gate": nrm(ks[13], (N_GLA, GLA_RANK, GLA_HK), GLA_RANK),
        "gla_b_gate": 0.1 * jax.random.normal(ks[14], (N_GLA, GLA_HK), jnp.float32),
        "gla_head_norm": gain(ks[15], (N_GLA, GLA_HEADS, GLA_DV)),
        "gla_w_out": nrm(ks[16], (N_GLA, GLA_HV, D_MODEL), GLA_HV),
        "final_norm": gain(ks[17], (D_MODEL,)),
    }


def _fwd_reference(x, meta_tokens, norm_ffn1, ffn1_w_in, ffn1_w_out, norm_mix, norm_ffn2,
              ffn2_w_in, ffn2_w_out, ret_w_in, ret_head_norm, ret_w_out,
              gla_w_in, gla_w_gate, gla_b_gate, gla_head_norm, gla_w_out, final_norm):
    b = x.shape[0]
    meta = jnp.broadcast_to(meta_tokens[None].astype(x.dtype), (b, N_META, D_MODEL))
    h = jnp.concatenate([meta, x], axis=1)
    for i in range(DEPTH):
        j = i // N_MIXERS
        h = h + 0.5 * swiglu(rmsnorm(h, norm_ffn1[i]), ffn1_w_in[i], ffn1_w_out[i])
        hn = rmsnorm(h, norm_mix[i])
        if i % N_MIXERS == 0:
            mix = retention(hn, ret_w_in[j], ret_head_norm[j], ret_w_out[j])
        else:
            mix = gla(hn, gla_w_in[j], gla_w_gate[j], gla_b_gate[j], gla_head_norm[j], gla_w_out[j])
        h = h + mix
        h = h + 0.5 * swiglu(rmsnorm(h, norm_ffn2[i]), ffn2_w_in[i], ffn2_w_out[i])
    h = rmsnorm(h, final_norm)
    return h[:, N_META:]


import jax as _jax
import jax.numpy as _jnp

TWIN_FORMAT = 'train_step'
FWD_PARAMS = ['x', 'meta_tokens', 'norm_ffn1', 'ffn1_w_in', 'ffn1_w_out', 'norm_mix', 'norm_ffn2', 'ffn2_w_in', 'ffn2_w_out', 'ret_w_in', 'ret_head_norm', 'ret_w_out', 'gla_w_in', 'gla_w_gate', 'gla_b_gate', 'gla_head_norm', 'gla_w_out', 'final_norm']
TWIN_WEIGHTS = ['meta_tokens', 'norm_ffn1', 'ffn1_w_in', 'ffn1_w_out', 'norm_mix', 'norm_ffn2', 'ffn2_w_in', 'ffn2_w_out', 'ret_w_in', 'ret_head_norm', 'ret_w_out', 'gla_w_in', 'gla_w_gate', 'gla_b_gate', 'gla_head_norm', 'gla_w_out', 'final_norm']
TWIN_DIFF_INPUT = 'x'
TWIN_INPUTS = ['x', 'meta_tokens', 'norm_ffn1', 'ffn1_w_in', 'ffn1_w_out', 'norm_mix', 'norm_ffn2', 'ffn2_w_in', 'ffn2_w_out', 'ret_w_in', 'ret_head_norm', 'ret_w_out', 'gla_w_in', 'gla_w_gate', 'gla_b_gate', 'gla_head_norm', 'gla_w_out', 'final_norm', 'loss_target', 'm_meta_tokens', 'm_norm_ffn1', 'm_ffn1_w_in', 'm_ffn1_w_out', 'm_norm_mix', 'm_norm_ffn2', 'm_ffn2_w_in', 'm_ffn2_w_out', 'm_ret_w_in', 'm_ret_head_norm', 'm_ret_w_out', 'm_gla_w_in', 'm_gla_w_gate', 'm_gla_b_gate', 'm_gla_head_norm', 'm_gla_w_out', 'm_final_norm', 'v_meta_tokens', 'v_norm_ffn1', 'v_ffn1_w_in', 'v_ffn1_w_out', 'v_norm_mix', 'v_norm_ffn2', 'v_ffn2_w_in', 'v_ffn2_w_out', 'v_ret_w_in', 'v_ret_head_norm', 'v_ret_w_out', 'v_gla_w_in', 'v_gla_w_gate', 'v_gla_b_gate', 'v_gla_head_norm', 'v_gla_w_out', 'v_final_norm']
TWIN_OUTPUTS = ['loss', 'grad_x', 'grad_meta_tokens', 'grad_norm_ffn1', 'grad_ffn1_w_in', 'grad_ffn1_w_out', 'grad_norm_mix', 'grad_norm_ffn2', 'grad_ffn2_w_in', 'grad_ffn2_w_out', 'grad_ret_w_in', 'grad_ret_head_norm', 'grad_ret_w_out', 'grad_gla_w_in', 'grad_gla_w_gate', 'grad_gla_b_gate', 'grad_gla_head_norm', 'grad_gla_w_out', 'grad_final_norm', 'delta_meta_tokens', 'delta_norm_ffn1', 'delta_ffn1_w_in', 'delta_ffn1_w_out', 'delta_norm_mix', 'delta_norm_ffn2', 'delta_ffn2_w_in', 'delta_ffn2_w_out', 'delta_ret_w_in', 'delta_ret_head_norm', 'delta_ret_w_out', 'delta_gla_w_in', 'delta_gla_w_gate', 'delta_gla_b_gate', 'delta_gla_head_norm', 'delta_gla_w_out', 'delta_final_norm', 'new_m_meta_tokens', 'new_m_norm_ffn1', 'new_m_ffn1_w_in', 'new_m_ffn1_w_out', 'new_m_norm_mix', 'new_m_norm_ffn2', 'new_m_ffn2_w_in', 'new_m_ffn2_w_out', 'new_m_ret_w_in', 'new_m_ret_head_norm', 'new_m_ret_w_out', 'new_m_gla_w_in', 'new_m_gla_w_gate', 'new_m_gla_b_gate', 'new_m_gla_head_norm', 'new_m_gla_w_out', 'new_m_final_norm', 'new_v_meta_tokens', 'new_v_norm_ffn1', 'new_v_ffn1_w_in', 'new_v_ffn1_w_out', 'new_v_norm_mix', 'new_v_norm_ffn2', 'new_v_ffn2_w_in', 'new_v_ffn2_w_out', 'new_v_ret_w_in', 'new_v_ret_head_norm', 'new_v_ret_w_out', 'new_v_gla_w_in', 'new_v_gla_w_gate', 'new_v_gla_b_gate', 'new_v_gla_head_norm', 'new_v_gla_w_out', 'new_v_final_norm']
TWIN_LEAF_KINDS = {'loss': 'loss', 'grad_x': 'grad_x', 'grad_meta_tokens': 'grad_w', 'grad_norm_ffn1': 'grad_w', 'grad_ffn1_w_in': 'grad_w', 'grad_ffn1_w_out': 'grad_w', 'grad_norm_mix': 'grad_w', 'grad_norm_ffn2': 'grad_w', 'grad_ffn2_w_in': 'grad_w', 'grad_ffn2_w_out': 'grad_w', 'grad_ret_w_in': 'grad_w', 'grad_ret_head_norm': 'grad_w', 'grad_ret_w_out': 'grad_w', 'grad_gla_w_in': 'grad_w', 'grad_gla_w_gate': 'grad_w', 'grad_gla_b_gate': 'grad_w', 'grad_gla_head_norm': 'grad_w', 'grad_gla_w_out': 'grad_w', 'grad_final_norm': 'grad_w', 'delta_meta_tokens': 'delta_w', 'delta_norm_ffn1': 'delta_w', 'delta_ffn1_w_in': 'delta_w', 'delta_ffn1_w_out': 'delta_w', 'delta_norm_mix': 'delta_w', 'delta_norm_ffn2': 'delta_w', 'delta_ffn2_w_in': 'delta_w', 'delta_ffn2_w_out': 'delta_w', 'delta_ret_w_in': 'delta_w', 'delta_ret_head_norm': 'delta_w', 'delta_ret_w_out': 'delta_w', 'delta_gla_w_in': 'delta_w', 'delta_gla_w_gate': 'delta_w', 'delta_gla_b_gate': 'delta_w', 'delta_gla_head_norm': 'delta_w', 'delta_gla_w_out': 'delta_w', 'delta_final_norm': 'delta_w', 'new_m_meta_tokens': 'new_m', 'new_m_norm_ffn1': 'new_m', 'new_m_ffn1_w_in': 'new_m', 'new_m_ffn1_w_out': 'new_m', 'new_m_norm_mix': 'new_m', 'new_m_norm_ffn2': 'new_m', 'new_m_ffn2_w_in': 'new_m', 'new_m_ffn2_w_out': 'new_m', 'new_m_ret_w_in': 'new_m', 'new_m_ret_head_norm': 'new_m', 'new_m_ret_w_out': 'new_m', 'new_m_gla_w_in': 'new_m', 'new_m_gla_w_gate': 'new_m', 'new_m_gla_b_gate': 'new_m', 'new_m_gla_head_norm': 'new_m', 'new_m_gla_w_out': 'new_m', 'new_m_final_norm': 'new_m', 'new_v_meta_tokens': 'new_v', 'new_v_norm_ffn1': 'new_v', 'new_v_ffn1_w_in': 'new_v', 'new_v_ffn1_w_out': 'new_v', 'new_v_norm_mix': 'new_v', 'new_v_norm_ffn2': 'new_v', 'new_v_ffn2_w_in': 'new_v', 'new_v_ffn2_w_out': 'new_v', 'new_v_ret_w_in': 'new_v', 'new_v_ret_head_norm': 'new_v', 'new_v_ret_w_out': 'new_v', 'new_v_gla_w_in': 'new_v', 'new_v_gla_w_gate': 'new_v', 'new_v_gla_b_gate': 'new_v', 'new_v_gla_head_norm': 'new_v', 'new_v_gla_w_out': 'new_v', 'new_v_final_norm': 'new_v'}


def _forward(args):
    return _fwd_reference(*[args[k] for k in FWD_PARAMS])


def _output_shape():
    def fwd():
        inp = _fwd_setup_inputs(0)
        return _fwd_reference(*[inp[k] for k in FWD_PARAMS])
    out = _jax.eval_shape(fwd)
    return out.shape, out.dtype

N_MICROBATCH = 1
ADAM_LR = 0.001
ADAM_B1 = 0.9
ADAM_B2 = 0.999
ADAM_EPS = 1e-08
ADAM_WD = 0.01
ADAM_STEP = 10
PER_EXAMPLE_BATCH_AXIS = {'x': 0, 'loss_target': 0}
SHARED_INPUTS = []
_WEIGHT_DTYPES = {'meta_tokens': _jnp.float32, 'norm_ffn1': _jnp.float32, 'ffn1_w_in': _jnp.float32, 'ffn1_w_out': _jnp.float32, 'norm_mix': _jnp.float32, 'norm_ffn2': _jnp.float32, 'ffn2_w_in': _jnp.float32, 'ffn2_w_out': _jnp.float32, 'ret_w_in': _jnp.float32, 'ret_head_norm': _jnp.float32, 'ret_w_out': _jnp.float32, 'gla_w_in': _jnp.float32, 'gla_w_gate': _jnp.float32, 'gla_b_gate': _jnp.float32, 'gla_head_norm': _jnp.float32, 'gla_w_out': _jnp.float32, 'final_norm': _jnp.float32}
MOMENT_SCALE = {'meta_tokens': 2.143368e-02, 'norm_ffn1': 1.538178e-01, 'ffn1_w_in': 6.440216e-02, 'ffn1_w_out': 1.050401e-01, 'norm_mix': 2.954671e-01, 'norm_ffn2': 1.011757e-01, 'ffn2_w_in': 4.112419e-02, 'ffn2_w_out': 6.716155e-02, 'ret_w_in': 1.395036e-01, 'ret_head_norm': 1.185941e-01, 'ret_w_out': 1.686307e-01, 'gla_w_in': 1.282828e-01, 'gla_w_gate': 1.721997e-02, 'gla_b_gate': 7.720220e-02, 'gla_head_norm': 1.085900e-01, 'gla_w_out': 1.087064e-01, 'final_norm': 6.395484e+01}


def _to_microbatches(a, axis):
    t = _jnp.moveaxis(a, axis, 0)
    t = t.reshape((N_MICROBATCH, t.shape[0] // N_MICROBATCH) + t.shape[1:])
    return _jnp.moveaxis(t, 1, axis + 1)


def setup_inputs(seed: int = 0) -> dict:
    inp = _fwd_setup_inputs(seed)
    key = _jax.random.fold_in(_jax.random.key(seed), 7919)
    shape, _ = _output_shape()
    out = dict(inp)
    out["loss_target"] = _jax.random.normal(_jax.random.fold_in(key, 0), shape, _jnp.float32)
    for i, name in enumerate(TWIN_WEIGHTS):
        w = inp[name].astype(_jnp.float32)
        if MOMENT_SCALE is None:
            s = _jnp.sqrt(_jnp.mean(_jnp.square(w)) + 1e-30)
        else:
            s = MOMENT_SCALE[name]
        km, kv = _jax.random.split(_jax.random.fold_in(key, i + 1))
        out[name] = w
        out["m_" + name] = s * _jax.random.normal(km, w.shape, _jnp.float32)
        out["v_" + name] = (s * s) * _jax.random.uniform(kv, w.shape, _jnp.float32, 0.5, 1.5)
    if N_MICROBATCH > 1:
        for name, axis in PER_EXAMPLE_BATCH_AXIS.items():
            out[name] = _to_microbatches(out[name], axis)
    return {'x': out['x'], 'meta_tokens': out['meta_tokens'], 'norm_ffn1': out['norm_ffn1'], 'ffn1_w_in': out['ffn1_w_in'], 'ffn1_w_out': out['ffn1_w_out'], 'norm_mix': out['norm_mix'], 'norm_ffn2': out['norm_ffn2'], 'ffn2_w_in': out['ffn2_w_in'], 'ffn2_w_out': out['ffn2_w_out'], 'ret_w_in': out['ret_w_in'], 'ret_head_norm': out['ret_head_norm'], 'ret_w_out': out['ret_w_out'], 'gla_w_in': out['gla_w_in'], 'gla_w_gate': out['gla_w_gate'], 'gla_b_gate': out['gla_b_gate'], 'gla_head_norm': out['gla_head_norm'], 'gla_w_out': out['gla_w_out'], 'final_norm': out['final_norm'], 'loss_target': out['loss_target'], 'm_meta_tokens': out['m_meta_tokens'], 'm_norm_ffn1': out['m_norm_ffn1'], 'm_ffn1_w_in': out['m_ffn1_w_in'], 'm_ffn1_w_out': out['m_ffn1_w_out'], 'm_norm_mix': out['m_norm_mix'], 'm_norm_ffn2': out['m_norm_ffn2'], 'm_ffn2_w_in': out['m_ffn2_w_in'], 'm_ffn2_w_out': out['m_ffn2_w_out'], 'm_ret_w_in': out['m_ret_w_in'], 'm_ret_head_norm': out['m_ret_head_norm'], 'm_ret_w_out': out['m_ret_w_out'], 'm_gla_w_in': out['m_gla_w_in'], 'm_gla_w_gate': out['m_gla_w_gate'], 'm_gla_b_gate': out['m_gla_b_gate'], 'm_gla_head_norm': out['m_gla_head_norm'], 'm_gla_w_out': out['m_gla_w_out'], 'm_final_norm': out['m_final_norm'], 'v_meta_tokens': out['v_meta_tokens'], 'v_norm_ffn1': out['v_norm_ffn1'], 'v_ffn1_w_in': out['v_ffn1_w_in'], 'v_ffn1_w_out': out['v_ffn1_w_out'], 'v_norm_mix': out['v_norm_mix'], 'v_norm_ffn2': out['v_norm_ffn2'], 'v_ffn2_w_in': out['v_ffn2_w_in'], 'v_ffn2_w_out': out['v_ffn2_w_out'], 'v_ret_w_in': out['v_ret_w_in'], 'v_ret_head_norm': out['v_ret_head_norm'], 'v_ret_w_out': out['v_ret_w_out'], 'v_gla_w_in': out['v_gla_w_in'], 'v_gla_w_gate': out['v_gla_w_gate'], 'v_gla_b_gate': out['v_gla_b_gate'], 'v_gla_head_norm': out['v_gla_head_norm'], 'v_gla_w_out': out['v_gla_w_out'], 'v_final_norm': out['v_final_norm']}


def _loss(weights, diff, rest, loss_target):
    with _jax.named_scope("forward"):
        args = {**rest, TWIN_DIFF_INPUT: diff, **{k: w.astype(_WEIGHT_DTYPES[k]) for k, w in weights.items()}}
        y = _forward(args)
    with _jax.named_scope("loss_head"):
        err = _jnp.square(y.astype(_jnp.float32) - loss_target)
        return 0.5 * _jnp.sum(_jnp.mean(err, axis=-1)) if err.ndim else 0.5 * err


def _adamw(w, g, m, v):
    m = ADAM_B1 * m + (1.0 - ADAM_B1) * g
    v = ADAM_B2 * v + (1.0 - ADAM_B2) * _jnp.square(g)
    m_hat = m / (1.0 - ADAM_B1 ** ADAM_STEP)
    v_hat = v / (1.0 - ADAM_B2 ** ADAM_STEP)
    delta = -ADAM_LR * (m_hat / (_jnp.sqrt(v_hat) + ADAM_EPS) + ADAM_WD * w)
    return delta, m, v


def reference(x, meta_tokens, norm_ffn1, ffn1_w_in, ffn1_w_out, norm_mix, norm_ffn2, ffn2_w_in, ffn2_w_out, ret_w_in, ret_head_norm, ret_w_out, gla_w_in, gla_w_gate, gla_b_gate, gla_head_norm, gla_w_out, final_norm, loss_target, m_meta_tokens, m_norm_ffn1, m_ffn1_w_in, m_ffn1_w_out, m_norm_mix, m_norm_ffn2, m_ffn2_w_in, m_ffn2_w_out, m_ret_w_in, m_ret_head_norm, m_ret_w_out, m_gla_w_in, m_gla_w_gate, m_gla_b_gate, m_gla_head_norm, m_gla_w_out, m_final_norm, v_meta_tokens, v_norm_ffn1, v_ffn1_w_in, v_ffn1_w_out, v_norm_mix, v_norm_ffn2, v_ffn2_w_in, v_ffn2_w_out, v_ret_w_in, v_ret_head_norm, v_ret_w_out, v_gla_w_in, v_gla_w_gate, v_gla_b_gate, v_gla_head_norm, v_gla_w_out, v_final_norm):
    given = dict(x=x, meta_tokens=meta_tokens, norm_ffn1=norm_ffn1, ffn1_w_in=ffn1_w_in, ffn1_w_out=ffn1_w_out, norm_mix=norm_mix, norm_ffn2=norm_ffn2, ffn2_w_in=ffn2_w_in, ffn2_w_out=ffn2_w_out, ret_w_in=ret_w_in, ret_head_norm=ret_head_norm, ret_w_out=ret_w_out, gla_w_in=gla_w_in, gla_w_gate=gla_w_gate, gla_b_gate=gla_b_gate, gla_head_norm=gla_head_norm, gla_w_out=gla_w_out, final_norm=final_norm, loss_target=loss_target, m_meta_tokens=m_meta_tokens, m_norm_ffn1=m_norm_ffn1, m_ffn1_w_in=m_ffn1_w_in, m_ffn1_w_out=m_ffn1_w_out, m_norm_mix=m_norm_mix, m_norm_ffn2=m_norm_ffn2, m_ffn2_w_in=m_ffn2_w_in, m_ffn2_w_out=m_ffn2_w_out, m_ret_w_in=m_ret_w_in, m_ret_head_norm=m_ret_head_norm, m_ret_w_out=m_ret_w_out, m_gla_w_in=m_gla_w_in, m_gla_w_gate=m_gla_w_gate, m_gla_b_gate=m_gla_b_gate, m_gla_head_norm=m_gla_head_norm, m_gla_w_out=m_gla_w_out, m_final_norm=m_final_norm, v_meta_tokens=v_meta_tokens, v_norm_ffn1=v_norm_ffn1, v_ffn1_w_in=v_ffn1_w_in, v_ffn1_w_out=v_ffn1_w_out, v_norm_mix=v_norm_mix, v_norm_ffn2=v_norm_ffn2, v_ffn2_w_in=v_ffn2_w_in, v_ffn2_w_out=v_ffn2_w_out, v_ret_w_in=v_ret_w_in, v_ret_head_norm=v_ret_head_norm, v_ret_w_out=v_ret_w_out, v_gla_w_in=v_gla_w_in, v_gla_w_gate=v_gla_w_gate, v_gla_b_gate=v_gla_b_gate, v_gla_head_norm=v_gla_head_norm, v_gla_w_out=v_gla_w_out, v_final_norm=v_final_norm)
    weights = {n: given[n] for n in TWIN_WEIGHTS}
    shared = {n: given[n] for n in SHARED_INPUTS}
    per_example = {n: given[n] for n in ['x']}
    grad_fn = _jax.value_and_grad(_loss, argnums=(0, 1))

    def one_microbatch(ex, loss_target):
        ex = dict(ex)
        diff = ex.pop(TWIN_DIFF_INPUT)
        return grad_fn(weights, diff, {**shared, **ex}, loss_target)

    if N_MICROBATCH == 1:
        loss, (grad_w, grad_x) = one_microbatch(per_example, given["loss_target"])
    else:
        def body(carry, xs):
            loss_sum, grad_sum = carry
            l_k, (gw_k, gx_k) = one_microbatch(xs[0], xs[1])
            with _jax.named_scope("update"):
                return (loss_sum + l_k, _jax.tree.map(_jnp.add, grad_sum, gw_k)), gx_k

        init = (_jnp.zeros((), _jnp.float32), _jax.tree.map(_jnp.zeros_like, weights))
        (loss, grad_w), grad_x = _jax.lax.scan(body, init, (per_example, given["loss_target"]))
    with _jax.named_scope("update"):
        delta_w, new_m, new_v = {}, {}, {}
        for n in TWIN_WEIGHTS:
            delta_w[n], new_m[n], new_v[n] = _adamw(weights[n], grad_w[n], given["m_" + n], given["v_" + n])
    return (loss, grad_x, *[grad_w[n] for n in TWIN_WEIGHTS], *[delta_w[n] for n in TWIN_WEIGHTS],
            *[new_m[n] for n in TWIN_WEIGHTS], *[new_v[n] for n in TWIN_WEIGHTS])
```

```python
import functools
import math

import numpy as np
import jax
import jax.numpy as jnp
from jax import lax
from jax.experimental import pallas as pl
from jax.experimental.pallas import tpu as pltpu

F32 = jnp.float32
BF16 = jnp.bfloat16
S = jax.ShapeDtypeStruct
ANY = pl.BlockSpec(memory_space=pl.ANY)
MESH = pl.DeviceIdType.MESH

D = 1024
N_META = 16
CHUNK = 64
PAD = CHUNK - N_META
EPS = 1e-6
N_DEV = 8
FF_SHARD = 704
N_FF_CHUNK = 4
RET_H, RET_DK, RET_DV = 4, 256, 512
RET_HW = 2 * RET_DK + RET_DV
RET_C = 192
GLA_H, GLA_DK, GLA_DV, GLA_RANK, GLA_TAU = 4, 128, 256, 16, 16.0
GLA_HW = 2 * GLA_DK + GLA_DV
GLA_N = 3200
GLA_ZBLK = 3072 // 128
SUB = 16
ROPE_BASE = 10000.0
ADAM_LR, ADAM_B1, ADAM_B2, ADAM_EPS, ADAM_WD, ADAM_STEP = 0.001, 0.9, 0.999, 1e-08, 0.01, 10
VMEM_LIMIT = 58 * 1024 * 1024


def _cp(**kw):
    return pltpu.CompilerParams(vmem_limit_bytes=VMEM_LIMIT, **kw)


def _row_tile(t, cap):
    best = 16
    for d in range(16, cap + 1, 16):
        if t % d == 0:
            best = d
    return best


def _dot(a, b):
    return jnp.dot(a, b, preferred_element_type=F32)


def _dot_nt(a, b):
    return lax.dot_general(a, b, (((1,), (1,)), ((), ())), preferred_element_type=F32)


def _dot_tn(a, b):
    return lax.dot_general(a, b, (((0,), (0,)), ((), ())), preferred_element_type=F32)


def _sigmoid(x):
    return 1.0 / (1.0 + jnp.exp(-x))


def _rms_bwd(dxn, x, gain):
    r = lax.rsqrt(jnp.mean(x * x, axis=-1, keepdims=True) + EPS)
    xh = x * r
    dxh = dxn * gain
    dx = r * (dxh - xh * jnp.mean(dxh * xh, axis=-1, keepdims=True))
    return dx, jnp.sum(dxn * xh, axis=0, keepdims=True)


def _xyc():
    return lax.axis_index("x"), lax.axis_index("y"), lax.axis_index("c")


def _allgather(xs, name):
    def body(x_ref, out_ref, send_sems, recv_sems, local_sem):
        x, y, c = _xyc()
        me, sibling = (x, y, c), (x, y, 1 - c)
        chips = [(1 - x, y), (x, 1 - y), (1 - x, 1 - y)]

        def blk(px, py, pc):
            return out_ref.at[4 * px + 2 * py + pc]

        def copy(k, block, to, src=None):
            return pltpu.make_async_remote_copy(
                src_ref=blk(*block) if src is None else src, dst_ref=blk(*block),
                send_sem=send_sems.at[k], recv_sem=recv_sems.at[k], device_id=to, device_id_type=MESH)

        mine = pltpu.make_async_copy(x_ref, blk(*me), local_sem)
        mine.start()
        first = [copy(0, me, sibling, src=x_ref)]
        first += [copy(1 + j, me, (*chip, c), src=x_ref) for j, chip in enumerate(chips)]
        for cp in first:
            cp.start()
        passed = [copy(4 + j, (*chip, c), sibling) for j, chip in enumerate(chips)]
        for j, chip in enumerate(chips):
            copy(1 + j, (*chip, c), me).wait_recv()
            passed[j].start()
        copy(0, sibling, me).wait_recv()
        for j, chip in enumerate(chips):
            copy(4 + j, (*chip, 1 - c), me).wait_recv()
        for cp in first + passed:
            cp.wait_send()
        mine.wait()

    return pl.pallas_call(
        body, name=name, out_shape=S((N_DEV,) + xs.shape, xs.dtype), in_specs=[ANY], out_specs=ANY,
        scratch_shapes=[pltpu.SemaphoreType.DMA((7,)), pltpu.SemaphoreType.DMA((7,)), pltpu.SemaphoreType.DMA],
    )(xs)


def _exchange(gs, name):
    n = len(gs)

    def body(*refs):
        g_refs, r_refs = refs[:n], refs[n:2 * n]
        send_sems, recv_sems, local_sems = refs[2 * n:]
        x, y, c = _xyc()
        me = 4 * x + 2 * y + c
        copies = []
        for t in range(n):
            own = pltpu.make_async_copy(g_refs[t].at[me], r_refs[t].at[me], local_sems.at[t])
            own.start()
            copies.append(own)
        remote = []
        for m in range(1, N_DEV):
            px, py, pc = x ^ (m >> 2), y ^ ((m >> 1) & 1), c ^ (m & 1)
            peer = 4 * px + 2 * py + pc
            for t in range(n):
                cp = pltpu.make_async_remote_copy(
                    src_ref=g_refs[t].at[peer], dst_ref=r_refs[t].at[me],
                    send_sem=send_sems.at[t, m - 1], recv_sem=recv_sems.at[t, m - 1],
                    device_id=(px, py, pc), device_id_type=MESH)
                cp.start()
                remote.append((cp, t, m, peer))
        for cp, t, m, peer in remote:
            pltpu.make_async_remote_copy(
                src_ref=g_refs[t].at[peer], dst_ref=r_refs[t].at[peer],
                send_sem=send_sems.at[t, m - 1], recv_sem=recv_sems.at[t, m - 1],
                device_id=(x, y, c), device_id_type=MESH).wait_recv()
        for cp, t, m, peer in remote:
            cp.wait_send()
        for own in copies:
            own.wait()

    outs = pl.pallas_call(
        body, name=name, out_shape=[S(g.shape, g.dtype) for g in gs], in_specs=[ANY] * n, out_specs=[ANY] * n,
        scratch_shapes=[pltpu.SemaphoreType.DMA((n, 7)), pltpu.SemaphoreType.DMA((n, 7)), pltpu.SemaphoreType.DMA((n,))],
    )(*gs)
    return list(outs)


def _ffn_fwd(h, gain, win_all, wout_all, f, name):
    t = h.shape[0]
    tm = _row_tile(t, 704)
    nt = t // tm

    def body(h_ref, g_ref, wg_ref, wu_ref, wo_ref, hn_ref, xn_ref, pg_ref, pu_ref, acc):
        c = pl.program_id(1)

        @pl.when(c == 0)
        def _():
            x = h_ref[...]
            r = lax.rsqrt(jnp.mean(x * x, axis=-1, keepdims=True) + EPS)
            xn_ref[...] = (x * r * g_ref[...]).astype(BF16)
            acc[...] = jnp.zeros_like(acc)

        xn = xn_ref[...]
        g = _dot(xn, wg_ref[...])
        u = _dot(xn, wu_ref[...])
        pg_ref[...] = g.astype(BF16)
        pu_ref[...] = u.astype(BF16)
        act = (g * _sigmoid(g) * u).astype(BF16)
        acc[...] += _dot(act, wo_ref[...].reshape(FF_SHARD, D))

        @pl.when(c == N_FF_CHUNK - 1)
        def _():
            hn_ref[...] = h_ref[...] + 0.5 * acc[...]

    return pl.pallas_call(
        body, name=name, grid=(nt, N_FF_CHUNK),
        in_specs=[
            pl.BlockSpec((tm, D), lambda i, c: (i, 0)),
            pl.BlockSpec((1, D), lambda i, c: (0, 0)),
            pl.BlockSpec((None, None, D, FF_SHARD), lambda i, c: (c, f, 0, 0)),
            pl.BlockSpec((None, None, D, FF_SHARD), lambda i, c: (c + N_FF_CHUNK, f, 0, 0)),
            pl.BlockSpec((2, None, FF_SHARD // 2, D), lambda i, c: (c, f, 0, 0)),
        ],
        out_specs=[
            pl.BlockSpec((tm, D), lambda i, c: (i, 0)),
            pl.BlockSpec((tm, D), lambda i, c: (i, 0)),
            pl.BlockSpec((None, tm, FF_SHARD), lambda i, c: (c, i, 0)),
            pl.BlockSpec((None, tm, FF_SHARD), lambda i, c: (c, i, 0)),
        ],
        out_shape=[S((t, D), F32), S((t, D), BF16), S((N_FF_CHUNK, t, FF_SHARD), BF16), S((N_FF_CHUNK, t, FF_SHARD), BF16)],
        scratch_shapes=[pltpu.VMEM((tm, D), F32)],
        compiler_params=_cp(dimension_semantics=("arbitrary", "arbitrary")),
    )(h, gain, win_all, win_all, wout_all)


def _ffn_bwd(dh, h, gain, pg, pu, win_all, wout_all, f, name):
    t = h.shape[0]
    tm = _row_tile(t, 704)
    nt = t // tm

    def body(dh_ref, h_ref, g_ref, pg_ref, pu_ref, wg_ref, wu_ref, wo_ref,
             dhi_ref, dob_ref, dpg_ref, dpu_ref, act_ref, dgain_ref, acc):
        i, c = pl.program_id(0), pl.program_id(1)

        @pl.when(c == 0)
        def _():
            dob_ref[...] = (0.5 * dh_ref[...]).astype(BF16)
            acc[...] = jnp.zeros_like(acc)

        @pl.when((i == 0) & (c == 0))
        def _():
            dgain_ref[...] = jnp.zeros_like(dgain_ref)

        dact = _dot_nt(dob_ref[...], wo_ref[...].reshape(FF_SHARD, D))
        g = pg_ref[...].astype(F32)
        u = pu_ref[...].astype(F32)
        s = _sigmoid(g)
        sl = g * s
        act_ref[...] = (sl * u).astype(BF16)
        dg = (dact * u * (s * (1.0 + g * (1.0 - s)))).astype(BF16)
        du = (dact * sl).astype(BF16)
        dpg_ref[...] = dg
        dpu_ref[...] = du
        acc[...] += _dot_nt(dg, wg_ref[...]) + _dot_nt(du, wu_ref[...])

        @pl.when(c == N_FF_CHUNK - 1)
        def _():
            dx, dgn = _rms_bwd(acc[...], h_ref[...], g_ref[...])
            dhi_ref[...] = dh_ref[...] + dx
            dgain_ref[0:1, :] += dgn

    blk = pl.BlockSpec((None, tm, FF_SHARD), lambda i, c: (c, i, 0))
    row = pl.BlockSpec((tm, D), lambda i, c: (i, 0))
    return pl.pallas_call(
        body, name=name, grid=(nt, N_FF_CHUNK),
        in_specs=[
            row, row, pl.BlockSpec((1, D), lambda i, c: (0, 0)), blk, blk,
            pl.BlockSpec((None, None, D, FF_SHARD), lambda i, c: (c, f, 0, 0)),
            pl.BlockSpec((None, None, D, FF_SHARD), lambda i, c: (c + N_FF_CHUNK, f, 0, 0)),
            pl.BlockSpec((2, None, FF_SHARD // 2, D), lambda i, c: (c, f, 0, 0)),
        ],
        out_specs=[row, row, blk, blk, blk, pl.BlockSpec((8, D), lambda i, c: (0, 0))],
        out_shape=[S((t, D), F32), S((t, D), BF16)] + [S((N_FF_CHUNK, t, FF_SHARD), BF16)] * 3 + [S((8, D), F32)],
        scratch_shapes=[pltpu.VMEM((tm, D), F32)],
        compiler_params=_cp(dimension_semantics=("arbitrary", "arbitrary")),
    )(dh, h, gain, pg, pu, win_all, win_all, wout_all)


def _ffn_dw_in(xn, dpg, dpu, name):
    t = xn.shape[0]
    tk = _row_tile(t, 1376)
    nk = t // tk

    def body(a_ref, bg_ref, bu_ref, o_ref, acc):
        c, k = pl.program_id(0), pl.program_id(1)

        @pl.when(k == 0)
        def _():
            acc[...] = jnp.zeros_like(acc)

        @pl.when(c < N_FF_CHUNK)
        def _():
            acc[...] += _dot_tn(a_ref[...], bg_ref[...])

        @pl.when(c >= N_FF_CHUNK)
        def _():
            acc[...] += _dot_tn(a_ref[...], bu_ref[...])

        @pl.when(k == nk - 1)
        def _():
            o_ref[...] = acc[...].astype(BF16)

    return pl.pallas_call(
        body, name=name, grid=(2 * N_FF_CHUNK, nk),
        in_specs=[
            pl.BlockSpec((tk, D), lambda c, k: (k, 0)),
            pl.BlockSpec((None, tk, FF_SHARD), lambda c, k: (jnp.minimum(c, N_FF_CHUNK - 1), k, 0)),
            pl.BlockSpec((None, tk, FF_SHARD), lambda c, k: (jnp.maximum(c - N_FF_CHUNK, 0), k, 0)),
        ],
        out_specs=pl.BlockSpec((None, D, FF_SHARD), lambda c, k: (c, 0, 0)),
        out_shape=S((2 * N_FF_CHUNK, D, FF_SHARD), BF16),
        scratch_shapes=[pltpu.VMEM((D, FF_SHARD), F32)],
        compiler_params=_cp(dimension_semantics=("arbitrary", "arbitrary")),
    )(xn, dpg, dpu)


def _mm_tn(a, b, tn, name):
    ca, t, m = a.shape
    cb, _, n = b.shape
    nc = max(ca, cb)
    tk = _row_tile(t, 1376)
    nk = t // tk
    nj = n // tn

    def body(a_ref, b_ref, o_ref, acc):
        k = pl.program_id(2)

        @pl.when(k == 0)
        def _():
            acc[...] = jnp.zeros_like(acc)

        acc[...] += _dot_tn(a_ref[...], b_ref[...])

        @pl.when(k == nk - 1)
        def _():
            o_ref[...] = acc[...].astype(BF16)

    return pl.pallas_call(
        body, name=name, grid=(nc, nj, nk),
        in_specs=[
            pl.BlockSpec((None, tk, m), (lambda c, j, k: (c, k, 0)) if ca > 1 else (lambda c, j, k: (0, k, 0))),
            pl.BlockSpec((None, tk, tn), (lambda c, j, k: (c, k, j)) if cb > 1 else (lambda c, j, k: (0, k, j))),
        ],
        out_specs=pl.BlockSpec((None, m, tn), lambda c, j, k: (c, 0, j)),
        out_shape=S((nc, m, n), BF16),
        scratch_shapes=[pltpu.VMEM((m, tn), F32)],
        compiler_params=_cp(dimension_semantics=("arbitrary", "arbitrary", "arbitrary")),
    )(a, b)


def _norm_mm(h, gain, w, tn, name):
    t = h.shape[0]
    n = w.shape[1]
    tm = _row_tile(t, 704)

    def body(h_ref, g_ref, w_ref, o_ref, xn_ref):
        @pl.when(pl.program_id(1) == 0)
        def _():
            x = h_ref[...]
            r = lax.rsqrt(jnp.mean(x * x, axis=-1, keepdims=True) + EPS)
            xn_ref[...] = (x * r * g_ref[...]).astype(BF16)

        o_ref[...] = _dot(xn_ref[...], w_ref[...]).astype(BF16)

    return pl.pallas_call(
        body, name=name, grid=(t // tm, n // tn),
        in_specs=[pl.BlockSpec((tm, D), lambda i, j: (i, 0)), pl.BlockSpec((1, D), lambda i, j: (0, 0)),
                  pl.BlockSpec((D, tn), lambda i, j: (0, j))],
        out_specs=[pl.BlockSpec((tm, tn), lambda i, j: (i, j)), pl.BlockSpec((tm, D), lambda i, j: (i, 0))],
        out_shape=[S((t, n), BF16), S((t, D), BF16)],
        compiler_params=_cp(dimension_semantics=("arbitrary", "arbitrary")),
    )(h, gain, w)


def _proj_bwd(dproj, w, dh, h, gain, tk, name):
    t, n = dproj.shape
    tm = _row_tile(t, 704)
    nk = n // tk

    def body(dp_ref, w_ref, dh_ref, h_ref, g_ref, dhi_ref, dgain_ref, acc):
        i, k = pl.program_id(0), pl.program_id(1)

        @pl.when(k == 0)
        def _():
            acc[...] = jnp.zeros_like(acc)

        @pl.when((i == 0) & (k == 0))
        def _():
            dgain_ref[...] = jnp.zeros_like(dgain_ref)

        acc[...] += _dot_nt(dp_ref[...], w_ref[...])

        @pl.when(k == nk - 1)
        def _():
            dx, dgn = _rms_bwd(acc[...], h_ref[...], g_ref[...])
            dhi_ref[...] = dh_ref[...] + dx
            dgain_ref[0:1, :] += dgn

    row = pl.BlockSpec((tm, D), lambda i, k: (i, 0))
    return pl.pallas_call(
        body, name=name, grid=(t // tm, nk),
        in_specs=[pl.BlockSpec((tm, tk), lambda i, k: (i, k)), pl.BlockSpec((D, tk), lambda i, k: (0, k)),
                  row, row, pl.BlockSpec((1, D), lambda i, k: (0, 0))],
        out_specs=[row, pl.BlockSpec((8, D), lambda i, k: (0, 0))],
        out_shape=[S((t, D), F32), S((8, D), F32)],
        scratch_shapes=[pltpu.VMEM((tm, D), F32)],
        compiler_params=_cp(dimension_semantics=("arbitrary", "arbitrary")),
    )(dproj, w, dh, h, gain)


def _post_fwd(o, proj, hgain, wout, h, nh, dv, name):
    t = h.shape[0]
    w = nh * dv
    tm = _row_tile(t, 704)

    def body(o_ref, g_ref, hg_ref, wo_ref, h_ref, hn_ref, og_ref):
        for hd in range(nh):
            sl = slice(hd * dv, (hd + 1) * dv)
            oh = o_ref[:, sl].astype(F32)
            r = lax.rsqrt(jnp.mean(oh * oh, axis=-1, keepdims=True) + EPS)
            gg = g_ref[:, sl].astype(F32)
            og_ref[:, sl] = (oh * r * hg_ref[:, sl] * (gg * _sigmoid(gg))).astype(BF16)
        hn_ref[...] = h_ref[...] + _dot(og_ref[...], wo_ref[...])

    return pl.pallas_call(
        body, name=name, grid=(t // tm,),
        in_specs=[pl.BlockSpec((tm, w), lambda i: (i, 0)), pl.BlockSpec((tm, w), lambda i: (i, 2)),
                  pl.BlockSpec((1, w), lambda i: (0, 0)), pl.BlockSpec((w, D), lambda i: (0, 0)),
                  pl.BlockSpec((tm, D), lambda i: (i, 0))],
        out_specs=[pl.BlockSpec((tm, D), lambda i: (i, 0)), pl.BlockSpec((tm, w), lambda i: (i, 0))],
        out_shape=[S((t, D), F32), S((t, w), BF16)],
        compiler_params=_cp(dimension_semantics=("arbitrary",)),
    )(o, proj, hgain, wout, h)


def _post_bwd(dh, o, proj, hgain, wout, nh, dv, nproj, name):
    t = dh.shape[0]
    w = nh * dv
    tm = _row_tile(t, 704)

    def body(dh_ref, o_ref, g_ref, hg_ref, wo_ref, do_ref, dg_ref, dhb_ref, dhg_ref):
        @pl.when(pl.program_id(0) == 0)
        def _():
            dhg_ref[...] = jnp.zeros_like(dhg_ref)

        dmix = dh_ref[...].astype(BF16)
        dhb_ref[...] = dmix
        dog = _dot_nt(dmix, wo_ref[...])
        for hd in range(nh):
            sl = slice(hd * dv, (hd + 1) * dv)
            oh = o_ref[:, sl].astype(F32)
            r = lax.rsqrt(jnp.mean(oh * oh, axis=-1, keepdims=True) + EPS)
            xh = oh * r
            gain = hg_ref[:, sl]
            gg = g_ref[:, sl].astype(F32)
            s = _sigmoid(gg)
            dogh = dog[:, sl]
            don = dogh * (gg * s)
            dg_ref[:, sl] = (dogh * (xh * gain) * (s * (1.0 + gg * (1.0 - s)))).astype(BF16)
            dxh = don * gain
            do_ref[:, sl] = (r * (dxh - xh * jnp.mean(dxh * xh, axis=-1, keepdims=True))).astype(BF16)
            dhg_ref[0:1, sl] += jnp.sum(don * xh, axis=0, keepdims=True)

    return pl.pallas_call(
        body, name=name, grid=(t // tm,),
        in_specs=[pl.BlockSpec((tm, D), lambda i: (i, 0)), pl.BlockSpec((tm, w), lambda i: (i, 0)),
                  pl.BlockSpec((tm, w), lambda i: (i, 2)), pl.BlockSpec((1, w), lambda i: (0, 0)),
                  pl.BlockSpec((w, D), lambda i: (0, 0))],
        out_specs=[pl.BlockSpec((tm, w), lambda i: (i, 0)), pl.BlockSpec((tm, w), lambda i: (i, 2)),
                   pl.BlockSpec((tm, D), lambda i: (i, 0)), pl.BlockSpec((8, w), lambda i: (0, 0))],
        out_shape=[S((t, w), BF16), S((t, nproj), BF16), S((t, D), BF16), S((8, w), F32)],
        compiler_params=_cp(dimension_semantics=("arbitrary",)),
    )(dh, o, proj, hgain, wout)


def _ret_consts():
    lg = np.log1p(-np.exp2(-5.0 - np.arange(RET_H, dtype=np.float32))).astype(np.float32)
    return jnp.asarray(np.broadcast_to(lg[:, None, None], (RET_H, 1, 128)).copy())


def _rope_tables(t):
    half = RET_DK // 2
    inv = 1.0 / (ROPE_BASE ** jnp.linspace(0.0, 1.0, half, dtype=F32))
    pos = jnp.maximum(jnp.arange(t) - PAD, 0).astype(F32)
    ang = pos[:, None] * inv[None, :]
    return jnp.cos(ang), jnp.sin(ang)


def _ret_chunk(blk_ref, cos_ref, sin_ref, lg):
    c = RET_C
    half = RET_DK // 2
    cs, sn = cos_ref[...], sin_ref[...]
    q1, q2 = blk_ref[:, 0:half].astype(F32), blk_ref[:, half:RET_DK].astype(F32)
    k1, k2 = blk_ref[:, RET_DK:RET_DK + half].astype(F32), blk_ref[:, RET_DK + half:2 * RET_DK].astype(F32)
    qr = jnp.concatenate([q1 * cs - q2 * sn, q1 * sn + q2 * cs], axis=1)
    kr = jnp.concatenate([k1 * cs - k2 * sn, k1 * sn + k2 * cs], axis=1) * (RET_DK ** -0.5)
    v = blk_ref[:, 2 * RET_DK:RET_HW]
    ii = lax.broadcasted_iota(jnp.int32, (c, 1), 0).astype(F32)
    jj = lax.broadcasted_iota(jnp.int32, (1, c), 1).astype(F32)
    rel = ii - jj
    dmat = jnp.where(rel >= 0, jnp.exp(lg * jnp.maximum(rel, 0.0)), 0.0)
    dq = jnp.exp(lg * (ii + 1.0))
    dk = jnp.exp(lg * (c - 1.0 - ii))
    dchunk = jnp.exp(lg * float(c))
    return qr, kr, v, dmat, dq, dk, dchunk


def _ret_scan_fwd(proj, cos, sin, lgam, name):
    t = proj.shape[0]
    c = RET_C
    nc = t // c

    def body(blk_ref, cos_ref, sin_ref, lg_ref, o_ref, st_ref, state):
        @pl.when(pl.program_id(1) == 0)
        def _():
            state[...] = jnp.zeros_like(state)

        lg = lg_ref[:, 0:1]
        qr, kr, v, dmat, dq, dk, dchunk = _ret_chunk(blk_ref, cos_ref, sin_ref, lg)
        sp = state[...]
        st_ref[...] = sp.astype(BF16)
        scores = _dot_nt(qr.astype(BF16), kr.astype(BF16)) * dmat
        o = _dot(scores.astype(BF16), v) + _dot((qr * dq).astype(BF16), sp.astype(BF16))
        o_ref[...] = o.astype(BF16)
        state[...] = sp * dchunk + _dot_tn((kr * dk).astype(BF16), v)

    return pl.pallas_call(
        body, name=name, grid=(RET_H, nc),
        in_specs=[pl.BlockSpec((c, RET_HW), lambda h, n: (n, h)), pl.BlockSpec((c, 128), lambda h, n: (n, 0)),
                  pl.BlockSpec((c, 128), lambda h, n: (n, 0)), pl.BlockSpec((None, 1, 128), lambda h, n: (h, 0, 0))],
        out_specs=[pl.BlockSpec((c, RET_DV), lambda h, n: (n, h)),
                   pl.BlockSpec((None, None, RET_DK, RET_DV), lambda h, n: (h, n, 0, 0))],
        out_shape=[S((t, RET_H * RET_DV), BF16), S((RET_H, nc, RET_DK, RET_DV), BF16)],
        scratch_shapes=[pltpu.VMEM((RET_DK, RET_DV), F32)],
        compiler_params=_cp(dimension_semantics=("arbitrary", "arbitrary")),
    )(proj, cos, sin, lgam)


def _ret_scan_bwd(proj, cos, sin, lgam, do, states, dproj, name):
    t = proj.shape[0]
    c = RET_C
    nc = t // c
    half = RET_DK // 2

    def body(blk_ref, cos_ref, sin_ref, lg_ref, do_ref, st_ref, dp_in, dp_ref, dstate):
        n = nc - 1 - pl.program_id(1)

        @pl.when(pl.program_id(1) == 0)
        def _():
            dstate[...] = jnp.zeros_like(dstate)

        lg = lg_ref[:, 0:1]
        qr, kr, v, dmat, dq, dk, dchunk = _ret_chunk(blk_ref, cos_ref, sin_ref, lg)
        qb, kb = qr.astype(BF16), kr.astype(BF16)
        dob = do_ref[...]
        sp = st_ref[...]
        ds = dstate[...]
        dsb = ds.astype(BF16)
        p = (_dot_nt(qb, kb) * dmat).astype(BF16)
        dvv = _dot_tn(p, dob) + _dot((kr * dk).astype(BF16), dsb)
        dp = (_dot_nt(dob, v) * dmat).astype(BF16)
        dqr = _dot(dp, kb) + _dot_nt(dob, sp) * dq
        dkr = (_dot_tn(dp, qb) + _dot_nt(v, dsb) * dk) * (RET_DK ** -0.5)
        dstate[...] = ds * dchunk + _dot_tn((qr * dq).astype(BF16), dob)
        cs, sn = cos_ref[...], sin_ref[...]
        rows = n * c + lax.broadcasted_iota(jnp.int32, (c, 1), 0)
        keep = rows >= PAD

        def unrot(d):
            d1, d2 = d[:, :half], d[:, half:]
            return jnp.concatenate([d1 * cs + d2 * sn, d2 * cs - d1 * sn], axis=1)

        out = jnp.concatenate([unrot(dqr), unrot(dkr), dvv], axis=1)
        dp_ref[...] = jnp.where(keep, out, 0.0).astype(BF16)

    nproj = dproj.shape[1]
    return pl.pallas_call(
        body, name=name, grid=(RET_H, nc),
        in_specs=[pl.BlockSpec((c, RET_HW), lambda h, n: (nc - 1 - n, h)), pl.BlockSpec((c, 128), lambda h, n: (nc - 1 - n, 0)),
                  pl.BlockSpec((c, 128), lambda h, n: (nc - 1 - n, 0)), pl.BlockSpec((None, 1, 128), lambda h, n: (h, 0, 0)),
                  pl.BlockSpec((c, RET_DV), lambda h, n: (nc - 1 - n, h)),
                  pl.BlockSpec((None, None, RET_DK, RET_DV), lambda h, n: (h, nc - 1 - n, 0, 0)), ANY],
        out_specs=pl.BlockSpec((c, RET_HW), lambda h, n: (nc - 1 - n, h)),
        out_shape=S((t, nproj), BF16),
        input_output_aliases={6: 0},
        scratch_shapes=[pltpu.VMEM((RET_DK, RET_DV), F32)],
        compiler_params=_cp(dimension_semantics=("arbitrary", "arbitrary")),
    )(proj, cos, sin, lgam, do, states, dproj)


def _split3(x):
    hi = x.astype(BF16)
    r1 = x - hi.astype(F32)
    mid = r1.astype(BF16)
    lo = (r1 - mid.astype(F32)).astype(BF16)
    return hi, mid, lo


def _gla_chunk(blk_ref, z_ref, wg_ref, bg_ref, n):
    c = CHUNK
    q = blk_ref[:, 0:GLA_DK].astype(F32) * (GLA_DK ** -0.5)
    k = blk_ref[:, GLA_DK:2 * GLA_DK].astype(F32)
    v = blk_ref[:, 2 * GLA_DK:GLA_HW]
    u = _dot(z_ref[...], wg_ref[...]) + bg_ref[...]
    la = (jnp.minimum(u, 0.0) - jnp.log(1.0 + jnp.exp(-jnp.abs(u)))) * (1.0 / GLA_TAU)
    rows = n * c + lax.broadcasted_iota(jnp.int32, (c, 1), 0)
    keep = rows >= PAD
    la = jnp.where(keep, la, 0.0)
    ii = lax.broadcasted_iota(jnp.int32, (c, c), 0)
    jj = lax.broadcasted_iota(jnp.int32, (c, c), 1)
    tril = (ii >= jj).astype(BF16)
    hi, mid, lo = _split3(la)
    b = _dot(tril, hi) + _dot(tril, mid) + _dot(tril, lo)
    return q, k, v, u, b, keep


def _gla_intra(q, k, b, a_ref):
    c = CHUNK
    col = lax.broadcasted_iota(jnp.int32, (1, c), 1)
    rowi = lax.broadcasted_iota(jnp.int32, (SUB, 1), 0)
    for blk in range(c // SUB):
        r = slice(SUB * blk, SUB * (blk + 1))
        b_i = b[r]
        q_i, k_i = q[r], k[r]
        if blk > 0:
            bprev = b[SUB * blk - 1:SUB * blk]
            qe = q_i * jnp.exp(b_i - bprev)
            kt = k * jnp.exp(jnp.minimum(bprev - b, 0.0))
            arow = jnp.where(col < SUB * blk, _dot_nt(qe.astype(BF16), kt.astype(BF16)), 0.0)
        else:
            arow = jnp.zeros((SUB, c), F32)
        for j in range(SUB):
            e = jnp.where(rowi >= j, jnp.exp(jnp.minimum(b_i - b_i[j:j + 1], 0.0)), 0.0)
            a = jnp.sum(q_i * k_i[j:j + 1] * e, axis=1, keepdims=True)
            arow = jnp.where(col == SUB * blk + j, a, arow)
        a_ref[r, :] = arow


def _gla_scan_fwd(proj, wgp, bg, name):
    t = proj.shape[0]
    c = CHUNK
    nc = t // c

    def body(blk_ref, z_ref, wg_ref, bg_ref, o_ref, st_ref, state, a_ref):
        n = pl.program_id(1)

        @pl.when(n == 0)
        def _():
            state[...] = jnp.zeros_like(state)

        q, k, v, u, b, keep = _gla_chunk(blk_ref, z_ref, wg_ref, bg_ref, n)
        _gla_intra(q, k, b, a_ref)
        sp = state[...]
        st_ref[...] = sp.astype(BF16)
        o = _dot(a_ref[...].astype(BF16), v) + _dot_nt((q * jnp.exp(b)).astype(BF16), sp.astype(BF16))
        o_ref[...] = o.astype(BF16)
        bc = b[c - 1:c]
        state[...] = sp * jnp.exp(bc) + _dot_tn(v, (k * jnp.exp(bc - b)).astype(BF16))

    return pl.pallas_call(
        body, name=name, grid=(GLA_H, nc),
        in_specs=[pl.BlockSpec((c, GLA_HW), lambda h, n: (n, h)), pl.BlockSpec((c, 128), lambda h, n: (n, GLA_ZBLK)),
                  pl.BlockSpec((128, GLA_DK), lambda h, n: (0, h)), pl.BlockSpec((1, GLA_DK), lambda h, n: (0, h))],
        out_specs=[pl.BlockSpec((c, GLA_DV), lambda h, n: (n, h)),
                   pl.BlockSpec((None, None, GLA_DV, GLA_DK), lambda h, n: (h, n, 0, 0))],
        out_shape=[S((t, GLA_H * GLA_DV), BF16), S((GLA_H, nc, GLA_DV, GLA_DK), BF16)],
        scratch_shapes=[pltpu.VMEM((GLA_DV, GLA_DK), F32), pltpu.VMEM((c, c), F32)],
        compiler_params=_cp(dimension_semantics=("arbitrary", "arbitrary")),
    )(proj, proj, wgp, bg)


def _gla_scan_bwd(proj, wgp, bg, do, states, dproj, name):
    t = proj.shape[0]
    c = CHUNK
    nc = t // c

    def body(blk_ref, z_ref, wg_ref, bg_ref, do_ref, st_ref, dp_in, dp_ref, du_ref, dstate, a_ref, dq_ref, dkd_ref):
        n = nc - 1 - pl.program_id(1)

        @pl.when(pl.program_id(1) == 0)
        def _():
            dstate[...] = jnp.zeros_like(dstate)

        q, k, v, u, b, keep = _gla_chunk(blk_ref, z_ref, wg_ref, bg_ref, n)
        _gla_intra(q, k, b, a_ref)
        ab = a_ref[...].astype(BF16)
        dob = do_ref[...]
        sp = st_ref[...]
        ds = dstate[...]
        dsb = ds.astype(BF16)
        bc = b[c - 1:c]
        eb = jnp.exp(b)
        ebc = jnp.exp(bc - b)
        ec = jnp.exp(bc)
        qb = (q * eb).astype(BF16)
        kb = (k * ebc).astype(BF16)
        dvv = _dot_tn(ab, dob) + _dot_nt(kb, dsb)
        ii = lax.broadcasted_iota(jnp.int32, (c, c), 0)
        jj = lax.broadcasted_iota(jnp.int32, (c, c), 1)
        da = jnp.where(ii >= jj, _dot_nt(dob, v), 0.0)
        dq_inter = _dot(dob, sp) * eb
        dk_state = _dot(v, dsb) * ebc
        dstate[...] = ds * ec + _dot_tn(dob, qb)

        col = lax.broadcasted_iota(jnp.int32, (1, c), 1)
        rowi = lax.broadcasted_iota(jnp.int32, (SUB, 1), 0)
        dk = jnp.zeros((c, GLA_DK), F32)
        for blk in range(c // SUB):
            r = slice(SUB * blk, SUB * (blk + 1))
            b_i = b[r]
            q_i, k_i = q[r], k[r]
            darow = da[r]
            if blk > 0:
                bprev = b[SUB * blk - 1:SUB * blk]
                e_i = jnp.exp(b_i - bprev)
                ek = jnp.exp(jnp.minimum(bprev - b, 0.0))
                daoff = jnp.where(col < SUB * blk, darow, 0.0).astype(BF16)
                dq_i = _dot(daoff, (k * ek).astype(BF16)) * e_i
                dk = dk + _dot_tn(daoff, (q_i * e_i).astype(BF16)) * ek
            else:
                dq_i = jnp.zeros((SUB, GLA_DK), F32)
            dkd = jnp.zeros((SUB, GLA_DK), F32)
            for j in range(SUB):
                e = jnp.where(rowi >= j, jnp.exp(jnp.minimum(b_i - b_i[j:j + 1], 0.0)), 0.0)
                dacol = jnp.sum(jnp.where(col == SUB * blk + j, darow, 0.0), axis=1, keepdims=True)
                tt = dacol * e
                dq_i = dq_i + tt * k_i[j:j + 1]
                dkd = jnp.where(rowi == j, jnp.sum(tt * q_i, axis=0, keepdims=True), dkd)
            dq_ref[r, :] = dq_i
            dkd_ref[r, :] = dkd
        dq = dq_ref[...] + dq_inter
        dk = dk + dkd_ref[...] + dk_state
        extra = jnp.sum(k * dk_state, axis=0, keepdims=True) + ec * jnp.sum(sp.astype(F32) * ds, axis=0, keepdims=True)
        rowc = lax.broadcasted_iota(jnp.int32, (c, 1), 0)
        db = q * dq - k * dk + jnp.where(rowc == c - 1, extra, 0.0)
        triu = (ii <= jj).astype(BF16)
        hi, mid, lo = _split3(db)
        dla = _dot(triu, hi) + _dot(triu, mid) + _dot(triu, lo)
        du = jnp.where(keep, dla * (1.0 / GLA_TAU) / (1.0 + jnp.exp(u)), 0.0)
        du_ref[...] = du.astype(BF16)
        out = jnp.concatenate([dq * (GLA_DK ** -0.5), dk, dvv], axis=1)
        dp_ref[...] = jnp.where(keep, out, 0.0).astype(BF16)

    nproj = dproj.shape[1]
    return pl.pallas_call(
        body, name=name, grid=(GLA_H, nc),
        in_specs=[pl.BlockSpec((c, GLA_HW), lambda h, n: (nc - 1 - n, h)), pl.BlockSpec((c, 128), lambda h, n: (nc - 1 - n, GLA_ZBLK)),
                  pl.BlockSpec((128, GLA_DK), lambda h, n: (0, h)), pl.BlockSpec((1, GLA_DK), lambda h, n: (0, h)),
                  pl.BlockSpec((c, GLA_DV), lambda h, n: (nc - 1 - n, h)),
                  pl.BlockSpec((None, None, GLA_DV, GLA_DK), lambda h, n: (h, nc - 1 - n, 0, 0)), ANY],
        out_specs=[pl.BlockSpec((c, GLA_HW), lambda h, n: (nc - 1 - n, h)), pl.BlockSpec((c, GLA_DK), lambda h, n: (nc - 1 - n, h))],
        out_shape=[S((t, nproj), BF16), S((t, GLA_H * GLA_DK), BF16)],
        input_output_aliases={6: 0},
        scratch_shapes=[pltpu.VMEM((GLA_DV, GLA_DK), F32), pltpu.VMEM((c, c), F32),
                        pltpu.VMEM((c, GLA_DK), F32), pltpu.VMEM((c, GLA_DK), F32)],
        compiler_params=_cp(dimension_semantics=("arbitrary", "arbitrary")),
    )(proj, proj, wgp, bg, do, states, dproj)


def _gla_gate_bwd(du, proj, wgp, dproj, name):
    t = du.shape[0]
    tm = _row_tile(t, 704)
    w = GLA_H * GLA_DK

    def body(du_ref, z_ref, wg_ref, dp_in, dp_ref, dwg_ref, dbg_ref):
        @pl.when(pl.program_id(0) == 0)
        def _():
            dwg_ref[...] = jnp.zeros_like(dwg_ref)
            dbg_ref[...] = jnp.zeros_like(dbg_ref)

        d = du_ref[...]
        dp_ref[...] = _dot_nt(d, wg_ref[...]).astype(BF16)
        dwg_ref[...] += _dot_tn(z_ref[...], d)
        dbg_ref[0:1, :] += jnp.sum(d.astype(F32), axis=0, keepdims=True)

    return pl.pallas_call(
        body, name=name, grid=(t // tm,),
        in_specs=[pl.BlockSpec((tm, w), lambda i: (i, 0)), pl.BlockSpec((tm, 128), lambda i: (i, GLA_ZBLK)),
                  pl.BlockSpec((128, w), lambda i: (0, 0)), ANY],
        out_specs=[pl.BlockSpec((tm, 128), lambda i: (i, GLA_ZBLK)), pl.BlockSpec((128, w), lambda i: (0, 0)),
                   pl.BlockSpec((8, w), lambda i: (0, 0))],
        out_shape=[S(dproj.shape, BF16), S((128, w), F32), S((8, w), F32)],
        input_output_aliases={3: 0},
        compiler_params=_cp(dimension_semantics=("arbitrary",)),
    )(du, proj, wgp, dproj)


def _final_loss(hx, gain, target, name):
    t = hx.shape[0]
    tm = _row_tile(t, 512)

    def body(h_ref, g_ref, t_ref, dh_ref, dgain_ref, loss_ref):
        @pl.when(pl.program_id(0) == 0)
        def _():
            dgain_ref[...] = jnp.zeros_like(dgain_ref)
            loss_ref[...] = jnp.zeros_like(loss_ref)

        x = h_ref[...]
        gain = g_ref[...]
        r = lax.rsqrt(jnp.mean(x * x, axis=-1, keepdims=True) + EPS)
        xh = x * r
        e = xh * gain - t_ref[...]
        loss_ref[...] += 0.5 * jnp.sum(jnp.mean(e * e, axis=-1, keepdims=True), axis=0, keepdims=True)
        dy = e * (1.0 / D)
        dgain_ref[0:1, :] += jnp.sum(dy * xh, axis=0, keepdims=True)
        dxh = dy * gain
        dh_ref[...] = r * (dxh - xh * jnp.mean(dxh * xh, axis=-1, keepdims=True))

    row = pl.BlockSpec((tm, D), lambda i: (i, 0))
    return pl.pallas_call(
        body, name=name, grid=(t // tm,),
        in_specs=[row, pl.BlockSpec((1, D), lambda i: (0, 0)), row],
        out_specs=[row, pl.BlockSpec((8, D), lambda i: (0, 0)), pl.BlockSpec((8, 128), lambda i: (0, 0))],
        out_shape=[S((t, D), F32), S((8, D), F32), S((8, 128), F32)],
        compiler_params=_cp(dimension_semantics=("arbitrary",)),
    )(hx, gain, target)


def _adam_math(w, g, m, v):
    m2 = ADAM_B1 * m + (1.0 - ADAM_B1) * g
    v2 = ADAM_B2 * v + (1.0 - ADAM_B2) * (g * g)
    m_hat = m2 / (1.0 - ADAM_B1 ** ADAM_STEP)
    v_hat = v2 / (1.0 - ADAM_B2 ** ADAM_STEP)
    delta = -ADAM_LR * (m_hat / (jnp.sqrt(v_hat) + ADAM_EPS) + ADAM_WD * w)
    return delta, m2, v2


def _adamw_reduce(recv, w, m, v, name):
    r, wd = w.shape
    tr = _row_tile(r, 256) if r % 16 == 0 else r

    def body(rv_ref, w_ref, m_ref, v_ref, g_ref, d_ref, m2_ref, v2_ref):
        g = rv_ref[0].astype(F32)
        for s in range(1, N_DEV):
            g = g + rv_ref[s].astype(F32)
        g_ref[...] = g
        d_ref[...], m2_ref[...], v2_ref[...] = _adam_math(w_ref[...], g, m_ref[...], v_ref[...])

    row = pl.BlockSpec((tr, wd), lambda i: (i, 0))
    return pl.pallas_call(
        body, name=name, grid=(r // tr,),
        in_specs=[pl.BlockSpec((N_DEV, tr, wd), lambda i: (0, i, 0)), row, row, row],
        out_specs=[row] * 4, out_shape=[S((r, wd), F32)] * 4,
        compiler_params=_cp(dimension_semantics=("arbitrary",)),
    )(recv, w, m, v)


def _small_reduce(parts, name):
    _, r, wd = parts.shape

    def body(p_ref, o_ref):
        g = p_ref[0]
        for s in range(1, N_DEV):
            g = g + p_ref[s]
        o_ref[...] = g

    return pl.pallas_call(body, name=name, out_shape=S((r, wd), F32), compiler_params=_cp())(parts)


def _adamw_small(w, g, m, v, name):
    def body(w_ref, g_ref, m_ref, v_ref, d_ref, m2_ref, v2_ref):
        d_ref[...], m2_ref[...], v2_ref[...] = _adam_math(w_ref[...], g_ref[...], m_ref[...], v_ref[...])

    return pl.pallas_call(body, name=name, out_shape=[S(w.shape, F32)] * 3, compiler_params=_cp())(w, g, m, v)


def _to_head_major(w, nh, dk, dv):
    kk = w.shape[0]
    q = w[:, :nh * dk].reshape(kk, nh, dk)
    k = w[:, nh * dk:2 * nh * dk].reshape(kk, nh, dk)
    v = w[:, 2 * nh * dk:2 * nh * dk + nh * dv].reshape(kk, nh, dv)
    heads = jnp.concatenate([q, k, v], axis=-1).reshape(kk, nh * (2 * dk + dv))
    return jnp.concatenate([heads, w[:, 2 * nh * dk + nh * dv:]], axis=1)


def _from_head_major(p, nh, dk, dv):
    kk = p.shape[0]
    hw = 2 * dk + dv
    heads = p[:, :nh * hw].reshape(kk, nh, hw)
    q = heads[:, :, :dk].reshape(kk, nh * dk)
    k = heads[:, :, dk:2 * dk].reshape(kk, nh * dk)
    v = heads[:, :, 2 * dk:].reshape(kk, nh * dv)
    return jnp.concatenate([q, k, v, p[:, nh * hw:]], axis=1)


def _unshard_cols(g):
    return jnp.transpose(g, (1, 0, 2)).reshape(g.shape[1], N_DEV * g.shape[2])


def _shard_cols(w):
    k, n8 = w.shape
    return jnp.transpose(w.reshape(k, N_DEV, n8 // N_DEV), (1, 0, 2))


def _my_cols(full, width):
    me = 4 * lax.axis_index("x") + 2 * lax.axis_index("y") + lax.axis_index("c")
    return lax.dynamic_slice_in_dim(full, me * width, width, axis=1)


def kernel(x, meta_tokens, norm_ffn1, ffn1_w_in, ffn1_w_out, norm_mix, norm_ffn2, ffn2_w_in, ffn2_w_out, ret_w_in, ret_head_norm, ret_w_out, gla_w_in, gla_w_gate, gla_b_gate, gla_head_norm, gla_w_out, final_norm, loss_target, m_meta_tokens, m_norm_ffn1, m_ffn1_w_in, m_ffn1_w_out, m_norm_mix, m_norm_ffn2, m_ffn2_w_in, m_ffn2_w_out, m_ret_w_in, m_ret_head_norm, m_ret_w_out, m_gla_w_in, m_gla_w_gate, m_gla_b_gate, m_gla_head_norm, m_gla_w_out, m_final_norm, v_meta_tokens, v_norm_ffn1, v_ffn1_w_in, v_ffn1_w_out, v_norm_mix, v_norm_ffn2, v_ffn2_w_in, v_ffn2_w_out, v_ret_w_in, v_ret_head_norm, v_ret_w_out, v_gla_w_in, v_gla_w_gate, v_gla_b_gate, v_gla_head_norm, v_gla_w_out, v_final_norm):
    seq = x.shape[1]
    t = seq + CHUNK
    xs = x[0]
    target = loss_target[0]

    win_all = _allgather(jnp.concatenate([ffn1_w_in, ffn2_w_in], axis=0).astype(BF16), "ag_ffn_w_in")
    wout_all = _allgather(jnp.concatenate([ffn1_w_out, ffn2_w_out], axis=0).astype(BF16), "ag_ffn_w_out")
    ret_win = _to_head_major(_unshard_cols(_allgather(ret_w_in[0].astype(BF16), "ag_ret_w_in")), RET_H, RET_DK, RET_DV)
    ret_wout = _allgather(ret_w_out[0].astype(BF16), "ag_ret_w_out").reshape(RET_H * RET_DV, D)
    gla_win = _to_head_major(_unshard_cols(_allgather(gla_w_in[0].astype(BF16), "ag_gla_w_in")), GLA_H, GLA_DK, GLA_DV)
    gla_win = jnp.pad(gla_win, ((0, 0), (0, GLA_N - gla_win.shape[1])))
    gla_wout = _allgather(gla_w_out[0].astype(BF16), "ag_gla_w_out").reshape(GLA_H * GLA_DV, D)
    small = jnp.concatenate([meta_tokens.reshape(-1), ret_head_norm.reshape(-1), gla_w_gate.reshape(-1),
                             gla_b_gate.reshape(-1), gla_head_norm.reshape(-1)])
    n_small = small.shape[0]
    small = jnp.pad(small, (0, 32 * 128 - n_small)).reshape(32, 128)
    sg = _allgather(small, "ag_small").reshape(N_DEV, 32 * 128)

    def small_cols(off, rows, width):
        return jnp.transpose(sg[:, off:off + rows * width].reshape(N_DEV, rows, width), (1, 0, 2)).reshape(rows, N_DEV * width)

    off = 0
    meta_full = small_cols(off, N_META, D // N_DEV); off += N_META * (D // N_DEV)
    ret_hn = small_cols(off, RET_H, RET_DV // N_DEV).reshape(1, RET_H * RET_DV); off += RET_H * RET_DV // N_DEV
    wgate = small_cols(off, GLA_RANK, GLA_H * GLA_DK // N_DEV); off += GLA_RANK * GLA_H * GLA_DK // N_DEV
    bgate = small_cols(off, 1, GLA_H * GLA_DK // N_DEV); off += GLA_H * GLA_DK // N_DEV
    gla_hn = small_cols(off, GLA_H, GLA_DV // N_DEV).reshape(1, GLA_H * GLA_DV)
    wgp = jnp.pad(wgate, ((0, 128 - GLA_RANK), (0, 0))).astype(BF16)

    cos, sin = _rope_tables(t)
    lgam = _ret_consts()

    h0 = jnp.concatenate([jnp.zeros((PAD, D), F32), meta_full, xs], axis=0)
    g1 = [norm_ffn1[i:i + 1] for i in range(2)]
    gm = [norm_mix[i:i + 1] for i in range(2)]
    g2 = [norm_ffn2[i:i + 1] for i in range(2)]

    h1, xn_a0, pg_a0, pu_a0 = _ffn_fwd(h0, g1[0], win_all, wout_all, 0, "ffn1_l0_fwd")
    rproj, rhn = _norm_mm(h1, gm[0], ret_win, 1536, "ret_proj_fwd")
    ro, rstates = _ret_scan_fwd(rproj, cos, sin, lgam, "ret_scan_fwd")
    h2, rog = _post_fwd(ro, rproj, ret_hn, ret_wout, h1, RET_H, RET_DV, "ret_post_fwd")
    h3, xn_b0, pg_b0, pu_b0 = _ffn_fwd(h2, g2[0], win_all, wout_all, 2, "ffn2_l0_fwd")
    h4, xn_a1, pg_a1, pu_a1 = _ffn_fwd(h3, g1[1], win_all, wout_all, 1, "ffn1_l1_fwd")
    gproj, ghn = _norm_mm(h4, gm[1], gla_win, 640, "gla_proj_fwd")
    go, gstates = _gla_scan_fwd(gproj, wgp, bgate, "gla_scan_fwd")
    h5, gog = _post_fwd(go, gproj, gla_hn, gla_wout, h4, GLA_H, GLA_DV, "gla_post_fwd")
    h6, xn_b1, pg_b1, pu_b1 = _ffn_fwd(h5, g2[1], win_all, wout_all, 3, "ffn2_l1_fwd")

    dhx, dfinal, loss_blk = _final_loss(h6[CHUNK:], final_norm.reshape(1, D), target, "final_loss")
    loss = lax.psum(loss_blk[0, 0], ("x", "y", "c"))
    dh = jnp.concatenate([jnp.zeros((CHUNK, D), F32), dhx], axis=0)

    def ffn_back(dh, h_in, xn, gain, pg, pu, f, tag):
        dh_in, dob, dpg, dpu, act, dgain = _ffn_bwd(dh, h_in, gain, pg, pu, win_all, wout_all, f, tag + "_bwd")
        dwin = _ffn_dw_in(xn, dpg, dpu, tag + "_dw_in")
        dwout = _mm_tn(act, dob[None], D, tag + "_dw_out").reshape(N_DEV, FF_SHARD // 2, D)
        return dh_in, dwin, dwout, dgain[0]

    dh, dwin_b1, dwout_b1, dg2_1 = ffn_back(dh, h5, xn_b1, g2[1], pg_b1, pu_b1, 3, "ffn2_l1")

    gdo, gdproj, gdhb, dghn = _post_bwd(dh, go, gproj, gla_hn, gla_wout, GLA_H, GLA_DV, GLA_N, "gla_post_bwd")
    d_gla_wout = _mm_tn(gog[None], gdhb[None], D, "gla_dw_out").reshape(N_DEV, GLA_H * GLA_DV // N_DEV, D)
    gdproj, gdu = _gla_scan_bwd(gproj, wgp, bgate, gdo, gstates, gdproj, "gla_scan_bwd")
    gdproj, dwg, dbg = _gla_gate_bwd(gdu, gproj, wgp, gdproj, "gla_gate_bwd")
    d_gla_win = _mm_tn(ghn[None], gdproj[None], 640, "gla_dw_in")[0]
    dh, dgm_1 = _proj_bwd(gdproj, gla_win, dh, h4, gm[1], 640, "gla_proj_bwd")

    dh, dwin_a1, dwout_a1, dg1_1 = ffn_back(dh, h3, xn_a1, g1[1], pg_a1, pu_a1, 1, "ffn1_l1")
    dh, dwin_b0, dwout_b0, dg2_0 = ffn_back(dh, h2, xn_b0, g2[0], pg_b0, pu_b0, 2, "ffn2_l0")

    rdo, rdproj, rdhb, drhn = _post_bwd(dh, ro, rproj, ret_hn, ret_wout, RET_H, RET_DV, 6 * D, "ret_post_bwd")
    d_ret_wout = _mm_tn(rog[None], rdhb[None], D, "ret_dw_out").reshape(N_DEV, RET_H * RET_DV // N_DEV, D)
    rdproj = _ret_scan_bwd(rproj, cos, sin, lgam, rdo, rstates, rdproj, "ret_scan_bwd")
    d_ret_win = _mm_tn(rhn[None], rdproj[None], 1536, "ret_dw_in")[0]
    dh, dgm_0 = _proj_bwd(rdproj, ret_win, dh, h1, gm[0], 1536, "ret_proj_bwd")

    dh, dwin_a0, dwout_a0, dg1_0 = ffn_back(dh, h0, xn_a0, g1[0], pg_a0, pu_a0, 0, "ffn1_l0")
    grad_x = dh[CHUNK:][None]

    d_ret_win = _shard_cols(_from_head_major(d_ret_win, RET_H, RET_DK, RET_DV))
    n_gla_in = 2 * GLA_H * GLA_DK + 2 * GLA_H * GLA_DV + GLA_RANK
    d_gla_win = _shard_cols(_from_head_major(d_gla_win[:, :n_gla_in], GLA_H, GLA_DK, GLA_DV))
    sends = [dwin_a0, dwin_a1, dwin_b0, dwin_b1, dwout_a0, dwout_a1, dwout_b0, dwout_b1,
             d_ret_win, d_ret_wout, d_gla_win, d_gla_wout]
    recv = _exchange(sends[0:4], "xchg_ffn_w_in") + _exchange(sends[4:8], "xchg_ffn_w_out") + _exchange(sends[8:12], "xchg_mixers")

    def update(rv, w, m, v, tag):
        shp = w.shape
        w2, m2, v2 = (a.reshape(-1, shp[-1]) for a in (w, m, v))
        outs = _adamw_reduce(rv.reshape(N_DEV, -1, shp[-1]), w2, m2, v2, tag)
        return [o.reshape(shp) for o in outs]

    def update2(rv0, rv1, w, m, v, tag):
        outs0 = update(rv0, w[0], m[0], v[0], tag + "_l0")
        outs1 = update(rv1, w[1], m[1], v[1], tag + "_l1")
        return [jnp.stack([a, b]) for a, b in zip(outs0, outs1)]

    u_ffn1_in = update2(recv[0], recv[1], ffn1_w_in, m_ffn1_w_in, v_ffn1_w_in, "adam_ffn1_w_in")
    u_ffn2_in = update2(recv[2], recv[3], ffn2_w_in, m_ffn2_w_in, v_ffn2_w_in, "adam_ffn2_w_in")
    u_ffn1_out = update2(recv[4], recv[5], ffn1_w_out, m_ffn1_w_out, v_ffn1_w_out, "adam_ffn1_w_out")
    u_ffn2_out = update2(recv[6], recv[7], ffn2_w_out, m_ffn2_w_out, v_ffn2_w_out, "adam_ffn2_w_out")
    u_ret_in = update(recv[8], ret_w_in, m_ret_w_in, v_ret_w_in, "adam_ret_w_in")
    u_ret_out = update(recv[9], ret_w_out, m_ret_w_out, v_ret_w_out, "adam_ret_w_out")
    u_gla_in = update(recv[10], gla_w_in, m_gla_w_in, v_gla_w_in, "adam_gla_w_in")
    u_gla_out = update(recv[11], gla_w_out, m_gla_w_out, v_gla_w_out, "adam_gla_w_out")

    dmeta = dh[PAD:CHUNK]
    parts = jnp.concatenate([
        dg1_0, dg1_1, dgm_0[0], dgm_1[0], dg2_0, dg2_1, dfinal[0], dmeta.reshape(-1), drhn[0], dwg[:GLA_RANK].reshape(-1),
        dbg[0], dghn[0]])
    n_parts = parts.shape[0]
    rows = -(-n_parts // D)
    rows = -(-rows // 8) * 8
    parts = jnp.pad(parts, (0, rows * D - n_parts)).reshape(rows, D)
    tot = _small_reduce(_allgather(parts, "ag_small_grads"), "small_grad_sum").reshape(-1)

    off = 0
    def take(nel):
        nonlocal off
        out = tot[off:off + nel]
        off += nel
        return out

    gr_norm_ffn1 = take(2 * D).reshape(2, D)
    gr_norm_mix = take(2 * D).reshape(2, D)
    gr_norm_ffn2 = take(2 * D).reshape(2, D)
    gr_final = take(D)
    gr_meta = _my_cols(take(N_META * D).reshape(N_META, D), D // N_DEV)
    gr_ret_hn = _my_cols(take(RET_H * RET_DV).reshape(RET_H, RET_DV), RET_DV // N_DEV)[None]
    gr_wgate = _my_cols(take(GLA_RANK * GLA_H * GLA_DK).reshape(GLA_RANK, GLA_H * GLA_DK), GLA_H * GLA_DK // N_DEV)[None]
    gr_bgate = _my_cols(take(GLA_H * GLA_DK).reshape(1, GLA_H * GLA_DK), GLA_H * GLA_DK // N_DEV)
    gr_gla_hn = _my_cols(take(GLA_H * GLA_DV).reshape(GLA_H, GLA_DV), GLA_DV // N_DEV)[None]

    small_w = [meta_tokens, norm_ffn1, norm_mix, norm_ffn2, ret_head_norm, gla_w_gate, gla_b_gate, gla_head_norm, final_norm]
    small_g = [gr_meta, gr_norm_ffn1, gr_norm_mix, gr_norm_ffn2, gr_ret_hn, gr_wgate, gr_bgate, gr_gla_hn, gr_final]
    small_m = [m_meta_tokens, m_norm_ffn1, m_norm_mix, m_norm_ffn2, m_ret_head_norm, m_gla_w_gate, m_gla_b_gate, m_gla_head_norm, m_final_norm]
    small_v = [v_meta_tokens, v_norm_ffn1, v_norm_mix, v_norm_ffn2, v_ret_head_norm, v_gla_w_gate, v_gla_b_gate, v_gla_head_norm, v_final_norm]

    def pack(arrs):
        flat = jnp.concatenate([a.reshape(-1) for a in arrs])
        n = flat.shape[0]
        r = -(-n // 128)
        r = -(-r // 8) * 8
        return jnp.pad(flat, (0, r * 128 - n), constant_values=1.0).reshape(r, 128)

    sd, sm, sv = _adamw_small(pack(small_w), pack(small_g), pack(small_m), pack(small_v), "adam_small")

    def unpack(buf):
        flat = buf.reshape(-1)
        outs, o = [], 0
        for a in small_w:
            outs.append(flat[o:o + a.size].reshape(a.shape))
            o += a.size
        return outs

    us_d, us_m, us_v = unpack(sd), unpack(sm), unpack(sv)

    def ordered(k, smalls):
        return (smalls[0], smalls[1], u_ffn1_in[k], u_ffn1_out[k], smalls[2], smalls[3], u_ffn2_in[k], u_ffn2_out[k],
                u_ret_in[k], smalls[4], u_ret_out[k], u_gla_in[k], smalls[5], smalls[6], smalls[7], u_gla_out[k], smalls[8])

    return (loss, grad_x, *ordered(0, small_g), *ordered(1, us_d), *ordered(2, us_m), *ordered(3, us_v))
```

```python
import functools
import math

import numpy as np
import jax
import jax.numpy as jnp
from jax import lax
from jax.experimental import pallas as pl
from jax.experimental.pallas import tpu as pltpu

F32 = jnp.float32
BF16 = jnp.bfloat16
S = jax.ShapeDtypeStruct
ANY = pl.BlockSpec(memory_space=pl.ANY)
MESH = pl.DeviceIdType.MESH

D = 1024
N_META = 16
CHUNK = 64
PAD = CHUNK - N_META
EPS = 1e-6
N_DEV = 8
FF_SHARD = 704
N_FF_CHUNK = 4
RET_H, RET_DK, RET_DV = 4, 256, 512
RET_HW = 2 * RET_DK + RET_DV
RET_C = 192
GLA_H, GLA_DK, GLA_DV, GLA_RANK, GLA_TAU = 4, 128, 256, 16, 16.0
GLA_HW = 2 * GLA_DK + GLA_DV
GLA_N = 3200
GLA_ZBLK = 3072 // 128
SUB = 16
ROPE_BASE = 10000.0
ADAM_LR, ADAM_B1, ADAM_B2, ADAM_EPS, ADAM_WD, ADAM_STEP = 0.001, 0.9, 0.999, 1e-08, 0.01, 10
VMEM_LIMIT = 58 * 1024 * 1024


def _cp(**kw):
    return pltpu.CompilerParams(vmem_limit_bytes=VMEM_LIMIT, **kw)


def _row_tile(t, cap):
    best = 16
    for d in range(16, cap + 1, 16):
        if t % d == 0:
            best = d
    return best


def _dot(a, b):
    return jnp.dot(a, b, preferred_element_type=F32)


def _dot_nt(a, b):
    return lax.dot_general(a, b, (((1,), (1,)), ((), ())), preferred_element_type=F32)


def _dot_tn(a, b):
    return lax.dot_general(a, b, (((0,), (0,)), ((), ())), preferred_element_type=F32)


def _sigmoid(x):
    return 1.0 / (1.0 + jnp.exp(-x))


def _rms_bwd(dxn, x, gain):
    r = lax.rsqrt(jnp.mean(x * x, axis=-1, keepdims=True) + EPS)
    xh = x * r
    dxh = dxn * gain
    dx = r * (dxh - xh * jnp.mean(dxh * xh, axis=-1, keepdims=True))
    return dx, jnp.sum(dxn * xh, axis=0, keepdims=True)


def _xyc():
    return lax.axis_index("x"), lax.axis_index("y"), lax.axis_index("c")


class _Gather:
    def __init__(self, xs):
        self.xs = list(xs)
        self.n = len(self.xs)

    def out_shape(self):
        return [S((N_DEV,) + a.shape, a.dtype) for a in self.xs]

    def scratch(self):
        return [pltpu.SemaphoreType.DMA((self.n, 7)), pltpu.SemaphoreType.DMA((self.n, 7)), pltpu.SemaphoreType.DMA((self.n,))]

    def phases(self, x_refs, out_refs, send_sems, recv_sems, local_sems):
        x, y, c = _xyc()
        me, sibling = (x, y, c), (x, y, 1 - c)
        chips = [(1 - x, y), (x, 1 - y), (1 - x, 1 - y)]

        def copy(t, k, block, to, src=None):
            px, py, pc = block
            dst = out_refs[t].at[4 * px + 2 * py + pc]
            return pltpu.make_async_remote_copy(
                src_ref=dst if src is None else src, dst_ref=dst,
                send_sem=send_sems.at[t, k], recv_sem=recv_sems.at[t, k], device_id=to, device_id_type=MESH)

        def own(t):
            return pltpu.make_async_copy(x_refs[t], out_refs[t].at[4 * x + 2 * y + c], local_sems.at[t])

        def first(t):
            return [copy(t, 0, me, sibling, src=x_refs[t])] + [
                copy(t, 1 + j, me, (*chip, c), src=x_refs[t]) for j, chip in enumerate(chips)]

        def passed(t):
            return [copy(t, 4 + j, (*chip, c), sibling) for j, chip in enumerate(chips)]

        def start():
            for t in range(self.n):
                own(t).start()
                for cp in first(t):
                    cp.start()

        def mid():
            for t in range(self.n):
                fw = passed(t)
                for j, chip in enumerate(chips):
                    copy(t, 1 + j, (*chip, c), me).wait_recv()
                    fw[j].start()

        def finish():
            for t in range(self.n):
                copy(t, 0, sibling, me).wait_recv()
                for j, chip in enumerate(chips):
                    copy(t, 4 + j, (*chip, 1 - c), me).wait_recv()
                for cp in first(t) + passed(t):
                    cp.wait_send()
                own(t).wait()

        return start, mid, finish


class _Exchange:
    def __init__(self, xs):
        self.xs = list(xs)
        self.n = len(self.xs)

    def out_shape(self):
        return [S(a.shape, a.dtype) for a in self.xs]

    def scratch(self):
        return [pltpu.SemaphoreType.DMA((self.n, 7)), pltpu.SemaphoreType.DMA((self.n, 7)), pltpu.SemaphoreType.DMA((self.n,))]

    def phases(self, g_refs, r_refs, send_sems, recv_sems, local_sems):
        x, y, c = _xyc()
        me = 4 * x + 2 * y + c

        def own(t):
            return pltpu.make_async_copy(g_refs[t].at[me], r_refs[t].at[me], local_sems.at[t])

        def send(t, m):
            px, py, pc = x ^ (m >> 2), y ^ ((m >> 1) & 1), c ^ (m & 1)
            return pltpu.make_async_remote_copy(
                src_ref=g_refs[t].at[4 * px + 2 * py + pc], dst_ref=r_refs[t].at[me],
                send_sem=send_sems.at[t, m - 1], recv_sem=recv_sems.at[t, m - 1],
                device_id=(px, py, pc), device_id_type=MESH)

        def arrival(t, m):
            peer = 4 * (x ^ (m >> 2)) + 2 * (y ^ ((m >> 1) & 1)) + (c ^ (m & 1))
            return pltpu.make_async_remote_copy(
                src_ref=g_refs[t].at[peer], dst_ref=r_refs[t].at[peer],
                send_sem=send_sems.at[t, m - 1], recv_sem=recv_sems.at[t, m - 1],
                device_id=(x, y, c), device_id_type=MESH)

        def start():
            for t in range(self.n):
                own(t).start()
            for m in range(1, N_DEV):
                for t in range(self.n):
                    send(t, m).start()

        def mid():
            pass

        def finish():
            for m in range(1, N_DEV):
                for t in range(self.n):
                    arrival(t, m).wait_recv()
            for m in range(1, N_DEV):
                for t in range(self.n):
                    send(t, m).wait_send()
            for t in range(self.n):
                own(t).wait()

        return start, mid, finish


def _run_side(side, name):
    n = side.n

    def body(*refs):
        start, mid, finish = side.phases(refs[:n], refs[n:2 * n], *refs[2 * n:])
        start()
        mid()
        finish()

    return list(pl.pallas_call(
        body, name=name, out_shape=side.out_shape(), in_specs=[ANY] * n, out_specs=[ANY] * n,
        scratch_shapes=side.scratch())(*side.xs))


def _grid_steps(grid):
    def ids():
        return [pl.program_id(a) for a in range(len(grid))]

    def first():
        return functools.reduce(jnp.logical_and, [i == 0 for i in ids()])

    def middle():
        i = ids()
        return functools.reduce(jnp.logical_and, [i[0] == grid[0] // 2] + [j == 0 for j in i[1:]])

    def last():
        return functools.reduce(jnp.logical_and, [i == g - 1 for i, g in zip(ids(), grid)])

    return first, middle, last


def _call(body, *, name, grid, in_specs, out_specs, out_shape, scratch_shapes, operands, side=None, aliases=None):
    n_in, n_out, n_scr = len(in_specs), len(out_shape), len(scratch_shapes)
    full = body
    if side is not None:
        ns = side.n
        first, middle, last = _grid_steps(grid)

        def full(*refs):
            a = n_in
            ins, sins = refs[:a], refs[a:a + ns]
            a += ns
            outs, souts = refs[a:a + n_out], refs[a + n_out:a + n_out + ns]
            a += n_out + ns
            scr, sems = refs[a:a + n_scr], refs[a + n_scr:]
            start, mid, finish = side.phases(sins, souts, *sems)
            pl.when(first())(start)
            body(*ins, *outs, *scr)
            pl.when(middle())(mid)
            pl.when(last())(finish)

        in_specs = list(in_specs) + [ANY] * ns
        out_specs = list(out_specs) + [ANY] * ns
        out_shape = list(out_shape) + side.out_shape()
        scratch_shapes = list(scratch_shapes) + side.scratch()
        operands = list(operands) + side.xs
    outs = pl.pallas_call(
        full, name=name, grid=grid, in_specs=list(in_specs), out_specs=list(out_specs), out_shape=list(out_shape),
        scratch_shapes=list(scratch_shapes), input_output_aliases=aliases or {},
        compiler_params=_cp(dimension_semantics=("arbitrary",) * len(grid)),
    )(*operands)
    return list(outs[:n_out]), list(outs[n_out:])


def _ffn_fwd(h, gain, win, wout, name, side=None):
    t = h.shape[0]
    tm = _row_tile(t, 704)
    nt = t // tm

    def body(h_ref, g_ref, wg_ref, wu_ref, wo_ref, hn_ref, xn_ref, pg_ref, pu_ref, acc):
        c = pl.program_id(1)

        @pl.when(c == 0)
        def _():
            x = h_ref[...]
            r = lax.rsqrt(jnp.mean(x * x, axis=-1, keepdims=True) + EPS)
            xn_ref[...] = (x * r * g_ref[...]).astype(BF16)
            acc[...] = jnp.zeros_like(acc)

        xn = xn_ref[...]
        g = _dot(xn, wg_ref[...])
        u = _dot(xn, wu_ref[...])
        pg_ref[...] = g.astype(BF16)
        pu_ref[...] = u.astype(BF16)
        act = (g * _sigmoid(g) * u).astype(BF16)
        acc[...] += _dot(act, wo_ref[...].reshape(FF_SHARD, D))

        @pl.when(c == N_FF_CHUNK - 1)
        def _():
            hn_ref[...] = h_ref[...] + 0.5 * acc[...]

    return _call(
        body, name=name, grid=(nt, N_FF_CHUNK), side=side,
        in_specs=[
            pl.BlockSpec((tm, D), lambda i, c: (i, 0)),
            pl.BlockSpec((1, D), lambda i, c: (0, 0)),
            pl.BlockSpec((None, D, FF_SHARD), lambda i, c: (c, 0, 0)),
            pl.BlockSpec((None, D, FF_SHARD), lambda i, c: (c + N_FF_CHUNK, 0, 0)),
            pl.BlockSpec((2, FF_SHARD // 2, D), lambda i, c: (c, 0, 0)),
        ],
        out_specs=[
            pl.BlockSpec((tm, D), lambda i, c: (i, 0)),
            pl.BlockSpec((tm, D), lambda i, c: (i, 0)),
            pl.BlockSpec((None, tm, FF_SHARD), lambda i, c: (c, i, 0)),
            pl.BlockSpec((None, tm, FF_SHARD), lambda i, c: (c, i, 0)),
        ],
        out_shape=[S((t, D), F32), S((t, D), BF16), S((N_FF_CHUNK, t, FF_SHARD), BF16), S((N_FF_CHUNK, t, FF_SHARD), BF16)],
        scratch_shapes=[pltpu.VMEM((tm, D), F32)],
        operands=[h, gain, win, win, wout])


def _ffn_bwd(dh, h, gain, pg, pu, win, wout, name, side=None):
    t = h.shape[0]
    tm = _row_tile(t, 704)
    nt = t // tm

    def body(dh_ref, h_ref, g_ref, pg_ref, pu_ref, wg_ref, wu_ref, wo_ref,
             dhi_ref, dob_ref, dpg_ref, dpu_ref, act_ref, dgain_ref, acc):
        i, c = pl.program_id(0), pl.program_id(1)

        @pl.when(c == 0)
        def _():
            dob_ref[...] = (0.5 * dh_ref[...]).astype(BF16)
            acc[...] = jnp.zeros_like(acc)

        @pl.when((i == 0) & (c == 0))
        def _():
            dgain_ref[...] = jnp.zeros_like(dgain_ref)

        dact = _dot_nt(dob_ref[...], wo_ref[...].reshape(FF_SHARD, D))
        g = pg_ref[...].astype(F32)
        u = pu_ref[...].astype(F32)
        s = _sigmoid(g)
        sl = g * s
        act_ref[...] = (sl * u).astype(BF16)
        dg = (dact * u * (s * (1.0 + g * (1.0 - s)))).astype(BF16)
        du = (dact * sl).astype(BF16)
        dpg_ref[...] = dg
        dpu_ref[...] = du
        acc[...] += _dot_nt(dg, wg_ref[...]) + _dot_nt(du, wu_ref[...])

        @pl.when(c == N_FF_CHUNK - 1)
        def _():
            dx, dgn = _rms_bwd(acc[...], h_ref[...], g_ref[...])
            dhi_ref[...] = dh_ref[...] + dx
            dgain_ref[0:1, :] += dgn

    blk = pl.BlockSpec((None, tm, FF_SHARD), lambda i, c: (c, i, 0))
    row = pl.BlockSpec((tm, D), lambda i, c: (i, 0))
    return _call(
        body, name=name, grid=(nt, N_FF_CHUNK), side=side,
        in_specs=[
            row, row, pl.BlockSpec((1, D), lambda i, c: (0, 0)), blk, blk,
            pl.BlockSpec((None, D, FF_SHARD), lambda i, c: (c, 0, 0)),
            pl.BlockSpec((None, D, FF_SHARD), lambda i, c: (c + N_FF_CHUNK, 0, 0)),
            pl.BlockSpec((2, FF_SHARD // 2, D), lambda i, c: (c, 0, 0)),
        ],
        out_specs=[row, row, blk, blk, blk, pl.BlockSpec((8, D), lambda i, c: (0, 0))],
        out_shape=[S((t, D), F32), S((t, D), BF16)] + [S((N_FF_CHUNK, t, FF_SHARD), BF16)] * 3 + [S((8, D), F32)],
        scratch_shapes=[pltpu.VMEM((tm, D), F32)],
        operands=[dh, h, gain, pg, pu, win, win, wout])


def _ffn_dw_in(xn, dpg, dpu, name, side=None):
    t = xn.shape[0]
    tk = _row_tile(t, 1376)
    nk = t // tk

    def body(a_ref, bg_ref, bu_ref, o_ref, acc):
        c, k = pl.program_id(0), pl.program_id(1)

        @pl.when(k == 0)
        def _():
            acc[...] = jnp.zeros_like(acc)

        @pl.when(c < N_FF_CHUNK)
        def _():
            acc[...] += _dot_tn(a_ref[...], bg_ref[...])

        @pl.when(c >= N_FF_CHUNK)
        def _():
            acc[...] += _dot_tn(a_ref[...], bu_ref[...])

        @pl.when(k == nk - 1)
        def _():
            o_ref[...] = acc[...].astype(BF16)

    return _call(
        body, name=name, grid=(2 * N_FF_CHUNK, nk), side=side,
        in_specs=[
            pl.BlockSpec((tk, D), lambda c, k: (k, 0)),
            pl.BlockSpec((None, tk, FF_SHARD), lambda c, k: (jnp.minimum(c, N_FF_CHUNK - 1), k, 0)),
            pl.BlockSpec((None, tk, FF_SHARD), lambda c, k: (jnp.maximum(c - N_FF_CHUNK, 0), k, 0)),
        ],
        out_specs=[pl.BlockSpec((None, D, FF_SHARD), lambda c, k: (c, 0, 0))],
        out_shape=[S((2 * N_FF_CHUNK, D, FF_SHARD), BF16)],
        scratch_shapes=[pltpu.VMEM((D, FF_SHARD), F32)],
        operands=[xn, dpg, dpu])


def _mm_tn(a, b, tn, name):
    ca, t, m = a.shape
    cb, _, n = b.shape
    nc = max(ca, cb)
    tk = _row_tile(t, 1376)
    nk = t // tk
    nj = n // tn

    def body(a_ref, b_ref, o_ref, acc):
        k = pl.program_id(2)

        @pl.when(k == 0)
        def _():
            acc[...] = jnp.zeros_like(acc)

        acc[...] += _dot_tn(a_ref[...], b_ref[...])

        @pl.when(k == nk - 1)
        def _():
            o_ref[...] = acc[...].astype(BF16)

    return pl.pallas_call(
        body, name=name, grid=(nc, nj, nk),
        in_specs=[
            pl.BlockSpec((None, tk, m), (lambda c, j, k: (c, k, 0)) if ca > 1 else (lambda c, j, k: (0, k, 0))),
            pl.BlockSpec((None, tk, tn), (lambda c, j, k: (c, k, j)) if cb > 1 else (lambda c, j, k: (0, k, j))),
        ],
        out_specs=pl.BlockSpec((None, m, tn), lambda c, j, k: (c, 0, j)),
        out_shape=S((nc, m, n), BF16),
        scratch_shapes=[pltpu.VMEM((m, tn), F32)],
        compiler_params=_cp(dimension_semantics=("arbitrary", "arbitrary", "arbitrary")),
    )(a, b)


def _norm_mm(h, gain, w, tn, name, side=None):
    t = h.shape[0]
    n = w.shape[1]
    tm = _row_tile(t, 704)

    def body(h_ref, g_ref, w_ref, o_ref, xn_ref):
        @pl.when(pl.program_id(1) == 0)
        def _():
            x = h_ref[...]
            r = lax.rsqrt(jnp.mean(x * x, axis=-1, keepdims=True) + EPS)
            xn_ref[...] = (x * r * g_ref[...]).astype(BF16)

        o_ref[...] = _dot(xn_ref[...], w_ref[...]).astype(BF16)

    return _call(
        body, name=name, grid=(t // tm, n // tn), side=side,
        in_specs=[pl.BlockSpec((tm, D), lambda i, j: (i, 0)), pl.BlockSpec((1, D), lambda i, j: (0, 0)),
                  pl.BlockSpec((D, tn), lambda i, j: (0, j))],
        out_specs=[pl.BlockSpec((tm, tn), lambda i, j: (i, j)), pl.BlockSpec((tm, D), lambda i, j: (i, 0))],
        out_shape=[S((t, n), BF16), S((t, D), BF16)], scratch_shapes=[],
        operands=[h, gain, w])


def _proj_bwd(dproj, w, dh, h, gain, tk, name):
    t, n = dproj.shape
    tm = _row_tile(t, 704)
    nk = n // tk

    def body(dp_ref, w_ref, dh_ref, h_ref, g_ref, dhi_ref, dgain_ref, acc):
        i, k = pl.program_id(0), pl.program_id(1)

        @pl.when(k == 0)
        def _():
            acc[...] = jnp.zeros_like(acc)

        @pl.when((i == 0) & (k == 0))
        def _():
            dgain_ref[...] = jnp.zeros_like(dgain_ref)

        acc[...] += _dot_nt(dp_ref[...], w_ref[...])

        @pl.when(k == nk - 1)
        def _():
            dx, dgn = _rms_bwd(acc[...], h_ref[...], g_ref[...])
            dhi_ref[...] = dh_ref[...] + dx
            dgain_ref[0:1, :] += dgn

    row = pl.BlockSpec((tm, D), lambda i, k: (i, 0))
    return pl.pallas_call(
        body, name=name, grid=(t // tm, nk),
        in_specs=[pl.BlockSpec((tm, tk), lambda i, k: (i, k)), pl.BlockSpec((D, tk), lambda i, k: (0, k)),
                  row, row, pl.BlockSpec((1, D), lambda i, k: (0, 0))],
        out_specs=[row, pl.BlockSpec((8, D), lambda i, k: (0, 0))],
        out_shape=[S((t, D), F32), S((8, D), F32)],
        scratch_shapes=[pltpu.VMEM((tm, D), F32)],
        compiler_params=_cp(dimension_semantics=("arbitrary", "arbitrary")),
    )(dproj, w, dh, h, gain)


def _post_fwd(o, proj, hgain, wout, h, nh, dv, name):
    t = h.shape[0]
    w = nh * dv
    tm = _row_tile(t, 704)

    def body(o_ref, g_ref, hg_ref, wo_ref, h_ref, hn_ref, og_ref):
        for hd in range(nh):
            sl = slice(hd * dv, (hd + 1) * dv)
            oh = o_ref[:, sl].astype(F32)
            r = lax.rsqrt(jnp.mean(oh * oh, axis=-1, keepdims=True) + EPS)
            gg = g_ref[:, sl].astype(F32)
            og_ref[:, sl] = (oh * r * hg_ref[:, sl] * (gg * _sigmoid(gg))).astype(BF16)
        hn_ref[...] = h_ref[...] + _dot(og_ref[...], wo_ref[...])

    return pl.pallas_call(
        body, name=name, grid=(t // tm,),
        in_specs=[pl.BlockSpec((tm, w), lambda i: (i, 0)), pl.BlockSpec((tm, w), lambda i: (i, 2)),
                  pl.BlockSpec((1, w), lambda i: (0, 0)), pl.BlockSpec((w, D), lambda i: (0, 0)),
                  pl.BlockSpec((tm, D), lambda i: (i, 0))],
        out_specs=[pl.BlockSpec((tm, D), lambda i: (i, 0)), pl.BlockSpec((tm, w), lambda i: (i, 0))],
        out_shape=[S((t, D), F32), S((t, w), BF16)],
        compiler_params=_cp(dimension_semantics=("arbitrary",)),
    )(o, proj, hgain, wout, h)


def _post_bwd(dh, o, proj, hgain, wout, nh, dv, nproj, name):
    t = dh.shape[0]
    w = nh * dv
    tm = _row_tile(t, 704)

    def body(dh_ref, o_ref, g_ref, hg_ref, wo_ref, do_ref, dg_ref, dhb_ref, dhg_ref):
        @pl.when(pl.program_id(0) == 0)
        def _():
            dhg_ref[...] = jnp.zeros_like(dhg_ref)

        dmix = dh_ref[...].astype(BF16)
        dhb_ref[...] = dmix
        dog = _dot_nt(dmix, wo_ref[...])
        for hd in range(nh):
            sl = slice(hd * dv, (hd + 1) * dv)
            oh = o_ref[:, sl].astype(F32)
            r = lax.rsqrt(jnp.mean(oh * oh, axis=-1, keepdims=True) + EPS)
            xh = oh * r
            gain = hg_ref[:, sl]
            gg = g_ref[:, sl].astype(F32)
            s = _sigmoid(gg)
            dogh = dog[:, sl]
            don = dogh * (gg * s)
            dg_ref[:, sl] = (dogh * (xh * gain) * (s * (1.0 + gg * (1.0 - s)))).astype(BF16)
            dxh = don * gain
            do_ref[:, sl] = (r * (dxh - xh * jnp.mean(dxh * xh, axis=-1, keepdims=True))).astype(BF16)
            dhg_ref[0:1, sl] += jnp.sum(don * xh, axis=0, keepdims=True)

    return pl.pallas_call(
        body, name=name, grid=(t // tm,),
        in_specs=[pl.BlockSpec((tm, D), lambda i: (i, 0)), pl.BlockSpec((tm, w), lambda i: (i, 0)),
                  pl.BlockSpec((tm, w), lambda i: (i, 2)), pl.BlockSpec((1, w), lambda i: (0, 0)),
                  pl.BlockSpec((w, D), lambda i: (0, 0))],
        out_specs=[pl.BlockSpec((tm, w), lambda i: (i, 0)), pl.BlockSpec((tm, w), lambda i: (i, 2)),
                   pl.BlockSpec((tm, D), lambda i: (i, 0)), pl.BlockSpec((8, w), lambda i: (0, 0))],
        out_shape=[S((t, w), BF16), S((t, nproj), BF16), S((t, D), BF16), S((8, w), F32)],
        compiler_params=_cp(dimension_semantics=("arbitrary",)),
    )(dh, o, proj, hgain, wout)


def _ret_consts():
    lg = np.log1p(-np.exp2(-5.0 - np.arange(RET_H, dtype=np.float32))).astype(np.float32)
    return jnp.asarray(np.broadcast_to(lg[:, None, None], (RET_H, 1, 128)).copy())


def _rope_tables(t):
    half = RET_DK // 2
    inv = 1.0 / (ROPE_BASE ** jnp.linspace(0.0, 1.0, half, dtype=F32))
    pos = jnp.maximum(jnp.arange(t) - PAD, 0).astype(F32)
    ang = pos[:, None] * inv[None, :]
    return jnp.cos(ang), jnp.sin(ang)


def _ret_chunk(blk_ref, cos_ref, sin_ref, lg):
    c = RET_C
    half = RET_DK // 2
    cs, sn = cos_ref[...], sin_ref[...]
    q1, q2 = blk_ref[:, 0:half].astype(F32), blk_ref[:, half:RET_DK].astype(F32)
    k1, k2 = blk_ref[:, RET_DK:RET_DK + half].astype(F32), blk_ref[:, RET_DK + half:2 * RET_DK].astype(F32)
    qr = jnp.concatenate([q1 * cs - q2 * sn, q1 * sn + q2 * cs], axis=1)
    kr = jnp.concatenate([k1 * cs - k2 * sn, k1 * sn + k2 * cs], axis=1) * (RET_DK ** -0.5)
    v = blk_ref[:, 2 * RET_DK:RET_HW]
    ii = lax.broadcasted_iota(jnp.int32, (c, 1), 0).astype(F32)
    jj = lax.broadcasted_iota(jnp.int32, (1, c), 1).astype(F32)
    rel = ii - jj
    dmat = jnp.where(rel >= 0, jnp.exp(lg * jnp.maximum(rel, 0.0)), 0.0)
    dq = jnp.exp(lg * (ii + 1.0))
    dk = jnp.exp(lg * (c - 1.0 - ii))
    dchunk = jnp.exp(lg * float(c))
    return qr, kr, v, dmat, dq, dk, dchunk


def _ret_scan_fwd(proj, cos, sin, lgam, name):
    t = proj.shape[0]
    c = RET_C
    nc = t // c

    def body(blk_ref, cos_ref, sin_ref, lg_ref, o_ref, st_ref, state):
        @pl.when(pl.program_id(1) == 0)
        def _():
            state[...] = jnp.zeros_like(state)

        lg = lg_ref[:, 0:1]
        qr, kr, v, dmat, dq, dk, dchunk = _ret_chunk(blk_ref, cos_ref, sin_ref, lg)
        sp = state[...]
        st_ref[...] = sp.astype(BF16)
        scores = _dot_nt(qr.astype(BF16), kr.astype(BF16)) * dmat
        o = _dot(scores.astype(BF16), v) + _dot((qr * dq).astype(BF16), sp.astype(BF16))
        o_ref[...] = o.astype(BF16)
        state[...] = sp * dchunk + _dot_tn((kr * dk).astype(BF16), v)

    return pl.pallas_call(
        body, name=name, grid=(RET_H, nc),
        in_specs=[pl.BlockSpec((c, RET_HW), lambda h, n: (n, h)), pl.BlockSpec((c, 128), lambda h, n: (n, 0)),
                  pl.BlockSpec((c, 128), lambda h, n: (n, 0)), pl.BlockSpec((None, 1, 128), lambda h, n: (h, 0, 0))],
        out_specs=[pl.BlockSpec((c, RET_DV), lambda h, n: (n, h)),
                   pl.BlockSpec((None, None, RET_DK, RET_DV), lambda h, n: (h, n, 0, 0))],
        out_shape=[S((t, RET_H * RET_DV), BF16), S((RET_H, nc, RET_DK, RET_DV), BF16)],
        scratch_shapes=[pltpu.VMEM((RET_DK, RET_DV), F32)],
        compiler_params=_cp(dimension_semantics=("arbitrary", "arbitrary")),
    )(proj, cos, sin, lgam)


def _ret_scan_bwd(proj, cos, sin, lgam, do, states, dproj, name, side=None):
    t = proj.shape[0]
    c = RET_C
    nc = t // c
    half = RET_DK // 2

    def body(blk_ref, cos_ref, sin_ref, lg_ref, do_ref, st_ref, dp_in, dp_ref, dstate):
        n = nc - 1 - pl.program_id(1)

        @pl.when(pl.program_id(1) == 0)
        def _():
            dstate[...] = jnp.zeros_like(dstate)

        lg = lg_ref[:, 0:1]
        qr, kr, v, dmat, dq, dk, dchunk = _ret_chunk(blk_ref, cos_ref, sin_ref, lg)
        qb, kb = qr.astype(BF16), kr.astype(BF16)
        dob = do_ref[...]
        sp = st_ref[...]
        ds = dstate[...]
        dsb = ds.astype(BF16)
        p = (_dot_nt(qb, kb) * dmat).astype(BF16)
        dvv = _dot_tn(p, dob) + _dot((kr * dk).astype(BF16), dsb)
        dp = (_dot_nt(dob, v) * dmat).astype(BF16)
        dqr = _dot(dp, kb) + _dot_nt(dob, sp) * dq
        dkr = (_dot_tn(dp, qb) + _dot_nt(v, dsb) * dk) * (RET_DK ** -0.5)
        dstate[...] = ds * dchunk + _dot_tn((qr * dq).astype(BF16), dob)
        cs, sn = cos_ref[...], sin_ref[...]
        rows = n * c + lax.broadcasted_iota(jnp.int32, (c, 1), 0)
        keep = rows >= PAD

        def unrot(d):
            d1, d2 = d[:, :half], d[:, half:]
            return jnp.concatenate([d1 * cs + d2 * sn, d2 * cs - d1 * sn], axis=1)

        out = jnp.concatenate([unrot(dqr), unrot(dkr), dvv], axis=1)
        dp_ref[...] = jnp.where(keep, out, 0.0).astype(BF16)

    nproj = dproj.shape[1]
    return _call(
        body, name=name, grid=(RET_H, nc), side=side, aliases={6: 0},
        in_specs=[pl.BlockSpec((c, RET_HW), lambda h, n: (nc - 1 - n, h)), pl.BlockSpec((c, 128), lambda h, n: (nc - 1 - n, 0)),
                  pl.BlockSpec((c, 128), lambda h, n: (nc - 1 - n, 0)), pl.BlockSpec((None, 1, 128), lambda h, n: (h, 0, 0)),
                  pl.BlockSpec((c, RET_DV), lambda h, n: (nc - 1 - n, h)),
                  pl.BlockSpec((None, None, RET_DK, RET_DV), lambda h, n: (h, nc - 1 - n, 0, 0)), ANY],
        out_specs=[pl.BlockSpec((c, RET_HW), lambda h, n: (nc - 1 - n, h))],
        out_shape=[S((t, nproj), BF16)],
        scratch_shapes=[pltpu.VMEM((RET_DK, RET_DV), F32)],
        operands=[proj, cos, sin, lgam, do, states, dproj])


def _split3(x):
    hi = x.astype(BF16)
    r1 = x - hi.astype(F32)
    mid = r1.astype(BF16)
    lo = (r1 - mid.astype(F32)).astype(BF16)
    return hi, mid, lo


def _gla_chunk(blk_ref, z_ref, wg_ref, bg_ref, n):
    c = CHUNK
    q = blk_ref[:, 0:GLA_DK].astype(F32) * (GLA_DK ** -0.5)
    k = blk_ref[:, GLA_DK:2 * GLA_DK].astype(F32)
    v = blk_ref[:, 2 * GLA_DK:GLA_HW]
    u = _dot(z_ref[...], wg_ref[...]) + bg_ref[...]
    la = (jnp.minimum(u, 0.0) - jnp.log(1.0 + jnp.exp(-jnp.abs(u)))) * (1.0 / GLA_TAU)
    rows = n * c + lax.broadcasted_iota(jnp.int32, (c, 1), 0)
    keep = rows >= PAD
    la = jnp.where(keep, la, 0.0)
    ii = lax.broadcasted_iota(jnp.int32, (c, c), 0)
    jj = lax.broadcasted_iota(jnp.int32, (c, c), 1)
    tril = (ii >= jj).astype(BF16)
    hi, mid, lo = _split3(la)
    b = _dot(tril, hi) + _dot(tril, mid) + _dot(tril, lo)
    return q, k, v, u, b, keep


def _gla_intra(q, k, b, a_ref):
    c = CHUNK
    col = lax.broadcasted_iota(jnp.int32, (1, c), 1)
    rowi = lax.broadcasted_iota(jnp.int32, (SUB, 1), 0)
    for blk in range(c // SUB):
        r = slice(SUB * blk, SUB * (blk + 1))
        b_i = b[r]
        q_i, k_i = q[r], k[r]
        if blk > 0:
            bprev = b[SUB * blk - 1:SUB * blk]
            qe = q_i * jnp.exp(b_i - bprev)
            kt = k * jnp.exp(jnp.minimum(bprev - b, 0.0))
            arow = jnp.where(col < SUB * blk, _dot_nt(qe.astype(BF16), kt.astype(BF16)), 0.0)
        else:
            arow = jnp.zeros((SUB, c), F32)
        for j in range(SUB):
            e = jnp.where(rowi >= j, jnp.exp(jnp.minimum(b_i - b_i[j:j + 1], 0.0)), 0.0)
            a = jnp.sum(q_i * k_i[j:j + 1] * e, axis=1, keepdims=True)
            arow = jnp.where(col == SUB * blk + j, a, arow)
        a_ref[r, :] = arow


def _gla_scan_fwd(proj, wgp, bg, name):
    t = proj.shape[0]
    c = CHUNK
    nc = t // c

    def body(blk_ref, z_ref, wg_ref, bg_ref, o_ref, st_ref, state, a_ref):
        n = pl.program_id(1)

        @pl.when(n == 0)
        def _():
            state[...] = jnp.zeros_like(state)

        q, k, v, u, b, keep = _gla_chunk(blk_ref, z_ref, wg_ref, bg_ref, n)
        _gla_intra(q, k, b, a_ref)
        sp = state[...]
        st_ref[...] = sp.astype(BF16)
        o = _dot(a_ref[...].astype(BF16), v) + _dot_nt((q * jnp.exp(b)).astype(BF16), sp.astype(BF16))
        o_ref[...] = o.astype(BF16)
        bc = b[c - 1:c]
        state[...] = sp * jnp.exp(bc) + _dot_tn(v, (k * jnp.exp(bc - b)).astype(BF16))

    return pl.pallas_call(
        body, name=name, grid=(GLA_H, nc),
        in_specs=[pl.BlockSpec((c, GLA_HW), lambda h, n: (n, h)), pl.BlockSpec((c, 128), lambda h, n: (n, GLA_ZBLK)),
                  pl.BlockSpec((128, GLA_DK), lambda h, n: (0, h)), pl.BlockSpec((1, GLA_DK), lambda h, n: (0, h))],
        out_specs=[pl.BlockSpec((c, GLA_DV), lambda h, n: (n, h)),
                   pl.BlockSpec((None, None, GLA_DV, GLA_DK), lambda h, n: (h, n, 0, 0))],
        out_shape=[S((t, GLA_H * GLA_DV), BF16), S((GLA_H, nc, GLA_DV, GLA_DK), BF16)],
        scratch_shapes=[pltpu.VMEM((GLA_DV, GLA_DK), F32), pltpu.VMEM((c, c), F32)],
        compiler_params=_cp(dimension_semantics=("arbitrary", "arbitrary")),
    )(proj, proj, wgp, bg)


def _gla_scan_bwd(proj, wgp, bg, do, states, dproj, name):
    t = proj.shape[0]
    c = CHUNK
    nc = t // c

    def body(blk_ref, z_ref, wg_ref, bg_ref, do_ref, st_ref, dp_in, dp_ref, du_ref, dstate, a_ref, dq_ref, dkd_ref):
        n = nc - 1 - pl.program_id(1)

        @pl.when(pl.program_id(1) == 0)
        def _():
            dstate[...] = jnp.zeros_like(dstate)

        q, k, v, u, b, keep = _gla_chunk(blk_ref, z_ref, wg_ref, bg_ref, n)
        _gla_intra(q, k, b, a_ref)
        ab = a_ref[...].astype(BF16)
        dob = do_ref[...]
        sp = st_ref[...]
        ds = dstate[...]
        dsb = ds.astype(BF16)
        bc = b[c - 1:c]
        eb = jnp.exp(b)
        ebc = jnp.exp(bc - b)
        ec = jnp.exp(bc)
        qb = (q * eb).astype(BF16)
        kb = (k * ebc).astype(BF16)
        dvv = _dot_tn(ab, dob) + _dot_nt(kb, dsb)
        ii = lax.broadcasted_iota(jnp.int32, (c, c), 0)
        jj = lax.broadcasted_iota(jnp.int32, (c, c), 1)
        da = jnp.where(ii >= jj, _dot_nt(dob, v), 0.0)
        dq_inter = _dot(dob, sp) * eb
        dk_state = _dot(v, dsb) * ebc
        dstate[...] = ds * ec + _dot_tn(dob, qb)

        col = lax.broadcasted_iota(jnp.int32, (1, c), 1)
        rowi = lax.broadcasted_iota(jnp.int32, (SUB, 1), 0)
        dk = jnp.zeros((c, GLA_DK), F32)
        for blk in range(c // SUB):
            r = slice(SUB * blk, SUB * (blk + 1))
            b_i = b[r]
            q_i, k_i = q[r], k[r]
            darow = da[r]
            if blk > 0:
                bprev = b[SUB * blk - 1:SUB * blk]
                e_i = jnp.exp(b_i - bprev)
                ek = jnp.exp(jnp.minimum(bprev - b, 0.0))
                daoff = jnp.where(col < SUB * blk, darow, 0.0).astype(BF16)
                dq_i = _dot(daoff, (k * ek).astype(BF16)) * e_i
                dk = dk + _dot_tn(daoff, (q_i * e_i).astype(BF16)) * ek
            else:
                dq_i = jnp.zeros((SUB, GLA_DK), F32)
            dkd = jnp.zeros((SUB, GLA_DK), F32)
            for j in range(SUB):
                e = jnp.where(rowi >= j, jnp.exp(jnp.minimum(b_i - b_i[j:j + 1], 0.0)), 0.0)
                dacol = jnp.sum(jnp.where(col == SUB * blk + j, darow, 0.0), axis=1, keepdims=True)
                tt = dacol * e
                dq_i = dq_i + tt * k_i[j:j + 1]
                dkd = jnp.where(rowi == j, jnp.sum(tt * q_i, axis=0, keepdims=True), dkd)
            dq_ref[r, :] = dq_i
            dkd_ref[r, :] = dkd
        dq = dq_ref[...] + dq_inter
        dk = dk + dkd_ref[...] + dk_state
        extra = jnp.sum(k * dk_state, axis=0, keepdims=True) + ec * jnp.sum(sp.astype(F32) * ds, axis=0, keepdims=True)
        rowc = lax.broadcasted_iota(jnp.int32, (c, 1), 0)
        db = q * dq - k * dk + jnp.where(rowc == c - 1, extra, 0.0)
        triu = (ii <= jj).astype(BF16)
        hi, mid, lo = _split3(db)
        dla = _dot(triu, hi) + _dot(triu, mid) + _dot(triu, lo)
        du = jnp.where(keep, dla * (1.0 / GLA_TAU) / (1.0 + jnp.exp(u)), 0.0)
        du_ref[...] = du.astype(BF16)
        out = jnp.concatenate([dq * (GLA_DK ** -0.5), dk, dvv], axis=1)
        dp_ref[...] = jnp.where(keep, out, 0.0).astype(BF16)

    nproj = dproj.shape[1]
    return pl.pallas_call(
        body, name=name, grid=(GLA_H, nc),
        in_specs=[pl.BlockSpec((c, GLA_HW), lambda h, n: (nc - 1 - n, h)), pl.BlockSpec((c, 128), lambda h, n: (nc - 1 - n, GLA_ZBLK)),
                  pl.BlockSpec((128, GLA_DK), lambda h, n: (0, h)), pl.BlockSpec((1, GLA_DK), lambda h, n: (0, h)),
                  pl.BlockSpec((c, GLA_DV), lambda h, n: (nc - 1 - n, h)),
                  pl.BlockSpec((None, None, GLA_DV, GLA_DK), lambda h, n: (h, nc - 1 - n, 0, 0)), ANY],
        out_specs=[pl.BlockSpec((c, GLA_HW), lambda h, n: (nc - 1 - n, h)), pl.BlockSpec((c, GLA_DK), lambda h, n: (nc - 1 - n, h))],
        out_shape=[S((t, nproj), BF16), S((t, GLA_H * GLA_DK), BF16)],
        input_output_aliases={6: 0},
        scratch_shapes=[pltpu.VMEM((GLA_DV, GLA_DK), F32), pltpu.VMEM((c, c), F32),
                        pltpu.VMEM((c, GLA_DK), F32), pltpu.VMEM((c, GLA_DK), F32)],
        compiler_params=_cp(dimension_semantics=("arbitrary", "arbitrary")),
    )(proj, proj, wgp, bg, do, states, dproj)


def _gla_gate_bwd(du, proj, wgp, dproj, name):
    t = du.shape[0]
    tm = _row_tile(t, 704)
    w = GLA_H * GLA_DK

    def body(du_ref, z_ref, wg_ref, dp_in, dp_ref, dwg_ref, dbg_ref):
        @pl.when(pl.program_id(0) == 0)
        def _():
            dwg_ref[...] = jnp.zeros_like(dwg_ref)
            dbg_ref[...] = jnp.zeros_like(dbg_ref)

        d = du_ref[...]
        dp_ref[...] = _dot_nt(d, wg_ref[...]).astype(BF16)
        dwg_ref[...] += _dot_tn(z_ref[...], d)
        dbg_ref[0:1, :] += jnp.sum(d.astype(F32), axis=0, keepdims=True)

    return pl.pallas_call(
        body, name=name, grid=(t // tm,),
        in_specs=[pl.BlockSpec((tm, w), lambda i: (i, 0)), pl.BlockSpec((tm, 128), lambda i: (i, GLA_ZBLK)),
                  pl.BlockSpec((128, w), lambda i: (0, 0)), ANY],
        out_specs=[pl.BlockSpec((tm, 128), lambda i: (i, GLA_ZBLK)), pl.BlockSpec((128, w), lambda i: (0, 0)),
                   pl.BlockSpec((8, w), lambda i: (0, 0))],
        out_shape=[S(dproj.shape, BF16), S((128, w), F32), S((8, w), F32)],
        input_output_aliases={3: 0},
        compiler_params=_cp(dimension_semantics=("arbitrary",)),
    )(du, proj, wgp, dproj)


def _final_loss(hx, gain, target, name):
    t = hx.shape[0]
    tm = _row_tile(t, 512)

    def body(h_ref, g_ref, t_ref, dh_ref, dgain_ref, loss_ref):
        @pl.when(pl.program_id(0) == 0)
        def _():
            dgain_ref[...] = jnp.zeros_like(dgain_ref)
            loss_ref[...] = jnp.zeros_like(loss_ref)

        x = h_ref[...]
        gain = g_ref[...]
        r = lax.rsqrt(jnp.mean(x * x, axis=-1, keepdims=True) + EPS)
        xh = x * r
        e = xh * gain - t_ref[...]
        loss_ref[...] += 0.5 * jnp.sum(jnp.mean(e * e, axis=-1, keepdims=True), axis=0, keepdims=True)
        dy = e * (1.0 / D)
        dgain_ref[0:1, :] += jnp.sum(dy * xh, axis=0, keepdims=True)
        dxh = dy * gain
        dh_ref[...] = r * (dxh - xh * jnp.mean(dxh * xh, axis=-1, keepdims=True))

    row = pl.BlockSpec((tm, D), lambda i: (i, 0))
    return pl.pallas_call(
        body, name=name, grid=(t // tm,),
        in_specs=[row, pl.BlockSpec((1, D), lambda i: (0, 0)), row],
        out_specs=[row, pl.BlockSpec((8, D), lambda i: (0, 0)), pl.BlockSpec((8, 128), lambda i: (0, 0))],
        out_shape=[S((t, D), F32), S((8, D), F32), S((8, 128), F32)],
        compiler_params=_cp(dimension_semantics=("arbitrary",)),
    )(hx, gain, target)


def _adam_math(w, g, m, v):
    m2 = ADAM_B1 * m + (1.0 - ADAM_B1) * g
    v2 = ADAM_B2 * v + (1.0 - ADAM_B2) * (g * g)
    m_hat = m2 / (1.0 - ADAM_B1 ** ADAM_STEP)
    v_hat = v2 / (1.0 - ADAM_B2 ** ADAM_STEP)
    delta = -ADAM_LR * (m_hat / (jnp.sqrt(v_hat) + ADAM_EPS) + ADAM_WD * w)
    return delta, m2, v2


def _adamw_reduce(recvs, w, m, v, name):
    nl, r, wd = w.shape
    tr = _row_tile(r, 256) if r % 16 == 0 else r
    nr = r // tr

    def body(*refs):
        rv_refs = refs[:nl]
        w_ref, m_ref, v_ref, g_ref, d_ref, m2_ref, v2_ref = refs[nl:]
        layer = pl.program_id(0)

        def total(rv_ref):
            g = rv_ref[0].astype(F32)
            for s in range(1, N_DEV):
                g = g + rv_ref[s].astype(F32)
            return g

        g = total(rv_refs[0])
        for k in range(1, nl):
            g = jnp.where(layer == k, total(rv_refs[k]), g)
        g_ref[...] = g
        d_ref[...], m2_ref[...], v2_ref[...] = _adam_math(w_ref[...], g, m_ref[...], v_ref[...])

    def rv_spec(k):
        return pl.BlockSpec((N_DEV, tr, wd), lambda l, i: (0, jnp.where(l == k, i, jnp.where(l < k, 0, nr - 1)), 0))

    row = pl.BlockSpec((None, tr, wd), lambda l, i: (l, i, 0))
    return pl.pallas_call(
        body, name=name, grid=(nl, nr),
        in_specs=[rv_spec(k) for k in range(nl)] + [row, row, row],
        out_specs=[row] * 4, out_shape=[S((nl, r, wd), F32)] * 4,
        compiler_params=_cp(dimension_semantics=("arbitrary", "arbitrary")),
    )(*recvs, w, m, v)


def _small_reduce(parts, name):
    _, r, wd = parts.shape

    def body(p_ref, o_ref):
        g = p_ref[0]
        for s in range(1, N_DEV):
            g = g + p_ref[s]
        o_ref[...] = g

    return pl.pallas_call(body, name=name, out_shape=S((r, wd), F32), compiler_params=_cp())(parts)


def _adamw_small(w, g, m, v, name):
    def body(w_ref, g_ref, m_ref, v_ref, d_ref, m2_ref, v2_ref):
        d_ref[...], m2_ref[...], v2_ref[...] = _adam_math(w_ref[...], g_ref[...], m_ref[...], v_ref[...])

    return pl.pallas_call(body, name=name, out_shape=[S(w.shape, F32)] * 3, compiler_params=_cp())(w, g, m, v)


def _to_head_major(w, nh, dk, dv):
    kk = w.shape[0]
    q = w[:, :nh * dk].reshape(kk, nh, dk)
    k = w[:, nh * dk:2 * nh * dk].reshape(kk, nh, dk)
    v = w[:, 2 * nh * dk:2 * nh * dk + nh * dv].reshape(kk, nh, dv)
    heads = jnp.concatenate([q, k, v], axis=-1).reshape(kk, nh * (2 * dk + dv))
    return jnp.concatenate([heads, w[:, 2 * nh * dk + nh * dv:]], axis=1)


def _from_head_major(p, nh, dk, dv):
    kk = p.shape[0]
    hw = 2 * dk + dv
    heads = p[:, :nh * hw].reshape(kk, nh, hw)
    q = heads[:, :, :dk].reshape(kk, nh * dk)
    k = heads[:, :, dk:2 * dk].reshape(kk, nh * dk)
    v = heads[:, :, 2 * dk:].reshape(kk, nh * dv)
    return jnp.concatenate([q, k, v, p[:, nh * hw:]], axis=1)


def _unshard_cols(g):
    return jnp.transpose(g, (1, 0, 2)).reshape(g.shape[1], N_DEV * g.shape[2])


def _shard_cols(w):
    k, n8 = w.shape
    return jnp.transpose(w.reshape(k, N_DEV, n8 // N_DEV), (1, 0, 2))


def _my_cols(full, width):
    me = 4 * lax.axis_index("x") + 2 * lax.axis_index("y") + lax.axis_index("c")
    return lax.dynamic_slice_in_dim(full, me * width, width, axis=1)


def kernel(x, meta_tokens, norm_ffn1, ffn1_w_in, ffn1_w_out, norm_mix, norm_ffn2, ffn2_w_in, ffn2_w_out, ret_w_in, ret_head_norm, ret_w_out, gla_w_in, gla_w_gate, gla_b_gate, gla_head_norm, gla_w_out, final_norm, loss_target, m_meta_tokens, m_norm_ffn1, m_ffn1_w_in, m_ffn1_w_out, m_norm_mix, m_norm_ffn2, m_ffn2_w_in, m_ffn2_w_out, m_ret_w_in, m_ret_head_norm, m_ret_w_out, m_gla_w_in, m_gla_w_gate, m_gla_b_gate, m_gla_head_norm, m_gla_w_out, m_final_norm, v_meta_tokens, v_norm_ffn1, v_ffn1_w_in, v_ffn1_w_out, v_norm_mix, v_norm_ffn2, v_ffn2_w_in, v_ffn2_w_out, v_ret_w_in, v_ret_head_norm, v_ret_w_out, v_gla_w_in, v_gla_w_gate, v_gla_b_gate, v_gla_head_norm, v_gla_w_out, v_final_norm):
    seq = x.shape[1]
    t = seq + CHUNK
    xs = x[0]
    target = loss_target[0]

    def ffn_w(f):
        w_in, w_out = (ffn1_w_in, ffn1_w_out) if f < 2 else (ffn2_w_in, ffn2_w_out)
        return [w_in[f % 2].astype(BF16), w_out[f % 2].astype(BF16)]

    small = jnp.concatenate([meta_tokens.reshape(-1), ret_head_norm.reshape(-1), gla_w_gate.reshape(-1),
                             gla_b_gate.reshape(-1), gla_head_norm.reshape(-1)])
    n_small = small.shape[0]
    small = jnp.pad(small, (0, 32 * 128 - n_small)).reshape(32, 128)
    sg, win0, wout0 = _run_side(_Gather([small] + ffn_w(0)), "ag_first")
    sg = sg.reshape(N_DEV, 32 * 128)

    def small_cols(off, rows, width):
        return jnp.transpose(sg[:, off:off + rows * width].reshape(N_DEV, rows, width), (1, 0, 2)).reshape(rows, N_DEV * width)

    off = 0
    meta_full = small_cols(off, N_META, D // N_DEV); off += N_META * (D // N_DEV)
    ret_hn = small_cols(off, RET_H, RET_DV // N_DEV).reshape(1, RET_H * RET_DV); off += RET_H * RET_DV // N_DEV
    wgate = small_cols(off, GLA_RANK, GLA_H * GLA_DK // N_DEV); off += GLA_RANK * GLA_H * GLA_DK // N_DEV
    bgate = small_cols(off, 1, GLA_H * GLA_DK // N_DEV); off += GLA_H * GLA_DK // N_DEV
    gla_hn = small_cols(off, GLA_H, GLA_DV // N_DEV).reshape(1, GLA_H * GLA_DV)
    wgp = jnp.pad(wgate, ((0, 128 - GLA_RANK), (0, 0))).astype(BF16)

    cos, sin = _rope_tables(t)
    lgam = _ret_consts()

    h0 = jnp.concatenate([jnp.zeros((PAD, D), F32), meta_full, xs], axis=0)
    g1 = [norm_ffn1[i:i + 1] for i in range(2)]
    gm = [norm_mix[i:i + 1] for i in range(2)]
    g2 = [norm_ffn2[i:i + 1] for i in range(2)]

    (h1, xn_a0, pg_a0, pu_a0), (ret_win_g, ret_wout_g) = _ffn_fwd(
        h0, g1[0], win0, wout0, "ffn1_l0_fwd", side=_Gather([ret_w_in[0].astype(BF16), ret_w_out[0].astype(BF16)]))
    ret_win = _to_head_major(_unshard_cols(ret_win_g), RET_H, RET_DK, RET_DV)
    ret_wout = ret_wout_g.reshape(RET_H * RET_DV, D)
    (rproj, rhn), (win2, wout2) = _norm_mm(h1, gm[0], ret_win, 1536, "ret_proj_fwd", side=_Gather(ffn_w(2)))
    ro, rstates = _ret_scan_fwd(rproj, cos, sin, lgam, "ret_scan_fwd")
    h2, rog = _post_fwd(ro, rproj, ret_hn, ret_wout, h1, RET_H, RET_DV, "ret_post_fwd")
    (h3, xn_b0, pg_b0, pu_b0), (win1, wout1) = _ffn_fwd(h2, g2[0], win2, wout2, "ffn2_l0_fwd", side=_Gather(ffn_w(1)))
    (h4, xn_a1, pg_a1, pu_a1), (gla_win_g, gla_wout_g) = _ffn_fwd(
        h3, g1[1], win1, wout1, "ffn1_l1_fwd", side=_Gather([gla_w_in[0].astype(BF16), gla_w_out[0].astype(BF16)]))
    gla_win = _to_head_major(_unshard_cols(gla_win_g), GLA_H, GLA_DK, GLA_DV)
    gla_win = jnp.pad(gla_win, ((0, 0), (0, GLA_N - gla_win.shape[1])))
    gla_wout = gla_wout_g.reshape(GLA_H * GLA_DV, D)
    (gproj, ghn), (win3, wout3) = _norm_mm(h4, gm[1], gla_win, 640, "gla_proj_fwd", side=_Gather(ffn_w(3)))
    go, gstates = _gla_scan_fwd(gproj, wgp, bgate, "gla_scan_fwd")
    h5, gog = _post_fwd(go, gproj, gla_hn, gla_wout, h4, GLA_H, GLA_DV, "gla_post_fwd")
    (h6, xn_b1, pg_b1, pu_b1), _ = _ffn_fwd(h5, g2[1], win3, wout3, "ffn2_l1_fwd")

    dhx, dfinal, loss_blk = _final_loss(h6[CHUNK:], final_norm.reshape(1, D), target, "final_loss")
    loss = lax.psum(loss_blk[0, 0], ("x", "y", "c"))
    dh = jnp.concatenate([jnp.zeros((CHUNK, D), F32), dhx], axis=0)

    def ffn_back(dh, h_in, xn, gain, pg, pu, win, wout, tag, side=None, dw_side=None):
        (dh_in, dob, dpg, dpu, act, dgain), got = _ffn_bwd(dh, h_in, gain, pg, pu, win, wout, tag + "_bwd", side=side)
        (dwin,), dw_got = _ffn_dw_in(xn, dpg, dpu, tag + "_dw_in", side=dw_side)
        dwout = _mm_tn(act, dob[None], D, tag + "_dw_out").reshape(N_DEV, FF_SHARD // 2, D)
        return dh_in, [dwin, dwout], dgain[0], got, dw_got

    dh, dw_b1, dg2_1, _, _ = ffn_back(dh, h5, xn_b1, g2[1], pg_b1, pu_b1, win3, wout3, "ffn2_l1")

    gdo, gdproj, gdhb, dghn = _post_bwd(dh, go, gproj, gla_hn, gla_wout, GLA_H, GLA_DV, GLA_N, "gla_post_bwd")
    d_gla_wout = _mm_tn(gog[None], gdhb[None], D, "gla_dw_out").reshape(N_DEV, GLA_H * GLA_DV // N_DEV, D)
    gdproj, gdu = _gla_scan_bwd(gproj, wgp, bgate, gdo, gstates, gdproj, "gla_scan_bwd")
    gdproj, dwg, dbg = _gla_gate_bwd(gdu, gproj, wgp, gdproj, "gla_gate_bwd")
    d_gla_win = _mm_tn(ghn[None], gdproj[None], 640, "gla_dw_in")[0]
    dh, dgm_1 = _proj_bwd(gdproj, gla_win, dh, h4, gm[1], 640, "gla_proj_bwd")
    n_gla_in = 2 * GLA_H * GLA_DK + 2 * GLA_H * GLA_DV + GLA_RANK
    d_gla_win = _shard_cols(_from_head_major(d_gla_win[:, :n_gla_in], GLA_H, GLA_DK, GLA_DV))

    dh, dw_a1, dg1_1, rv_b1, rv_gla = ffn_back(dh, h3, xn_a1, g1[1], pg_a1, pu_a1, win1, wout1, "ffn1_l1",
                                               side=_Exchange(dw_b1), dw_side=_Exchange([d_gla_win, d_gla_wout]))
    dh, dw_b0, dg2_0, rv_a1, _ = ffn_back(dh, h2, xn_b0, g2[0], pg_b0, pu_b0, win2, wout2, "ffn2_l0", side=_Exchange(dw_a1))

    rdo, rdproj, rdhb, drhn = _post_bwd(dh, ro, rproj, ret_hn, ret_wout, RET_H, RET_DV, 6 * D, "ret_post_bwd")
    d_ret_wout = _mm_tn(rog[None], rdhb[None], D, "ret_dw_out").reshape(N_DEV, RET_H * RET_DV // N_DEV, D)
    (rdproj,), rv_b0 = _ret_scan_bwd(rproj, cos, sin, lgam, rdo, rstates, rdproj, "ret_scan_bwd", side=_Exchange(dw_b0))
    d_ret_win = _mm_tn(rhn[None], rdproj[None], 1536, "ret_dw_in")[0]
    dh, dgm_0 = _proj_bwd(rdproj, ret_win, dh, h1, gm[0], 1536, "ret_proj_bwd")
    d_ret_win = _shard_cols(_from_head_major(d_ret_win, RET_H, RET_DK, RET_DV))

    dh, dw_a0, dg1_0, rv_ret, _ = ffn_back(dh, h0, xn_a0, g1[0], pg_a0, pu_a0, win0, wout0, "ffn1_l0",
                                           side=_Exchange([d_ret_win, d_ret_wout]))
    rv_a0 = _run_side(_Exchange(dw_a0), "xchg_last")
    grad_x = dh[CHUNK:][None]

    u_ffn1_in = _adamw_reduce([rv_a0[0], rv_a1[0]], ffn1_w_in, m_ffn1_w_in, v_ffn1_w_in, "adam_ffn1_w_in")
    u_ffn2_in = _adamw_reduce([rv_b0[0], rv_b1[0]], ffn2_w_in, m_ffn2_w_in, v_ffn2_w_in, "adam_ffn2_w_in")
    u_ffn1_out = _adamw_reduce([rv_a0[1], rv_a1[1]], ffn1_w_out, m_ffn1_w_out, v_ffn1_w_out, "adam_ffn1_w_out")
    u_ffn2_out = _adamw_reduce([rv_b0[1], rv_b1[1]], ffn2_w_out, m_ffn2_w_out, v_ffn2_w_out, "adam_ffn2_w_out")
    u_ret_in = _adamw_reduce([rv_ret[0]], ret_w_in, m_ret_w_in, v_ret_w_in, "adam_ret_w_in")
    u_ret_out = _adamw_reduce([rv_ret[1]], ret_w_out, m_ret_w_out, v_ret_w_out, "adam_ret_w_out")
    u_gla_in = _adamw_reduce([rv_gla[0]], gla_w_in, m_gla_w_in, v_gla_w_in, "adam_gla_w_in")
    u_gla_out = _adamw_reduce([rv_gla[1]], gla_w_out, m_gla_w_out, v_gla_w_out, "adam_gla_w_out")

    dmeta = dh[PAD:CHUNK]
    parts = jnp.concatenate([
        dg1_0, dg1_1, dgm_0[0], dgm_1[0], dg2_0, dg2_1, dfinal[0], dmeta.reshape(-1), drhn[0], dwg[:GLA_RANK].reshape(-1),
        dbg[0], dghn[0]])
    n_parts = parts.shape[0]
    rows = -(-n_parts // D)
    rows = -(-rows // 8) * 8
    parts = jnp.pad(parts, (0, rows * D - n_parts)).reshape(rows, D)
    tot = _small_reduce(_run_side(_Gather([parts]), "ag_small_grads")[0], "small_grad_sum").reshape(-1)

    off = 0
    def take(nel):
        nonlocal off
        out = tot[off:off + nel]
        off += nel
        return out

    gr_norm_ffn1 = take(2 * D).reshape(2, D)
    gr_norm_mix = take(2 * D).reshape(2, D)
    gr_norm_ffn2 = take(2 * D).reshape(2, D)
    gr_final = take(D)
    gr_meta = _my_cols(take(N_META * D).reshape(N_META, D), D // N_DEV)
    gr_ret_hn = _my_cols(take(RET_H * RET_DV).reshape(RET_H, RET_DV), RET_DV // N_DEV)[None]
    gr_wgate = _my_cols(take(GLA_RANK * GLA_H * GLA_DK).reshape(GLA_RANK, GLA_H * GLA_DK), GLA_H * GLA_DK // N_DEV)[None]
    gr_bgate = _my_cols(take(GLA_H * GLA_DK).reshape(1, GLA_H * GLA_DK), GLA_H * GLA_DK // N_DEV)
    gr_gla_hn = _my_cols(take(GLA_H * GLA_DV).reshape(GLA_H, GLA_DV), GLA_DV // N_DEV)[None]

    small_w = [meta_tokens, norm_ffn1, norm_mix, norm_ffn2, ret_head_norm, gla_w_gate, gla_b_gate, gla_head_norm, final_norm]
    small_g = [gr_meta, gr_norm_ffn1, gr_norm_mix, gr_norm_ffn2, gr_ret_hn, gr_wgate, gr_bgate, gr_gla_hn, gr_final]
    small_m = [m_meta_tokens, m_norm_ffn1, m_norm_mix, m_norm_ffn2, m_ret_head_norm, m_gla_w_gate, m_gla_b_gate, m_gla_head_norm, m_final_norm]
    small_v = [v_meta_tokens, v_norm_ffn1, v_norm_mix, v_norm_ffn2, v_ret_head_norm, v_gla_w_gate, v_gla_b_gate, v_gla_head_norm, v_final_norm]

    def pack(arrs):
        flat = jnp.concatenate([a.reshape(-1) for a in arrs])
        n = flat.shape[0]
        r = -(-n // 128)
        r = -(-r // 8) * 8
        return jnp.pad(flat, (0, r * 128 - n), constant_values=1.0).reshape(r, 128)

    sd, sm, sv = _adamw_small(pack(small_w), pack(small_g), pack(small_m), pack(small_v), "adam_small")

    def unpack(buf):
        flat = buf.reshape(-1)
        outs, o = [], 0
        for a in small_w:
            outs.append(flat[o:o + a.size].reshape(a.shape))
            o += a.size
        return outs

    us_d, us_m, us_v = unpack(sd), unpack(sm), unpack(sv)

    def ordered(k, smalls):
        return (smalls[0], smalls[1], u_ffn1_in[k], u_ffn1_out[k], smalls[2], smalls[3], u_ffn2_in[k], u_ffn2_out[k],
                u_ret_in[k], smalls[4], u_ret_out[k], u_gla_in[k], smalls[5], smalls[6], smalls[7], u_gla_out[k], smalls[8])

    return (loss, grad_x, *ordered(0, small_g), *ordered(1, us_d), *ordered(2, us_m), *ordered(3, us_v))
```

```python
import functools
import math

import numpy as np
import jax
import jax.numpy as jnp
from jax import lax
from jax.experimental import pallas as pl
from jax.experimental.pallas import tpu as pltpu

F32 = jnp.float32
BF16 = jnp.bfloat16
S = jax.ShapeDtypeStruct
ANY = pl.BlockSpec(memory_space=pl.ANY)
MESH = pl.DeviceIdType.MESH

D = 1024
N_META = 16
CHUNK = 64
PAD = CHUNK - N_META
EPS = 1e-6
N_DEV = 8
FF_SHARD = 704
N_FF_CHUNK = 4
RET_H, RET_DK, RET_DV = 4, 256, 512
RET_HW = 2 * RET_DK + RET_DV
RET_C = 192
GLA_H, GLA_DK, GLA_DV, GLA_RANK, GLA_TAU = 4, 128, 256, 16, 16.0
GLA_HW = 2 * GLA_DK + GLA_DV
GLA_N = 3200
GLA_ZBLK = 3072 // 128
SUB = 16
ROPE_BASE = 10000.0
ADAM_LR, ADAM_B1, ADAM_B2, ADAM_EPS, ADAM_WD, ADAM_STEP = 0.001, 0.9, 0.999, 1e-08, 0.01, 10
VMEM_LIMIT = 58 * 1024 * 1024


def _cp(**kw):
    return pltpu.CompilerParams(vmem_limit_bytes=VMEM_LIMIT, **kw)


def _row_tile(t, cap):
    best = 16
    for d in range(16, cap + 1, 16):
        if t % d == 0:
            best = d
    return best


def _sub_rows(tm, parts=2):
    units = tm // 16
    cuts = [16 * (units * p // parts) for p in range(parts + 1)]
    return [slice(a, b) for a, b in zip(cuts[:-1], cuts[1:]) if b > a]


def _dot(a, b):
    return jnp.dot(a, b, preferred_element_type=F32)


def _dot_nt(a, b):
    return lax.dot_general(a, b, (((1,), (1,)), ((), ())), preferred_element_type=F32)


def _dot_tn(a, b):
    return lax.dot_general(a, b, (((0,), (0,)), ((), ())), preferred_element_type=F32)


def _sigmoid(x):
    return pl.reciprocal(1.0 + jnp.exp(-x), approx=True)


def _rms_bwd(dxn, x, gain):
    r = lax.rsqrt(jnp.mean(x * x, axis=-1, keepdims=True) + EPS)
    xh = x * r
    dxh = dxn * gain
    dx = r * (dxh - xh * jnp.mean(dxh * xh, axis=-1, keepdims=True))
    return dx, jnp.sum(dxn * xh, axis=0, keepdims=True)


def _xyc():
    return lax.axis_index("x"), lax.axis_index("y"), lax.axis_index("c")


class _Gather:
    def __init__(self, xs):
        self.xs = list(xs)
        self.n = len(self.xs)

    def out_shape(self):
        return [S((N_DEV,) + a.shape, a.dtype) for a in self.xs]

    def scratch(self):
        return [pltpu.SemaphoreType.DMA((self.n, 7)), pltpu.SemaphoreType.DMA((self.n, 7)), pltpu.SemaphoreType.DMA((self.n,))]

    def phases(self, x_refs, out_refs, send_sems, recv_sems, local_sems):
        x, y, c = _xyc()
        me, sibling = (x, y, c), (x, y, 1 - c)
        chips = [(1 - x, y), (x, 1 - y), (1 - x, 1 - y)]

        def copy(t, k, block, to, src=None):
            px, py, pc = block
            dst = out_refs[t].at[4 * px + 2 * py + pc]
            return pltpu.make_async_remote_copy(
                src_ref=dst if src is None else src, dst_ref=dst,
                send_sem=send_sems.at[t, k], recv_sem=recv_sems.at[t, k], device_id=to, device_id_type=MESH)

        def own(t):
            return pltpu.make_async_copy(x_refs[t], out_refs[t].at[4 * x + 2 * y + c], local_sems.at[t])

        def first(t):
            return [copy(t, 0, me, sibling, src=x_refs[t])] + [
                copy(t, 1 + j, me, (*chip, c), src=x_refs[t]) for j, chip in enumerate(chips)]

        def passed(t):
            return [copy(t, 4 + j, (*chip, c), sibling) for j, chip in enumerate(chips)]

        def start():
            for t in range(self.n):
                own(t).start()
                for cp in first(t):
                    cp.start()

        def mid():
            for t in range(self.n):
                fw = passed(t)
                for j, chip in enumerate(chips):
                    copy(t, 1 + j, (*chip, c), me).wait_recv()
                    fw[j].start()

        def finish():
            for t in range(self.n):
                copy(t, 0, sibling, me).wait_recv()
                for j, chip in enumerate(chips):
                    copy(t, 4 + j, (*chip, 1 - c), me).wait_recv()
                for cp in first(t) + passed(t):
                    cp.wait_send()
                own(t).wait()

        return start, mid, finish


class _Exchange:
    def __init__(self, xs):
        self.xs = list(xs)
        self.n = len(self.xs)

    def out_shape(self):
        return [S(a.shape, a.dtype) for a in self.xs]

    def scratch(self):
        return [pltpu.SemaphoreType.DMA((self.n, 7)), pltpu.SemaphoreType.DMA((self.n, 7)), pltpu.SemaphoreType.DMA((self.n,))]

    def phases(self, g_refs, r_refs, send_sems, recv_sems, local_sems):
        x, y, c = _xyc()
        me = 4 * x + 2 * y + c

        def own(t):
            return pltpu.make_async_copy(g_refs[t].at[me], r_refs[t].at[me], local_sems.at[t])

        def send(t, m):
            px, py, pc = x ^ (m >> 2), y ^ ((m >> 1) & 1), c ^ (m & 1)
            return pltpu.make_async_remote_copy(
                src_ref=g_refs[t].at[4 * px + 2 * py + pc], dst_ref=r_refs[t].at[me],
                send_sem=send_sems.at[t, m - 1], recv_sem=recv_sems.at[t, m - 1],
                device_id=(px, py, pc), device_id_type=MESH)

        def arrival(t, m):
            peer = 4 * (x ^ (m >> 2)) + 2 * (y ^ ((m >> 1) & 1)) + (c ^ (m & 1))
            return pltpu.make_async_remote_copy(
                src_ref=g_refs[t].at[peer], dst_ref=r_refs[t].at[peer],
                send_sem=send_sems.at[t, m - 1], recv_sem=recv_sems.at[t, m - 1],
                device_id=(x, y, c), device_id_type=MESH)

        def start():
            for t in range(self.n):
                own(t).start()
            for m in range(1, N_DEV):
                for t in range(self.n):
                    send(t, m).start()

        def mid():
            pass

        def finish():
            for m in range(1, N_DEV):
                for t in range(self.n):
                    arrival(t, m).wait_recv()
            for m in range(1, N_DEV):
                for t in range(self.n):
                    send(t, m).wait_send()
            for t in range(self.n):
                own(t).wait()

        return start, mid, finish


def _run_side(side, name):
    n = side.n

    def body(*refs):
        start, mid, finish = side.phases(refs[:n], refs[n:2 * n], *refs[2 * n:])
        start()
        mid()
        finish()

    return list(pl.pallas_call(
        body, name=name, out_shape=side.out_shape(), in_specs=[ANY] * n, out_specs=[ANY] * n,
        scratch_shapes=side.scratch())(*side.xs))


def _grid_steps(grid):
    def ids():
        return [pl.program_id(a) for a in range(len(grid))]

    def first():
        return functools.reduce(jnp.logical_and, [i == 0 for i in ids()])

    def middle():
        i = ids()
        return functools.reduce(jnp.logical_and, [i[0] == (3 * grid[0]) // 4] + [j == 0 for j in i[1:]])

    def last():
        return functools.reduce(jnp.logical_and, [i == g - 1 for i, g in zip(ids(), grid)])

    return first, middle, last


def _call(body, *, name, grid, in_specs, out_specs, out_shape, scratch_shapes, operands, side=None, aliases=None):
    n_in, n_out, n_scr = len(in_specs), len(out_shape), len(scratch_shapes)
    full = body
    if side is not None:
        ns = side.n
        first, middle, last = _grid_steps(grid)

        def full(*refs):
            a = n_in
            ins, sins = refs[:a], refs[a:a + ns]
            a += ns
            outs, souts = refs[a:a + n_out], refs[a + n_out:a + n_out + ns]
            a += n_out + ns
            scr, sems = refs[a:a + n_scr], refs[a + n_scr:]
            start, mid, finish = side.phases(sins, souts, *sems)
            pl.when(first())(start)
            body(*ins, *outs, *scr)
            pl.when(middle())(mid)
            pl.when(last())(finish)

        in_specs = list(in_specs) + [ANY] * ns
        out_specs = list(out_specs) + [ANY] * ns
        out_shape = list(out_shape) + side.out_shape()
        scratch_shapes = list(scratch_shapes) + side.scratch()
        operands = list(operands) + side.xs
    outs = pl.pallas_call(
        full, name=name, grid=grid, in_specs=list(in_specs), out_specs=list(out_specs), out_shape=list(out_shape),
        scratch_shapes=list(scratch_shapes), input_output_aliases=aliases or {},
        compiler_params=_cp(dimension_semantics=("arbitrary",) * len(grid)),
    )(*operands)
    return list(outs[:n_out]), list(outs[n_out:])


def _ffn_fwd(h, gain, win, wout, name, side=None):
    t = h.shape[0]
    tm = _row_tile(t, 704)
    nt = t // tm

    def body(h_ref, g_ref, wg_ref, wu_ref, wo_ref, hn_ref, xn_ref, pg_ref, pu_ref, acc):
        c = pl.program_id(1)

        @pl.when(c == 0)
        def _():
            x = h_ref[...]
            r = lax.rsqrt(jnp.mean(x * x, axis=-1, keepdims=True) + EPS)
            xn_ref[...] = (x * r * g_ref[...]).astype(BF16)
            acc[...] = jnp.zeros_like(acc)

        wo = wo_ref[...].reshape(FF_SHARD, D)
        subs = _sub_rows(tm)
        gus = [(_dot(xn_ref[r, :], wg_ref[...]), _dot(xn_ref[r, :], wu_ref[...])) for r in subs]
        for r, (g, u) in zip(subs, gus):
            pg_ref[r, :] = g.astype(BF16)
            pu_ref[r, :] = u.astype(BF16)
            act = (g * _sigmoid(g) * u).astype(BF16)
            acc[r, :] += _dot(act, wo)

        @pl.when(c == N_FF_CHUNK - 1)
        def _():
            hn_ref[...] = h_ref[...] + 0.5 * acc[...]

    return _call(
        body, name=name, grid=(nt, N_FF_CHUNK), side=side,
        in_specs=[
            pl.BlockSpec((tm, D), lambda i, c: (i, 0)),
            pl.BlockSpec((1, D), lambda i, c: (0, 0)),
            pl.BlockSpec((None, D, FF_SHARD), lambda i, c: (c, 0, 0)),
            pl.BlockSpec((None, D, FF_SHARD), lambda i, c: (c + N_FF_CHUNK, 0, 0)),
            pl.BlockSpec((2, FF_SHARD // 2, D), lambda i, c: (c, 0, 0)),
        ],
        out_specs=[
            pl.BlockSpec((tm, D), lambda i, c: (i, 0)),
            pl.BlockSpec((tm, D), lambda i, c: (i, 0)),
            pl.BlockSpec((None, tm, FF_SHARD), lambda i, c: (c, i, 0)),
            pl.BlockSpec((None, tm, FF_SHARD), lambda i, c: (c, i, 0)),
        ],
        out_shape=[S((t, D), F32), S((t, D), BF16), S((N_FF_CHUNK, t, FF_SHARD), BF16), S((N_FF_CHUNK, t, FF_SHARD), BF16)],
        scratch_shapes=[pltpu.VMEM((tm, D), F32)],
        operands=[h, gain, win, win, wout])


def _ffn_bwd(dh, h, gain, pg, pu, win, wout, name, side=None):
    t = h.shape[0]
    tm = _row_tile(t, 704)
    nt = t // tm

    def body(dh_ref, h_ref, g_ref, pg_ref, pu_ref, wg_ref, wu_ref, wo_ref,
             dhi_ref, dob_ref, dpg_ref, dpu_ref, act_ref, dgain_ref, acc):
        i, c = pl.program_id(0), pl.program_id(1)

        @pl.when(c == 0)
        def _():
            dob_ref[...] = (0.5 * dh_ref[...]).astype(BF16)
            acc[...] = jnp.zeros_like(acc)

        @pl.when((i == 0) & (c == 0))
        def _():
            dgain_ref[...] = jnp.zeros_like(dgain_ref)

        wo = wo_ref[...].reshape(FF_SHARD, D)
        subs = _sub_rows(tm)
        dacts = [_dot_nt(dob_ref[r, :], wo) for r in subs]
        for r, dact in zip(subs, dacts):
            g = pg_ref[r, :].astype(F32)
            u = pu_ref[r, :].astype(F32)
            s = _sigmoid(g)
            sl = g * s
            act_ref[r, :] = (sl * u).astype(BF16)
            dg = (dact * u * (s * (1.0 + g * (1.0 - s)))).astype(BF16)
            du = (dact * sl).astype(BF16)
            dpg_ref[r, :] = dg
            dpu_ref[r, :] = du
            acc[r, :] += _dot_nt(dg, wg_ref[...]) + _dot_nt(du, wu_ref[...])

        @pl.when(c == N_FF_CHUNK - 1)
        def _():
            dx, dgn = _rms_bwd(acc[...], h_ref[...], g_ref[...])
            dhi_ref[...] = dh_ref[...] + dx
            dgain_ref[0:1, :] += dgn

    blk = pl.BlockSpec((None, tm, FF_SHARD), lambda i, c: (c, i, 0))
    row = pl.BlockSpec((tm, D), lambda i, c: (i, 0))
    return _call(
        body, name=name, grid=(nt, N_FF_CHUNK), side=side,
        in_specs=[
            row, row, pl.BlockSpec((1, D), lambda i, c: (0, 0)), blk, blk,
            pl.BlockSpec((None, D, FF_SHARD), lambda i, c: (c, 0, 0)),
            pl.BlockSpec((None, D, FF_SHARD), lambda i, c: (c + N_FF_CHUNK, 0, 0)),
            pl.BlockSpec((2, FF_SHARD // 2, D), lambda i, c: (c, 0, 0)),
        ],
        out_specs=[row, row, blk, blk, blk, pl.BlockSpec((8, D), lambda i, c: (0, 0))],
        out_shape=[S((t, D), F32), S((t, D), BF16)] + [S((N_FF_CHUNK, t, FF_SHARD), BF16)] * 3 + [S((8, D), F32)],
        scratch_shapes=[pltpu.VMEM((tm, D), F32)],
        operands=[dh, h, gain, pg, pu, win, win, wout])


def _ffn_dw_in(xn, dpg, dpu, name, side=None):
    t = xn.shape[0]
    tk = _row_tile(t, 1376)
    nk = t // tk

    def body(a_ref, bg_ref, bu_ref, o_ref, acc):
        c, k = pl.program_id(0), pl.program_id(1)

        @pl.when(k == 0)
        def _():
            acc[...] = jnp.zeros_like(acc)

        @pl.when(c < N_FF_CHUNK)
        def _():
            acc[...] += _dot_tn(a_ref[...], bg_ref[...])

        @pl.when(c >= N_FF_CHUNK)
        def _():
            acc[...] += _dot_tn(a_ref[...], bu_ref[...])

        @pl.when(k == nk - 1)
        def _():
            o_ref[...] = acc[...].astype(BF16)

    return _call(
        body, name=name, grid=(2 * N_FF_CHUNK, nk), side=side,
        in_specs=[
            pl.BlockSpec((tk, D), lambda c, k: (k, 0)),
            pl.BlockSpec((None, tk, FF_SHARD), lambda c, k: (jnp.minimum(c, N_FF_CHUNK - 1), k, 0)),
            pl.BlockSpec((None, tk, FF_SHARD), lambda c, k: (jnp.maximum(c - N_FF_CHUNK, 0), k, 0)),
        ],
        out_specs=[pl.BlockSpec((None, D, FF_SHARD), lambda c, k: (c, 0, 0))],
        out_shape=[S((2 * N_FF_CHUNK, D, FF_SHARD), BF16)],
        scratch_shapes=[pltpu.VMEM((D, FF_SHARD), F32)],
        operands=[xn, dpg, dpu])


def _mm_tn(a, b, tn, name):
    ca, t, m = a.shape
    cb, _, n = b.shape
    nc = max(ca, cb)
    tk = _row_tile(t, 1376)
    nk = t // tk
    nj = n // tn

    def body(a_ref, b_ref, o_ref, acc):
        k = pl.program_id(2)

        @pl.when(k == 0)
        def _():
            acc[...] = jnp.zeros_like(acc)

        acc[...] += _dot_tn(a_ref[...], b_ref[...])

        @pl.when(k == nk - 1)
        def _():
            o_ref[...] = acc[...].astype(BF16)

    return pl.pallas_call(
        body, name=name, grid=(nc, nj, nk),
        in_specs=[
            pl.BlockSpec((None, tk, m), (lambda c, j, k: (c, k, 0)) if ca > 1 else (lambda c, j, k: (0, k, 0))),
            pl.BlockSpec((None, tk, tn), (lambda c, j, k: (c, k, j)) if cb > 1 else (lambda c, j, k: (0, k, j))),
        ],
        out_specs=pl.BlockSpec((None, m, tn), lambda c, j, k: (c, 0, j)),
        out_shape=S((nc, m, n), BF16),
        scratch_shapes=[pltpu.VMEM((m, tn), F32)],
        compiler_params=_cp(dimension_semantics=("arbitrary", "arbitrary", "arbitrary")),
    )(a, b)


def _norm_mm(h, gain, w, tn, name, side=None):
    t = h.shape[0]
    n = w.shape[1]
    tm = _row_tile(t, 704)

    def body(h_ref, g_ref, w_ref, o_ref, xn_ref):
        @pl.when(pl.program_id(1) == 0)
        def _():
            x = h_ref[...]
            r = lax.rsqrt(jnp.mean(x * x, axis=-1, keepdims=True) + EPS)
            xn_ref[...] = (x * r * g_ref[...]).astype(BF16)

        o_ref[...] = _dot(xn_ref[...], w_ref[...]).astype(BF16)

    return _call(
        body, name=name, grid=(t // tm, n // tn), side=side,
        in_specs=[pl.BlockSpec((tm, D), lambda i, j: (i, 0)), pl.BlockSpec((1, D), lambda i, j: (0, 0)),
                  pl.BlockSpec((D, tn), lambda i, j: (0, j))],
        out_specs=[pl.BlockSpec((tm, tn), lambda i, j: (i, j)), pl.BlockSpec((tm, D), lambda i, j: (i, 0))],
        out_shape=[S((t, n), BF16), S((t, D), BF16)], scratch_shapes=[],
        operands=[h, gain, w])


def _proj_bwd(dproj, w, dh, h, gain, tk, name, side=None):
    t, n = dproj.shape
    tm = _row_tile(t, 704)
    nk = n // tk

    def body(dp_ref, w_ref, dh_ref, h_ref, g_ref, dhi_ref, dgain_ref, acc):
        i, k = pl.program_id(0), pl.program_id(1)

        @pl.when(k == 0)
        def _():
            acc[...] = jnp.zeros_like(acc)

        @pl.when((i == 0) & (k == 0))
        def _():
            dgain_ref[...] = jnp.zeros_like(dgain_ref)

        acc[...] += _dot_nt(dp_ref[...], w_ref[...])

        @pl.when(k == nk - 1)
        def _():
            dx, dgn = _rms_bwd(acc[...], h_ref[...], g_ref[...])
            dhi_ref[...] = dh_ref[...] + dx
            dgain_ref[0:1, :] += dgn

    row = pl.BlockSpec((tm, D), lambda i, k: (i, 0))
    return _call(
        body, name=name, grid=(t // tm, nk), side=side,
        in_specs=[pl.BlockSpec((tm, tk), lambda i, k: (i, k)), pl.BlockSpec((D, tk), lambda i, k: (0, k)),
                  row, row, pl.BlockSpec((1, D), lambda i, k: (0, 0))],
        out_specs=[row, pl.BlockSpec((8, D), lambda i, k: (0, 0))],
        out_shape=[S((t, D), F32), S((8, D), F32)],
        scratch_shapes=[pltpu.VMEM((tm, D), F32)],
        operands=[dproj, w, dh, h, gain])


def _post_fwd(o, proj, hgain, wout, h, nh, dv, name):
    t = h.shape[0]
    w = nh * dv
    tm = _row_tile(t, 704)

    def body(o_ref, g_ref, hg_ref, wo_ref, h_ref, hn_ref, og_ref):
        for hd in range(nh):
            sl = slice(hd * dv, (hd + 1) * dv)
            oh = o_ref[:, sl].astype(F32)
            r = lax.rsqrt(jnp.mean(oh * oh, axis=-1, keepdims=True) + EPS)
            gg = g_ref[:, sl].astype(F32)
            og_ref[:, sl] = (oh * r * hg_ref[:, sl] * (gg * _sigmoid(gg))).astype(BF16)
        hn_ref[...] = h_ref[...] + _dot(og_ref[...], wo_ref[...])

    return pl.pallas_call(
        body, name=name, grid=(t // tm,),
        in_specs=[pl.BlockSpec((tm, w), lambda i: (i, 0)), pl.BlockSpec((tm, w), lambda i: (i, 2)),
                  pl.BlockSpec((1, w), lambda i: (0, 0)), pl.BlockSpec((w, D), lambda i: (0, 0)),
                  pl.BlockSpec((tm, D), lambda i: (i, 0))],
        out_specs=[pl.BlockSpec((tm, D), lambda i: (i, 0)), pl.BlockSpec((tm, w), lambda i: (i, 0))],
        out_shape=[S((t, D), F32), S((t, w), BF16)],
        compiler_params=_cp(dimension_semantics=("arbitrary",)),
    )(o, proj, hgain, wout, h)


def _post_bwd(dh, o, proj, hgain, wout, nh, dv, nproj, name, side=None):
    t = dh.shape[0]
    w = nh * dv
    tm = _row_tile(t, 704)

    def body(dh_ref, o_ref, g_ref, hg_ref, wo_ref, do_ref, dg_ref, dhb_ref, dhg_ref):
        @pl.when(pl.program_id(0) == 0)
        def _():
            dhg_ref[...] = jnp.zeros_like(dhg_ref)

        dmix = dh_ref[...].astype(BF16)
        dhb_ref[...] = dmix
        dog = _dot_nt(dmix, wo_ref[...])
        for hd in range(nh):
            sl = slice(hd * dv, (hd + 1) * dv)
            oh = o_ref[:, sl].astype(F32)
            r = lax.rsqrt(jnp.mean(oh * oh, axis=-1, keepdims=True) + EPS)
            xh = oh * r
            gain = hg_ref[:, sl]
            gg = g_ref[:, sl].astype(F32)
            s = _sigmoid(gg)
            dogh = dog[:, sl]
            don = dogh * (gg * s)
            dg_ref[:, sl] = (dogh * (xh * gain) * (s * (1.0 + gg * (1.0 - s)))).astype(BF16)
            dxh = don * gain
            do_ref[:, sl] = (r * (dxh - xh * jnp.mean(dxh * xh, axis=-1, keepdims=True))).astype(BF16)
            dhg_ref[0:1, sl] += jnp.sum(don * xh, axis=0, keepdims=True)

    return _call(
        body, name=name, grid=(t // tm,), side=side,
        in_specs=[pl.BlockSpec((tm, D), lambda i: (i, 0)), pl.BlockSpec((tm, w), lambda i: (i, 0)),
                  pl.BlockSpec((tm, w), lambda i: (i, 2)), pl.BlockSpec((1, w), lambda i: (0, 0)),
                  pl.BlockSpec((w, D), lambda i: (0, 0))],
        out_specs=[pl.BlockSpec((tm, w), lambda i: (i, 0)), pl.BlockSpec((tm, w), lambda i: (i, 2)),
                   pl.BlockSpec((tm, D), lambda i: (i, 0)), pl.BlockSpec((8, w), lambda i: (0, 0))],
        out_shape=[S((t, w), BF16), S((t, nproj), BF16), S((t, D), BF16), S((8, w), F32)], scratch_shapes=[],
        operands=[dh, o, proj, hgain, wout])


def _ret_consts():
    lg = np.log1p(-np.exp2(-5.0 - np.arange(RET_H, dtype=np.float32))).astype(np.float32)
    return jnp.asarray(np.broadcast_to(lg[:, None, None], (RET_H, 1, 128)).copy())


def _rope_tables(t):
    half = RET_DK // 2
    inv = 1.0 / (ROPE_BASE ** jnp.linspace(0.0, 1.0, half, dtype=F32))
    pos = jnp.maximum(jnp.arange(t) - PAD, 0).astype(F32)
    ang = pos[:, None] * inv[None, :]
    return jnp.cos(ang), jnp.sin(ang)


def _ret_chunk(blk_ref, cos_ref, sin_ref, lg, h):
    c = RET_C
    half = RET_DK // 2
    o = h * RET_HW
    cs, sn = cos_ref[...], sin_ref[...]
    q1, q2 = blk_ref[:, o:o + half].astype(F32), blk_ref[:, o + half:o + RET_DK].astype(F32)
    k1, k2 = blk_ref[:, o + RET_DK:o + RET_DK + half].astype(F32), blk_ref[:, o + RET_DK + half:o + 2 * RET_DK].astype(F32)
    qr = jnp.concatenate([q1 * cs - q2 * sn, q1 * sn + q2 * cs], axis=1)
    kr = jnp.concatenate([k1 * cs - k2 * sn, k1 * sn + k2 * cs], axis=1) * (RET_DK ** -0.5)
    v = blk_ref[:, o + 2 * RET_DK:o + RET_HW]
    ii = lax.broadcasted_iota(jnp.int32, (c, 1), 0).astype(F32)
    jj = lax.broadcasted_iota(jnp.int32, (1, c), 1).astype(F32)
    rel = ii - jj
    dmat = jnp.where(rel >= 0, jnp.exp(lg * jnp.maximum(rel, 0.0)), 0.0)
    dq = jnp.exp(lg * (ii + 1.0))
    dk = jnp.exp(lg * (c - 1.0 - ii))
    dchunk = jnp.exp(lg * float(c))
    return qr, kr, v, dmat, dq, dk, dchunk


def _ret_scan_fwd(proj, cos, sin, lgam, name):
    t = proj.shape[0]
    c = RET_C
    nc = t // c

    def body(blk_ref, cos_ref, sin_ref, lg_ref, o_ref, st_ref, state):
        @pl.when(pl.program_id(0) == 0)
        def _():
            state[...] = jnp.zeros_like(state)

        for h in range(RET_H):
            qr, kr, v, dmat, dq, dk, dchunk = _ret_chunk(blk_ref, cos_ref, sin_ref, lg_ref[h, :, 0:1], h)
            sp = state[h]
            st_ref[h] = sp.astype(BF16)
            scores = _dot_nt(qr.astype(BF16), kr.astype(BF16)) * dmat
            o = _dot(scores.astype(BF16), v) + _dot((qr * dq).astype(BF16), sp.astype(BF16))
            o_ref[:, h * RET_DV:(h + 1) * RET_DV] = o.astype(BF16)
            state[h] = sp * dchunk + _dot_tn((kr * dk).astype(BF16), v)

    return pl.pallas_call(
        body, name=name, grid=(nc,),
        in_specs=[pl.BlockSpec((c, RET_H * RET_HW), lambda n: (n, 0)), pl.BlockSpec((c, 128), lambda n: (n, 0)),
                  pl.BlockSpec((c, 128), lambda n: (n, 0)), pl.BlockSpec((RET_H, 1, 128), lambda n: (0, 0, 0))],
        out_specs=[pl.BlockSpec((c, RET_H * RET_DV), lambda n: (n, 0)),
                   pl.BlockSpec((RET_H, None, RET_DK, RET_DV), lambda n: (0, n, 0, 0))],
        out_shape=[S((t, RET_H * RET_DV), BF16), S((RET_H, nc, RET_DK, RET_DV), BF16)],
        scratch_shapes=[pltpu.VMEM((RET_H, RET_DK, RET_DV), F32)],
        compiler_params=_cp(dimension_semantics=("arbitrary",)),
    )(proj, cos, sin, lgam)


def _ret_scan_bwd(proj, cos, sin, lgam, do, states, dproj, name, side=None):
    t = proj.shape[0]
    c = RET_C
    nc = t // c
    half = RET_DK // 2

    def body(blk_ref, cos_ref, sin_ref, lg_ref, do_ref, st_ref, dp_in, dp_ref, dstate):
        n = nc - 1 - pl.program_id(0)

        @pl.when(pl.program_id(0) == 0)
        def _():
            dstate[...] = jnp.zeros_like(dstate)

        cs, sn = cos_ref[...], sin_ref[...]
        rows = n * c + lax.broadcasted_iota(jnp.int32, (c, 1), 0)
        keep = rows >= PAD

        def unrot(d):
            d1, d2 = d[:, :half], d[:, half:]
            return jnp.concatenate([d1 * cs + d2 * sn, d2 * cs - d1 * sn], axis=1)

        for h in range(RET_H):
            qr, kr, v, dmat, dq, dk, dchunk = _ret_chunk(blk_ref, cos_ref, sin_ref, lg_ref[h, :, 0:1], h)
            qb, kb = qr.astype(BF16), kr.astype(BF16)
            dob = do_ref[:, h * RET_DV:(h + 1) * RET_DV]
            sp = st_ref[h]
            ds = dstate[h]
            dsb = ds.astype(BF16)
            p = (_dot_nt(qb, kb) * dmat).astype(BF16)
            dvv = _dot_tn(p, dob) + _dot((kr * dk).astype(BF16), dsb)
            dp = (_dot_nt(dob, v) * dmat).astype(BF16)
            dqr = _dot(dp, kb) + _dot_nt(dob, sp) * dq
            dkr = (_dot_tn(dp, qb) + _dot_nt(v, dsb) * dk) * (RET_DK ** -0.5)
            dstate[h] = ds * dchunk + _dot_tn((qr * dq).astype(BF16), dob)
            out = jnp.concatenate([unrot(dqr), unrot(dkr), dvv], axis=1)
            dp_ref[:, h * RET_HW:(h + 1) * RET_HW] = jnp.where(keep, out, 0.0).astype(BF16)

    return _call(
        body, name=name, grid=(nc,), side=side, aliases={6: 0},
        in_specs=[pl.BlockSpec((c, RET_H * RET_HW), lambda n: (nc - 1 - n, 0)), pl.BlockSpec((c, 128), lambda n: (nc - 1 - n, 0)),
                  pl.BlockSpec((c, 128), lambda n: (nc - 1 - n, 0)), pl.BlockSpec((RET_H, 1, 128), lambda n: (0, 0, 0)),
                  pl.BlockSpec((c, RET_H * RET_DV), lambda n: (nc - 1 - n, 0)),
                  pl.BlockSpec((RET_H, None, RET_DK, RET_DV), lambda n: (0, nc - 1 - n, 0, 0)), ANY],
        out_specs=[pl.BlockSpec((c, RET_H * RET_HW), lambda n: (nc - 1 - n, 0))],
        out_shape=[S((t, dproj.shape[1]), BF16)],
        scratch_shapes=[pltpu.VMEM((RET_H, RET_DK, RET_DV), F32)],
        operands=[proj, cos, sin, lgam, do, states, dproj])


def _split3(x):
    hi = x.astype(BF16)
    r1 = x - hi.astype(F32)
    mid = r1.astype(BF16)
    lo = (r1 - mid.astype(F32)).astype(BF16)
    return hi, mid, lo


def _gla_chunk(blk_ref, z_ref, wg_ref, bg_ref, n, h):
    c = CHUNK
    o = h * GLA_HW
    q = blk_ref[:, o:o + GLA_DK].astype(F32) * (GLA_DK ** -0.5)
    k = blk_ref[:, o + GLA_DK:o + 2 * GLA_DK].astype(F32)
    v = blk_ref[:, o + 2 * GLA_DK:o + GLA_HW]
    hs = slice(h * GLA_DK, (h + 1) * GLA_DK)
    u = _dot(z_ref[...], wg_ref[:, hs]) + bg_ref[:, hs]
    la = (jnp.minimum(u, 0.0) - jnp.log(1.0 + jnp.exp(-jnp.abs(u)))) * (1.0 / GLA_TAU)
    rows = n * c + lax.broadcasted_iota(jnp.int32, (c, 1), 0)
    keep = rows >= PAD
    la = jnp.where(keep, la, 0.0)
    ii = lax.broadcasted_iota(jnp.int32, (c, c), 0)
    jj = lax.broadcasted_iota(jnp.int32, (c, c), 1)
    tril = (ii >= jj).astype(BF16)
    hi, mid, lo = _split3(la)
    b = _dot(tril, hi) + _dot(tril, mid) + _dot(tril, lo)
    return q, k, v, u, b, keep


def _gla_intra(qs, ks, bs, a_ref):
    c = CHUNK
    nh = len(qs)
    col = lax.broadcasted_iota(jnp.int32, (1, c), 1)
    rowi = lax.broadcasted_iota(jnp.int32, (SUB, 1), 0)
    for blk in range(c // SUB):
        r = slice(SUB * blk, SUB * (blk + 1))
        arows = []
        for h in range(nh):
            q, k, b = qs[h], ks[h], bs[h]
            if blk > 0:
                bprev = b[SUB * blk - 1:SUB * blk]
                qe = q[r] * jnp.exp(b[r] - bprev)
                kt = k * jnp.exp(jnp.minimum(bprev - b, 0.0))
                arows.append(jnp.where(col < SUB * blk, _dot_nt(qe.astype(BF16), kt.astype(BF16)), 0.0))
            else:
                arows.append(jnp.zeros((SUB, c), F32))
        for j in range(SUB):
            for h in range(nh):
                b_i = bs[h][r]
                e = jnp.exp(b_i - b_i[j:j + 1])
                a = jnp.sum(qs[h][r] * ks[h][r][j:j + 1] * e, axis=1, keepdims=True)
                arows[h] = jnp.where(col == SUB * blk + j, a, arows[h])
        for h in range(nh):
            a_ref[h, r, :] = jnp.where(col - SUB * blk <= rowi, arows[h], 0.0)


def _gla_scan_fwd(proj, wgp, bg, name):
    t = proj.shape[0]
    c = CHUNK
    nc = t // c
    heads = range(GLA_H)

    def body(blk_ref, z_ref, wg_ref, bg_ref, o_ref, st_ref, state, a_ref):
        n = pl.program_id(0)

        @pl.when(n == 0)
        def _():
            state[...] = jnp.zeros_like(state)

        qs, ks, vs, us, bs, keeps = zip(*[_gla_chunk(blk_ref, z_ref, wg_ref, bg_ref, n, h) for h in heads])
        _gla_intra(qs, ks, bs, a_ref)
        for h in heads:
            q, k, v, b = qs[h], ks[h], vs[h], bs[h]
            sp = state[h]
            st_ref[h] = sp.astype(BF16)
            o = _dot(a_ref[h].astype(BF16), v) + _dot_nt((q * jnp.exp(b)).astype(BF16), sp.astype(BF16))
            o_ref[:, h * GLA_DV:(h + 1) * GLA_DV] = o.astype(BF16)
            bc = b[c - 1:c]
            state[h] = sp * jnp.exp(bc) + _dot_tn(v, (k * jnp.exp(bc - b)).astype(BF16))

    return pl.pallas_call(
        body, name=name, grid=(nc,),
        in_specs=[pl.BlockSpec((c, GLA_H * GLA_HW), lambda n: (n, 0)), pl.BlockSpec((c, 128), lambda n: (n, GLA_ZBLK)),
                  pl.BlockSpec((128, GLA_H * GLA_DK), lambda n: (0, 0)), pl.BlockSpec((1, GLA_H * GLA_DK), lambda n: (0, 0))],
        out_specs=[pl.BlockSpec((c, GLA_H * GLA_DV), lambda n: (n, 0)),
                   pl.BlockSpec((GLA_H, None, GLA_DV, GLA_DK), lambda n: (0, n, 0, 0))],
        out_shape=[S((t, GLA_H * GLA_DV), BF16), S((GLA_H, nc, GLA_DV, GLA_DK), BF16)],
        scratch_shapes=[pltpu.VMEM((GLA_H, GLA_DV, GLA_DK), F32), pltpu.VMEM((GLA_H, c, c), F32)],
        compiler_params=_cp(dimension_semantics=("arbitrary",)),
    )(proj, proj, wgp, bg)


def _gla_scan_bwd(proj, wgp, bg, do, states, dproj, name):
    t = proj.shape[0]
    c = CHUNK
    nc = t // c
    heads = range(GLA_H)

    def body(blk_ref, z_ref, wg_ref, bg_ref, do_ref, st_ref, dp_in, dp_ref, du_ref, dstate, a_ref, dq_ref, dkd_ref):
        n = nc - 1 - pl.program_id(0)

        @pl.when(pl.program_id(0) == 0)
        def _():
            dstate[...] = jnp.zeros_like(dstate)

        qs, ks, vs, us, bs, keeps = zip(*[_gla_chunk(blk_ref, z_ref, wg_ref, bg_ref, n, h) for h in heads])
        _gla_intra(qs, ks, bs, a_ref)
        ii = lax.broadcasted_iota(jnp.int32, (c, c), 0)
        jj = lax.broadcasted_iota(jnp.int32, (c, c), 1)
        col = lax.broadcasted_iota(jnp.int32, (1, c), 1)
        rowi = lax.broadcasted_iota(jnp.int32, (SUB, 1), 0)
        rowc = lax.broadcasted_iota(jnp.int32, (c, 1), 0)
        das, dvs, dq_inters, dk_states, extras, dks = [], [], [], [], [], []
        for h in heads:
            q, k, v, b = qs[h], ks[h], vs[h], bs[h]
            ab = a_ref[h].astype(BF16)
            dob = do_ref[:, h * GLA_DV:(h + 1) * GLA_DV]
            sp = st_ref[h]
            ds = dstate[h]
            dsb = ds.astype(BF16)
            bc = b[c - 1:c]
            eb = jnp.exp(b)
            ebc = jnp.exp(bc - b)
            ec = jnp.exp(bc)
            qb = (q * eb).astype(BF16)
            kb = (k * ebc).astype(BF16)
            dvs.append(_dot_tn(ab, dob) + _dot_nt(kb, dsb))
            das.append(jnp.where(ii >= jj, _dot_nt(dob, v), 0.0))
            dq_inters.append(_dot(dob, sp) * eb)
            dk_state = _dot(v, dsb) * ebc
            dk_states.append(dk_state)
            extras.append(jnp.sum(k * dk_state, axis=0, keepdims=True)
                          + ec * jnp.sum(sp.astype(F32) * ds, axis=0, keepdims=True))
            dstate[h] = ds * ec + _dot_tn(dob, qb)
            dks.append(jnp.zeros((c, GLA_DK), F32))

        for blk in range(c // SUB):
            r = slice(SUB * blk, SUB * (blk + 1))
            dq_is, dkds = [], []
            for h in heads:
                q, k, b = qs[h], ks[h], bs[h]
                if blk > 0:
                    bprev = b[SUB * blk - 1:SUB * blk]
                    e_i = jnp.exp(b[r] - bprev)
                    ek = jnp.exp(jnp.minimum(bprev - b, 0.0))
                    daoff = jnp.where(col < SUB * blk, das[h][r], 0.0).astype(BF16)
                    dq_is.append(_dot(daoff, (k * ek).astype(BF16)) * e_i)
                    dks[h] = dks[h] + _dot_tn(daoff, (q[r] * e_i).astype(BF16)) * ek
                else:
                    dq_is.append(jnp.zeros((SUB, GLA_DK), F32))
                dkds.append(jnp.zeros((SUB, GLA_DK), F32))
            for j in range(SUB):
                for h in heads:
                    b_i = bs[h][r]
                    e = jnp.where(rowi >= j, jnp.exp(b_i - b_i[j:j + 1]), 0.0)
                    dacol = jnp.sum(jnp.where(col == SUB * blk + j, das[h][r], 0.0), axis=1, keepdims=True)
                    tt = dacol * e
                    dq_is[h] = dq_is[h] + tt * ks[h][r][j:j + 1]
                    dkds[h] = jnp.where(rowi == j, jnp.sum(tt * qs[h][r], axis=0, keepdims=True), dkds[h])
            for h in heads:
                dq_ref[h, r, :] = dq_is[h]
                dkd_ref[h, r, :] = dkds[h]

        for h in heads:
            q, k, b, u, keep = qs[h], ks[h], bs[h], us[h], keeps[h]
            dq = dq_ref[h] + dq_inters[h]
            dk = dks[h] + dkd_ref[h] + dk_states[h]
            db = q * dq - k * dk + jnp.where(rowc == c - 1, extras[h], 0.0)
            triu = (ii <= jj).astype(BF16)
            hi, mid, lo = _split3(db)
            dla = _dot(triu, hi) + _dot(triu, mid) + _dot(triu, lo)
            du = jnp.where(keep, dla * (1.0 / GLA_TAU) / (1.0 + jnp.exp(u)), 0.0)
            du_ref[:, h * GLA_DK:(h + 1) * GLA_DK] = du.astype(BF16)
            out = jnp.concatenate([dq * (GLA_DK ** -0.5), dk, dvs[h]], axis=1)
            dp_ref[:, h * GLA_HW:(h + 1) * GLA_HW] = jnp.where(keep, out, 0.0).astype(BF16)

    nproj = dproj.shape[1]
    return pl.pallas_call(
        body, name=name, grid=(nc,),
        in_specs=[pl.BlockSpec((c, GLA_H * GLA_HW), lambda n: (nc - 1 - n, 0)), pl.BlockSpec((c, 128), lambda n: (nc - 1 - n, GLA_ZBLK)),
                  pl.BlockSpec((128, GLA_H * GLA_DK), lambda n: (0, 0)), pl.BlockSpec((1, GLA_H * GLA_DK), lambda n: (0, 0)),
                  pl.BlockSpec((c, GLA_H * GLA_DV), lambda n: (nc - 1 - n, 0)),
                  pl.BlockSpec((GLA_H, None, GLA_DV, GLA_DK), lambda n: (0, nc - 1 - n, 0, 0)), ANY],
        out_specs=[pl.BlockSpec((c, GLA_H * GLA_HW), lambda n: (nc - 1 - n, 0)),
                   pl.BlockSpec((c, GLA_H * GLA_DK), lambda n: (nc - 1 - n, 0))],
        out_shape=[S((t, nproj), BF16), S((t, GLA_H * GLA_DK), BF16)],
        input_output_aliases={6: 0},
        scratch_shapes=[pltpu.VMEM((GLA_H, GLA_DV, GLA_DK), F32), pltpu.VMEM((GLA_H, c, c), F32),
                        pltpu.VMEM((GLA_H, c, GLA_DK), F32), pltpu.VMEM((GLA_H, c, GLA_DK), F32)],
        compiler_params=_cp(dimension_semantics=("arbitrary",)),
    )(proj, proj, wgp, bg, do, states, dproj)


def _gla_gate_bwd(du, proj, wgp, dproj, name):
    t = du.shape[0]
    tm = _row_tile(t, 704)
    w = GLA_H * GLA_DK

    def body(du_ref, z_ref, wg_ref, dp_in, dp_ref, dwg_ref, dbg_ref):
        @pl.when(pl.program_id(0) == 0)
        def _():
            dwg_ref[...] = jnp.zeros_like(dwg_ref)
            dbg_ref[...] = jnp.zeros_like(dbg_ref)

        d = du_ref[...]
        dp_ref[...] = _dot_nt(d, wg_ref[...]).astype(BF16)
        dwg_ref[...] += _dot_tn(z_ref[...], d)
        dbg_ref[0:1, :] += jnp.sum(d.astype(F32), axis=0, keepdims=True)

    return pl.pallas_call(
        body, name=name, grid=(t // tm,),
        in_specs=[pl.BlockSpec((tm, w), lambda i: (i, 0)), pl.BlockSpec((tm, 128), lambda i: (i, GLA_ZBLK)),
                  pl.BlockSpec((128, w), lambda i: (0, 0)), ANY],
        out_specs=[pl.BlockSpec((tm, 128), lambda i: (i, GLA_ZBLK)), pl.BlockSpec((128, w), lambda i: (0, 0)),
                   pl.BlockSpec((8, w), lambda i: (0, 0))],
        out_shape=[S(dproj.shape, BF16), S((128, w), F32), S((8, w), F32)],
        input_output_aliases={3: 0},
        compiler_params=_cp(dimension_semantics=("arbitrary",)),
    )(du, proj, wgp, dproj)


def _final_loss(hx, gain, target, name):
    t = hx.shape[0]
    tm = _row_tile(t, 512)

    def body(h_ref, g_ref, t_ref, dh_ref, dgain_ref, loss_ref):
        @pl.when(pl.program_id(0) == 0)
        def _():
            dgain_ref[...] = jnp.zeros_like(dgain_ref)
            loss_ref[...] = jnp.zeros_like(loss_ref)

        x = h_ref[...]
        gain = g_ref[...]
        r = lax.rsqrt(jnp.mean(x * x, axis=-1, keepdims=True) + EPS)
        xh = x * r
        e = xh * gain - t_ref[...]
        loss_ref[...] += 0.5 * jnp.sum(jnp.mean(e * e, axis=-1, keepdims=True), axis=0, keepdims=True)
        dy = e * (1.0 / D)
        dgain_ref[0:1, :] += jnp.sum(dy * xh, axis=0, keepdims=True)
        dxh = dy * gain
        dh_ref[...] = r * (dxh - xh * jnp.mean(dxh * xh, axis=-1, keepdims=True))

    row = pl.BlockSpec((tm, D), lambda i: (i, 0))
    return pl.pallas_call(
        body, name=name, grid=(t // tm,),
        in_specs=[row, pl.BlockSpec((1, D), lambda i: (0, 0)), row],
        out_specs=[row, pl.BlockSpec((8, D), lambda i: (0, 0)), pl.BlockSpec((8, 128), lambda i: (0, 0))],
        out_shape=[S((t, D), F32), S((8, D), F32), S((8, 128), F32)],
        compiler_params=_cp(dimension_semantics=("arbitrary",)),
    )(hx, gain, target)


def _adam_math(w, g, m, v):
    m2 = ADAM_B1 * m + (1.0 - ADAM_B1) * g
    v2 = ADAM_B2 * v + (1.0 - ADAM_B2) * (g * g)
    m_hat = m2 / (1.0 - ADAM_B1 ** ADAM_STEP)
    v_hat = v2 / (1.0 - ADAM_B2 ** ADAM_STEP)
    delta = -ADAM_LR * (m_hat / (jnp.sqrt(v_hat) + ADAM_EPS) + ADAM_WD * w)
    return delta, m2, v2


def _adamw_reduce(recvs, w, m, v, name):
    nl, r, wd = w.shape
    tr = _row_tile(r, 256) if r % 16 == 0 else r
    nr = r // tr

    def body(*refs):
        rv_refs = refs[:nl]
        w_ref, m_ref, v_ref, g_ref, d_ref, m2_ref, v2_ref = refs[nl:]
        layer = pl.program_id(0)

        def total(rv_ref):
            g = rv_ref[0].astype(F32)
            for s in range(1, N_DEV):
                g = g + rv_ref[s].astype(F32)
            return g

        g = total(rv_refs[0])
        for k in range(1, nl):
            g = jnp.where(layer == k, total(rv_refs[k]), g)
        g_ref[...] = g
        d_ref[...], m2_ref[...], v2_ref[...] = _adam_math(w_ref[...], g, m_ref[...], v_ref[...])

    def rv_spec(k):
        return pl.BlockSpec((N_DEV, tr, wd), lambda l, i: (0, jnp.where(l == k, i, jnp.where(l < k, 0, nr - 1)), 0))

    row = pl.BlockSpec((None, tr, wd), lambda l, i: (l, i, 0))
    return pl.pallas_call(
        body, name=name, grid=(nl, nr),
        in_specs=[rv_spec(k) for k in range(nl)] + [row, row, row],
        out_specs=[row] * 4, out_shape=[S((nl, r, wd), F32)] * 4,
        compiler_params=_cp(dimension_semantics=("arbitrary", "arbitrary")),
    )(*recvs, w, m, v)


def _small_reduce(parts, name):
    _, r, wd = parts.shape

    def body(p_ref, o_ref):
        g = p_ref[0]
        for s in range(1, N_DEV):
            g = g + p_ref[s]
        o_ref[...] = g

    return pl.pallas_call(body, name=name, out_shape=S((r, wd), F32), compiler_params=_cp())(parts)


def _adamw_small(w, g, m, v, name):
    def body(w_ref, g_ref, m_ref, v_ref, d_ref, m2_ref, v2_ref):
        d_ref[...], m2_ref[...], v2_ref[...] = _adam_math(w_ref[...], g_ref[...], m_ref[...], v_ref[...])

    return pl.pallas_call(body, name=name, out_shape=[S(w.shape, F32)] * 3, compiler_params=_cp())(w, g, m, v)


def _to_head_major(w, nh, dk, dv):
    kk = w.shape[0]
    q = w[:, :nh * dk].reshape(kk, nh, dk)
    k = w[:, nh * dk:2 * nh * dk].reshape(kk, nh, dk)
    v = w[:, 2 * nh * dk:2 * nh * dk + nh * dv].reshape(kk, nh, dv)
    heads = jnp.concatenate([q, k, v], axis=-1).reshape(kk, nh * (2 * dk + dv))
    return jnp.concatenate([heads, w[:, 2 * nh * dk + nh * dv:]], axis=1)


def _from_head_major(p, nh, dk, dv):
    kk = p.shape[0]
    hw = 2 * dk + dv
    heads = p[:, :nh * hw].reshape(kk, nh, hw)
    q = heads[:, :, :dk].reshape(kk, nh * dk)
    k = heads[:, :, dk:2 * dk].reshape(kk, nh * dk)
    v = heads[:, :, 2 * dk:].reshape(kk, nh * dv)
    return jnp.concatenate([q, k, v, p[:, nh * hw:]], axis=1)


def _unshard_cols(g):
    return jnp.transpose(g, (1, 0, 2)).reshape(g.shape[1], N_DEV * g.shape[2])


def _shard_cols(w):
    k, n8 = w.shape
    return jnp.transpose(w.reshape(k, N_DEV, n8 // N_DEV), (1, 0, 2))


def _my_cols(full, width):
    me = 4 * lax.axis_index("x") + 2 * lax.axis_index("y") + lax.axis_index("c")
    return lax.dynamic_slice_in_dim(full, me * width, width, axis=1)


def kernel(x, meta_tokens, norm_ffn1, ffn1_w_in, ffn1_w_out, norm_mix, norm_ffn2, ffn2_w_in, ffn2_w_out, ret_w_in, ret_head_norm, ret_w_out, gla_w_in, gla_w_gate, gla_b_gate, gla_head_norm, gla_w_out, final_norm, loss_target, m_meta_tokens, m_norm_ffn1, m_ffn1_w_in, m_ffn1_w_out, m_norm_mix, m_norm_ffn2, m_ffn2_w_in, m_ffn2_w_out, m_ret_w_in, m_ret_head_norm, m_ret_w_out, m_gla_w_in, m_gla_w_gate, m_gla_b_gate, m_gla_head_norm, m_gla_w_out, m_final_norm, v_meta_tokens, v_norm_ffn1, v_ffn1_w_in, v_ffn1_w_out, v_norm_mix, v_norm_ffn2, v_ffn2_w_in, v_ffn2_w_out, v_ret_w_in, v_ret_head_norm, v_ret_w_out, v_gla_w_in, v_gla_w_gate, v_gla_b_gate, v_gla_head_norm, v_gla_w_out, v_final_norm):
    seq = x.shape[1]
    t = seq + CHUNK
    xs = x[0]
    target = loss_target[0]

    def ffn_w(f):
        w_in, w_out = (ffn1_w_in, ffn1_w_out) if f < 2 else (ffn2_w_in, ffn2_w_out)
        return [w_in[f % 2].astype(BF16), w_out[f % 2].astype(BF16)]

    small = jnp.concatenate([meta_tokens.reshape(-1), ret_head_norm.reshape(-1), gla_w_gate.reshape(-1),
                             gla_b_gate.reshape(-1), gla_head_norm.reshape(-1)])
    n_small = small.shape[0]
    small = jnp.pad(small, (0, 32 * 128 - n_small)).reshape(32, 128)
    sg, win0, wout0 = _run_side(_Gather([small] + ffn_w(0)), "ag_first")
    sg = sg.reshape(N_DEV, 32 * 128)

    def small_cols(off, rows, width):
        return jnp.transpose(sg[:, off:off + rows * width].reshape(N_DEV, rows, width), (1, 0, 2)).reshape(rows, N_DEV * width)

    off = 0
    meta_full = small_cols(off, N_META, D // N_DEV); off += N_META * (D // N_DEV)
    ret_hn = small_cols(off, RET_H, RET_DV // N_DEV).reshape(1, RET_H * RET_DV); off += RET_H * RET_DV // N_DEV
    wgate = small_cols(off, GLA_RANK, GLA_H * GLA_DK // N_DEV); off += GLA_RANK * GLA_H * GLA_DK // N_DEV
    bgate = small_cols(off, 1, GLA_H * GLA_DK // N_DEV); off += GLA_H * GLA_DK // N_DEV
    gla_hn = small_cols(off, GLA_H, GLA_DV // N_DEV).reshape(1, GLA_H * GLA_DV)
    wgp = jnp.pad(wgate, ((0, 128 - GLA_RANK), (0, 0))).astype(BF16)

    cos, sin = _rope_tables(t)
    lgam = _ret_consts()

    h0 = jnp.concatenate([jnp.zeros((PAD, D), F32), meta_full, xs], axis=0)
    g1 = [norm_ffn1[i:i + 1] for i in range(2)]
    gm = [norm_mix[i:i + 1] for i in range(2)]
    g2 = [norm_ffn2[i:i + 1] for i in range(2)]

    (h1, xn_a0, pg_a0, pu_a0), (ret_win_g, ret_wout_g) = _ffn_fwd(
        h0, g1[0], win0, wout0, "ffn1_l0_fwd", side=_Gather([ret_w_in[0].astype(BF16), ret_w_out[0].astype(BF16)]))
    ret_win = _to_head_major(_unshard_cols(ret_win_g), RET_H, RET_DK, RET_DV)
    ret_wout = ret_wout_g.reshape(RET_H * RET_DV, D)
    (rproj, rhn), (win2, wout2) = _norm_mm(h1, gm[0], ret_win, 1536, "ret_proj_fwd", side=_Gather(ffn_w(2)))
    ro, rstates = _ret_scan_fwd(rproj, cos, sin, lgam, "ret_scan_fwd")
    h2, rog = _post_fwd(ro, rproj, ret_hn, ret_wout, h1, RET_H, RET_DV, "ret_post_fwd")
    (h3, xn_b0, pg_b0, pu_b0), (win1, wout1) = _ffn_fwd(h2, g2[0], win2, wout2, "ffn2_l0_fwd", side=_Gather(ffn_w(1)))
    (h4, xn_a1, pg_a1, pu_a1), (gla_win_g, gla_wout_g) = _ffn_fwd(
        h3, g1[1], win1, wout1, "ffn1_l1_fwd", side=_Gather([gla_w_in[0].astype(BF16), gla_w_out[0].astype(BF16)]))
    gla_win = _to_head_major(_unshard_cols(gla_win_g), GLA_H, GLA_DK, GLA_DV)
    gla_win = jnp.pad(gla_win, ((0, 0), (0, GLA_N - gla_win.shape[1])))
    gla_wout = gla_wout_g.reshape(GLA_H * GLA_DV, D)
    (gproj, ghn), (win3, wout3) = _norm_mm(h4, gm[1], gla_win, 640, "gla_proj_fwd", side=_Gather(ffn_w(3)))
    go, gstates = _gla_scan_fwd(gproj, wgp, bgate, "gla_scan_fwd")
    h5, gog = _post_fwd(go, gproj, gla_hn, gla_wout, h4, GLA_H, GLA_DV, "gla_post_fwd")
    (h6, xn_b1, pg_b1, pu_b1), _ = _ffn_fwd(h5, g2[1], win3, wout3, "ffn2_l1_fwd")

    dhx, dfinal, loss_blk = _final_loss(h6[CHUNK:], final_norm.reshape(1, D), target, "final_loss")
    loss = lax.psum(loss_blk[0, 0], ("x", "y", "c"))
    dh = jnp.concatenate([jnp.zeros((CHUNK, D), F32), dhx], axis=0)

    def ffn_back(dh, h_in, xn, gain, pg, pu, win, wout, tag, side=None, dw_side=None):
        (dh_in, dob, dpg, dpu, act, dgain), got = _ffn_bwd(dh, h_in, gain, pg, pu, win, wout, tag + "_bwd", side=side)
        (dwin,), dw_got = _ffn_dw_in(xn, dpg, dpu, tag + "_dw_in", side=dw_side)
        dwout = _mm_tn(act, dob[None], D, tag + "_dw_out").reshape(N_DEV, FF_SHARD // 2, D)
        return dh_in, [dwin, dwout], dgain[0], got, dw_got

    dh, dw_b1, dg2_1, _, _ = ffn_back(dh, h5, xn_b1, g2[1], pg_b1, pu_b1, win3, wout3, "ffn2_l1")

    (gdo, gdproj, gdhb, dghn), _ = _post_bwd(dh, go, gproj, gla_hn, gla_wout, GLA_H, GLA_DV, GLA_N, "gla_post_bwd")
    d_gla_wout = _mm_tn(gog[None], gdhb[None], D, "gla_dw_out").reshape(N_DEV, GLA_H * GLA_DV // N_DEV, D)
    gdproj, gdu = _gla_scan_bwd(gproj, wgp, bgate, gdo, gstates, gdproj, "gla_scan_bwd")
    gdproj, dwg, dbg = _gla_gate_bwd(gdu, gproj, wgp, gdproj, "gla_gate_bwd")
    d_gla_win = _mm_tn(ghn[None], gdproj[None], 640, "gla_dw_in")[0]
    (dh, dgm_1), _ = _proj_bwd(gdproj, gla_win, dh, h4, gm[1], 640, "gla_proj_bwd")
    n_gla_in = 2 * GLA_H * GLA_DK + 2 * GLA_H * GLA_DV + GLA_RANK
    d_gla_win = _shard_cols(_from_head_major(d_gla_win[:, :n_gla_in], GLA_H, GLA_DK, GLA_DV))

    dh, dw_a1, dg1_1, rv_b1, rv_gla = ffn_back(dh, h3, xn_a1, g1[1], pg_a1, pu_a1, win1, wout1, "ffn1_l1",
                                               side=_Exchange(dw_b1), dw_side=_Exchange([d_gla_win, d_gla_wout]))
    dh, dw_b0, dg2_0, rv_a1, _ = ffn_back(dh, h2, xn_b0, g2[0], pg_b0, pu_b0, win2, wout2, "ffn2_l0", side=_Exchange(dw_a1))

    (rdo, rdproj, rdhb, drhn), rv_b0_out = _post_bwd(dh, ro, rproj, ret_hn, ret_wout, RET_H, RET_DV, 6 * D, "ret_post_bwd",
                                                     side=_Exchange(dw_b0[1:]))
    d_ret_wout = _mm_tn(rog[None], rdhb[None], D, "ret_dw_out").reshape(N_DEV, RET_H * RET_DV // N_DEV, D)
    (rdproj,), rv_b0_in = _ret_scan_bwd(rproj, cos, sin, lgam, rdo, rstates, rdproj, "ret_scan_bwd", side=_Exchange(dw_b0[:1]))
    rv_b0 = rv_b0_in + rv_b0_out
    d_ret_win = _mm_tn(rhn[None], rdproj[None], 1536, "ret_dw_in")[0]
    (dh, dgm_0), rv_ret_out = _proj_bwd(rdproj, ret_win, dh, h1, gm[0], 1536, "ret_proj_bwd", side=_Exchange([d_ret_wout]))
    d_ret_win = _shard_cols(_from_head_major(d_ret_win, RET_H, RET_DK, RET_DV))

    dh, dw_a0, dg1_0, rv_ret_in, _ = ffn_back(dh, h0, xn_a0, g1[0], pg_a0, pu_a0, win0, wout0, "ffn1_l0",
                                              side=_Exchange([d_ret_win]))
    rv_ret = rv_ret_in + rv_ret_out
    rv_a0 = _run_side(_Exchange(dw_a0), "xchg_last")
    grad_x = dh[CHUNK:][None]

    u_ffn1_in = _adamw_reduce([rv_a0[0], rv_a1[0]], ffn1_w_in, m_ffn1_w_in, v_ffn1_w_in, "adam_ffn1_w_in")
    u_ffn2_in = _adamw_reduce([rv_b0[0], rv_b1[0]], ffn2_w_in, m_ffn2_w_in, v_ffn2_w_in, "adam_ffn2_w_in")
    u_ffn1_out = _adamw_reduce([rv_a0[1], rv_a1[1]], ffn1_w_out, m_ffn1_w_out, v_ffn1_w_out, "adam_ffn1_w_out")
    u_ffn2_out = _adamw_reduce([rv_b0[1], rv_b1[1]], ffn2_w_out, m_ffn2_w_out, v_ffn2_w_out, "adam_ffn2_w_out")
    u_ret_in = _adamw_reduce([rv_ret[0]], ret_w_in, m_ret_w_in, v_ret_w_in, "adam_ret_w_in")
    u_ret_out = _adamw_reduce([rv_ret[1]], ret_w_out, m_ret_w_out, v_ret_w_out, "adam_ret_w_out")
    u_gla_in = _adamw_reduce([rv_gla[0]], gla_w_in, m_gla_w_in, v_gla_w_in, "adam_gla_w_in")
    u_gla_out = _adamw_reduce([rv_gla[1]], gla_w_out, m_gla_w_out, v_gla_w_out, "adam_gla_w_out")

    dmeta = dh[PAD:CHUNK]
    parts = jnp.concatenate([
        dg1_0, dg1_1, dgm_0[0], dgm_1[0], dg2_0, dg2_1, dfinal[0], dmeta.reshape(-1), drhn[0], dwg[:GLA_RANK].reshape(-1),
        dbg[0], dghn[0]])
    n_parts = parts.shape[0]
    rows = -(-n_parts // D)
    rows = -(-rows // 8) * 8
    parts = jnp.pad(parts, (0, rows * D - n_parts)).reshape(rows, D)
    tot = _small_reduce(_run_side(_Gather([parts]), "ag_small_grads")[0], "small_grad_sum").reshape(-1)

    off = 0
    def take(nel):
        nonlocal off
        out = tot[off:off + nel]
        off += nel
        return out

    gr_norm_ffn1 = take(2 * D).reshape(2, D)
    gr_norm_mix = take(2 * D).reshape(2, D)
    gr_norm_ffn2 = take(2 * D).reshape(2, D)
    gr_final = take(D)
    gr_meta = _my_cols(take(N_META * D).reshape(N_META, D), D // N_DEV)
    gr_ret_hn = _my_cols(take(RET_H * RET_DV).reshape(RET_H, RET_DV), RET_DV // N_DEV)[None]
    gr_wgate = _my_cols(take(GLA_RANK * GLA_H * GLA_DK).reshape(GLA_RANK, GLA_H * GLA_DK), GLA_H * GLA_DK // N_DEV)[None]
    gr_bgate = _my_cols(take(GLA_H * GLA_DK).reshape(1, GLA_H * GLA_DK), GLA_H * GLA_DK // N_DEV)
    gr_gla_hn = _my_cols(take(GLA_H * GLA_DV).reshape(GLA_H, GLA_DV), GLA_DV // N_DEV)[None]

    small_w = [meta_tokens, norm_ffn1, norm_mix, norm_ffn2, ret_head_norm, gla_w_gate, gla_b_gate, gla_head_norm, final_norm]
    small_g = [gr_meta, gr_norm_ffn1, gr_norm_mix, gr_norm_ffn2, gr_ret_hn, gr_wgate, gr_bgate, gr_gla_hn, gr_final]
    small_m = [m_meta_tokens, m_norm_ffn1, m_norm_mix, m_norm_ffn2, m_ret_head_norm, m_gla_w_gate, m_gla_b_gate, m_gla_head_norm, m_final_norm]
    small_v = [v_meta_tokens, v_norm_ffn1, v_norm_mix, v_norm_ffn2, v_ret_head_norm, v_gla_w_gate, v_gla_b_gate, v_gla_head_norm, v_final_norm]

    def pack(arrs):
        flat = jnp.concatenate([a.reshape(-1) for a in arrs])
        n = flat.shape[0]
        r = -(-n // 128)
        r = -(-r // 8) * 8
        return jnp.pad(flat, (0, r * 128 - n), constant_values=1.0).reshape(r, 128)

    sd, sm, sv = _adamw_small(pack(small_w), pack(small_g), pack(small_m), pack(small_v), "adam_small")

    def unpack(buf):
        flat = buf.reshape(-1)
        outs, o = [], 0
        for a in small_w:
            outs.append(flat[o:o + a.size].reshape(a.shape))
            o += a.size
        return outs

    us_d, us_m, us_v = unpack(sd), unpack(sm), unpack(sv)

    def ordered(k, smalls):
        return (smalls[0], smalls[1], u_ffn1_in[k], u_ffn1_out[k], smalls[2], smalls[3], u_ffn2_in[k], u_ffn2_out[k],
                u_ret_in[k], smalls[4], u_ret_out[k], u_gla_in[k], smalls[5], smalls[6], smalls[7], u_gla_out[k], smalls[8])

    return (loss, grad_x, *ordered(0, small_g), *ordered(1, us_d), *ordered(2, us_m), *ordered(3, us_v))
```

```python
import functools
import math

import numpy as np
import jax
import jax.numpy as jnp
from jax import lax
from jax.experimental import pallas as pl
from jax.experimental.pallas import tpu as pltpu

F32 = jnp.float32
BF16 = jnp.bfloat16
S = jax.ShapeDtypeStruct
ANY = pl.BlockSpec(memory_space=pl.ANY)
MESH = pl.DeviceIdType.MESH

D = 1024
N_META = 16
CHUNK = 64
PAD = CHUNK - N_META
EPS = 1e-6
N_DEV = 8
FF_SHARD = 704
N_FF_CHUNK = 4
RET_H, RET_DK, RET_DV = 4, 256, 512
RET_HW = 2 * RET_DK + RET_DV
RET_C = 192
GLA_H, GLA_DK, GLA_DV, GLA_RANK, GLA_TAU = 4, 128, 256, 16, 16.0
GLA_HW = 2 * GLA_DK + GLA_DV
GLA_N = 3200
GLA_ZBLK = 3072 // 128
SUB = 16
ROPE_BASE = 10000.0
ADAM_LR, ADAM_B1, ADAM_B2, ADAM_EPS, ADAM_WD, ADAM_STEP = 0.001, 0.9, 0.999, 1e-08, 0.01, 10
VMEM_LIMIT = 58 * 1024 * 1024


def _cp(**kw):
    return pltpu.CompilerParams(vmem_limit_bytes=VMEM_LIMIT, **kw)


def _row_tile(t, cap):
    best = 16
    for d in range(16, cap + 1, 16):
        if t % d == 0:
            best = d
    return best


def _sub_rows(tm, parts=2):
    units = tm // 16
    cuts = [16 * (units * p // parts) for p in range(parts + 1)]
    return [slice(a, b) for a, b in zip(cuts[:-1], cuts[1:]) if b > a]


def _dot(a, b):
    return jnp.dot(a, b, preferred_element_type=F32)


def _dot_nt(a, b):
    return lax.dot_general(a, b, (((1,), (1,)), ((), ())), preferred_element_type=F32)


def _dot_tn(a, b):
    return lax.dot_general(a, b, (((0,), (0,)), ((), ())), preferred_element_type=F32)


def _sigmoid(x):
    return pl.reciprocal(1.0 + jnp.exp(-x), approx=True)


def _rms_bwd(dxn, x, gain):
    r = lax.rsqrt(jnp.mean(x * x, axis=-1, keepdims=True) + EPS)
    xh = x * r
    dxh = dxn * gain
    dx = r * (dxh - xh * jnp.mean(dxh * xh, axis=-1, keepdims=True))
    return dx, jnp.sum(dxn * xh, axis=0, keepdims=True)


def _xyc():
    return lax.axis_index("x"), lax.axis_index("y"), lax.axis_index("c")


class _Gather:
    def __init__(self, xs):
        self.xs = list(xs)
        self.n = len(self.xs)

    def out_shape(self):
        return [S((N_DEV,) + a.shape, a.dtype) for a in self.xs]

    def scratch(self):
        return [pltpu.SemaphoreType.DMA((self.n, 7)), pltpu.SemaphoreType.DMA((self.n, 7)), pltpu.SemaphoreType.DMA((self.n,))]

    def phases(self, x_refs, out_refs, send_sems, recv_sems, local_sems):
        x, y, c = _xyc()
        me, sibling = (x, y, c), (x, y, 1 - c)
        chips = [(1 - x, y), (x, 1 - y), (1 - x, 1 - y)]

        def copy(t, k, block, to, src=None):
            px, py, pc = block
            dst = out_refs[t].at[4 * px + 2 * py + pc]
            return pltpu.make_async_remote_copy(
                src_ref=dst if src is None else src, dst_ref=dst,
                send_sem=send_sems.at[t, k], recv_sem=recv_sems.at[t, k], device_id=to, device_id_type=MESH)

        def own(t):
            return pltpu.make_async_copy(x_refs[t], out_refs[t].at[4 * x + 2 * y + c], local_sems.at[t])

        def first(t):
            return [copy(t, 0, me, sibling, src=x_refs[t])] + [
                copy(t, 1 + j, me, (*chip, c), src=x_refs[t]) for j, chip in enumerate(chips)]

        def passed(t):
            return [copy(t, 4 + j, (*chip, c), sibling) for j, chip in enumerate(chips)]

        def start():
            for t in range(self.n):
                own(t).start()
                for cp in first(t):
                    cp.start()

        def mid():
            for t in range(self.n):
                fw = passed(t)
                for j, chip in enumerate(chips):
                    copy(t, 1 + j, (*chip, c), me).wait_recv()
                    fw[j].start()

        def finish():
            for t in range(self.n):
                copy(t, 0, sibling, me).wait_recv()
                for j, chip in enumerate(chips):
                    copy(t, 4 + j, (*chip, 1 - c), me).wait_recv()
                for cp in first(t) + passed(t):
                    cp.wait_send()
                own(t).wait()

        return start, mid, finish


class _Exchange:
    def __init__(self, xs):
        self.xs = list(xs)
        self.n = len(self.xs)

    def out_shape(self):
        return [S(a.shape, a.dtype) for a in self.xs]

    def scratch(self):
        return [pltpu.SemaphoreType.DMA((self.n, 7)), pltpu.SemaphoreType.DMA((self.n, 7)), pltpu.SemaphoreType.DMA((self.n,))]

    def phases(self, g_refs, r_refs, send_sems, recv_sems, local_sems):
        x, y, c = _xyc()
        me = 4 * x + 2 * y + c

        def own(t):
            return pltpu.make_async_copy(g_refs[t].at[me], r_refs[t].at[me], local_sems.at[t])

        def send(t, m):
            px, py, pc = x ^ (m >> 2), y ^ ((m >> 1) & 1), c ^ (m & 1)
            return pltpu.make_async_remote_copy(
                src_ref=g_refs[t].at[4 * px + 2 * py + pc], dst_ref=r_refs[t].at[me],
                send_sem=send_sems.at[t, m - 1], recv_sem=recv_sems.at[t, m - 1],
                device_id=(px, py, pc), device_id_type=MESH)

        def arrival(t, m):
            peer = 4 * (x ^ (m >> 2)) + 2 * (y ^ ((m >> 1) & 1)) + (c ^ (m & 1))
            return pltpu.make_async_remote_copy(
                src_ref=g_refs[t].at[peer], dst_ref=r_refs[t].at[peer],
                send_sem=send_sems.at[t, m - 1], recv_sem=recv_sems.at[t, m - 1],
                device_id=(x, y, c), device_id_type=MESH)

        def start():
            for t in range(self.n):
                own(t).start()
            for m in range(1, N_DEV):
                for t in range(self.n):
                    send(t, m).start()

        def mid():
            pass

        def finish():
            for m in range(1, N_DEV):
                for t in range(self.n):
                    arrival(t, m).wait_recv()
            for m in range(1, N_DEV):
                for t in range(self.n):
                    send(t, m).wait_send()
            for t in range(self.n):
                own(t).wait()

        return start, mid, finish


def _run_side(side, name):
    n = side.n

    def body(*refs):
        start, mid, finish = side.phases(refs[:n], refs[n:2 * n], *refs[2 * n:])
        start()
        mid()
        finish()

    return list(pl.pallas_call(
        body, name=name, out_shape=side.out_shape(), in_specs=[ANY] * n, out_specs=[ANY] * n,
        scratch_shapes=side.scratch())(*side.xs))


def _grid_steps(grid):
    def ids():
        return [pl.program_id(a) for a in range(len(grid))]

    def first():
        return functools.reduce(jnp.logical_and, [i == 0 for i in ids()])

    def middle():
        i = ids()
        return functools.reduce(jnp.logical_and, [i[0] == (3 * grid[0]) // 4] + [j == 0 for j in i[1:]])

    def last():
        return functools.reduce(jnp.logical_and, [i == g - 1 for i, g in zip(ids(), grid)])

    return first, middle, last


def _call(body, *, name, grid, in_specs, out_specs, out_shape, scratch_shapes, operands, side=None, aliases=None):
    n_in, n_out, n_scr = len(in_specs), len(out_shape), len(scratch_shapes)
    full = body
    if side is not None:
        ns = side.n
        first, middle, last = _grid_steps(grid)

        def full(*refs):
            a = n_in
            ins, sins = refs[:a], refs[a:a + ns]
            a += ns
            outs, souts = refs[a:a + n_out], refs[a + n_out:a + n_out + ns]
            a += n_out + ns
            scr, sems = refs[a:a + n_scr], refs[a + n_scr:]
            start, mid, finish = side.phases(sins, souts, *sems)
            pl.when(first())(start)
            body(*ins, *outs, *scr)
            pl.when(middle())(mid)
            pl.when(last())(finish)

        in_specs = list(in_specs) + [ANY] * ns
        out_specs = list(out_specs) + [ANY] * ns
        out_shape = list(out_shape) + side.out_shape()
        scratch_shapes = list(scratch_shapes) + side.scratch()
        operands = list(operands) + side.xs
    outs = pl.pallas_call(
        full, name=name, grid=grid, in_specs=list(in_specs), out_specs=list(out_specs), out_shape=list(out_shape),
        scratch_shapes=list(scratch_shapes), input_output_aliases=aliases or {},
        compiler_params=_cp(dimension_semantics=("arbitrary",) * len(grid)),
    )(*operands)
    return list(outs[:n_out]), list(outs[n_out:])


def _ffn_fwd(h, gain, win, wout, name, side=None):
    t = h.shape[0]
    tm = _row_tile(t, 704)
    nt = t // tm

    def body(h_ref, g_ref, wg_ref, wu_ref, wo_ref, hn_ref, xn_ref, pg_ref, pu_ref, acc):
        c = pl.program_id(1)

        @pl.when(c == 0)
        def _():
            x = h_ref[...]
            r = lax.rsqrt(jnp.mean(x * x, axis=-1, keepdims=True) + EPS)
            xn_ref[...] = (x * r * g_ref[...]).astype(BF16)
            acc[...] = jnp.zeros_like(acc)

        wo = wo_ref[...].reshape(FF_SHARD, D)
        subs = _sub_rows(tm)
        gus = [(_dot(xn_ref[r, :], wg_ref[...]), _dot(xn_ref[r, :], wu_ref[...])) for r in subs]
        for r, (g, u) in zip(subs, gus):
            pg_ref[r, :] = g.astype(BF16)
            pu_ref[r, :] = u.astype(BF16)
            act = (g * _sigmoid(g) * u).astype(BF16)
            acc[r, :] += _dot(act, wo)

        @pl.when(c == N_FF_CHUNK - 1)
        def _():
            hn_ref[...] = h_ref[...] + 0.5 * acc[...]

    return _call(
        body, name=name, grid=(nt, N_FF_CHUNK), side=side,
        in_specs=[
            pl.BlockSpec((tm, D), lambda i, c: (i, 0)),
            pl.BlockSpec((1, D), lambda i, c: (0, 0)),
            pl.BlockSpec((None, D, FF_SHARD), lambda i, c: (c, 0, 0)),
            pl.BlockSpec((None, D, FF_SHARD), lambda i, c: (c + N_FF_CHUNK, 0, 0)),
            pl.BlockSpec((2, FF_SHARD // 2, D), lambda i, c: (c, 0, 0)),
        ],
        out_specs=[
            pl.BlockSpec((tm, D), lambda i, c: (i, 0)),
            pl.BlockSpec((tm, D), lambda i, c: (i, 0)),
            pl.BlockSpec((None, tm, FF_SHARD), lambda i, c: (c, i, 0)),
            pl.BlockSpec((None, tm, FF_SHARD), lambda i, c: (c, i, 0)),
        ],
        out_shape=[S((t, D), F32), S((t, D), BF16), S((N_FF_CHUNK, t, FF_SHARD), BF16), S((N_FF_CHUNK, t, FF_SHARD), BF16)],
        scratch_shapes=[pltpu.VMEM((tm, D), F32)],
        operands=[h, gain, win, win, wout])


def _ffn_bwd(dh, h, gain, pg, pu, win, wout, name, side=None):
    t = h.shape[0]
    tm = _row_tile(t, 704)
    nt = t // tm

    def body(dh_ref, h_ref, g_ref, pg_ref, pu_ref, wg_ref, wu_ref, wo_ref,
             dhi_ref, dob_ref, dpg_ref, dpu_ref, act_ref, dgain_ref, acc):
        i, c = pl.program_id(0), pl.program_id(1)

        @pl.when(c == 0)
        def _():
            dob_ref[...] = (0.5 * dh_ref[...]).astype(BF16)
            acc[...] = jnp.zeros_like(acc)

        @pl.when((i == 0) & (c == 0))
        def _():
            dgain_ref[...] = jnp.zeros_like(dgain_ref)

        wo = wo_ref[...].reshape(FF_SHARD, D)
        subs = _sub_rows(tm)
        dacts = [_dot_nt(dob_ref[r, :], wo) for r in subs]
        for r, dact in zip(subs, dacts):
            g = pg_ref[r, :].astype(F32)
            u = pu_ref[r, :].astype(F32)
            s = _sigmoid(g)
            sl = g * s
            act_ref[r, :] = (sl * u).astype(BF16)
            dg = (dact * u * (s * (1.0 + g * (1.0 - s)))).astype(BF16)
            du = (dact * sl).astype(BF16)
            dpg_ref[r, :] = dg
            dpu_ref[r, :] = du
            acc[r, :] += _dot_nt(dg, wg_ref[...]) + _dot_nt(du, wu_ref[...])

        @pl.when(c == N_FF_CHUNK - 1)
        def _():
            dx, dgn = _rms_bwd(acc[...], h_ref[...], g_ref[...])
            dhi_ref[...] = dh_ref[...] + dx
            dgain_ref[0:1, :] += dgn

    blk = pl.BlockSpec((None, tm, FF_SHARD), lambda i, c: (c, i, 0))
    row = pl.BlockSpec((tm, D), lambda i, c: (i, 0))
    return _call(
        body, name=name, grid=(nt, N_FF_CHUNK), side=side,
        in_specs=[
            row, row, pl.BlockSpec((1, D), lambda i, c: (0, 0)), blk, blk,
            pl.BlockSpec((None, D, FF_SHARD), lambda i, c: (c, 0, 0)),
            pl.BlockSpec((None, D, FF_SHARD), lambda i, c: (c + N_FF_CHUNK, 0, 0)),
            pl.BlockSpec((2, FF_SHARD // 2, D), lambda i, c: (c, 0, 0)),
        ],
        out_specs=[row, row, blk, blk, blk, pl.BlockSpec((8, D), lambda i, c: (0, 0))],
        out_shape=[S((t, D), F32), S((t, D), BF16)] + [S((N_FF_CHUNK, t, FF_SHARD), BF16)] * 3 + [S((8, D), F32)],
        scratch_shapes=[pltpu.VMEM((tm, D), F32)],
        operands=[dh, h, gain, pg, pu, win, win, wout])


def _ffn_dw_in(xn, dpg, dpu, name, side=None):
    t = xn.shape[0]
    tk = _row_tile(t, 1376)
    nk = t // tk

    def body(a_ref, bg_ref, bu_ref, o_ref, acc):
        c, k = pl.program_id(0), pl.program_id(1)

        @pl.when(k == 0)
        def _():
            acc[...] = jnp.zeros_like(acc)

        @pl.when(c < N_FF_CHUNK)
        def _():
            acc[...] += _dot_tn(bg_ref[...], a_ref[...])

        @pl.when(c >= N_FF_CHUNK)
        def _():
            acc[...] += _dot_tn(bu_ref[...], a_ref[...])

        @pl.when(k == nk - 1)
        def _():
            o_ref[...] = acc[...].astype(BF16)

    return _call(
        body, name=name, grid=(2 * N_FF_CHUNK, nk), side=side,
        in_specs=[
            pl.BlockSpec((tk, D), lambda c, k: (k, 0)),
            pl.BlockSpec((None, tk, FF_SHARD), lambda c, k: (jnp.minimum(c, N_FF_CHUNK - 1), k, 0)),
            pl.BlockSpec((None, tk, FF_SHARD), lambda c, k: (jnp.maximum(c - N_FF_CHUNK, 0), k, 0)),
        ],
        out_specs=[pl.BlockSpec((None, FF_SHARD, D), lambda c, k: (c, 0, 0))],
        out_shape=[S((2 * N_FF_CHUNK, FF_SHARD, D), BF16)],
        scratch_shapes=[pltpu.VMEM((FF_SHARD, D), F32)],
        operands=[xn, dpg, dpu])


def _mm_tn(a, b, tn, name, tm=None):
    ca, t, m = a.shape
    cb, _, n = b.shape
    nc = max(ca, cb)
    tm = m if tm is None else tm
    tk = _row_tile(t, 1376)
    nk = t // tk

    def body(a_ref, b_ref, o_ref, acc):
        k = pl.program_id(3)

        @pl.when(k == 0)
        def _():
            acc[...] = jnp.zeros_like(acc)

        acc[...] += _dot_tn(a_ref[...], b_ref[...])

        @pl.when(k == nk - 1)
        def _():
            o_ref[...] = acc[...].astype(BF16)

    return pl.pallas_call(
        body, name=name, grid=(nc, m // tm, n // tn, nk),
        in_specs=[
            pl.BlockSpec((None, tk, tm), (lambda c, i, j, k: (c, k, i)) if ca > 1 else (lambda c, i, j, k: (0, k, i))),
            pl.BlockSpec((None, tk, tn), (lambda c, i, j, k: (c, k, j)) if cb > 1 else (lambda c, i, j, k: (0, k, j))),
        ],
        out_specs=pl.BlockSpec((None, tm, tn), lambda c, i, j, k: (c, i, j)),
        out_shape=S((nc, m, n), BF16),
        scratch_shapes=[pltpu.VMEM((tm, tn), F32)],
        compiler_params=_cp(dimension_semantics=("arbitrary",) * 4),
    )(a, b)


def _norm_mm(h, gain, w, tn, name, side=None):
    t = h.shape[0]
    n = w.shape[1]
    tm = _row_tile(t, 704)

    def body(h_ref, g_ref, w_ref, o_ref, xn_ref):
        @pl.when(pl.program_id(1) == 0)
        def _():
            x = h_ref[...]
            r = lax.rsqrt(jnp.mean(x * x, axis=-1, keepdims=True) + EPS)
            xn_ref[...] = (x * r * g_ref[...]).astype(BF16)

        o_ref[...] = _dot(xn_ref[...], w_ref[...]).astype(BF16)

    return _call(
        body, name=name, grid=(t // tm, n // tn), side=side,
        in_specs=[pl.BlockSpec((tm, D), lambda i, j: (i, 0)), pl.BlockSpec((1, D), lambda i, j: (0, 0)),
                  pl.BlockSpec((D, tn), lambda i, j: (0, j))],
        out_specs=[pl.BlockSpec((tm, tn), lambda i, j: (i, j)), pl.BlockSpec((tm, D), lambda i, j: (i, 0))],
        out_shape=[S((t, n), BF16), S((t, D), BF16)], scratch_shapes=[],
        operands=[h, gain, w])


def _proj_bwd(dproj, w, dh, h, gain, tk, name, side=None):
    t, n = dproj.shape
    tm = _row_tile(t, 704)
    nk = n // tk

    def body(dp_ref, w_ref, dh_ref, h_ref, g_ref, dhi_ref, dgain_ref, acc):
        i, k = pl.program_id(0), pl.program_id(1)

        @pl.when(k == 0)
        def _():
            acc[...] = jnp.zeros_like(acc)

        @pl.when((i == 0) & (k == 0))
        def _():
            dgain_ref[...] = jnp.zeros_like(dgain_ref)

        acc[...] += _dot_nt(dp_ref[...], w_ref[...])

        @pl.when(k == nk - 1)
        def _():
            dx, dgn = _rms_bwd(acc[...], h_ref[...], g_ref[...])
            dhi_ref[...] = dh_ref[...] + dx
            dgain_ref[0:1, :] += dgn

    row = pl.BlockSpec((tm, D), lambda i, k: (i, 0))
    return _call(
        body, name=name, grid=(t // tm, nk), side=side,
        in_specs=[pl.BlockSpec((tm, tk), lambda i, k: (i, k)), pl.BlockSpec((D, tk), lambda i, k: (0, k)),
                  row, row, pl.BlockSpec((1, D), lambda i, k: (0, 0))],
        out_specs=[row, pl.BlockSpec((8, D), lambda i, k: (0, 0))],
        out_shape=[S((t, D), F32), S((8, D), F32)],
        scratch_shapes=[pltpu.VMEM((tm, D), F32)],
        operands=[dproj, w, dh, h, gain])


def _post_fwd(o, proj, hgain, wout, h, nh, dv, name):
    t = h.shape[0]
    w = nh * dv
    tm = _row_tile(t, 704)

    def body(o_ref, g_ref, hg_ref, wo_ref, h_ref, hn_ref, og_ref):
        for hd in range(nh):
            sl = slice(hd * dv, (hd + 1) * dv)
            oh = o_ref[:, sl].astype(F32)
            r = lax.rsqrt(jnp.mean(oh * oh, axis=-1, keepdims=True) + EPS)
            gg = g_ref[:, sl].astype(F32)
            og_ref[:, sl] = (oh * r * hg_ref[:, sl] * (gg * _sigmoid(gg))).astype(BF16)
        hn_ref[...] = h_ref[...] + _dot(og_ref[...], wo_ref[...])

    return pl.pallas_call(
        body, name=name, grid=(t // tm,),
        in_specs=[pl.BlockSpec((tm, w), lambda i: (i, 0)), pl.BlockSpec((tm, w), lambda i: (i, 2)),
                  pl.BlockSpec((1, w), lambda i: (0, 0)), pl.BlockSpec((w, D), lambda i: (0, 0)),
                  pl.BlockSpec((tm, D), lambda i: (i, 0))],
        out_specs=[pl.BlockSpec((tm, D), lambda i: (i, 0)), pl.BlockSpec((tm, w), lambda i: (i, 0))],
        out_shape=[S((t, D), F32), S((t, w), BF16)],
        compiler_params=_cp(dimension_semantics=("arbitrary",)),
    )(o, proj, hgain, wout, h)


def _post_bwd(dh, o, proj, hgain, wout, nh, dv, nproj, name, side=None):
    t = dh.shape[0]
    w = nh * dv
    tm = _row_tile(t, 704)

    def body(dh_ref, o_ref, g_ref, hg_ref, wo_ref, do_ref, dg_ref, dhb_ref, dhg_ref):
        @pl.when(pl.program_id(0) == 0)
        def _():
            dhg_ref[...] = jnp.zeros_like(dhg_ref)

        dmix = dh_ref[...].astype(BF16)
        dhb_ref[...] = dmix
        dog = _dot_nt(dmix, wo_ref[...])
        for hd in range(nh):
            sl = slice(hd * dv, (hd + 1) * dv)
            oh = o_ref[:, sl].astype(F32)
            r = lax.rsqrt(jnp.mean(oh * oh, axis=-1, keepdims=True) + EPS)
            xh = oh * r
            gain = hg_ref[:, sl]
            gg = g_ref[:, sl].astype(F32)
            s = _sigmoid(gg)
            dogh = dog[:, sl]
            don = dogh * (gg * s)
            dg_ref[:, sl] = (dogh * (xh * gain) * (s * (1.0 + gg * (1.0 - s)))).astype(BF16)
            dxh = don * gain
            do_ref[:, sl] = (r * (dxh - xh * jnp.mean(dxh * xh, axis=-1, keepdims=True))).astype(BF16)
            dhg_ref[0:1, sl] += jnp.sum(don * xh, axis=0, keepdims=True)

    return _call(
        body, name=name, grid=(t // tm,), side=side,
        in_specs=[pl.BlockSpec((tm, D), lambda i: (i, 0)), pl.BlockSpec((tm, w), lambda i: (i, 0)),
                  pl.BlockSpec((tm, w), lambda i: (i, 2)), pl.BlockSpec((1, w), lambda i: (0, 0)),
                  pl.BlockSpec((w, D), lambda i: (0, 0))],
        out_specs=[pl.BlockSpec((tm, w), lambda i: (i, 0)), pl.BlockSpec((tm, w), lambda i: (i, 2)),
                   pl.BlockSpec((tm, D), lambda i: (i, 0)), pl.BlockSpec((8, w), lambda i: (0, 0))],
        out_shape=[S((t, w), BF16), S((t, nproj), BF16), S((t, D), BF16), S((8, w), F32)], scratch_shapes=[],
        operands=[dh, o, proj, hgain, wout])


def _ret_consts():
    lg = np.log1p(-np.exp2(-5.0 - np.arange(RET_H, dtype=np.float32))).astype(np.float32)
    return jnp.asarray(np.broadcast_to(lg[:, None, None], (RET_H, 1, 128)).copy())


def _rope_tables(t):
    half = RET_DK // 2
    inv = 1.0 / (ROPE_BASE ** jnp.linspace(0.0, 1.0, half, dtype=F32))
    pos = jnp.maximum(jnp.arange(t) - PAD, 0).astype(F32)
    ang = pos[:, None] * inv[None, :]
    return jnp.cos(ang), jnp.sin(ang)


def _ret_chunk(blk_ref, cos_ref, sin_ref, lg, h):
    c = RET_C
    half = RET_DK // 2
    o = h * RET_HW
    cs, sn = cos_ref[...], sin_ref[...]
    q1, q2 = blk_ref[:, o:o + half].astype(F32), blk_ref[:, o + half:o + RET_DK].astype(F32)
    k1, k2 = blk_ref[:, o + RET_DK:o + RET_DK + half].astype(F32), blk_ref[:, o + RET_DK + half:o + 2 * RET_DK].astype(F32)
    qr = jnp.concatenate([q1 * cs - q2 * sn, q1 * sn + q2 * cs], axis=1)
    kr = jnp.concatenate([k1 * cs - k2 * sn, k1 * sn + k2 * cs], axis=1) * (RET_DK ** -0.5)
    v = blk_ref[:, o + 2 * RET_DK:o + RET_HW]
    ii = lax.broadcasted_iota(jnp.int32, (c, 1), 0).astype(F32)
    jj = lax.broadcasted_iota(jnp.int32, (1, c), 1).astype(F32)
    rel = ii - jj
    dmat = jnp.where(rel >= 0, jnp.exp(lg * jnp.maximum(rel, 0.0)), 0.0)
    dq = jnp.exp(lg * (ii + 1.0))
    dk = jnp.exp(lg * (c - 1.0 - ii))
    dchunk = jnp.exp(lg * float(c))
    return qr, kr, v, dmat, dq, dk, dchunk


def _ret_scan_fwd(proj, cos, sin, lgam, name):
    t = proj.shape[0]
    c = RET_C
    nc = t // c

    def body(blk_ref, cos_ref, sin_ref, lg_ref, o_ref, st_ref, state):
        @pl.when(pl.program_id(0) == 0)
        def _():
            state[...] = jnp.zeros_like(state)

        for h in range(RET_H):
            qr, kr, v, dmat, dq, dk, dchunk = _ret_chunk(blk_ref, cos_ref, sin_ref, lg_ref[h, :, 0:1], h)
            sp = state[h]
            st_ref[h] = sp.astype(BF16)
            scores = _dot_nt(qr.astype(BF16), kr.astype(BF16)) * dmat
            o = _dot(scores.astype(BF16), v) + _dot((qr * dq).astype(BF16), sp.astype(BF16))
            o_ref[:, h * RET_DV:(h + 1) * RET_DV] = o.astype(BF16)
            state[h] = sp * dchunk + _dot_tn((kr * dk).astype(BF16), v)

    return pl.pallas_call(
        body, name=name, grid=(nc,),
        in_specs=[pl.BlockSpec((c, RET_H * RET_HW), lambda n: (n, 0)), pl.BlockSpec((c, 128), lambda n: (n, 0)),
                  pl.BlockSpec((c, 128), lambda n: (n, 0)), pl.BlockSpec((RET_H, 1, 128), lambda n: (0, 0, 0))],
        out_specs=[pl.BlockSpec((c, RET_H * RET_DV), lambda n: (n, 0)),
                   pl.BlockSpec((RET_H, None, RET_DK, RET_DV), lambda n: (0, n, 0, 0))],
        out_shape=[S((t, RET_H * RET_DV), BF16), S((RET_H, nc, RET_DK, RET_DV), BF16)],
        scratch_shapes=[pltpu.VMEM((RET_H, RET_DK, RET_DV), F32)],
        compiler_params=_cp(dimension_semantics=("arbitrary",)),
    )(proj, cos, sin, lgam)


def _ret_scan_bwd(proj, cos, sin, lgam, do, states, dproj, name, side=None):
    t = proj.shape[0]
    c = RET_C
    nc = t // c
    half = RET_DK // 2

    def body(blk_ref, cos_ref, sin_ref, lg_ref, do_ref, st_ref, dp_in, dp_ref, dstate):
        n = nc - 1 - pl.program_id(0)

        @pl.when(pl.program_id(0) == 0)
        def _():
            dstate[...] = jnp.zeros_like(dstate)

        cs, sn = cos_ref[...], sin_ref[...]
        rows = n * c + lax.broadcasted_iota(jnp.int32, (c, 1), 0)
        keep = rows >= PAD

        def unrot(d):
            d1, d2 = d[:, :half], d[:, half:]
            return jnp.concatenate([d1 * cs + d2 * sn, d2 * cs - d1 * sn], axis=1)

        for h in range(RET_H):
            qr, kr, v, dmat, dq, dk, dchunk = _ret_chunk(blk_ref, cos_ref, sin_ref, lg_ref[h, :, 0:1], h)
            qb, kb = qr.astype(BF16), kr.astype(BF16)
            dob = do_ref[:, h * RET_DV:(h + 1) * RET_DV]
            sp = st_ref[h]
            ds = dstate[h]
            dsb = ds.astype(BF16)
            p = (_dot_nt(qb, kb) * dmat).astype(BF16)
            dvv = _dot_tn(p, dob) + _dot((kr * dk).astype(BF16), dsb)
            dp = (_dot_nt(dob, v) * dmat).astype(BF16)
            dqr = _dot(dp, kb) + _dot_nt(dob, sp) * dq
            dkr = (_dot_tn(dp, qb) + _dot_nt(v, dsb) * dk) * (RET_DK ** -0.5)
            dstate[h] = ds * dchunk + _dot_tn((qr * dq).astype(BF16), dob)
            out = jnp.concatenate([unrot(dqr), unrot(dkr), dvv], axis=1)
            dp_ref[:, h * RET_HW:(h + 1) * RET_HW] = jnp.where(keep, out, 0.0).astype(BF16)

    return _call(
        body, name=name, grid=(nc,), side=side, aliases={6: 0},
        in_specs=[pl.BlockSpec((c, RET_H * RET_HW), lambda n: (nc - 1 - n, 0)), pl.BlockSpec((c, 128), lambda n: (nc - 1 - n, 0)),
                  pl.BlockSpec((c, 128), lambda n: (nc - 1 - n, 0)), pl.BlockSpec((RET_H, 1, 128), lambda n: (0, 0, 0)),
                  pl.BlockSpec((c, RET_H * RET_DV), lambda n: (nc - 1 - n, 0)),
                  pl.BlockSpec((RET_H, None, RET_DK, RET_DV), lambda n: (0, nc - 1 - n, 0, 0)), ANY],
        out_specs=[pl.BlockSpec((c, RET_H * RET_HW), lambda n: (nc - 1 - n, 0))],
        out_shape=[S((t, dproj.shape[1]), BF16)],
        scratch_shapes=[pltpu.VMEM((RET_H, RET_DK, RET_DV), F32)],
        operands=[proj, cos, sin, lgam, do, states, dproj])


def _split3(x):
    hi = x.astype(BF16)
    r1 = x - hi.astype(F32)
    mid = r1.astype(BF16)
    lo = (r1 - mid.astype(F32)).astype(BF16)
    return hi, mid, lo


def _gla_chunk(blk_ref, z_ref, wg_ref, bg_ref, n, h):
    c = CHUNK
    o = h * GLA_HW
    q = blk_ref[:, o:o + GLA_DK].astype(F32) * (GLA_DK ** -0.5)
    k = blk_ref[:, o + GLA_DK:o + 2 * GLA_DK].astype(F32)
    v = blk_ref[:, o + 2 * GLA_DK:o + GLA_HW]
    hs = slice(h * GLA_DK, (h + 1) * GLA_DK)
    u = _dot(z_ref[...], wg_ref[:, hs]) + bg_ref[:, hs]
    la = (jnp.minimum(u, 0.0) - jnp.log(1.0 + jnp.exp(-jnp.abs(u)))) * (1.0 / GLA_TAU)
    rows = n * c + lax.broadcasted_iota(jnp.int32, (c, 1), 0)
    keep = rows >= PAD
    la = jnp.where(keep, la, 0.0)
    ii = lax.broadcasted_iota(jnp.int32, (c, c), 0)
    jj = lax.broadcasted_iota(jnp.int32, (c, c), 1)
    tril = (ii >= jj).astype(BF16)
    hi, mid, lo = _split3(la)
    b = _dot(tril, hi) + _dot(tril, mid) + _dot(tril, lo)
    return q, k, v, u, b, keep


def _gla_intra(qs, ks, bs, a_ref):
    c = CHUNK
    nh = len(qs)
    col = lax.broadcasted_iota(jnp.int32, (1, c), 1)
    rowi = lax.broadcasted_iota(jnp.int32, (SUB, 1), 0)
    for blk in range(c // SUB):
        r = slice(SUB * blk, SUB * (blk + 1))
        arows = []
        for h in range(nh):
            q, k, b = qs[h], ks[h], bs[h]
            if blk > 0:
                bprev = b[SUB * blk - 1:SUB * blk]
                qe = q[r] * jnp.exp(b[r] - bprev)
                kt = k * jnp.exp(jnp.minimum(bprev - b, 0.0))
                arows.append(jnp.where(col < SUB * blk, _dot_nt(qe.astype(BF16), kt.astype(BF16)), 0.0))
            else:
                arows.append(jnp.zeros((SUB, c), F32))
        for j in range(SUB):
            for h in range(nh):
                b_i = bs[h][r]
                e = jnp.exp(b_i - b_i[j:j + 1])
                a = jnp.sum(qs[h][r] * ks[h][r][j:j + 1] * e, axis=1, keepdims=True)
                arows[h] = jnp.where(col == SUB * blk + j, a, arows[h])
        for h in range(nh):
            a_ref[h, r, :] = jnp.where(col - SUB * blk <= rowi, arows[h], 0.0)


def _gla_scan_fwd(proj, wgp, bg, name):
    t = proj.shape[0]
    c = CHUNK
    nc = t // c
    heads = range(GLA_H)

    def body(blk_ref, z_ref, wg_ref, bg_ref, o_ref, st_ref, state, a_ref):
        n = pl.program_id(0)

        @pl.when(n == 0)
        def _():
            state[...] = jnp.zeros_like(state)

        qs, ks, vs, us, bs, keeps = zip(*[_gla_chunk(blk_ref, z_ref, wg_ref, bg_ref, n, h) for h in heads])
        _gla_intra(qs, ks, bs, a_ref)
        for h in heads:
            q, k, v, b = qs[h], ks[h], vs[h], bs[h]
            sp = state[h]
            st_ref[h] = sp.astype(BF16)
            o = _dot(a_ref[h].astype(BF16), v) + _dot_nt((q * jnp.exp(b)).astype(BF16), sp.astype(BF16))
            o_ref[:, h * GLA_DV:(h + 1) * GLA_DV] = o.astype(BF16)
            bc = b[c - 1:c]
            state[h] = sp * jnp.exp(bc) + _dot_tn(v, (k * jnp.exp(bc - b)).astype(BF16))

    return pl.pallas_call(
        body, name=name, grid=(nc,),
        in_specs=[pl.BlockSpec((c, GLA_H * GLA_HW), lambda n: (n, 0)), pl.BlockSpec((c, 128), lambda n: (n, GLA_ZBLK)),
                  pl.BlockSpec((128, GLA_H * GLA_DK), lambda n: (0, 0)), pl.BlockSpec((1, GLA_H * GLA_DK), lambda n: (0, 0))],
        out_specs=[pl.BlockSpec((c, GLA_H * GLA_DV), lambda n: (n, 0)),
                   pl.BlockSpec((GLA_H, None, GLA_DV, GLA_DK), lambda n: (0, n, 0, 0))],
        out_shape=[S((t, GLA_H * GLA_DV), BF16), S((GLA_H, nc, GLA_DV, GLA_DK), BF16)],
        scratch_shapes=[pltpu.VMEM((GLA_H, GLA_DV, GLA_DK), F32), pltpu.VMEM((GLA_H, c, c), F32)],
        compiler_params=_cp(dimension_semantics=("arbitrary",)),
    )(proj, proj, wgp, bg)


def _gla_scan_bwd(proj, wgp, bg, do, states, dproj, name):
    t = proj.shape[0]
    c = CHUNK
    nc = t // c
    heads = range(GLA_H)

    def body(blk_ref, z_ref, wg_ref, bg_ref, do_ref, st_ref, dp_in, dp_ref, du_ref, dstate, a_ref, dq_ref, dkd_ref):
        n = nc - 1 - pl.program_id(0)

        @pl.when(pl.program_id(0) == 0)
        def _():
            dstate[...] = jnp.zeros_like(dstate)

        qs, ks, vs, us, bs, keeps = zip(*[_gla_chunk(blk_ref, z_ref, wg_ref, bg_ref, n, h) for h in heads])
        _gla_intra(qs, ks, bs, a_ref)
        ii = lax.broadcasted_iota(jnp.int32, (c, c), 0)
        jj = lax.broadcasted_iota(jnp.int32, (c, c), 1)
        col = lax.broadcasted_iota(jnp.int32, (1, c), 1)
        rowi = lax.broadcasted_iota(jnp.int32, (SUB, 1), 0)
        rowc = lax.broadcasted_iota(jnp.int32, (c, 1), 0)
        das, dvs, dq_inters, dk_states, extras, dks = [], [], [], [], [], []
        for h in heads:
            q, k, v, b = qs[h], ks[h], vs[h], bs[h]
            ab = a_ref[h].astype(BF16)
            dob = do_ref[:, h * GLA_DV:(h + 1) * GLA_DV]
            sp = st_ref[h]
            ds = dstate[h]
            dsb = ds.astype(BF16)
            bc = b[c - 1:c]
            eb = jnp.exp(b)
            ebc = jnp.exp(bc - b)
            ec = jnp.exp(bc)
            qb = (q * eb).astype(BF16)
            kb = (k * ebc).astype(BF16)
            dvs.append(_dot_tn(ab, dob) + _dot_nt(kb, dsb))
            das.append(jnp.where(ii >= jj, _dot_nt(dob, v), 0.0))
            dq_inters.append(_dot(dob, sp) * eb)
            dk_state = _dot(v, dsb) * ebc
            dk_states.append(dk_state)
            extras.append(jnp.sum(k * dk_state, axis=0, keepdims=True)
                          + ec * jnp.sum(sp.astype(F32) * ds, axis=0, keepdims=True))
            dstate[h] = ds * ec + _dot_tn(dob, qb)
            dks.append(jnp.zeros((c, GLA_DK), F32))

        for blk in range(c // SUB):
            r = slice(SUB * blk, SUB * (blk + 1))
            dq_is, dkds = [], []
            for h in heads:
                q, k, b = qs[h], ks[h], bs[h]
                if blk > 0:
                    bprev = b[SUB * blk - 1:SUB * blk]
                    e_i = jnp.exp(b[r] - bprev)
                    ek = jnp.exp(jnp.minimum(bprev - b, 0.0))
                    daoff = jnp.where(col < SUB * blk, das[h][r], 0.0).astype(BF16)
                    dq_is.append(_dot(daoff, (k * ek).astype(BF16)) * e_i)
                    dks[h] = dks[h] + _dot_tn(daoff, (q[r] * e_i).astype(BF16)) * ek
                else:
                    dq_is.append(jnp.zeros((SUB, GLA_DK), F32))
                dkds.append(jnp.zeros((SUB, GLA_DK), F32))
            for j in range(SUB):
                for h in heads:
                    b_i = bs[h][r]
                    e = jnp.where(rowi >= j, jnp.exp(b_i - b_i[j:j + 1]), 0.0)
                    dacol = jnp.sum(jnp.where(col == SUB * blk + j, das[h][r], 0.0), axis=1, keepdims=True)
                    tt = dacol * e
                    dq_is[h] = dq_is[h] + tt * ks[h][r][j:j + 1]
                    dkds[h] = jnp.where(rowi == j, jnp.sum(tt * qs[h][r], axis=0, keepdims=True), dkds[h])
            for h in heads:
                dq_ref[h, r, :] = dq_is[h]
                dkd_ref[h, r, :] = dkds[h]

        for h in heads:
            q, k, b, u, keep = qs[h], ks[h], bs[h], us[h], keeps[h]
            dq = dq_ref[h] + dq_inters[h]
            dk = dks[h] + dkd_ref[h] + dk_states[h]
            db = q * dq - k * dk + jnp.where(rowc == c - 1, extras[h], 0.0)
            triu = (ii <= jj).astype(BF16)
            hi, mid, lo = _split3(db)
            dla = _dot(triu, hi) + _dot(triu, mid) + _dot(triu, lo)
            du = jnp.where(keep, dla * (1.0 / GLA_TAU) / (1.0 + jnp.exp(u)), 0.0)
            du_ref[:, h * GLA_DK:(h + 1) * GLA_DK] = du.astype(BF16)
            out = jnp.concatenate([dq * (GLA_DK ** -0.5), dk, dvs[h]], axis=1)
            dp_ref[:, h * GLA_HW:(h + 1) * GLA_HW] = jnp.where(keep, out, 0.0).astype(BF16)

    nproj = dproj.shape[1]
    return pl.pallas_call(
        body, name=name, grid=(nc,),
        in_specs=[pl.BlockSpec((c, GLA_H * GLA_HW), lambda n: (nc - 1 - n, 0)), pl.BlockSpec((c, 128), lambda n: (nc - 1 - n, GLA_ZBLK)),
                  pl.BlockSpec((128, GLA_H * GLA_DK), lambda n: (0, 0)), pl.BlockSpec((1, GLA_H * GLA_DK), lambda n: (0, 0)),
                  pl.BlockSpec((c, GLA_H * GLA_DV), lambda n: (nc - 1 - n, 0)),
                  pl.BlockSpec((GLA_H, None, GLA_DV, GLA_DK), lambda n: (0, nc - 1 - n, 0, 0)), ANY],
        out_specs=[pl.BlockSpec((c, GLA_H * GLA_HW), lambda n: (nc - 1 - n, 0)),
                   pl.BlockSpec((c, GLA_H * GLA_DK), lambda n: (nc - 1 - n, 0))],
        out_shape=[S((t, nproj), BF16), S((t, GLA_H * GLA_DK), BF16)],
        input_output_aliases={6: 0},
        scratch_shapes=[pltpu.VMEM((GLA_H, GLA_DV, GLA_DK), F32), pltpu.VMEM((GLA_H, c, c), F32),
                        pltpu.VMEM((GLA_H, c, GLA_DK), F32), pltpu.VMEM((GLA_H, c, GLA_DK), F32)],
        compiler_params=_cp(dimension_semantics=("arbitrary",)),
    )(proj, proj, wgp, bg, do, states, dproj)


def _gla_gate_bwd(du, proj, wgp, dproj, name):
    t = du.shape[0]
    tm = _row_tile(t, 704)
    w = GLA_H * GLA_DK

    def body(du_ref, z_ref, wg_ref, dp_in, dp_ref, dwg_ref, dbg_ref):
        @pl.when(pl.program_id(0) == 0)
        def _():
            dwg_ref[...] = jnp.zeros_like(dwg_ref)
            dbg_ref[...] = jnp.zeros_like(dbg_ref)

        d = du_ref[...]
        dp_ref[...] = _dot_nt(d, wg_ref[...]).astype(BF16)
        dwg_ref[...] += _dot_tn(z_ref[...], d)
        dbg_ref[0:1, :] += jnp.sum(d.astype(F32), axis=0, keepdims=True)

    return pl.pallas_call(
        body, name=name, grid=(t // tm,),
        in_specs=[pl.BlockSpec((tm, w), lambda i: (i, 0)), pl.BlockSpec((tm, 128), lambda i: (i, GLA_ZBLK)),
                  pl.BlockSpec((128, w), lambda i: (0, 0)), ANY],
        out_specs=[pl.BlockSpec((tm, 128), lambda i: (i, GLA_ZBLK)), pl.BlockSpec((128, w), lambda i: (0, 0)),
                   pl.BlockSpec((8, w), lambda i: (0, 0))],
        out_shape=[S(dproj.shape, BF16), S((128, w), F32), S((8, w), F32)],
        input_output_aliases={3: 0},
        compiler_params=_cp(dimension_semantics=("arbitrary",)),
    )(du, proj, wgp, dproj)


def _final_loss(hx, gain, target, name):
    t = hx.shape[0]
    tm = _row_tile(t, 512)

    def body(h_ref, g_ref, t_ref, dh_ref, dgain_ref, loss_ref):
        @pl.when(pl.program_id(0) == 0)
        def _():
            dgain_ref[...] = jnp.zeros_like(dgain_ref)
            loss_ref[...] = jnp.zeros_like(loss_ref)

        x = h_ref[...]
        gain = g_ref[...]
        r = lax.rsqrt(jnp.mean(x * x, axis=-1, keepdims=True) + EPS)
        xh = x * r
        e = xh * gain - t_ref[...]
        loss_ref[...] += 0.5 * jnp.sum(jnp.mean(e * e, axis=-1, keepdims=True), axis=0, keepdims=True)
        dy = e * (1.0 / D)
        dgain_ref[0:1, :] += jnp.sum(dy * xh, axis=0, keepdims=True)
        dxh = dy * gain
        dh_ref[...] = r * (dxh - xh * jnp.mean(dxh * xh, axis=-1, keepdims=True))

    row = pl.BlockSpec((tm, D), lambda i: (i, 0))
    return pl.pallas_call(
        body, name=name, grid=(t // tm,),
        in_specs=[row, pl.BlockSpec((1, D), lambda i: (0, 0)), row],
        out_specs=[row, pl.BlockSpec((8, D), lambda i: (0, 0)), pl.BlockSpec((8, 128), lambda i: (0, 0))],
        out_shape=[S((t, D), F32), S((8, D), F32), S((8, 128), F32)],
        compiler_params=_cp(dimension_semantics=("arbitrary",)),
    )(hx, gain, target)


def _adam_math(w, g, m, v):
    m2 = ADAM_B1 * m + (1.0 - ADAM_B1) * g
    v2 = ADAM_B2 * v + (1.0 - ADAM_B2) * (g * g)
    m_hat = m2 / (1.0 - ADAM_B1 ** ADAM_STEP)
    v_hat = v2 / (1.0 - ADAM_B2 ** ADAM_STEP)
    delta = -ADAM_LR * (m_hat / (jnp.sqrt(v_hat) + ADAM_EPS) + ADAM_WD * w)
    return delta, m2, v2


def _adamw_reduce(recvs, w, m, v, name):
    nl, r, wd = w.shape
    tr = _row_tile(r, 256) if r % 16 == 0 else r
    nr = r // tr

    def body(*refs):
        rv_refs = refs[:nl]
        w_ref, m_ref, v_ref, g_ref, d_ref, m2_ref, v2_ref = refs[nl:]
        layer = pl.program_id(0)

        def total(rv_ref):
            g = rv_ref[0].astype(F32)
            for s in range(1, N_DEV):
                g = g + rv_ref[s].astype(F32)
            return g

        g = total(rv_refs[0])
        for k in range(1, nl):
            g = jnp.where(layer == k, total(rv_refs[k]), g)
        g_ref[...] = g
        d_ref[...], m2_ref[...], v2_ref[...] = _adam_math(w_ref[...], g, m_ref[...], v_ref[...])

    def rv_spec(k):
        return pl.BlockSpec((N_DEV, tr, wd), lambda l, i: (0, jnp.where(l == k, i, jnp.where(l < k, 0, nr - 1)), 0))

    row = pl.BlockSpec((None, tr, wd), lambda l, i: (l, i, 0))
    return pl.pallas_call(
        body, name=name, grid=(nl, nr),
        in_specs=[rv_spec(k) for k in range(nl)] + [row, row, row],
        out_specs=[row] * 4, out_shape=[S((nl, r, wd), F32)] * 4,
        compiler_params=_cp(dimension_semantics=("arbitrary", "arbitrary")),
    )(*recvs, w, m, v)


def _small_reduce(parts, name):
    _, r, wd = parts.shape

    def body(p_ref, o_ref):
        g = p_ref[0]
        for s in range(1, N_DEV):
            g = g + p_ref[s]
        o_ref[...] = g

    return pl.pallas_call(body, name=name, out_shape=S((r, wd), F32), compiler_params=_cp())(parts)


def _adamw_small(w, g, m, v, name):
    def body(w_ref, g_ref, m_ref, v_ref, d_ref, m2_ref, v2_ref):
        d_ref[...], m2_ref[...], v2_ref[...] = _adam_math(w_ref[...], g_ref[...], m_ref[...], v_ref[...])

    return pl.pallas_call(body, name=name, out_shape=[S(w.shape, F32)] * 3, compiler_params=_cp())(w, g, m, v)


def _to_head_major(w, nh, dk, dv):
    kk = w.shape[0]
    q = w[:, :nh * dk].reshape(kk, nh, dk)
    k = w[:, nh * dk:2 * nh * dk].reshape(kk, nh, dk)
    v = w[:, 2 * nh * dk:2 * nh * dk + nh * dv].reshape(kk, nh, dv)
    heads = jnp.concatenate([q, k, v], axis=-1).reshape(kk, nh * (2 * dk + dv))
    return jnp.concatenate([heads, w[:, 2 * nh * dk + nh * dv:]], axis=1)


def _from_head_major(p, nh, dk, dv):
    kk = p.shape[0]
    hw = 2 * dk + dv
    heads = p[:, :nh * hw].reshape(kk, nh, hw)
    q = heads[:, :, :dk].reshape(kk, nh * dk)
    k = heads[:, :, dk:2 * dk].reshape(kk, nh * dk)
    v = heads[:, :, 2 * dk:].reshape(kk, nh * dv)
    return jnp.concatenate([q, k, v, p[:, nh * hw:]], axis=1)


def _rows_from_head_major(p, nh, dk, dv):
    n = p.shape[1]
    hw = 2 * dk + dv
    heads = p[:nh * hw].reshape(nh, hw, n)
    q = heads[:, :dk].reshape(nh * dk, n)
    k = heads[:, dk:2 * dk].reshape(nh * dk, n)
    v = heads[:, 2 * dk:].reshape(nh * dv, n)
    return jnp.concatenate([q, k, v, p[nh * hw:]], axis=0)


def _unshard_cols(g):
    return jnp.transpose(g, (1, 0, 2)).reshape(g.shape[1], N_DEV * g.shape[2])


def _shard_cols(w):
    k, n8 = w.shape
    return jnp.transpose(w.reshape(k, N_DEV, n8 // N_DEV), (1, 0, 2))


def _my_cols(full, width):
    me = 4 * lax.axis_index("x") + 2 * lax.axis_index("y") + lax.axis_index("c")
    return lax.dynamic_slice_in_dim(full, me * width, width, axis=1)


def kernel(x, meta_tokens, norm_ffn1, ffn1_w_in, ffn1_w_out, norm_mix, norm_ffn2, ffn2_w_in, ffn2_w_out, ret_w_in, ret_head_norm, ret_w_out, gla_w_in, gla_w_gate, gla_b_gate, gla_head_norm, gla_w_out, final_norm, loss_target, m_meta_tokens, m_norm_ffn1, m_ffn1_w_in, m_ffn1_w_out, m_norm_mix, m_norm_ffn2, m_ffn2_w_in, m_ffn2_w_out, m_ret_w_in, m_ret_head_norm, m_ret_w_out, m_gla_w_in, m_gla_w_gate, m_gla_b_gate, m_gla_head_norm, m_gla_w_out, m_final_norm, v_meta_tokens, v_norm_ffn1, v_ffn1_w_in, v_ffn1_w_out, v_norm_mix, v_norm_ffn2, v_ffn2_w_in, v_ffn2_w_out, v_ret_w_in, v_ret_head_norm, v_ret_w_out, v_gla_w_in, v_gla_w_gate, v_gla_b_gate, v_gla_head_norm, v_gla_w_out, v_final_norm):
    seq = x.shape[1]
    t = seq + CHUNK
    xs = x[0]
    target = loss_target[0]

    def ffn_w(f):
        w_in, w_out = (ffn1_w_in, ffn1_w_out) if f < 2 else (ffn2_w_in, ffn2_w_out)
        return [w_in[f % 2].astype(BF16), w_out[f % 2].astype(BF16)]

    small = jnp.concatenate([meta_tokens.reshape(-1), ret_head_norm.reshape(-1), gla_w_gate.reshape(-1),
                             gla_b_gate.reshape(-1), gla_head_norm.reshape(-1)])
    n_small = small.shape[0]
    small = jnp.pad(small, (0, 32 * 128 - n_small)).reshape(32, 128)
    sg, win0, wout0 = _run_side(_Gather([small] + ffn_w(0)), "ag_first")
    sg = sg.reshape(N_DEV, 32 * 128)

    def small_cols(off, rows, width):
        return jnp.transpose(sg[:, off:off + rows * width].reshape(N_DEV, rows, width), (1, 0, 2)).reshape(rows, N_DEV * width)

    off = 0
    meta_full = small_cols(off, N_META, D // N_DEV); off += N_META * (D // N_DEV)
    ret_hn = small_cols(off, RET_H, RET_DV // N_DEV).reshape(1, RET_H * RET_DV); off += RET_H * RET_DV // N_DEV
    wgate = small_cols(off, GLA_RANK, GLA_H * GLA_DK // N_DEV); off += GLA_RANK * GLA_H * GLA_DK // N_DEV
    bgate = small_cols(off, 1, GLA_H * GLA_DK // N_DEV); off += GLA_H * GLA_DK // N_DEV
    gla_hn = small_cols(off, GLA_H, GLA_DV // N_DEV).reshape(1, GLA_H * GLA_DV)
    wgp = jnp.pad(wgate, ((0, 128 - GLA_RANK), (0, 0))).astype(BF16)

    cos, sin = _rope_tables(t)
    lgam = _ret_consts()

    h0 = jnp.concatenate([jnp.zeros((PAD, D), F32), meta_full, xs], axis=0)
    g1 = [norm_ffn1[i:i + 1] for i in range(2)]
    gm = [norm_mix[i:i + 1] for i in range(2)]
    g2 = [norm_ffn2[i:i + 1] for i in range(2)]

    (h1, xn_a0, pg_a0, pu_a0), (ret_win_g, ret_wout_g) = _ffn_fwd(
        h0, g1[0], win0, wout0, "ffn1_l0_fwd", side=_Gather([ret_w_in[0].astype(BF16), ret_w_out[0].astype(BF16)]))
    ret_win = _to_head_major(_unshard_cols(ret_win_g), RET_H, RET_DK, RET_DV)
    ret_wout = ret_wout_g.reshape(RET_H * RET_DV, D)
    (rproj, rhn), (win2, wout2) = _norm_mm(h1, gm[0], ret_win, 1536, "ret_proj_fwd", side=_Gather(ffn_w(2)))
    ro, rstates = _ret_scan_fwd(rproj, cos, sin, lgam, "ret_scan_fwd")
    h2, rog = _post_fwd(ro, rproj, ret_hn, ret_wout, h1, RET_H, RET_DV, "ret_post_fwd")
    (h3, xn_b0, pg_b0, pu_b0), (win1, wout1) = _ffn_fwd(h2, g2[0], win2, wout2, "ffn2_l0_fwd", side=_Gather(ffn_w(1)))
    (h4, xn_a1, pg_a1, pu_a1), (gla_win_g, gla_wout_g) = _ffn_fwd(
        h3, g1[1], win1, wout1, "ffn1_l1_fwd", side=_Gather([gla_w_in[0].astype(BF16), gla_w_out[0].astype(BF16)]))
    gla_win = _to_head_major(_unshard_cols(gla_win_g), GLA_H, GLA_DK, GLA_DV)
    gla_win = jnp.pad(gla_win, ((0, 0), (0, GLA_N - gla_win.shape[1])))
    gla_wout = gla_wout_g.reshape(GLA_H * GLA_DV, D)
    (gproj, ghn), (win3, wout3) = _norm_mm(h4, gm[1], gla_win, 640, "gla_proj_fwd", side=_Gather(ffn_w(3)))
    go, gstates = _gla_scan_fwd(gproj, wgp, bgate, "gla_scan_fwd")
    h5, gog = _post_fwd(go, gproj, gla_hn, gla_wout, h4, GLA_H, GLA_DV, "gla_post_fwd")
    (h6, xn_b1, pg_b1, pu_b1), _ = _ffn_fwd(h5, g2[1], win3, wout3, "ffn2_l1_fwd")

    dhx, dfinal, loss_blk = _final_loss(h6[CHUNK:], final_norm.reshape(1, D), target, "final_loss")
    loss = lax.psum(loss_blk[0, 0], ("x", "y", "c"))
    dh = jnp.concatenate([jnp.zeros((CHUNK, D), F32), dhx], axis=0)

    def ffn_back(dh, h_in, xn, gain, pg, pu, win, wout, tag, side=None, dw_side=None):
        (dh_in, dob, dpg, dpu, act, dgain), got = _ffn_bwd(dh, h_in, gain, pg, pu, win, wout, tag + "_bwd", side=side)
        dwout = _mm_tn(act, dob[None], D, tag + "_dw_out").reshape(N_DEV, FF_SHARD // 2, D)
        if dw_side == "own_dw_out":
            dw_side = _Exchange([dwout])
        (dwin,), dw_got = _ffn_dw_in(xn, dpg, dpu, tag + "_dw_in", side=dw_side)
        return dh_in, [dwin, dwout], dgain[0], got, dw_got

    dh, dw_b1, dg2_1, _, _ = ffn_back(dh, h5, xn_b1, g2[1], pg_b1, pu_b1, win3, wout3, "ffn2_l1")

    (gdo, gdproj, gdhb, dghn), _ = _post_bwd(dh, go, gproj, gla_hn, gla_wout, GLA_H, GLA_DV, GLA_N, "gla_post_bwd")
    d_gla_wout = _mm_tn(gog[None], gdhb[None], D, "gla_dw_out").reshape(N_DEV, GLA_H * GLA_DV // N_DEV, D)
    gdproj, gdu = _gla_scan_bwd(gproj, wgp, bgate, gdo, gstates, gdproj, "gla_scan_bwd")
    gdproj, dwg, dbg = _gla_gate_bwd(gdu, gproj, wgp, gdproj, "gla_gate_bwd")
    d_gla_win = _mm_tn(gdproj[None], ghn[None], D, "gla_dw_in", tm=640)[0]
    (dh, dgm_1), _ = _proj_bwd(gdproj, gla_win, dh, h4, gm[1], 640, "gla_proj_bwd")
    n_gla_in = 2 * GLA_H * GLA_DK + 2 * GLA_H * GLA_DV + GLA_RANK
    d_gla_win = _rows_from_head_major(d_gla_win[:n_gla_in], GLA_H, GLA_DK, GLA_DV).reshape(N_DEV, n_gla_in // N_DEV, D)

    dh, dw_a1, dg1_1, rv_b1, rv_gla = ffn_back(dh, h3, xn_a1, g1[1], pg_a1, pu_a1, win1, wout1, "ffn1_l1",
                                               side=_Exchange(dw_b1), dw_side=_Exchange([d_gla_win, d_gla_wout]))
    dh, dw_b0, dg2_0, rv_a1, _ = ffn_back(dh, h2, xn_b0, g2[0], pg_b0, pu_b0, win2, wout2, "ffn2_l0", side=_Exchange(dw_a1))

    (rdo, rdproj, rdhb, drhn), rv_b0_out = _post_bwd(dh, ro, rproj, ret_hn, ret_wout, RET_H, RET_DV, 6 * D, "ret_post_bwd",
                                                     side=_Exchange(dw_b0[1:]))
    d_ret_wout = _mm_tn(rog[None], rdhb[None], D, "ret_dw_out").reshape(N_DEV, RET_H * RET_DV // N_DEV, D)
    (rdproj,), rv_b0_in = _ret_scan_bwd(rproj, cos, sin, lgam, rdo, rstates, rdproj, "ret_scan_bwd", side=_Exchange(dw_b0[:1]))
    rv_b0 = rv_b0_in + rv_b0_out
    d_ret_win = _mm_tn(rhn[None], rdproj[None], 1536, "ret_dw_in")[0]
    (dh, dgm_0), rv_ret_out = _proj_bwd(rdproj, ret_win, dh, h1, gm[0], 1536, "ret_proj_bwd", side=_Exchange([d_ret_wout]))
    d_ret_win = _shard_cols(_from_head_major(d_ret_win, RET_H, RET_DK, RET_DV))

    dh, dw_a0, dg1_0, rv_ret_in, rv_a0_out = ffn_back(dh, h0, xn_a0, g1[0], pg_a0, pu_a0, win0, wout0, "ffn1_l0",
                                                      side=_Exchange([d_ret_win]), dw_side="own_dw_out")
    rv_ret = rv_ret_in + rv_ret_out
    rv_a0 = _run_side(_Exchange(dw_a0[:1]), "xchg_last") + rv_a0_out
    grad_x = dh[CHUNK:][None]

    def adam_t(recvs, w, m, v, tag):
        outs = _adamw_reduce(recvs, *(jnp.swapaxes(a, 1, 2) for a in (w, m, v)), tag)
        return [jnp.swapaxes(o, 1, 2) for o in outs]

    u_ffn1_in = adam_t([rv_a0[0], rv_a1[0]], ffn1_w_in, m_ffn1_w_in, v_ffn1_w_in, "adam_ffn1_w_in")
    u_ffn2_in = adam_t([rv_b0[0], rv_b1[0]], ffn2_w_in, m_ffn2_w_in, v_ffn2_w_in, "adam_ffn2_w_in")
    u_ffn1_out = _adamw_reduce([rv_a0[1], rv_a1[1]], ffn1_w_out, m_ffn1_w_out, v_ffn1_w_out, "adam_ffn1_w_out")
    u_ffn2_out = _adamw_reduce([rv_b0[1], rv_b1[1]], ffn2_w_out, m_ffn2_w_out, v_ffn2_w_out, "adam_ffn2_w_out")
    u_ret_in = _adamw_reduce([rv_ret[0]], ret_w_in, m_ret_w_in, v_ret_w_in, "adam_ret_w_in")
    u_ret_out = _adamw_reduce([rv_ret[1]], ret_w_out, m_ret_w_out, v_ret_w_out, "adam_ret_w_out")
    u_gla_in = adam_t([rv_gla[0]], gla_w_in, m_gla_w_in, v_gla_w_in, "adam_gla_w_in")
    u_gla_out = _adamw_reduce([rv_gla[1]], gla_w_out, m_gla_w_out, v_gla_w_out, "adam_gla_w_out")

    dmeta = dh[PAD:CHUNK]
    parts = jnp.concatenate([
        dg1_0, dg1_1, dgm_0[0], dgm_1[0], dg2_0, dg2_1, dfinal[0], dmeta.reshape(-1), drhn[0], dwg[:GLA_RANK].reshape(-1),
        dbg[0], dghn[0]])
    n_parts = parts.shape[0]
    rows = -(-n_parts // D)
    rows = -(-rows // 8) * 8
    parts = jnp.pad(parts, (0, rows * D - n_parts)).reshape(rows, D)
    tot = _small_reduce(_run_side(_Gather([parts]), "ag_small_grads")[0], "small_grad_sum").reshape(-1)

    off = 0
    def take(nel):
        nonlocal off
        out = tot[off:off + nel]
        off += nel
        return out

    gr_norm_ffn1 = take(2 * D).reshape(2, D)
    gr_norm_mix = take(2 * D).reshape(2, D)
    gr_norm_ffn2 = take(2 * D).reshape(2, D)
    gr_final = take(D)
    gr_meta = _my_cols(take(N_META * D).reshape(N_META, D), D // N_DEV)
    gr_ret_hn = _my_cols(take(RET_H * RET_DV).reshape(RET_H, RET_DV), RET_DV // N_DEV)[None]
    gr_wgate = _my_cols(take(GLA_RANK * GLA_H * GLA_DK).reshape(GLA_RANK, GLA_H * GLA_DK), GLA_H * GLA_DK // N_DEV)[None]
    gr_bgate = _my_cols(take(GLA_H * GLA_DK).reshape(1, GLA_H * GLA_DK), GLA_H * GLA_DK // N_DEV)
    gr_gla_hn = _my_cols(take(GLA_H * GLA_DV).reshape(GLA_H, GLA_DV), GLA_DV // N_DEV)[None]

    small_w = [meta_tokens, norm_ffn1, norm_mix, norm_ffn2, ret_head_norm, gla_w_gate, gla_b_gate, gla_head_norm, final_norm]
    small_g = [gr_meta, gr_norm_ffn1, gr_norm_mix, gr_norm_ffn2, gr_ret_hn, gr_wgate, gr_bgate, gr_gla_hn, gr_final]
    small_m = [m_meta_tokens, m_norm_ffn1, m_norm_mix, m_norm_ffn2, m_ret_head_norm, m_gla_w_gate, m_gla_b_gate, m_gla_head_norm, m_final_norm]
    small_v = [v_meta_tokens, v_norm_ffn1, v_norm_mix, v_norm_ffn2, v_ret_head_norm, v_gla_w_gate, v_gla_b_gate, v_gla_head_norm, v_final_norm]

    def pack(arrs):
        flat = jnp.concatenate([a.reshape(-1) for a in arrs])
        n = flat.shape[0]
        r = -(-n // 128)
        r = -(-r // 8) * 8
        return jnp.pad(flat, (0, r * 128 - n), constant_values=1.0).reshape(r, 128)

    sd, sm, sv = _adamw_small(pack(small_w), pack(small_g), pack(small_m), pack(small_v), "adam_small")

    def unpack(buf):
        flat = buf.reshape(-1)
        outs, o = [], 0
        for a in small_w:
            outs.append(flat[o:o + a.size].reshape(a.shape))
            o += a.size
        return outs

    us_d, us_m, us_v = unpack(sd), unpack(sm), unpack(sv)

    def ordered(k, smalls):
        return (smalls[0], smalls[1], u_ffn1_in[k], u_ffn1_out[k], smalls[2], smalls[3], u_ffn2_in[k], u_ffn2_out[k],
                u_ret_in[k], smalls[4], u_ret_out[k], u_gla_in[k], smalls[5], smalls[6], smalls[7], u_gla_out[k], smalls[8])

    return (loss, grad_x, *ordered(0, small_g), *ordered(1, us_d), *ordered(2, us_m), *ordered(3, us_v))
```

```python
import functools
import math

import numpy as np
import jax
import jax.numpy as jnp
from jax import lax
from jax.experimental import pallas as pl
from jax.experimental.pallas import tpu as pltpu

F32 = jnp.float32
BF16 = jnp.bfloat16
S = jax.ShapeDtypeStruct
ANY = pl.BlockSpec(memory_space=pl.ANY)
MESH = pl.DeviceIdType.MESH

D = 1024
N_META = 16
CHUNK = 64
PAD = CHUNK - N_META
EPS = 1e-6
N_DEV = 8
FF_SHARD = 704
N_FF_CHUNK = 4
RET_H, RET_DK, RET_DV = 4, 256, 512
RET_QKV = RET_H * (2 * RET_DK + RET_DV)
RET_C = 192
GLA_H, GLA_DK, GLA_DV, GLA_RANK, GLA_TAU = 4, 128, 256, 16, 16.0
GLA_QKV = GLA_H * (2 * GLA_DK + GLA_DV)
GLA_N = 3200
GLA_ZBLK = 3072 // 128
SUB = 16
ROPE_BASE = 10000.0
ADAM_LR, ADAM_B1, ADAM_B2, ADAM_EPS, ADAM_WD, ADAM_STEP = 0.001, 0.9, 0.999, 1e-08, 0.01, 10
VMEM_LIMIT = 58 * 1024 * 1024
DW_ROWS = 2752


def _cp(**kw):
    return pltpu.CompilerParams(vmem_limit_bytes=VMEM_LIMIT, **kw)


def _row_tile(t, cap):
    best = 16
    for d in range(16, cap + 1, 16):
        if t % d == 0:
            best = d
    return best


def _sub_rows(tm, parts=2):
    units = tm // 16
    cuts = [16 * (units * p // parts) for p in range(parts + 1)]
    return [slice(a, b) for a, b in zip(cuts[:-1], cuts[1:]) if b > a]


def _dot(a, b):
    return jnp.dot(a, b, preferred_element_type=F32)


def _dot_nt(a, b):
    return lax.dot_general(a, b, (((1,), (1,)), ((), ())), preferred_element_type=F32)


def _dot_tn(a, b):
    return lax.dot_general(a, b, (((0,), (0,)), ((), ())), preferred_element_type=F32)


def _sigmoid(x):
    return pl.reciprocal(1.0 + jnp.exp(-x), approx=True)


def _rms_bwd(dxn, x, gain):
    r = lax.rsqrt(jnp.mean(x * x, axis=-1, keepdims=True) + EPS)
    xh = x * r
    dxh = dxn * gain
    dx = r * (dxh - xh * jnp.mean(dxh * xh, axis=-1, keepdims=True))
    return dx, jnp.sum(dxn * xh, axis=0, keepdims=True)


def _xyc():
    return lax.axis_index("x"), lax.axis_index("y"), lax.axis_index("c")


class _Gather:
    def __init__(self, xs):
        self.xs = list(xs)
        self.n = len(self.xs)

    def out_shape(self):
        return [S((N_DEV,) + a.shape, a.dtype) for a in self.xs]

    def scratch(self):
        return [pltpu.SemaphoreType.DMA((self.n, 7)), pltpu.SemaphoreType.DMA((self.n, 7)), pltpu.SemaphoreType.DMA((self.n,))]

    def phases(self, x_refs, out_refs, send_sems, recv_sems, local_sems):
        x, y, c = _xyc()
        me, sibling = (x, y, c), (x, y, 1 - c)
        chips = [(1 - x, y), (x, 1 - y), (1 - x, 1 - y)]

        def copy(t, k, block, to, src=None):
            px, py, pc = block
            dst = out_refs[t].at[4 * px + 2 * py + pc]
            return pltpu.make_async_remote_copy(
                src_ref=dst if src is None else src, dst_ref=dst,
                send_sem=send_sems.at[t, k], recv_sem=recv_sems.at[t, k], device_id=to, device_id_type=MESH)

        def own(t):
            return pltpu.make_async_copy(x_refs[t], out_refs[t].at[4 * x + 2 * y + c], local_sems.at[t])

        def first(t):
            return [copy(t, 0, me, sibling, src=x_refs[t])] + [
                copy(t, 1 + j, me, (*chip, c), src=x_refs[t]) for j, chip in enumerate(chips)]

        def passed(t):
            return [copy(t, 4 + j, (*chip, c), sibling) for j, chip in enumerate(chips)]

        def start():
            for t in range(self.n):
                own(t).start()
                for cp in first(t):
                    cp.start()

        def mid():
            for t in range(self.n):
                fw = passed(t)
                for j, chip in enumerate(chips):
                    copy(t, 1 + j, (*chip, c), me).wait_recv()
                    fw[j].start()

        def finish():
            for t in range(self.n):
                copy(t, 0, sibling, me).wait_recv()
                for j, chip in enumerate(chips):
                    copy(t, 4 + j, (*chip, 1 - c), me).wait_recv()
                for cp in first(t) + passed(t):
                    cp.wait_send()
                own(t).wait()

        return start, mid, finish


class _Exchange:
    def __init__(self, xs):
        self.xs = list(xs)
        self.n = len(self.xs)

    def out_shape(self):
        return [S(a.shape, a.dtype) for a in self.xs]

    def scratch(self):
        return [pltpu.SemaphoreType.DMA((self.n, 7)), pltpu.SemaphoreType.DMA((self.n, 7)), pltpu.SemaphoreType.DMA((self.n,))]

    def phases(self, g_refs, r_refs, send_sems, recv_sems, local_sems):
        x, y, c = _xyc()
        me = 4 * x + 2 * y + c

        def own(t):
            return pltpu.make_async_copy(g_refs[t].at[me], r_refs[t].at[me], local_sems.at[t])

        def send(t, m):
            px, py, pc = x ^ (m >> 2), y ^ ((m >> 1) & 1), c ^ (m & 1)
            return pltpu.make_async_remote_copy(
                src_ref=g_refs[t].at[4 * px + 2 * py + pc], dst_ref=r_refs[t].at[me],
                send_sem=send_sems.at[t, m - 1], recv_sem=recv_sems.at[t, m - 1],
                device_id=(px, py, pc), device_id_type=MESH)

        def arrival(t, m):
            peer = 4 * (x ^ (m >> 2)) + 2 * (y ^ ((m >> 1) & 1)) + (c ^ (m & 1))
            return pltpu.make_async_remote_copy(
                src_ref=g_refs[t].at[peer], dst_ref=r_refs[t].at[peer],
                send_sem=send_sems.at[t, m - 1], recv_sem=recv_sems.at[t, m - 1],
                device_id=(x, y, c), device_id_type=MESH)

        def start():
            for t in range(self.n):
                own(t).start()
            for m in range(1, N_DEV):
                for t in range(self.n):
                    send(t, m).start()

        def mid():
            pass

        def finish():
            for m in range(1, N_DEV):
                for t in range(self.n):
                    arrival(t, m).wait_recv()
            for m in range(1, N_DEV):
                for t in range(self.n):
                    send(t, m).wait_send()
            for t in range(self.n):
                own(t).wait()

        return start, mid, finish


def _run_side(side, name):
    n = side.n

    def body(*refs):
        start, mid, finish = side.phases(refs[:n], refs[n:2 * n], *refs[2 * n:])
        start()
        mid()
        finish()

    return list(pl.pallas_call(
        body, name=name, out_shape=side.out_shape(), in_specs=[ANY] * n, out_specs=[ANY] * n,
        scratch_shapes=side.scratch())(*side.xs))


def _grid_steps(grid):
    def ids():
        return [pl.program_id(a) for a in range(len(grid))]

    def first():
        return functools.reduce(jnp.logical_and, [i == 0 for i in ids()])

    def middle():
        i = ids()
        return functools.reduce(jnp.logical_and, [i[0] == (3 * grid[0]) // 4] + [j == 0 for j in i[1:]])

    def last():
        return functools.reduce(jnp.logical_and, [i == g - 1 for i, g in zip(ids(), grid)])

    return first, middle, last


def _call(body, *, name, grid, in_specs, out_specs, out_shape, scratch_shapes, operands, side=None, aliases=None):
    n_in, n_out, n_scr = len(in_specs), len(out_shape), len(scratch_shapes)
    full = body
    if side is not None:
        ns = side.n
        first, middle, last = _grid_steps(grid)

        def full(*refs):
            a = n_in
            ins, sins = refs[:a], refs[a:a + ns]
            a += ns
            outs, souts = refs[a:a + n_out], refs[a + n_out:a + n_out + ns]
            a += n_out + ns
            scr, sems = refs[a:a + n_scr], refs[a + n_scr:]
            start, mid, finish = side.phases(sins, souts, *sems)
            pl.when(first())(start)
            body(*ins, *outs, *scr)
            pl.when(middle())(mid)
            pl.when(last())(finish)

        in_specs = list(in_specs) + [ANY] * ns
        out_specs = list(out_specs) + [ANY] * ns
        out_shape = list(out_shape) + side.out_shape()
        scratch_shapes = list(scratch_shapes) + side.scratch()
        operands = list(operands) + side.xs
    outs = pl.pallas_call(
        full, name=name, grid=grid, in_specs=list(in_specs), out_specs=list(out_specs), out_shape=list(out_shape),
        scratch_shapes=list(scratch_shapes), input_output_aliases=aliases or {},
        compiler_params=_cp(dimension_semantics=("arbitrary",) * len(grid)),
    )(*operands)
    return list(outs[:n_out]), list(outs[n_out:])


def _ffn_fwd(h, gain, win, wout, name, side=None):
    t = h.shape[0]
    tm = _row_tile(t, 704)
    nt = t // tm

    def body(h_ref, g_ref, wg_ref, wu_ref, wo_ref, hn_ref, xn_ref, pg_ref, pu_ref, acc):
        c = pl.program_id(1)

        @pl.when(c == 0)
        def _():
            x = h_ref[...]
            r = lax.rsqrt(jnp.mean(x * x, axis=-1, keepdims=True) + EPS)
            xn_ref[...] = (x * r * g_ref[...]).astype(BF16)
            acc[...] = jnp.zeros_like(acc)

        wo = wo_ref[...].reshape(FF_SHARD, D)
        subs = _sub_rows(tm)
        gus = [(_dot(xn_ref[r, :], wg_ref[...]), _dot(xn_ref[r, :], wu_ref[...])) for r in subs]
        for r, (g, u) in zip(subs, gus):
            pg_ref[r, :] = g.astype(BF16)
            pu_ref[r, :] = u.astype(BF16)
            act = (g * _sigmoid(g) * u).astype(BF16)
            acc[r, :] += _dot(act, wo)

        @pl.when(c == N_FF_CHUNK - 1)
        def _():
            hn_ref[...] = h_ref[...] + 0.5 * acc[...]

    return _call(
        body, name=name, grid=(nt, N_FF_CHUNK), side=side,
        in_specs=[
            pl.BlockSpec((tm, D), lambda i, c: (i, 0)),
            pl.BlockSpec((1, D), lambda i, c: (0, 0)),
            pl.BlockSpec((None, D, FF_SHARD), lambda i, c: (c, 0, 0)),
            pl.BlockSpec((None, D, FF_SHARD), lambda i, c: (c + N_FF_CHUNK, 0, 0)),
            pl.BlockSpec((2, FF_SHARD // 2, D), lambda i, c: (c, 0, 0)),
        ],
        out_specs=[
            pl.BlockSpec((tm, D), lambda i, c: (i, 0)),
            pl.BlockSpec((tm, D), lambda i, c: (i, 0)),
            pl.BlockSpec((None, tm, FF_SHARD), lambda i, c: (c, i, 0)),
            pl.BlockSpec((None, tm, FF_SHARD), lambda i, c: (c, i, 0)),
        ],
        out_shape=[S((t, D), F32), S((t, D), BF16), S((N_FF_CHUNK, t, FF_SHARD), BF16), S((N_FF_CHUNK, t, FF_SHARD), BF16)],
        scratch_shapes=[pltpu.VMEM((tm, D), F32)],
        operands=[h, gain, win, win, wout])


def _ffn_bwd(dh, h, gain, pg, pu, win, wout, name, side=None):
    t = h.shape[0]
    tm = _row_tile(t, 704)
    nt = t // tm

    def body(dh_ref, h_ref, g_ref, pg_ref, pu_ref, wg_ref, wu_ref, wo_ref,
             dhi_ref, dob_ref, dpg_ref, dpu_ref, act_ref, dgain_ref, acc):
        i, c = pl.program_id(0), pl.program_id(1)

        @pl.when(c == 0)
        def _():
            dob_ref[...] = (0.5 * dh_ref[...]).astype(BF16)
            acc[...] = jnp.zeros_like(acc)

        @pl.when((i == 0) & (c == 0))
        def _():
            dgain_ref[...] = jnp.zeros_like(dgain_ref)

        wo = wo_ref[...].reshape(FF_SHARD, D)
        subs = _sub_rows(tm)
        dacts = [_dot_nt(dob_ref[r, :], wo) for r in subs]
        for r, dact in zip(subs, dacts):
            g = pg_ref[r, :].astype(F32)
            u = pu_ref[r, :].astype(F32)
            s = _sigmoid(g)
            sl = g * s
            act_ref[r, :] = (sl * u).astype(BF16)
            dg = (dact * u * (s * (1.0 + g * (1.0 - s)))).astype(BF16)
            du = (dact * sl).astype(BF16)
            dpg_ref[r, :] = dg
            dpu_ref[r, :] = du
            acc[r, :] += _dot_nt(dg, wg_ref[...]) + _dot_nt(du, wu_ref[...])

        @pl.when(c == N_FF_CHUNK - 1)
        def _():
            dx, dgn = _rms_bwd(acc[...], h_ref[...], g_ref[...])
            dhi_ref[...] = dh_ref[...] + dx
            dgain_ref[0:1, :] += dgn

    blk = pl.BlockSpec((None, tm, FF_SHARD), lambda i, c: (c, i, 0))
    row = pl.BlockSpec((tm, D), lambda i, c: (i, 0))
    return _call(
        body, name=name, grid=(nt, N_FF_CHUNK), side=side,
        in_specs=[
            row, row, pl.BlockSpec((1, D), lambda i, c: (0, 0)), blk, blk,
            pl.BlockSpec((None, D, FF_SHARD), lambda i, c: (c, 0, 0)),
            pl.BlockSpec((None, D, FF_SHARD), lambda i, c: (c + N_FF_CHUNK, 0, 0)),
            pl.BlockSpec((2, FF_SHARD // 2, D), lambda i, c: (c, 0, 0)),
        ],
        out_specs=[row, row, blk, blk, blk, pl.BlockSpec((8, D), lambda i, c: (0, 0))],
        out_shape=[S((t, D), F32), S((t, D), BF16)] + [S((N_FF_CHUNK, t, FF_SHARD), BF16)] * 3 + [S((8, D), F32)],
        scratch_shapes=[pltpu.VMEM((tm, D), F32)],
        operands=[dh, h, gain, pg, pu, win, win, wout])


def _ffn_dw_in(xn, dpg, dpu, name, side=None):
    t = xn.shape[0]
    tk = _row_tile(t, DW_ROWS)
    nk = t // tk

    def body(a_ref, bg_ref, bu_ref, o_ref, acc):
        c, k = pl.program_id(0), pl.program_id(1)

        @pl.when(k == 0)
        def _():
            acc[...] = jnp.zeros_like(acc)

        @pl.when(c < N_FF_CHUNK)
        def _():
            acc[...] += _dot_tn(bg_ref[...], a_ref[...])

        @pl.when(c >= N_FF_CHUNK)
        def _():
            acc[...] += _dot_tn(bu_ref[...], a_ref[...])

        @pl.when(k == nk - 1)
        def _():
            o_ref[...] = acc[...].astype(BF16)

    return _call(
        body, name=name, grid=(2 * N_FF_CHUNK, nk), side=side,
        in_specs=[
            pl.BlockSpec((tk, D), lambda c, k: (k, 0)),
            pl.BlockSpec((None, tk, FF_SHARD), lambda c, k: (jnp.minimum(c, N_FF_CHUNK - 1), k, 0)),
            pl.BlockSpec((None, tk, FF_SHARD), lambda c, k: (jnp.maximum(c - N_FF_CHUNK, 0), k, 0)),
        ],
        out_specs=[pl.BlockSpec((None, FF_SHARD, D), lambda c, k: (c, 0, 0))],
        out_shape=[S((2 * N_FF_CHUNK, FF_SHARD, D), BF16)],
        scratch_shapes=[pltpu.VMEM((FF_SHARD, D), F32)],
        operands=[xn, dpg, dpu])


def _mm_tn(a, b, tn, name, tm=None, rows=DW_ROWS, shard_out=False):
    ca, t, m = a.shape
    cb, _, n = b.shape
    nc = max(ca, cb)
    tm = m if tm is None else tm
    tk = _row_tile(t, rows)
    nk = t // tk

    def body(a_ref, b_ref, o_ref, acc):
        k = pl.program_id(3)

        @pl.when(k == 0)
        def _():
            acc[...] = jnp.zeros_like(acc)

        acc[...] += _dot_tn(a_ref[...], b_ref[...])

        @pl.when(k == nk - 1)
        def _():
            o_ref[...] = acc[...].astype(BF16)

    if shard_out:
        out_spec = pl.BlockSpec((None, tm, tn), lambda c, i, j, k: (j, 0, 0))
        out_shape = S((n // tn, m, tn), BF16)
    else:
        out_spec = pl.BlockSpec((None, tm, tn), lambda c, i, j, k: (c, i, j))
        out_shape = S((nc, m, n), BF16)
    return pl.pallas_call(
        body, name=name, grid=(nc, m // tm, n // tn, nk),
        in_specs=[
            pl.BlockSpec((None, tk, tm), (lambda c, i, j, k: (c, k, i)) if ca > 1 else (lambda c, i, j, k: (0, k, i))),
            pl.BlockSpec((None, tk, tn), (lambda c, i, j, k: (c, k, j)) if cb > 1 else (lambda c, i, j, k: (0, k, j))),
        ],
        out_specs=out_spec, out_shape=out_shape,
        scratch_shapes=[pltpu.VMEM((tm, tn), F32)],
        compiler_params=_cp(dimension_semantics=("arbitrary",) * 4),
    )(a, b)


def _norm_mm(h, gain, w, tn, name, side=None):
    t = h.shape[0]
    n = w.shape[-1] if w.ndim == 2 else w.shape[0] * w.shape[2]
    tm = _row_tile(t, 704)
    w_spec = (pl.BlockSpec((D, tn), lambda i, j: (0, j)) if w.ndim == 2
              else pl.BlockSpec((None, D, tn), lambda i, j: (j, 0, 0)))

    def body(h_ref, g_ref, w_ref, o_ref, xn_ref):
        @pl.when(pl.program_id(1) == 0)
        def _():
            x = h_ref[...]
            r = lax.rsqrt(jnp.mean(x * x, axis=-1, keepdims=True) + EPS)
            xn_ref[...] = (x * r * g_ref[...]).astype(BF16)

        o_ref[...] = _dot(xn_ref[...], w_ref[...]).astype(BF16)

    return _call(
        body, name=name, grid=(t // tm, n // tn), side=side,
        in_specs=[pl.BlockSpec((tm, D), lambda i, j: (i, 0)), pl.BlockSpec((1, D), lambda i, j: (0, 0)), w_spec],
        out_specs=[pl.BlockSpec((tm, tn), lambda i, j: (i, j)), pl.BlockSpec((tm, D), lambda i, j: (i, 0))],
        out_shape=[S((t, n), BF16), S((t, D), BF16)], scratch_shapes=[],
        operands=[h, gain, w])


def _proj_bwd(dproj, w, dh, h, gain, tk, name, side=None):
    t, n = dproj.shape
    tm = _row_tile(t, 704)
    nk = n // tk
    w_spec = (pl.BlockSpec((D, tk), lambda i, k: (0, k)) if w.ndim == 2
              else pl.BlockSpec((None, D, tk), lambda i, k: (k, 0, 0)))

    def body(dp_ref, w_ref, dh_ref, h_ref, g_ref, dhi_ref, dgain_ref, acc):
        i, k = pl.program_id(0), pl.program_id(1)

        @pl.when(k == 0)
        def _():
            acc[...] = jnp.zeros_like(acc)

        @pl.when((i == 0) & (k == 0))
        def _():
            dgain_ref[...] = jnp.zeros_like(dgain_ref)

        acc[...] += _dot_nt(dp_ref[...], w_ref[...])

        @pl.when(k == nk - 1)
        def _():
            dx, dgn = _rms_bwd(acc[...], h_ref[...], g_ref[...])
            dhi_ref[...] = dh_ref[...] + dx
            dgain_ref[0:1, :] += dgn

    row = pl.BlockSpec((tm, D), lambda i, k: (i, 0))
    return _call(
        body, name=name, grid=(t // tm, nk), side=side,
        in_specs=[pl.BlockSpec((tm, tk), lambda i, k: (i, k)), w_spec,
                  row, row, pl.BlockSpec((1, D), lambda i, k: (0, 0))],
        out_specs=[row, pl.BlockSpec((8, D), lambda i, k: (0, 0))],
        out_shape=[S((t, D), F32), S((8, D), F32)],
        scratch_shapes=[pltpu.VMEM((tm, D), F32)],
        operands=[dproj, w, dh, h, gain])


def _post_fwd(o, proj, hgain, wout, h, nh, dv, name):
    t = h.shape[0]
    w = nh * dv
    tm = _row_tile(t, 704)

    def body(o_ref, g_ref, hg_ref, wo_ref, h_ref, hn_ref, og_ref):
        for hd in range(nh):
            sl = slice(hd * dv, (hd + 1) * dv)
            oh = o_ref[:, sl].astype(F32)
            r = lax.rsqrt(jnp.mean(oh * oh, axis=-1, keepdims=True) + EPS)
            gg = g_ref[:, sl].astype(F32)
            og_ref[:, sl] = (oh * r * hg_ref[:, sl] * (gg * _sigmoid(gg))).astype(BF16)
        hn_ref[...] = h_ref[...] + _dot(og_ref[...], wo_ref[...])

    return pl.pallas_call(
        body, name=name, grid=(t // tm,),
        in_specs=[pl.BlockSpec((tm, w), lambda i: (i, 0)), pl.BlockSpec((tm, w), lambda i: (i, 2)),
                  pl.BlockSpec((1, w), lambda i: (0, 0)), pl.BlockSpec((w, D), lambda i: (0, 0)),
                  pl.BlockSpec((tm, D), lambda i: (i, 0))],
        out_specs=[pl.BlockSpec((tm, D), lambda i: (i, 0)), pl.BlockSpec((tm, w), lambda i: (i, 0))],
        out_shape=[S((t, D), F32), S((t, w), BF16)],
        compiler_params=_cp(dimension_semantics=("arbitrary",)),
    )(o, proj, hgain, wout, h)


def _post_bwd(dh, o, proj, hgain, wout, nh, dv, nproj, name, side=None):
    t = dh.shape[0]
    w = nh * dv
    tm = _row_tile(t, 704)

    def body(dh_ref, o_ref, g_ref, hg_ref, wo_ref, do_ref, dg_ref, dhb_ref, dhg_ref):
        @pl.when(pl.program_id(0) == 0)
        def _():
            dhg_ref[...] = jnp.zeros_like(dhg_ref)

        dmix = dh_ref[...].astype(BF16)
        dhb_ref[...] = dmix
        dog = _dot_nt(dmix, wo_ref[...])
        for hd in range(nh):
            sl = slice(hd * dv, (hd + 1) * dv)
            oh = o_ref[:, sl].astype(F32)
            r = lax.rsqrt(jnp.mean(oh * oh, axis=-1, keepdims=True) + EPS)
            xh = oh * r
            gain = hg_ref[:, sl]
            gg = g_ref[:, sl].astype(F32)
            s = _sigmoid(gg)
            dogh = dog[:, sl]
            don = dogh * (gg * s)
            dg_ref[:, sl] = (dogh * (xh * gain) * (s * (1.0 + gg * (1.0 - s)))).astype(BF16)
            dxh = don * gain
            do_ref[:, sl] = (r * (dxh - xh * jnp.mean(dxh * xh, axis=-1, keepdims=True))).astype(BF16)
            dhg_ref[0:1, sl] += jnp.sum(don * xh, axis=0, keepdims=True)

    return _call(
        body, name=name, grid=(t // tm,), side=side,
        in_specs=[pl.BlockSpec((tm, D), lambda i: (i, 0)), pl.BlockSpec((tm, w), lambda i: (i, 0)),
                  pl.BlockSpec((tm, w), lambda i: (i, 2)), pl.BlockSpec((1, w), lambda i: (0, 0)),
                  pl.BlockSpec((w, D), lambda i: (0, 0))],
        out_specs=[pl.BlockSpec((tm, w), lambda i: (i, 0)), pl.BlockSpec((tm, w), lambda i: (i, 2)),
                   pl.BlockSpec((tm, D), lambda i: (i, 0)), pl.BlockSpec((8, w), lambda i: (0, 0))],
        out_shape=[S((t, w), BF16), S((t, nproj), BF16), S((t, D), BF16), S((8, w), F32)], scratch_shapes=[],
        operands=[dh, o, proj, hgain, wout])


def _ret_consts():
    lg = np.log1p(-np.exp2(-5.0 - np.arange(RET_H, dtype=np.float32))).astype(np.float32)
    return jnp.asarray(np.broadcast_to(lg[:, None, None], (RET_H, 1, 128)).copy())


def _rope_tables(t):
    half = RET_DK // 2
    inv = 1.0 / (ROPE_BASE ** jnp.linspace(0.0, 1.0, half, dtype=F32))
    pos = jnp.maximum(jnp.arange(t) - PAD, 0).astype(F32)
    ang = pos[:, None] * inv[None, :]
    return jnp.cos(ang), jnp.sin(ang)


def _ret_chunk(blk_ref, cos_ref, sin_ref, lg, h):
    c = RET_C
    half = RET_DK // 2
    oq, ok, ov = h * RET_DK, RET_H * RET_DK + h * RET_DK, 2 * RET_H * RET_DK + h * RET_DV
    cs, sn = cos_ref[...], sin_ref[...]
    q1, q2 = blk_ref[:, oq:oq + half].astype(F32), blk_ref[:, oq + half:oq + RET_DK].astype(F32)
    k1, k2 = blk_ref[:, ok:ok + half].astype(F32), blk_ref[:, ok + half:ok + RET_DK].astype(F32)
    qr = jnp.concatenate([q1 * cs - q2 * sn, q1 * sn + q2 * cs], axis=1)
    kr = jnp.concatenate([k1 * cs - k2 * sn, k1 * sn + k2 * cs], axis=1) * (RET_DK ** -0.5)
    v = blk_ref[:, ov:ov + RET_DV]
    ii = lax.broadcasted_iota(jnp.int32, (c, 1), 0).astype(F32)
    jj = lax.broadcasted_iota(jnp.int32, (1, c), 1).astype(F32)
    rel = ii - jj
    dmat = jnp.where(rel >= 0, jnp.exp(lg * jnp.maximum(rel, 0.0)), 0.0)
    dq = jnp.exp(lg * (ii + 1.0))
    dk = jnp.exp(lg * (c - 1.0 - ii))
    dchunk = jnp.exp(lg * float(c))
    return qr, kr, v, dmat, dq, dk, dchunk


def _ret_scan_fwd(proj, cos, sin, lgam, name):
    t = proj.shape[0]
    c = RET_C
    nc = t // c

    def body(blk_ref, cos_ref, sin_ref, lg_ref, o_ref, st_ref, state):
        @pl.when(pl.program_id(0) == 0)
        def _():
            state[...] = jnp.zeros_like(state)

        for h in range(RET_H):
            qr, kr, v, dmat, dq, dk, dchunk = _ret_chunk(blk_ref, cos_ref, sin_ref, lg_ref[h, :, 0:1], h)
            sp = state[h]
            st_ref[h] = sp.astype(BF16)
            scores = _dot_nt(qr.astype(BF16), kr.astype(BF16)) * dmat
            o = _dot(scores.astype(BF16), v) + _dot((qr * dq).astype(BF16), sp.astype(BF16))
            o_ref[:, h * RET_DV:(h + 1) * RET_DV] = o.astype(BF16)
            state[h] = sp * dchunk + _dot_tn((kr * dk).astype(BF16), v)

    return pl.pallas_call(
        body, name=name, grid=(nc,),
        in_specs=[pl.BlockSpec((c, RET_QKV), lambda n: (n, 0)), pl.BlockSpec((c, 128), lambda n: (n, 0)),
                  pl.BlockSpec((c, 128), lambda n: (n, 0)), pl.BlockSpec((RET_H, 1, 128), lambda n: (0, 0, 0))],
        out_specs=[pl.BlockSpec((c, RET_H * RET_DV), lambda n: (n, 0)),
                   pl.BlockSpec((RET_H, None, RET_DK, RET_DV), lambda n: (0, n, 0, 0))],
        out_shape=[S((t, RET_H * RET_DV), BF16), S((RET_H, nc, RET_DK, RET_DV), BF16)],
        scratch_shapes=[pltpu.VMEM((RET_H, RET_DK, RET_DV), F32)],
        compiler_params=_cp(dimension_semantics=("arbitrary",)),
    )(proj, cos, sin, lgam)


def _ret_scan_bwd(proj, cos, sin, lgam, do, states, dproj, name, side=None):
    t = proj.shape[0]
    c = RET_C
    nc = t // c
    half = RET_DK // 2

    def body(blk_ref, cos_ref, sin_ref, lg_ref, do_ref, st_ref, dp_in, dp_ref, dstate):
        n = nc - 1 - pl.program_id(0)

        @pl.when(pl.program_id(0) == 0)
        def _():
            dstate[...] = jnp.zeros_like(dstate)

        cs, sn = cos_ref[...], sin_ref[...]
        rows = n * c + lax.broadcasted_iota(jnp.int32, (c, 1), 0)
        keep = rows >= PAD

        def unrot(d):
            d1, d2 = d[:, :half], d[:, half:]
            return jnp.concatenate([d1 * cs + d2 * sn, d2 * cs - d1 * sn], axis=1)

        for h in range(RET_H):
            qr, kr, v, dmat, dq, dk, dchunk = _ret_chunk(blk_ref, cos_ref, sin_ref, lg_ref[h, :, 0:1], h)
            qb, kb = qr.astype(BF16), kr.astype(BF16)
            dob = do_ref[:, h * RET_DV:(h + 1) * RET_DV]
            sp = st_ref[h]
            ds = dstate[h]
            dsb = ds.astype(BF16)
            p = (_dot_nt(qb, kb) * dmat).astype(BF16)
            dvv = _dot_tn(p, dob) + _dot((kr * dk).astype(BF16), dsb)
            dp = (_dot_nt(dob, v) * dmat).astype(BF16)
            dqr = _dot(dp, kb) + _dot_nt(dob, sp) * dq
            dkr = (_dot_tn(dp, qb) + _dot_nt(v, dsb) * dk) * (RET_DK ** -0.5)
            dstate[h] = ds * dchunk + _dot_tn((qr * dq).astype(BF16), dob)
            oq, ok, ov = h * RET_DK, RET_H * RET_DK + h * RET_DK, 2 * RET_H * RET_DK + h * RET_DV
            dp_ref[:, oq:oq + RET_DK] = jnp.where(keep, unrot(dqr), 0.0).astype(BF16)
            dp_ref[:, ok:ok + RET_DK] = jnp.where(keep, unrot(dkr), 0.0).astype(BF16)
            dp_ref[:, ov:ov + RET_DV] = jnp.where(keep, dvv, 0.0).astype(BF16)

    return _call(
        body, name=name, grid=(nc,), side=side, aliases={6: 0},
        in_specs=[pl.BlockSpec((c, RET_QKV), lambda n: (nc - 1 - n, 0)), pl.BlockSpec((c, 128), lambda n: (nc - 1 - n, 0)),
                  pl.BlockSpec((c, 128), lambda n: (nc - 1 - n, 0)), pl.BlockSpec((RET_H, 1, 128), lambda n: (0, 0, 0)),
                  pl.BlockSpec((c, RET_H * RET_DV), lambda n: (nc - 1 - n, 0)),
                  pl.BlockSpec((RET_H, None, RET_DK, RET_DV), lambda n: (0, nc - 1 - n, 0, 0)), ANY],
        out_specs=[pl.BlockSpec((c, RET_QKV), lambda n: (nc - 1 - n, 0))],
        out_shape=[S((t, dproj.shape[1]), BF16)],
        scratch_shapes=[pltpu.VMEM((RET_H, RET_DK, RET_DV), F32)],
        operands=[proj, cos, sin, lgam, do, states, dproj])


def _split3(x):
    hi = x.astype(BF16)
    r1 = x - hi.astype(F32)
    mid = r1.astype(BF16)
    lo = (r1 - mid.astype(F32)).astype(BF16)
    return hi, mid, lo


def _gla_chunk(blk_ref, z_ref, wg_ref, bg_ref, n, h):
    c = CHUNK
    oq, ok, ov = h * GLA_DK, GLA_H * GLA_DK + h * GLA_DK, 2 * GLA_H * GLA_DK + h * GLA_DV
    q = blk_ref[:, oq:oq + GLA_DK].astype(F32) * (GLA_DK ** -0.5)
    k = blk_ref[:, ok:ok + GLA_DK].astype(F32)
    v = blk_ref[:, ov:ov + GLA_DV]
    hs = slice(h * GLA_DK, (h + 1) * GLA_DK)
    u = _dot(z_ref[...], wg_ref[:, hs]) + bg_ref[:, hs]
    la = (jnp.minimum(u, 0.0) - jnp.log(1.0 + jnp.exp(-jnp.abs(u)))) * (1.0 / GLA_TAU)
    rows = n * c + lax.broadcasted_iota(jnp.int32, (c, 1), 0)
    keep = rows >= PAD
    la = jnp.where(keep, la, 0.0)
    ii = lax.broadcasted_iota(jnp.int32, (c, c), 0)
    jj = lax.broadcasted_iota(jnp.int32, (c, c), 1)
    tril = (ii >= jj).astype(BF16)
    hi, mid, lo = _split3(la)
    b = _dot(tril, hi) + _dot(tril, mid) + _dot(tril, lo)
    return q, k, v, u, b, keep


def _gla_intra(qs, ks, bs, a_ref):
    c = CHUNK
    nh = len(qs)
    col = lax.broadcasted_iota(jnp.int32, (1, c), 1)
    rowi = lax.broadcasted_iota(jnp.int32, (SUB, 1), 0)
    for blk in range(c // SUB):
        r = slice(SUB * blk, SUB * (blk + 1))
        arows = []
        for h in range(nh):
            q, k, b = qs[h], ks[h], bs[h]
            if blk > 0:
                bprev = b[SUB * blk - 1:SUB * blk]
                qe = q[r] * jnp.exp(b[r] - bprev)
                kt = k * jnp.exp(jnp.minimum(bprev - b, 0.0))
                arows.append(jnp.where(col < SUB * blk, _dot_nt(qe.astype(BF16), kt.astype(BF16)), 0.0))
            else:
                arows.append(jnp.zeros((SUB, c), F32))
        for j in range(SUB):
            for h in range(nh):
                b_i = bs[h][r]
                e = jnp.exp(b_i - b_i[j:j + 1])
                a = jnp.sum(qs[h][r] * ks[h][r][j:j + 1] * e, axis=1, keepdims=True)
                arows[h] = jnp.where(col == SUB * blk + j, a, arows[h])
        for h in range(nh):
            a_ref[h, r, :] = jnp.where(col - SUB * blk <= rowi, arows[h], 0.0)


def _gla_scan_fwd(proj, wgp, bg, name):
    t = proj.shape[0]
    c = CHUNK
    nc = t // c
    heads = range(GLA_H)

    def body(blk_ref, z_ref, wg_ref, bg_ref, o_ref, st_ref, state, a_ref):
        n = pl.program_id(0)

        @pl.when(n == 0)
        def _():
            state[...] = jnp.zeros_like(state)

        qs, ks, vs, us, bs, keeps = zip(*[_gla_chunk(blk_ref, z_ref, wg_ref, bg_ref, n, h) for h in heads])
        _gla_intra(qs, ks, bs, a_ref)
        for h in heads:
            q, k, v, b = qs[h], ks[h], vs[h], bs[h]
            sp = state[h]
            st_ref[h] = sp.astype(BF16)
            o = _dot(a_ref[h].astype(BF16), v) + _dot_nt((q * jnp.exp(b)).astype(BF16), sp.astype(BF16))
            o_ref[:, h * GLA_DV:(h + 1) * GLA_DV] = o.astype(BF16)
            bc = b[c - 1:c]
            state[h] = sp * jnp.exp(bc) + _dot_tn(v, (k * jnp.exp(bc - b)).astype(BF16))

    return pl.pallas_call(
        body, name=name, grid=(nc,),
        in_specs=[pl.BlockSpec((c, GLA_QKV), lambda n: (n, 0)), pl.BlockSpec((c, 128), lambda n: (n, GLA_ZBLK)),
                  pl.BlockSpec((128, GLA_H * GLA_DK), lambda n: (0, 0)), pl.BlockSpec((1, GLA_H * GLA_DK), lambda n: (0, 0))],
        out_specs=[pl.BlockSpec((c, GLA_H * GLA_DV), lambda n: (n, 0)),
                   pl.BlockSpec((GLA_H, None, GLA_DV, GLA_DK), lambda n: (0, n, 0, 0))],
        out_shape=[S((t, GLA_H * GLA_DV), BF16), S((GLA_H, nc, GLA_DV, GLA_DK), BF16)],
        scratch_shapes=[pltpu.VMEM((GLA_H, GLA_DV, GLA_DK), F32), pltpu.VMEM((GLA_H, c, c), F32)],
        compiler_params=_cp(dimension_semantics=("arbitrary",)),
    )(proj, proj, wgp, bg)


def _gla_scan_bwd(proj, wgp, bg, do, states, dproj, name):
    t = proj.shape[0]
    c = CHUNK
    nc = t // c
    heads = range(GLA_H)

    def body(blk_ref, z_ref, wg_ref, bg_ref, do_ref, st_ref, dp_in, dp_ref, du_ref, dstate, a_ref, dq_ref, dkd_ref):
        n = nc - 1 - pl.program_id(0)

        @pl.when(pl.program_id(0) == 0)
        def _():
            dstate[...] = jnp.zeros_like(dstate)

        qs, ks, vs, us, bs, keeps = zip(*[_gla_chunk(blk_ref, z_ref, wg_ref, bg_ref, n, h) for h in heads])
        _gla_intra(qs, ks, bs, a_ref)
        ii = lax.broadcasted_iota(jnp.int32, (c, c), 0)
        jj = lax.broadcasted_iota(jnp.int32, (c, c), 1)
        col = lax.broadcasted_iota(jnp.int32, (1, c), 1)
        rowi = lax.broadcasted_iota(jnp.int32, (SUB, 1), 0)
        rowc = lax.broadcasted_iota(jnp.int32, (c, 1), 0)
        das, dvs, dq_inters, dk_states, extras, dks = [], [], [], [], [], []
        for h in heads:
            q, k, v, b = qs[h], ks[h], vs[h], bs[h]
            ab = a_ref[h].astype(BF16)
            dob = do_ref[:, h * GLA_DV:(h + 1) * GLA_DV]
            sp = st_ref[h]
            ds = dstate[h]
            dsb = ds.astype(BF16)
            bc = b[c - 1:c]
            eb = jnp.exp(b)
            ebc = jnp.exp(bc - b)
            ec = jnp.exp(bc)
            qb = (q * eb).astype(BF16)
            kb = (k * ebc).astype(BF16)
            dvs.append(_dot_tn(ab, dob) + _dot_nt(kb, dsb))
            das.append(jnp.where(ii >= jj, _dot_nt(dob, v), 0.0))
            dq_inters.append(_dot(dob, sp) * eb)
            dk_state = _dot(v, dsb) * ebc
            dk_states.append(dk_state)
            extras.append(jnp.sum(k * dk_state, axis=0, keepdims=True)
                          + ec * jnp.sum(sp.astype(F32) * ds, axis=0, keepdims=True))
            dstate[h] = ds * ec + _dot_tn(dob, qb)
            dks.append(jnp.zeros((c, GLA_DK), F32))

        for blk in range(c // SUB):
            r = slice(SUB * blk, SUB * (blk + 1))
            dq_is, dkds = [], []
            for h in heads:
                q, k, b = qs[h], ks[h], bs[h]
                if blk > 0:
                    bprev = b[SUB * blk - 1:SUB * blk]
                    e_i = jnp.exp(b[r] - bprev)
                    ek = jnp.exp(jnp.minimum(bprev - b, 0.0))
                    daoff = jnp.where(col < SUB * blk, das[h][r], 0.0).astype(BF16)
                    dq_is.append(_dot(daoff, (k * ek).astype(BF16)) * e_i)
                    dks[h] = dks[h] + _dot_tn(daoff, (q[r] * e_i).astype(BF16)) * ek
                else:
                    dq_is.append(jnp.zeros((SUB, GLA_DK), F32))
                dkds.append(jnp.zeros((SUB, GLA_DK), F32))
            for j in range(SUB):
                for h in heads:
                    b_i = bs[h][r]
                    e = jnp.where(rowi >= j, jnp.exp(b_i - b_i[j:j + 1]), 0.0)
                    dacol = jnp.sum(jnp.where(col == SUB * blk + j, das[h][r], 0.0), axis=1, keepdims=True)
                    tt = dacol * e
                    dq_is[h] = dq_is[h] + tt * ks[h][r][j:j + 1]
                    dkds[h] = jnp.where(rowi == j, jnp.sum(tt * qs[h][r], axis=0, keepdims=True), dkds[h])
            for h in heads:
                dq_ref[h, r, :] = dq_is[h]
                dkd_ref[h, r, :] = dkds[h]

        for h in heads:
            q, k, b, u, keep = qs[h], ks[h], bs[h], us[h], keeps[h]
            dq = dq_ref[h] + dq_inters[h]
            dk = dks[h] + dkd_ref[h] + dk_states[h]
            db = q * dq - k * dk + jnp.where(rowc == c - 1, extras[h], 0.0)
            triu = (ii <= jj).astype(BF16)
            hi, mid, lo = _split3(db)
            dla = _dot(triu, hi) + _dot(triu, mid) + _dot(triu, lo)
            du = jnp.where(keep, dla * (1.0 / GLA_TAU) / (1.0 + jnp.exp(u)), 0.0)
            du_ref[:, h * GLA_DK:(h + 1) * GLA_DK] = du.astype(BF16)
            oq, ok, ov = h * GLA_DK, GLA_H * GLA_DK + h * GLA_DK, 2 * GLA_H * GLA_DK + h * GLA_DV
            dp_ref[:, oq:oq + GLA_DK] = jnp.where(keep, dq * (GLA_DK ** -0.5), 0.0).astype(BF16)
            dp_ref[:, ok:ok + GLA_DK] = jnp.where(keep, dk, 0.0).astype(BF16)
            dp_ref[:, ov:ov + GLA_DV] = jnp.where(keep, dvs[h], 0.0).astype(BF16)

    nproj = dproj.shape[1]
    return pl.pallas_call(
        body, name=name, grid=(nc,),
        in_specs=[pl.BlockSpec((c, GLA_QKV), lambda n: (nc - 1 - n, 0)), pl.BlockSpec((c, 128), lambda n: (nc - 1 - n, GLA_ZBLK)),
                  pl.BlockSpec((128, GLA_H * GLA_DK), lambda n: (0, 0)), pl.BlockSpec((1, GLA_H * GLA_DK), lambda n: (0, 0)),
                  pl.BlockSpec((c, GLA_H * GLA_DV), lambda n: (nc - 1 - n, 0)),
                  pl.BlockSpec((GLA_H, None, GLA_DV, GLA_DK), lambda n: (0, nc - 1 - n, 0, 0)), ANY],
        out_specs=[pl.BlockSpec((c, GLA_QKV), lambda n: (nc - 1 - n, 0)),
                   pl.BlockSpec((c, GLA_H * GLA_DK), lambda n: (nc - 1 - n, 0))],
        out_shape=[S((t, nproj), BF16), S((t, GLA_H * GLA_DK), BF16)],
        input_output_aliases={6: 0},
        scratch_shapes=[pltpu.VMEM((GLA_H, GLA_DV, GLA_DK), F32), pltpu.VMEM((GLA_H, c, c), F32),
                        pltpu.VMEM((GLA_H, c, GLA_DK), F32), pltpu.VMEM((GLA_H, c, GLA_DK), F32)],
        compiler_params=_cp(dimension_semantics=("arbitrary",)),
    )(proj, proj, wgp, bg, do, states, dproj)


def _gla_gate_bwd(du, proj, wgp, dproj, name):
    t = du.shape[0]
    tm = _row_tile(t, 704)
    w = GLA_H * GLA_DK

    def body(du_ref, z_ref, wg_ref, dp_in, dp_ref, dwg_ref, dbg_ref):
        @pl.when(pl.program_id(0) == 0)
        def _():
            dwg_ref[...] = jnp.zeros_like(dwg_ref)
            dbg_ref[...] = jnp.zeros_like(dbg_ref)

        d = du_ref[...]
        dp_ref[...] = _dot_nt(d, wg_ref[...]).astype(BF16)
        dwg_ref[...] += _dot_tn(z_ref[...], d)
        dbg_ref[0:1, :] += jnp.sum(d.astype(F32), axis=0, keepdims=True)

    return pl.pallas_call(
        body, name=name, grid=(t // tm,),
        in_specs=[pl.BlockSpec((tm, w), lambda i: (i, 0)), pl.BlockSpec((tm, 128), lambda i: (i, GLA_ZBLK)),
                  pl.BlockSpec((128, w), lambda i: (0, 0)), ANY],
        out_specs=[pl.BlockSpec((tm, 128), lambda i: (i, GLA_ZBLK)), pl.BlockSpec((128, w), lambda i: (0, 0)),
                   pl.BlockSpec((8, w), lambda i: (0, 0))],
        out_shape=[S(dproj.shape, BF16), S((128, w), F32), S((8, w), F32)],
        input_output_aliases={3: 0},
        compiler_params=_cp(dimension_semantics=("arbitrary",)),
    )(du, proj, wgp, dproj)


def _final_loss(h, gain, target, name):
    t = h.shape[0]
    tm = _row_tile(t, 704)

    def body(h_ref, g_ref, t_ref, dh_ref, dgain_ref, loss_ref):
        i = pl.program_id(0)

        @pl.when(i == 0)
        def _():
            dgain_ref[...] = jnp.zeros_like(dgain_ref)
            loss_ref[...] = jnp.zeros_like(loss_ref)

        x = h_ref[...]
        gain = g_ref[...]
        r = lax.rsqrt(jnp.mean(x * x, axis=-1, keepdims=True) + EPS)
        xh = x * r
        rows = i * tm + lax.broadcasted_iota(jnp.int32, (tm, 1), 0)
        e = jnp.where(rows >= CHUNK, xh * gain - t_ref[...], 0.0)
        loss_ref[...] += 0.5 * jnp.sum(jnp.mean(e * e, axis=-1, keepdims=True), axis=0, keepdims=True)
        dy = e * (1.0 / D)
        dgain_ref[0:1, :] += jnp.sum(dy * xh, axis=0, keepdims=True)
        dxh = dy * gain
        dh_ref[...] = r * (dxh - xh * jnp.mean(dxh * xh, axis=-1, keepdims=True))

    row = pl.BlockSpec((tm, D), lambda i: (i, 0))
    return pl.pallas_call(
        body, name=name, grid=(t // tm,),
        in_specs=[row, pl.BlockSpec((1, D), lambda i: (0, 0)), row],
        out_specs=[row, pl.BlockSpec((8, D), lambda i: (0, 0)), pl.BlockSpec((8, 128), lambda i: (0, 0))],
        out_shape=[S((t, D), F32), S((8, D), F32), S((8, 128), F32)],
        compiler_params=_cp(dimension_semantics=("arbitrary",)),
    )(h, gain, target)


def _adam_math(w, g, m, v):
    m2 = ADAM_B1 * m + (1.0 - ADAM_B1) * g
    v2 = ADAM_B2 * v + (1.0 - ADAM_B2) * (g * g)
    m_hat = m2 / (1.0 - ADAM_B1 ** ADAM_STEP)
    v_hat = v2 / (1.0 - ADAM_B2 ** ADAM_STEP)
    delta = -ADAM_LR * (m_hat / (jnp.sqrt(v_hat) + ADAM_EPS) + ADAM_WD * w)
    return delta, m2, v2


def _adamw_reduce(recvs, w, m, v, name):
    nl, r, wd = w.shape
    tr = _row_tile(r, 256) if r % 16 == 0 else r
    nr = r // tr

    def body(*refs):
        rv_refs = refs[:nl]
        w_ref, m_ref, v_ref, g_ref, d_ref, m2_ref, v2_ref = refs[nl:]
        layer = pl.program_id(0)

        def total(rv_ref):
            g = rv_ref[0].astype(F32)
            for s in range(1, N_DEV):
                g = g + rv_ref[s].astype(F32)
            return g

        g = total(rv_refs[0])
        for k in range(1, nl):
            g = jnp.where(layer == k, total(rv_refs[k]), g)
        g_ref[...] = g
        d_ref[...], m2_ref[...], v2_ref[...] = _adam_math(w_ref[...], g, m_ref[...], v_ref[...])

    def rv_spec(k):
        return pl.BlockSpec((N_DEV, tr, wd), lambda l, i: (0, jnp.where(l == k, i, jnp.where(l < k, 0, nr - 1)), 0))

    row = pl.BlockSpec((None, tr, wd), lambda l, i: (l, i, 0))
    return pl.pallas_call(
        body, name=name, grid=(nl, nr),
        in_specs=[rv_spec(k) for k in range(nl)] + [row, row, row],
        out_specs=[row] * 4, out_shape=[S((nl, r, wd), F32)] * 4,
        compiler_params=_cp(dimension_semantics=("arbitrary", "arbitrary")),
    )(*recvs, w, m, v)


def _small_reduce(parts, name):
    _, r, wd = parts.shape

    def body(p_ref, o_ref):
        g = p_ref[0]
        for s in range(1, N_DEV):
            g = g + p_ref[s]
        o_ref[...] = g

    return pl.pallas_call(body, name=name, out_shape=S((r, wd), F32), compiler_params=_cp())(parts)


def _adamw_small(w, g, m, v, name):
    def body(w_ref, g_ref, m_ref, v_ref, d_ref, m2_ref, v2_ref):
        d_ref[...], m2_ref[...], v2_ref[...] = _adam_math(w_ref[...], g_ref[...], m_ref[...], v_ref[...])

    return pl.pallas_call(body, name=name, out_shape=[S(w.shape, F32)] * 3, compiler_params=_cp())(w, g, m, v)


def _unshard_cols(g):
    return jnp.transpose(g, (1, 0, 2)).reshape(g.shape[1], N_DEV * g.shape[2])


def _my_cols(full, width):
    me = 4 * lax.axis_index("x") + 2 * lax.axis_index("y") + lax.axis_index("c")
    return lax.dynamic_slice_in_dim(full, me * width, width, axis=1)


def kernel(x, meta_tokens, norm_ffn1, ffn1_w_in, ffn1_w_out, norm_mix, norm_ffn2, ffn2_w_in, ffn2_w_out, ret_w_in, ret_head_norm, ret_w_out, gla_w_in, gla_w_gate, gla_b_gate, gla_head_norm, gla_w_out, final_norm, loss_target, m_meta_tokens, m_norm_ffn1, m_ffn1_w_in, m_ffn1_w_out, m_norm_mix, m_norm_ffn2, m_ffn2_w_in, m_ffn2_w_out, m_ret_w_in, m_ret_head_norm, m_ret_w_out, m_gla_w_in, m_gla_w_gate, m_gla_b_gate, m_gla_head_norm, m_gla_w_out, m_final_norm, v_meta_tokens, v_norm_ffn1, v_ffn1_w_in, v_ffn1_w_out, v_norm_mix, v_norm_ffn2, v_ffn2_w_in, v_ffn2_w_out, v_ret_w_in, v_ret_head_norm, v_ret_w_out, v_gla_w_in, v_gla_w_gate, v_gla_b_gate, v_gla_head_norm, v_gla_w_out, v_final_norm):
    seq = x.shape[1]
    t = seq + CHUNK
    xs = x[0]
    target = loss_target[0]

    def ffn_w(f):
        w_in, w_out = (ffn1_w_in, ffn1_w_out) if f < 2 else (ffn2_w_in, ffn2_w_out)
        return [w_in[f % 2].astype(BF16), w_out[f % 2].astype(BF16)]

    small = jnp.concatenate([meta_tokens.reshape(-1), ret_head_norm.reshape(-1), gla_w_gate.reshape(-1),
                             gla_b_gate.reshape(-1), gla_head_norm.reshape(-1)])
    n_small = small.shape[0]
    small = jnp.pad(small, (0, 32 * 128 - n_small)).reshape(32, 128)
    sg, win0, wout0 = _run_side(_Gather([small] + ffn_w(0)), "ag_first")
    sg = sg.reshape(N_DEV, 32 * 128)

    def small_cols(off, rows, width):
        return jnp.transpose(sg[:, off:off + rows * width].reshape(N_DEV, rows, width), (1, 0, 2)).reshape(rows, N_DEV * width)

    off = 0
    meta_full = small_cols(off, N_META, D // N_DEV); off += N_META * (D // N_DEV)
    ret_hn = small_cols(off, RET_H, RET_DV // N_DEV).reshape(1, RET_H * RET_DV); off += RET_H * RET_DV // N_DEV
    wgate = small_cols(off, GLA_RANK, GLA_H * GLA_DK // N_DEV); off += GLA_RANK * GLA_H * GLA_DK // N_DEV
    bgate = small_cols(off, 1, GLA_H * GLA_DK // N_DEV); off += GLA_H * GLA_DK // N_DEV
    gla_hn = small_cols(off, GLA_H, GLA_DV // N_DEV).reshape(1, GLA_H * GLA_DV)
    wgp = jnp.pad(wgate, ((0, 128 - GLA_RANK), (0, 0))).astype(BF16)

    cos, sin = _rope_tables(t)
    lgam = _ret_consts()

    h0 = jnp.concatenate([jnp.zeros((PAD, D), F32), meta_full, xs], axis=0)
    g1 = [norm_ffn1[i:i + 1] for i in range(2)]
    gm = [norm_mix[i:i + 1] for i in range(2)]
    g2 = [norm_ffn2[i:i + 1] for i in range(2)]

    (h1, xn_a0, pg_a0, pu_a0), (ret_win_g, ret_wout_g) = _ffn_fwd(
        h0, g1[0], win0, wout0, "ffn1_l0_fwd", side=_Gather([ret_w_in[0].astype(BF16), ret_w_out[0].astype(BF16)]))
    ret_win = ret_win_g
    ret_wout = ret_wout_g.reshape(RET_H * RET_DV, D)
    (rproj, rhn), (win2, wout2) = _norm_mm(h1, gm[0], ret_win, ret_win.shape[2], "ret_proj_fwd", side=_Gather(ffn_w(2)))
    ro, rstates = _ret_scan_fwd(rproj, cos, sin, lgam, "ret_scan_fwd")
    h2, rog = _post_fwd(ro, rproj, ret_hn, ret_wout, h1, RET_H, RET_DV, "ret_post_fwd")
    (h3, xn_b0, pg_b0, pu_b0), (win1, wout1) = _ffn_fwd(h2, g2[0], win2, wout2, "ffn2_l0_fwd", side=_Gather(ffn_w(1)))
    (h4, xn_a1, pg_a1, pu_a1), (gla_win_g, gla_wout_g) = _ffn_fwd(
        h3, g1[1], win1, wout1, "ffn1_l1_fwd", side=_Gather([gla_w_in[0].astype(BF16), gla_w_out[0].astype(BF16)]))
    gla_win = _unshard_cols(gla_win_g)
    gla_win = jnp.pad(gla_win, ((0, 0), (0, GLA_N - gla_win.shape[1])))
    gla_wout = gla_wout_g.reshape(GLA_H * GLA_DV, D)
    (gproj, ghn), (win3, wout3) = _norm_mm(h4, gm[1], gla_win, 640, "gla_proj_fwd", side=_Gather(ffn_w(3)))
    go, gstates = _gla_scan_fwd(gproj, wgp, bgate, "gla_scan_fwd")
    h5, gog = _post_fwd(go, gproj, gla_hn, gla_wout, h4, GLA_H, GLA_DV, "gla_post_fwd")
    (h6, xn_b1, pg_b1, pu_b1), _ = _ffn_fwd(h5, g2[1], win3, wout3, "ffn2_l1_fwd")

    dh, dfinal, loss_blk = _final_loss(h6, final_norm.reshape(1, D), jnp.pad(target, ((CHUNK, 0), (0, 0))), "final_loss")
    loss = lax.psum(loss_blk[0, 0], ("x", "y", "c"))

    def ffn_back(dh, h_in, xn, gain, pg, pu, win, wout, tag, side=None, dw_side=None):
        (dh_in, dob, dpg, dpu, act, dgain), got = _ffn_bwd(dh, h_in, gain, pg, pu, win, wout, tag + "_bwd", side=side)
        dwout = _mm_tn(act, dob[None], D, tag + "_dw_out").reshape(N_DEV, FF_SHARD // 2, D)
        if dw_side == "own_dw_out":
            dw_side = _Exchange([dwout])
        (dwin,), dw_got = _ffn_dw_in(xn, dpg, dpu, tag + "_dw_in", side=dw_side)
        return dh_in, [dwin, dwout], dgain[0], got, dw_got

    dh, dw_b1, dg2_1, _, _ = ffn_back(dh, h5, xn_b1, g2[1], pg_b1, pu_b1, win3, wout3, "ffn2_l1")

    (gdo, gdproj, gdhb, dghn), _ = _post_bwd(dh, go, gproj, gla_hn, gla_wout, GLA_H, GLA_DV, GLA_N, "gla_post_bwd")
    d_gla_wout = _mm_tn(gog[None], gdhb[None], D, "gla_dw_out").reshape(N_DEV, GLA_H * GLA_DV // N_DEV, D)
    gdproj, gdu = _gla_scan_bwd(gproj, wgp, bgate, gdo, gstates, gdproj, "gla_scan_bwd")
    gdproj, dwg, dbg = _gla_gate_bwd(gdu, gproj, wgp, gdproj, "gla_gate_bwd")
    d_gla_win = _mm_tn(gdproj[None], ghn[None], D, "gla_dw_in", tm=640)[0]
    (dh, dgm_1), _ = _proj_bwd(gdproj, gla_win, dh, h4, gm[1], 640, "gla_proj_bwd")
    n_gla_in = 2 * GLA_H * GLA_DK + 2 * GLA_H * GLA_DV + GLA_RANK
    d_gla_win = d_gla_win[:n_gla_in].reshape(N_DEV, n_gla_in // N_DEV, D)

    dh, dw_a1, dg1_1, rv_b1, rv_gla = ffn_back(dh, h3, xn_a1, g1[1], pg_a1, pu_a1, win1, wout1, "ffn1_l1",
                                               side=_Exchange(dw_b1), dw_side=_Exchange([d_gla_win, d_gla_wout]))
    dh, dw_b0, dg2_0, rv_a1, _ = ffn_back(dh, h2, xn_b0, g2[0], pg_b0, pu_b0, win2, wout2, "ffn2_l0", side=_Exchange(dw_a1))

    (rdo, rdproj, rdhb, drhn), rv_b0_out = _post_bwd(dh, ro, rproj, ret_hn, ret_wout, RET_H, RET_DV, 6 * D, "ret_post_bwd",
                                                     side=_Exchange(dw_b0[1:]))
    d_ret_wout = _mm_tn(rog[None], rdhb[None], D, "ret_dw_out", rows=DW_ROWS // 2).reshape(N_DEV, RET_H * RET_DV // N_DEV, D)
    (rdproj,), rv_b0_in = _ret_scan_bwd(rproj, cos, sin, lgam, rdo, rstates, rdproj, "ret_scan_bwd", side=_Exchange(dw_b0[:1]))
    rv_b0 = rv_b0_in + rv_b0_out
    d_ret_win = _mm_tn(rhn[None], rdproj[None], ret_win.shape[2], "ret_dw_in", shard_out=True)
    (dh, dgm_0), rv_ret_out = _proj_bwd(rdproj, ret_win, dh, h1, gm[0], ret_win.shape[2], "ret_proj_bwd", side=_Exchange([d_ret_wout]))

    dh, dw_a0, dg1_0, rv_ret_in, rv_a0_out = ffn_back(dh, h0, xn_a0, g1[0], pg_a0, pu_a0, win0, wout0, "ffn1_l0",
                                                      side=_Exchange([d_ret_win]), dw_side="own_dw_out")
    rv_ret = rv_ret_in + rv_ret_out
    rv_a0 = _run_side(_Exchange(dw_a0[:1]), "xchg_last") + rv_a0_out
    grad_x = dh[CHUNK:][None]

    def adam_t(recvs, w, m, v, tag):
        outs = _adamw_reduce(recvs, *(jnp.swapaxes(a, 1, 2) for a in (w, m, v)), tag)
        return [jnp.swapaxes(o, 1, 2) for o in outs]

    u_ffn1_in = adam_t([rv_a0[0], rv_a1[0]], ffn1_w_in, m_ffn1_w_in, v_ffn1_w_in, "adam_ffn1_w_in")
    u_ffn2_in = adam_t([rv_b0[0], rv_b1[0]], ffn2_w_in, m_ffn2_w_in, v_ffn2_w_in, "adam_ffn2_w_in")
    u_ffn1_out = _adamw_reduce([rv_a0[1], rv_a1[1]], ffn1_w_out, m_ffn1_w_out, v_ffn1_w_out, "adam_ffn1_w_out")
    u_ffn2_out = _adamw_reduce([rv_b0[1], rv_b1[1]], ffn2_w_out, m_ffn2_w_out, v_ffn2_w_out, "adam_ffn2_w_out")
    u_ret_in = _adamw_reduce([rv_ret[0]], ret_w_in, m_ret_w_in, v_ret_w_in, "adam_ret_w_in")
    u_ret_out = _adamw_reduce([rv_ret[1]], ret_w_out, m_ret_w_out, v_ret_w_out, "adam_ret_w_out")
    u_gla_in = adam_t([rv_gla[0]], gla_w_in, m_gla_w_in, v_gla_w_in, "adam_gla_w_in")
    u_gla_out = _adamw_reduce([rv_gla[1]], gla_w_out, m_gla_w_out, v_gla_w_out, "adam_gla_w_out")

    dmeta = dh[PAD:CHUNK]
    parts = jnp.concatenate([
        dg1_0, dg1_1, dgm_0[0], dgm_1[0], dg2_0, dg2_1, dfinal[0], dmeta.reshape(-1), drhn[0], dwg[:GLA_RANK].reshape(-1),
        dbg[0], dghn[0]])
    n_parts = parts.shape[0]
    rows = -(-n_parts // D)
    rows = -(-rows // 8) * 8
    parts = jnp.pad(parts, (0, rows * D - n_parts)).reshape(rows, D)
    tot = _small_reduce(_run_side(_Gather([parts]), "ag_small_grads")[0], "small_grad_sum").reshape(-1)

    off = 0
    def take(nel):
        nonlocal off
        out = tot[off:off + nel]
        off += nel
        return out

    gr_norm_ffn1 = take(2 * D).reshape(2, D)
    gr_norm_mix = take(2 * D).reshape(2, D)
    gr_norm_ffn2 = take(2 * D).reshape(2, D)
    gr_final = take(D)
    gr_meta = _my_cols(take(N_META * D).reshape(N_META, D), D // N_DEV)
    gr_ret_hn = _my_cols(take(RET_H * RET_DV).reshape(RET_H, RET_DV), RET_DV // N_DEV)[None]
    gr_wgate = _my_cols(take(GLA_RANK * GLA_H * GLA_DK).reshape(GLA_RANK, GLA_H * GLA_DK), GLA_H * GLA_DK // N_DEV)[None]
    gr_bgate = _my_cols(take(GLA_H * GLA_DK).reshape(1, GLA_H * GLA_DK), GLA_H * GLA_DK // N_DEV)
    gr_gla_hn = _my_cols(take(GLA_H * GLA_DV).reshape(GLA_H, GLA_DV), GLA_DV // N_DEV)[None]

    small_w = [meta_tokens, norm_ffn1, norm_mix, norm_ffn2, ret_head_norm, gla_w_gate, gla_b_gate, gla_head_norm, final_norm]
    small_g = [gr_meta, gr_norm_ffn1, gr_norm_mix, gr_norm_ffn2, gr_ret_hn, gr_wgate, gr_bgate, gr_gla_hn, gr_final]
    small_m = [m_meta_tokens, m_norm_ffn1, m_norm_mix, m_norm_ffn2, m_ret_head_norm, m_gla_w_gate, m_gla_b_gate, m_gla_head_norm, m_final_norm]
    small_v = [v_meta_tokens, v_norm_ffn1, v_norm_mix, v_norm_ffn2, v_ret_head_norm, v_gla_w_gate, v_gla_b_gate, v_gla_head_norm, v_final_norm]

    def pack(arrs):
        flat = jnp.concatenate([a.reshape(-1) for a in arrs])
        n = flat.shape[0]
        r = -(-n // 128)
        r = -(-r // 8) * 8
        return jnp.pad(flat, (0, r * 128 - n), constant_values=1.0).reshape(r, 128)

    sd, sm, sv = _adamw_small(pack(small_w), pack(small_g), pack(small_m), pack(small_v), "adam_small")

    def unpack(buf):
        flat = buf.reshape(-1)
        outs, o = [], 0
        for a in small_w:
            outs.append(flat[o:o + a.size].reshape(a.shape))
            o += a.size
        return outs

    us_d, us_m, us_v = unpack(sd), unpack(sm), unpack(sv)

    def ordered(k, smalls):
        return (smalls[0], smalls[1], u_ffn1_in[k], u_ffn1_out[k], smalls[2], smalls[3], u_ffn2_in[k], u_ffn2_out[k],
                u_ret_in[k], smalls[4], u_ret_out[k], u_gla_in[k], smalls[5], smalls[6], smalls[7], u_gla_out[k], smalls[8])

    return (loss, grad_x, *ordered(0, small_g), *ordered(1, us_d), *ordered(2, us_m), *ordered(3, us_v))
```

```python
import functools
import math

import numpy as np
import jax
import jax.numpy as jnp
from jax import lax
from jax.experimental import pallas as pl
from jax.experimental.pallas import tpu as pltpu

F32 = jnp.float32
BF16 = jnp.bfloat16
S = jax.ShapeDtypeStruct
ANY = pl.BlockSpec(memory_space=pl.ANY)
MESH = pl.DeviceIdType.MESH

D = 1024
N_META = 16
CHUNK = 64
PAD = CHUNK - N_META
EPS = 1e-6
N_DEV = 8
FF_SHARD = 704
N_FF_CHUNK = 4
RET_H, RET_DK, RET_DV = 4, 256, 512
RET_QKV = RET_H * (2 * RET_DK + RET_DV)
RET_C = 192
GLA_H, GLA_DK, GLA_DV, GLA_RANK, GLA_TAU = 4, 128, 256, 16, 16.0
GLA_QKV = GLA_H * (2 * GLA_DK + GLA_DV)
GLA_N = 3200
GLA_ZBLK = 3072 // 128
SUB = 16
ROPE_BASE = 10000.0
ADAM_LR, ADAM_B1, ADAM_B2, ADAM_EPS, ADAM_WD, ADAM_STEP = 0.001, 0.9, 0.999, 1e-08, 0.01, 10
VMEM_LIMIT = 58 * 1024 * 1024
DW_ROWS = 2752


def _cp(**kw):
    return pltpu.CompilerParams(vmem_limit_bytes=VMEM_LIMIT, **kw)


def _row_tile(t, cap):
    best = 16
    for d in range(16, cap + 1, 16):
        if t % d == 0:
            best = d
    return best


def _sub_rows(tm, parts=2):
    units = tm // 16
    cuts = [16 * (units * p // parts) for p in range(parts + 1)]
    return [slice(a, b) for a, b in zip(cuts[:-1], cuts[1:]) if b > a]


def _dot(a, b):
    return jnp.dot(a, b, preferred_element_type=F32)


def _dot_nt(a, b):
    return lax.dot_general(a, b, (((1,), (1,)), ((), ())), preferred_element_type=F32)


def _dot_tn(a, b):
    return lax.dot_general(a, b, (((0,), (0,)), ((), ())), preferred_element_type=F32)


def _sigmoid(x):
    return pl.reciprocal(1.0 + jnp.exp(-x), approx=True)


def _rms_bwd(dxn, x, gain):
    r = lax.rsqrt(jnp.mean(x * x, axis=-1, keepdims=True) + EPS)
    xh = x * r
    dxh = dxn * gain
    dx = r * (dxh - xh * jnp.mean(dxh * xh, axis=-1, keepdims=True))
    return dx, jnp.sum(dxn * xh, axis=0, keepdims=True)


def _xyc():
    return lax.axis_index("x"), lax.axis_index("y"), lax.axis_index("c")


class _Gather:
    def __init__(self, xs):
        self.xs = list(xs)
        self.n = len(self.xs)

    def out_shape(self):
        return [S((N_DEV,) + a.shape, a.dtype) for a in self.xs]

    def scratch(self):
        return [pltpu.SemaphoreType.DMA((self.n, 7)), pltpu.SemaphoreType.DMA((self.n, 7)), pltpu.SemaphoreType.DMA((self.n,))]

    def phases(self, x_refs, out_refs, send_sems, recv_sems, local_sems):
        x, y, c = _xyc()
        me, sibling = (x, y, c), (x, y, 1 - c)
        chips = [(1 - x, y), (x, 1 - y), (1 - x, 1 - y)]

        def copy(t, k, block, to, src=None):
            px, py, pc = block
            dst = out_refs[t].at[4 * px + 2 * py + pc]
            return pltpu.make_async_remote_copy(
                src_ref=dst if src is None else src, dst_ref=dst,
                send_sem=send_sems.at[t, k], recv_sem=recv_sems.at[t, k], device_id=to, device_id_type=MESH)

        def own(t):
            return pltpu.make_async_copy(x_refs[t], out_refs[t].at[4 * x + 2 * y + c], local_sems.at[t])

        def first(t):
            return [copy(t, 0, me, sibling, src=x_refs[t])] + [
                copy(t, 1 + j, me, (*chip, c), src=x_refs[t]) for j, chip in enumerate(chips)]

        def passed(t):
            return [copy(t, 4 + j, (*chip, c), sibling) for j, chip in enumerate(chips)]

        def start():
            for t in range(self.n):
                own(t).start()
                for cp in first(t):
                    cp.start()

        def mid():
            for t in range(self.n):
                fw = passed(t)
                for j, chip in enumerate(chips):
                    copy(t, 1 + j, (*chip, c), me).wait_recv()
                    fw[j].start()

        def finish():
            for t in range(self.n):
                copy(t, 0, sibling, me).wait_recv()
                for j, chip in enumerate(chips):
                    copy(t, 4 + j, (*chip, 1 - c), me).wait_recv()
                for cp in first(t) + passed(t):
                    cp.wait_send()
                own(t).wait()

        return start, mid, finish


class _Exchange:
    def __init__(self, xs):
        self.xs = list(xs)
        self.n = len(self.xs)

    def out_shape(self):
        return [S(a.shape, a.dtype) for a in self.xs]

    def scratch(self):
        return [pltpu.SemaphoreType.DMA((self.n, 7)), pltpu.SemaphoreType.DMA((self.n, 7)), pltpu.SemaphoreType.DMA((self.n,))]

    def phases(self, g_refs, r_refs, send_sems, recv_sems, local_sems):
        x, y, c = _xyc()
        me = 4 * x + 2 * y + c

        def own(t):
            return pltpu.make_async_copy(g_refs[t].at[me], r_refs[t].at[me], local_sems.at[t])

        def send(t, m):
            px, py, pc = x ^ (m >> 2), y ^ ((m >> 1) & 1), c ^ (m & 1)
            return pltpu.make_async_remote_copy(
                src_ref=g_refs[t].at[4 * px + 2 * py + pc], dst_ref=r_refs[t].at[me],
                send_sem=send_sems.at[t, m - 1], recv_sem=recv_sems.at[t, m - 1],
                device_id=(px, py, pc), device_id_type=MESH)

        def arrival(t, m):
            peer = 4 * (x ^ (m >> 2)) + 2 * (y ^ ((m >> 1) & 1)) + (c ^ (m & 1))
            return pltpu.make_async_remote_copy(
                src_ref=g_refs[t].at[peer], dst_ref=r_refs[t].at[peer],
                send_sem=send_sems.at[t, m - 1], recv_sem=recv_sems.at[t, m - 1],
                device_id=(x, y, c), device_id_type=MESH)

        def start():
            for t in range(self.n):
                own(t).start()
            for m in range(1, N_DEV):
                for t in range(self.n):
                    send(t, m).start()

        def mid():
            pass

        def finish():
            for m in range(1, N_DEV):
                for t in range(self.n):
                    arrival(t, m).wait_recv()
            for m in range(1, N_DEV):
                for t in range(self.n):
                    send(t, m).wait_send()
            for t in range(self.n):
                own(t).wait()

        return start, mid, finish


def _run_side(side, name):
    n = side.n

    def body(*refs):
        start, mid, finish = side.phases(refs[:n], refs[n:2 * n], *refs[2 * n:])
        start()
        mid()
        finish()

    return list(pl.pallas_call(
        body, name=name, out_shape=side.out_shape(), in_specs=[ANY] * n, out_specs=[ANY] * n,
        scratch_shapes=side.scratch())(*side.xs))


def _grid_steps(grid):
    def ids():
        return [pl.program_id(a) for a in range(len(grid))]

    def first():
        return functools.reduce(jnp.logical_and, [i == 0 for i in ids()])

    def middle():
        i = ids()
        return functools.reduce(jnp.logical_and, [i[0] == (3 * grid[0]) // 4] + [j == 0 for j in i[1:]])

    def last():
        return functools.reduce(jnp.logical_and, [i == g - 1 for i, g in zip(ids(), grid)])

    return first, middle, last


def _call(body, *, name, grid, in_specs, out_specs, out_shape, scratch_shapes, operands, side=None, aliases=None):
    n_in, n_out, n_scr = len(in_specs), len(out_shape), len(scratch_shapes)
    full = body
    if side is not None:
        ns = side.n
        first, middle, last = _grid_steps(grid)

        def full(*refs):
            a = n_in
            ins, sins = refs[:a], refs[a:a + ns]
            a += ns
            outs, souts = refs[a:a + n_out], refs[a + n_out:a + n_out + ns]
            a += n_out + ns
            scr, sems = refs[a:a + n_scr], refs[a + n_scr:]
            start, mid, finish = side.phases(sins, souts, *sems)
            pl.when(first())(start)
            body(*ins, *outs, *scr)
            pl.when(middle())(mid)
            pl.when(last())(finish)

        in_specs = list(in_specs) + [ANY] * ns
        out_specs = list(out_specs) + [ANY] * ns
        out_shape = list(out_shape) + side.out_shape()
        scratch_shapes = list(scratch_shapes) + side.scratch()
        operands = list(operands) + side.xs
    outs = pl.pallas_call(
        full, name=name, grid=grid, in_specs=list(in_specs), out_specs=list(out_specs), out_shape=list(out_shape),
        scratch_shapes=list(scratch_shapes), input_output_aliases=aliases or {},
        compiler_params=_cp(dimension_semantics=("arbitrary",) * len(grid)),
    )(*operands)
    return list(outs[:n_out]), list(outs[n_out:])


def _ffn_fwd(h, gain, win, wout, name, side=None):
    t = h.shape[0]
    tm = _row_tile(t, 704)
    nt = t // tm

    def body(h_ref, g_ref, wg_ref, wu_ref, wo_ref, hn_ref, xn_ref, pg_ref, pu_ref, acc):
        c = pl.program_id(1)

        @pl.when(c == 0)
        def _():
            x = h_ref[...]
            r = lax.rsqrt(jnp.mean(x * x, axis=-1, keepdims=True) + EPS)
            xn_ref[...] = (x * r * g_ref[...]).astype(BF16)
            acc[...] = jnp.zeros_like(acc)

        wo = wo_ref[...].reshape(FF_SHARD, D)
        subs = _sub_rows(tm)
        gus = [(_dot(xn_ref[r, :], wg_ref[...]), _dot(xn_ref[r, :], wu_ref[...])) for r in subs]
        for r, (g, u) in zip(subs, gus):
            pg_ref[r, :] = g.astype(BF16)
            pu_ref[r, :] = u.astype(BF16)
            act = (g * _sigmoid(g) * u).astype(BF16)
            acc[r, :] += _dot(act, wo)

        @pl.when(c == N_FF_CHUNK - 1)
        def _():
            hn_ref[...] = h_ref[...] + 0.5 * acc[...]

    return _call(
        body, name=name, grid=(nt, N_FF_CHUNK), side=side,
        in_specs=[
            pl.BlockSpec((tm, D), lambda i, c: (i, 0)),
            pl.BlockSpec((1, D), lambda i, c: (0, 0)),
            pl.BlockSpec((None, D, FF_SHARD), lambda i, c: (c, 0, 0)),
            pl.BlockSpec((None, D, FF_SHARD), lambda i, c: (c + N_FF_CHUNK, 0, 0)),
            pl.BlockSpec((2, FF_SHARD // 2, D), lambda i, c: (c, 0, 0)),
        ],
        out_specs=[
            pl.BlockSpec((tm, D), lambda i, c: (i, 0)),
            pl.BlockSpec((tm, D), lambda i, c: (i, 0)),
            pl.BlockSpec((None, tm, FF_SHARD), lambda i, c: (c, i, 0)),
            pl.BlockSpec((None, tm, FF_SHARD), lambda i, c: (c, i, 0)),
        ],
        out_shape=[S((t, D), F32), S((t, D), BF16), S((N_FF_CHUNK, t, FF_SHARD), BF16), S((N_FF_CHUNK, t, FF_SHARD), BF16)],
        scratch_shapes=[pltpu.VMEM((tm, D), F32)],
        operands=[h, gain, win, win, wout])


def _ffn_bwd(dh, h, gain, pg, pu, win, wout, name, side=None):
    t = h.shape[0]
    tm = _row_tile(t, 704)
    nt = t // tm

    def body(dh_ref, h_ref, g_ref, pg_ref, pu_ref, wg_ref, wu_ref, wo_ref,
             dhi_ref, dob_ref, dpg_ref, dpu_ref, act_ref, dgain_ref, acc):
        i, c = pl.program_id(0), pl.program_id(1)

        @pl.when(c == 0)
        def _():
            dob_ref[...] = (0.5 * dh_ref[...]).astype(BF16)
            acc[...] = jnp.zeros_like(acc)

        @pl.when((i == 0) & (c == 0))
        def _():
            dgain_ref[...] = jnp.zeros_like(dgain_ref)

        wo = wo_ref[...].reshape(FF_SHARD, D)
        subs = _sub_rows(tm)
        dacts = [_dot_nt(dob_ref[r, :], wo) for r in subs]
        for r, dact in zip(subs, dacts):
            g = pg_ref[r, :].astype(F32)
            u = pu_ref[r, :].astype(F32)
            s = _sigmoid(g)
            sl = g * s
            act_ref[r, :] = (sl * u).astype(BF16)
            dg = (dact * u * (s * (1.0 + g * (1.0 - s)))).astype(BF16)
            du = (dact * sl).astype(BF16)
            dpg_ref[r, :] = dg
            dpu_ref[r, :] = du
            acc[r, :] += _dot_nt(dg, wg_ref[...]) + _dot_nt(du, wu_ref[...])

        @pl.when(c == N_FF_CHUNK - 1)
        def _():
            dx, dgn = _rms_bwd(acc[...], h_ref[...], g_ref[...])
            dhi_ref[...] = dh_ref[...] + dx
            dgain_ref[0:1, :] += dgn

    blk = pl.BlockSpec((None, tm, FF_SHARD), lambda i, c: (c, i, 0))
    row = pl.BlockSpec((tm, D), lambda i, c: (i, 0))
    return _call(
        body, name=name, grid=(nt, N_FF_CHUNK), side=side,
        in_specs=[
            row, row, pl.BlockSpec((1, D), lambda i, c: (0, 0)), blk, blk,
            pl.BlockSpec((None, D, FF_SHARD), lambda i, c: (c, 0, 0)),
            pl.BlockSpec((None, D, FF_SHARD), lambda i, c: (c + N_FF_CHUNK, 0, 0)),
            pl.BlockSpec((2, FF_SHARD // 2, D), lambda i, c: (c, 0, 0)),
        ],
        out_specs=[row, row, blk, blk, blk, pl.BlockSpec((8, D), lambda i, c: (0, 0))],
        out_shape=[S((t, D), F32), S((t, D), BF16)] + [S((N_FF_CHUNK, t, FF_SHARD), BF16)] * 3 + [S((8, D), F32)],
        scratch_shapes=[pltpu.VMEM((tm, D), F32)],
        operands=[dh, h, gain, pg, pu, win, win, wout])


def _ffn_dw_in(xn, dpg, dpu, name, side=None):
    t = xn.shape[0]
    tk = _row_tile(t, DW_ROWS)
    nk = t // tk

    def body(a_ref, bg_ref, bu_ref, o_ref, acc):
        c, k = pl.program_id(0), pl.program_id(1)

        @pl.when(k == 0)
        def _():
            acc[...] = jnp.zeros_like(acc)

        @pl.when(c < N_FF_CHUNK)
        def _():
            acc[...] += _dot_tn(bg_ref[...], a_ref[...])

        @pl.when(c >= N_FF_CHUNK)
        def _():
            acc[...] += _dot_tn(bu_ref[...], a_ref[...])

        @pl.when(k == nk - 1)
        def _():
            o_ref[...] = acc[...].astype(BF16)

    return _call(
        body, name=name, grid=(2 * N_FF_CHUNK, nk), side=side,
        in_specs=[
            pl.BlockSpec((tk, D), lambda c, k: (k, 0)),
            pl.BlockSpec((None, tk, FF_SHARD), lambda c, k: (jnp.minimum(c, N_FF_CHUNK - 1), k, 0)),
            pl.BlockSpec((None, tk, FF_SHARD), lambda c, k: (jnp.maximum(c - N_FF_CHUNK, 0), k, 0)),
        ],
        out_specs=[pl.BlockSpec((None, FF_SHARD, D), lambda c, k: (c, 0, 0))],
        out_shape=[S((2 * N_FF_CHUNK, FF_SHARD, D), BF16)],
        scratch_shapes=[pltpu.VMEM((FF_SHARD, D), F32)],
        operands=[xn, dpg, dpu])


def _mm_tn(a, b, tn, name, tm=None, rows=DW_ROWS, shard_out=False):
    ca, t, m = a.shape
    cb, _, n = b.shape
    nc = max(ca, cb)
    tm = m if tm is None else tm
    tk = _row_tile(t, rows)
    nk = t // tk

    def body(a_ref, b_ref, o_ref, acc):
        k = pl.program_id(3)

        @pl.when(k == 0)
        def _():
            acc[...] = jnp.zeros_like(acc)

        acc[...] += _dot_tn(a_ref[...], b_ref[...])

        @pl.when(k == nk - 1)
        def _():
            o_ref[...] = acc[...].astype(BF16)

    if shard_out:
        out_spec = pl.BlockSpec((None, tm, tn), lambda c, i, j, k: (j, 0, 0))
        out_shape = S((n // tn, m, tn), BF16)
    else:
        out_spec = pl.BlockSpec((None, tm, tn), lambda c, i, j, k: (c, i, j))
        out_shape = S((nc, m, n), BF16)
    return pl.pallas_call(
        body, name=name, grid=(nc, m // tm, n // tn, nk),
        in_specs=[
            pl.BlockSpec((None, tk, tm), (lambda c, i, j, k: (c, k, i)) if ca > 1 else (lambda c, i, j, k: (0, k, i))),
            pl.BlockSpec((None, tk, tn), (lambda c, i, j, k: (c, k, j)) if cb > 1 else (lambda c, i, j, k: (0, k, j))),
        ],
        out_specs=out_spec, out_shape=out_shape,
        scratch_shapes=[pltpu.VMEM((tm, tn), F32)],
        compiler_params=_cp(dimension_semantics=("arbitrary",) * 4),
    )(a, b)


def _norm_mm(h, gain, w, tn, name, side=None):
    t = h.shape[0]
    n = w.shape[-1] if w.ndim == 2 else w.shape[0] * w.shape[2]
    tm = _row_tile(t, 704)
    kb = 1 if w.ndim == 2 else tn // w.shape[2]
    w_spec = (pl.BlockSpec((D, tn), lambda i, j: (0, j)) if w.ndim == 2
              else pl.BlockSpec((kb, D, tn // kb), lambda i, j: (j, 0, 0)))

    def body(h_ref, g_ref, w_ref, o_ref, xn_ref):
        @pl.when(pl.program_id(1) == 0)
        def _():
            x = h_ref[...]
            r = lax.rsqrt(jnp.mean(x * x, axis=-1, keepdims=True) + EPS)
            xn_ref[...] = (x * r * g_ref[...]).astype(BF16)

        if w.ndim == 2:
            o_ref[...] = _dot(xn_ref[...], w_ref[...]).astype(BF16)
        else:
            for b in range(kb):
                o_ref[:, b * (tn // kb):(b + 1) * (tn // kb)] = _dot(xn_ref[...], w_ref[b]).astype(BF16)

    return _call(
        body, name=name, grid=(t // tm, n // tn), side=side,
        in_specs=[pl.BlockSpec((tm, D), lambda i, j: (i, 0)), pl.BlockSpec((1, D), lambda i, j: (0, 0)), w_spec],
        out_specs=[pl.BlockSpec((tm, tn), lambda i, j: (i, j)), pl.BlockSpec((tm, D), lambda i, j: (i, 0))],
        out_shape=[S((t, n), BF16), S((t, D), BF16)], scratch_shapes=[],
        operands=[h, gain, w])


def _proj_bwd(dproj, w, dh, h, gain, tk, name, side=None):
    t, n = dproj.shape
    tm = _row_tile(t, 704)
    nk = n // tk
    kb = 1 if w.ndim == 2 else tk // w.shape[2]
    w_spec = (pl.BlockSpec((D, tk), lambda i, k: (0, k)) if w.ndim == 2
              else pl.BlockSpec((kb, D, tk // kb), lambda i, k: (k, 0, 0)))

    def body(dp_ref, w_ref, dh_ref, h_ref, g_ref, dhi_ref, dgain_ref, acc):
        i, k = pl.program_id(0), pl.program_id(1)

        @pl.when(k == 0)
        def _():
            acc[...] = jnp.zeros_like(acc)

        @pl.when((i == 0) & (k == 0))
        def _():
            dgain_ref[...] = jnp.zeros_like(dgain_ref)

        if w.ndim == 2:
            acc[...] += _dot_nt(dp_ref[...], w_ref[...])
        else:
            for b in range(kb):
                acc[...] += _dot_nt(dp_ref[:, b * (tk // kb):(b + 1) * (tk // kb)], w_ref[b])

        @pl.when(k == nk - 1)
        def _():
            dx, dgn = _rms_bwd(acc[...], h_ref[...], g_ref[...])
            dhi_ref[...] = dh_ref[...] + dx
            dgain_ref[0:1, :] += dgn

    row = pl.BlockSpec((tm, D), lambda i, k: (i, 0))
    return _call(
        body, name=name, grid=(t // tm, nk), side=side,
        in_specs=[pl.BlockSpec((tm, tk), lambda i, k: (i, k)), w_spec,
                  row, row, pl.BlockSpec((1, D), lambda i, k: (0, 0))],
        out_specs=[row, pl.BlockSpec((8, D), lambda i, k: (0, 0))],
        out_shape=[S((t, D), F32), S((8, D), F32)],
        scratch_shapes=[pltpu.VMEM((tm, D), F32)],
        operands=[dproj, w, dh, h, gain])


def _post_fwd(o, proj, hgain, wout, h, nh, dv, name):
    t = h.shape[0]
    w = nh * dv
    tm = _row_tile(t, 704)

    def body(o_ref, g_ref, hg_ref, wo_ref, h_ref, hn_ref, og_ref):
        for hd in range(nh):
            sl = slice(hd * dv, (hd + 1) * dv)
            oh = o_ref[:, sl].astype(F32)
            r = lax.rsqrt(jnp.mean(oh * oh, axis=-1, keepdims=True) + EPS)
            gg = g_ref[:, sl].astype(F32)
            og_ref[:, sl] = (oh * r * hg_ref[:, sl] * (gg * _sigmoid(gg))).astype(BF16)
        hn_ref[...] = h_ref[...] + _dot(og_ref[...], wo_ref[...])

    return pl.pallas_call(
        body, name=name, grid=(t // tm,),
        in_specs=[pl.BlockSpec((tm, w), lambda i: (i, 0)), pl.BlockSpec((tm, w), lambda i: (i, 2)),
                  pl.BlockSpec((1, w), lambda i: (0, 0)), pl.BlockSpec((w, D), lambda i: (0, 0)),
                  pl.BlockSpec((tm, D), lambda i: (i, 0))],
        out_specs=[pl.BlockSpec((tm, D), lambda i: (i, 0)), pl.BlockSpec((tm, w), lambda i: (i, 0))],
        out_shape=[S((t, D), F32), S((t, w), BF16)],
        compiler_params=_cp(dimension_semantics=("arbitrary",)),
    )(o, proj, hgain, wout, h)


def _post_bwd(dh, o, proj, hgain, wout, nh, dv, nproj, name, side=None):
    t = dh.shape[0]
    w = nh * dv
    tm = _row_tile(t, 704)

    def body(dh_ref, o_ref, g_ref, hg_ref, wo_ref, do_ref, dg_ref, dhb_ref, dhg_ref):
        @pl.when(pl.program_id(0) == 0)
        def _():
            dhg_ref[...] = jnp.zeros_like(dhg_ref)

        dmix = dh_ref[...].astype(BF16)
        dhb_ref[...] = dmix
        dog = _dot_nt(dmix, wo_ref[...])
        for hd in range(nh):
            sl = slice(hd * dv, (hd + 1) * dv)
            oh = o_ref[:, sl].astype(F32)
            r = lax.rsqrt(jnp.mean(oh * oh, axis=-1, keepdims=True) + EPS)
            xh = oh * r
            gain = hg_ref[:, sl]
            gg = g_ref[:, sl].astype(F32)
            s = _sigmoid(gg)
            dogh = dog[:, sl]
            don = dogh * (gg * s)
            dg_ref[:, sl] = (dogh * (xh * gain) * (s * (1.0 + gg * (1.0 - s)))).astype(BF16)
            dxh = don * gain
            do_ref[:, sl] = (r * (dxh - xh * jnp.mean(dxh * xh, axis=-1, keepdims=True))).astype(BF16)
            dhg_ref[0:1, sl] += jnp.sum(don * xh, axis=0, keepdims=True)

    return _call(
        body, name=name, grid=(t // tm,), side=side,
        in_specs=[pl.BlockSpec((tm, D), lambda i: (i, 0)), pl.BlockSpec((tm, w), lambda i: (i, 0)),
                  pl.BlockSpec((tm, w), lambda i: (i, 2)), pl.BlockSpec((1, w), lambda i: (0, 0)),
                  pl.BlockSpec((w, D), lambda i: (0, 0))],
        out_specs=[pl.BlockSpec((tm, w), lambda i: (i, 0)), pl.BlockSpec((tm, w), lambda i: (i, 2)),
                   pl.BlockSpec((tm, D), lambda i: (i, 0)), pl.BlockSpec((8, w), lambda i: (0, 0))],
        out_shape=[S((t, w), BF16), S((t, nproj), BF16), S((t, D), BF16), S((8, w), F32)], scratch_shapes=[],
        operands=[dh, o, proj, hgain, wout])


def _ret_consts():
    lg = np.log1p(-np.exp2(-5.0 - np.arange(RET_H, dtype=np.float32))).astype(np.float32)
    return jnp.asarray(np.broadcast_to(lg[:, None, None], (RET_H, 1, 128)).copy())


def _rope_tables(t):
    half = RET_DK // 2
    inv = 1.0 / (ROPE_BASE ** jnp.linspace(0.0, 1.0, half, dtype=F32))
    base = (jnp.arange(t // CHUNK) * CHUNK - PAD).astype(F32)[:, None] * inv[None, :]
    off = jnp.arange(CHUNK).astype(F32)[:, None] * inv[None, :]
    ca, sa = jnp.cos(base)[:, None, :], jnp.sin(base)[:, None, :]
    cb, sb = jnp.cos(off)[None], jnp.sin(off)[None]
    return (ca * cb - sa * sb).reshape(t, half), (sa * cb + ca * sb).reshape(t, half)


def _ret_chunk(blk_ref, cos_ref, sin_ref, lg, h):
    c = RET_C
    half = RET_DK // 2
    oq, ok, ov = h * RET_DK, RET_H * RET_DK + h * RET_DK, 2 * RET_H * RET_DK + h * RET_DV
    cs, sn = cos_ref[...], sin_ref[...]
    q1, q2 = blk_ref[:, oq:oq + half].astype(F32), blk_ref[:, oq + half:oq + RET_DK].astype(F32)
    k1, k2 = blk_ref[:, ok:ok + half].astype(F32), blk_ref[:, ok + half:ok + RET_DK].astype(F32)
    qr = jnp.concatenate([q1 * cs - q2 * sn, q1 * sn + q2 * cs], axis=1)
    kr = jnp.concatenate([k1 * cs - k2 * sn, k1 * sn + k2 * cs], axis=1) * (RET_DK ** -0.5)
    v = blk_ref[:, ov:ov + RET_DV]
    ii = lax.broadcasted_iota(jnp.int32, (c, 1), 0).astype(F32)
    jj = lax.broadcasted_iota(jnp.int32, (1, c), 1).astype(F32)
    rel = ii - jj
    dmat = jnp.where(rel >= 0, jnp.exp(lg * jnp.maximum(rel, 0.0)), 0.0)
    dq = jnp.exp(lg * (ii + 1.0))
    dk = jnp.exp(lg * (c - 1.0 - ii))
    dchunk = jnp.exp(lg * float(c))
    return qr, kr, v, dmat, dq, dk, dchunk


def _ret_scan_fwd(proj, cos, sin, lgam, name):
    t = proj.shape[0]
    c = RET_C
    nc = t // c

    def body(blk_ref, cos_ref, sin_ref, lg_ref, o_ref, st_ref, state):
        @pl.when(pl.program_id(0) == 0)
        def _():
            state[...] = jnp.zeros_like(state)

        for h in range(RET_H):
            qr, kr, v, dmat, dq, dk, dchunk = _ret_chunk(blk_ref, cos_ref, sin_ref, lg_ref[h, :, 0:1], h)
            sp = state[h]
            st_ref[h] = sp.astype(BF16)
            scores = _dot_nt(qr.astype(BF16), kr.astype(BF16)) * dmat
            o = _dot(scores.astype(BF16), v) + _dot((qr * dq).astype(BF16), sp.astype(BF16))
            o_ref[:, h * RET_DV:(h + 1) * RET_DV] = o.astype(BF16)
            state[h] = sp * dchunk + _dot_tn((kr * dk).astype(BF16), v)

    return pl.pallas_call(
        body, name=name, grid=(nc,),
        in_specs=[pl.BlockSpec((c, RET_QKV), lambda n: (n, 0)), pl.BlockSpec((c, 128), lambda n: (n, 0)),
                  pl.BlockSpec((c, 128), lambda n: (n, 0)), pl.BlockSpec((RET_H, 1, 128), lambda n: (0, 0, 0))],
        out_specs=[pl.BlockSpec((c, RET_H * RET_DV), lambda n: (n, 0)),
                   pl.BlockSpec((RET_H, None, RET_DK, RET_DV), lambda n: (0, n, 0, 0))],
        out_shape=[S((t, RET_H * RET_DV), BF16), S((RET_H, nc, RET_DK, RET_DV), BF16)],
        scratch_shapes=[pltpu.VMEM((RET_H, RET_DK, RET_DV), F32)],
        compiler_params=_cp(dimension_semantics=("arbitrary",)),
    )(proj, cos, sin, lgam)


def _ret_scan_bwd(proj, cos, sin, lgam, do, states, dproj, name, side=None):
    t = proj.shape[0]
    c = RET_C
    nc = t // c
    half = RET_DK // 2

    def body(blk_ref, cos_ref, sin_ref, lg_ref, do_ref, st_ref, dp_in, dp_ref, dstate):
        n = nc - 1 - pl.program_id(0)

        @pl.when(pl.program_id(0) == 0)
        def _():
            dstate[...] = jnp.zeros_like(dstate)

        cs, sn = cos_ref[...], sin_ref[...]
        rows = n * c + lax.broadcasted_iota(jnp.int32, (c, 1), 0)
        keep = rows >= PAD

        def unrot(d):
            d1, d2 = d[:, :half], d[:, half:]
            return jnp.concatenate([d1 * cs + d2 * sn, d2 * cs - d1 * sn], axis=1)

        for h in range(RET_H):
            qr, kr, v, dmat, dq, dk, dchunk = _ret_chunk(blk_ref, cos_ref, sin_ref, lg_ref[h, :, 0:1], h)
            qb, kb = qr.astype(BF16), kr.astype(BF16)
            dob = do_ref[:, h * RET_DV:(h + 1) * RET_DV]
            sp = st_ref[h]
            ds = dstate[h]
            dsb = ds.astype(BF16)
            p = (_dot_nt(qb, kb) * dmat).astype(BF16)
            dvv = _dot_tn(p, dob) + _dot((kr * dk).astype(BF16), dsb)
            dp = (_dot_nt(dob, v) * dmat).astype(BF16)
            dqr = _dot(dp, kb) + _dot_nt(dob, sp) * dq
            dkr = (_dot_tn(dp, qb) + _dot_nt(v, dsb) * dk) * (RET_DK ** -0.5)
            dstate[h] = ds * dchunk + _dot_tn((qr * dq).astype(BF16), dob)
            oq, ok, ov = h * RET_DK, RET_H * RET_DK + h * RET_DK, 2 * RET_H * RET_DK + h * RET_DV
            dp_ref[:, oq:oq + RET_DK] = jnp.where(keep, unrot(dqr), 0.0).astype(BF16)
            dp_ref[:, ok:ok + RET_DK] = jnp.where(keep, unrot(dkr), 0.0).astype(BF16)
            dp_ref[:, ov:ov + RET_DV] = jnp.where(keep, dvv, 0.0).astype(BF16)

    return _call(
        body, name=name, grid=(nc,), side=side, aliases={6: 0},
        in_specs=[pl.BlockSpec((c, RET_QKV), lambda n: (nc - 1 - n, 0)), pl.BlockSpec((c, 128), lambda n: (nc - 1 - n, 0)),
                  pl.BlockSpec((c, 128), lambda n: (nc - 1 - n, 0)), pl.BlockSpec((RET_H, 1, 128), lambda n: (0, 0, 0)),
                  pl.BlockSpec((c, RET_H * RET_DV), lambda n: (nc - 1 - n, 0)),
                  pl.BlockSpec((RET_H, None, RET_DK, RET_DV), lambda n: (0, nc - 1 - n, 0, 0)), ANY],
        out_specs=[pl.BlockSpec((c, RET_QKV), lambda n: (nc - 1 - n, 0))],
        out_shape=[S((t, dproj.shape[1]), BF16)],
        scratch_shapes=[pltpu.VMEM((RET_H, RET_DK, RET_DV), F32)],
        operands=[proj, cos, sin, lgam, do, states, dproj])


def _split3(x):
    hi = x.astype(BF16)
    r1 = x - hi.astype(F32)
    mid = r1.astype(BF16)
    lo = (r1 - mid.astype(F32)).astype(BF16)
    return hi, mid, lo


def _gla_chunk(blk_ref, z_ref, wg_ref, bg_ref, n, h):
    c = CHUNK
    oq, ok, ov = h * GLA_DK, GLA_H * GLA_DK + h * GLA_DK, 2 * GLA_H * GLA_DK + h * GLA_DV
    q = blk_ref[:, oq:oq + GLA_DK].astype(F32) * (GLA_DK ** -0.5)
    k = blk_ref[:, ok:ok + GLA_DK].astype(F32)
    v = blk_ref[:, ov:ov + GLA_DV]
    hs = slice(h * GLA_DK, (h + 1) * GLA_DK)
    u = _dot(z_ref[...], wg_ref[:, hs]) + bg_ref[:, hs]
    la = (jnp.minimum(u, 0.0) - jnp.log(1.0 + jnp.exp(-jnp.abs(u)))) * (1.0 / GLA_TAU)
    rows = n * c + lax.broadcasted_iota(jnp.int32, (c, 1), 0)
    keep = rows >= PAD
    la = jnp.where(keep, la, 0.0)
    ii = lax.broadcasted_iota(jnp.int32, (c, c), 0)
    jj = lax.broadcasted_iota(jnp.int32, (c, c), 1)
    tril = (ii >= jj).astype(BF16)
    hi, mid, lo = _split3(la)
    b = _dot(tril, hi) + _dot(tril, mid) + _dot(tril, lo)
    return q, k, v, u, b, keep


def _gla_intra(qs, ks, bs, a_ref):
    c = CHUNK
    nh = len(qs)
    col = lax.broadcasted_iota(jnp.int32, (1, c), 1)
    rowi = lax.broadcasted_iota(jnp.int32, (SUB, 1), 0)
    for blk in range(c // SUB):
        r = slice(SUB * blk, SUB * (blk + 1))
        arows = []
        for h in range(nh):
            q, k, b = qs[h], ks[h], bs[h]
            if blk > 0:
                bprev = b[SUB * blk - 1:SUB * blk]
                qe = q[r] * jnp.exp(b[r] - bprev)
                kt = k * jnp.exp(jnp.minimum(bprev - b, 0.0))
                arows.append(jnp.where(col < SUB * blk, _dot_nt(qe.astype(BF16), kt.astype(BF16)), 0.0))
            else:
                arows.append(jnp.zeros((SUB, c), F32))
        for j in range(SUB):
            for h in range(nh):
                b_i = bs[h][r]
                e = jnp.exp(b_i - b_i[j:j + 1])
                a = jnp.sum(qs[h][r] * ks[h][r][j:j + 1] * e, axis=1, keepdims=True)
                arows[h] = jnp.where(col == SUB * blk + j, a, arows[h])
        for h in range(nh):
            a_ref[h, r, :] = jnp.where(col - SUB * blk <= rowi, arows[h], 0.0)


def _gla_scan_fwd(proj, wgp, bg, name):
    t = proj.shape[0]
    c = CHUNK
    nc = t // c
    heads = range(GLA_H)

    def body(blk_ref, z_ref, wg_ref, bg_ref, o_ref, st_ref, am_ref, state, a_ref):
        n = pl.program_id(0)

        @pl.when(n == 0)
        def _():
            state[...] = jnp.zeros_like(state)

        qs, ks, vs, us, bs, keeps = zip(*[_gla_chunk(blk_ref, z_ref, wg_ref, bg_ref, n, h) for h in heads])
        _gla_intra(qs, ks, bs, a_ref)
        for h in heads:
            q, k, v, b = qs[h], ks[h], vs[h], bs[h]
            sp = state[h]
            st_ref[h] = sp.astype(BF16)
            ab = a_ref[h].astype(BF16)
            am_ref[:, h * c:(h + 1) * c] = ab
            o = _dot(ab, v) + _dot_nt((q * jnp.exp(b)).astype(BF16), sp.astype(BF16))
            o_ref[:, h * GLA_DV:(h + 1) * GLA_DV] = o.astype(BF16)
            bc = b[c - 1:c]
            state[h] = sp * jnp.exp(bc) + _dot_tn(v, (k * jnp.exp(bc - b)).astype(BF16))

    return pl.pallas_call(
        body, name=name, grid=(nc,),
        in_specs=[pl.BlockSpec((c, GLA_QKV), lambda n: (n, 0)), pl.BlockSpec((c, 128), lambda n: (n, GLA_ZBLK)),
                  pl.BlockSpec((128, GLA_H * GLA_DK), lambda n: (0, 0)), pl.BlockSpec((1, GLA_H * GLA_DK), lambda n: (0, 0))],
        out_specs=[pl.BlockSpec((c, GLA_H * GLA_DV), lambda n: (n, 0)),
                   pl.BlockSpec((GLA_H, None, GLA_DV, GLA_DK), lambda n: (0, n, 0, 0)),
                   pl.BlockSpec((c, GLA_H * c), lambda n: (n, 0))],
        out_shape=[S((t, GLA_H * GLA_DV), BF16), S((GLA_H, nc, GLA_DV, GLA_DK), BF16), S((t, GLA_H * c), BF16)],
        scratch_shapes=[pltpu.VMEM((GLA_H, GLA_DV, GLA_DK), F32), pltpu.VMEM((GLA_H, c, c), F32)],
        compiler_params=_cp(dimension_semantics=("arbitrary",)),
    )(proj, proj, wgp, bg)


def _gla_scan_bwd(proj, wgp, bg, do, states, amat, dproj, name):
    t = proj.shape[0]
    c = CHUNK
    nc = t // c
    heads = range(GLA_H)

    def body(blk_ref, z_ref, wg_ref, bg_ref, do_ref, st_ref, am_ref, dp_in, dp_ref, du_ref, dstate, dq_ref, dkd_ref):
        n = nc - 1 - pl.program_id(0)

        @pl.when(pl.program_id(0) == 0)
        def _():
            dstate[...] = jnp.zeros_like(dstate)

        qs, ks, vs, us, bs, keeps = zip(*[_gla_chunk(blk_ref, z_ref, wg_ref, bg_ref, n, h) for h in heads])
        ii = lax.broadcasted_iota(jnp.int32, (c, c), 0)
        jj = lax.broadcasted_iota(jnp.int32, (c, c), 1)
        col = lax.broadcasted_iota(jnp.int32, (1, c), 1)
        rowi = lax.broadcasted_iota(jnp.int32, (SUB, 1), 0)
        rowc = lax.broadcasted_iota(jnp.int32, (c, 1), 0)
        das, dvs, dq_inters, dk_states, extras, dks = [], [], [], [], [], []
        for h in heads:
            q, k, v, b = qs[h], ks[h], vs[h], bs[h]
            ab = am_ref[:, h * c:(h + 1) * c]
            dob = do_ref[:, h * GLA_DV:(h + 1) * GLA_DV]
            sp = st_ref[h]
            ds = dstate[h]
            dsb = ds.astype(BF16)
            bc = b[c - 1:c]
            eb = jnp.exp(b)
            ebc = jnp.exp(bc - b)
            ec = jnp.exp(bc)
            qb = (q * eb).astype(BF16)
            kb = (k * ebc).astype(BF16)
            dvs.append(_dot_tn(ab, dob) + _dot_nt(kb, dsb))
            das.append(jnp.where(ii >= jj, _dot_nt(dob, v), 0.0))
            dq_inters.append(_dot(dob, sp) * eb)
            dk_state = _dot(v, dsb) * ebc
            dk_states.append(dk_state)
            extras.append(jnp.sum(k * dk_state, axis=0, keepdims=True)
                          + ec * jnp.sum(sp.astype(F32) * ds, axis=0, keepdims=True))
            dstate[h] = ds * ec + _dot_tn(dob, qb)
            dks.append(jnp.zeros((c, GLA_DK), F32))

        for blk in range(c // SUB):
            r = slice(SUB * blk, SUB * (blk + 1))
            dq_is, dkds = [], []
            for h in heads:
                q, k, b = qs[h], ks[h], bs[h]
                if blk > 0:
                    bprev = b[SUB * blk - 1:SUB * blk]
                    e_i = jnp.exp(b[r] - bprev)
                    ek = jnp.exp(jnp.minimum(bprev - b, 0.0))
                    daoff = jnp.where(col < SUB * blk, das[h][r], 0.0).astype(BF16)
                    dq_is.append(_dot(daoff, (k * ek).astype(BF16)) * e_i)
                    dks[h] = dks[h] + _dot_tn(daoff, (q[r] * e_i).astype(BF16)) * ek
                else:
                    dq_is.append(jnp.zeros((SUB, GLA_DK), F32))
                dkds.append(jnp.zeros((SUB, GLA_DK), F32))
            for j in range(SUB):
                for h in heads:
                    b_i = bs[h][r]
                    e = jnp.where(rowi >= j, jnp.exp(b_i - b_i[j:j + 1]), 0.0)
                    dacol = jnp.sum(jnp.where(col == SUB * blk + j, das[h][r], 0.0), axis=1, keepdims=True)
                    tt = dacol * e
                    dq_is[h] = dq_is[h] + tt * ks[h][r][j:j + 1]
                    dkds[h] = jnp.where(rowi == j, jnp.sum(tt * qs[h][r], axis=0, keepdims=True), dkds[h])
            for h in heads:
                dq_ref[h, r, :] = dq_is[h]
                dkd_ref[h, r, :] = dkds[h]

        for h in heads:
            q, k, b, u, keep = qs[h], ks[h], bs[h], us[h], keeps[h]
            dq = dq_ref[h] + dq_inters[h]
            dk = dks[h] + dkd_ref[h] + dk_states[h]
            db = q * dq - k * dk + jnp.where(rowc == c - 1, extras[h], 0.0)
            triu = (ii <= jj).astype(BF16)
            hi, mid, lo = _split3(db)
            dla = _dot(triu, hi) + _dot(triu, mid) + _dot(triu, lo)
            du = jnp.where(keep, dla * (1.0 / GLA_TAU) / (1.0 + jnp.exp(u)), 0.0)
            du_ref[:, h * GLA_DK:(h + 1) * GLA_DK] = du.astype(BF16)
            oq, ok, ov = h * GLA_DK, GLA_H * GLA_DK + h * GLA_DK, 2 * GLA_H * GLA_DK + h * GLA_DV
            dp_ref[:, oq:oq + GLA_DK] = jnp.where(keep, dq * (GLA_DK ** -0.5), 0.0).astype(BF16)
            dp_ref[:, ok:ok + GLA_DK] = jnp.where(keep, dk, 0.0).astype(BF16)
            dp_ref[:, ov:ov + GLA_DV] = jnp.where(keep, dvs[h], 0.0).astype(BF16)

    nproj = dproj.shape[1]
    return pl.pallas_call(
        body, name=name, grid=(nc,),
        in_specs=[pl.BlockSpec((c, GLA_QKV), lambda n: (nc - 1 - n, 0)), pl.BlockSpec((c, 128), lambda n: (nc - 1 - n, GLA_ZBLK)),
                  pl.BlockSpec((128, GLA_H * GLA_DK), lambda n: (0, 0)), pl.BlockSpec((1, GLA_H * GLA_DK), lambda n: (0, 0)),
                  pl.BlockSpec((c, GLA_H * GLA_DV), lambda n: (nc - 1 - n, 0)),
                  pl.BlockSpec((GLA_H, None, GLA_DV, GLA_DK), lambda n: (0, nc - 1 - n, 0, 0)),
                  pl.BlockSpec((c, GLA_H * c), lambda n: (nc - 1 - n, 0)), ANY],
        out_specs=[pl.BlockSpec((c, GLA_QKV), lambda n: (nc - 1 - n, 0)),
                   pl.BlockSpec((c, GLA_H * GLA_DK), lambda n: (nc - 1 - n, 0))],
        out_shape=[S((t, nproj), BF16), S((t, GLA_H * GLA_DK), BF16)],
        input_output_aliases={7: 0},
        scratch_shapes=[pltpu.VMEM((GLA_H, GLA_DV, GLA_DK), F32),
                        pltpu.VMEM((GLA_H, c, GLA_DK), F32), pltpu.VMEM((GLA_H, c, GLA_DK), F32)],
        compiler_params=_cp(dimension_semantics=("arbitrary",)),
    )(proj, proj, wgp, bg, do, states, amat, dproj)


def _gla_gate_bwd(du, proj, wgp, dproj, name):
    t = du.shape[0]
    tm = _row_tile(t, 704)
    w = GLA_H * GLA_DK

    def body(du_ref, z_ref, wg_ref, dp_in, dp_ref, dwg_ref, dbg_ref):
        @pl.when(pl.program_id(0) == 0)
        def _():
            dwg_ref[...] = jnp.zeros_like(dwg_ref)
            dbg_ref[...] = jnp.zeros_like(dbg_ref)

        d = du_ref[...]
        dp_ref[...] = _dot_nt(d, wg_ref[...]).astype(BF16)
        dwg_ref[...] += _dot_tn(z_ref[...], d)
        dbg_ref[0:1, :] += jnp.sum(d.astype(F32), axis=0, keepdims=True)

    return pl.pallas_call(
        body, name=name, grid=(t // tm,),
        in_specs=[pl.BlockSpec((tm, w), lambda i: (i, 0)), pl.BlockSpec((tm, 128), lambda i: (i, GLA_ZBLK)),
                  pl.BlockSpec((128, w), lambda i: (0, 0)), ANY],
        out_specs=[pl.BlockSpec((tm, 128), lambda i: (i, GLA_ZBLK)), pl.BlockSpec((128, w), lambda i: (0, 0)),
                   pl.BlockSpec((8, w), lambda i: (0, 0))],
        out_shape=[S(dproj.shape, BF16), S((128, w), F32), S((8, w), F32)],
        input_output_aliases={3: 0},
        compiler_params=_cp(dimension_semantics=("arbitrary",)),
    )(du, proj, wgp, dproj)


def _final_loss(h, gain, target, name):
    t = h.shape[0]
    tm = _row_tile(t, 704)

    def body(h_ref, g_ref, t_ref, dh_ref, dgain_ref, loss_ref):
        i = pl.program_id(0)

        @pl.when(i == 0)
        def _():
            dgain_ref[...] = jnp.zeros_like(dgain_ref)
            loss_ref[...] = jnp.zeros_like(loss_ref)

        x = h_ref[...]
        gain = g_ref[...]
        r = lax.rsqrt(jnp.mean(x * x, axis=-1, keepdims=True) + EPS)
        xh = x * r
        rows = i * tm + lax.broadcasted_iota(jnp.int32, (tm, 1), 0)
        e = jnp.where(rows >= CHUNK, xh * gain - t_ref[...], 0.0)
        loss_ref[...] += 0.5 * jnp.sum(jnp.mean(e * e, axis=-1, keepdims=True), axis=0, keepdims=True)
        dy = e * (1.0 / D)
        dgain_ref[0:1, :] += jnp.sum(dy * xh, axis=0, keepdims=True)
        dxh = dy * gain
        dh_ref[...] = r * (dxh - xh * jnp.mean(dxh * xh, axis=-1, keepdims=True))

    row = pl.BlockSpec((tm, D), lambda i: (i, 0))
    return pl.pallas_call(
        body, name=name, grid=(t // tm,),
        in_specs=[row, pl.BlockSpec((1, D), lambda i: (0, 0)), row],
        out_specs=[row, pl.BlockSpec((8, D), lambda i: (0, 0)), pl.BlockSpec((8, 128), lambda i: (0, 0))],
        out_shape=[S((t, D), F32), S((8, D), F32), S((8, 128), F32)],
        compiler_params=_cp(dimension_semantics=("arbitrary",)),
    )(h, gain, target)


def _adam_math(w, g, m, v):
    m2 = ADAM_B1 * m + (1.0 - ADAM_B1) * g
    v2 = ADAM_B2 * v + (1.0 - ADAM_B2) * (g * g)
    m_hat = m2 / (1.0 - ADAM_B1 ** ADAM_STEP)
    v_hat = v2 / (1.0 - ADAM_B2 ** ADAM_STEP)
    delta = -ADAM_LR * (m_hat / (jnp.sqrt(v_hat) + ADAM_EPS) + ADAM_WD * w)
    return delta, m2, v2


def _adamw_reduce(recvs, w, m, v, name):
    nl, r, wd = w.shape
    tr = _row_tile(r, 256) if r % 16 == 0 else r
    nr = r // tr

    def body(*refs):
        rv_refs = refs[:nl]
        w_ref, m_ref, v_ref, g_ref, d_ref, m2_ref, v2_ref = refs[nl:]
        layer = pl.program_id(0)

        def total(rv_ref):
            g = rv_ref[0].astype(F32)
            for s in range(1, N_DEV):
                g = g + rv_ref[s].astype(F32)
            return g

        g = total(rv_refs[0])
        for k in range(1, nl):
            g = jnp.where(layer == k, total(rv_refs[k]), g)
        g_ref[...] = g
        d_ref[...], m2_ref[...], v2_ref[...] = _adam_math(w_ref[...], g, m_ref[...], v_ref[...])

    def rv_spec(k):
        return pl.BlockSpec((N_DEV, tr, wd), lambda l, i: (0, jnp.where(l == k, i, jnp.where(l < k, 0, nr - 1)), 0))

    row = pl.BlockSpec((None, tr, wd), lambda l, i: (l, i, 0))
    return pl.pallas_call(
        body, name=name, grid=(nl, nr),
        in_specs=[rv_spec(k) for k in range(nl)] + [row, row, row],
        out_specs=[row] * 4, out_shape=[S((nl, r, wd), F32)] * 4,
        compiler_params=_cp(dimension_semantics=("arbitrary", "arbitrary")),
    )(*recvs, w, m, v)


def _small_reduce(parts, name):
    _, r, wd = parts.shape

    def body(p_ref, o_ref):
        g = p_ref[0]
        for s in range(1, N_DEV):
            g = g + p_ref[s]
        o_ref[...] = g

    return pl.pallas_call(body, name=name, out_shape=S((r, wd), F32), compiler_params=_cp())(parts)


def _adamw_small(w, g, m, v, name):
    def body(w_ref, g_ref, m_ref, v_ref, d_ref, m2_ref, v2_ref):
        d_ref[...], m2_ref[...], v2_ref[...] = _adam_math(w_ref[...], g_ref[...], m_ref[...], v_ref[...])

    return pl.pallas_call(body, name=name, out_shape=[S(w.shape, F32)] * 3, compiler_params=_cp())(w, g, m, v)


def _unshard_cols(g):
    return jnp.transpose(g, (1, 0, 2)).reshape(g.shape[1], N_DEV * g.shape[2])


def _my_cols(full, width):
    me = 4 * lax.axis_index("x") + 2 * lax.axis_index("y") + lax.axis_index("c")
    return lax.dynamic_slice_in_dim(full, me * width, width, axis=1)


def kernel(x, meta_tokens, norm_ffn1, ffn1_w_in, ffn1_w_out, norm_mix, norm_ffn2, ffn2_w_in, ffn2_w_out, ret_w_in, ret_head_norm, ret_w_out, gla_w_in, gla_w_gate, gla_b_gate, gla_head_norm, gla_w_out, final_norm, loss_target, m_meta_tokens, m_norm_ffn1, m_ffn1_w_in, m_ffn1_w_out, m_norm_mix, m_norm_ffn2, m_ffn2_w_in, m_ffn2_w_out, m_ret_w_in, m_ret_head_norm, m_ret_w_out, m_gla_w_in, m_gla_w_gate, m_gla_b_gate, m_gla_head_norm, m_gla_w_out, m_final_norm, v_meta_tokens, v_norm_ffn1, v_ffn1_w_in, v_ffn1_w_out, v_norm_mix, v_norm_ffn2, v_ffn2_w_in, v_ffn2_w_out, v_ret_w_in, v_ret_head_norm, v_ret_w_out, v_gla_w_in, v_gla_w_gate, v_gla_b_gate, v_gla_head_norm, v_gla_w_out, v_final_norm):
    seq = x.shape[1]
    t = seq + CHUNK
    xs = x[0]
    target = loss_target[0]

    def ffn_w(f):
        w_in, w_out = (ffn1_w_in, ffn1_w_out) if f < 2 else (ffn2_w_in, ffn2_w_out)
        return [w_in[f % 2].astype(BF16), w_out[f % 2].astype(BF16)]

    small = jnp.concatenate([meta_tokens.reshape(-1), ret_head_norm.reshape(-1), gla_w_gate.reshape(-1),
                             gla_b_gate.reshape(-1), gla_head_norm.reshape(-1)])
    n_small = small.shape[0]
    small = jnp.pad(small, (0, 32 * 128 - n_small)).reshape(32, 128)
    sg, win0, wout0 = _run_side(_Gather([small] + ffn_w(0)), "ag_first")
    sg = sg.reshape(N_DEV, 32 * 128)

    def small_cols(off, rows, width):
        return jnp.transpose(sg[:, off:off + rows * width].reshape(N_DEV, rows, width), (1, 0, 2)).reshape(rows, N_DEV * width)

    off = 0
    meta_full = small_cols(off, N_META, D // N_DEV); off += N_META * (D // N_DEV)
    ret_hn = small_cols(off, RET_H, RET_DV // N_DEV).reshape(1, RET_H * RET_DV); off += RET_H * RET_DV // N_DEV
    wgate = small_cols(off, GLA_RANK, GLA_H * GLA_DK // N_DEV); off += GLA_RANK * GLA_H * GLA_DK // N_DEV
    bgate = small_cols(off, 1, GLA_H * GLA_DK // N_DEV); off += GLA_H * GLA_DK // N_DEV
    gla_hn = small_cols(off, GLA_H, GLA_DV // N_DEV).reshape(1, GLA_H * GLA_DV)
    wgp = jnp.pad(wgate, ((0, 128 - GLA_RANK), (0, 0))).astype(BF16)

    cos, sin = _rope_tables(t)
    lgam = _ret_consts()

    h0 = jnp.concatenate([jnp.zeros((PAD, D), F32), meta_full, xs], axis=0)
    g1 = [norm_ffn1[i:i + 1] for i in range(2)]
    gm = [norm_mix[i:i + 1] for i in range(2)]
    g2 = [norm_ffn2[i:i + 1] for i in range(2)]

    (h1, xn_a0, pg_a0, pu_a0), (ret_win_g, ret_wout_g) = _ffn_fwd(
        h0, g1[0], win0, wout0, "ffn1_l0_fwd", side=_Gather([ret_w_in[0].astype(BF16), ret_w_out[0].astype(BF16)]))
    ret_win = ret_win_g
    ret_wout = ret_wout_g.reshape(RET_H * RET_DV, D)
    (rproj, rhn), (win2, wout2) = _norm_mm(h1, gm[0], ret_win, 2 * ret_win.shape[2], "ret_proj_fwd", side=_Gather(ffn_w(2)))
    ro, rstates = _ret_scan_fwd(rproj, cos, sin, lgam, "ret_scan_fwd")
    h2, rog = _post_fwd(ro, rproj, ret_hn, ret_wout, h1, RET_H, RET_DV, "ret_post_fwd")
    (h3, xn_b0, pg_b0, pu_b0), (win1, wout1) = _ffn_fwd(h2, g2[0], win2, wout2, "ffn2_l0_fwd", side=_Gather(ffn_w(1)))
    (h4, xn_a1, pg_a1, pu_a1), (gla_win_g, gla_wout_g) = _ffn_fwd(
        h3, g1[1], win1, wout1, "ffn1_l1_fwd", side=_Gather([gla_w_in[0].astype(BF16), gla_w_out[0].astype(BF16)]))
    gla_win = _unshard_cols(gla_win_g)
    gla_win = jnp.pad(gla_win, ((0, 0), (0, GLA_N - gla_win.shape[1])))
    gla_wout = gla_wout_g.reshape(GLA_H * GLA_DV, D)
    (gproj, ghn), (win3, wout3) = _norm_mm(h4, gm[1], gla_win, 640, "gla_proj_fwd", side=_Gather(ffn_w(3)))
    go, gstates, gamat = _gla_scan_fwd(gproj, wgp, bgate, "gla_scan_fwd")
    h5, gog = _post_fwd(go, gproj, gla_hn, gla_wout, h4, GLA_H, GLA_DV, "gla_post_fwd")
    (h6, xn_b1, pg_b1, pu_b1), _ = _ffn_fwd(h5, g2[1], win3, wout3, "ffn2_l1_fwd")

    dh, dfinal, loss_blk = _final_loss(h6, final_norm.reshape(1, D), jnp.pad(target, ((CHUNK, 0), (0, 0))), "final_loss")
    loss = lax.psum(loss_blk[0, 0], ("x", "y", "c"))

    def ffn_back(dh, h_in, xn, gain, pg, pu, win, wout, tag, side=None, dw_side=None):
        (dh_in, dob, dpg, dpu, act, dgain), got = _ffn_bwd(dh, h_in, gain, pg, pu, win, wout, tag + "_bwd", side=side)
        dwout = _mm_tn(act, dob[None], D, tag + "_dw_out").reshape(N_DEV, FF_SHARD // 2, D)
        if dw_side == "own_dw_out":
            dw_side = _Exchange([dwout])
        (dwin,), dw_got = _ffn_dw_in(xn, dpg, dpu, tag + "_dw_in", side=dw_side)
        return dh_in, [dwin, dwout], dgain[0], got, dw_got

    dh, dw_b1, dg2_1, _, _ = ffn_back(dh, h5, xn_b1, g2[1], pg_b1, pu_b1, win3, wout3, "ffn2_l1")

    (gdo, gdproj, gdhb, dghn), _ = _post_bwd(dh, go, gproj, gla_hn, gla_wout, GLA_H, GLA_DV, GLA_N, "gla_post_bwd")
    d_gla_wout = _mm_tn(gog[None], gdhb[None], D, "gla_dw_out").reshape(N_DEV, GLA_H * GLA_DV // N_DEV, D)
    gdproj, gdu = _gla_scan_bwd(gproj, wgp, bgate, gdo, gstates, gamat, gdproj, "gla_scan_bwd")
    gdproj, dwg, dbg = _gla_gate_bwd(gdu, gproj, wgp, gdproj, "gla_gate_bwd")
    d_gla_win = _mm_tn(gdproj[None], ghn[None], D, "gla_dw_in", tm=640)[0]
    (dh, dgm_1), _ = _proj_bwd(gdproj, gla_win, dh, h4, gm[1], 640, "gla_proj_bwd")
    n_gla_in = 2 * GLA_H * GLA_DK + 2 * GLA_H * GLA_DV + GLA_RANK
    d_gla_win = d_gla_win[:n_gla_in].reshape(N_DEV, n_gla_in // N_DEV, D)

    dh, dw_a1, dg1_1, rv_b1, rv_gla = ffn_back(dh, h3, xn_a1, g1[1], pg_a1, pu_a1, win1, wout1, "ffn1_l1",
                                               side=_Exchange(dw_b1), dw_side=_Exchange([d_gla_win, d_gla_wout]))
    dh, dw_b0, dg2_0, rv_a1, _ = ffn_back(dh, h2, xn_b0, g2[0], pg_b0, pu_b0, win2, wout2, "ffn2_l0", side=_Exchange(dw_a1))

    (rdo, rdproj, rdhb, drhn), rv_b0_out = _post_bwd(dh, ro, rproj, ret_hn, ret_wout, RET_H, RET_DV, 6 * D, "ret_post_bwd",
                                                     side=_Exchange(dw_b0[1:]))
    d_ret_wout = _mm_tn(rog[None], rdhb[None], D, "ret_dw_out", rows=DW_ROWS // 2).reshape(N_DEV, RET_H * RET_DV // N_DEV, D)
    (rdproj,), rv_b0_in = _ret_scan_bwd(rproj, cos, sin, lgam, rdo, rstates, rdproj, "ret_scan_bwd", side=_Exchange(dw_b0[:1]))
    rv_b0 = rv_b0_in + rv_b0_out
    d_ret_win = _mm_tn(rhn[None], rdproj[None], ret_win.shape[2], "ret_dw_in", shard_out=True)
    (dh, dgm_0), rv_ret_out = _proj_bwd(rdproj, ret_win, dh, h1, gm[0], 2 * ret_win.shape[2], "ret_proj_bwd", side=_Exchange([d_ret_wout]))

    dh, dw_a0, dg1_0, rv_ret_in, rv_a0_out = ffn_back(dh, h0, xn_a0, g1[0], pg_a0, pu_a0, win0, wout0, "ffn1_l0",
                                                      side=_Exchange([d_ret_win]), dw_side="own_dw_out")
    rv_ret = rv_ret_in + rv_ret_out
    rv_a0 = _run_side(_Exchange(dw_a0[:1]), "xchg_last") + rv_a0_out
    grad_x = dh[CHUNK:][None]

    def adam_t(recvs, w, m, v, tag):
        outs = _adamw_reduce(recvs, *(jnp.swapaxes(a, 1, 2) for a in (w, m, v)), tag)
        return [jnp.swapaxes(o, 1, 2) for o in outs]

    u_ffn1_in = adam_t([rv_a0[0], rv_a1[0]], ffn1_w_in, m_ffn1_w_in, v_ffn1_w_in, "adam_ffn1_w_in")
    u_ffn2_in = adam_t([rv_b0[0], rv_b1[0]], ffn2_w_in, m_ffn2_w_in, v_ffn2_w_in, "adam_ffn2_w_in")
    u_ffn1_out = _adamw_reduce([rv_a0[1], rv_a1[1]], ffn1_w_out, m_ffn1_w_out, v_ffn1_w_out, "adam_ffn1_w_out")
    u_ffn2_out = _adamw_reduce([rv_b0[1], rv_b1[1]], ffn2_w_out, m_ffn2_w_out, v_ffn2_w_out, "adam_ffn2_w_out")
    u_ret_in = _adamw_reduce([rv_ret[0]], ret_w_in, m_ret_w_in, v_ret_w_in, "adam_ret_w_in")
    u_ret_out = _adamw_reduce([rv_ret[1]], ret_w_out, m_ret_w_out, v_ret_w_out, "adam_ret_w_out")
    u_gla_in = adam_t([rv_gla[0]], gla_w_in, m_gla_w_in, v_gla_w_in, "adam_gla_w_in")
    u_gla_out = _adamw_reduce([rv_gla[1]], gla_w_out, m_gla_w_out, v_gla_w_out, "adam_gla_w_out")

    dmeta = dh[PAD:CHUNK]
    parts = jnp.concatenate([
        dg1_0, dg1_1, dgm_0[0], dgm_1[0], dg2_0, dg2_1, dfinal[0], dmeta.reshape(-1), drhn[0], dwg[:GLA_RANK].reshape(-1),
        dbg[0], dghn[0]])
    n_parts = parts.shape[0]
    rows = -(-n_parts // D)
    rows = -(-rows // 8) * 8
    parts = jnp.pad(parts, (0, rows * D - n_parts)).reshape(rows, D)
    tot = _small_reduce(_run_side(_Gather([parts]), "ag_small_grads")[0], "small_grad_sum").reshape(-1)

    off = 0
    def take(nel):
        nonlocal off
        out = tot[off:off + nel]
        off += nel
        return out

    gr_norm_ffn1 = take(2 * D).reshape(2, D)
    gr_norm_mix = take(2 * D).reshape(2, D)
    gr_norm_ffn2 = take(2 * D).reshape(2, D)
    gr_final = take(D)
    gr_meta = _my_cols(take(N_META * D).reshape(N_META, D), D // N_DEV)
    gr_ret_hn = _my_cols(take(RET_H * RET_DV).reshape(RET_H, RET_DV), RET_DV // N_DEV)[None]
    gr_wgate = _my_cols(take(GLA_RANK * GLA_H * GLA_DK).reshape(GLA_RANK, GLA_H * GLA_DK), GLA_H * GLA_DK // N_DEV)[None]
    gr_bgate = _my_cols(take(GLA_H * GLA_DK).reshape(1, GLA_H * GLA_DK), GLA_H * GLA_DK // N_DEV)
    gr_gla_hn = _my_cols(take(GLA_H * GLA_DV).reshape(GLA_H, GLA_DV), GLA_DV // N_DEV)[None]

    small_w = [meta_tokens, norm_ffn1, norm_mix, norm_ffn2, ret_head_norm, gla_w_gate, gla_b_gate, gla_head_norm, final_norm]
    small_g = [gr_meta, gr_norm_ffn1, gr_norm_mix, gr_norm_ffn2, gr_ret_hn, gr_wgate, gr_bgate, gr_gla_hn, gr_final]
    small_m = [m_meta_tokens, m_norm_ffn1, m_norm_mix, m_norm_ffn2, m_ret_head_norm, m_gla_w_gate, m_gla_b_gate, m_gla_head_norm, m_final_norm]
    small_v = [v_meta_tokens, v_norm_ffn1, v_norm_mix, v_norm_ffn2, v_ret_head_norm, v_gla_w_gate, v_gla_b_gate, v_gla_head_norm, v_final_norm]

    def pack(arrs):
        flat = jnp.concatenate([a.reshape(-1) for a in arrs])
        n = flat.shape[0]
        r = -(-n // 128)
        r = -(-r // 8) * 8
        return jnp.pad(flat, (0, r * 128 - n), constant_values=1.0).reshape(r, 128)

    sd, sm, sv = _adamw_small(pack(small_w), pack(small_g), pack(small_m), pack(small_v), "adam_small")

    def unpack(buf):
        flat = buf.reshape(-1)
        outs, o = [], 0
        for a in small_w:
            outs.append(flat[o:o + a.size].reshape(a.shape))
            o += a.size
        return outs

    us_d, us_m, us_v = unpack(sd), unpack(sm), unpack(sv)

    def ordered(k, smalls):
        return (smalls[0], smalls[1], u_ffn1_in[k], u_ffn1_out[k], smalls[2], smalls[3], u_ffn2_in[k], u_ffn2_out[k],
                u_ret_in[k], smalls[4], u_ret_out[k], u_gla_in[k], smalls[5], smalls[6], smalls[7], u_gla_out[k], smalls[8])

    return (loss, grad_x, *ordered(0, small_g), *ordered(1, us_d), *ordered(2, us_m), *ordered(3, us_v))
```

```python
import functools
import math

import numpy as np
import jax
import jax.numpy as jnp
from jax import lax
from jax.experimental import pallas as pl
from jax.experimental.pallas import tpu as pltpu

F32 = jnp.float32
BF16 = jnp.bfloat16
S = jax.ShapeDtypeStruct
ANY = pl.BlockSpec(memory_space=pl.ANY)
MESH = pl.DeviceIdType.MESH

D = 1024
N_META = 16
CHUNK = 64
PAD = CHUNK - N_META
EPS = 1e-6
N_DEV = 8
FF_SHARD = 704
N_FF_CHUNK = 4
RET_H, RET_DK, RET_DV = 4, 256, 512
RET_QKV = RET_H * (2 * RET_DK + RET_DV)
RET_C = 192
GLA_H, GLA_DK, GLA_DV, GLA_RANK, GLA_TAU = 4, 128, 256, 16, 16.0
GLA_QKV = GLA_H * (2 * GLA_DK + GLA_DV)
GLA_N = 3200
GLA_ZBLK = 3072 // 128
SUB = 16
ROPE_BASE = 10000.0
ADAM_LR, ADAM_B1, ADAM_B2, ADAM_EPS, ADAM_WD, ADAM_STEP = 0.001, 0.9, 0.999, 1e-08, 0.01, 10
VMEM_LIMIT = 58 * 1024 * 1024
DW_ROWS = 2752


def _cp(**kw):
    return pltpu.CompilerParams(vmem_limit_bytes=VMEM_LIMIT, **kw)


def _row_tile(t, cap):
    best = 16
    for d in range(16, cap + 1, 16):
        if t % d == 0:
            best = d
    return best


def _sub_rows(tm, parts=2):
    units = tm // 16
    cuts = [16 * (units * p // parts) for p in range(parts + 1)]
    return [slice(a, b) for a, b in zip(cuts[:-1], cuts[1:]) if b > a]


def _dot(a, b):
    return jnp.dot(a, b, preferred_element_type=F32)


def _dot_nt(a, b):
    return lax.dot_general(a, b, (((1,), (1,)), ((), ())), preferred_element_type=F32)


def _dot_tn(a, b):
    return lax.dot_general(a, b, (((0,), (0,)), ((), ())), preferred_element_type=F32)


def _sigmoid(x):
    return pl.reciprocal(1.0 + jnp.exp(-x), approx=True)


def _rms_bwd(dxn, x, gain):
    r = lax.rsqrt(jnp.mean(x * x, axis=-1, keepdims=True) + EPS)
    xh = x * r
    dxh = dxn * gain
    dx = r * (dxh - xh * jnp.mean(dxh * xh, axis=-1, keepdims=True))
    return dx, jnp.sum(dxn * xh, axis=0, keepdims=True)


def _xyc():
    return lax.axis_index("x"), lax.axis_index("y"), lax.axis_index("c")


class _Gather:
    def __init__(self, xs):
        self.xs = list(xs)
        self.n = len(self.xs)

    def out_shape(self):
        return [S((N_DEV,) + a.shape, a.dtype) for a in self.xs]

    def scratch(self):
        return [pltpu.SemaphoreType.DMA((self.n, 7)), pltpu.SemaphoreType.DMA((self.n, 7)), pltpu.SemaphoreType.DMA((self.n,))]

    def phases(self, x_refs, out_refs, send_sems, recv_sems, local_sems):
        x, y, c = _xyc()
        me, sibling = (x, y, c), (x, y, 1 - c)
        chips = [(1 - x, y), (x, 1 - y), (1 - x, 1 - y)]

        def copy(t, k, block, to, src=None):
            px, py, pc = block
            dst = out_refs[t].at[4 * px + 2 * py + pc]
            return pltpu.make_async_remote_copy(
                src_ref=dst if src is None else src, dst_ref=dst,
                send_sem=send_sems.at[t, k], recv_sem=recv_sems.at[t, k], device_id=to, device_id_type=MESH)

        def own(t):
            return pltpu.make_async_copy(x_refs[t], out_refs[t].at[4 * x + 2 * y + c], local_sems.at[t])

        def first(t):
            return [copy(t, 0, me, sibling, src=x_refs[t])] + [
                copy(t, 1 + j, me, (*chip, c), src=x_refs[t]) for j, chip in enumerate(chips)]

        def passed(t):
            return [copy(t, 4 + j, (*chip, c), sibling) for j, chip in enumerate(chips)]

        def start():
            for t in range(self.n):
                own(t).start()
                for cp in first(t):
                    cp.start()

        def mid():
            for t in range(self.n):
                fw = passed(t)
                for j, chip in enumerate(chips):
                    copy(t, 1 + j, (*chip, c), me).wait_recv()
                    fw[j].start()

        def finish():
            for t in range(self.n):
                copy(t, 0, sibling, me).wait_recv()
                for j, chip in enumerate(chips):
                    copy(t, 4 + j, (*chip, 1 - c), me).wait_recv()
                for cp in first(t) + passed(t):
                    cp.wait_send()
                own(t).wait()

        return start, mid, finish


class _Exchange:
    def __init__(self, xs):
        self.xs = list(xs)
        self.n = len(self.xs)

    def out_shape(self):
        return [S(a.shape, a.dtype) for a in self.xs]

    def scratch(self):
        return [pltpu.SemaphoreType.DMA((self.n, 7)), pltpu.SemaphoreType.DMA((self.n, 7)), pltpu.SemaphoreType.DMA((self.n,))]

    def phases(self, g_refs, r_refs, send_sems, recv_sems, local_sems):
        x, y, c = _xyc()
        me = 4 * x + 2 * y + c

        def own(t):
            return pltpu.make_async_copy(g_refs[t].at[me], r_refs[t].at[me], local_sems.at[t])

        def send(t, m):
            px, py, pc = x ^ (m >> 2), y ^ ((m >> 1) & 1), c ^ (m & 1)
            return pltpu.make_async_remote_copy(
                src_ref=g_refs[t].at[4 * px + 2 * py + pc], dst_ref=r_refs[t].at[me],
                send_sem=send_sems.at[t, m - 1], recv_sem=recv_sems.at[t, m - 1],
                device_id=(px, py, pc), device_id_type=MESH)

        def arrival(t, m):
            peer = 4 * (x ^ (m >> 2)) + 2 * (y ^ ((m >> 1) & 1)) + (c ^ (m & 1))
            return pltpu.make_async_remote_copy(
                src_ref=g_refs[t].at[peer], dst_ref=r_refs[t].at[peer],
                send_sem=send_sems.at[t, m - 1], recv_sem=recv_sems.at[t, m - 1],
                device_id=(x, y, c), device_id_type=MESH)

        def start():
            for t in range(self.n):
                own(t).start()
            for m in range(1, N_DEV):
                for t in range(self.n):
                    send(t, m).start()

        def mid():
            pass

        def finish():
            for m in range(1, N_DEV):
                for t in range(self.n):
                    arrival(t, m).wait_recv()
            for m in range(1, N_DEV):
                for t in range(self.n):
                    send(t, m).wait_send()
            for t in range(self.n):
                own(t).wait()

        return start, mid, finish


def _run_side(side, name):
    n = side.n

    def body(*refs):
        start, mid, finish = side.phases(refs[:n], refs[n:2 * n], *refs[2 * n:])
        start()
        mid()
        finish()

    return list(pl.pallas_call(
        body, name=name, out_shape=side.out_shape(), in_specs=[ANY] * n, out_specs=[ANY] * n,
        scratch_shapes=side.scratch())(*side.xs))


def _grid_steps(grid):
    def ids():
        return [pl.program_id(a) for a in range(len(grid))]

    def first():
        return functools.reduce(jnp.logical_and, [i == 0 for i in ids()])

    def middle():
        i = ids()
        return functools.reduce(jnp.logical_and, [i[0] == (3 * grid[0]) // 4] + [j == 0 for j in i[1:]])

    def last():
        return functools.reduce(jnp.logical_and, [i == g - 1 for i, g in zip(ids(), grid)])

    return first, middle, last


def _call(body, *, name, grid, in_specs, out_specs, out_shape, scratch_shapes, operands, side=None, aliases=None):
    n_in, n_out, n_scr = len(in_specs), len(out_shape), len(scratch_shapes)
    full = body
    if side is not None:
        ns = side.n
        first, middle, last = _grid_steps(grid)

        def full(*refs):
            a = n_in
            ins, sins = refs[:a], refs[a:a + ns]
            a += ns
            outs, souts = refs[a:a + n_out], refs[a + n_out:a + n_out + ns]
            a += n_out + ns
            scr, sems = refs[a:a + n_scr], refs[a + n_scr:]
            start, mid, finish = side.phases(sins, souts, *sems)
            pl.when(first())(start)
            body(*ins, *outs, *scr)
            pl.when(middle())(mid)
            pl.when(last())(finish)

        in_specs = list(in_specs) + [ANY] * ns
        out_specs = list(out_specs) + [ANY] * ns
        out_shape = list(out_shape) + side.out_shape()
        scratch_shapes = list(scratch_shapes) + side.scratch()
        operands = list(operands) + side.xs
    outs = pl.pallas_call(
        full, name=name, grid=grid, in_specs=list(in_specs), out_specs=list(out_specs), out_shape=list(out_shape),
        scratch_shapes=list(scratch_shapes), input_output_aliases=aliases or {},
        compiler_params=_cp(dimension_semantics=("arbitrary",) * len(grid)),
    )(*operands)
    return list(outs[:n_out]), list(outs[n_out:])


def _ffn_fwd(h, gain, win, wout, name, side=None):
    t = h.shape[0]
    tm = _row_tile(t, 704)
    nt = t // tm

    def body(h_ref, g_ref, wg_ref, wu_ref, wo_ref, hn_ref, xn_ref, pg_ref, pu_ref, acc):
        c = pl.program_id(1)

        @pl.when(c == 0)
        def _():
            x = h_ref[...]
            r = lax.rsqrt(jnp.mean(x * x, axis=-1, keepdims=True) + EPS)
            xn_ref[...] = (x * r * g_ref[...]).astype(BF16)
            acc[...] = jnp.zeros_like(acc)

        wo = wo_ref[...].reshape(FF_SHARD, D)
        subs = _sub_rows(tm)
        gus = [(_dot(xn_ref[r, :], wg_ref[...]), _dot(xn_ref[r, :], wu_ref[...])) for r in subs]
        for r, (g, u) in zip(subs, gus):
            pg_ref[r, :] = g.astype(BF16)
            pu_ref[r, :] = u.astype(BF16)
            act = (g * _sigmoid(g) * u).astype(BF16)
            acc[r, :] += _dot(act, wo)

        @pl.when(c == N_FF_CHUNK - 1)
        def _():
            hn_ref[...] = h_ref[...] + 0.5 * acc[...]

    return _call(
        body, name=name, grid=(nt, N_FF_CHUNK), side=side,
        in_specs=[
            pl.BlockSpec((tm, D), lambda i, c: (i, 0)),
            pl.BlockSpec((1, D), lambda i, c: (0, 0)),
            pl.BlockSpec((None, D, FF_SHARD), lambda i, c: (c, 0, 0)),
            pl.BlockSpec((None, D, FF_SHARD), lambda i, c: (c + N_FF_CHUNK, 0, 0)),
            pl.BlockSpec((2, FF_SHARD // 2, D), lambda i, c: (c, 0, 0)),
        ],
        out_specs=[
            pl.BlockSpec((tm, D), lambda i, c: (i, 0)),
            pl.BlockSpec((tm, D), lambda i, c: (i, 0)),
            pl.BlockSpec((None, tm, FF_SHARD), lambda i, c: (c, i, 0)),
            pl.BlockSpec((None, tm, FF_SHARD), lambda i, c: (c, i, 0)),
        ],
        out_shape=[S((t, D), F32), S((t, D), BF16), S((N_FF_CHUNK, t, FF_SHARD), BF16), S((N_FF_CHUNK, t, FF_SHARD), BF16)],
        scratch_shapes=[pltpu.VMEM((tm, D), F32)],
        operands=[h, gain, win, win, wout])


def _ffn_bwd(dh, h, gain, pg, pu, win, wout, name, side=None):
    t = h.shape[0]
    tm = _row_tile(t, 704)
    nt = t // tm

    def body(dh_ref, h_ref, g_ref, pg_ref, pu_ref, wg_ref, wu_ref, wo_ref,
             dhi_ref, dob_ref, dpg_ref, dpu_ref, act_ref, dgain_ref, acc):
        i, c = pl.program_id(0), pl.program_id(1)

        @pl.when(c == 0)
        def _():
            dob_ref[...] = (0.5 * dh_ref[...]).astype(BF16)
            acc[...] = jnp.zeros_like(acc)

        @pl.when((i == 0) & (c == 0))
        def _():
            dgain_ref[...] = jnp.zeros_like(dgain_ref)

        wo = wo_ref[...].reshape(FF_SHARD, D)
        subs = _sub_rows(tm)
        dacts = [_dot_nt(dob_ref[r, :], wo) for r in subs]
        for r, dact in zip(subs, dacts):
            g = pg_ref[r, :].astype(F32)
            u = pu_ref[r, :].astype(F32)
            s = _sigmoid(g)
            sl = g * s
            act_ref[r, :] = (sl * u).astype(BF16)
            dg = (dact * u * (s * (1.0 + g * (1.0 - s)))).astype(BF16)
            du = (dact * sl).astype(BF16)
            dpg_ref[r, :] = dg
            dpu_ref[r, :] = du
            acc[r, :] += _dot_nt(dg, wg_ref[...]) + _dot_nt(du, wu_ref[...])

        @pl.when(c == N_FF_CHUNK - 1)
        def _():
            dx, dgn = _rms_bwd(acc[...], h_ref[...], g_ref[...])
            dhi_ref[...] = dh_ref[...] + dx
            dgain_ref[0:1, :] += dgn

    blk = pl.BlockSpec((None, tm, FF_SHARD), lambda i, c: (c, i, 0))
    row = pl.BlockSpec((tm, D), lambda i, c: (i, 0))
    return _call(
        body, name=name, grid=(nt, N_FF_CHUNK), side=side,
        in_specs=[
            row, row, pl.BlockSpec((1, D), lambda i, c: (0, 0)), blk, blk,
            pl.BlockSpec((None, D, FF_SHARD), lambda i, c: (c, 0, 0)),
            pl.BlockSpec((None, D, FF_SHARD), lambda i, c: (c + N_FF_CHUNK, 0, 0)),
            pl.BlockSpec((2, FF_SHARD // 2, D), lambda i, c: (c, 0, 0)),
        ],
        out_specs=[row, row, blk, blk, blk, pl.BlockSpec((8, D), lambda i, c: (0, 0))],
        out_shape=[S((t, D), F32), S((t, D), BF16)] + [S((N_FF_CHUNK, t, FF_SHARD), BF16)] * 3 + [S((8, D), F32)],
        scratch_shapes=[pltpu.VMEM((tm, D), F32)],
        operands=[dh, h, gain, pg, pu, win, win, wout])


def _ffn_dw_in(xn, dpg, dpu, name, side=None):
    t = xn.shape[0]
    tk = _row_tile(t, DW_ROWS)
    nk = t // tk

    def body(a_ref, bg_ref, bu_ref, o_ref, acc):
        c, k = pl.program_id(0), pl.program_id(1)

        @pl.when(k == 0)
        def _():
            acc[...] = jnp.zeros_like(acc)

        @pl.when(c < N_FF_CHUNK)
        def _():
            acc[...] += _dot_tn(bg_ref[...], a_ref[...])

        @pl.when(c >= N_FF_CHUNK)
        def _():
            acc[...] += _dot_tn(bu_ref[...], a_ref[...])

        @pl.when(k == nk - 1)
        def _():
            o_ref[...] = acc[...].astype(BF16)

    return _call(
        body, name=name, grid=(2 * N_FF_CHUNK, nk), side=side,
        in_specs=[
            pl.BlockSpec((tk, D), lambda c, k: (k, 0)),
            pl.BlockSpec((None, tk, FF_SHARD), lambda c, k: (jnp.minimum(c, N_FF_CHUNK - 1), k, 0)),
            pl.BlockSpec((None, tk, FF_SHARD), lambda c, k: (jnp.maximum(c - N_FF_CHUNK, 0), k, 0)),
        ],
        out_specs=[pl.BlockSpec((None, FF_SHARD, D), lambda c, k: (c, 0, 0))],
        out_shape=[S((2 * N_FF_CHUNK, FF_SHARD, D), BF16)],
        scratch_shapes=[pltpu.VMEM((FF_SHARD, D), F32)],
        operands=[xn, dpg, dpu])


def _mm_tn(a, b, tn, name, tm=None, rows=DW_ROWS, shard_out=False):
    ca, t, m = a.shape
    cb, _, n = b.shape
    nc = max(ca, cb)
    tm = m if tm is None else tm
    tk = _row_tile(t, rows)
    nk = t // tk

    def body(a_ref, b_ref, o_ref, acc):
        k = pl.program_id(3)

        @pl.when(k == 0)
        def _():
            acc[...] = jnp.zeros_like(acc)

        acc[...] += _dot_tn(a_ref[...], b_ref[...])

        @pl.when(k == nk - 1)
        def _():
            o_ref[...] = acc[...].astype(BF16)

    if shard_out:
        out_spec = pl.BlockSpec((None, tm, tn), lambda c, i, j, k: (j, 0, 0))
        out_shape = S((n // tn, m, tn), BF16)
    else:
        out_spec = pl.BlockSpec((None, tm, tn), lambda c, i, j, k: (c, i, j))
        out_shape = S((nc, m, n), BF16)
    return pl.pallas_call(
        body, name=name, grid=(nc, m // tm, n // tn, nk),
        in_specs=[
            pl.BlockSpec((None, tk, tm), (lambda c, i, j, k: (c, k, i)) if ca > 1 else (lambda c, i, j, k: (0, k, i))),
            pl.BlockSpec((None, tk, tn), (lambda c, i, j, k: (c, k, j)) if cb > 1 else (lambda c, i, j, k: (0, k, j))),
        ],
        out_specs=out_spec, out_shape=out_shape,
        scratch_shapes=[pltpu.VMEM((tm, tn), F32)],
        compiler_params=_cp(dimension_semantics=("arbitrary",) * 4),
    )(a, b)


def _norm_mm(h, gain, w, tn, name, side=None):
    t = h.shape[0]
    n = w.shape[-1] if w.ndim == 2 else w.shape[0] * w.shape[2]
    tm = _row_tile(t, 704)
    kb = 1 if w.ndim == 2 else tn // w.shape[2]
    w_spec = (pl.BlockSpec((D, tn), lambda i, j: (0, j)) if w.ndim == 2
              else pl.BlockSpec((kb, D, tn // kb), lambda i, j: (j, 0, 0)))

    def body(h_ref, g_ref, w_ref, o_ref, xn_ref):
        @pl.when(pl.program_id(1) == 0)
        def _():
            x = h_ref[...]
            r = lax.rsqrt(jnp.mean(x * x, axis=-1, keepdims=True) + EPS)
            xn_ref[...] = (x * r * g_ref[...]).astype(BF16)

        if w.ndim == 2:
            o_ref[...] = _dot(xn_ref[...], w_ref[...]).astype(BF16)
        else:
            for b in range(kb):
                o_ref[:, b * (tn // kb):(b + 1) * (tn // kb)] = _dot(xn_ref[...], w_ref[b]).astype(BF16)

    return _call(
        body, name=name, grid=(t // tm, n // tn), side=side,
        in_specs=[pl.BlockSpec((tm, D), lambda i, j: (i, 0)), pl.BlockSpec((1, D), lambda i, j: (0, 0)), w_spec],
        out_specs=[pl.BlockSpec((tm, tn), lambda i, j: (i, j)), pl.BlockSpec((tm, D), lambda i, j: (i, 0))],
        out_shape=[S((t, n), BF16), S((t, D), BF16)], scratch_shapes=[],
        operands=[h, gain, w])


def _proj_bwd(dproj, w, dh, h, gain, tk, name, side=None):
    t, n = dproj.shape
    tm = _row_tile(t, 704)
    nk = n // tk
    kb = 1 if w.ndim == 2 else tk // w.shape[2]
    w_spec = (pl.BlockSpec((D, tk), lambda i, k: (0, k)) if w.ndim == 2
              else pl.BlockSpec((kb, D, tk // kb), lambda i, k: (k, 0, 0)))

    def body(dp_ref, w_ref, dh_ref, h_ref, g_ref, dhi_ref, dgain_ref, acc):
        i, k = pl.program_id(0), pl.program_id(1)

        @pl.when(k == 0)
        def _():
            acc[...] = jnp.zeros_like(acc)

        @pl.when((i == 0) & (k == 0))
        def _():
            dgain_ref[...] = jnp.zeros_like(dgain_ref)

        if w.ndim == 2:
            acc[...] += _dot_nt(dp_ref[...], w_ref[...])
        else:
            for b in range(kb):
                acc[...] += _dot_nt(dp_ref[:, b * (tk // kb):(b + 1) * (tk // kb)], w_ref[b])

        @pl.when(k == nk - 1)
        def _():
            dx, dgn = _rms_bwd(acc[...], h_ref[...], g_ref[...])
            dhi_ref[...] = dh_ref[...] + dx
            dgain_ref[0:1, :] += dgn

    row = pl.BlockSpec((tm, D), lambda i, k: (i, 0))
    return _call(
        body, name=name, grid=(t // tm, nk), side=side,
        in_specs=[pl.BlockSpec((tm, tk), lambda i, k: (i, k)), w_spec,
                  row, row, pl.BlockSpec((1, D), lambda i, k: (0, 0))],
        out_specs=[row, pl.BlockSpec((8, D), lambda i, k: (0, 0))],
        out_shape=[S((t, D), F32), S((8, D), F32)],
        scratch_shapes=[pltpu.VMEM((tm, D), F32)],
        operands=[dproj, w, dh, h, gain])


def _post_fwd(o, proj, hgain, wout, h, nh, dv, name):
    t = h.shape[0]
    w = nh * dv
    tm = _row_tile(t, 704)

    def body(o_ref, g_ref, hg_ref, wo_ref, h_ref, hn_ref, og_ref):
        for hd in range(nh):
            sl = slice(hd * dv, (hd + 1) * dv)
            oh = o_ref[:, sl].astype(F32)
            r = lax.rsqrt(jnp.mean(oh * oh, axis=-1, keepdims=True) + EPS)
            gg = g_ref[:, sl].astype(F32)
            og_ref[:, sl] = (oh * r * hg_ref[:, sl] * (gg * _sigmoid(gg))).astype(BF16)
        hn_ref[...] = h_ref[...] + _dot(og_ref[...], wo_ref[...])

    return pl.pallas_call(
        body, name=name, grid=(t // tm,),
        in_specs=[pl.BlockSpec((tm, w), lambda i: (i, 0)), pl.BlockSpec((tm, w), lambda i: (i, 2)),
                  pl.BlockSpec((1, w), lambda i: (0, 0)), pl.BlockSpec((w, D), lambda i: (0, 0)),
                  pl.BlockSpec((tm, D), lambda i: (i, 0))],
        out_specs=[pl.BlockSpec((tm, D), lambda i: (i, 0)), pl.BlockSpec((tm, w), lambda i: (i, 0))],
        out_shape=[S((t, D), F32), S((t, w), BF16)],
        compiler_params=_cp(dimension_semantics=("arbitrary",)),
    )(o, proj, hgain, wout, h)


def _post_bwd(dh, o, proj, hgain, wout, nh, dv, nproj, name, side=None):
    t = dh.shape[0]
    w = nh * dv
    tm = _row_tile(t, 704)

    def body(dh_ref, o_ref, g_ref, hg_ref, wo_ref, do_ref, dg_ref, dhb_ref, dhg_ref):
        @pl.when(pl.program_id(0) == 0)
        def _():
            dhg_ref[...] = jnp.zeros_like(dhg_ref)

        dmix = dh_ref[...].astype(BF16)
        dhb_ref[...] = dmix
        dog = _dot_nt(dmix, wo_ref[...])
        for hd in range(nh):
            sl = slice(hd * dv, (hd + 1) * dv)
            oh = o_ref[:, sl].astype(F32)
            r = lax.rsqrt(jnp.mean(oh * oh, axis=-1, keepdims=True) + EPS)
            xh = oh * r
            gain = hg_ref[:, sl]
            gg = g_ref[:, sl].astype(F32)
            s = _sigmoid(gg)
            dogh = dog[:, sl]
            don = dogh * (gg * s)
            dg_ref[:, sl] = (dogh * (xh * gain) * (s * (1.0 + gg * (1.0 - s)))).astype(BF16)
            dxh = don * gain
            do_ref[:, sl] = (r * (dxh - xh * jnp.mean(dxh * xh, axis=-1, keepdims=True))).astype(BF16)
            dhg_ref[0:1, sl] += jnp.sum(don * xh, axis=0, keepdims=True)

    return _call(
        body, name=name, grid=(t // tm,), side=side,
        in_specs=[pl.BlockSpec((tm, D), lambda i: (i, 0)), pl.BlockSpec((tm, w), lambda i: (i, 0)),
                  pl.BlockSpec((tm, w), lambda i: (i, 2)), pl.BlockSpec((1, w), lambda i: (0, 0)),
                  pl.BlockSpec((w, D), lambda i: (0, 0))],
        out_specs=[pl.BlockSpec((tm, w), lambda i: (i, 0)), pl.BlockSpec((tm, w), lambda i: (i, 2)),
                   pl.BlockSpec((tm, D), lambda i: (i, 0)), pl.BlockSpec((8, w), lambda i: (0, 0))],
        out_shape=[S((t, w), BF16), S((t, nproj), BF16), S((t, D), BF16), S((8, w), F32)], scratch_shapes=[],
        operands=[dh, o, proj, hgain, wout])


def _ret_consts():
    lg = np.log1p(-np.exp2(-5.0 - np.arange(RET_H, dtype=np.float32))).astype(np.float32)
    return jnp.asarray(np.broadcast_to(lg[:, None, None], (RET_H, 1, 128)).copy())


def _rope_tables(t):
    half = RET_DK // 2
    inv = 1.0 / (ROPE_BASE ** jnp.linspace(0.0, 1.0, half, dtype=F32))
    base = (jnp.arange(t // CHUNK) * CHUNK - PAD).astype(F32)[:, None] * inv[None, :]
    off = jnp.arange(CHUNK).astype(F32)[:, None] * inv[None, :]
    ca, sa = jnp.cos(base)[:, None, :], jnp.sin(base)[:, None, :]
    cb, sb = jnp.cos(off)[None], jnp.sin(off)[None]
    return (ca * cb - sa * sb).reshape(t, half), (sa * cb + ca * sb).reshape(t, half)


def _ret_chunk(blk_ref, cos_ref, sin_ref, lg, h):
    c = RET_C
    half = RET_DK // 2
    oq, ok, ov = h * RET_DK, RET_H * RET_DK + h * RET_DK, 2 * RET_H * RET_DK + h * RET_DV
    cs, sn = cos_ref[...], sin_ref[...]
    q1, q2 = blk_ref[:, oq:oq + half].astype(F32), blk_ref[:, oq + half:oq + RET_DK].astype(F32)
    k1, k2 = blk_ref[:, ok:ok + half].astype(F32), blk_ref[:, ok + half:ok + RET_DK].astype(F32)
    qr = jnp.concatenate([q1 * cs - q2 * sn, q1 * sn + q2 * cs], axis=1)
    kr = jnp.concatenate([k1 * cs - k2 * sn, k1 * sn + k2 * cs], axis=1) * (RET_DK ** -0.5)
    v = blk_ref[:, ov:ov + RET_DV]
    ii = lax.broadcasted_iota(jnp.int32, (c, 1), 0).astype(F32)
    jj = lax.broadcasted_iota(jnp.int32, (1, c), 1).astype(F32)
    rel = ii - jj
    dmat = jnp.where(rel >= 0, jnp.exp(lg * jnp.maximum(rel, 0.0)), 0.0)
    dq = jnp.exp(lg * (ii + 1.0))
    dk = jnp.exp(lg * (c - 1.0 - ii))
    dchunk = jnp.exp(lg * float(c))
    return qr, kr, v, dmat, dq, dk, dchunk


def _ret_scan_fwd(proj, cos, sin, lgam, name):
    t = proj.shape[0]
    c = RET_C
    nc = t // c

    def body(blk_ref, cos_ref, sin_ref, lg_ref, o_ref, st_ref, state):
        @pl.when(pl.program_id(0) == 0)
        def _():
            state[...] = jnp.zeros_like(state)

        for h in range(RET_H):
            qr, kr, v, dmat, dq, dk, dchunk = _ret_chunk(blk_ref, cos_ref, sin_ref, lg_ref[h, :, 0:1], h)
            sp = state[h]
            st_ref[h] = sp.astype(BF16)
            scores = _dot_nt(qr.astype(BF16), kr.astype(BF16)) * dmat
            o = _dot(scores.astype(BF16), v) + _dot((qr * dq).astype(BF16), sp.astype(BF16))
            o_ref[:, h * RET_DV:(h + 1) * RET_DV] = o.astype(BF16)
            state[h] = sp * dchunk + _dot_tn((kr * dk).astype(BF16), v)

    return pl.pallas_call(
        body, name=name, grid=(nc,),
        in_specs=[pl.BlockSpec((c, RET_QKV), lambda n: (n, 0)), pl.BlockSpec((c, 128), lambda n: (n, 0)),
                  pl.BlockSpec((c, 128), lambda n: (n, 0)), pl.BlockSpec((RET_H, 1, 128), lambda n: (0, 0, 0))],
        out_specs=[pl.BlockSpec((c, RET_H * RET_DV), lambda n: (n, 0)),
                   pl.BlockSpec((RET_H, None, RET_DK, RET_DV), lambda n: (0, n, 0, 0))],
        out_shape=[S((t, RET_H * RET_DV), BF16), S((RET_H, nc, RET_DK, RET_DV), BF16)],
        scratch_shapes=[pltpu.VMEM((RET_H, RET_DK, RET_DV), F32)],
        compiler_params=_cp(dimension_semantics=("arbitrary",)),
    )(proj, cos, sin, lgam)


def _ret_scan_bwd(proj, cos, sin, lgam, do, states, dproj, name, side=None):
    t = proj.shape[0]
    c = RET_C
    nc = t // c
    half = RET_DK // 2

    def body(blk_ref, cos_ref, sin_ref, lg_ref, do_ref, st_ref, dp_in, dp_ref, dstate):
        n = nc - 1 - pl.program_id(0)

        @pl.when(pl.program_id(0) == 0)
        def _():
            dstate[...] = jnp.zeros_like(dstate)

        cs, sn = cos_ref[...], sin_ref[...]
        rows = n * c + lax.broadcasted_iota(jnp.int32, (c, 1), 0)
        keep = rows >= PAD

        def unrot(d):
            d1, d2 = d[:, :half], d[:, half:]
            return jnp.concatenate([d1 * cs + d2 * sn, d2 * cs - d1 * sn], axis=1)

        for h in range(RET_H):
            qr, kr, v, dmat, dq, dk, dchunk = _ret_chunk(blk_ref, cos_ref, sin_ref, lg_ref[h, :, 0:1], h)
            qb, kb = qr.astype(BF16), kr.astype(BF16)
            dob = do_ref[:, h * RET_DV:(h + 1) * RET_DV]
            sp = st_ref[h]
            ds = dstate[h]
            dsb = ds.astype(BF16)
            p = (_dot_nt(qb, kb) * dmat).astype(BF16)
            dvv = _dot_tn(p, dob) + _dot((kr * dk).astype(BF16), dsb)
            dp = (_dot_nt(dob, v) * dmat).astype(BF16)
            dqr = _dot(dp, kb) + _dot_nt(dob, sp) * dq
            dkr = (_dot_tn(dp, qb) + _dot_nt(v, dsb) * dk) * (RET_DK ** -0.5)
            dstate[h] = ds * dchunk + _dot_tn((qr * dq).astype(BF16), dob)
            oq, ok, ov = h * RET_DK, RET_H * RET_DK + h * RET_DK, 2 * RET_H * RET_DK + h * RET_DV
            dp_ref[:, oq:oq + RET_DK] = jnp.where(keep, unrot(dqr), 0.0).astype(BF16)
            dp_ref[:, ok:ok + RET_DK] = jnp.where(keep, unrot(dkr), 0.0).astype(BF16)
            dp_ref[:, ov:ov + RET_DV] = jnp.where(keep, dvv, 0.0).astype(BF16)

    return _call(
        body, name=name, grid=(nc,), side=side, aliases={6: 0},
        in_specs=[pl.BlockSpec((c, RET_QKV), lambda n: (nc - 1 - n, 0)), pl.BlockSpec((c, 128), lambda n: (nc - 1 - n, 0)),
                  pl.BlockSpec((c, 128), lambda n: (nc - 1 - n, 0)), pl.BlockSpec((RET_H, 1, 128), lambda n: (0, 0, 0)),
                  pl.BlockSpec((c, RET_H * RET_DV), lambda n: (nc - 1 - n, 0)),
                  pl.BlockSpec((RET_H, None, RET_DK, RET_DV), lambda n: (0, nc - 1 - n, 0, 0)), ANY],
        out_specs=[pl.BlockSpec((c, RET_QKV), lambda n: (nc - 1 - n, 0))],
        out_shape=[S((t, dproj.shape[1]), BF16)],
        scratch_shapes=[pltpu.VMEM((RET_H, RET_DK, RET_DV), F32)],
        operands=[proj, cos, sin, lgam, do, states, dproj])


def _split3(x):
    hi = x.astype(BF16)
    r1 = x - hi.astype(F32)
    mid = r1.astype(BF16)
    lo = (r1 - mid.astype(F32)).astype(BF16)
    return hi, mid, lo


def _gla_chunk(blk_ref, z_ref, wg_ref, bg_ref, n, h):
    c = CHUNK
    oq, ok, ov = h * GLA_DK, GLA_H * GLA_DK + h * GLA_DK, 2 * GLA_H * GLA_DK + h * GLA_DV
    q = blk_ref[:, oq:oq + GLA_DK].astype(F32) * (GLA_DK ** -0.5)
    k = blk_ref[:, ok:ok + GLA_DK].astype(F32)
    v = blk_ref[:, ov:ov + GLA_DV]
    hs = slice(h * GLA_DK, (h + 1) * GLA_DK)
    u = _dot(z_ref[...], wg_ref[:, hs]) + bg_ref[:, hs]
    la = (jnp.minimum(u, 0.0) - jnp.log(1.0 + jnp.exp(-jnp.abs(u)))) * (1.0 / GLA_TAU)
    rows = n * c + lax.broadcasted_iota(jnp.int32, (c, 1), 0)
    keep = rows >= PAD
    la = jnp.where(keep, la, 0.0)
    ii = lax.broadcasted_iota(jnp.int32, (c, c), 0)
    jj = lax.broadcasted_iota(jnp.int32, (c, c), 1)
    tril = (ii >= jj).astype(BF16)
    hi, mid, lo = _split3(la)
    b = _dot(tril, hi) + _dot(tril, mid) + _dot(tril, lo)
    return q, k, v, u, b, keep


def _gla_intra(qs, ks, bs, a_ref):
    c = CHUNK
    nh = len(qs)
    col = lax.broadcasted_iota(jnp.int32, (1, c), 1)
    rowi = lax.broadcasted_iota(jnp.int32, (SUB, 1), 0)
    for blk in range(c // SUB):
        r = slice(SUB * blk, SUB * (blk + 1))
        arows = []
        for h in range(nh):
            q, k, b = qs[h], ks[h], bs[h]
            if blk > 0:
                bprev = b[SUB * blk - 1:SUB * blk]
                qe = q[r] * jnp.exp(b[r] - bprev)
                kt = k * jnp.exp(jnp.minimum(bprev - b, 0.0))
                arows.append(jnp.where(col < SUB * blk, _dot_nt(qe.astype(BF16), kt.astype(BF16)), 0.0))
            else:
                arows.append(jnp.zeros((SUB, c), F32))
        for j in range(SUB):
            for h in range(nh):
                b_i = bs[h][r]
                e = jnp.exp(b_i - b_i[j:j + 1])
                a = jnp.sum(qs[h][r] * ks[h][r][j:j + 1] * e, axis=1, keepdims=True)
                arows[h] = jnp.where(col == SUB * blk + j, a, arows[h])
        for h in range(nh):
            a_ref[h, r, :] = jnp.where(col - SUB * blk <= rowi, arows[h], 0.0)


def _gla_scan_fwd(proj, wgp, bg, name):
    t = proj.shape[0]
    c = CHUNK
    nc = t // c
    heads = range(GLA_H)

    def body(blk_ref, z_ref, wg_ref, bg_ref, o_ref, st_ref, am_ref, state, a_ref):
        n = pl.program_id(0)

        @pl.when(n == 0)
        def _():
            state[...] = jnp.zeros_like(state)

        qs, ks, vs, us, bs, keeps = zip(*[_gla_chunk(blk_ref, z_ref, wg_ref, bg_ref, n, h) for h in heads])
        _gla_intra(qs, ks, bs, a_ref)
        for h in heads:
            q, k, v, b = qs[h], ks[h], vs[h], bs[h]
            sp = state[h]
            st_ref[h] = sp.astype(BF16)
            ab = a_ref[h].astype(BF16)
            am_ref[:, h * c:(h + 1) * c] = ab
            o = _dot(ab, v) + _dot_nt((q * jnp.exp(b)).astype(BF16), sp.astype(BF16))
            o_ref[:, h * GLA_DV:(h + 1) * GLA_DV] = o.astype(BF16)
            bc = b[c - 1:c]
            state[h] = sp * jnp.exp(bc) + _dot_tn(v, (k * jnp.exp(bc - b)).astype(BF16))

    return pl.pallas_call(
        body, name=name, grid=(nc,),
        in_specs=[pl.BlockSpec((c, GLA_QKV), lambda n: (n, 0)), pl.BlockSpec((c, 128), lambda n: (n, GLA_ZBLK)),
                  pl.BlockSpec((128, GLA_H * GLA_DK), lambda n: (0, 0)), pl.BlockSpec((1, GLA_H * GLA_DK), lambda n: (0, 0))],
        out_specs=[pl.BlockSpec((c, GLA_H * GLA_DV), lambda n: (n, 0)),
                   pl.BlockSpec((GLA_H, None, GLA_DV, GLA_DK), lambda n: (0, n, 0, 0)),
                   pl.BlockSpec((c, GLA_H * c), lambda n: (n, 0))],
        out_shape=[S((t, GLA_H * GLA_DV), BF16), S((GLA_H, nc, GLA_DV, GLA_DK), BF16), S((t, GLA_H * c), BF16)],
        scratch_shapes=[pltpu.VMEM((GLA_H, GLA_DV, GLA_DK), F32), pltpu.VMEM((GLA_H, c, c), F32)],
        compiler_params=_cp(dimension_semantics=("arbitrary",)),
    )(proj, proj, wgp, bg)


def _gla_scan_bwd(proj, wgp, bg, do, states, amat, dproj, name):
    t = proj.shape[0]
    c = CHUNK
    nc = t // c
    heads = range(GLA_H)

    def body(blk_ref, z_ref, wg_ref, bg_ref, do_ref, st_ref, am_ref, dp_in, dp_ref, du_ref, dstate, dq_ref, dkd_ref):
        n = nc - 1 - pl.program_id(0)

        @pl.when(pl.program_id(0) == 0)
        def _():
            dstate[...] = jnp.zeros_like(dstate)

        qs, ks, vs, us, bs, keeps = zip(*[_gla_chunk(blk_ref, z_ref, wg_ref, bg_ref, n, h) for h in heads])
        ii = lax.broadcasted_iota(jnp.int32, (c, c), 0)
        jj = lax.broadcasted_iota(jnp.int32, (c, c), 1)
        col = lax.broadcasted_iota(jnp.int32, (1, c), 1)
        rowi = lax.broadcasted_iota(jnp.int32, (SUB, 1), 0)
        rowc = lax.broadcasted_iota(jnp.int32, (c, 1), 0)
        das, dvs, dq_inters, dk_states, extras, dks = [], [], [], [], [], []
        for h in heads:
            q, k, v, b = qs[h], ks[h], vs[h], bs[h]
            ab = am_ref[:, h * c:(h + 1) * c]
            dob = do_ref[:, h * GLA_DV:(h + 1) * GLA_DV]
            sp = st_ref[h]
            ds = dstate[h]
            dsb = ds.astype(BF16)
            bc = b[c - 1:c]
            eb = jnp.exp(b)
            ebc = jnp.exp(bc - b)
            ec = jnp.exp(bc)
            qb = (q * eb).astype(BF16)
            kb = (k * ebc).astype(BF16)
            dvs.append(_dot_tn(ab, dob) + _dot_nt(kb, dsb))
            das.append(jnp.where(ii >= jj, _dot_nt(dob, v), 0.0))
            dq_inters.append(_dot(dob, sp) * eb)
            dk_state = _dot(v, dsb) * ebc
            dk_states.append(dk_state)
            extras.append(jnp.sum(k * dk_state, axis=0, keepdims=True)
                          + ec * jnp.sum(sp.astype(F32) * ds, axis=0, keepdims=True))
            dstate[h] = ds * ec + _dot_tn(dob, qb)
            dks.append(jnp.zeros((c, GLA_DK), F32))

        for blk in range(c // SUB):
            r = slice(SUB * blk, SUB * (blk + 1))
            dq_is, dkds = [], []
            for h in heads:
                q, k, b = qs[h], ks[h], bs[h]
                if blk > 0:
                    bprev = b[SUB * blk - 1:SUB * blk]
                    e_i = jnp.exp(b[r] - bprev)
                    ek = jnp.exp(jnp.minimum(bprev - b, 0.0))
                    daoff = jnp.where(col < SUB * blk, das[h][r], 0.0).astype(BF16)
                    dq_is.append(_dot(daoff, (k * ek).astype(BF16)) * e_i)
                    dks[h] = dks[h] + _dot_tn(daoff, (q[r] * e_i).astype(BF16)) * ek
                else:
                    dq_is.append(jnp.zeros((SUB, GLA_DK), F32))
                dkds.append(jnp.zeros((SUB, GLA_DK), F32))
            for j in range(SUB):
                for h in heads:
                    b_i = bs[h][r]
                    e = jnp.where(rowi >= j, jnp.exp(b_i - b_i[j:j + 1]), 0.0)
                    dacol = jnp.sum(jnp.where(col == SUB * blk + j, das[h][r], 0.0), axis=1, keepdims=True)
                    tt = dacol * e
                    dq_is[h] = dq_is[h] + tt * ks[h][r][j:j + 1]
                    dkds[h] = jnp.where(rowi == j, jnp.sum(tt * qs[h][r], axis=0, keepdims=True), dkds[h])
            for h in heads:
                dq_ref[h, r, :] = dq_is[h]
                dkd_ref[h, r, :] = dkds[h]

        for h in heads:
            q, k, b, u, keep = qs[h], ks[h], bs[h], us[h], keeps[h]
            dq = dq_ref[h] + dq_inters[h]
            dk = dks[h] + dkd_ref[h] + dk_states[h]
            db = q * dq - k * dk + jnp.where(rowc == c - 1, extras[h], 0.0)
            triu = (ii <= jj).astype(BF16)
            hi, mid, lo = _split3(db)
            dla = _dot(triu, hi) + _dot(triu, mid) + _dot(triu, lo)
            du = jnp.where(keep, dla * (1.0 / GLA_TAU) / (1.0 + jnp.exp(u)), 0.0)
            du_ref[:, h * GLA_DK:(h + 1) * GLA_DK] = du.astype(BF16)
            oq, ok, ov = h * GLA_DK, GLA_H * GLA_DK + h * GLA_DK, 2 * GLA_H * GLA_DK + h * GLA_DV
            dp_ref[:, oq:oq + GLA_DK] = jnp.where(keep, dq * (GLA_DK ** -0.5), 0.0).astype(BF16)
            dp_ref[:, ok:ok + GLA_DK] = jnp.where(keep, dk, 0.0).astype(BF16)
            dp_ref[:, ov:ov + GLA_DV] = jnp.where(keep, dvs[h], 0.0).astype(BF16)

    nproj = dproj.shape[1]
    return pl.pallas_call(
        body, name=name, grid=(nc,),
        in_specs=[pl.BlockSpec((c, GLA_QKV), lambda n: (nc - 1 - n, 0)), pl.BlockSpec((c, 128), lambda n: (nc - 1 - n, GLA_ZBLK)),
                  pl.BlockSpec((128, GLA_H * GLA_DK), lambda n: (0, 0)), pl.BlockSpec((1, GLA_H * GLA_DK), lambda n: (0, 0)),
                  pl.BlockSpec((c, GLA_H * GLA_DV), lambda n: (nc - 1 - n, 0)),
                  pl.BlockSpec((GLA_H, None, GLA_DV, GLA_DK), lambda n: (0, nc - 1 - n, 0, 0)),
                  pl.BlockSpec((c, GLA_H * c), lambda n: (nc - 1 - n, 0)), ANY],
        out_specs=[pl.BlockSpec((c, GLA_QKV), lambda n: (nc - 1 - n, 0)),
                   pl.BlockSpec((c, GLA_H * GLA_DK), lambda n: (nc - 1 - n, 0))],
        out_shape=[S((t, nproj), BF16), S((t, GLA_H * GLA_DK), BF16)],
        input_output_aliases={7: 0},
        scratch_shapes=[pltpu.VMEM((GLA_H, GLA_DV, GLA_DK), F32),
                        pltpu.VMEM((GLA_H, c, GLA_DK), F32), pltpu.VMEM((GLA_H, c, GLA_DK), F32)],
        compiler_params=_cp(dimension_semantics=("arbitrary",)),
    )(proj, proj, wgp, bg, do, states, amat, dproj)


def _gla_gate_bwd(du, proj, wgp, dproj, name):
    t = du.shape[0]
    tm = _row_tile(t, 704)
    w = GLA_H * GLA_DK

    def body(du_ref, z_ref, wg_ref, dp_in, dp_ref, dwg_ref, dbg_ref):
        @pl.when(pl.program_id(0) == 0)
        def _():
            dwg_ref[...] = jnp.zeros_like(dwg_ref)
            dbg_ref[...] = jnp.zeros_like(dbg_ref)

        d = du_ref[...]
        dp_ref[...] = _dot_nt(d, wg_ref[...]).astype(BF16)
        dwg_ref[...] += _dot_tn(z_ref[...], d)
        dbg_ref[0:1, :] += jnp.sum(d.astype(F32), axis=0, keepdims=True)

    return pl.pallas_call(
        body, name=name, grid=(t // tm,),
        in_specs=[pl.BlockSpec((tm, w), lambda i: (i, 0)), pl.BlockSpec((tm, 128), lambda i: (i, GLA_ZBLK)),
                  pl.BlockSpec((128, w), lambda i: (0, 0)), ANY],
        out_specs=[pl.BlockSpec((tm, 128), lambda i: (i, GLA_ZBLK)), pl.BlockSpec((128, w), lambda i: (0, 0)),
                   pl.BlockSpec((8, w), lambda i: (0, 0))],
        out_shape=[S(dproj.shape, BF16), S((128, w), F32), S((8, w), F32)],
        input_output_aliases={3: 0},
        compiler_params=_cp(dimension_semantics=("arbitrary",)),
    )(du, proj, wgp, dproj)


def _final_loss(h, gain, target, name):
    t = h.shape[0]
    tm = _row_tile(t, 704)

    def body(h_ref, g_ref, t_ref, dh_ref, dgain_ref, loss_ref):
        i = pl.program_id(0)

        @pl.when(i == 0)
        def _():
            dgain_ref[...] = jnp.zeros_like(dgain_ref)
            loss_ref[...] = jnp.zeros_like(loss_ref)

        x = h_ref[...]
        gain = g_ref[...]
        r = lax.rsqrt(jnp.mean(x * x, axis=-1, keepdims=True) + EPS)
        xh = x * r
        rows = i * tm + lax.broadcasted_iota(jnp.int32, (tm, 1), 0)
        e = jnp.where(rows >= CHUNK, xh * gain - t_ref[...], 0.0)
        loss_ref[...] += 0.5 * jnp.sum(jnp.mean(e * e, axis=-1, keepdims=True), axis=0, keepdims=True)
        dy = e * (1.0 / D)
        dgain_ref[0:1, :] += jnp.sum(dy * xh, axis=0, keepdims=True)
        dxh = dy * gain
        dh_ref[...] = r * (dxh - xh * jnp.mean(dxh * xh, axis=-1, keepdims=True))

    row = pl.BlockSpec((tm, D), lambda i: (i, 0))
    return pl.pallas_call(
        body, name=name, grid=(t // tm,),
        in_specs=[row, pl.BlockSpec((1, D), lambda i: (0, 0)), row],
        out_specs=[row, pl.BlockSpec((8, D), lambda i: (0, 0)), pl.BlockSpec((8, 128), lambda i: (0, 0))],
        out_shape=[S((t, D), F32), S((8, D), F32), S((8, 128), F32)],
        compiler_params=_cp(dimension_semantics=("arbitrary",)),
    )(h, gain, target)


def _adam_math(w, g, m, v):
    m2 = ADAM_B1 * m + (1.0 - ADAM_B1) * g
    v2 = ADAM_B2 * v + (1.0 - ADAM_B2) * (g * g)
    m_hat = m2 / (1.0 - ADAM_B1 ** ADAM_STEP)
    v_hat = v2 / (1.0 - ADAM_B2 ** ADAM_STEP)
    delta = -ADAM_LR * (m_hat / (jnp.sqrt(v_hat) + ADAM_EPS) + ADAM_WD * w)
    return delta, m2, v2


def _adamw_reduce(recvs, w, m, v, name):
    nl, r, wd = w.shape
    tr = _row_tile(r, 256) if r % 16 == 0 else r
    nr = r // tr

    def body(*refs):
        rv_refs = refs[:nl]
        w_ref, m_ref, v_ref, g_ref, d_ref, m2_ref, v2_ref = refs[nl:]
        layer = pl.program_id(0)

        def total(rv_ref):
            g = rv_ref[0].astype(F32)
            for s in range(1, N_DEV):
                g = g + rv_ref[s].astype(F32)
            return g

        g = total(rv_refs[0])
        for k in range(1, nl):
            g = jnp.where(layer == k, total(rv_refs[k]), g)
        g_ref[...] = g
        d_ref[...], m2_ref[...], v2_ref[...] = _adam_math(w_ref[...], g, m_ref[...], v_ref[...])

    def rv_spec(k):
        return pl.BlockSpec((N_DEV, tr, wd), lambda l, i: (0, jnp.where(l == k, i, jnp.where(l < k, 0, nr - 1)), 0))

    row = pl.BlockSpec((None, tr, wd), lambda l, i: (l, i, 0))
    return pl.pallas_call(
        body, name=name, grid=(nl, nr),
        in_specs=[rv_spec(k) for k in range(nl)] + [row, row, row],
        out_specs=[row] * 4, out_shape=[S((nl, r, wd), F32)] * 4,
        compiler_params=_cp(dimension_semantics=("arbitrary", "arbitrary")),
    )(*recvs, w, m, v)


def _small_reduce(parts, name):
    _, r, wd = parts.shape

    def body(p_ref, o_ref):
        g = p_ref[0]
        for s in range(1, N_DEV):
            g = g + p_ref[s]
        o_ref[...] = g

    return pl.pallas_call(body, name=name, out_shape=S((r, wd), F32), compiler_params=_cp())(parts)


def _adamw_small(w, g, m, v, name):
    def body(w_ref, g_ref, m_ref, v_ref, d_ref, m2_ref, v2_ref):
        d_ref[...], m2_ref[...], v2_ref[...] = _adam_math(w_ref[...], g_ref[...], m_ref[...], v_ref[...])

    return pl.pallas_call(body, name=name, out_shape=[S(w.shape, F32)] * 3, compiler_params=_cp())(w, g, m, v)


def _unshard_cols(g):
    return jnp.transpose(g, (1, 0, 2)).reshape(g.shape[1], N_DEV * g.shape[2])


def _my_cols(full, width):
    me = 4 * lax.axis_index("x") + 2 * lax.axis_index("y") + lax.axis_index("c")
    return lax.dynamic_slice_in_dim(full, me * width, width, axis=1)


def kernel(x, meta_tokens, norm_ffn1, ffn1_w_in, ffn1_w_out, norm_mix, norm_ffn2, ffn2_w_in, ffn2_w_out, ret_w_in, ret_head_norm, ret_w_out, gla_w_in, gla_w_gate, gla_b_gate, gla_head_norm, gla_w_out, final_norm, loss_target, m_meta_tokens, m_norm_ffn1, m_ffn1_w_in, m_ffn1_w_out, m_norm_mix, m_norm_ffn2, m_ffn2_w_in, m_ffn2_w_out, m_ret_w_in, m_ret_head_norm, m_ret_w_out, m_gla_w_in, m_gla_w_gate, m_gla_b_gate, m_gla_head_norm, m_gla_w_out, m_final_norm, v_meta_tokens, v_norm_ffn1, v_ffn1_w_in, v_ffn1_w_out, v_norm_mix, v_norm_ffn2, v_ffn2_w_in, v_ffn2_w_out, v_ret_w_in, v_ret_head_norm, v_ret_w_out, v_gla_w_in, v_gla_w_gate, v_gla_b_gate, v_gla_head_norm, v_gla_w_out, v_final_norm):
    seq = x.shape[1]
    t = seq + CHUNK
    xs = x[0]
    target = loss_target[0]

    def ffn_w(f):
        w_in, w_out = (ffn1_w_in, ffn1_w_out) if f < 2 else (ffn2_w_in, ffn2_w_out)
        return [w_in[f % 2].astype(BF16), w_out[f % 2].astype(BF16)]

    small = jnp.concatenate([meta_tokens.reshape(-1), ret_head_norm.reshape(-1), gla_w_gate.reshape(-1),
                             gla_b_gate.reshape(-1), gla_head_norm.reshape(-1)])
    n_small = small.shape[0]
    small = jnp.pad(small, (0, 32 * 128 - n_small)).reshape(32, 128)
    sg, win0, wout0 = _run_side(_Gather([small] + ffn_w(0)), "ag_first")
    sg = sg.reshape(N_DEV, 32 * 128)

    def small_cols(off, rows, width):
        return jnp.transpose(sg[:, off:off + rows * width].reshape(N_DEV, rows, width), (1, 0, 2)).reshape(rows, N_DEV * width)

    off = 0
    meta_full = small_cols(off, N_META, D // N_DEV); off += N_META * (D // N_DEV)
    ret_hn = small_cols(off, RET_H, RET_DV // N_DEV).reshape(1, RET_H * RET_DV); off += RET_H * RET_DV // N_DEV
    wgate = small_cols(off, GLA_RANK, GLA_H * GLA_DK // N_DEV); off += GLA_RANK * GLA_H * GLA_DK // N_DEV
    bgate = small_cols(off, 1, GLA_H * GLA_DK // N_DEV); off += GLA_H * GLA_DK // N_DEV
    gla_hn = small_cols(off, GLA_H, GLA_DV // N_DEV).reshape(1, GLA_H * GLA_DV)
    wgp = jnp.pad(wgate, ((0, 128 - GLA_RANK), (0, 0))).astype(BF16)

    cos, sin = _rope_tables(t)
    lgam = _ret_consts()

    h0 = jnp.concatenate([jnp.zeros((PAD, D), F32), meta_full, xs], axis=0)
    g1 = [norm_ffn1[i:i + 1] for i in range(2)]
    gm = [norm_mix[i:i + 1] for i in range(2)]
    g2 = [norm_ffn2[i:i + 1] for i in range(2)]

    (h1, xn_a0, pg_a0, pu_a0), (ret_win_g, ret_wout_g, win2, wout2) = _ffn_fwd(
        h0, g1[0], win0, wout0, "ffn1_l0_fwd",
        side=_Gather([ret_w_in[0].astype(BF16), ret_w_out[0].astype(BF16)] + ffn_w(2)))
    ret_win = ret_win_g
    ret_wout = ret_wout_g.reshape(RET_H * RET_DV, D)
    (rproj, rhn), _ = _norm_mm(h1, gm[0], ret_win, 4 * ret_win.shape[2], "ret_proj_fwd")
    ro, rstates = _ret_scan_fwd(rproj, cos, sin, lgam, "ret_scan_fwd")
    h2, rog = _post_fwd(ro, rproj, ret_hn, ret_wout, h1, RET_H, RET_DV, "ret_post_fwd")
    (h3, xn_b0, pg_b0, pu_b0), (win1, wout1) = _ffn_fwd(h2, g2[0], win2, wout2, "ffn2_l0_fwd", side=_Gather(ffn_w(1)))
    (h4, xn_a1, pg_a1, pu_a1), (gla_win_g, gla_wout_g, win3, wout3) = _ffn_fwd(
        h3, g1[1], win1, wout1, "ffn1_l1_fwd",
        side=_Gather([gla_w_in[0].astype(BF16), gla_w_out[0].astype(BF16)] + ffn_w(3)))
    gla_win = _unshard_cols(gla_win_g)
    gla_win = jnp.pad(gla_win, ((0, 0), (0, GLA_N - gla_win.shape[1])))
    gla_wout = gla_wout_g.reshape(GLA_H * GLA_DV, D)
    (gproj, ghn), _ = _norm_mm(h4, gm[1], gla_win, GLA_N, "gla_proj_fwd")
    go, gstates, gamat = _gla_scan_fwd(gproj, wgp, bgate, "gla_scan_fwd")
    h5, gog = _post_fwd(go, gproj, gla_hn, gla_wout, h4, GLA_H, GLA_DV, "gla_post_fwd")
    (h6, xn_b1, pg_b1, pu_b1), _ = _ffn_fwd(h5, g2[1], win3, wout3, "ffn2_l1_fwd")

    dh, dfinal, loss_blk = _final_loss(h6, final_norm.reshape(1, D), jnp.pad(target, ((CHUNK, 0), (0, 0))), "final_loss")
    loss = lax.psum(loss_blk[0, 0], ("x", "y", "c"))

    def ffn_back(dh, h_in, xn, gain, pg, pu, win, wout, tag, side=None, dw_side=None):
        (dh_in, dob, dpg, dpu, act, dgain), got = _ffn_bwd(dh, h_in, gain, pg, pu, win, wout, tag + "_bwd", side=side)
        dwout = _mm_tn(act, dob[None], D, tag + "_dw_out").reshape(N_DEV, FF_SHARD // 2, D)
        if dw_side == "own_dw_out":
            dw_side = _Exchange([dwout])
        (dwin,), dw_got = _ffn_dw_in(xn, dpg, dpu, tag + "_dw_in", side=dw_side)
        return dh_in, [dwin, dwout], dgain[0], got, dw_got

    dh, dw_b1, dg2_1, _, _ = ffn_back(dh, h5, xn_b1, g2[1], pg_b1, pu_b1, win3, wout3, "ffn2_l1")

    (gdo, gdproj, gdhb, dghn), _ = _post_bwd(dh, go, gproj, gla_hn, gla_wout, GLA_H, GLA_DV, GLA_N, "gla_post_bwd")
    d_gla_wout = _mm_tn(gog[None], gdhb[None], D, "gla_dw_out").reshape(N_DEV, GLA_H * GLA_DV // N_DEV, D)
    gdproj, gdu = _gla_scan_bwd(gproj, wgp, bgate, gdo, gstates, gamat, gdproj, "gla_scan_bwd")
    gdproj, dwg, dbg = _gla_gate_bwd(gdu, gproj, wgp, gdproj, "gla_gate_bwd")
    d_gla_win = _mm_tn(gdproj[None], ghn[None], D, "gla_dw_in", tm=640)[0]
    (dh, dgm_1), _ = _proj_bwd(gdproj, gla_win, dh, h4, gm[1], GLA_N, "gla_proj_bwd")
    n_gla_in = 2 * GLA_H * GLA_DK + 2 * GLA_H * GLA_DV + GLA_RANK
    d_gla_win = d_gla_win[:n_gla_in].reshape(N_DEV, n_gla_in // N_DEV, D)

    dh, dw_a1, dg1_1, rv_b1, rv_gla = ffn_back(dh, h3, xn_a1, g1[1], pg_a1, pu_a1, win1, wout1, "ffn1_l1",
                                               side=_Exchange(dw_b1), dw_side=_Exchange([d_gla_win, d_gla_wout]))
    dh, dw_b0, dg2_0, rv_a1, _ = ffn_back(dh, h2, xn_b0, g2[0], pg_b0, pu_b0, win2, wout2, "ffn2_l0", side=_Exchange(dw_a1))

    (rdo, rdproj, rdhb, drhn), rv_b0_out = _post_bwd(dh, ro, rproj, ret_hn, ret_wout, RET_H, RET_DV, 6 * D, "ret_post_bwd",
                                                     side=_Exchange(dw_b0[1:]))
    d_ret_wout = _mm_tn(rog[None], rdhb[None], D, "ret_dw_out", rows=DW_ROWS // 2).reshape(N_DEV, RET_H * RET_DV // N_DEV, D)
    (rdproj,), rv_b0_in = _ret_scan_bwd(rproj, cos, sin, lgam, rdo, rstates, rdproj, "ret_scan_bwd", side=_Exchange(dw_b0[:1]))
    rv_b0 = rv_b0_in + rv_b0_out
    d_ret_win = _mm_tn(rhn[None], rdproj[None], ret_win.shape[2], "ret_dw_in", shard_out=True)
    (dh, dgm_0), rv_ret_out = _proj_bwd(rdproj, ret_win, dh, h1, gm[0], 4 * ret_win.shape[2], "ret_proj_bwd", side=_Exchange([d_ret_wout]))

    dh, dw_a0, dg1_0, rv_ret_in, rv_a0_out = ffn_back(dh, h0, xn_a0, g1[0], pg_a0, pu_a0, win0, wout0, "ffn1_l0",
                                                      side=_Exchange([d_ret_win]), dw_side="own_dw_out")
    rv_ret = rv_ret_in + rv_ret_out
    rv_a0 = _run_side(_Exchange(dw_a0[:1]), "xchg_last") + rv_a0_out
    grad_x = dh[CHUNK:][None]

    def adam_t(recvs, w, m, v, tag):
        outs = _adamw_reduce(recvs, *(jnp.swapaxes(a, 1, 2) for a in (w, m, v)), tag)
        return [jnp.swapaxes(o, 1, 2) for o in outs]

    u_ffn1_in = adam_t([rv_a0[0], rv_a1[0]], ffn1_w_in, m_ffn1_w_in, v_ffn1_w_in, "adam_ffn1_w_in")
    u_ffn2_in = adam_t([rv_b0[0], rv_b1[0]], ffn2_w_in, m_ffn2_w_in, v_ffn2_w_in, "adam_ffn2_w_in")
    u_ffn1_out = _adamw_reduce([rv_a0[1], rv_a1[1]], ffn1_w_out, m_ffn1_w_out, v_ffn1_w_out, "adam_ffn1_w_out")
    u_ffn2_out = _adamw_reduce([rv_b0[1], rv_b1[1]], ffn2_w_out, m_ffn2_w_out, v_ffn2_w_out, "adam_ffn2_w_out")
    u_ret_in = _adamw_reduce([rv_ret[0]], ret_w_in, m_ret_w_in, v_ret_w_in, "adam_ret_w_in")
    u_ret_out = _adamw_reduce([rv_ret[1]], ret_w_out, m_ret_w_out, v_ret_w_out, "adam_ret_w_out")
    u_gla_in = adam_t([rv_gla[0]], gla_w_in, m_gla_w_in, v_gla_w_in, "adam_gla_w_in")
    u_gla_out = _adamw_reduce([rv_gla[1]], gla_w_out, m_gla_w_out, v_gla_w_out, "adam_gla_w_out")

    dmeta = dh[PAD:CHUNK]
    parts = jnp.concatenate([
        dg1_0, dg1_1, dgm_0[0], dgm_1[0], dg2_0, dg2_1, dfinal[0], dmeta.reshape(-1), drhn[0], dwg[:GLA_RANK].reshape(-1),
        dbg[0], dghn[0]])
    n_parts = parts.shape[0]
    rows = -(-n_parts // D)
    rows = -(-rows // 8) * 8
    parts = jnp.pad(parts, (0, rows * D - n_parts)).reshape(rows, D)
    tot = _small_reduce(_run_side(_Gather([parts]), "ag_small_grads")[0], "small_grad_sum").reshape(-1)

    off = 0
    def take(nel):
        nonlocal off
        out = tot[off:off + nel]
        off += nel
        return out

    gr_norm_ffn1 = take(2 * D).reshape(2, D)
    gr_norm_mix = take(2 * D).reshape(2, D)
    gr_norm_ffn2 = take(2 * D).reshape(2, D)
    gr_final = take(D)
    gr_meta = _my_cols(take(N_META * D).reshape(N_META, D), D // N_DEV)
    gr_ret_hn = _my_cols(take(RET_H * RET_DV).reshape(RET_H, RET_DV), RET_DV // N_DEV)[None]
    gr_wgate = _my_cols(take(GLA_RANK * GLA_H * GLA_DK).reshape(GLA_RANK, GLA_H * GLA_DK), GLA_H * GLA_DK // N_DEV)[None]
    gr_bgate = _my_cols(take(GLA_H * GLA_DK).reshape(1, GLA_H * GLA_DK), GLA_H * GLA_DK // N_DEV)
    gr_gla_hn = _my_cols(take(GLA_H * GLA_DV).reshape(GLA_H, GLA_DV), GLA_DV // N_DEV)[None]

    small_w = [meta_tokens, norm_ffn1, norm_mix, norm_ffn2, ret_head_norm, gla_w_gate, gla_b_gate, gla_head_norm, final_norm]
    small_g = [gr_meta, gr_norm_ffn1, gr_norm_mix, gr_norm_ffn2, gr_ret_hn, gr_wgate, gr_bgate, gr_gla_hn, gr_final]
    small_m = [m_meta_tokens, m_norm_ffn1, m_norm_mix, m_norm_ffn2, m_ret_head_norm, m_gla_w_gate, m_gla_b_gate, m_gla_head_norm, m_final_norm]
    small_v = [v_meta_tokens, v_norm_ffn1, v_norm_mix, v_norm_ffn2, v_ret_head_norm, v_gla_w_gate, v_gla_b_gate, v_gla_head_norm, v_final_norm]

    def pack(arrs):
        flat = jnp.concatenate([a.reshape(-1) for a in arrs])
        n = flat.shape[0]
        r = -(-n // 128)
        r = -(-r // 8) * 8
        return jnp.pad(flat, (0, r * 128 - n), constant_values=1.0).reshape(r, 128)

    sd, sm, sv = _adamw_small(pack(small_w), pack(small_g), pack(small_m), pack(small_v), "adam_small")

    def unpack(buf):
        flat = buf.reshape(-1)
        outs, o = [], 0
        for a in small_w:
            outs.append(flat[o:o + a.size].reshape(a.shape))
            o += a.size
        return outs

    us_d, us_m, us_v = unpack(sd), unpack(sm), unpack(sv)

    def ordered(k, smalls):
        return (smalls[0], smalls[1], u_ffn1_in[k], u_ffn1_out[k], smalls[2], smalls[3], u_ffn2_in[k], u_ffn2_out[k],
                u_ret_in[k], smalls[4], u_ret_out[k], u_gla_in[k], smalls[5], smalls[6], smalls[7], u_gla_out[k], smalls[8])

    return (loss, grad_x, *ordered(0, small_g), *ordered(1, us_d), *ordered(2, us_m), *ordered(3, us_v))
```

```python
import functools
import math

import numpy as np
import jax
import jax.numpy as jnp
from jax import lax
from jax.experimental import pallas as pl
from jax.experimental.pallas import tpu as pltpu

F32 = jnp.float32
BF16 = jnp.bfloat16
S = jax.ShapeDtypeStruct
ANY = pl.BlockSpec(memory_space=pl.ANY)
MESH = pl.DeviceIdType.MESH

D = 1024
N_META = 16
CHUNK = 64
PAD = CHUNK - N_META
EPS = 1e-6
N_DEV = 8
FF_SHARD = 704
N_FF_CHUNK = 4
RET_H, RET_DK, RET_DV = 4, 256, 512
RET_QKV = RET_H * (2 * RET_DK + RET_DV)
RET_C = 192
GLA_H, GLA_DK, GLA_DV, GLA_RANK, GLA_TAU = 4, 128, 256, 16, 16.0
GLA_QKV = GLA_H * (2 * GLA_DK + GLA_DV)
GLA_N = 3200
GLA_ZBLK = 3072 // 128
SUB = 16
ROPE_BASE = 10000.0
ADAM_LR, ADAM_B1, ADAM_B2, ADAM_EPS, ADAM_WD, ADAM_STEP = 0.001, 0.9, 0.999, 1e-08, 0.01, 10
VMEM_LIMIT = 58 * 1024 * 1024
DW_ROWS = 2752


def _cp(**kw):
    return pltpu.CompilerParams(vmem_limit_bytes=VMEM_LIMIT, **kw)


def _row_tile(t, cap):
    best = 16
    for d in range(16, cap + 1, 16):
        if t % d == 0:
            best = d
    return best


def _sub_rows(tm, parts=2):
    units = tm // 16
    cuts = [16 * (units * p // parts) for p in range(parts + 1)]
    return [slice(a, b) for a, b in zip(cuts[:-1], cuts[1:]) if b > a]


def _dot(a, b):
    return jnp.dot(a, b, preferred_element_type=F32)


def _dot_nt(a, b):
    return lax.dot_general(a, b, (((1,), (1,)), ((), ())), preferred_element_type=F32)


def _dot_tn(a, b):
    return lax.dot_general(a, b, (((0,), (0,)), ((), ())), preferred_element_type=F32)


def _sigmoid(x):
    return pl.reciprocal(1.0 + jnp.exp(-x), approx=True)


def _rms_bwd(dxn, x, gain):
    r = lax.rsqrt(jnp.mean(x * x, axis=-1, keepdims=True) + EPS)
    xh = x * r
    dxh = dxn * gain
    dx = r * (dxh - xh * jnp.mean(dxh * xh, axis=-1, keepdims=True))
    return dx, jnp.sum(dxn * xh, axis=0, keepdims=True)


def _xyc():
    return lax.axis_index("x"), lax.axis_index("y"), lax.axis_index("c")


class _Gather:
    def __init__(self, xs):
        self.xs = list(xs)
        self.n = len(self.xs)

    def out_shape(self):
        return [S((N_DEV,) + a.shape, a.dtype) for a in self.xs]

    def scratch(self):
        return [pltpu.SemaphoreType.DMA((self.n, 7)), pltpu.SemaphoreType.DMA((self.n, 7)), pltpu.SemaphoreType.DMA((self.n,))]

    def phases(self, x_refs, out_refs, send_sems, recv_sems, local_sems):
        x, y, c = _xyc()
        me, sibling = (x, y, c), (x, y, 1 - c)
        chips = [(1 - x, y), (x, 1 - y), (1 - x, 1 - y)]

        def copy(t, k, block, to, src=None):
            px, py, pc = block
            dst = out_refs[t].at[4 * px + 2 * py + pc]
            return pltpu.make_async_remote_copy(
                src_ref=dst if src is None else src, dst_ref=dst,
                send_sem=send_sems.at[t, k], recv_sem=recv_sems.at[t, k], device_id=to, device_id_type=MESH)

        def own(t):
            return pltpu.make_async_copy(x_refs[t], out_refs[t].at[4 * x + 2 * y + c], local_sems.at[t])

        def first(t):
            return [copy(t, 0, me, sibling, src=x_refs[t])] + [
                copy(t, 1 + j, me, (*chip, c), src=x_refs[t]) for j, chip in enumerate(chips)]

        def passed(t):
            return [copy(t, 4 + j, (*chip, c), sibling) for j, chip in enumerate(chips)]

        def start():
            for t in range(self.n):
                own(t).start()
                for cp in first(t):
                    cp.start()

        def mid():
            for t in range(self.n):
                fw = passed(t)
                for j, chip in enumerate(chips):
                    copy(t, 1 + j, (*chip, c), me).wait_recv()
                    fw[j].start()

        def finish():
            for t in range(self.n):
                copy(t, 0, sibling, me).wait_recv()
                for j, chip in enumerate(chips):
                    copy(t, 4 + j, (*chip, 1 - c), me).wait_recv()
                for cp in first(t) + passed(t):
                    cp.wait_send()
                own(t).wait()

        return start, mid, finish


class _Exchange:
    def __init__(self, xs):
        self.xs = list(xs)
        self.n = len(self.xs)

    def out_shape(self):
        return [S(a.shape, a.dtype) for a in self.xs]

    def scratch(self):
        return [pltpu.SemaphoreType.DMA((self.n, 7)), pltpu.SemaphoreType.DMA((self.n, 7)), pltpu.SemaphoreType.DMA((self.n,))]

    def phases(self, g_refs, r_refs, send_sems, recv_sems, local_sems):
        x, y, c = _xyc()
        me = 4 * x + 2 * y + c

        def own(t):
            return pltpu.make_async_copy(g_refs[t].at[me], r_refs[t].at[me], local_sems.at[t])

        def send(t, m):
            px, py, pc = x ^ (m >> 2), y ^ ((m >> 1) & 1), c ^ (m & 1)
            return pltpu.make_async_remote_copy(
                src_ref=g_refs[t].at[4 * px + 2 * py + pc], dst_ref=r_refs[t].at[me],
                send_sem=send_sems.at[t, m - 1], recv_sem=recv_sems.at[t, m - 1],
                device_id=(px, py, pc), device_id_type=MESH)

        def arrival(t, m):
            peer = 4 * (x ^ (m >> 2)) + 2 * (y ^ ((m >> 1) & 1)) + (c ^ (m & 1))
            return pltpu.make_async_remote_copy(
                src_ref=g_refs[t].at[peer], dst_ref=r_refs[t].at[peer],
                send_sem=send_sems.at[t, m - 1], recv_sem=recv_sems.at[t, m - 1],
                device_id=(x, y, c), device_id_type=MESH)

        def start():
            for t in range(self.n):
                own(t).start()
            for m in range(1, N_DEV):
                for t in range(self.n):
                    send(t, m).start()

        def mid():
            pass

        def finish():
            for m in range(1, N_DEV):
                for t in range(self.n):
                    arrival(t, m).wait_recv()
            for m in range(1, N_DEV):
                for t in range(self.n):
                    send(t, m).wait_send()
            for t in range(self.n):
                own(t).wait()

        return start, mid, finish


def _run_side(side, name):
    n = side.n

    def body(*refs):
        start, mid, finish = side.phases(refs[:n], refs[n:2 * n], *refs[2 * n:])
        start()
        mid()
        finish()

    return list(pl.pallas_call(
        body, name=name, out_shape=side.out_shape(), in_specs=[ANY] * n, out_specs=[ANY] * n,
        scratch_shapes=side.scratch())(*side.xs))


def _grid_steps(grid):
    def ids():
        return [pl.program_id(a) for a in range(len(grid))]

    def first():
        return functools.reduce(jnp.logical_and, [i == 0 for i in ids()])

    def middle():
        i = ids()
        return functools.reduce(jnp.logical_and, [i[0] == (3 * grid[0]) // 4] + [j == 0 for j in i[1:]])

    def last():
        return functools.reduce(jnp.logical_and, [i == g - 1 for i, g in zip(ids(), grid)])

    return first, middle, last


def _call(body, *, name, grid, in_specs, out_specs, out_shape, scratch_shapes, operands, side=None, aliases=None):
    n_in, n_out, n_scr = len(in_specs), len(out_shape), len(scratch_shapes)
    full = body
    if side is not None:
        ns = side.n
        first, middle, last = _grid_steps(grid)

        def full(*refs):
            a = n_in
            ins, sins = refs[:a], refs[a:a + ns]
            a += ns
            outs, souts = refs[a:a + n_out], refs[a + n_out:a + n_out + ns]
            a += n_out + ns
            scr, sems = refs[a:a + n_scr], refs[a + n_scr:]
            start, mid, finish = side.phases(sins, souts, *sems)
            pl.when(first())(start)
            body(*ins, *outs, *scr)
            pl.when(middle())(mid)
            pl.when(last())(finish)

        in_specs = list(in_specs) + [ANY] * ns
        out_specs = list(out_specs) + [ANY] * ns
        out_shape = list(out_shape) + side.out_shape()
        scratch_shapes = list(scratch_shapes) + side.scratch()
        operands = list(operands) + side.xs
    outs = pl.pallas_call(
        full, name=name, grid=grid, in_specs=list(in_specs), out_specs=list(out_specs), out_shape=list(out_shape),
        scratch_shapes=list(scratch_shapes), input_output_aliases=aliases or {},
        compiler_params=_cp(dimension_semantics=("arbitrary",) * len(grid)),
    )(*operands)
    return list(outs[:n_out]), list(outs[n_out:])


def _ffn_fwd(h, gain, win, wout, name, side=None):
    t = h.shape[0]
    tm = _row_tile(t, 704)
    nt = t // tm

    def body(h_ref, g_ref, wg_ref, wu_ref, wo_ref, hn_ref, xn_ref, pg_ref, pu_ref, acc):
        c = pl.program_id(1)

        @pl.when(c == 0)
        def _():
            x = h_ref[...]
            r = lax.rsqrt(jnp.mean(x * x, axis=-1, keepdims=True) + EPS)
            xn_ref[...] = (x * r * g_ref[...]).astype(BF16)
            acc[...] = jnp.zeros_like(acc)

        wo = wo_ref[...].reshape(FF_SHARD, D)
        subs = _sub_rows(tm)
        gus = [(_dot(xn_ref[r, :], wg_ref[...]), _dot(xn_ref[r, :], wu_ref[...])) for r in subs]
        for r, (g, u) in zip(subs, gus):
            pg_ref[r, :] = g.astype(BF16)
            pu_ref[r, :] = u.astype(BF16)
            act = (g * _sigmoid(g) * u).astype(BF16)
            acc[r, :] += _dot(act, wo)

        @pl.when(c == N_FF_CHUNK - 1)
        def _():
            hn_ref[...] = h_ref[...] + 0.5 * acc[...]

    return _call(
        body, name=name, grid=(nt, N_FF_CHUNK), side=side,
        in_specs=[
            pl.BlockSpec((tm, D), lambda i, c: (i, 0)),
            pl.BlockSpec((1, D), lambda i, c: (0, 0)),
            pl.BlockSpec((None, D, FF_SHARD), lambda i, c: (c, 0, 0)),
            pl.BlockSpec((None, D, FF_SHARD), lambda i, c: (c + N_FF_CHUNK, 0, 0)),
            pl.BlockSpec((2, FF_SHARD // 2, D), lambda i, c: (c, 0, 0)),
        ],
        out_specs=[
            pl.BlockSpec((tm, D), lambda i, c: (i, 0)),
            pl.BlockSpec((tm, D), lambda i, c: (i, 0)),
            pl.BlockSpec((None, tm, FF_SHARD), lambda i, c: (c, i, 0)),
            pl.BlockSpec((None, tm, FF_SHARD), lambda i, c: (c, i, 0)),
        ],
        out_shape=[S((t, D), F32), S((t, D), BF16), S((N_FF_CHUNK, t, FF_SHARD), BF16), S((N_FF_CHUNK, t, FF_SHARD), BF16)],
        scratch_shapes=[pltpu.VMEM((tm, D), F32)],
        operands=[h, gain, win, win, wout])


def _ffn_bwd(dh, h, gain, pg, pu, win, wout, name, side=None):
    t = h.shape[0]
    tm = _row_tile(t, 704)
    nt = t // tm

    def body(dh_ref, h_ref, g_ref, pg_ref, pu_ref, wg_ref, wu_ref, wo_ref,
             dhi_ref, dob_ref, dpg_ref, dpu_ref, act_ref, dgain_ref, acc):
        i, c = pl.program_id(0), pl.program_id(1)

        @pl.when(c == 0)
        def _():
            dob_ref[...] = (0.5 * dh_ref[...]).astype(BF16)
            acc[...] = jnp.zeros_like(acc)

        @pl.when((i == 0) & (c == 0))
        def _():
            dgain_ref[...] = jnp.zeros_like(dgain_ref)

        wo = wo_ref[...].reshape(FF_SHARD, D)
        subs = _sub_rows(tm)
        dacts = [_dot_nt(dob_ref[r, :], wo) for r in subs]
        for r, dact in zip(subs, dacts):
            g = pg_ref[r, :].astype(F32)
            u = pu_ref[r, :].astype(F32)
            s = _sigmoid(g)
            sl = g * s
            act_ref[r, :] = (sl * u).astype(BF16)
            dg = (dact * u * (s * (1.0 + g * (1.0 - s)))).astype(BF16)
            du = (dact * sl).astype(BF16)
            dpg_ref[r, :] = dg
            dpu_ref[r, :] = du
            acc[r, :] += _dot_nt(dg, wg_ref[...]) + _dot_nt(du, wu_ref[...])

        @pl.when(c == N_FF_CHUNK - 1)
        def _():
            dx, dgn = _rms_bwd(acc[...], h_ref[...], g_ref[...])
            dhi_ref[...] = dh_ref[...] + dx
            dgain_ref[0:1, :] += dgn

    blk = pl.BlockSpec((None, tm, FF_SHARD), lambda i, c: (c, i, 0))
    row = pl.BlockSpec((tm, D), lambda i, c: (i, 0))
    return _call(
        body, name=name, grid=(nt, N_FF_CHUNK), side=side,
        in_specs=[
            row, row, pl.BlockSpec((1, D), lambda i, c: (0, 0)), blk, blk,
            pl.BlockSpec((None, D, FF_SHARD), lambda i, c: (c, 0, 0)),
            pl.BlockSpec((None, D, FF_SHARD), lambda i, c: (c + N_FF_CHUNK, 0, 0)),
            pl.BlockSpec((2, FF_SHARD // 2, D), lambda i, c: (c, 0, 0)),
        ],
        out_specs=[row, row, blk, blk, blk, pl.BlockSpec((8, D), lambda i, c: (0, 0))],
        out_shape=[S((t, D), F32), S((t, D), BF16)] + [S((N_FF_CHUNK, t, FF_SHARD), BF16)] * 3 + [S((8, D), F32)],
        scratch_shapes=[pltpu.VMEM((tm, D), F32)],
        operands=[dh, h, gain, pg, pu, win, win, wout])


def _ffn_dw_in(xn, dpg, dpu, name, side=None, part=(0, 1)):
    t = xn.shape[0]
    pj, pn = part
    dn = D // pn
    tk = _row_tile(t, DW_ROWS)
    nk = t // tk

    def body(a_ref, bg_ref, bu_ref, o_ref, acc):
        c, k = pl.program_id(0), pl.program_id(1)

        @pl.when(k == 0)
        def _():
            acc[...] = jnp.zeros_like(acc)

        @pl.when(c < N_FF_CHUNK)
        def _():
            acc[...] += _dot_tn(bg_ref[...], a_ref[...])

        @pl.when(c >= N_FF_CHUNK)
        def _():
            acc[...] += _dot_tn(bu_ref[...], a_ref[...])

        @pl.when(k == nk - 1)
        def _():
            o_ref[...] = acc[...].astype(BF16)

    return _call(
        body, name=name, grid=(2 * N_FF_CHUNK, nk), side=side,
        in_specs=[
            pl.BlockSpec((tk, dn), lambda c, k: (k, pj)),
            pl.BlockSpec((None, tk, FF_SHARD), lambda c, k: (jnp.minimum(c, N_FF_CHUNK - 1), k, 0)),
            pl.BlockSpec((None, tk, FF_SHARD), lambda c, k: (jnp.maximum(c - N_FF_CHUNK, 0), k, 0)),
        ],
        out_specs=[pl.BlockSpec((None, FF_SHARD, dn), lambda c, k: (c, 0, 0))],
        out_shape=[S((2 * N_FF_CHUNK, FF_SHARD, dn), BF16)],
        scratch_shapes=[pltpu.VMEM((FF_SHARD, dn), F32)],
        operands=[xn, dpg, dpu])


def _mm_tn(a, b, tn, name, tm=None, rows=DW_ROWS, shard_out=False):
    ca, t, m = a.shape
    cb, _, n = b.shape
    nc = max(ca, cb)
    tm = m if tm is None else tm
    tk = _row_tile(t, rows)
    nk = t // tk

    def body(a_ref, b_ref, o_ref, acc):
        k = pl.program_id(3)

        @pl.when(k == 0)
        def _():
            acc[...] = jnp.zeros_like(acc)

        acc[...] += _dot_tn(a_ref[...], b_ref[...])

        @pl.when(k == nk - 1)
        def _():
            o_ref[...] = acc[...].astype(BF16)

    if shard_out:
        out_spec = pl.BlockSpec((None, tm, tn), lambda c, i, j, k: (j, 0, 0))
        out_shape = S((n // tn, m, tn), BF16)
    else:
        out_spec = pl.BlockSpec((None, tm, tn), lambda c, i, j, k: (c, i, j))
        out_shape = S((nc, m, n), BF16)
    return pl.pallas_call(
        body, name=name, grid=(nc, m // tm, n // tn, nk),
        in_specs=[
            pl.BlockSpec((None, tk, tm), (lambda c, i, j, k: (c, k, i)) if ca > 1 else (lambda c, i, j, k: (0, k, i))),
            pl.BlockSpec((None, tk, tn), (lambda c, i, j, k: (c, k, j)) if cb > 1 else (lambda c, i, j, k: (0, k, j))),
        ],
        out_specs=out_spec, out_shape=out_shape,
        scratch_shapes=[pltpu.VMEM((tm, tn), F32)],
        compiler_params=_cp(dimension_semantics=("arbitrary",) * 4),
    )(a, b)


def _norm_mm(h, gain, w, tn, name, side=None):
    t = h.shape[0]
    n = w.shape[-1] if w.ndim == 2 else w.shape[0] * w.shape[2]
    tm = _row_tile(t, 704)
    kb = 1 if w.ndim == 2 else tn // w.shape[2]
    w_spec = (pl.BlockSpec((D, tn), lambda i, j: (0, j)) if w.ndim == 2
              else pl.BlockSpec((kb, D, tn // kb), lambda i, j: (j, 0, 0)))

    def body(h_ref, g_ref, w_ref, o_ref, xn_ref):
        @pl.when(pl.program_id(1) == 0)
        def _():
            x = h_ref[...]
            r = lax.rsqrt(jnp.mean(x * x, axis=-1, keepdims=True) + EPS)
            xn_ref[...] = (x * r * g_ref[...]).astype(BF16)

        if w.ndim == 2:
            o_ref[...] = _dot(xn_ref[...], w_ref[...]).astype(BF16)
        else:
            for b in range(kb):
                o_ref[:, b * (tn // kb):(b + 1) * (tn // kb)] = _dot(xn_ref[...], w_ref[b]).astype(BF16)

    return _call(
        body, name=name, grid=(t // tm, n // tn), side=side,
        in_specs=[pl.BlockSpec((tm, D), lambda i, j: (i, 0)), pl.BlockSpec((1, D), lambda i, j: (0, 0)), w_spec],
        out_specs=[pl.BlockSpec((tm, tn), lambda i, j: (i, j)), pl.BlockSpec((tm, D), lambda i, j: (i, 0))],
        out_shape=[S((t, n), BF16), S((t, D), BF16)], scratch_shapes=[],
        operands=[h, gain, w])


def _proj_bwd(dproj, w, dh, h, gain, tk, name, side=None):
    t, n = dproj.shape
    tm = _row_tile(t, 704)
    nk = n // tk
    kb = 1 if w.ndim == 2 else tk // w.shape[2]
    w_spec = (pl.BlockSpec((D, tk), lambda i, k: (0, k)) if w.ndim == 2
              else pl.BlockSpec((kb, D, tk // kb), lambda i, k: (k, 0, 0)))

    def body(dp_ref, w_ref, dh_ref, h_ref, g_ref, dhi_ref, dgain_ref, acc):
        i, k = pl.program_id(0), pl.program_id(1)

        @pl.when(k == 0)
        def _():
            acc[...] = jnp.zeros_like(acc)

        @pl.when((i == 0) & (k == 0))
        def _():
            dgain_ref[...] = jnp.zeros_like(dgain_ref)

        if w.ndim == 2:
            acc[...] += _dot_nt(dp_ref[...], w_ref[...])
        else:
            for b in range(kb):
                acc[...] += _dot_nt(dp_ref[:, b * (tk // kb):(b + 1) * (tk // kb)], w_ref[b])

        @pl.when(k == nk - 1)
        def _():
            dx, dgn = _rms_bwd(acc[...], h_ref[...], g_ref[...])
            dhi_ref[...] = dh_ref[...] + dx
            dgain_ref[0:1, :] += dgn

    row = pl.BlockSpec((tm, D), lambda i, k: (i, 0))
    return _call(
        body, name=name, grid=(t // tm, nk), side=side,
        in_specs=[pl.BlockSpec((tm, tk), lambda i, k: (i, k)), w_spec,
                  row, row, pl.BlockSpec((1, D), lambda i, k: (0, 0))],
        out_specs=[row, pl.BlockSpec((8, D), lambda i, k: (0, 0))],
        out_shape=[S((t, D), F32), S((8, D), F32)],
        scratch_shapes=[pltpu.VMEM((tm, D), F32)],
        operands=[dproj, w, dh, h, gain])


def _post_fwd(o, proj, hgain, wout, h, nh, dv, name, side=None):
    t = h.shape[0]
    w = nh * dv
    tm = _row_tile(t, 704)

    def body(o_ref, g_ref, hg_ref, wo_ref, h_ref, hn_ref, og_ref):
        for hd in range(nh):
            sl = slice(hd * dv, (hd + 1) * dv)
            oh = o_ref[:, sl].astype(F32)
            r = lax.rsqrt(jnp.mean(oh * oh, axis=-1, keepdims=True) + EPS)
            gg = g_ref[:, sl].astype(F32)
            og_ref[:, sl] = (oh * r * hg_ref[:, sl] * (gg * _sigmoid(gg))).astype(BF16)
        hn_ref[...] = h_ref[...] + _dot(og_ref[...], wo_ref[...])

    return _call(
        body, name=name, grid=(t // tm,), side=side,
        in_specs=[pl.BlockSpec((tm, w), lambda i: (i, 0)), pl.BlockSpec((tm, w), lambda i: (i, 2)),
                  pl.BlockSpec((1, w), lambda i: (0, 0)), pl.BlockSpec((w, D), lambda i: (0, 0)),
                  pl.BlockSpec((tm, D), lambda i: (i, 0))],
        out_specs=[pl.BlockSpec((tm, D), lambda i: (i, 0)), pl.BlockSpec((tm, w), lambda i: (i, 0))],
        out_shape=[S((t, D), F32), S((t, w), BF16)], scratch_shapes=[],
        operands=[o, proj, hgain, wout, h])


def _post_bwd(dh, o, proj, hgain, wout, nh, dv, nproj, name, side=None):
    t = dh.shape[0]
    w = nh * dv
    tm = _row_tile(t, 704)

    def body(dh_ref, o_ref, g_ref, hg_ref, wo_ref, do_ref, dg_ref, dhb_ref, dhg_ref):
        @pl.when(pl.program_id(0) == 0)
        def _():
            dhg_ref[...] = jnp.zeros_like(dhg_ref)

        dmix = dh_ref[...].astype(BF16)
        dhb_ref[...] = dmix
        dog = _dot_nt(dmix, wo_ref[...])
        for hd in range(nh):
            sl = slice(hd * dv, (hd + 1) * dv)
            oh = o_ref[:, sl].astype(F32)
            r = lax.rsqrt(jnp.mean(oh * oh, axis=-1, keepdims=True) + EPS)
            xh = oh * r
            gain = hg_ref[:, sl]
            gg = g_ref[:, sl].astype(F32)
            s = _sigmoid(gg)
            dogh = dog[:, sl]
            don = dogh * (gg * s)
            dg_ref[:, sl] = (dogh * (xh * gain) * (s * (1.0 + gg * (1.0 - s)))).astype(BF16)
            dxh = don * gain
            do_ref[:, sl] = (r * (dxh - xh * jnp.mean(dxh * xh, axis=-1, keepdims=True))).astype(BF16)
            dhg_ref[0:1, sl] += jnp.sum(don * xh, axis=0, keepdims=True)

    return _call(
        body, name=name, grid=(t // tm,), side=side,
        in_specs=[pl.BlockSpec((tm, D), lambda i: (i, 0)), pl.BlockSpec((tm, w), lambda i: (i, 0)),
                  pl.BlockSpec((tm, w), lambda i: (i, 2)), pl.BlockSpec((1, w), lambda i: (0, 0)),
                  pl.BlockSpec((w, D), lambda i: (0, 0))],
        out_specs=[pl.BlockSpec((tm, w), lambda i: (i, 0)), pl.BlockSpec((tm, w), lambda i: (i, 2)),
                   pl.BlockSpec((tm, D), lambda i: (i, 0)), pl.BlockSpec((8, w), lambda i: (0, 0))],
        out_shape=[S((t, w), BF16), S((t, nproj), BF16), S((t, D), BF16), S((8, w), F32)], scratch_shapes=[],
        operands=[dh, o, proj, hgain, wout])


def _ret_consts():
    lg = np.log1p(-np.exp2(-5.0 - np.arange(RET_H, dtype=np.float32))).astype(np.float32)
    return jnp.asarray(np.broadcast_to(lg[:, None, None], (RET_H, 1, 128)).copy())


def _rope_tables(t):
    half = RET_DK // 2
    inv = 1.0 / (ROPE_BASE ** jnp.linspace(0.0, 1.0, half, dtype=F32))
    base = (jnp.arange(t // CHUNK) * CHUNK - PAD).astype(F32)[:, None] * inv[None, :]
    off = jnp.arange(CHUNK).astype(F32)[:, None] * inv[None, :]
    ca, sa = jnp.cos(base)[:, None, :], jnp.sin(base)[:, None, :]
    cb, sb = jnp.cos(off)[None], jnp.sin(off)[None]
    return (ca * cb - sa * sb).reshape(t, half), (sa * cb + ca * sb).reshape(t, half)


def _ret_chunk(blk_ref, cos_ref, sin_ref, lg, h):
    c = RET_C
    half = RET_DK // 2
    oq, ok, ov = h * RET_DK, RET_H * RET_DK + h * RET_DK, 2 * RET_H * RET_DK + h * RET_DV
    cs, sn = cos_ref[...], sin_ref[...]
    q1, q2 = blk_ref[:, oq:oq + half].astype(F32), blk_ref[:, oq + half:oq + RET_DK].astype(F32)
    k1, k2 = blk_ref[:, ok:ok + half].astype(F32), blk_ref[:, ok + half:ok + RET_DK].astype(F32)
    qr = jnp.concatenate([q1 * cs - q2 * sn, q1 * sn + q2 * cs], axis=1)
    kr = jnp.concatenate([k1 * cs - k2 * sn, k1 * sn + k2 * cs], axis=1) * (RET_DK ** -0.5)
    v = blk_ref[:, ov:ov + RET_DV]
    ii = lax.broadcasted_iota(jnp.int32, (c, 1), 0).astype(F32)
    jj = lax.broadcasted_iota(jnp.int32, (1, c), 1).astype(F32)
    rel = ii - jj
    dmat = jnp.where(rel >= 0, jnp.exp(lg * jnp.maximum(rel, 0.0)), 0.0)
    dq = jnp.exp(lg * (ii + 1.0))
    dk = jnp.exp(lg * (c - 1.0 - ii))
    dchunk = jnp.exp(lg * float(c))
    return qr, kr, v, dmat, dq, dk, dchunk


def _ret_scan_fwd(proj, cos, sin, lgam, name):
    t = proj.shape[0]
    c = RET_C
    nc = t // c

    def body(blk_ref, cos_ref, sin_ref, lg_ref, o_ref, st_ref, state):
        @pl.when(pl.program_id(0) == 0)
        def _():
            state[...] = jnp.zeros_like(state)

        for h in range(RET_H):
            qr, kr, v, dmat, dq, dk, dchunk = _ret_chunk(blk_ref, cos_ref, sin_ref, lg_ref[h, :, 0:1], h)
            sp = state[h]
            st_ref[h] = sp.astype(BF16)
            scores = _dot_nt(qr.astype(BF16), kr.astype(BF16)) * dmat
            o = _dot(scores.astype(BF16), v) + _dot((qr * dq).astype(BF16), sp.astype(BF16))
            o_ref[:, h * RET_DV:(h + 1) * RET_DV] = o.astype(BF16)
            state[h] = sp * dchunk + _dot_tn((kr * dk).astype(BF16), v)

    return pl.pallas_call(
        body, name=name, grid=(nc,),
        in_specs=[pl.BlockSpec((c, RET_QKV), lambda n: (n, 0)), pl.BlockSpec((c, 128), lambda n: (n, 0)),
                  pl.BlockSpec((c, 128), lambda n: (n, 0)), pl.BlockSpec((RET_H, 1, 128), lambda n: (0, 0, 0))],
        out_specs=[pl.BlockSpec((c, RET_H * RET_DV), lambda n: (n, 0)),
                   pl.BlockSpec((RET_H, None, RET_DK, RET_DV), lambda n: (0, n, 0, 0))],
        out_shape=[S((t, RET_H * RET_DV), BF16), S((RET_H, nc, RET_DK, RET_DV), BF16)],
        scratch_shapes=[pltpu.VMEM((RET_H, RET_DK, RET_DV), F32)],
        compiler_params=_cp(dimension_semantics=("arbitrary",)),
    )(proj, cos, sin, lgam)


def _ret_scan_bwd(proj, cos, sin, lgam, do, states, dproj, name, side=None):
    t = proj.shape[0]
    c = RET_C
    nc = t // c
    half = RET_DK // 2

    def body(blk_ref, cos_ref, sin_ref, lg_ref, do_ref, st_ref, dp_in, dp_ref, dstate):
        n = nc - 1 - pl.program_id(0)

        @pl.when(pl.program_id(0) == 0)
        def _():
            dstate[...] = jnp.zeros_like(dstate)

        cs, sn = cos_ref[...], sin_ref[...]
        rows = n * c + lax.broadcasted_iota(jnp.int32, (c, 1), 0)
        keep = rows >= PAD

        def unrot(d):
            d1, d2 = d[:, :half], d[:, half:]
            return jnp.concatenate([d1 * cs + d2 * sn, d2 * cs - d1 * sn], axis=1)

        for h in range(RET_H):
            qr, kr, v, dmat, dq, dk, dchunk = _ret_chunk(blk_ref, cos_ref, sin_ref, lg_ref[h, :, 0:1], h)
            qb, kb = qr.astype(BF16), kr.astype(BF16)
            dob = do_ref[:, h * RET_DV:(h + 1) * RET_DV]
            sp = st_ref[h]
            ds = dstate[h]
            dsb = ds.astype(BF16)
            p = (_dot_nt(qb, kb) * dmat).astype(BF16)
            dvv = _dot_tn(p, dob) + _dot((kr * dk).astype(BF16), dsb)
            dp = (_dot_nt(dob, v) * dmat).astype(BF16)
            dqr = _dot(dp, kb) + _dot_nt(dob, sp) * dq
            dkr = (_dot_tn(dp, qb) + _dot_nt(v, dsb) * dk) * (RET_DK ** -0.5)
            dstate[h] = ds * dchunk + _dot_tn((qr * dq).astype(BF16), dob)
            oq, ok, ov = h * RET_DK, RET_H * RET_DK + h * RET_DK, 2 * RET_H * RET_DK + h * RET_DV
            dp_ref[:, oq:oq + RET_DK] = jnp.where(keep, unrot(dqr), 0.0).astype(BF16)
            dp_ref[:, ok:ok + RET_DK] = jnp.where(keep, unrot(dkr), 0.0).astype(BF16)
            dp_ref[:, ov:ov + RET_DV] = jnp.where(keep, dvv, 0.0).astype(BF16)

    return _call(
        body, name=name, grid=(nc,), side=side, aliases={6: 0},
        in_specs=[pl.BlockSpec((c, RET_QKV), lambda n: (nc - 1 - n, 0)), pl.BlockSpec((c, 128), lambda n: (nc - 1 - n, 0)),
                  pl.BlockSpec((c, 128), lambda n: (nc - 1 - n, 0)), pl.BlockSpec((RET_H, 1, 128), lambda n: (0, 0, 0)),
                  pl.BlockSpec((c, RET_H * RET_DV), lambda n: (nc - 1 - n, 0)),
                  pl.BlockSpec((RET_H, None, RET_DK, RET_DV), lambda n: (0, nc - 1 - n, 0, 0)), ANY],
        out_specs=[pl.BlockSpec((c, RET_QKV), lambda n: (nc - 1 - n, 0))],
        out_shape=[S((t, dproj.shape[1]), BF16)],
        scratch_shapes=[pltpu.VMEM((RET_H, RET_DK, RET_DV), F32)],
        operands=[proj, cos, sin, lgam, do, states, dproj])


def _split3(x):
    hi = x.astype(BF16)
    r1 = x - hi.astype(F32)
    mid = r1.astype(BF16)
    lo = (r1 - mid.astype(F32)).astype(BF16)
    return hi, mid, lo


def _gla_chunk(blk_ref, z_ref, wg_ref, bg_ref, n, h):
    c = CHUNK
    oq, ok, ov = h * GLA_DK, GLA_H * GLA_DK + h * GLA_DK, 2 * GLA_H * GLA_DK + h * GLA_DV
    q = blk_ref[:, oq:oq + GLA_DK].astype(F32) * (GLA_DK ** -0.5)
    k = blk_ref[:, ok:ok + GLA_DK].astype(F32)
    v = blk_ref[:, ov:ov + GLA_DV]
    hs = slice(h * GLA_DK, (h + 1) * GLA_DK)
    u = _dot(z_ref[...], wg_ref[:, hs]) + bg_ref[:, hs]
    la = (jnp.minimum(u, 0.0) - jnp.log(1.0 + jnp.exp(-jnp.abs(u)))) * (1.0 / GLA_TAU)
    rows = n * c + lax.broadcasted_iota(jnp.int32, (c, 1), 0)
    keep = rows >= PAD
    la = jnp.where(keep, la, 0.0)
    ii = lax.broadcasted_iota(jnp.int32, (c, c), 0)
    jj = lax.broadcasted_iota(jnp.int32, (c, c), 1)
    tril = (ii >= jj).astype(BF16)
    hi, mid, lo = _split3(la)
    b = _dot(tril, hi) + _dot(tril, mid) + _dot(tril, lo)
    return q, k, v, u, b, keep


def _gla_intra(qs, ks, bs, a_ref):
    c = CHUNK
    nh = len(qs)
    col = lax.broadcasted_iota(jnp.int32, (1, c), 1)
    rowi = lax.broadcasted_iota(jnp.int32, (SUB, 1), 0)
    for blk in range(c // SUB):
        r = slice(SUB * blk, SUB * (blk + 1))
        arows = []
        for h in range(nh):
            q, k, b = qs[h], ks[h], bs[h]
            if blk > 0:
                bprev = b[SUB * blk - 1:SUB * blk]
                qe = q[r] * jnp.exp(b[r] - bprev)
                kt = k * jnp.exp(jnp.minimum(bprev - b, 0.0))
                arows.append(jnp.where(col < SUB * blk, _dot_nt(qe.astype(BF16), kt.astype(BF16)), 0.0))
            else:
                arows.append(jnp.zeros((SUB, c), F32))
        for j in range(SUB):
            for h in range(nh):
                b_i = bs[h][r]
                e = jnp.exp(b_i - b_i[j:j + 1])
                a = jnp.sum(qs[h][r] * ks[h][r][j:j + 1] * e, axis=1, keepdims=True)
                arows[h] = jnp.where(col == SUB * blk + j, a, arows[h])
        for h in range(nh):
            a_ref[h, r, :] = jnp.where(col - SUB * blk <= rowi, arows[h], 0.0)


def _gla_scan_fwd(proj, wgp, bg, name, side=None):
    t = proj.shape[0]
    c = CHUNK
    nc = t // c
    heads = range(GLA_H)

    def body(blk_ref, z_ref, wg_ref, bg_ref, o_ref, st_ref, am_ref, state, a_ref):
        n = pl.program_id(0)

        @pl.when(n == 0)
        def _():
            state[...] = jnp.zeros_like(state)

        qs, ks, vs, us, bs, keeps = zip(*[_gla_chunk(blk_ref, z_ref, wg_ref, bg_ref, n, h) for h in heads])
        _gla_intra(qs, ks, bs, a_ref)
        for h in heads:
            q, k, v, b = qs[h], ks[h], vs[h], bs[h]
            sp = state[h]
            st_ref[h] = sp.astype(BF16)
            ab = a_ref[h].astype(BF16)
            am_ref[:, h * c:(h + 1) * c] = ab
            o = _dot(ab, v) + _dot_nt((q * jnp.exp(b)).astype(BF16), sp.astype(BF16))
            o_ref[:, h * GLA_DV:(h + 1) * GLA_DV] = o.astype(BF16)
            bc = b[c - 1:c]
            state[h] = sp * jnp.exp(bc) + _dot_tn(v, (k * jnp.exp(bc - b)).astype(BF16))

    return _call(
        body, name=name, grid=(nc,), side=side,
        in_specs=[pl.BlockSpec((c, GLA_QKV), lambda n: (n, 0)), pl.BlockSpec((c, 128), lambda n: (n, GLA_ZBLK)),
                  pl.BlockSpec((128, GLA_H * GLA_DK), lambda n: (0, 0)), pl.BlockSpec((1, GLA_H * GLA_DK), lambda n: (0, 0))],
        out_specs=[pl.BlockSpec((c, GLA_H * GLA_DV), lambda n: (n, 0)),
                   pl.BlockSpec((GLA_H, None, GLA_DV, GLA_DK), lambda n: (0, n, 0, 0)),
                   pl.BlockSpec((c, GLA_H * c), lambda n: (n, 0))],
        out_shape=[S((t, GLA_H * GLA_DV), BF16), S((GLA_H, nc, GLA_DV, GLA_DK), BF16), S((t, GLA_H * c), BF16)],
        scratch_shapes=[pltpu.VMEM((GLA_H, GLA_DV, GLA_DK), F32), pltpu.VMEM((GLA_H, c, c), F32)],
        operands=[proj, proj, wgp, bg])


def _gla_scan_bwd(proj, wgp, bg, do, states, amat, dproj, name):
    t = proj.shape[0]
    c = CHUNK
    nc = t // c
    heads = range(GLA_H)

    def body(blk_ref, z_ref, wg_ref, bg_ref, do_ref, st_ref, am_ref, dp_in, dp_ref, du_ref, dstate, dq_ref, dkd_ref):
        n = nc - 1 - pl.program_id(0)

        @pl.when(pl.program_id(0) == 0)
        def _():
            dstate[...] = jnp.zeros_like(dstate)

        qs, ks, vs, us, bs, keeps = zip(*[_gla_chunk(blk_ref, z_ref, wg_ref, bg_ref, n, h) for h in heads])
        ii = lax.broadcasted_iota(jnp.int32, (c, c), 0)
        jj = lax.broadcasted_iota(jnp.int32, (c, c), 1)
        col = lax.broadcasted_iota(jnp.int32, (1, c), 1)
        rowi = lax.broadcasted_iota(jnp.int32, (SUB, 1), 0)
        rowc = lax.broadcasted_iota(jnp.int32, (c, 1), 0)
        das, dvs, dq_inters, dk_states, extras, dks = [], [], [], [], [], []
        for h in heads:
            q, k, v, b = qs[h], ks[h], vs[h], bs[h]
            ab = am_ref[:, h * c:(h + 1) * c]
            dob = do_ref[:, h * GLA_DV:(h + 1) * GLA_DV]
            sp = st_ref[h]
            ds = dstate[h]
            dsb = ds.astype(BF16)
            bc = b[c - 1:c]
            eb = jnp.exp(b)
            ebc = jnp.exp(bc - b)
            ec = jnp.exp(bc)
            qb = (q * eb).astype(BF16)
            kb = (k * ebc).astype(BF16)
            dvs.append(_dot_tn(ab, dob) + _dot_nt(kb, dsb))
            das.append(jnp.where(ii >= jj, _dot_nt(dob, v), 0.0))
            dq_inters.append(_dot(dob, sp) * eb)
            dk_state = _dot(v, dsb) * ebc
            dk_states.append(dk_state)
            extras.append(jnp.sum(k * dk_state, axis=0, keepdims=True)
                          + ec * jnp.sum(sp.astype(F32) * ds, axis=0, keepdims=True))
            dstate[h] = ds * ec + _dot_tn(dob, qb)
            dks.append(jnp.zeros((c, GLA_DK), F32))

        for blk in range(c // SUB):
            r = slice(SUB * blk, SUB * (blk + 1))
            dq_is, dkds = [], []
            for h in heads:
                q, k, b = qs[h], ks[h], bs[h]
                if blk > 0:
                    bprev = b[SUB * blk - 1:SUB * blk]
                    e_i = jnp.exp(b[r] - bprev)
                    ek = jnp.exp(jnp.minimum(bprev - b, 0.0))
                    daoff = jnp.where(col < SUB * blk, das[h][r], 0.0).astype(BF16)
                    dq_is.append(_dot(daoff, (k * ek).astype(BF16)) * e_i)
                    dks[h] = dks[h] + _dot_tn(daoff, (q[r] * e_i).astype(BF16)) * ek
                else:
                    dq_is.append(jnp.zeros((SUB, GLA_DK), F32))
                dkds.append(jnp.zeros((SUB, GLA_DK), F32))
            for j in range(SUB):
                for h in heads:
                    b_i = bs[h][r]
                    e = jnp.where(rowi >= j, jnp.exp(b_i - b_i[j:j + 1]), 0.0)
                    dacol = jnp.sum(jnp.where(col == SUB * blk + j, das[h][r], 0.0), axis=1, keepdims=True)
                    tt = dacol * e
                    dq_is[h] = dq_is[h] + tt * ks[h][r][j:j + 1]
                    dkds[h] = jnp.where(rowi == j, jnp.sum(tt * qs[h][r], axis=0, keepdims=True), dkds[h])
            for h in heads:
                dq_ref[h, r, :] = dq_is[h]
                dkd_ref[h, r, :] = dkds[h]

        for h in heads:
            q, k, b, u, keep = qs[h], ks[h], bs[h], us[h], keeps[h]
            dq = dq_ref[h] + dq_inters[h]
            dk = dks[h] + dkd_ref[h] + dk_states[h]
            db = q * dq - k * dk + jnp.where(rowc == c - 1, extras[h], 0.0)
            triu = (ii <= jj).astype(BF16)
            hi, mid, lo = _split3(db)
            dla = _dot(triu, hi) + _dot(triu, mid) + _dot(triu, lo)
            du = jnp.where(keep, dla * (1.0 / GLA_TAU) / (1.0 + jnp.exp(u)), 0.0)
            du_ref[:, h * GLA_DK:(h + 1) * GLA_DK] = du.astype(BF16)
            oq, ok, ov = h * GLA_DK, GLA_H * GLA_DK + h * GLA_DK, 2 * GLA_H * GLA_DK + h * GLA_DV
            dp_ref[:, oq:oq + GLA_DK] = jnp.where(keep, dq * (GLA_DK ** -0.5), 0.0).astype(BF16)
            dp_ref[:, ok:ok + GLA_DK] = jnp.where(keep, dk, 0.0).astype(BF16)
            dp_ref[:, ov:ov + GLA_DV] = jnp.where(keep, dvs[h], 0.0).astype(BF16)

    nproj = dproj.shape[1]
    return pl.pallas_call(
        body, name=name, grid=(nc,),
        in_specs=[pl.BlockSpec((c, GLA_QKV), lambda n: (nc - 1 - n, 0)), pl.BlockSpec((c, 128), lambda n: (nc - 1 - n, GLA_ZBLK)),
                  pl.BlockSpec((128, GLA_H * GLA_DK), lambda n: (0, 0)), pl.BlockSpec((1, GLA_H * GLA_DK), lambda n: (0, 0)),
                  pl.BlockSpec((c, GLA_H * GLA_DV), lambda n: (nc - 1 - n, 0)),
                  pl.BlockSpec((GLA_H, None, GLA_DV, GLA_DK), lambda n: (0, nc - 1 - n, 0, 0)),
                  pl.BlockSpec((c, GLA_H * c), lambda n: (nc - 1 - n, 0)), ANY],
        out_specs=[pl.BlockSpec((c, GLA_QKV), lambda n: (nc - 1 - n, 0)),
                   pl.BlockSpec((c, GLA_H * GLA_DK), lambda n: (nc - 1 - n, 0))],
        out_shape=[S((t, nproj), BF16), S((t, GLA_H * GLA_DK), BF16)],
        input_output_aliases={7: 0},
        scratch_shapes=[pltpu.VMEM((GLA_H, GLA_DV, GLA_DK), F32),
                        pltpu.VMEM((GLA_H, c, GLA_DK), F32), pltpu.VMEM((GLA_H, c, GLA_DK), F32)],
        compiler_params=_cp(dimension_semantics=("arbitrary",)),
    )(proj, proj, wgp, bg, do, states, amat, dproj)


def _gla_gate_bwd(du, proj, wgp, dproj, name):
    t = du.shape[0]
    tm = _row_tile(t, 704)
    w = GLA_H * GLA_DK

    def body(du_ref, z_ref, wg_ref, dp_in, dp_ref, dwg_ref, dbg_ref):
        @pl.when(pl.program_id(0) == 0)
        def _():
            dwg_ref[...] = jnp.zeros_like(dwg_ref)
            dbg_ref[...] = jnp.zeros_like(dbg_ref)

        d = du_ref[...]
        dp_ref[...] = _dot_nt(d, wg_ref[...]).astype(BF16)
        dwg_ref[...] += _dot_tn(z_ref[...], d)
        dbg_ref[0:1, :] += jnp.sum(d.astype(F32), axis=0, keepdims=True)

    return pl.pallas_call(
        body, name=name, grid=(t // tm,),
        in_specs=[pl.BlockSpec((tm, w), lambda i: (i, 0)), pl.BlockSpec((tm, 128), lambda i: (i, GLA_ZBLK)),
                  pl.BlockSpec((128, w), lambda i: (0, 0)), ANY],
        out_specs=[pl.BlockSpec((tm, 128), lambda i: (i, GLA_ZBLK)), pl.BlockSpec((128, w), lambda i: (0, 0)),
                   pl.BlockSpec((8, w), lambda i: (0, 0))],
        out_shape=[S(dproj.shape, BF16), S((128, w), F32), S((8, w), F32)],
        input_output_aliases={3: 0},
        compiler_params=_cp(dimension_semantics=("arbitrary",)),
    )(du, proj, wgp, dproj)


def _final_loss(h, gain, target, name):
    t = h.shape[0]
    tm = _row_tile(t, 704)

    def body(h_ref, g_ref, t_ref, dh_ref, dgain_ref, loss_ref):
        i = pl.program_id(0)

        @pl.when(i == 0)
        def _():
            dgain_ref[...] = jnp.zeros_like(dgain_ref)
            loss_ref[...] = jnp.zeros_like(loss_ref)

        x = h_ref[...]
        gain = g_ref[...]
        r = lax.rsqrt(jnp.mean(x * x, axis=-1, keepdims=True) + EPS)
        xh = x * r
        rows = i * tm + lax.broadcasted_iota(jnp.int32, (tm, 1), 0)
        e = jnp.where(rows >= CHUNK, xh * gain - t_ref[...], 0.0)
        loss_ref[...] += 0.5 * jnp.sum(jnp.mean(e * e, axis=-1, keepdims=True), axis=0, keepdims=True)
        dy = e * (1.0 / D)
        dgain_ref[0:1, :] += jnp.sum(dy * xh, axis=0, keepdims=True)
        dxh = dy * gain
        dh_ref[...] = r * (dxh - xh * jnp.mean(dxh * xh, axis=-1, keepdims=True))

    row = pl.BlockSpec((tm, D), lambda i: (i, 0))
    return pl.pallas_call(
        body, name=name, grid=(t // tm,),
        in_specs=[row, pl.BlockSpec((1, D), lambda i: (0, 0)), row],
        out_specs=[row, pl.BlockSpec((8, D), lambda i: (0, 0)), pl.BlockSpec((8, 128), lambda i: (0, 0))],
        out_shape=[S((t, D), F32), S((8, D), F32), S((8, 128), F32)],
        compiler_params=_cp(dimension_semantics=("arbitrary",)),
    )(h, gain, target)


def _adam_math(w, g, m, v):
    m2 = ADAM_B1 * m + (1.0 - ADAM_B1) * g
    v2 = ADAM_B2 * v + (1.0 - ADAM_B2) * (g * g)
    m_hat = m2 / (1.0 - ADAM_B1 ** ADAM_STEP)
    v_hat = v2 / (1.0 - ADAM_B2 ** ADAM_STEP)
    delta = -ADAM_LR * (m_hat / (jnp.sqrt(v_hat) + ADAM_EPS) + ADAM_WD * w)
    return delta, m2, v2


def _adamw_reduce(recvs, w, m, v, name):
    nl, r, wd = w.shape
    tr = _row_tile(r, 256) if r % 16 == 0 else r
    nr = r // tr

    def body(*refs):
        rv_refs = refs[:nl]
        w_ref, m_ref, v_ref, g_ref, d_ref, m2_ref, v2_ref = refs[nl:]
        layer = pl.program_id(0)

        def total(rv_ref):
            g = rv_ref[0].astype(F32)
            for s in range(1, N_DEV):
                g = g + rv_ref[s].astype(F32)
            return g

        g = total(rv_refs[0])
        for k in range(1, nl):
            g = jnp.where(layer == k, total(rv_refs[k]), g)
        g_ref[...] = g
        d_ref[...], m2_ref[...], v2_ref[...] = _adam_math(w_ref[...], g, m_ref[...], v_ref[...])

    def rv_spec(k):
        return pl.BlockSpec((N_DEV, tr, wd), lambda l, i: (0, jnp.where(l == k, i, jnp.where(l < k, 0, nr - 1)), 0))

    row = pl.BlockSpec((None, tr, wd), lambda l, i: (l, i, 0))
    return pl.pallas_call(
        body, name=name, grid=(nl, nr),
        in_specs=[rv_spec(k) for k in range(nl)] + [row, row, row],
        out_specs=[row] * 4, out_shape=[S((nl, r, wd), F32)] * 4,
        compiler_params=_cp(dimension_semantics=("arbitrary", "arbitrary")),
    )(*recvs, w, m, v)


def _small_reduce(parts, name):
    _, r, wd = parts.shape

    def body(p_ref, o_ref):
        g = p_ref[0]
        for s in range(1, N_DEV):
            g = g + p_ref[s]
        o_ref[...] = g

    return pl.pallas_call(body, name=name, out_shape=S((r, wd), F32), compiler_params=_cp())(parts)


def _adamw_small(w, g, m, v, name):
    def body(w_ref, g_ref, m_ref, v_ref, d_ref, m2_ref, v2_ref):
        d_ref[...], m2_ref[...], v2_ref[...] = _adam_math(w_ref[...], g_ref[...], m_ref[...], v_ref[...])

    return pl.pallas_call(body, name=name, out_shape=[S(w.shape, F32)] * 3, compiler_params=_cp())(w, g, m, v)


def _unshard_cols(g):
    return jnp.transpose(g, (1, 0, 2)).reshape(g.shape[1], N_DEV * g.shape[2])


def _my_cols(full, width):
    me = 4 * lax.axis_index("x") + 2 * lax.axis_index("y") + lax.axis_index("c")
    return lax.dynamic_slice_in_dim(full, me * width, width, axis=1)


def kernel(x, meta_tokens, norm_ffn1, ffn1_w_in, ffn1_w_out, norm_mix, norm_ffn2, ffn2_w_in, ffn2_w_out, ret_w_in, ret_head_norm, ret_w_out, gla_w_in, gla_w_gate, gla_b_gate, gla_head_norm, gla_w_out, final_norm, loss_target, m_meta_tokens, m_norm_ffn1, m_ffn1_w_in, m_ffn1_w_out, m_norm_mix, m_norm_ffn2, m_ffn2_w_in, m_ffn2_w_out, m_ret_w_in, m_ret_head_norm, m_ret_w_out, m_gla_w_in, m_gla_w_gate, m_gla_b_gate, m_gla_head_norm, m_gla_w_out, m_final_norm, v_meta_tokens, v_norm_ffn1, v_ffn1_w_in, v_ffn1_w_out, v_norm_mix, v_norm_ffn2, v_ffn2_w_in, v_ffn2_w_out, v_ret_w_in, v_ret_head_norm, v_ret_w_out, v_gla_w_in, v_gla_w_gate, v_gla_b_gate, v_gla_head_norm, v_gla_w_out, v_final_norm):
    seq = x.shape[1]
    t = seq + CHUNK
    xs = x[0]
    target = loss_target[0]

    def ffn_w(f):
        w_in, w_out = (ffn1_w_in, ffn1_w_out) if f < 2 else (ffn2_w_in, ffn2_w_out)
        return [w_in[f % 2].astype(BF16), w_out[f % 2].astype(BF16)]

    small = jnp.concatenate([meta_tokens.reshape(-1), ret_head_norm.reshape(-1), gla_w_gate.reshape(-1),
                             gla_b_gate.reshape(-1), gla_head_norm.reshape(-1)])
    n_small = small.shape[0]
    small = jnp.pad(small, (0, 32 * 128 - n_small)).reshape(32, 128)
    sg, win0, wout0 = _run_side(_Gather([small] + ffn_w(0)), "ag_first")
    sg = sg.reshape(N_DEV, 32 * 128)

    def small_cols(off, rows, width):
        return jnp.transpose(sg[:, off:off + rows * width].reshape(N_DEV, rows, width), (1, 0, 2)).reshape(rows, N_DEV * width)

    off = 0
    meta_full = small_cols(off, N_META, D // N_DEV); off += N_META * (D // N_DEV)
    ret_hn = small_cols(off, RET_H, RET_DV // N_DEV).reshape(1, RET_H * RET_DV); off += RET_H * RET_DV // N_DEV
    wgate = small_cols(off, GLA_RANK, GLA_H * GLA_DK // N_DEV); off += GLA_RANK * GLA_H * GLA_DK // N_DEV
    bgate = small_cols(off, 1, GLA_H * GLA_DK // N_DEV); off += GLA_H * GLA_DK // N_DEV
    gla_hn = small_cols(off, GLA_H, GLA_DV // N_DEV).reshape(1, GLA_H * GLA_DV)
    wgp = jnp.pad(wgate, ((0, 128 - GLA_RANK), (0, 0))).astype(BF16)

    cos, sin = _rope_tables(t)
    lgam = _ret_consts()

    h0 = jnp.concatenate([jnp.zeros((PAD, D), F32), meta_full, xs], axis=0)
    g1 = [norm_ffn1[i:i + 1] for i in range(2)]
    gm = [norm_mix[i:i + 1] for i in range(2)]
    g2 = [norm_ffn2[i:i + 1] for i in range(2)]

    (h1, xn_a0, pg_a0, pu_a0), (ret_win_g, ret_wout_g) = _ffn_fwd(
        h0, g1[0], win0, wout0, "ffn1_l0_fwd", side=_Gather([ret_w_in[0].astype(BF16), ret_w_out[0].astype(BF16)]))
    ret_win = ret_win_g
    ret_wout = ret_wout_g.reshape(RET_H * RET_DV, D)
    (rproj, rhn), (win2,) = _norm_mm(h1, gm[0], ret_win, 4 * ret_win.shape[2], "ret_proj_fwd", side=_Gather(ffn_w(2)[:1]))
    ro, rstates = _ret_scan_fwd(rproj, cos, sin, lgam, "ret_scan_fwd")
    (h2, rog), (wout2,) = _post_fwd(ro, rproj, ret_hn, ret_wout, h1, RET_H, RET_DV, "ret_post_fwd", side=_Gather(ffn_w(2)[1:]))
    (h3, xn_b0, pg_b0, pu_b0), (win1, wout1) = _ffn_fwd(h2, g2[0], win2, wout2, "ffn2_l0_fwd", side=_Gather(ffn_w(1)))
    (h4, xn_a1, pg_a1, pu_a1), (gla_win_g, gla_wout_g) = _ffn_fwd(
        h3, g1[1], win1, wout1, "ffn1_l1_fwd", side=_Gather([gla_w_in[0].astype(BF16), gla_w_out[0].astype(BF16)]))
    gla_win = _unshard_cols(gla_win_g)
    gla_win = jnp.pad(gla_win, ((0, 0), (0, GLA_N - gla_win.shape[1])))
    gla_wout = gla_wout_g.reshape(GLA_H * GLA_DV, D)
    (gproj, ghn), _ = _norm_mm(h4, gm[1], gla_win, GLA_N, "gla_proj_fwd")
    (go, gstates, gamat), (win3, wout3) = _gla_scan_fwd(gproj, wgp, bgate, "gla_scan_fwd", side=_Gather(ffn_w(3)))
    (h5, gog), _ = _post_fwd(go, gproj, gla_hn, gla_wout, h4, GLA_H, GLA_DV, "gla_post_fwd")
    (h6, xn_b1, pg_b1, pu_b1), _ = _ffn_fwd(h5, g2[1], win3, wout3, "ffn2_l1_fwd")

    dh, dfinal, loss_blk = _final_loss(h6, final_norm.reshape(1, D), jnp.pad(target, ((CHUNK, 0), (0, 0))), "final_loss")
    loss = lax.psum(loss_blk[0, 0], ("x", "y", "c"))

    def ffn_back(dh, h_in, xn, gain, pg, pu, win, wout, tag, side=None, dw_side=None):
        (dh_in, dob, dpg, dpu, act, dgain), got = _ffn_bwd(dh, h_in, gain, pg, pu, win, wout, tag + "_bwd", side=side)
        dwout = _mm_tn(act, dob[None], D, tag + "_dw_out").reshape(N_DEV, FF_SHARD // 2, D)
        if dw_side == "split":
            (lo,), got_out = _ffn_dw_in(xn, dpg, dpu, tag + "_dw_in_lo", side=_Exchange([dwout]), part=(0, 2))
            (hi,), got_lo = _ffn_dw_in(xn, dpg, dpu, tag + "_dw_in_hi", side=_Exchange([lo]), part=(1, 2))
            return dh_in, [hi, dwout], dgain[0], got, got_lo + got_out
        (dwin,), dw_got = _ffn_dw_in(xn, dpg, dpu, tag + "_dw_in", side=dw_side)
        return dh_in, [dwin, dwout], dgain[0], got, dw_got

    dh, dw_b1, dg2_1, _, _ = ffn_back(dh, h5, xn_b1, g2[1], pg_b1, pu_b1, win3, wout3, "ffn2_l1")

    (gdo, gdproj, gdhb, dghn), _ = _post_bwd(dh, go, gproj, gla_hn, gla_wout, GLA_H, GLA_DV, GLA_N, "gla_post_bwd")
    d_gla_wout = _mm_tn(gog[None], gdhb[None], D, "gla_dw_out").reshape(N_DEV, GLA_H * GLA_DV // N_DEV, D)
    gdproj, gdu = _gla_scan_bwd(gproj, wgp, bgate, gdo, gstates, gamat, gdproj, "gla_scan_bwd")
    gdproj, dwg, dbg = _gla_gate_bwd(gdu, gproj, wgp, gdproj, "gla_gate_bwd")
    d_gla_win = _mm_tn(gdproj[None], ghn[None], D, "gla_dw_in", tm=640)[0]
    (dh, dgm_1), _ = _proj_bwd(gdproj, gla_win, dh, h4, gm[1], GLA_N, "gla_proj_bwd")
    n_gla_in = 2 * GLA_H * GLA_DK + 2 * GLA_H * GLA_DV + GLA_RANK
    d_gla_win = d_gla_win[:n_gla_in].reshape(N_DEV, n_gla_in // N_DEV, D)

    dh, dw_a1, dg1_1, rv_b1, rv_gla = ffn_back(dh, h3, xn_a1, g1[1], pg_a1, pu_a1, win1, wout1, "ffn1_l1",
                                               side=_Exchange(dw_b1), dw_side=_Exchange([d_gla_win, d_gla_wout]))
    dh, dw_b0, dg2_0, rv_a1, _ = ffn_back(dh, h2, xn_b0, g2[0], pg_b0, pu_b0, win2, wout2, "ffn2_l0", side=_Exchange(dw_a1))

    (rdo, rdproj, rdhb, drhn), rv_b0_out = _post_bwd(dh, ro, rproj, ret_hn, ret_wout, RET_H, RET_DV, 6 * D, "ret_post_bwd",
                                                     side=_Exchange(dw_b0[1:]))
    d_ret_wout = _mm_tn(rog[None], rdhb[None], D, "ret_dw_out", rows=DW_ROWS // 2).reshape(N_DEV, RET_H * RET_DV // N_DEV, D)
    (rdproj,), rv_b0_in = _ret_scan_bwd(rproj, cos, sin, lgam, rdo, rstates, rdproj, "ret_scan_bwd", side=_Exchange(dw_b0[:1]))
    rv_b0 = rv_b0_in + rv_b0_out
    d_ret_win = _mm_tn(rhn[None], rdproj[None], ret_win.shape[2], "ret_dw_in", shard_out=True)
    (dh, dgm_0), rv_ret_out = _proj_bwd(rdproj, ret_win, dh, h1, gm[0], 4 * ret_win.shape[2], "ret_proj_bwd", side=_Exchange([d_ret_wout]))

    dh, dw_a0, dg1_0, rv_ret_in, (rv_a0_lo, rv_a0_out) = ffn_back(dh, h0, xn_a0, g1[0], pg_a0, pu_a0, win0, wout0, "ffn1_l0",
                                                                  side=_Exchange([d_ret_win]), dw_side="split")
    rv_ret = rv_ret_in + rv_ret_out
    rv_a0_hi = _run_side(_Exchange(dw_a0[:1]), "xchg_last")[0]
    rv_a0 = [jnp.concatenate([rv_a0_lo, rv_a0_hi], axis=2), rv_a0_out]
    grad_x = dh[CHUNK:][None]

    def adam_t(recvs, w, m, v, tag):
        outs = _adamw_reduce(recvs, *(jnp.swapaxes(a, 1, 2) for a in (w, m, v)), tag)
        return [jnp.swapaxes(o, 1, 2) for o in outs]

    u_ffn1_in = adam_t([rv_a0[0], rv_a1[0]], ffn1_w_in, m_ffn1_w_in, v_ffn1_w_in, "adam_ffn1_w_in")
    u_ffn2_in = adam_t([rv_b0[0], rv_b1[0]], ffn2_w_in, m_ffn2_w_in, v_ffn2_w_in, "adam_ffn2_w_in")
    u_ffn1_out = _adamw_reduce([rv_a0[1], rv_a1[1]], ffn1_w_out, m_ffn1_w_out, v_ffn1_w_out, "adam_ffn1_w_out")
    u_ffn2_out = _adamw_reduce([rv_b0[1], rv_b1[1]], ffn2_w_out, m_ffn2_w_out, v_ffn2_w_out, "adam_ffn2_w_out")
    u_ret_in = _adamw_reduce([rv_ret[0]], ret_w_in, m_ret_w_in, v_ret_w_in, "adam_ret_w_in")
    u_ret_out = _adamw_reduce([rv_ret[1]], ret_w_out, m_ret_w_out, v_ret_w_out, "adam_ret_w_out")
    u_gla_in = adam_t([rv_gla[0]], gla_w_in, m_gla_w_in, v_gla_w_in, "adam_gla_w_in")
    u_gla_out = _adamw_reduce([rv_gla[1]], gla_w_out, m_gla_w_out, v_gla_w_out, "adam_gla_w_out")

    dmeta = dh[PAD:CHUNK]
    parts = jnp.concatenate([
        dg1_0, dg1_1, dgm_0[0], dgm_1[0], dg2_0, dg2_1, dfinal[0], dmeta.reshape(-1), drhn[0], dwg[:GLA_RANK].reshape(-1),
        dbg[0], dghn[0]])
    n_parts = parts.shape[0]
    rows = -(-n_parts // D)
    rows = -(-rows // 8) * 8
    parts = jnp.pad(parts, (0, rows * D - n_parts)).reshape(rows, D)
    tot = _small_reduce(_run_side(_Gather([parts]), "ag_small_grads")[0], "small_grad_sum").reshape(-1)

    off = 0
    def take(nel):
        nonlocal off
        out = tot[off:off + nel]
        off += nel
        return out

    gr_norm_ffn1 = take(2 * D).reshape(2, D)
    gr_norm_mix = take(2 * D).reshape(2, D)
    gr_norm_ffn2 = take(2 * D).reshape(2, D)
    gr_final = take(D)
    gr_meta = _my_cols(take(N_META * D).reshape(N_META, D), D // N_DEV)
    gr_ret_hn = _my_cols(take(RET_H * RET_DV).reshape(RET_H, RET_DV), RET_DV // N_DEV)[None]
    gr_wgate = _my_cols(take(GLA_RANK * GLA_H * GLA_DK).reshape(GLA_RANK, GLA_H * GLA_DK), GLA_H * GLA_DK // N_DEV)[None]
    gr_bgate = _my_cols(take(GLA_H * GLA_DK).reshape(1, GLA_H * GLA_DK), GLA_H * GLA_DK // N_DEV)
    gr_gla_hn = _my_cols(take(GLA_H * GLA_DV).reshape(GLA_H, GLA_DV), GLA_DV // N_DEV)[None]

    small_w = [meta_tokens, norm_ffn1, norm_mix, norm_ffn2, ret_head_norm, gla_w_gate, gla_b_gate, gla_head_norm, final_norm]
    small_g = [gr_meta, gr_norm_ffn1, gr_norm_mix, gr_norm_ffn2, gr_ret_hn, gr_wgate, gr_bgate, gr_gla_hn, gr_final]
    small_m = [m_meta_tokens, m_norm_ffn1, m_norm_mix, m_norm_ffn2, m_ret_head_norm, m_gla_w_gate, m_gla_b_gate, m_gla_head_norm, m_final_norm]
    small_v = [v_meta_tokens, v_norm_ffn1, v_norm_mix, v_norm_ffn2, v_ret_head_norm, v_gla_w_gate, v_gla_b_gate, v_gla_head_norm, v_final_norm]

    def pack(arrs):
        flat = jnp.concatenate([a.reshape(-1) for a in arrs])
        n = flat.shape[0]
        r = -(-n // 128)
        r = -(-r // 8) * 8
        return jnp.pad(flat, (0, r * 128 - n), constant_values=1.0).reshape(r, 128)

    sd, sm, sv = _adamw_small(pack(small_w), pack(small_g), pack(small_m), pack(small_v), "adam_small")

    def unpack(buf):
        flat = buf.reshape(-1)
        outs, o = [], 0
        for a in small_w:
            outs.append(flat[o:o + a.size].reshape(a.shape))
            o += a.size
        return outs

    us_d, us_m, us_v = unpack(sd), unpack(sm), unpack(sv)

    def ordered(k, smalls):
        return (smalls[0], smalls[1], u_ffn1_in[k], u_ffn1_out[k], smalls[2], smalls[3], u_ffn2_in[k], u_ffn2_out[k],
                u_ret_in[k], smalls[4], u_ret_out[k], u_gla_in[k], smalls[5], smalls[6], smalls[7], u_gla_out[k], smalls[8])

    return (loss, grad_x, *ordered(0, small_g), *ordered(1, us_d), *ordered(2, us_m), *ordered(3, us_v))
```

```python
import functools
import math

import numpy as np
import jax
import jax.numpy as jnp
from jax import lax
from jax.experimental import pallas as pl
from jax.experimental.pallas import tpu as pltpu

F32 = jnp.float32
BF16 = jnp.bfloat16
S = jax.ShapeDtypeStruct
ANY = pl.BlockSpec(memory_space=pl.ANY)
MESH = pl.DeviceIdType.MESH

D = 1024
N_META = 16
CHUNK = 64
PAD = CHUNK - N_META
EPS = 1e-6
N_DEV = 8
FF_SHARD = 704
N_FF_CHUNK = 4
RET_H, RET_DK, RET_DV = 4, 256, 512
RET_QKV = RET_H * (2 * RET_DK + RET_DV)
RET_C = 192
GLA_H, GLA_DK, GLA_DV, GLA_RANK, GLA_TAU = 4, 128, 256, 16, 16.0
GLA_QKV = GLA_H * (2 * GLA_DK + GLA_DV)
GLA_N = 3200
GLA_ZBLK = 3072 // 128
SUB = 16
ROPE_BASE = 10000.0
ADAM_LR, ADAM_B1, ADAM_B2, ADAM_EPS, ADAM_WD, ADAM_STEP = 0.001, 0.9, 0.999, 1e-08, 0.01, 10
VMEM_LIMIT = 58 * 1024 * 1024
DW_ROWS = 2752


def _cp(**kw):
    return pltpu.CompilerParams(vmem_limit_bytes=VMEM_LIMIT, **kw)


def _row_tile(t, cap):
    best = 16
    for d in range(16, cap + 1, 16):
        if t % d == 0:
            best = d
    return best


def _sub_rows(tm, parts=2):
    units = tm // 16
    cuts = [16 * (units * p // parts) for p in range(parts + 1)]
    return [slice(a, b) for a, b in zip(cuts[:-1], cuts[1:]) if b > a]


def _dot(a, b):
    return jnp.dot(a, b, preferred_element_type=F32)


def _dot_nt(a, b):
    return lax.dot_general(a, b, (((1,), (1,)), ((), ())), preferred_element_type=F32)


def _dot_tn(a, b):
    return lax.dot_general(a, b, (((0,), (0,)), ((), ())), preferred_element_type=F32)


def _sigmoid(x):
    return pl.reciprocal(1.0 + jnp.exp(-x), approx=True)


def _rms_bwd(dxn, x, gain):
    r = lax.rsqrt(jnp.mean(x * x, axis=-1, keepdims=True) + EPS)
    xh = x * r
    dxh = dxn * gain
    dx = r * (dxh - xh * jnp.mean(dxh * xh, axis=-1, keepdims=True))
    return dx, jnp.sum(dxn * xh, axis=0, keepdims=True)


def _xyc():
    return lax.axis_index("x"), lax.axis_index("y"), lax.axis_index("c")


class _Gather:
    def __init__(self, xs):
        self.xs = list(xs)
        self.n = len(self.xs)

    def out_shape(self):
        return [S((N_DEV,) + a.shape, a.dtype) for a in self.xs]

    def scratch(self):
        return [pltpu.SemaphoreType.DMA((self.n, 7)), pltpu.SemaphoreType.DMA((self.n, 7)), pltpu.SemaphoreType.DMA((self.n,))]

    def phases(self, x_refs, out_refs, send_sems, recv_sems, local_sems):
        x, y, c = _xyc()
        me, sibling = (x, y, c), (x, y, 1 - c)
        chips = [(1 - x, y), (x, 1 - y), (1 - x, 1 - y)]

        def copy(t, k, block, to, src=None):
            px, py, pc = block
            dst = out_refs[t].at[4 * px + 2 * py + pc]
            return pltpu.make_async_remote_copy(
                src_ref=dst if src is None else src, dst_ref=dst,
                send_sem=send_sems.at[t, k], recv_sem=recv_sems.at[t, k], device_id=to, device_id_type=MESH)

        def own(t):
            return pltpu.make_async_copy(x_refs[t], out_refs[t].at[4 * x + 2 * y + c], local_sems.at[t])

        def first(t):
            return [copy(t, 0, me, sibling, src=x_refs[t])] + [
                copy(t, 1 + j, me, (*chip, c), src=x_refs[t]) for j, chip in enumerate(chips)]

        def passed(t):
            return [copy(t, 4 + j, (*chip, c), sibling) for j, chip in enumerate(chips)]

        def start():
            for t in range(self.n):
                own(t).start()
                for cp in first(t):
                    cp.start()

        def mid():
            for t in range(self.n):
                fw = passed(t)
                for j, chip in enumerate(chips):
                    copy(t, 1 + j, (*chip, c), me).wait_recv()
                    fw[j].start()

        def finish():
            for t in range(self.n):
                copy(t, 0, sibling, me).wait_recv()
                for j, chip in enumerate(chips):
                    copy(t, 4 + j, (*chip, 1 - c), me).wait_recv()
                for cp in first(t) + passed(t):
                    cp.wait_send()
                own(t).wait()

        return start, mid, finish


class _Exchange:
    def __init__(self, xs):
        self.xs = list(xs)
        self.n = len(self.xs)

    def out_shape(self):
        return [S(a.shape, a.dtype) for a in self.xs]

    def scratch(self):
        return [pltpu.SemaphoreType.DMA((self.n, 7)), pltpu.SemaphoreType.DMA((self.n, 7)), pltpu.SemaphoreType.DMA((self.n,))]

    def phases(self, g_refs, r_refs, send_sems, recv_sems, local_sems):
        x, y, c = _xyc()
        me = 4 * x + 2 * y + c

        def own(t):
            return pltpu.make_async_copy(g_refs[t].at[me], r_refs[t].at[me], local_sems.at[t])

        def send(t, m):
            px, py, pc = x ^ (m >> 2), y ^ ((m >> 1) & 1), c ^ (m & 1)
            return pltpu.make_async_remote_copy(
                src_ref=g_refs[t].at[4 * px + 2 * py + pc], dst_ref=r_refs[t].at[me],
                send_sem=send_sems.at[t, m - 1], recv_sem=recv_sems.at[t, m - 1],
                device_id=(px, py, pc), device_id_type=MESH)

        def arrival(t, m):
            peer = 4 * (x ^ (m >> 2)) + 2 * (y ^ ((m >> 1) & 1)) + (c ^ (m & 1))
            return pltpu.make_async_remote_copy(
                src_ref=g_refs[t].at[peer], dst_ref=r_refs[t].at[peer],
                send_sem=send_sems.at[t, m - 1], recv_sem=recv_sems.at[t, m - 1],
                device_id=(x, y, c), device_id_type=MESH)

        def start():
            for t in range(self.n):
                own(t).start()
            for m in range(1, N_DEV):
                for t in range(self.n):
                    send(t, m).start()

        def mid():
            pass

        def finish():
            for m in range(1, N_DEV):
                for t in range(self.n):
                    arrival(t, m).wait_recv()
            for m in range(1, N_DEV):
                for t in range(self.n):
                    send(t, m).wait_send()
            for t in range(self.n):
                own(t).wait()

        return start, mid, finish


def _run_side(side, name):
    n = side.n

    def body(*refs):
        start, mid, finish = side.phases(refs[:n], refs[n:2 * n], *refs[2 * n:])
        start()
        mid()
        finish()

    return list(pl.pallas_call(
        body, name=name, out_shape=side.out_shape(), in_specs=[ANY] * n, out_specs=[ANY] * n,
        scratch_shapes=side.scratch())(*side.xs))


def _grid_steps(grid):
    def ids():
        return [pl.program_id(a) for a in range(len(grid))]

    def first():
        return functools.reduce(jnp.logical_and, [i == 0 for i in ids()])

    def middle():
        i = ids()
        return functools.reduce(jnp.logical_and, [i[0] == (3 * grid[0]) // 4] + [j == 0 for j in i[1:]])

    def last():
        return functools.reduce(jnp.logical_and, [i == g - 1 for i, g in zip(ids(), grid)])

    return first, middle, last


def _call(body, *, name, grid, in_specs, out_specs, out_shape, scratch_shapes, operands, side=None, aliases=None):
    n_in, n_out, n_scr = len(in_specs), len(out_shape), len(scratch_shapes)
    full = body
    if side is not None:
        ns = side.n
        first, middle, last = _grid_steps(grid)

        def full(*refs):
            a = n_in
            ins, sins = refs[:a], refs[a:a + ns]
            a += ns
            outs, souts = refs[a:a + n_out], refs[a + n_out:a + n_out + ns]
            a += n_out + ns
            scr, sems = refs[a:a + n_scr], refs[a + n_scr:]
            start, mid, finish = side.phases(sins, souts, *sems)
            pl.when(first())(start)
            body(*ins, *outs, *scr)
            pl.when(middle())(mid)
            pl.when(last())(finish)

        in_specs = list(in_specs) + [ANY] * ns
        out_specs = list(out_specs) + [ANY] * ns
        out_shape = list(out_shape) + side.out_shape()
        scratch_shapes = list(scratch_shapes) + side.scratch()
        operands = list(operands) + side.xs
    outs = pl.pallas_call(
        full, name=name, grid=grid, in_specs=list(in_specs), out_specs=list(out_specs), out_shape=list(out_shape),
        scratch_shapes=list(scratch_shapes), input_output_aliases=aliases or {},
        compiler_params=_cp(dimension_semantics=("arbitrary",) * len(grid)),
    )(*operands)
    return list(outs[:n_out]), list(outs[n_out:])


def _ffn_fwd(h, gain, win, wout, name, side=None):
    t = h.shape[0]
    tm = _row_tile(t, 704)
    nt = t // tm

    def body(h_ref, g_ref, wg_ref, wu_ref, wo_ref, hn_ref, xn_ref, pg_ref, pu_ref, acc):
        c = pl.program_id(1)

        @pl.when(c == 0)
        def _():
            x = h_ref[...]
            r = lax.rsqrt(jnp.mean(x * x, axis=-1, keepdims=True) + EPS)
            xn_ref[...] = (x * r * g_ref[...]).astype(BF16)
            acc[...] = jnp.zeros_like(acc)

        wo = wo_ref[...].reshape(FF_SHARD, D)
        subs = _sub_rows(tm)
        gus = [(_dot(xn_ref[r, :], wg_ref[...]), _dot(xn_ref[r, :], wu_ref[...])) for r in subs]
        for r, (g, u) in zip(subs, gus):
            pg_ref[r, :] = g.astype(BF16)
            pu_ref[r, :] = u.astype(BF16)
            act = (g * _sigmoid(g) * u).astype(BF16)
            acc[r, :] += _dot(act, wo)

        @pl.when(c == N_FF_CHUNK - 1)
        def _():
            hn_ref[...] = h_ref[...] + 0.5 * acc[...]

    return _call(
        body, name=name, grid=(nt, N_FF_CHUNK), side=side,
        in_specs=[
            pl.BlockSpec((tm, D), lambda i, c: (i, 0)),
            pl.BlockSpec((1, D), lambda i, c: (0, 0)),
            pl.BlockSpec((None, D, FF_SHARD), lambda i, c: (c, 0, 0)),
            pl.BlockSpec((None, D, FF_SHARD), lambda i, c: (c + N_FF_CHUNK, 0, 0)),
            pl.BlockSpec((2, FF_SHARD // 2, D), lambda i, c: (c, 0, 0)),
        ],
        out_specs=[
            pl.BlockSpec((tm, D), lambda i, c: (i, 0)),
            pl.BlockSpec((tm, D), lambda i, c: (i, 0)),
            pl.BlockSpec((None, tm, FF_SHARD), lambda i, c: (c, i, 0)),
            pl.BlockSpec((None, tm, FF_SHARD), lambda i, c: (c, i, 0)),
        ],
        out_shape=[S((t, D), F32), S((t, D), BF16), S((N_FF_CHUNK, t, FF_SHARD), BF16), S((N_FF_CHUNK, t, FF_SHARD), BF16)],
        scratch_shapes=[pltpu.VMEM((tm, D), F32)],
        operands=[h, gain, win, win, wout])


def _ffn_bwd(dh, h, gain, pg, pu, win, wout, name, side=None):
    t = h.shape[0]
    tm = _row_tile(t, 704)
    nt = t // tm

    def body(dh_ref, h_ref, g_ref, pg_ref, pu_ref, wg_ref, wu_ref, wo_ref,
             dhi_ref, dob_ref, dpg_ref, dpu_ref, act_ref, dgain_ref, acc):
        i, c = pl.program_id(0), pl.program_id(1)

        @pl.when(c == 0)
        def _():
            dob_ref[...] = (0.5 * dh_ref[...]).astype(BF16)
            acc[...] = jnp.zeros_like(acc)

        @pl.when((i == 0) & (c == 0))
        def _():
            dgain_ref[...] = jnp.zeros_like(dgain_ref)

        wo = wo_ref[...].reshape(FF_SHARD, D)
        subs = _sub_rows(tm)
        dacts = [_dot_nt(dob_ref[r, :], wo) for r in subs]
        for r, dact in zip(subs, dacts):
            g = pg_ref[r, :].astype(F32)
            u = pu_ref[r, :].astype(F32)
            s = _sigmoid(g)
            sl = g * s
            act_ref[r, :] = (sl * u).astype(BF16)
            dg = (dact * u * (s * (1.0 + g * (1.0 - s)))).astype(BF16)
            du = (dact * sl).astype(BF16)
            dpg_ref[r, :] = dg
            dpu_ref[r, :] = du
            acc[r, :] += _dot_nt(dg, wg_ref[...]) + _dot_nt(du, wu_ref[...])

        @pl.when(c == N_FF_CHUNK - 1)
        def _():
            dx, dgn = _rms_bwd(acc[...], h_ref[...], g_ref[...])
            dhi_ref[...] = dh_ref[...] + dx
            dgain_ref[0:1, :] += dgn

    blk = pl.BlockSpec((None, tm, FF_SHARD), lambda i, c: (c, i, 0))
    row = pl.BlockSpec((tm, D), lambda i, c: (i, 0))
    return _call(
        body, name=name, grid=(nt, N_FF_CHUNK), side=side,
        in_specs=[
            row, row, pl.BlockSpec((1, D), lambda i, c: (0, 0)), blk, blk,
            pl.BlockSpec((None, D, FF_SHARD), lambda i, c: (c, 0, 0)),
            pl.BlockSpec((None, D, FF_SHARD), lambda i, c: (c + N_FF_CHUNK, 0, 0)),
            pl.BlockSpec((2, FF_SHARD // 2, D), lambda i, c: (c, 0, 0)),
        ],
        out_specs=[row, row, blk, blk, blk, pl.BlockSpec((8, D), lambda i, c: (0, 0))],
        out_shape=[S((t, D), F32), S((t, D), BF16)] + [S((N_FF_CHUNK, t, FF_SHARD), BF16)] * 3 + [S((8, D), F32)],
        scratch_shapes=[pltpu.VMEM((tm, D), F32)],
        operands=[dh, h, gain, pg, pu, win, win, wout])


def _ffn_dw_in(xn, dpg, dpu, name, side=None, part=(0, 1)):
    t = xn.shape[0]
    pj, pn = part
    dn = D // pn
    tk = _row_tile(t, DW_ROWS)
    nk = t // tk

    def body(a_ref, bg_ref, bu_ref, o_ref, acc):
        c, k = pl.program_id(0), pl.program_id(1)

        @pl.when(k == 0)
        def _():
            acc[...] = jnp.zeros_like(acc)

        @pl.when(c < N_FF_CHUNK)
        def _():
            acc[...] += _dot_tn(bg_ref[...], a_ref[...])

        @pl.when(c >= N_FF_CHUNK)
        def _():
            acc[...] += _dot_tn(bu_ref[...], a_ref[...])

        @pl.when(k == nk - 1)
        def _():
            o_ref[...] = acc[...].astype(BF16)

    return _call(
        body, name=name, grid=(2 * N_FF_CHUNK, nk), side=side,
        in_specs=[
            pl.BlockSpec((tk, dn), lambda c, k: (k, pj)),
            pl.BlockSpec((None, tk, FF_SHARD), lambda c, k: (jnp.minimum(c, N_FF_CHUNK - 1), k, 0)),
            pl.BlockSpec((None, tk, FF_SHARD), lambda c, k: (jnp.maximum(c - N_FF_CHUNK, 0), k, 0)),
        ],
        out_specs=[pl.BlockSpec((None, FF_SHARD, dn), lambda c, k: (c, 0, 0))],
        out_shape=[S((2 * N_FF_CHUNK, FF_SHARD, dn), BF16)],
        scratch_shapes=[pltpu.VMEM((FF_SHARD, dn), F32)],
        operands=[xn, dpg, dpu])


def _mm_tn(a, b, tn, name, tm=None, rows=DW_ROWS, shard_out=False, side=None):
    ca, t, m = a.shape
    cb, _, n = b.shape
    nc = max(ca, cb)
    tm = m if tm is None else tm
    tk = _row_tile(t, rows)
    nk = t // tk

    def body(a_ref, b_ref, o_ref, acc):
        k = pl.program_id(3)

        @pl.when(k == 0)
        def _():
            acc[...] = jnp.zeros_like(acc)

        acc[...] += _dot_tn(a_ref[...], b_ref[...])

        @pl.when(k == nk - 1)
        def _():
            o_ref[...] = acc[...].astype(BF16)

    if shard_out:
        out_spec = pl.BlockSpec((None, tm, tn), lambda c, i, j, k: (j, 0, 0))
        out_shape = S((n // tn, m, tn), BF16)
    else:
        out_spec = pl.BlockSpec((None, tm, tn), lambda c, i, j, k: (c, i, j))
        out_shape = S((nc, m, n), BF16)
    (out,), got = _call(
        body, name=name, grid=(nc, m // tm, n // tn, nk), side=side,
        in_specs=[
            pl.BlockSpec((None, tk, tm), (lambda c, i, j, k: (c, k, i)) if ca > 1 else (lambda c, i, j, k: (0, k, i))),
            pl.BlockSpec((None, tk, tn), (lambda c, i, j, k: (c, k, j)) if cb > 1 else (lambda c, i, j, k: (0, k, j))),
        ],
        out_specs=[out_spec], out_shape=[out_shape],
        scratch_shapes=[pltpu.VMEM((tm, tn), F32)],
        operands=[a, b])
    return out if side is None else (out, got)


def _norm_mm(h, gain, w, tn, name, side=None):
    t = h.shape[0]
    n = w.shape[-1] if w.ndim == 2 else w.shape[0] * w.shape[2]
    tm = _row_tile(t, 704)
    kb = 1 if w.ndim == 2 else tn // w.shape[2]
    w_spec = (pl.BlockSpec((D, tn), lambda i, j: (0, j)) if w.ndim == 2
              else pl.BlockSpec((kb, D, tn // kb), lambda i, j: (j, 0, 0)))

    def body(h_ref, g_ref, w_ref, o_ref, xn_ref):
        @pl.when(pl.program_id(1) == 0)
        def _():
            x = h_ref[...]
            r = lax.rsqrt(jnp.mean(x * x, axis=-1, keepdims=True) + EPS)
            xn_ref[...] = (x * r * g_ref[...]).astype(BF16)

        if w.ndim == 2:
            o_ref[...] = _dot(xn_ref[...], w_ref[...]).astype(BF16)
        else:
            for b in range(kb):
                o_ref[:, b * (tn // kb):(b + 1) * (tn // kb)] = _dot(xn_ref[...], w_ref[b]).astype(BF16)

    return _call(
        body, name=name, grid=(t // tm, n // tn), side=side,
        in_specs=[pl.BlockSpec((tm, D), lambda i, j: (i, 0)), pl.BlockSpec((1, D), lambda i, j: (0, 0)), w_spec],
        out_specs=[pl.BlockSpec((tm, tn), lambda i, j: (i, j)), pl.BlockSpec((tm, D), lambda i, j: (i, 0))],
        out_shape=[S((t, n), BF16), S((t, D), BF16)], scratch_shapes=[],
        operands=[h, gain, w])


def _proj_bwd(dproj, w, dh, h, gain, tk, name, side=None):
    t, n = dproj.shape
    tm = _row_tile(t, 704)
    nk = n // tk
    kb = 1 if w.ndim == 2 else tk // w.shape[2]
    w_spec = (pl.BlockSpec((D, tk), lambda i, k: (0, k)) if w.ndim == 2
              else pl.BlockSpec((kb, D, tk // kb), lambda i, k: (k, 0, 0)))

    def body(dp_ref, w_ref, dh_ref, h_ref, g_ref, dhi_ref, dgain_ref, acc):
        i, k = pl.program_id(0), pl.program_id(1)

        @pl.when(k == 0)
        def _():
            acc[...] = jnp.zeros_like(acc)

        @pl.when((i == 0) & (k == 0))
        def _():
            dgain_ref[...] = jnp.zeros_like(dgain_ref)

        if w.ndim == 2:
            acc[...] += _dot_nt(dp_ref[...], w_ref[...])
        else:
            for b in range(kb):
                acc[...] += _dot_nt(dp_ref[:, b * (tk // kb):(b + 1) * (tk // kb)], w_ref[b])

        @pl.when(k == nk - 1)
        def _():
            dx, dgn = _rms_bwd(acc[...], h_ref[...], g_ref[...])
            dhi_ref[...] = dh_ref[...] + dx
            dgain_ref[0:1, :] += dgn

    row = pl.BlockSpec((tm, D), lambda i, k: (i, 0))
    return _call(
        body, name=name, grid=(t // tm, nk), side=side,
        in_specs=[pl.BlockSpec((tm, tk), lambda i, k: (i, k)), w_spec,
                  row, row, pl.BlockSpec((1, D), lambda i, k: (0, 0))],
        out_specs=[row, pl.BlockSpec((8, D), lambda i, k: (0, 0))],
        out_shape=[S((t, D), F32), S((8, D), F32)],
        scratch_shapes=[pltpu.VMEM((tm, D), F32)],
        operands=[dproj, w, dh, h, gain])


def _post_fwd(o, proj, hgain, wout, h, nh, dv, name, side=None):
    t = h.shape[0]
    w = nh * dv
    tm = _row_tile(t, 704)

    def body(o_ref, g_ref, hg_ref, wo_ref, h_ref, hn_ref, og_ref):
        for hd in range(nh):
            sl = slice(hd * dv, (hd + 1) * dv)
            oh = o_ref[:, sl].astype(F32)
            r = lax.rsqrt(jnp.mean(oh * oh, axis=-1, keepdims=True) + EPS)
            gg = g_ref[:, sl].astype(F32)
            og_ref[:, sl] = (oh * r * hg_ref[:, sl] * (gg * _sigmoid(gg))).astype(BF16)
        hn_ref[...] = h_ref[...] + _dot(og_ref[...], wo_ref[...])

    return _call(
        body, name=name, grid=(t // tm,), side=side,
        in_specs=[pl.BlockSpec((tm, w), lambda i: (i, 0)), pl.BlockSpec((tm, w), lambda i: (i, 2)),
                  pl.BlockSpec((1, w), lambda i: (0, 0)), pl.BlockSpec((w, D), lambda i: (0, 0)),
                  pl.BlockSpec((tm, D), lambda i: (i, 0))],
        out_specs=[pl.BlockSpec((tm, D), lambda i: (i, 0)), pl.BlockSpec((tm, w), lambda i: (i, 0))],
        out_shape=[S((t, D), F32), S((t, w), BF16)], scratch_shapes=[],
        operands=[o, proj, hgain, wout, h])


def _post_bwd(dh, o, proj, hgain, wout, nh, dv, nproj, name, side=None):
    t = dh.shape[0]
    w = nh * dv
    tm = _row_tile(t, 704)

    def body(dh_ref, o_ref, g_ref, hg_ref, wo_ref, do_ref, dg_ref, dhb_ref, dhg_ref):
        @pl.when(pl.program_id(0) == 0)
        def _():
            dhg_ref[...] = jnp.zeros_like(dhg_ref)

        dmix = dh_ref[...].astype(BF16)
        dhb_ref[...] = dmix
        dog = _dot_nt(dmix, wo_ref[...])
        for hd in range(nh):
            sl = slice(hd * dv, (hd + 1) * dv)
            oh = o_ref[:, sl].astype(F32)
            r = lax.rsqrt(jnp.mean(oh * oh, axis=-1, keepdims=True) + EPS)
            xh = oh * r
            gain = hg_ref[:, sl]
            gg = g_ref[:, sl].astype(F32)
            s = _sigmoid(gg)
            dogh = dog[:, sl]
            don = dogh * (gg * s)
            dg_ref[:, sl] = (dogh * (xh * gain) * (s * (1.0 + gg * (1.0 - s)))).astype(BF16)
            dxh = don * gain
            do_ref[:, sl] = (r * (dxh - xh * jnp.mean(dxh * xh, axis=-1, keepdims=True))).astype(BF16)
            dhg_ref[0:1, sl] += jnp.sum(don * xh, axis=0, keepdims=True)

    return _call(
        body, name=name, grid=(t // tm,), side=side,
        in_specs=[pl.BlockSpec((tm, D), lambda i: (i, 0)), pl.BlockSpec((tm, w), lambda i: (i, 0)),
                  pl.BlockSpec((tm, w), lambda i: (i, 2)), pl.BlockSpec((1, w), lambda i: (0, 0)),
                  pl.BlockSpec((w, D), lambda i: (0, 0))],
        out_specs=[pl.BlockSpec((tm, w), lambda i: (i, 0)), pl.BlockSpec((tm, w), lambda i: (i, 2)),
                   pl.BlockSpec((tm, D), lambda i: (i, 0)), pl.BlockSpec((8, w), lambda i: (0, 0))],
        out_shape=[S((t, w), BF16), S((t, nproj), BF16), S((t, D), BF16), S((8, w), F32)], scratch_shapes=[],
        operands=[dh, o, proj, hgain, wout])


def _ret_consts():
    lg = np.log1p(-np.exp2(-5.0 - np.arange(RET_H, dtype=np.float32))).astype(np.float32)
    return jnp.asarray(np.broadcast_to(lg[:, None, None], (RET_H, 1, 128)).copy())


def _rope_tables(t):
    half = RET_DK // 2
    inv = 1.0 / (ROPE_BASE ** jnp.linspace(0.0, 1.0, half, dtype=F32))
    base = (jnp.arange(t // CHUNK) * CHUNK - PAD).astype(F32)[:, None] * inv[None, :]
    off = jnp.arange(CHUNK).astype(F32)[:, None] * inv[None, :]
    ca, sa = jnp.cos(base)[:, None, :], jnp.sin(base)[:, None, :]
    cb, sb = jnp.cos(off)[None], jnp.sin(off)[None]
    return (ca * cb - sa * sb).reshape(t, half), (sa * cb + ca * sb).reshape(t, half)


def _ret_chunk(blk_ref, cos_ref, sin_ref, lg, h):
    c = RET_C
    half = RET_DK // 2
    oq, ok, ov = h * RET_DK, RET_H * RET_DK + h * RET_DK, 2 * RET_H * RET_DK + h * RET_DV
    cs, sn = cos_ref[...], sin_ref[...]
    q1, q2 = blk_ref[:, oq:oq + half].astype(F32), blk_ref[:, oq + half:oq + RET_DK].astype(F32)
    k1, k2 = blk_ref[:, ok:ok + half].astype(F32), blk_ref[:, ok + half:ok + RET_DK].astype(F32)
    qr = jnp.concatenate([q1 * cs - q2 * sn, q1 * sn + q2 * cs], axis=1)
    kr = jnp.concatenate([k1 * cs - k2 * sn, k1 * sn + k2 * cs], axis=1) * (RET_DK ** -0.5)
    v = blk_ref[:, ov:ov + RET_DV]
    ii = lax.broadcasted_iota(jnp.int32, (c, 1), 0).astype(F32)
    jj = lax.broadcasted_iota(jnp.int32, (1, c), 1).astype(F32)
    rel = ii - jj
    dmat = jnp.where(rel >= 0, jnp.exp(lg * jnp.maximum(rel, 0.0)), 0.0)
    dq = jnp.exp(lg * (ii + 1.0))
    dk = jnp.exp(lg * (c - 1.0 - ii))
    dchunk = jnp.exp(lg * float(c))
    return qr, kr, v, dmat, dq, dk, dchunk


def _ret_scan_fwd(proj, cos, sin, lgam, name):
    t = proj.shape[0]
    c = RET_C
    nc = t // c

    def body(blk_ref, cos_ref, sin_ref, lg_ref, o_ref, st_ref, state):
        @pl.when(pl.program_id(0) == 0)
        def _():
            state[...] = jnp.zeros_like(state)

        for h in range(RET_H):
            qr, kr, v, dmat, dq, dk, dchunk = _ret_chunk(blk_ref, cos_ref, sin_ref, lg_ref[h, :, 0:1], h)
            sp = state[h]
            st_ref[h] = sp.astype(BF16)
            scores = _dot_nt(qr.astype(BF16), kr.astype(BF16)) * dmat
            o = _dot(scores.astype(BF16), v) + _dot((qr * dq).astype(BF16), sp.astype(BF16))
            o_ref[:, h * RET_DV:(h + 1) * RET_DV] = o.astype(BF16)
            state[h] = sp * dchunk + _dot_tn((kr * dk).astype(BF16), v)

    return pl.pallas_call(
        body, name=name, grid=(nc,),
        in_specs=[pl.BlockSpec((c, RET_QKV), lambda n: (n, 0)), pl.BlockSpec((c, 128), lambda n: (n, 0)),
                  pl.BlockSpec((c, 128), lambda n: (n, 0)), pl.BlockSpec((RET_H, 1, 128), lambda n: (0, 0, 0))],
        out_specs=[pl.BlockSpec((c, RET_H * RET_DV), lambda n: (n, 0)),
                   pl.BlockSpec((RET_H, None, RET_DK, RET_DV), lambda n: (0, n, 0, 0))],
        out_shape=[S((t, RET_H * RET_DV), BF16), S((RET_H, nc, RET_DK, RET_DV), BF16)],
        scratch_shapes=[pltpu.VMEM((RET_H, RET_DK, RET_DV), F32)],
        compiler_params=_cp(dimension_semantics=("arbitrary",)),
    )(proj, cos, sin, lgam)


def _ret_scan_bwd(proj, cos, sin, lgam, do, states, dproj, name, side=None):
    t = proj.shape[0]
    c = RET_C
    nc = t // c
    half = RET_DK // 2

    def body(blk_ref, cos_ref, sin_ref, lg_ref, do_ref, st_ref, dp_in, dp_ref, dstate):
        n = nc - 1 - pl.program_id(0)

        @pl.when(pl.program_id(0) == 0)
        def _():
            dstate[...] = jnp.zeros_like(dstate)

        cs, sn = cos_ref[...], sin_ref[...]
        rows = n * c + lax.broadcasted_iota(jnp.int32, (c, 1), 0)
        keep = rows >= PAD

        def unrot(d):
            d1, d2 = d[:, :half], d[:, half:]
            return jnp.concatenate([d1 * cs + d2 * sn, d2 * cs - d1 * sn], axis=1)

        for h in range(RET_H):
            qr, kr, v, dmat, dq, dk, dchunk = _ret_chunk(blk_ref, cos_ref, sin_ref, lg_ref[h, :, 0:1], h)
            qb, kb = qr.astype(BF16), kr.astype(BF16)
            dob = do_ref[:, h * RET_DV:(h + 1) * RET_DV]
            sp = st_ref[h]
            ds = dstate[h]
            dsb = ds.astype(BF16)
            p = (_dot_nt(qb, kb) * dmat).astype(BF16)
            dvv = _dot_tn(p, dob) + _dot((kr * dk).astype(BF16), dsb)
            dp = (_dot_nt(dob, v) * dmat).astype(BF16)
            dqr = _dot(dp, kb) + _dot_nt(dob, sp) * dq
            dkr = (_dot_tn(dp, qb) + _dot_nt(v, dsb) * dk) * (RET_DK ** -0.5)
            dstate[h] = ds * dchunk + _dot_tn((qr * dq).astype(BF16), dob)
            oq, ok, ov = h * RET_DK, RET_H * RET_DK + h * RET_DK, 2 * RET_H * RET_DK + h * RET_DV
            dp_ref[:, oq:oq + RET_DK] = jnp.where(keep, unrot(dqr), 0.0).astype(BF16)
            dp_ref[:, ok:ok + RET_DK] = jnp.where(keep, unrot(dkr), 0.0).astype(BF16)
            dp_ref[:, ov:ov + RET_DV] = jnp.where(keep, dvv, 0.0).astype(BF16)

    return _call(
        body, name=name, grid=(nc,), side=side, aliases={6: 0},
        in_specs=[pl.BlockSpec((c, RET_QKV), lambda n: (nc - 1 - n, 0)), pl.BlockSpec((c, 128), lambda n: (nc - 1 - n, 0)),
                  pl.BlockSpec((c, 128), lambda n: (nc - 1 - n, 0)), pl.BlockSpec((RET_H, 1, 128), lambda n: (0, 0, 0)),
                  pl.BlockSpec((c, RET_H * RET_DV), lambda n: (nc - 1 - n, 0)),
                  pl.BlockSpec((RET_H, None, RET_DK, RET_DV), lambda n: (0, nc - 1 - n, 0, 0)), ANY],
        out_specs=[pl.BlockSpec((c, RET_QKV), lambda n: (nc - 1 - n, 0))],
        out_shape=[S((t, dproj.shape[1]), BF16)],
        scratch_shapes=[pltpu.VMEM((RET_H, RET_DK, RET_DV), F32)],
        operands=[proj, cos, sin, lgam, do, states, dproj])


def _split3(x):
    hi = x.astype(BF16)
    r1 = x - hi.astype(F32)
    mid = r1.astype(BF16)
    lo = (r1 - mid.astype(F32)).astype(BF16)
    return hi, mid, lo


def _gla_chunk(blk_ref, z_ref, wg_ref, bg_ref, n, h, b_ref=None):
    c = CHUNK
    oq, ok, ov = h * GLA_DK, GLA_H * GLA_DK + h * GLA_DK, 2 * GLA_H * GLA_DK + h * GLA_DV
    q = blk_ref[:, oq:oq + GLA_DK].astype(F32) * (GLA_DK ** -0.5)
    k = blk_ref[:, ok:ok + GLA_DK].astype(F32)
    v = blk_ref[:, ov:ov + GLA_DV]
    hs = slice(h * GLA_DK, (h + 1) * GLA_DK)
    u = _dot(z_ref[...], wg_ref[:, hs]) + bg_ref[:, hs]
    rows = n * c + lax.broadcasted_iota(jnp.int32, (c, 1), 0)
    keep = rows >= PAD
    if b_ref is not None:
        return q, k, v, u, b_ref[:, hs], keep
    la = (jnp.minimum(u, 0.0) - jnp.log(1.0 + jnp.exp(-jnp.abs(u)))) * (1.0 / GLA_TAU)
    la = jnp.where(keep, la, 0.0)
    ii = lax.broadcasted_iota(jnp.int32, (c, c), 0)
    jj = lax.broadcasted_iota(jnp.int32, (c, c), 1)
    tril = (ii >= jj).astype(BF16)
    hi, mid, lo = _split3(la)
    b = _dot(tril, hi) + _dot(tril, mid) + _dot(tril, lo)
    return q, k, v, u, b, keep


def _gla_intra(qs, ks, bs, a_ref):
    c = CHUNK
    nh = len(qs)
    col = lax.broadcasted_iota(jnp.int32, (1, c), 1)
    rowi = lax.broadcasted_iota(jnp.int32, (SUB, 1), 0)
    for blk in range(c // SUB):
        r = slice(SUB * blk, SUB * (blk + 1))
        arows = []
        for h in range(nh):
            q, k, b = qs[h], ks[h], bs[h]
            if blk > 0:
                bprev = b[SUB * blk - 1:SUB * blk]
                qe = q[r] * jnp.exp(b[r] - bprev)
                kt = k * jnp.exp(jnp.minimum(bprev - b, 0.0))
                arows.append(jnp.where(col < SUB * blk, _dot_nt(qe.astype(BF16), kt.astype(BF16)), 0.0))
            else:
                arows.append(jnp.zeros((SUB, c), F32))
        for j in range(SUB):
            for h in range(nh):
                b_i = bs[h][r]
                e = jnp.exp(b_i - b_i[j:j + 1])
                a = jnp.sum(qs[h][r] * ks[h][r][j:j + 1] * e, axis=1, keepdims=True)
                arows[h] = jnp.where(col == SUB * blk + j, a, arows[h])
        for h in range(nh):
            a_ref[h, r, :] = jnp.where(col - SUB * blk <= rowi, arows[h], 0.0)


def _gla_scan_fwd(proj, wgp, bg, name, side=None):
    t = proj.shape[0]
    c = CHUNK
    nc = t // c
    heads = range(GLA_H)

    def body(blk_ref, z_ref, wg_ref, bg_ref, o_ref, st_ref, am_ref, bs_ref, state, a_ref):
        n = pl.program_id(0)

        @pl.when(n == 0)
        def _():
            state[...] = jnp.zeros_like(state)

        qs, ks, vs, us, bs, keeps = zip(*[_gla_chunk(blk_ref, z_ref, wg_ref, bg_ref, n, h) for h in heads])
        _gla_intra(qs, ks, bs, a_ref)
        for h in heads:
            q, k, v, b = qs[h], ks[h], vs[h], bs[h]
            sp = state[h]
            st_ref[h] = sp.astype(BF16)
            ab = a_ref[h].astype(BF16)
            am_ref[:, h * c:(h + 1) * c] = ab
            bs_ref[:, h * GLA_DK:(h + 1) * GLA_DK] = b
            o = _dot(ab, v) + _dot_nt((q * jnp.exp(b)).astype(BF16), sp.astype(BF16))
            o_ref[:, h * GLA_DV:(h + 1) * GLA_DV] = o.astype(BF16)
            bc = b[c - 1:c]
            state[h] = sp * jnp.exp(bc) + _dot_tn(v, (k * jnp.exp(bc - b)).astype(BF16))

    return _call(
        body, name=name, grid=(nc,), side=side,
        in_specs=[pl.BlockSpec((c, GLA_QKV), lambda n: (n, 0)), pl.BlockSpec((c, 128), lambda n: (n, GLA_ZBLK)),
                  pl.BlockSpec((128, GLA_H * GLA_DK), lambda n: (0, 0)), pl.BlockSpec((1, GLA_H * GLA_DK), lambda n: (0, 0))],
        out_specs=[pl.BlockSpec((c, GLA_H * GLA_DV), lambda n: (n, 0)),
                   pl.BlockSpec((GLA_H, None, GLA_DV, GLA_DK), lambda n: (0, n, 0, 0)),
                   pl.BlockSpec((c, GLA_H * c), lambda n: (n, 0)),
                   pl.BlockSpec((c, GLA_H * GLA_DK), lambda n: (n, 0))],
        out_shape=[S((t, GLA_H * GLA_DV), BF16), S((GLA_H, nc, GLA_DV, GLA_DK), BF16), S((t, GLA_H * c), BF16),
                   S((t, GLA_H * GLA_DK), F32)],
        scratch_shapes=[pltpu.VMEM((GLA_H, GLA_DV, GLA_DK), F32), pltpu.VMEM((GLA_H, c, c), F32)],
        operands=[proj, proj, wgp, bg])


def _gla_scan_bwd(proj, wgp, bg, do, states, amat, bcum, dproj, name):
    t = proj.shape[0]
    c = CHUNK
    nc = t // c
    heads = range(GLA_H)

    def body(blk_ref, z_ref, wg_ref, bg_ref, do_ref, st_ref, am_ref, bs_ref, dp_in, dp_ref, du_ref, dstate, dq_ref, dkd_ref):
        n = nc - 1 - pl.program_id(0)

        @pl.when(pl.program_id(0) == 0)
        def _():
            dstate[...] = jnp.zeros_like(dstate)

        qs, ks, vs, us, bs, keeps = zip(*[_gla_chunk(blk_ref, z_ref, wg_ref, bg_ref, n, h, bs_ref) for h in heads])
        ii = lax.broadcasted_iota(jnp.int32, (c, c), 0)
        jj = lax.broadcasted_iota(jnp.int32, (c, c), 1)
        col = lax.broadcasted_iota(jnp.int32, (1, c), 1)
        rowi = lax.broadcasted_iota(jnp.int32, (SUB, 1), 0)
        rowc = lax.broadcasted_iota(jnp.int32, (c, 1), 0)
        das, dvs, dq_inters, dk_states, extras, dks = [], [], [], [], [], []
        for h in heads:
            q, k, v, b = qs[h], ks[h], vs[h], bs[h]
            ab = am_ref[:, h * c:(h + 1) * c]
            dob = do_ref[:, h * GLA_DV:(h + 1) * GLA_DV]
            sp = st_ref[h]
            ds = dstate[h]
            dsb = ds.astype(BF16)
            bc = b[c - 1:c]
            eb = jnp.exp(b)
            ebc = jnp.exp(bc - b)
            ec = jnp.exp(bc)
            qb = (q * eb).astype(BF16)
            kb = (k * ebc).astype(BF16)
            dvs.append(_dot_tn(ab, dob) + _dot_nt(kb, dsb))
            das.append(jnp.where(ii >= jj, _dot_nt(dob, v), 0.0))
            dq_inters.append(_dot(dob, sp) * eb)
            dk_state = _dot(v, dsb) * ebc
            dk_states.append(dk_state)
            extras.append(jnp.sum(k * dk_state, axis=0, keepdims=True)
                          + ec * jnp.sum(sp.astype(F32) * ds, axis=0, keepdims=True))
            dstate[h] = ds * ec + _dot_tn(dob, qb)
            dks.append(jnp.zeros((c, GLA_DK), F32))

        for blk in range(c // SUB):
            r = slice(SUB * blk, SUB * (blk + 1))
            dq_is, dkds = [], []
            for h in heads:
                q, k, b = qs[h], ks[h], bs[h]
                if blk > 0:
                    bprev = b[SUB * blk - 1:SUB * blk]
                    e_i = jnp.exp(b[r] - bprev)
                    ek = jnp.exp(jnp.minimum(bprev - b, 0.0))
                    daoff = jnp.where(col < SUB * blk, das[h][r], 0.0).astype(BF16)
                    dq_is.append(_dot(daoff, (k * ek).astype(BF16)) * e_i)
                    dks[h] = dks[h] + _dot_tn(daoff, (q[r] * e_i).astype(BF16)) * ek
                else:
                    dq_is.append(jnp.zeros((SUB, GLA_DK), F32))
                dkds.append(jnp.zeros((SUB, GLA_DK), F32))
            for j in range(SUB):
                for h in heads:
                    b_i = bs[h][r]
                    e = jnp.where(rowi >= j, jnp.exp(b_i - b_i[j:j + 1]), 0.0)
                    dacol = jnp.sum(jnp.where(col == SUB * blk + j, das[h][r], 0.0), axis=1, keepdims=True)
                    tt = dacol * e
                    dq_is[h] = dq_is[h] + tt * ks[h][r][j:j + 1]
                    dkds[h] = jnp.where(rowi == j, jnp.sum(tt * qs[h][r], axis=0, keepdims=True), dkds[h])
            for h in heads:
                dq_ref[h, r, :] = dq_is[h]
                dkd_ref[h, r, :] = dkds[h]

        for h in heads:
            q, k, b, u, keep = qs[h], ks[h], bs[h], us[h], keeps[h]
            dq = dq_ref[h] + dq_inters[h]
            dk = dks[h] + dkd_ref[h] + dk_states[h]
            db = q * dq - k * dk + jnp.where(rowc == c - 1, extras[h], 0.0)
            triu = (ii <= jj).astype(BF16)
            hi, mid, lo = _split3(db)
            dla = _dot(triu, hi) + _dot(triu, mid) + _dot(triu, lo)
            du = jnp.where(keep, dla * (1.0 / GLA_TAU) / (1.0 + jnp.exp(u)), 0.0)
            du_ref[:, h * GLA_DK:(h + 1) * GLA_DK] = du.astype(BF16)
            oq, ok, ov = h * GLA_DK, GLA_H * GLA_DK + h * GLA_DK, 2 * GLA_H * GLA_DK + h * GLA_DV
            dp_ref[:, oq:oq + GLA_DK] = jnp.where(keep, dq * (GLA_DK ** -0.5), 0.0).astype(BF16)
            dp_ref[:, ok:ok + GLA_DK] = jnp.where(keep, dk, 0.0).astype(BF16)
            dp_ref[:, ov:ov + GLA_DV] = jnp.where(keep, dvs[h], 0.0).astype(BF16)

    nproj = dproj.shape[1]
    return pl.pallas_call(
        body, name=name, grid=(nc,),
        in_specs=[pl.BlockSpec((c, GLA_QKV), lambda n: (nc - 1 - n, 0)), pl.BlockSpec((c, 128), lambda n: (nc - 1 - n, GLA_ZBLK)),
                  pl.BlockSpec((128, GLA_H * GLA_DK), lambda n: (0, 0)), pl.BlockSpec((1, GLA_H * GLA_DK), lambda n: (0, 0)),
                  pl.BlockSpec((c, GLA_H * GLA_DV), lambda n: (nc - 1 - n, 0)),
                  pl.BlockSpec((GLA_H, None, GLA_DV, GLA_DK), lambda n: (0, nc - 1 - n, 0, 0)),
                  pl.BlockSpec((c, GLA_H * c), lambda n: (nc - 1 - n, 0)),
                  pl.BlockSpec((c, GLA_H * GLA_DK), lambda n: (nc - 1 - n, 0)), ANY],
        out_specs=[pl.BlockSpec((c, GLA_QKV), lambda n: (nc - 1 - n, 0)),
                   pl.BlockSpec((c, GLA_H * GLA_DK), lambda n: (nc - 1 - n, 0))],
        out_shape=[S((t, nproj), BF16), S((t, GLA_H * GLA_DK), BF16)],
        input_output_aliases={8: 0},
        scratch_shapes=[pltpu.VMEM((GLA_H, GLA_DV, GLA_DK), F32),
                        pltpu.VMEM((GLA_H, c, GLA_DK), F32), pltpu.VMEM((GLA_H, c, GLA_DK), F32)],
        compiler_params=_cp(dimension_semantics=("arbitrary",)),
    )(proj, proj, wgp, bg, do, states, amat, bcum, dproj)


def _gla_gate_bwd(du, proj, wgp, dproj, name):
    t = du.shape[0]
    tm = _row_tile(t, 704)
    w = GLA_H * GLA_DK

    def body(du_ref, z_ref, wg_ref, dp_in, dp_ref, dwg_ref, dbg_ref):
        @pl.when(pl.program_id(0) == 0)
        def _():
            dwg_ref[...] = jnp.zeros_like(dwg_ref)
            dbg_ref[...] = jnp.zeros_like(dbg_ref)

        d = du_ref[...]
        dp_ref[...] = _dot_nt(d, wg_ref[...]).astype(BF16)
        dwg_ref[...] += _dot_tn(z_ref[...], d)
        dbg_ref[0:1, :] += jnp.sum(d.astype(F32), axis=0, keepdims=True)

    return pl.pallas_call(
        body, name=name, grid=(t // tm,),
        in_specs=[pl.BlockSpec((tm, w), lambda i: (i, 0)), pl.BlockSpec((tm, 128), lambda i: (i, GLA_ZBLK)),
                  pl.BlockSpec((128, w), lambda i: (0, 0)), ANY],
        out_specs=[pl.BlockSpec((tm, 128), lambda i: (i, GLA_ZBLK)), pl.BlockSpec((128, w), lambda i: (0, 0)),
                   pl.BlockSpec((8, w), lambda i: (0, 0))],
        out_shape=[S(dproj.shape, BF16), S((128, w), F32), S((8, w), F32)],
        input_output_aliases={3: 0},
        compiler_params=_cp(dimension_semantics=("arbitrary",)),
    )(du, proj, wgp, dproj)


def _final_loss(h, gain, target, name):
    t = h.shape[0]
    tm = _row_tile(t, 704)

    def body(h_ref, g_ref, t_ref, dh_ref, dgain_ref, loss_ref):
        i = pl.program_id(0)

        @pl.when(i == 0)
        def _():
            dgain_ref[...] = jnp.zeros_like(dgain_ref)
            loss_ref[...] = jnp.zeros_like(loss_ref)

        x = h_ref[...]
        gain = g_ref[...]
        r = lax.rsqrt(jnp.mean(x * x, axis=-1, keepdims=True) + EPS)
        xh = x * r
        rows = i * tm + lax.broadcasted_iota(jnp.int32, (tm, 1), 0)
        e = jnp.where(rows >= CHUNK, xh * gain - t_ref[...], 0.0)
        loss_ref[...] += 0.5 * jnp.sum(jnp.mean(e * e, axis=-1, keepdims=True), axis=0, keepdims=True)
        dy = e * (1.0 / D)
        dgain_ref[0:1, :] += jnp.sum(dy * xh, axis=0, keepdims=True)
        dxh = dy * gain
        dh_ref[...] = r * (dxh - xh * jnp.mean(dxh * xh, axis=-1, keepdims=True))

    row = pl.BlockSpec((tm, D), lambda i: (i, 0))
    return pl.pallas_call(
        body, name=name, grid=(t // tm,),
        in_specs=[row, pl.BlockSpec((1, D), lambda i: (0, 0)), row],
        out_specs=[row, pl.BlockSpec((8, D), lambda i: (0, 0)), pl.BlockSpec((8, 128), lambda i: (0, 0))],
        out_shape=[S((t, D), F32), S((8, D), F32), S((8, 128), F32)],
        compiler_params=_cp(dimension_semantics=("arbitrary",)),
    )(h, gain, target)


def _adam_math(w, g, m, v):
    m2 = ADAM_B1 * m + (1.0 - ADAM_B1) * g
    v2 = ADAM_B2 * v + (1.0 - ADAM_B2) * (g * g)
    m_hat = m2 / (1.0 - ADAM_B1 ** ADAM_STEP)
    v_hat = v2 / (1.0 - ADAM_B2 ** ADAM_STEP)
    delta = -ADAM_LR * (m_hat / (jnp.sqrt(v_hat) + ADAM_EPS) + ADAM_WD * w)
    return delta, m2, v2


def _adamw_reduce(recvs, w, m, v, name):
    nl, r, wd = w.shape
    tr = _row_tile(r, 256) if r % 16 == 0 else r
    nr = r // tr

    def body(*refs):
        rv_refs = refs[:nl]
        w_ref, m_ref, v_ref, g_ref, d_ref, m2_ref, v2_ref = refs[nl:]
        layer = pl.program_id(0)

        def total(rv_ref):
            g = rv_ref[0].astype(F32)
            for s in range(1, N_DEV):
                g = g + rv_ref[s].astype(F32)
            return g

        g = total(rv_refs[0])
        for k in range(1, nl):
            g = jnp.where(layer == k, total(rv_refs[k]), g)
        g_ref[...] = g
        d_ref[...], m2_ref[...], v2_ref[...] = _adam_math(w_ref[...], g, m_ref[...], v_ref[...])

    def rv_spec(k):
        return pl.BlockSpec((N_DEV, tr, wd), lambda l, i: (0, jnp.where(l == k, i, jnp.where(l < k, 0, nr - 1)), 0))

    row = pl.BlockSpec((None, tr, wd), lambda l, i: (l, i, 0))
    return pl.pallas_call(
        body, name=name, grid=(nl, nr),
        in_specs=[rv_spec(k) for k in range(nl)] + [row, row, row],
        out_specs=[row] * 4, out_shape=[S((nl, r, wd), F32)] * 4,
        compiler_params=_cp(dimension_semantics=("arbitrary", "arbitrary")),
    )(*recvs, w, m, v)


def _small_reduce(parts, name):
    _, r, wd = parts.shape

    def body(p_ref, o_ref):
        g = p_ref[0]
        for s in range(1, N_DEV):
            g = g + p_ref[s]
        o_ref[...] = g

    return pl.pallas_call(body, name=name, out_shape=S((r, wd), F32), compiler_params=_cp())(parts)


def _adamw_small(w, g, m, v, name):
    def body(w_ref, g_ref, m_ref, v_ref, d_ref, m2_ref, v2_ref):
        d_ref[...], m2_ref[...], v2_ref[...] = _adam_math(w_ref[...], g_ref[...], m_ref[...], v_ref[...])

    return pl.pallas_call(body, name=name, out_shape=[S(w.shape, F32)] * 3, compiler_params=_cp())(w, g, m, v)


def _unshard_cols(g):
    return jnp.transpose(g, (1, 0, 2)).reshape(g.shape[1], N_DEV * g.shape[2])


def _my_cols(full, width):
    me = 4 * lax.axis_index("x") + 2 * lax.axis_index("y") + lax.axis_index("c")
    return lax.dynamic_slice_in_dim(full, me * width, width, axis=1)


def kernel(x, meta_tokens, norm_ffn1, ffn1_w_in, ffn1_w_out, norm_mix, norm_ffn2, ffn2_w_in, ffn2_w_out, ret_w_in, ret_head_norm, ret_w_out, gla_w_in, gla_w_gate, gla_b_gate, gla_head_norm, gla_w_out, final_norm, loss_target, m_meta_tokens, m_norm_ffn1, m_ffn1_w_in, m_ffn1_w_out, m_norm_mix, m_norm_ffn2, m_ffn2_w_in, m_ffn2_w_out, m_ret_w_in, m_ret_head_norm, m_ret_w_out, m_gla_w_in, m_gla_w_gate, m_gla_b_gate, m_gla_head_norm, m_gla_w_out, m_final_norm, v_meta_tokens, v_norm_ffn1, v_ffn1_w_in, v_ffn1_w_out, v_norm_mix, v_norm_ffn2, v_ffn2_w_in, v_ffn2_w_out, v_ret_w_in, v_ret_head_norm, v_ret_w_out, v_gla_w_in, v_gla_w_gate, v_gla_b_gate, v_gla_head_norm, v_gla_w_out, v_final_norm):
    seq = x.shape[1]
    t = seq + CHUNK
    xs = x[0]
    target = loss_target[0]

    def ffn_w(f):
        w_in, w_out = (ffn1_w_in, ffn1_w_out) if f < 2 else (ffn2_w_in, ffn2_w_out)
        return [w_in[f % 2].astype(BF16), w_out[f % 2].astype(BF16)]

    small = jnp.concatenate([meta_tokens.reshape(-1), ret_head_norm.reshape(-1), gla_w_gate.reshape(-1),
                             gla_b_gate.reshape(-1), gla_head_norm.reshape(-1)])
    n_small = small.shape[0]
    small = jnp.pad(small, (0, 32 * 128 - n_small)).reshape(32, 128)
    sg, win0, wout0 = _run_side(_Gather([small] + ffn_w(0)), "ag_first")
    sg = sg.reshape(N_DEV, 32 * 128)

    def small_cols(off, rows, width):
        return jnp.transpose(sg[:, off:off + rows * width].reshape(N_DEV, rows, width), (1, 0, 2)).reshape(rows, N_DEV * width)

    off = 0
    meta_full = small_cols(off, N_META, D // N_DEV); off += N_META * (D // N_DEV)
    ret_hn = small_cols(off, RET_H, RET_DV // N_DEV).reshape(1, RET_H * RET_DV); off += RET_H * RET_DV // N_DEV
    wgate = small_cols(off, GLA_RANK, GLA_H * GLA_DK // N_DEV); off += GLA_RANK * GLA_H * GLA_DK // N_DEV
    bgate = small_cols(off, 1, GLA_H * GLA_DK // N_DEV); off += GLA_H * GLA_DK // N_DEV
    gla_hn = small_cols(off, GLA_H, GLA_DV // N_DEV).reshape(1, GLA_H * GLA_DV)
    wgp = jnp.pad(wgate, ((0, 128 - GLA_RANK), (0, 0))).astype(BF16)

    cos, sin = _rope_tables(t)
    lgam = _ret_consts()

    h0 = jnp.concatenate([jnp.zeros((PAD, D), F32), meta_full, xs], axis=0)
    g1 = [norm_ffn1[i:i + 1] for i in range(2)]
    gm = [norm_mix[i:i + 1] for i in range(2)]
    g2 = [norm_ffn2[i:i + 1] for i in range(2)]

    (h1, xn_a0, pg_a0, pu_a0), (ret_win_g, ret_wout_g) = _ffn_fwd(
        h0, g1[0], win0, wout0, "ffn1_l0_fwd", side=_Gather([ret_w_in[0].astype(BF16), ret_w_out[0].astype(BF16)]))
    ret_win = ret_win_g
    ret_wout = ret_wout_g.reshape(RET_H * RET_DV, D)
    (rproj, rhn), (win2,) = _norm_mm(h1, gm[0], ret_win, 4 * ret_win.shape[2], "ret_proj_fwd", side=_Gather(ffn_w(2)[:1]))
    ro, rstates = _ret_scan_fwd(rproj, cos, sin, lgam, "ret_scan_fwd")
    (h2, rog), (wout2,) = _post_fwd(ro, rproj, ret_hn, ret_wout, h1, RET_H, RET_DV, "ret_post_fwd", side=_Gather(ffn_w(2)[1:]))
    (h3, xn_b0, pg_b0, pu_b0), (win1, wout1) = _ffn_fwd(h2, g2[0], win2, wout2, "ffn2_l0_fwd", side=_Gather(ffn_w(1)))
    (h4, xn_a1, pg_a1, pu_a1), (gla_win_g, gla_wout_g) = _ffn_fwd(
        h3, g1[1], win1, wout1, "ffn1_l1_fwd", side=_Gather([gla_w_in[0].astype(BF16), gla_w_out[0].astype(BF16)]))
    gla_win = _unshard_cols(gla_win_g)
    gla_win = jnp.pad(gla_win, ((0, 0), (0, GLA_N - gla_win.shape[1])))
    gla_wout = gla_wout_g.reshape(GLA_H * GLA_DV, D)
    (gproj, ghn), _ = _norm_mm(h4, gm[1], gla_win, GLA_N, "gla_proj_fwd")
    (go, gstates, gamat, gbcum), (win3, wout3) = _gla_scan_fwd(gproj, wgp, bgate, "gla_scan_fwd", side=_Gather(ffn_w(3)))
    (h5, gog), _ = _post_fwd(go, gproj, gla_hn, gla_wout, h4, GLA_H, GLA_DV, "gla_post_fwd")
    (h6, xn_b1, pg_b1, pu_b1), _ = _ffn_fwd(h5, g2[1], win3, wout3, "ffn2_l1_fwd")

    dh, dfinal, loss_blk = _final_loss(h6, final_norm.reshape(1, D), jnp.pad(target, ((CHUNK, 0), (0, 0))), "final_loss")
    loss = lax.psum(loss_blk[0, 0], ("x", "y", "c"))

    def ffn_back(dh, h_in, xn, gain, pg, pu, win, wout, tag, side=None, dw_side=None):
        (dh_in, dob, dpg, dpu, act, dgain), got = _ffn_bwd(dh, h_in, gain, pg, pu, win, wout, tag + "_bwd", side=side)
        if dw_side == "split":
            (lo,), _ = _ffn_dw_in(xn, dpg, dpu, tag + "_dw_in_lo", part=(0, 2))
            dwout, got_lo = _mm_tn(act, dob[None], D, tag + "_dw_out", side=_Exchange([lo]))
            dwout = dwout.reshape(N_DEV, FF_SHARD // 2, D)
            (hi,), got_out = _ffn_dw_in(xn, dpg, dpu, tag + "_dw_in_hi", side=_Exchange([dwout]), part=(1, 2))
            return dh_in, [hi, dwout], dgain[0], got, got_lo + got_out
        dwout = _mm_tn(act, dob[None], D, tag + "_dw_out").reshape(N_DEV, FF_SHARD // 2, D)
        (dwin,), dw_got = _ffn_dw_in(xn, dpg, dpu, tag + "_dw_in", side=dw_side)
        return dh_in, [dwin, dwout], dgain[0], got, dw_got

    dh, dw_b1, dg2_1, _, _ = ffn_back(dh, h5, xn_b1, g2[1], pg_b1, pu_b1, win3, wout3, "ffn2_l1")

    (gdo, gdproj, gdhb, dghn), _ = _post_bwd(dh, go, gproj, gla_hn, gla_wout, GLA_H, GLA_DV, GLA_N, "gla_post_bwd")
    d_gla_wout = _mm_tn(gog[None], gdhb[None], D, "gla_dw_out").reshape(N_DEV, GLA_H * GLA_DV // N_DEV, D)
    gdproj, gdu = _gla_scan_bwd(gproj, wgp, bgate, gdo, gstates, gamat, gbcum, gdproj, "gla_scan_bwd")
    gdproj, dwg, dbg = _gla_gate_bwd(gdu, gproj, wgp, gdproj, "gla_gate_bwd")
    d_gla_win = _mm_tn(gdproj[None], ghn[None], D, "gla_dw_in", tm=640)[0]
    (dh, dgm_1), _ = _proj_bwd(gdproj, gla_win, dh, h4, gm[1], GLA_N, "gla_proj_bwd")
    n_gla_in = 2 * GLA_H * GLA_DK + 2 * GLA_H * GLA_DV + GLA_RANK
    d_gla_win = d_gla_win[:n_gla_in].reshape(N_DEV, n_gla_in // N_DEV, D)

    dh, dw_a1, dg1_1, rv_b1, rv_gla = ffn_back(dh, h3, xn_a1, g1[1], pg_a1, pu_a1, win1, wout1, "ffn1_l1",
                                               side=_Exchange(dw_b1), dw_side=_Exchange([d_gla_win, d_gla_wout]))
    dh, dw_b0, dg2_0, rv_a1, _ = ffn_back(dh, h2, xn_b0, g2[0], pg_b0, pu_b0, win2, wout2, "ffn2_l0", side=_Exchange(dw_a1))

    (rdo, rdproj, rdhb, drhn), rv_b0_out = _post_bwd(dh, ro, rproj, ret_hn, ret_wout, RET_H, RET_DV, 6 * D, "ret_post_bwd",
                                                     side=_Exchange(dw_b0[1:]))
    d_ret_wout = _mm_tn(rog[None], rdhb[None], D, "ret_dw_out", rows=DW_ROWS // 2).reshape(N_DEV, RET_H * RET_DV // N_DEV, D)
    (rdproj,), rv_b0_in = _ret_scan_bwd(rproj, cos, sin, lgam, rdo, rstates, rdproj, "ret_scan_bwd", side=_Exchange(dw_b0[:1]))
    rv_b0 = rv_b0_in + rv_b0_out
    d_ret_win = _mm_tn(rhn[None], rdproj[None], ret_win.shape[2], "ret_dw_in", shard_out=True)
    (dh, dgm_0), rv_ret_out = _proj_bwd(rdproj, ret_win, dh, h1, gm[0], 4 * ret_win.shape[2], "ret_proj_bwd", side=_Exchange([d_ret_wout]))

    dh, dw_a0, dg1_0, rv_ret_in, (rv_a0_lo, rv_a0_out) = ffn_back(dh, h0, xn_a0, g1[0], pg_a0, pu_a0, win0, wout0, "ffn1_l0",
                                                                  side=_Exchange([d_ret_win]), dw_side="split")
    rv_ret = rv_ret_in + rv_ret_out
    rv_a0_hi = _run_side(_Exchange(dw_a0[:1]), "xchg_last")[0]
    rv_a0 = [jnp.concatenate([rv_a0_lo, rv_a0_hi], axis=2), rv_a0_out]
    grad_x = dh[CHUNK:][None]

    def adam_t(recvs, w, m, v, tag):
        outs = _adamw_reduce(recvs, *(jnp.swapaxes(a, 1, 2) for a in (w, m, v)), tag)
        return [jnp.swapaxes(o, 1, 2) for o in outs]

    u_ffn1_in = adam_t([rv_a0[0], rv_a1[0]], ffn1_w_in, m_ffn1_w_in, v_ffn1_w_in, "adam_ffn1_w_in")
    u_ffn2_in = adam_t([rv_b0[0], rv_b1[0]], ffn2_w_in, m_ffn2_w_in, v_ffn2_w_in, "adam_ffn2_w_in")
    u_ffn1_out = _adamw_reduce([rv_a0[1], rv_a1[1]], ffn1_w_out, m_ffn1_w_out, v_ffn1_w_out, "adam_ffn1_w_out")
    u_ffn2_out = _adamw_reduce([rv_b0[1], rv_b1[1]], ffn2_w_out, m_ffn2_w_out, v_ffn2_w_out, "adam_ffn2_w_out")
    u_ret_in = _adamw_reduce([rv_ret[0]], ret_w_in, m_ret_w_in, v_ret_w_in, "adam_ret_w_in")
    u_ret_out = _adamw_reduce([rv_ret[1]], ret_w_out, m_ret_w_out, v_ret_w_out, "adam_ret_w_out")
    u_gla_in = adam_t([rv_gla[0]], gla_w_in, m_gla_w_in, v_gla_w_in, "adam_gla_w_in")
    u_gla_out = _adamw_reduce([rv_gla[1]], gla_w_out, m_gla_w_out, v_gla_w_out, "adam_gla_w_out")

    dmeta = dh[PAD:CHUNK]
    parts = jnp.concatenate([
        dg1_0, dg1_1, dgm_0[0], dgm_1[0], dg2_0, dg2_1, dfinal[0], dmeta.reshape(-1), drhn[0], dwg[:GLA_RANK].reshape(-1),
        dbg[0], dghn[0]])
    n_parts = parts.shape[0]
    rows = -(-n_parts // D)
    rows = -(-rows // 8) * 8
    parts = jnp.pad(parts, (0, rows * D - n_parts)).reshape(rows, D)
    tot = _small_reduce(_run_side(_Gather([parts]), "ag_small_grads")[0], "small_grad_sum").reshape(-1)

    off = 0
    def take(nel):
        nonlocal off
        out = tot[off:off + nel]
        off += nel
        return out

    gr_norm_ffn1 = take(2 * D).reshape(2, D)
    gr_norm_mix = take(2 * D).reshape(2, D)
    gr_norm_ffn2 = take(2 * D).reshape(2, D)
    gr_final = take(D)
    gr_meta = _my_cols(take(N_META * D).reshape(N_META, D), D // N_DEV)
    gr_ret_hn = _my_cols(take(RET_H * RET_DV).reshape(RET_H, RET_DV), RET_DV // N_DEV)[None]
    gr_wgate = _my_cols(take(GLA_RANK * GLA_H * GLA_DK).reshape(GLA_RANK, GLA_H * GLA_DK), GLA_H * GLA_DK // N_DEV)[None]
    gr_bgate = _my_cols(take(GLA_H * GLA_DK).reshape(1, GLA_H * GLA_DK), GLA_H * GLA_DK // N_DEV)
    gr_gla_hn = _my_cols(take(GLA_H * GLA_DV).reshape(GLA_H, GLA_DV), GLA_DV // N_DEV)[None]

    small_w = [meta_tokens, norm_ffn1, norm_mix, norm_ffn2, ret_head_norm, gla_w_gate, gla_b_gate, gla_head_norm, final_norm]
    small_g = [gr_meta, gr_norm_ffn1, gr_norm_mix, gr_norm_ffn2, gr_ret_hn, gr_wgate, gr_bgate, gr_gla_hn, gr_final]
    small_m = [m_meta_tokens, m_norm_ffn1, m_norm_mix, m_norm_ffn2, m_ret_head_norm, m_gla_w_gate, m_gla_b_gate, m_gla_head_norm, m_final_norm]
    small_v = [v_meta_tokens, v_norm_ffn1, v_norm_mix, v_norm_ffn2, v_ret_head_norm, v_gla_w_gate, v_gla_b_gate, v_gla_head_norm, v_final_norm]

    def pack(arrs):
        flat = jnp.concatenate([a.reshape(-1) for a in arrs])
        n = flat.shape[0]
        r = -(-n // 128)
        r = -(-r // 8) * 8
        return jnp.pad(flat, (0, r * 128 - n), constant_values=1.0).reshape(r, 128)

    sd, sm, sv = _adamw_small(pack(small_w), pack(small_g), pack(small_m), pack(small_v), "adam_small")

    def unpack(buf):
        flat = buf.reshape(-1)
        outs, o = [], 0
        for a in small_w:
            outs.append(flat[o:o + a.size].reshape(a.shape))
            o += a.size
        return outs

    us_d, us_m, us_v = unpack(sd), unpack(sm), unpack(sv)

    def ordered(k, smalls):
        return (smalls[0], smalls[1], u_ffn1_in[k], u_ffn1_out[k], smalls[2], smalls[3], u_ffn2_in[k], u_ffn2_out[k],
                u_ret_in[k], smalls[4], u_ret_out[k], u_gla_in[k], smalls[5], smalls[6], smalls[7], u_gla_out[k], smalls[8])

    return (loss, grad_x, *ordered(0, small_g), *ordered(1, us_d), *ordered(2, us_m), *ordered(3, us_v))
```

```python
import functools
import math

import numpy as np
import jax
import jax.numpy as jnp
from jax import lax
from jax.experimental import pallas as pl
from jax.experimental.pallas import tpu as pltpu

F32 = jnp.float32
BF16 = jnp.bfloat16
S = jax.ShapeDtypeStruct
ANY = pl.BlockSpec(memory_space=pl.ANY)
MESH = pl.DeviceIdType.MESH

D = 1024
N_META = 16
CHUNK = 64
PAD = CHUNK - N_META
EPS = 1e-6
N_DEV = 8
FF_SHARD = 704
N_FF_CHUNK = 4
RET_H, RET_DK, RET_DV = 4, 256, 512
RET_QKV = RET_H * (2 * RET_DK + RET_DV)
RET_C = 192
GLA_H, GLA_DK, GLA_DV, GLA_RANK, GLA_TAU = 4, 128, 256, 16, 16.0
GLA_QKV = GLA_H * (2 * GLA_DK + GLA_DV)
GLA_N = 3200
GLA_ZBLK = 3072 // 128
SUB = 16
ROPE_BASE = 10000.0
ADAM_LR, ADAM_B1, ADAM_B2, ADAM_EPS, ADAM_WD, ADAM_STEP = 0.001, 0.9, 0.999, 1e-08, 0.01, 10
VMEM_LIMIT = 58 * 1024 * 1024
DW_ROWS = 2752


def _cp(**kw):
    return pltpu.CompilerParams(vmem_limit_bytes=VMEM_LIMIT, **kw)


def _row_tile(t, cap):
    best = 16
    for d in range(16, cap + 1, 16):
        if t % d == 0:
            best = d
    return best


def _sub_rows(tm, parts=2):
    units = tm // 16
    cuts = [16 * (units * p // parts) for p in range(parts + 1)]
    return [slice(a, b) for a, b in zip(cuts[:-1], cuts[1:]) if b > a]


def _dot(a, b):
    return jnp.dot(a, b, preferred_element_type=F32)


def _dot_nt(a, b):
    return lax.dot_general(a, b, (((1,), (1,)), ((), ())), preferred_element_type=F32)


def _dot_tn(a, b):
    return lax.dot_general(a, b, (((0,), (0,)), ((), ())), preferred_element_type=F32)


def _sigmoid(x):
    return pl.reciprocal(1.0 + jnp.exp(-x), approx=True)


def _rms_bwd(dxn, x, gain):
    r = lax.rsqrt(jnp.mean(x * x, axis=-1, keepdims=True) + EPS)
    xh = x * r
    dxh = dxn * gain
    dx = r * (dxh - xh * jnp.mean(dxh * xh, axis=-1, keepdims=True))
    return dx, jnp.sum(dxn * xh, axis=0, keepdims=True)


def _xyc():
    return lax.axis_index("x"), lax.axis_index("y"), lax.axis_index("c")


class _Gather:
    def __init__(self, xs):
        self.xs = list(xs)
        self.n = len(self.xs)

    def out_shape(self):
        return [S((N_DEV,) + a.shape, a.dtype) for a in self.xs]

    def scratch(self):
        return [pltpu.SemaphoreType.DMA((self.n, 7)), pltpu.SemaphoreType.DMA((self.n, 7)), pltpu.SemaphoreType.DMA((self.n,))]

    def phases(self, x_refs, out_refs, send_sems, recv_sems, local_sems):
        x, y, c = _xyc()
        me, sibling = (x, y, c), (x, y, 1 - c)
        chips = [(1 - x, y), (x, 1 - y), (1 - x, 1 - y)]

        def copy(t, k, block, to, src=None):
            px, py, pc = block
            dst = out_refs[t].at[4 * px + 2 * py + pc]
            return pltpu.make_async_remote_copy(
                src_ref=dst if src is None else src, dst_ref=dst,
                send_sem=send_sems.at[t, k], recv_sem=recv_sems.at[t, k], device_id=to, device_id_type=MESH)

        def own(t):
            return pltpu.make_async_copy(x_refs[t], out_refs[t].at[4 * x + 2 * y + c], local_sems.at[t])

        def first(t):
            return [copy(t, 0, me, sibling, src=x_refs[t])] + [
                copy(t, 1 + j, me, (*chip, c), src=x_refs[t]) for j, chip in enumerate(chips)]

        def passed(t):
            return [copy(t, 4 + j, (*chip, c), sibling) for j, chip in enumerate(chips)]

        def start():
            for t in range(self.n):
                own(t).start()
                for cp in first(t):
                    cp.start()

        def mid():
            for t in range(self.n):
                fw = passed(t)
                for j, chip in enumerate(chips):
                    copy(t, 1 + j, (*chip, c), me).wait_recv()
                    fw[j].start()

        def finish():
            for t in range(self.n):
                copy(t, 0, sibling, me).wait_recv()
                for j, chip in enumerate(chips):
                    copy(t, 4 + j, (*chip, 1 - c), me).wait_recv()
                for cp in first(t) + passed(t):
                    cp.wait_send()
                own(t).wait()

        return start, mid, finish


class _Exchange:
    def __init__(self, xs):
        self.xs = list(xs)
        self.n = len(self.xs)

    def out_shape(self):
        return [S(a.shape, a.dtype) for a in self.xs]

    def scratch(self):
        return [pltpu.SemaphoreType.DMA((self.n, 7)), pltpu.SemaphoreType.DMA((self.n, 7)), pltpu.SemaphoreType.DMA((self.n,))]

    def phases(self, g_refs, r_refs, send_sems, recv_sems, local_sems):
        x, y, c = _xyc()
        me = 4 * x + 2 * y + c

        def own(t):
            return pltpu.make_async_copy(g_refs[t].at[me], r_refs[t].at[me], local_sems.at[t])

        def send(t, m):
            px, py, pc = x ^ (m >> 2), y ^ ((m >> 1) & 1), c ^ (m & 1)
            return pltpu.make_async_remote_copy(
                src_ref=g_refs[t].at[4 * px + 2 * py + pc], dst_ref=r_refs[t].at[me],
                send_sem=send_sems.at[t, m - 1], recv_sem=recv_sems.at[t, m - 1],
                device_id=(px, py, pc), device_id_type=MESH)

        def arrival(t, m):
            peer = 4 * (x ^ (m >> 2)) + 2 * (y ^ ((m >> 1) & 1)) + (c ^ (m & 1))
            return pltpu.make_async_remote_copy(
                src_ref=g_refs[t].at[peer], dst_ref=r_refs[t].at[peer],
                send_sem=send_sems.at[t, m - 1], recv_sem=recv_sems.at[t, m - 1],
                device_id=(x, y, c), device_id_type=MESH)

        def start():
            for t in range(self.n):
                own(t).start()
            for m in range(1, N_DEV):
                for t in range(self.n):
                    send(t, m).start()

        def mid():
            pass

        def finish():
            for m in range(1, N_DEV):
                for t in range(self.n):
                    arrival(t, m).wait_recv()
            for m in range(1, N_DEV):
                for t in range(self.n):
                    send(t, m).wait_send()
            for t in range(self.n):
                own(t).wait()

        return start, mid, finish


def _run_side(side, name):
    n = side.n

    def body(*refs):
        start, mid, finish = side.phases(refs[:n], refs[n:2 * n], *refs[2 * n:])
        start()
        mid()
        finish()

    return list(pl.pallas_call(
        body, name=name, out_shape=side.out_shape(), in_specs=[ANY] * n, out_specs=[ANY] * n,
        scratch_shapes=side.scratch())(*side.xs))


def _grid_steps(grid):
    def ids():
        return [pl.program_id(a) for a in range(len(grid))]

    def first():
        return functools.reduce(jnp.logical_and, [i == 0 for i in ids()])

    def middle():
        i = ids()
        return functools.reduce(jnp.logical_and, [i[0] == (3 * grid[0]) // 4] + [j == 0 for j in i[1:]])

    def last():
        return functools.reduce(jnp.logical_and, [i == g - 1 for i, g in zip(ids(), grid)])

    return first, middle, last


def _call(body, *, name, grid, in_specs, out_specs, out_shape, scratch_shapes, operands, side=None, aliases=None):
    n_in, n_out, n_scr = len(in_specs), len(out_shape), len(scratch_shapes)
    full = body
    if side is not None:
        ns = side.n
        first, middle, last = _grid_steps(grid)

        def full(*refs):
            a = n_in
            ins, sins = refs[:a], refs[a:a + ns]
            a += ns
            outs, souts = refs[a:a + n_out], refs[a + n_out:a + n_out + ns]
            a += n_out + ns
            scr, sems = refs[a:a + n_scr], refs[a + n_scr:]
            start, mid, finish = side.phases(sins, souts, *sems)
            pl.when(first())(start)
            body(*ins, *outs, *scr)
            pl.when(middle())(mid)
            pl.when(last())(finish)

        in_specs = list(in_specs) + [ANY] * ns
        out_specs = list(out_specs) + [ANY] * ns
        out_shape = list(out_shape) + side.out_shape()
        scratch_shapes = list(scratch_shapes) + side.scratch()
        operands = list(operands) + side.xs
    outs = pl.pallas_call(
        full, name=name, grid=grid, in_specs=list(in_specs), out_specs=list(out_specs), out_shape=list(out_shape),
        scratch_shapes=list(scratch_shapes), input_output_aliases=aliases or {},
        compiler_params=_cp(dimension_semantics=("arbitrary",) * len(grid)),
    )(*operands)
    return list(outs[:n_out]), list(outs[n_out:])


def _ffn_fwd(h, gain, win, wout, name, side=None):
    t = h.shape[0]
    tm = _row_tile(t, 704)
    nt = t // tm

    def body(h_ref, g_ref, wg_ref, wu_ref, wo_ref, hn_ref, xn_ref, pg_ref, pu_ref, acc):
        c = pl.program_id(1)

        @pl.when(c == 0)
        def _():
            x = h_ref[...]
            r = lax.rsqrt(jnp.mean(x * x, axis=-1, keepdims=True) + EPS)
            xn_ref[...] = (x * r * g_ref[...]).astype(BF16)
            acc[...] = jnp.zeros_like(acc)

        wo = wo_ref[...].reshape(FF_SHARD, D)
        subs = _sub_rows(tm)
        gus = [(_dot(xn_ref[r, :], wg_ref[...]), _dot(xn_ref[r, :], wu_ref[...])) for r in subs]
        for r, (g, u) in zip(subs, gus):
            pg_ref[r, :] = g.astype(BF16)
            pu_ref[r, :] = u.astype(BF16)
            act = (g * _sigmoid(g) * u).astype(BF16)
            acc[r, :] += _dot(act, wo)

        @pl.when(c == N_FF_CHUNK - 1)
        def _():
            hn_ref[...] = h_ref[...] + 0.5 * acc[...]

    return _call(
        body, name=name, grid=(nt, N_FF_CHUNK), side=side,
        in_specs=[
            pl.BlockSpec((tm, D), lambda i, c: (i, 0)),
            pl.BlockSpec((1, D), lambda i, c: (0, 0)),
            pl.BlockSpec((None, D, FF_SHARD), lambda i, c: (c, 0, 0)),
            pl.BlockSpec((None, D, FF_SHARD), lambda i, c: (c + N_FF_CHUNK, 0, 0)),
            pl.BlockSpec((2, FF_SHARD // 2, D), lambda i, c: (c, 0, 0)),
        ],
        out_specs=[
            pl.BlockSpec((tm, D), lambda i, c: (i, 0)),
            pl.BlockSpec((tm, D), lambda i, c: (i, 0)),
            pl.BlockSpec((None, tm, FF_SHARD), lambda i, c: (c, i, 0)),
            pl.BlockSpec((None, tm, FF_SHARD), lambda i, c: (c, i, 0)),
        ],
        out_shape=[S((t, D), F32), S((t, D), BF16), S((N_FF_CHUNK, t, FF_SHARD), BF16), S((N_FF_CHUNK, t, FF_SHARD), BF16)],
        scratch_shapes=[pltpu.VMEM((tm, D), F32)],
        operands=[h, gain, win, win, wout])


def _ffn_bwd(dh, h, gain, pg, pu, win, wout, name, side=None):
    t = h.shape[0]
    tm = _row_tile(t, 704)
    nt = t // tm

    def body(dh_ref, h_ref, g_ref, pg_ref, pu_ref, wg_ref, wu_ref, wo_ref,
             dhi_ref, dob_ref, dpg_ref, dpu_ref, act_ref, dgain_ref, acc):
        i, c = pl.program_id(0), pl.program_id(1)

        @pl.when(c == 0)
        def _():
            dob_ref[...] = (0.5 * dh_ref[...]).astype(BF16)
            acc[...] = jnp.zeros_like(acc)

        @pl.when((i == 0) & (c == 0))
        def _():
            dgain_ref[...] = jnp.zeros_like(dgain_ref)

        wo = wo_ref[...].reshape(FF_SHARD, D)
        subs = _sub_rows(tm)
        dacts = [_dot_nt(dob_ref[r, :], wo) for r in subs]
        for r, dact in zip(subs, dacts):
            g = pg_ref[r, :].astype(F32)
            u = pu_ref[r, :].astype(F32)
            s = _sigmoid(g)
            sl = g * s
            act_ref[r, :] = (sl * u).astype(BF16)
            dg = (dact * u * (s * (1.0 + g * (1.0 - s)))).astype(BF16)
            du = (dact * sl).astype(BF16)
            dpg_ref[r, :] = dg
            dpu_ref[r, :] = du
            acc[r, :] += _dot_nt(dg, wg_ref[...]) + _dot_nt(du, wu_ref[...])

        @pl.when(c == N_FF_CHUNK - 1)
        def _():
            dx, dgn = _rms_bwd(acc[...], h_ref[...], g_ref[...])
            dhi_ref[...] = dh_ref[...] + dx
            dgain_ref[0:1, :] += dgn

    blk = pl.BlockSpec((None, tm, FF_SHARD), lambda i, c: (c, i, 0))
    row = pl.BlockSpec((tm, D), lambda i, c: (i, 0))
    return _call(
        body, name=name, grid=(nt, N_FF_CHUNK), side=side,
        in_specs=[
            row, row, pl.BlockSpec((1, D), lambda i, c: (0, 0)), blk, blk,
            pl.BlockSpec((None, D, FF_SHARD), lambda i, c: (c, 0, 0)),
            pl.BlockSpec((None, D, FF_SHARD), lambda i, c: (c + N_FF_CHUNK, 0, 0)),
            pl.BlockSpec((2, FF_SHARD // 2, D), lambda i, c: (c, 0, 0)),
        ],
        out_specs=[row, row, blk, blk, blk, pl.BlockSpec((8, D), lambda i, c: (0, 0))],
        out_shape=[S((t, D), F32), S((t, D), BF16)] + [S((N_FF_CHUNK, t, FF_SHARD), BF16)] * 3 + [S((8, D), F32)],
        scratch_shapes=[pltpu.VMEM((tm, D), F32)],
        operands=[dh, h, gain, pg, pu, win, win, wout])


def _ffn_dw_in(xn, dpg, dpu, name, side=None):
    t = xn.shape[0]
    tk = _row_tile(t, DW_ROWS)
    nk = t // tk

    def body(a_ref, bg_ref, bu_ref, o_ref, acc):
        c, k = pl.program_id(0), pl.program_id(1)

        @pl.when(k == 0)
        def _():
            acc[...] = jnp.zeros_like(acc)

        @pl.when(c < N_FF_CHUNK)
        def _():
            acc[...] += _dot_tn(bg_ref[...], a_ref[...])

        @pl.when(c >= N_FF_CHUNK)
        def _():
            acc[...] += _dot_tn(bu_ref[...], a_ref[...])

        @pl.when(k == nk - 1)
        def _():
            o_ref[...] = acc[...].astype(BF16)

    return _call(
        body, name=name, grid=(2 * N_FF_CHUNK, nk), side=side,
        in_specs=[
            pl.BlockSpec((tk, D), lambda c, k: (k, 0)),
            pl.BlockSpec((None, tk, FF_SHARD), lambda c, k: (jnp.minimum(c, N_FF_CHUNK - 1), k, 0)),
            pl.BlockSpec((None, tk, FF_SHARD), lambda c, k: (jnp.maximum(c - N_FF_CHUNK, 0), k, 0)),
        ],
        out_specs=[pl.BlockSpec((None, FF_SHARD, D), lambda c, k: (c, 0, 0))],
        out_shape=[S((2 * N_FF_CHUNK, FF_SHARD, D), BF16)],
        scratch_shapes=[pltpu.VMEM((FF_SHARD, D), F32)],
        operands=[xn, dpg, dpu])


def _mm_tn(a, b, tn, name, tm=None, rows=DW_ROWS, shard_out=False):
    ca, t, m = a.shape
    cb, _, n = b.shape
    nc = max(ca, cb)
    tm = m if tm is None else tm
    tk = _row_tile(t, rows)
    nk = t // tk

    def body(a_ref, b_ref, o_ref, acc):
        k = pl.program_id(3)

        @pl.when(k == 0)
        def _():
            acc[...] = jnp.zeros_like(acc)

        acc[...] += _dot_tn(a_ref[...], b_ref[...])

        @pl.when(k == nk - 1)
        def _():
            o_ref[...] = acc[...].astype(BF16)

    if shard_out:
        out_spec = pl.BlockSpec((None, tm, tn), lambda c, i, j, k: (j, 0, 0))
        out_shape = S((n // tn, m, tn), BF16)
    else:
        out_spec = pl.BlockSpec((None, tm, tn), lambda c, i, j, k: (c, i, j))
        out_shape = S((nc, m, n), BF16)
    return pl.pallas_call(
        body, name=name, grid=(nc, m // tm, n // tn, nk),
        in_specs=[
            pl.BlockSpec((None, tk, tm), (lambda c, i, j, k: (c, k, i)) if ca > 1 else (lambda c, i, j, k: (0, k, i))),
            pl.BlockSpec((None, tk, tn), (lambda c, i, j, k: (c, k, j)) if cb > 1 else (lambda c, i, j, k: (0, k, j))),
        ],
        out_specs=out_spec, out_shape=out_shape,
        scratch_shapes=[pltpu.VMEM((tm, tn), F32)],
        compiler_params=_cp(dimension_semantics=("arbitrary",) * 4),
    )(a, b)


def _norm_mm(h, gain, w, tn, name, side=None):
    t = h.shape[0]
    n = w.shape[-1] if w.ndim == 2 else w.shape[0] * w.shape[2]
    tm = _row_tile(t, 704)
    kb = 1 if w.ndim == 2 else tn // w.shape[2]
    w_spec = (pl.BlockSpec((D, tn), lambda i, j: (0, j)) if w.ndim == 2
              else pl.BlockSpec((kb, D, tn // kb), lambda i, j: (j, 0, 0)))

    def body(h_ref, g_ref, w_ref, o_ref, xn_ref):
        @pl.when(pl.program_id(1) == 0)
        def _():
            x = h_ref[...]
            r = lax.rsqrt(jnp.mean(x * x, axis=-1, keepdims=True) + EPS)
            xn_ref[...] = (x * r * g_ref[...]).astype(BF16)

        if w.ndim == 2:
            o_ref[...] = _dot(xn_ref[...], w_ref[...]).astype(BF16)
        else:
            for b in range(kb):
                o_ref[:, b * (tn // kb):(b + 1) * (tn // kb)] = _dot(xn_ref[...], w_ref[b]).astype(BF16)

    return _call(
        body, name=name, grid=(t // tm, n // tn), side=side,
        in_specs=[pl.BlockSpec((tm, D), lambda i, j: (i, 0)), pl.BlockSpec((1, D), lambda i, j: (0, 0)), w_spec],
        out_specs=[pl.BlockSpec((tm, tn), lambda i, j: (i, j)), pl.BlockSpec((tm, D), lambda i, j: (i, 0))],
        out_shape=[S((t, n), BF16), S((t, D), BF16)], scratch_shapes=[],
        operands=[h, gain, w])


def _proj_bwd(dproj, w, dh, h, gain, tk, name, side=None):
    t, n = dproj.shape
    tm = _row_tile(t, 704)
    nk = n // tk
    kb = 1 if w.ndim == 2 else tk // w.shape[2]
    w_spec = (pl.BlockSpec((D, tk), lambda i, k: (0, k)) if w.ndim == 2
              else pl.BlockSpec((kb, D, tk // kb), lambda i, k: (k, 0, 0)))

    def body(dp_ref, w_ref, dh_ref, h_ref, g_ref, dhi_ref, dgain_ref, acc):
        i, k = pl.program_id(0), pl.program_id(1)

        @pl.when(k == 0)
        def _():
            acc[...] = jnp.zeros_like(acc)

        @pl.when((i == 0) & (k == 0))
        def _():
            dgain_ref[...] = jnp.zeros_like(dgain_ref)

        if w.ndim == 2:
            acc[...] += _dot_nt(dp_ref[...], w_ref[...])
        else:
            for b in range(kb):
                acc[...] += _dot_nt(dp_ref[:, b * (tk // kb):(b + 1) * (tk // kb)], w_ref[b])

        @pl.when(k == nk - 1)
        def _():
            dx, dgn = _rms_bwd(acc[...], h_ref[...], g_ref[...])
            dhi_ref[...] = dh_ref[...] + dx
            dgain_ref[0:1, :] += dgn

    row = pl.BlockSpec((tm, D), lambda i, k: (i, 0))
    return _call(
        body, name=name, grid=(t // tm, nk), side=side,
        in_specs=[pl.BlockSpec((tm, tk), lambda i, k: (i, k)), w_spec,
                  row, row, pl.BlockSpec((1, D), lambda i, k: (0, 0))],
        out_specs=[row, pl.BlockSpec((8, D), lambda i, k: (0, 0))],
        out_shape=[S((t, D), F32), S((8, D), F32)],
        scratch_shapes=[pltpu.VMEM((tm, D), F32)],
        operands=[dproj, w, dh, h, gain])


def _post_fwd(o, proj, hgain, wout, h, nh, dv, name, side=None):
    t = h.shape[0]
    w = nh * dv
    tm = _row_tile(t, 704)

    def body(o_ref, g_ref, hg_ref, wo_ref, h_ref, hn_ref, og_ref):
        for hd in range(nh):
            sl = slice(hd * dv, (hd + 1) * dv)
            oh = o_ref[:, sl].astype(F32)
            r = lax.rsqrt(jnp.mean(oh * oh, axis=-1, keepdims=True) + EPS)
            gg = g_ref[:, sl].astype(F32)
            og_ref[:, sl] = (oh * r * hg_ref[:, sl] * (gg * _sigmoid(gg))).astype(BF16)
        hn_ref[...] = h_ref[...] + _dot(og_ref[...], wo_ref[...])

    return _call(
        body, name=name, grid=(t // tm,), side=side,
        in_specs=[pl.BlockSpec((tm, w), lambda i: (i, 0)), pl.BlockSpec((tm, w), lambda i: (i, 2)),
                  pl.BlockSpec((1, w), lambda i: (0, 0)), pl.BlockSpec((w, D), lambda i: (0, 0)),
                  pl.BlockSpec((tm, D), lambda i: (i, 0))],
        out_specs=[pl.BlockSpec((tm, D), lambda i: (i, 0)), pl.BlockSpec((tm, w), lambda i: (i, 0))],
        out_shape=[S((t, D), F32), S((t, w), BF16)], scratch_shapes=[],
        operands=[o, proj, hgain, wout, h])


def _post_bwd(dh, o, proj, hgain, wout, nh, dv, nproj, name, side=None):
    t = dh.shape[0]
    w = nh * dv
    tm = _row_tile(t, 704)

    def body(dh_ref, o_ref, g_ref, hg_ref, wo_ref, do_ref, dg_ref, dhb_ref, dhg_ref):
        @pl.when(pl.program_id(0) == 0)
        def _():
            dhg_ref[...] = jnp.zeros_like(dhg_ref)

        dmix = dh_ref[...].astype(BF16)
        dhb_ref[...] = dmix
        dog = _dot_nt(dmix, wo_ref[...])
        for hd in range(nh):
            sl = slice(hd * dv, (hd + 1) * dv)
            oh = o_ref[:, sl].astype(F32)
            r = lax.rsqrt(jnp.mean(oh * oh, axis=-1, keepdims=True) + EPS)
            xh = oh * r
            gain = hg_ref[:, sl]
            gg = g_ref[:, sl].astype(F32)
            s = _sigmoid(gg)
            dogh = dog[:, sl]
            don = dogh * (gg * s)
            dg_ref[:, sl] = (dogh * (xh * gain) * (s * (1.0 + gg * (1.0 - s)))).astype(BF16)
            dxh = don * gain
            do_ref[:, sl] = (r * (dxh - xh * jnp.mean(dxh * xh, axis=-1, keepdims=True))).astype(BF16)
            dhg_ref[0:1, sl] += jnp.sum(don * xh, axis=0, keepdims=True)

    return _call(
        body, name=name, grid=(t // tm,), side=side,
        in_specs=[pl.BlockSpec((tm, D), lambda i: (i, 0)), pl.BlockSpec((tm, w), lambda i: (i, 0)),
                  pl.BlockSpec((tm, w), lambda i: (i, 2)), pl.BlockSpec((1, w), lambda i: (0, 0)),
                  pl.BlockSpec((w, D), lambda i: (0, 0))],
        out_specs=[pl.BlockSpec((tm, w), lambda i: (i, 0)), pl.BlockSpec((tm, w), lambda i: (i, 2)),
                   pl.BlockSpec((tm, D), lambda i: (i, 0)), pl.BlockSpec((8, w), lambda i: (0, 0))],
        out_shape=[S((t, w), BF16), S((t, nproj), BF16), S((t, D), BF16), S((8, w), F32)], scratch_shapes=[],
        operands=[dh, o, proj, hgain, wout])


def _ret_consts():
    lg = np.log1p(-np.exp2(-5.0 - np.arange(RET_H, dtype=np.float32))).astype(np.float32)
    return jnp.asarray(np.broadcast_to(lg[:, None, None], (RET_H, 1, 128)).copy())


def _rope_tables(t):
    half = RET_DK // 2
    inv = 1.0 / (ROPE_BASE ** jnp.linspace(0.0, 1.0, half, dtype=F32))
    base = (jnp.arange(t // CHUNK) * CHUNK - PAD).astype(F32)[:, None] * inv[None, :]
    off = jnp.arange(CHUNK).astype(F32)[:, None] * inv[None, :]
    ca, sa = jnp.cos(base)[:, None, :], jnp.sin(base)[:, None, :]
    cb, sb = jnp.cos(off)[None], jnp.sin(off)[None]
    return (ca * cb - sa * sb).reshape(t, half), (sa * cb + ca * sb).reshape(t, half)


def _ret_chunk(blk_ref, cos_ref, sin_ref, lg, h):
    c = RET_C
    half = RET_DK // 2
    oq, ok, ov = h * RET_DK, RET_H * RET_DK + h * RET_DK, 2 * RET_H * RET_DK + h * RET_DV
    cs, sn = cos_ref[...], sin_ref[...]
    q1, q2 = blk_ref[:, oq:oq + half].astype(F32), blk_ref[:, oq + half:oq + RET_DK].astype(F32)
    k1, k2 = blk_ref[:, ok:ok + half].astype(F32), blk_ref[:, ok + half:ok + RET_DK].astype(F32)
    qr = jnp.concatenate([q1 * cs - q2 * sn, q1 * sn + q2 * cs], axis=1)
    kr = jnp.concatenate([k1 * cs - k2 * sn, k1 * sn + k2 * cs], axis=1) * (RET_DK ** -0.5)
    v = blk_ref[:, ov:ov + RET_DV]
    ii = lax.broadcasted_iota(jnp.int32, (c, 1), 0).astype(F32)
    jj = lax.broadcasted_iota(jnp.int32, (1, c), 1).astype(F32)
    rel = ii - jj
    dmat = jnp.where(rel >= 0, jnp.exp(lg * jnp.maximum(rel, 0.0)), 0.0)
    dq = jnp.exp(lg * (ii + 1.0))
    dk = jnp.exp(lg * (c - 1.0 - ii))
    dchunk = jnp.exp(lg * float(c))
    return qr, kr, v, dmat, dq, dk, dchunk


def _ret_scan_fwd(proj, cos, sin, lgam, name, side=None):
    t = proj.shape[0]
    c = RET_C
    nc = t // c

    def body(blk_ref, cos_ref, sin_ref, lg_ref, o_ref, st_ref, state):
        @pl.when(pl.program_id(0) == 0)
        def _():
            state[...] = jnp.zeros_like(state)

        for h in range(RET_H):
            qr, kr, v, dmat, dq, dk, dchunk = _ret_chunk(blk_ref, cos_ref, sin_ref, lg_ref[h, :, 0:1], h)
            sp = state[h]
            st_ref[h] = sp.astype(BF16)
            scores = _dot_nt(qr.astype(BF16), kr.astype(BF16)) * dmat
            o = _dot(scores.astype(BF16), v) + _dot((qr * dq).astype(BF16), sp.astype(BF16))
            o_ref[:, h * RET_DV:(h + 1) * RET_DV] = o.astype(BF16)
            state[h] = sp * dchunk + _dot_tn((kr * dk).astype(BF16), v)

    return _call(
        body, name=name, grid=(nc,), side=side,
        in_specs=[pl.BlockSpec((c, RET_QKV), lambda n: (n, 0)), pl.BlockSpec((c, 128), lambda n: (n, 0)),
                  pl.BlockSpec((c, 128), lambda n: (n, 0)), pl.BlockSpec((RET_H, 1, 128), lambda n: (0, 0, 0))],
        out_specs=[pl.BlockSpec((c, RET_H * RET_DV), lambda n: (n, 0)),
                   pl.BlockSpec((RET_H, None, RET_DK, RET_DV), lambda n: (0, n, 0, 0))],
        out_shape=[S((t, RET_H * RET_DV), BF16), S((RET_H, nc, RET_DK, RET_DV), BF16)],
        scratch_shapes=[pltpu.VMEM((RET_H, RET_DK, RET_DV), F32)],
        operands=[proj, cos, sin, lgam])


def _ret_scan_bwd(proj, cos, sin, lgam, do, states, dproj, name, side=None):
    t = proj.shape[0]
    c = RET_C
    nc = t // c
    half = RET_DK // 2

    def body(blk_ref, cos_ref, sin_ref, lg_ref, do_ref, st_ref, dp_in, dp_ref, dstate):
        n = nc - 1 - pl.program_id(0)

        @pl.when(pl.program_id(0) == 0)
        def _():
            dstate[...] = jnp.zeros_like(dstate)

        cs, sn = cos_ref[...], sin_ref[...]
        rows = n * c + lax.broadcasted_iota(jnp.int32, (c, 1), 0)
        keep = rows >= PAD

        def unrot(d):
            d1, d2 = d[:, :half], d[:, half:]
            return jnp.concatenate([d1 * cs + d2 * sn, d2 * cs - d1 * sn], axis=1)

        for h in range(RET_H):
            qr, kr, v, dmat, dq, dk, dchunk = _ret_chunk(blk_ref, cos_ref, sin_ref, lg_ref[h, :, 0:1], h)
            qb, kb = qr.astype(BF16), kr.astype(BF16)
            dob = do_ref[:, h * RET_DV:(h + 1) * RET_DV]
            sp = st_ref[h]
            ds = dstate[h]
            dsb = ds.astype(BF16)
            p = (_dot_nt(qb, kb) * dmat).astype(BF16)
            dvv = _dot_tn(p, dob) + _dot((kr * dk).astype(BF16), dsb)
            dp = (_dot_nt(dob, v) * dmat).astype(BF16)
            dqr = _dot(dp, kb) + _dot_nt(dob, sp) * dq
            dkr = (_dot_tn(dp, qb) + _dot_nt(v, dsb) * dk) * (RET_DK ** -0.5)
            dstate[h] = ds * dchunk + _dot_tn((qr * dq).astype(BF16), dob)
            oq, ok, ov = h * RET_DK, RET_H * RET_DK + h * RET_DK, 2 * RET_H * RET_DK + h * RET_DV
            dp_ref[:, oq:oq + RET_DK] = jnp.where(keep, unrot(dqr), 0.0).astype(BF16)
            dp_ref[:, ok:ok + RET_DK] = jnp.where(keep, unrot(dkr), 0.0).astype(BF16)
            dp_ref[:, ov:ov + RET_DV] = jnp.where(keep, dvv, 0.0).astype(BF16)

    return _call(
        body, name=name, grid=(nc,), side=side, aliases={6: 0},
        in_specs=[pl.BlockSpec((c, RET_QKV), lambda n: (nc - 1 - n, 0)), pl.BlockSpec((c, 128), lambda n: (nc - 1 - n, 0)),
                  pl.BlockSpec((c, 128), lambda n: (nc - 1 - n, 0)), pl.BlockSpec((RET_H, 1, 128), lambda n: (0, 0, 0)),
                  pl.BlockSpec((c, RET_H * RET_DV), lambda n: (nc - 1 - n, 0)),
                  pl.BlockSpec((RET_H, None, RET_DK, RET_DV), lambda n: (0, nc - 1 - n, 0, 0)), ANY],
        out_specs=[pl.BlockSpec((c, RET_QKV), lambda n: (nc - 1 - n, 0))],
        out_shape=[S((t, dproj.shape[1]), BF16)],
        scratch_shapes=[pltpu.VMEM((RET_H, RET_DK, RET_DV), F32)],
        operands=[proj, cos, sin, lgam, do, states, dproj])


def _split3(x):
    hi = x.astype(BF16)
    r1 = x - hi.astype(F32)
    mid = r1.astype(BF16)
    lo = (r1 - mid.astype(F32)).astype(BF16)
    return hi, mid, lo


def _gla_chunk(blk_ref, z_ref, wg_ref, bg_ref, n, h, b_ref=None):
    c = CHUNK
    oq, ok, ov = h * GLA_DK, GLA_H * GLA_DK + h * GLA_DK, 2 * GLA_H * GLA_DK + h * GLA_DV
    q = blk_ref[:, oq:oq + GLA_DK].astype(F32) * (GLA_DK ** -0.5)
    k = blk_ref[:, ok:ok + GLA_DK].astype(F32)
    v = blk_ref[:, ov:ov + GLA_DV]
    hs = slice(h * GLA_DK, (h + 1) * GLA_DK)
    u = _dot(z_ref[...], wg_ref[:, hs]) + bg_ref[:, hs]
    rows = n * c + lax.broadcasted_iota(jnp.int32, (c, 1), 0)
    keep = rows >= PAD
    if b_ref is not None:
        return q, k, v, u, b_ref[:, hs], keep
    la = (jnp.minimum(u, 0.0) - jnp.log(1.0 + jnp.exp(-jnp.abs(u)))) * (1.0 / GLA_TAU)
    la = jnp.where(keep, la, 0.0)
    ii = lax.broadcasted_iota(jnp.int32, (c, c), 0)
    jj = lax.broadcasted_iota(jnp.int32, (c, c), 1)
    tril = (ii >= jj).astype(BF16)
    hi, mid, lo = _split3(la)
    b = _dot(tril, hi) + _dot(tril, mid) + _dot(tril, lo)
    return q, k, v, u, b, keep


def _gla_intra(qs, ks, bs, a_ref):
    c = CHUNK
    nh = len(qs)
    col = lax.broadcasted_iota(jnp.int32, (1, c), 1)
    rowi = lax.broadcasted_iota(jnp.int32, (SUB, 1), 0)
    for blk in range(c // SUB):
        r = slice(SUB * blk, SUB * (blk + 1))
        arows = []
        for h in range(nh):
            q, k, b = qs[h], ks[h], bs[h]
            if blk > 0:
                bprev = b[SUB * blk - 1:SUB * blk]
                qe = q[r] * jnp.exp(b[r] - bprev)
                kt = k * jnp.exp(jnp.minimum(bprev - b, 0.0))
                arows.append(jnp.where(col < SUB * blk, _dot_nt(qe.astype(BF16), kt.astype(BF16)), 0.0))
            else:
                arows.append(jnp.zeros((SUB, c), F32))
        half = SUB // 2
        lo = slice(SUB * blk + half, SUB * (blk + 1))
        tops = [a[:half] for a in arows]
        bots = [a[half:] for a in arows]
        for j in range(SUB):
            for h in range(nh):
                bj, kj = bs[h][SUB * blk + j:SUB * blk + j + 1], ks[h][SUB * blk + j:SUB * blk + j + 1]
                if j < half:
                    a = jnp.sum(qs[h][r] * kj * jnp.exp(bs[h][r] - bj), axis=1, keepdims=True)
                    tops[h] = jnp.where(col == SUB * blk + j, a[:half], tops[h])
                    bots[h] = jnp.where(col == SUB * blk + j, a[half:], bots[h])
                else:
                    a = jnp.sum(qs[h][lo] * kj * jnp.exp(bs[h][lo] - bj), axis=1, keepdims=True)
                    bots[h] = jnp.where(col == SUB * blk + j, a, bots[h])
        for h in range(nh):
            arow = jnp.concatenate([tops[h], bots[h]], axis=0)
            a_ref[h, r, :] = jnp.where(col - SUB * blk <= rowi, arow, 0.0)


def _gla_scan_fwd(proj, wgp, bg, name, side=None):
    t = proj.shape[0]
    c = CHUNK
    nc = t // c
    heads = range(GLA_H)

    def body(blk_ref, z_ref, wg_ref, bg_ref, o_ref, st_ref, am_ref, bs_ref, state, a_ref):
        n = pl.program_id(0)

        @pl.when(n == 0)
        def _():
            state[...] = jnp.zeros_like(state)

        qs, ks, vs, us, bs, keeps = zip(*[_gla_chunk(blk_ref, z_ref, wg_ref, bg_ref, n, h) for h in heads])
        _gla_intra(qs, ks, bs, a_ref)
        for h in heads:
            q, k, v, b = qs[h], ks[h], vs[h], bs[h]
            sp = state[h]
            st_ref[h] = sp.astype(BF16)
            ab = a_ref[h].astype(BF16)
            am_ref[:, h * c:(h + 1) * c] = ab
            bs_ref[:, h * GLA_DK:(h + 1) * GLA_DK] = b
            o = _dot(ab, v) + _dot_nt((q * jnp.exp(b)).astype(BF16), sp.astype(BF16))
            o_ref[:, h * GLA_DV:(h + 1) * GLA_DV] = o.astype(BF16)
            bc = b[c - 1:c]
            state[h] = sp * jnp.exp(bc) + _dot_tn(v, (k * jnp.exp(bc - b)).astype(BF16))

    return _call(
        body, name=name, grid=(nc,), side=side,
        in_specs=[pl.BlockSpec((c, GLA_QKV), lambda n: (n, 0)), pl.BlockSpec((c, 128), lambda n: (n, GLA_ZBLK)),
                  pl.BlockSpec((128, GLA_H * GLA_DK), lambda n: (0, 0)), pl.BlockSpec((1, GLA_H * GLA_DK), lambda n: (0, 0))],
        out_specs=[pl.BlockSpec((c, GLA_H * GLA_DV), lambda n: (n, 0)),
                   pl.BlockSpec((GLA_H, None, GLA_DV, GLA_DK), lambda n: (0, n, 0, 0)),
                   pl.BlockSpec((c, GLA_H * c), lambda n: (n, 0)),
                   pl.BlockSpec((c, GLA_H * GLA_DK), lambda n: (n, 0))],
        out_shape=[S((t, GLA_H * GLA_DV), BF16), S((GLA_H, nc, GLA_DV, GLA_DK), BF16), S((t, GLA_H * c), BF16),
                   S((t, GLA_H * GLA_DK), F32)],
        scratch_shapes=[pltpu.VMEM((GLA_H, GLA_DV, GLA_DK), F32), pltpu.VMEM((GLA_H, c, c), F32)],
        operands=[proj, proj, wgp, bg])


def _gla_scan_bwd(proj, wgp, bg, do, states, amat, bcum, dproj, name):
    t = proj.shape[0]
    c = CHUNK
    nc = t // c
    heads = range(GLA_H)

    def body(blk_ref, z_ref, wg_ref, bg_ref, do_ref, st_ref, am_ref, bs_ref, dp_in, dp_ref, du_ref, dstate, dq_ref, dkd_ref):
        n = nc - 1 - pl.program_id(0)

        @pl.when(pl.program_id(0) == 0)
        def _():
            dstate[...] = jnp.zeros_like(dstate)

        qs, ks, vs, us, bs, keeps = zip(*[_gla_chunk(blk_ref, z_ref, wg_ref, bg_ref, n, h, bs_ref) for h in heads])
        ii = lax.broadcasted_iota(jnp.int32, (c, c), 0)
        jj = lax.broadcasted_iota(jnp.int32, (c, c), 1)
        col = lax.broadcasted_iota(jnp.int32, (1, c), 1)
        rowi = lax.broadcasted_iota(jnp.int32, (SUB, 1), 0)
        rowc = lax.broadcasted_iota(jnp.int32, (c, 1), 0)
        das, dvs, dq_inters, dk_states, extras, dks = [], [], [], [], [], []
        for h in heads:
            q, k, v, b = qs[h], ks[h], vs[h], bs[h]
            ab = am_ref[:, h * c:(h + 1) * c]
            dob = do_ref[:, h * GLA_DV:(h + 1) * GLA_DV]
            sp = st_ref[h]
            ds = dstate[h]
            dsb = ds.astype(BF16)
            bc = b[c - 1:c]
            eb = jnp.exp(b)
            ebc = jnp.exp(bc - b)
            ec = jnp.exp(bc)
            qb = (q * eb).astype(BF16)
            kb = (k * ebc).astype(BF16)
            dvs.append(_dot_tn(ab, dob) + _dot_nt(kb, dsb))
            das.append(jnp.where(ii >= jj, _dot_nt(dob, v), 0.0))
            dq_inters.append(_dot(dob, sp) * eb)
            dk_state = _dot(v, dsb) * ebc
            dk_states.append(dk_state)
            extras.append(jnp.sum(k * dk_state, axis=0, keepdims=True)
                          + ec * jnp.sum(sp.astype(F32) * ds, axis=0, keepdims=True))
            dstate[h] = ds * ec + _dot_tn(dob, qb)
            dks.append(jnp.zeros((c, GLA_DK), F32))

        for blk in range(c // SUB):
            r = slice(SUB * blk, SUB * (blk + 1))
            dq_is, dkds = [], []
            for h in heads:
                q, k, b = qs[h], ks[h], bs[h]
                if blk > 0:
                    bprev = b[SUB * blk - 1:SUB * blk]
                    e_i = jnp.exp(b[r] - bprev)
                    ek = jnp.exp(jnp.minimum(bprev - b, 0.0))
                    daoff = jnp.where(col < SUB * blk, das[h][r], 0.0).astype(BF16)
                    dq_is.append(_dot(daoff, (k * ek).astype(BF16)) * e_i)
                    dks[h] = dks[h] + _dot_tn(daoff, (q[r] * e_i).astype(BF16)) * ek
                else:
                    dq_is.append(jnp.zeros((SUB, GLA_DK), F32))
                dkds.append(jnp.zeros((SUB, GLA_DK), F32))
            half = SUB // 2
            lo = slice(SUB * blk + half, SUB * (blk + 1))
            row8 = rowi[:half]
            dq_tops = [a[:half] for a in dq_is]
            dq_bots = [a[half:] for a in dq_is]
            for j in range(SUB):
                for h in heads:
                    bj, kj = bs[h][SUB * blk + j:SUB * blk + j + 1], ks[h][SUB * blk + j:SUB * blk + j + 1]
                    if j < half:
                        e = jnp.where(rowi >= j, jnp.exp(bs[h][r] - bj), 0.0)
                        dacol = jnp.sum(jnp.where(col == SUB * blk + j, das[h][r], 0.0), axis=1, keepdims=True)
                        tt = dacol * e
                        dq_tops[h] = dq_tops[h] + tt[:half] * kj
                        dq_bots[h] = dq_bots[h] + tt[half:] * kj
                        dkrow = jnp.sum(tt * qs[h][r], axis=0, keepdims=True)
                    else:
                        e = jnp.where(row8 + half >= j, jnp.exp(bs[h][lo] - bj), 0.0)
                        dacol = jnp.sum(jnp.where(col == SUB * blk + j, das[h][lo], 0.0), axis=1, keepdims=True)
                        tt = dacol * e
                        dq_bots[h] = dq_bots[h] + tt * kj
                        dkrow = jnp.sum(tt * qs[h][lo], axis=0, keepdims=True)
                    dkds[h] = jnp.where(rowi == j, dkrow, dkds[h])
            for h in heads:
                dq_ref[h, r, :] = jnp.concatenate([dq_tops[h], dq_bots[h]], axis=0)
                dkd_ref[h, r, :] = dkds[h]

        for h in heads:
            q, k, b, u, keep = qs[h], ks[h], bs[h], us[h], keeps[h]
            dq = dq_ref[h] + dq_inters[h]
            dk = dks[h] + dkd_ref[h] + dk_states[h]
            db = q * dq - k * dk + jnp.where(rowc == c - 1, extras[h], 0.0)
            triu = (ii <= jj).astype(BF16)
            hi, mid, lo = _split3(db)
            dla = _dot(triu, hi) + _dot(triu, mid) + _dot(triu, lo)
            du = jnp.where(keep, dla * (1.0 / GLA_TAU) / (1.0 + jnp.exp(u)), 0.0)
            du_ref[:, h * GLA_DK:(h + 1) * GLA_DK] = du.astype(BF16)
            oq, ok, ov = h * GLA_DK, GLA_H * GLA_DK + h * GLA_DK, 2 * GLA_H * GLA_DK + h * GLA_DV
            dp_ref[:, oq:oq + GLA_DK] = jnp.where(keep, dq * (GLA_DK ** -0.5), 0.0).astype(BF16)
            dp_ref[:, ok:ok + GLA_DK] = jnp.where(keep, dk, 0.0).astype(BF16)
            dp_ref[:, ov:ov + GLA_DV] = jnp.where(keep, dvs[h], 0.0).astype(BF16)

    nproj = dproj.shape[1]
    return pl.pallas_call(
        body, name=name, grid=(nc,),
        in_specs=[pl.BlockSpec((c, GLA_QKV), lambda n: (nc - 1 - n, 0)), pl.BlockSpec((c, 128), lambda n: (nc - 1 - n, GLA_ZBLK)),
                  pl.BlockSpec((128, GLA_H * GLA_DK), lambda n: (0, 0)), pl.BlockSpec((1, GLA_H * GLA_DK), lambda n: (0, 0)),
                  pl.BlockSpec((c, GLA_H * GLA_DV), lambda n: (nc - 1 - n, 0)),
                  pl.BlockSpec((GLA_H, None, GLA_DV, GLA_DK), lambda n: (0, nc - 1 - n, 0, 0)),
                  pl.BlockSpec((c, GLA_H * c), lambda n: (nc - 1 - n, 0)),
                  pl.BlockSpec((c, GLA_H * GLA_DK), lambda n: (nc - 1 - n, 0)), ANY],
        out_specs=[pl.BlockSpec((c, GLA_QKV), lambda n: (nc - 1 - n, 0)),
                   pl.BlockSpec((c, GLA_H * GLA_DK), lambda n: (nc - 1 - n, 0))],
        out_shape=[S((t, nproj), BF16), S((t, GLA_H * GLA_DK), BF16)],
        input_output_aliases={8: 0},
        scratch_shapes=[pltpu.VMEM((GLA_H, GLA_DV, GLA_DK), F32),
                        pltpu.VMEM((GLA_H, c, GLA_DK), F32), pltpu.VMEM((GLA_H, c, GLA_DK), F32)],
        compiler_params=_cp(dimension_semantics=("arbitrary",)),
    )(proj, proj, wgp, bg, do, states, amat, bcum, dproj)


def _gla_gate_bwd(du, proj, wgp, dproj, name):
    t = du.shape[0]
    tm = _row_tile(t, 704)
    w = GLA_H * GLA_DK

    def body(du_ref, z_ref, wg_ref, dp_in, dp_ref, dwg_ref, dbg_ref):
        @pl.when(pl.program_id(0) == 0)
        def _():
            dwg_ref[...] = jnp.zeros_like(dwg_ref)
            dbg_ref[...] = jnp.zeros_like(dbg_ref)

        d = du_ref[...]
        dp_ref[...] = _dot_nt(d, wg_ref[...]).astype(BF16)
        dwg_ref[...] += _dot_tn(z_ref[...], d)
        dbg_ref[0:1, :] += jnp.sum(d.astype(F32), axis=0, keepdims=True)

    return pl.pallas_call(
        body, name=name, grid=(t // tm,),
        in_specs=[pl.BlockSpec((tm, w), lambda i: (i, 0)), pl.BlockSpec((tm, 128), lambda i: (i, GLA_ZBLK)),
                  pl.BlockSpec((128, w), lambda i: (0, 0)), ANY],
        out_specs=[pl.BlockSpec((tm, 128), lambda i: (i, GLA_ZBLK)), pl.BlockSpec((128, w), lambda i: (0, 0)),
                   pl.BlockSpec((8, w), lambda i: (0, 0))],
        out_shape=[S(dproj.shape, BF16), S((128, w), F32), S((8, w), F32)],
        input_output_aliases={3: 0},
        compiler_params=_cp(dimension_semantics=("arbitrary",)),
    )(du, proj, wgp, dproj)


def _final_loss(h, gain, target, name):
    t = h.shape[0]
    tm = _row_tile(t, 704)

    def body(h_ref, g_ref, t_ref, dh_ref, dgain_ref, loss_ref):
        i = pl.program_id(0)

        @pl.when(i == 0)
        def _():
            dgain_ref[...] = jnp.zeros_like(dgain_ref)
            loss_ref[...] = jnp.zeros_like(loss_ref)

        x = h_ref[...]
        gain = g_ref[...]
        r = lax.rsqrt(jnp.mean(x * x, axis=-1, keepdims=True) + EPS)
        xh = x * r
        rows = i * tm + lax.broadcasted_iota(jnp.int32, (tm, 1), 0)
        e = jnp.where(rows >= CHUNK, xh * gain - t_ref[...], 0.0)
        loss_ref[...] += 0.5 * jnp.sum(jnp.mean(e * e, axis=-1, keepdims=True), axis=0, keepdims=True)
        dy = e * (1.0 / D)
        dgain_ref[0:1, :] += jnp.sum(dy * xh, axis=0, keepdims=True)
        dxh = dy * gain
        dh_ref[...] = r * (dxh - xh * jnp.mean(dxh * xh, axis=-1, keepdims=True))

    row = pl.BlockSpec((tm, D), lambda i: (i, 0))
    return pl.pallas_call(
        body, name=name, grid=(t // tm,),
        in_specs=[row, pl.BlockSpec((1, D), lambda i: (0, 0)), row],
        out_specs=[row, pl.BlockSpec((8, D), lambda i: (0, 0)), pl.BlockSpec((8, 128), lambda i: (0, 0))],
        out_shape=[S((t, D), F32), S((8, D), F32), S((8, 128), F32)],
        compiler_params=_cp(dimension_semantics=("arbitrary",)),
    )(h, gain, target)


def _adam_math(w, g, m, v):
    m2 = ADAM_B1 * m + (1.0 - ADAM_B1) * g
    v2 = ADAM_B2 * v + (1.0 - ADAM_B2) * (g * g)
    m_hat = m2 / (1.0 - ADAM_B1 ** ADAM_STEP)
    v_hat = v2 / (1.0 - ADAM_B2 ** ADAM_STEP)
    delta = -ADAM_LR * (m_hat / (jnp.sqrt(v_hat) + ADAM_EPS) + ADAM_WD * w)
    return delta, m2, v2


def _adamw_reduce(recvs, w, m, v, name):
    nl, r, wd = w.shape
    tr = _row_tile(r, 256) if r % 16 == 0 else r
    nr = r // tr

    def body(*refs):
        rv_refs = refs[:nl]
        w_ref, m_ref, v_ref, g_ref, d_ref, m2_ref, v2_ref = refs[nl:]
        layer = pl.program_id(0)

        def total(rv_ref):
            g = rv_ref[0].astype(F32)
            for s in range(1, N_DEV):
                g = g + rv_ref[s].astype(F32)
            return g

        g = total(rv_refs[0])
        for k in range(1, nl):
            g = jnp.where(layer == k, total(rv_refs[k]), g)
        g_ref[...] = g
        d_ref[...], m2_ref[...], v2_ref[...] = _adam_math(w_ref[...], g, m_ref[...], v_ref[...])

    def rv_spec(k):
        return pl.BlockSpec((N_DEV, tr, wd), lambda l, i: (0, jnp.where(l == k, i, jnp.where(l < k, 0, nr - 1)), 0))

    row = pl.BlockSpec((None, tr, wd), lambda l, i: (l, i, 0))
    return pl.pallas_call(
        body, name=name, grid=(nl, nr),
        in_specs=[rv_spec(k) for k in range(nl)] + [row, row, row],
        out_specs=[row] * 4, out_shape=[S((nl, r, wd), F32)] * 4,
        compiler_params=_cp(dimension_semantics=("arbitrary", "arbitrary")),
    )(*recvs, w, m, v)


def _small_reduce(parts, name):
    _, r, wd = parts.shape

    def body(p_ref, o_ref):
        g = p_ref[0]
        for s in range(1, N_DEV):
            g = g + p_ref[s]
        o_ref[...] = g

    return pl.pallas_call(body, name=name, out_shape=S((r, wd), F32), compiler_params=_cp())(parts)


def _adamw_small(w, g, m, v, name):
    def body(w_ref, g_ref, m_ref, v_ref, d_ref, m2_ref, v2_ref):
        d_ref[...], m2_ref[...], v2_ref[...] = _adam_math(w_ref[...], g_ref[...], m_ref[...], v_ref[...])

    return pl.pallas_call(body, name=name, out_shape=[S(w.shape, F32)] * 3, compiler_params=_cp())(w, g, m, v)


def _unshard_cols(g):
    return jnp.transpose(g, (1, 0, 2)).reshape(g.shape[1], N_DEV * g.shape[2])


def _my_cols(full, width):
    me = 4 * lax.axis_index("x") + 2 * lax.axis_index("y") + lax.axis_index("c")
    return lax.dynamic_slice_in_dim(full, me * width, width, axis=1)


def kernel(x, meta_tokens, norm_ffn1, ffn1_w_in, ffn1_w_out, norm_mix, norm_ffn2, ffn2_w_in, ffn2_w_out, ret_w_in, ret_head_norm, ret_w_out, gla_w_in, gla_w_gate, gla_b_gate, gla_head_norm, gla_w_out, final_norm, loss_target, m_meta_tokens, m_norm_ffn1, m_ffn1_w_in, m_ffn1_w_out, m_norm_mix, m_norm_ffn2, m_ffn2_w_in, m_ffn2_w_out, m_ret_w_in, m_ret_head_norm, m_ret_w_out, m_gla_w_in, m_gla_w_gate, m_gla_b_gate, m_gla_head_norm, m_gla_w_out, m_final_norm, v_meta_tokens, v_norm_ffn1, v_ffn1_w_in, v_ffn1_w_out, v_norm_mix, v_norm_ffn2, v_ffn2_w_in, v_ffn2_w_out, v_ret_w_in, v_ret_head_norm, v_ret_w_out, v_gla_w_in, v_gla_w_gate, v_gla_b_gate, v_gla_head_norm, v_gla_w_out, v_final_norm):
    seq = x.shape[1]
    t = seq + CHUNK
    xs = x[0]
    target = loss_target[0]

    def ffn_w(f):
        w_in, w_out = (ffn1_w_in, ffn1_w_out) if f < 2 else (ffn2_w_in, ffn2_w_out)
        return [w_in[f % 2].astype(BF16), w_out[f % 2].astype(BF16)]

    small = jnp.concatenate([meta_tokens.reshape(-1), ret_head_norm.reshape(-1), gla_w_gate.reshape(-1),
                             gla_b_gate.reshape(-1), gla_head_norm.reshape(-1)])
    n_small = small.shape[0]
    small = jnp.pad(small, (0, 32 * 128 - n_small)).reshape(32, 128)
    sg, win0, wout0 = _run_side(_Gather([small] + ffn_w(0)), "ag_first")
    sg = sg.reshape(N_DEV, 32 * 128)

    def small_cols(off, rows, width):
        return jnp.transpose(sg[:, off:off + rows * width].reshape(N_DEV, rows, width), (1, 0, 2)).reshape(rows, N_DEV * width)

    off = 0
    meta_full = small_cols(off, N_META, D // N_DEV); off += N_META * (D // N_DEV)
    ret_hn = small_cols(off, RET_H, RET_DV // N_DEV).reshape(1, RET_H * RET_DV); off += RET_H * RET_DV // N_DEV
    wgate = small_cols(off, GLA_RANK, GLA_H * GLA_DK // N_DEV); off += GLA_RANK * GLA_H * GLA_DK // N_DEV
    bgate = small_cols(off, 1, GLA_H * GLA_DK // N_DEV); off += GLA_H * GLA_DK // N_DEV
    gla_hn = small_cols(off, GLA_H, GLA_DV // N_DEV).reshape(1, GLA_H * GLA_DV)
    wgp = jnp.pad(wgate, ((0, 128 - GLA_RANK), (0, 0))).astype(BF16)

    cos, sin = _rope_tables(t)
    lgam = _ret_consts()

    h0 = jnp.concatenate([jnp.zeros((PAD, D), F32), meta_full, xs], axis=0)
    g1 = [norm_ffn1[i:i + 1] for i in range(2)]
    gm = [norm_mix[i:i + 1] for i in range(2)]
    g2 = [norm_ffn2[i:i + 1] for i in range(2)]

    (h1, xn_a0, pg_a0, pu_a0), (ret_win_g, ret_wout_g) = _ffn_fwd(
        h0, g1[0], win0, wout0, "ffn1_l0_fwd", side=_Gather([ret_w_in[0].astype(BF16), ret_w_out[0].astype(BF16)]))
    ret_win = ret_win_g
    ret_wout = ret_wout_g.reshape(RET_H * RET_DV, D)
    (rproj, rhn), (win2,) = _norm_mm(h1, gm[0], ret_win, 4 * ret_win.shape[2], "ret_proj_fwd", side=_Gather(ffn_w(2)[:1]))
    (ro, rstates), (wout2,) = _ret_scan_fwd(rproj, cos, sin, lgam, "ret_scan_fwd", side=_Gather(ffn_w(2)[1:]))
    (h2, rog), _ = _post_fwd(ro, rproj, ret_hn, ret_wout, h1, RET_H, RET_DV, "ret_post_fwd")
    (h3, xn_b0, pg_b0, pu_b0), (win1, wout1) = _ffn_fwd(h2, g2[0], win2, wout2, "ffn2_l0_fwd", side=_Gather(ffn_w(1)))
    (h4, xn_a1, pg_a1, pu_a1), (gla_win_g, gla_wout_g) = _ffn_fwd(
        h3, g1[1], win1, wout1, "ffn1_l1_fwd", side=_Gather([gla_w_in[0].astype(BF16), gla_w_out[0].astype(BF16)]))
    gla_win = _unshard_cols(gla_win_g)
    gla_win = jnp.pad(gla_win, ((0, 0), (0, GLA_N - gla_win.shape[1])))
    gla_wout = gla_wout_g.reshape(GLA_H * GLA_DV, D)
    (gproj, ghn), _ = _norm_mm(h4, gm[1], gla_win, GLA_N, "gla_proj_fwd")
    (go, gstates, gamat, gbcum), (win3, wout3) = _gla_scan_fwd(gproj, wgp, bgate, "gla_scan_fwd", side=_Gather(ffn_w(3)))
    (h5, gog), _ = _post_fwd(go, gproj, gla_hn, gla_wout, h4, GLA_H, GLA_DV, "gla_post_fwd")
    (h6, xn_b1, pg_b1, pu_b1), _ = _ffn_fwd(h5, g2[1], win3, wout3, "ffn2_l1_fwd")

    dh, dfinal, loss_blk = _final_loss(h6, final_norm.reshape(1, D), jnp.pad(target, ((CHUNK, 0), (0, 0))), "final_loss")
    loss = lax.psum(loss_blk[0, 0], ("x", "y", "c"))

    def ffn_back(dh, h_in, xn, gain, pg, pu, win, wout, tag, side=None, dw_side=None):
        (dh_in, dob, dpg, dpu, act, dgain), got = _ffn_bwd(dh, h_in, gain, pg, pu, win, wout, tag + "_bwd", side=side)
        dwout = _mm_tn(act, dob[None], D, tag + "_dw_out").reshape(N_DEV, FF_SHARD // 2, D)
        if dw_side == "own_dw_out":
            dw_side = _Exchange([dwout])
        (dwin,), dw_got = _ffn_dw_in(xn, dpg, dpu, tag + "_dw_in", side=dw_side)
        return dh_in, [dwin, dwout], dgain[0], got, dw_got

    dh, dw_b1, dg2_1, _, _ = ffn_back(dh, h5, xn_b1, g2[1], pg_b1, pu_b1, win3, wout3, "ffn2_l1")

    (gdo, gdproj, gdhb, dghn), _ = _post_bwd(dh, go, gproj, gla_hn, gla_wout, GLA_H, GLA_DV, GLA_N, "gla_post_bwd")
    d_gla_wout = _mm_tn(gog[None], gdhb[None], D, "gla_dw_out").reshape(N_DEV, GLA_H * GLA_DV // N_DEV, D)
    gdproj, gdu = _gla_scan_bwd(gproj, wgp, bgate, gdo, gstates, gamat, gbcum, gdproj, "gla_scan_bwd")
    gdproj, dwg, dbg = _gla_gate_bwd(gdu, gproj, wgp, gdproj, "gla_gate_bwd")
    d_gla_win = _mm_tn(gdproj[None], ghn[None], D, "gla_dw_in", tm=640)[0]
    (dh, dgm_1), _ = _proj_bwd(gdproj, gla_win, dh, h4, gm[1], GLA_N, "gla_proj_bwd")
    n_gla_in = 2 * GLA_H * GLA_DK + 2 * GLA_H * GLA_DV + GLA_RANK
    d_gla_win = d_gla_win[:n_gla_in].reshape(N_DEV, n_gla_in // N_DEV, D)

    dh, dw_a1, dg1_1, rv_b1, rv_gla = ffn_back(dh, h3, xn_a1, g1[1], pg_a1, pu_a1, win1, wout1, "ffn1_l1",
                                               side=_Exchange(dw_b1), dw_side=_Exchange([d_gla_win, d_gla_wout]))
    dh, dw_b0, dg2_0, rv_a1, _ = ffn_back(dh, h2, xn_b0, g2[0], pg_b0, pu_b0, win2, wout2, "ffn2_l0", side=_Exchange(dw_a1))

    (rdo, rdproj, rdhb, drhn), rv_b0_out = _post_bwd(dh, ro, rproj, ret_hn, ret_wout, RET_H, RET_DV, 6 * D, "ret_post_bwd",
                                                     side=_Exchange(dw_b0[1:]))
    d_ret_wout = _mm_tn(rog[None], rdhb[None], D, "ret_dw_out", rows=DW_ROWS // 2).reshape(N_DEV, RET_H * RET_DV // N_DEV, D)
    (rdproj,), rv_b0_in = _ret_scan_bwd(rproj, cos, sin, lgam, rdo, rstates, rdproj, "ret_scan_bwd", side=_Exchange(dw_b0[:1]))
    rv_b0 = rv_b0_in + rv_b0_out
    d_ret_win = _mm_tn(rhn[None], rdproj[None], ret_win.shape[2], "ret_dw_in", shard_out=True)
    (dh, dgm_0), rv_ret_out = _proj_bwd(rdproj, ret_win, dh, h1, gm[0], 4 * ret_win.shape[2], "ret_proj_bwd", side=_Exchange([d_ret_wout]))

    dh, dw_a0, dg1_0, rv_ret_in, rv_a0_out = ffn_back(dh, h0, xn_a0, g1[0], pg_a0, pu_a0, win0, wout0, "ffn1_l0",
                                                      side=_Exchange([d_ret_win]), dw_side="own_dw_out")
    rv_ret = rv_ret_in + rv_ret_out
    rv_a0 = _run_side(_Exchange(dw_a0[:1]), "xchg_last") + rv_a0_out
    grad_x = dh[CHUNK:][None]

    def adam_t(recvs, w, m, v, tag):
        outs = _adamw_reduce(recvs, *(jnp.swapaxes(a, 1, 2) for a in (w, m, v)), tag)
        return [jnp.swapaxes(o, 1, 2) for o in outs]

    u_ffn1_in = adam_t([rv_a0[0], rv_a1[0]], ffn1_w_in, m_ffn1_w_in, v_ffn1_w_in, "adam_ffn1_w_in")
    u_ffn2_in = adam_t([rv_b0[0], rv_b1[0]], ffn2_w_in, m_ffn2_w_in, v_ffn2_w_in, "adam_ffn2_w_in")
    u_ffn1_out = _adamw_reduce([rv_a0[1], rv_a1[1]], ffn1_w_out, m_ffn1_w_out, v_ffn1_w_out, "adam_ffn1_w_out")
    u_ffn2_out = _adamw_reduce([rv_b0[1], rv_b1[1]], ffn2_w_out, m_ffn2_w_out, v_ffn2_w_out, "adam_ffn2_w_out")
    u_ret_in = _adamw_reduce([rv_ret[0]], ret_w_in, m_ret_w_in, v_ret_w_in, "adam_ret_w_in")
    u_ret_out = _adamw_reduce([rv_ret[1]], ret_w_out, m_ret_w_out, v_ret_w_out, "adam_ret_w_out")
    u_gla_in = adam_t([rv_gla[0]], gla_w_in, m_gla_w_in, v_gla_w_in, "adam_gla_w_in")
    u_gla_out = _adamw_reduce([rv_gla[1]], gla_w_out, m_gla_w_out, v_gla_w_out, "adam_gla_w_out")

    dmeta = dh[PAD:CHUNK]
    parts = jnp.concatenate([
        dg1_0, dg1_1, dgm_0[0], dgm_1[0], dg2_0, dg2_1, dfinal[0], dmeta.reshape(-1), drhn[0], dwg[:GLA_RANK].reshape(-1),
        dbg[0], dghn[0]])
    n_parts = parts.shape[0]
    rows = -(-n_parts // D)
    rows = -(-rows // 8) * 8
    parts = jnp.pad(parts, (0, rows * D - n_parts)).reshape(rows, D)
    tot = _small_reduce(_run_side(_Gather([parts]), "ag_small_grads")[0], "small_grad_sum").reshape(-1)

    off = 0
    def take(nel):
        nonlocal off
        out = tot[off:off + nel]
        off += nel
        return out

    gr_norm_ffn1 = take(2 * D).reshape(2, D)
    gr_norm_mix = take(2 * D).reshape(2, D)
    gr_norm_ffn2 = take(2 * D).reshape(2, D)
    gr_final = take(D)
    gr_meta = _my_cols(take(N_META * D).reshape(N_META, D), D // N_DEV)
    gr_ret_hn = _my_cols(take(RET_H * RET_DV).reshape(RET_H, RET_DV), RET_DV // N_DEV)[None]
    gr_wgate = _my_cols(take(GLA_RANK * GLA_H * GLA_DK).reshape(GLA_RANK, GLA_H * GLA_DK), GLA_H * GLA_DK // N_DEV)[None]
    gr_bgate = _my_cols(take(GLA_H * GLA_DK).reshape(1, GLA_H * GLA_DK), GLA_H * GLA_DK // N_DEV)
    gr_gla_hn = _my_cols(take(GLA_H * GLA_DV).reshape(GLA_H, GLA_DV), GLA_DV // N_DEV)[None]

    small_w = [meta_tokens, norm_ffn1, norm_mix, norm_ffn2, ret_head_norm, gla_w_gate, gla_b_gate, gla_head_norm, final_norm]
    small_g = [gr_meta, gr_norm_ffn1, gr_norm_mix, gr_norm_ffn2, gr_ret_hn, gr_wgate, gr_bgate, gr_gla_hn, gr_final]
    small_m = [m_meta_tokens, m_norm_ffn1, m_norm_mix, m_norm_ffn2, m_ret_head_norm, m_gla_w_gate, m_gla_b_gate, m_gla_head_norm, m_final_norm]
    small_v = [v_meta_tokens, v_norm_ffn1, v_norm_mix, v_norm_ffn2, v_ret_head_norm, v_gla_w_gate, v_gla_b_gate, v_gla_head_norm, v_final_norm]

    def pack(arrs):
        flat = jnp.concatenate([a.reshape(-1) for a in arrs])
        n = flat.shape[0]
        r = -(-n // 128)
        r = -(-r // 8) * 8
        return jnp.pad(flat, (0, r * 128 - n), constant_values=1.0).reshape(r, 128)

    sd, sm, sv = _adamw_small(pack(small_w), pack(small_g), pack(small_m), pack(small_v), "adam_small")

    def unpack(buf):
        flat = buf.reshape(-1)
        outs, o = [], 0
        for a in small_w:
            outs.append(flat[o:o + a.size].reshape(a.shape))
            o += a.size
        return outs

    us_d, us_m, us_v = unpack(sd), unpack(sm), unpack(sv)

    def ordered(k, smalls):
        return (smalls[0], smalls[1], u_ffn1_in[k], u_ffn1_out[k], smalls[2], smalls[3], u_ffn2_in[k], u_ffn2_out[k],
                u_ret_in[k], smalls[4], u_ret_out[k], u_gla_in[k], smalls[5], smalls[6], smalls[7], u_gla_out[k], smalls[8])

    return (loss, grad_x, *ordered(0, small_g), *ordered(1, us_d), *ordered(2, us_m), *ordered(3, us_v))
```

```python
import functools
import math

import numpy as np
import jax
import jax.numpy as jnp
from jax import lax
from jax.experimental import pallas as pl
from jax.experimental.pallas import tpu as pltpu

F32 = jnp.float32
BF16 = jnp.bfloat16
S = jax.ShapeDtypeStruct
ANY = pl.BlockSpec(memory_space=pl.ANY)
MESH = pl.DeviceIdType.MESH

D = 1024
N_META = 16
CHUNK = 64
PAD = CHUNK - N_META
EPS = 1e-6
N_DEV = 8
FF_SHARD = 704
N_FF_CHUNK = 4
RET_H, RET_DK, RET_DV = 4, 256, 512
RET_QKV = RET_H * (2 * RET_DK + RET_DV)
RET_C = 192
GLA_H, GLA_DK, GLA_DV, GLA_RANK, GLA_TAU = 4, 128, 256, 16, 16.0
GLA_QKV = GLA_H * (2 * GLA_DK + GLA_DV)
GLA_N = 3200
GLA_ZBLK = 3072 // 128
SUB = 16
ROPE_BASE = 10000.0
ADAM_LR, ADAM_B1, ADAM_B2, ADAM_EPS, ADAM_WD, ADAM_STEP = 0.001, 0.9, 0.999, 1e-08, 0.01, 10
VMEM_LIMIT = 58 * 1024 * 1024
DW_ROWS = 2752


def _cp(**kw):
    return pltpu.CompilerParams(vmem_limit_bytes=VMEM_LIMIT, **kw)


def _row_tile(t, cap):
    best = 16
    for d in range(16, cap + 1, 16):
        if t % d == 0:
            best = d
    return best


def _sub_rows(tm, parts=2):
    units = tm // 16
    cuts = [16 * (units * p // parts) for p in range(parts + 1)]
    return [slice(a, b) for a, b in zip(cuts[:-1], cuts[1:]) if b > a]


def _dot(a, b):
    return jnp.dot(a, b, preferred_element_type=F32)


def _dot_nt(a, b):
    return lax.dot_general(a, b, (((1,), (1,)), ((), ())), preferred_element_type=F32)


def _dot_tn(a, b):
    return lax.dot_general(a, b, (((0,), (0,)), ((), ())), preferred_element_type=F32)


def _sigmoid(x):
    return pl.reciprocal(1.0 + jnp.exp(-x), approx=True)


def _rms_bwd(dxn, x, gain):
    r = lax.rsqrt(jnp.mean(x * x, axis=-1, keepdims=True) + EPS)
    xh = x * r
    dxh = dxn * gain
    dx = r * (dxh - xh * jnp.mean(dxh * xh, axis=-1, keepdims=True))
    return dx, jnp.sum(dxn * xh, axis=0, keepdims=True)


def _xyc():
    return lax.axis_index("x"), lax.axis_index("y"), lax.axis_index("c")


class _Gather:
    def __init__(self, xs):
        self.xs = list(xs)
        self.n = len(self.xs)

    def out_shape(self):
        return [S((N_DEV,) + a.shape, a.dtype) for a in self.xs]

    def scratch(self):
        return [pltpu.SemaphoreType.DMA((self.n, 7)), pltpu.SemaphoreType.DMA((self.n, 7)), pltpu.SemaphoreType.DMA((self.n,))]

    def phases(self, x_refs, out_refs, send_sems, recv_sems, local_sems):
        x, y, c = _xyc()
        me, sibling = (x, y, c), (x, y, 1 - c)
        chips = [(1 - x, y), (x, 1 - y), (1 - x, 1 - y)]

        def copy(t, k, block, to, src=None):
            px, py, pc = block
            dst = out_refs[t].at[4 * px + 2 * py + pc]
            return pltpu.make_async_remote_copy(
                src_ref=dst if src is None else src, dst_ref=dst,
                send_sem=send_sems.at[t, k], recv_sem=recv_sems.at[t, k], device_id=to, device_id_type=MESH)

        def own(t):
            return pltpu.make_async_copy(x_refs[t], out_refs[t].at[4 * x + 2 * y + c], local_sems.at[t])

        def first(t):
            return [copy(t, 0, me, sibling, src=x_refs[t])] + [
                copy(t, 1 + j, me, (*chip, c), src=x_refs[t]) for j, chip in enumerate(chips)]

        def passed(t):
            return [copy(t, 4 + j, (*chip, c), sibling) for j, chip in enumerate(chips)]

        def start():
            for t in range(self.n):
                own(t).start()
                for cp in first(t):
                    cp.start()

        def mid():
            for t in range(self.n):
                fw = passed(t)
                for j, chip in enumerate(chips):
                    copy(t, 1 + j, (*chip, c), me).wait_recv()
                    fw[j].start()

        def finish():
            for t in range(self.n):
                copy(t, 0, sibling, me).wait_recv()
                for j, chip in enumerate(chips):
                    copy(t, 4 + j, (*chip, 1 - c), me).wait_recv()
                for cp in first(t) + passed(t):
                    cp.wait_send()
                own(t).wait()

        return start, mid, finish


class _Exchange:
    def __init__(self, xs):
        self.xs = list(xs)
        self.n = len(self.xs)

    def out_shape(self):
        return [S(a.shape, a.dtype) for a in self.xs]

    def scratch(self):
        return [pltpu.SemaphoreType.DMA((self.n, 7)), pltpu.SemaphoreType.DMA((self.n, 7)), pltpu.SemaphoreType.DMA((self.n,))]

    def phases(self, g_refs, r_refs, send_sems, recv_sems, local_sems):
        x, y, c = _xyc()
        me = 4 * x + 2 * y + c

        def own(t):
            return pltpu.make_async_copy(g_refs[t].at[me], r_refs[t].at[me], local_sems.at[t])

        def send(t, m):
            px, py, pc = x ^ (m >> 2), y ^ ((m >> 1) & 1), c ^ (m & 1)
            return pltpu.make_async_remote_copy(
                src_ref=g_refs[t].at[4 * px + 2 * py + pc], dst_ref=r_refs[t].at[me],
                send_sem=send_sems.at[t, m - 1], recv_sem=recv_sems.at[t, m - 1],
                device_id=(px, py, pc), device_id_type=MESH)

        def arrival(t, m):
            peer = 4 * (x ^ (m >> 2)) + 2 * (y ^ ((m >> 1) & 1)) + (c ^ (m & 1))
            return pltpu.make_async_remote_copy(
                src_ref=g_refs[t].at[peer], dst_ref=r_refs[t].at[peer],
                send_sem=send_sems.at[t, m - 1], recv_sem=recv_sems.at[t, m - 1],
                device_id=(x, y, c), device_id_type=MESH)

        def start():
            for t in range(self.n):
                own(t).start()
            for m in range(1, N_DEV):
                for t in range(self.n):
                    send(t, m).start()

        def mid():
            pass

        def finish():
            for m in range(1, N_DEV):
                for t in range(self.n):
                    arrival(t, m).wait_recv()
            for m in range(1, N_DEV):
                for t in range(self.n):
                    send(t, m).wait_send()
            for t in range(self.n):
                own(t).wait()

        return start, mid, finish


class _ExchangeHalf:
    def __init__(self, xs, half, into=None):
        self.xs = list(xs)
        self.n = len(self.xs)
        self.half = half
        self.into = None if into is None else list(into)

    def out_shape(self):
        return [S((N_DEV,) + a.shape[1:], a.dtype) for a in self.xs]

    def scratch(self):
        return [pltpu.SemaphoreType.DMA((self.n, 7)), pltpu.SemaphoreType.DMA((self.n, 7)), pltpu.SemaphoreType.DMA((self.n,))]

    def phases(self, g_refs, r_refs, send_sems, recv_sems, local_sems):
        x, y, c = _xyc()
        me = 4 * x + 2 * y + c
        mine = 2 * y + c
        dest = x == self.half
        near, far = (1, 2, 3), (4, 5, 6, 7)

        def own(t):
            return pltpu.make_async_copy(g_refs[t].at[mine], r_refs[t].at[me], local_sems.at[t])

        def send(t, m):
            py, pc = y ^ ((m >> 1) & 1), c ^ (m & 1)
            return pltpu.make_async_remote_copy(
                src_ref=g_refs[t].at[2 * py + pc], dst_ref=r_refs[t].at[me],
                send_sem=send_sems.at[t, m - 1], recv_sem=recv_sems.at[t, m - 1],
                device_id=(x ^ (m >> 2), py, pc), device_id_type=MESH)

        def arrival(t, m):
            peer = 4 * (x ^ (m >> 2)) + 2 * (y ^ ((m >> 1) & 1)) + (c ^ (m & 1))
            return pltpu.make_async_remote_copy(
                src_ref=g_refs[t].at[mine], dst_ref=r_refs[t].at[peer],
                send_sem=send_sems.at[t, m - 1], recv_sem=recv_sems.at[t, m - 1],
                device_id=(x, y, c), device_id_type=MESH)

        def start():
            @pl.when(dest)
            def _():
                for t in range(self.n):
                    own(t).start()
                for m in near:
                    for t in range(self.n):
                        send(t, m).start()

            @pl.when(jnp.logical_not(dest))
            def _():
                for m in far:
                    for t in range(self.n):
                        send(t, m).start()

        def mid():
            pass

        def finish():
            @pl.when(dest)
            def _():
                for m in near + far:
                    for t in range(self.n):
                        arrival(t, m).wait_recv()
                for m in near:
                    for t in range(self.n):
                        send(t, m).wait_send()
                for t in range(self.n):
                    own(t).wait()

            @pl.when(jnp.logical_not(dest))
            def _():
                for m in far:
                    for t in range(self.n):
                        send(t, m).wait_send()

        return start, mid, finish


class _Both:
    def __init__(self, a, b):
        self.a, self.b = a, b
        self.xs = a.xs + b.xs
        self.n = a.n + b.n
        self.into = getattr(a, "into", None)

    def out_shape(self):
        return self.a.out_shape() + self.b.out_shape()

    def scratch(self):
        return self.a.scratch() + self.b.scratch()

    def phases(self, x_refs, out_refs, *sems):
        na = self.a.n
        pa = self.a.phases(x_refs[:na], out_refs[:na], *sems[:3])
        pb = self.b.phases(x_refs[na:], out_refs[na:], *sems[3:])
        return tuple((lambda f, g: (lambda: (f(), g())))(f, g) for f, g in zip(pa, pb))


def _run_side(side, name):
    n = side.n
    into = getattr(side, "into", None) or []

    def body(*refs):
        outs = refs[n + len(into):2 * n + len(into)]
        start, mid, finish = side.phases(refs[:n], outs, *refs[2 * n + len(into):])
        start()
        mid()
        finish()

    return list(pl.pallas_call(
        body, name=name, out_shape=side.out_shape(), in_specs=[ANY] * (n + len(into)), out_specs=[ANY] * n,
        input_output_aliases={n + t: t for t in range(len(into))},
        scratch_shapes=side.scratch())(*side.xs, *into))


def _grid_steps(grid):
    def ids():
        return [pl.program_id(a) for a in range(len(grid))]

    def first():
        return functools.reduce(jnp.logical_and, [i == 0 for i in ids()])

    def middle():
        i = ids()
        return functools.reduce(jnp.logical_and, [i[0] == (3 * grid[0]) // 4] + [j == 0 for j in i[1:]])

    def last():
        return functools.reduce(jnp.logical_and, [i == g - 1 for i, g in zip(ids(), grid)])

    return first, middle, last


def _call(body, *, name, grid, in_specs, out_specs, out_shape, scratch_shapes, operands, side=None, aliases=None):
    n_in, n_out, n_scr = len(in_specs), len(out_shape), len(scratch_shapes)
    full = body
    if side is not None:
        ns = side.n
        first, middle, last = _grid_steps(grid)

        def full(*refs):
            a = n_in
            ins, sins = refs[:a], refs[a:a + ns]
            a += ns
            outs, souts = refs[a:a + n_out], refs[a + n_out:a + n_out + ns]
            a += n_out + ns
            scr, sems = refs[a:a + n_scr], refs[a + n_scr:]
            start, mid, finish = side.phases(sins, souts, *sems)
            pl.when(first())(start)
            body(*ins, *outs, *scr)
            pl.when(middle())(mid)
            pl.when(last())(finish)

        in_specs = list(in_specs) + [ANY] * ns
        out_specs = list(out_specs) + [ANY] * ns
        out_shape = list(out_shape) + side.out_shape()
        scratch_shapes = list(scratch_shapes) + side.scratch()
        operands = list(operands) + side.xs
    outs = pl.pallas_call(
        full, name=name, grid=grid, in_specs=list(in_specs), out_specs=list(out_specs), out_shape=list(out_shape),
        scratch_shapes=list(scratch_shapes), input_output_aliases=aliases or {},
        compiler_params=_cp(dimension_semantics=("arbitrary",) * len(grid)),
    )(*operands)
    return list(outs[:n_out]), list(outs[n_out:])


def _ffn_fwd(h, gain, win, wout, name, side=None):
    t = h.shape[0]
    tm = _row_tile(t, 704)
    nt = t // tm

    def body(h_ref, g_ref, wg_ref, wu_ref, wo_ref, hn_ref, xn_ref, pg_ref, pu_ref, acc):
        c = pl.program_id(1)

        @pl.when(c == 0)
        def _():
            x = h_ref[...]
            r = lax.rsqrt(jnp.mean(x * x, axis=-1, keepdims=True) + EPS)
            xn_ref[...] = (x * r * g_ref[...]).astype(BF16)
            acc[...] = jnp.zeros_like(acc)

        wo = wo_ref[...].reshape(FF_SHARD, D)
        subs = _sub_rows(tm)
        gus = [(_dot(xn_ref[r, :], wg_ref[...]), _dot(xn_ref[r, :], wu_ref[...])) for r in subs]
        for r, (g, u) in zip(subs, gus):
            pg_ref[r, :] = g.astype(BF16)
            pu_ref[r, :] = u.astype(BF16)
            act = (g * _sigmoid(g) * u).astype(BF16)
            acc[r, :] += _dot(act, wo)

        @pl.when(c == N_FF_CHUNK - 1)
        def _():
            hn_ref[...] = h_ref[...] + 0.5 * acc[...]

    return _call(
        body, name=name, grid=(nt, N_FF_CHUNK), side=side,
        in_specs=[
            pl.BlockSpec((tm, D), lambda i, c: (i, 0)),
            pl.BlockSpec((1, D), lambda i, c: (0, 0)),
            pl.BlockSpec((None, D, FF_SHARD), lambda i, c: (c, 0, 0)),
            pl.BlockSpec((None, D, FF_SHARD), lambda i, c: (c + N_FF_CHUNK, 0, 0)),
            pl.BlockSpec((2, FF_SHARD // 2, D), lambda i, c: (c, 0, 0)),
        ],
        out_specs=[
            pl.BlockSpec((tm, D), lambda i, c: (i, 0)),
            pl.BlockSpec((tm, D), lambda i, c: (i, 0)),
            pl.BlockSpec((None, tm, FF_SHARD), lambda i, c: (c, i, 0)),
            pl.BlockSpec((None, tm, FF_SHARD), lambda i, c: (c, i, 0)),
        ],
        out_shape=[S((t, D), F32), S((t, D), BF16), S((N_FF_CHUNK, t, FF_SHARD), BF16), S((N_FF_CHUNK, t, FF_SHARD), BF16)],
        scratch_shapes=[pltpu.VMEM((tm, D), F32)],
        operands=[h, gain, win, win, wout])


def _ffn_bwd(dh, h, gain, pg, pu, win, wout, name, side=None):
    t = h.shape[0]
    tm = _row_tile(t, 704)
    nt = t // tm

    def body(dh_ref, h_ref, g_ref, pg_ref, pu_ref, wg_ref, wu_ref, wo_ref,
             dhi_ref, dob_ref, dpg_ref, dpu_ref, act_ref, dgain_ref, acc):
        i, c = pl.program_id(0), pl.program_id(1)

        @pl.when(c == 0)
        def _():
            dob_ref[...] = (0.5 * dh_ref[...]).astype(BF16)
            acc[...] = jnp.zeros_like(acc)

        @pl.when((i == 0) & (c == 0))
        def _():
            dgain_ref[...] = jnp.zeros_like(dgain_ref)

        wo = wo_ref[...].reshape(FF_SHARD, D)
        subs = _sub_rows(tm)
        dacts = [_dot_nt(dob_ref[r, :], wo) for r in subs]
        for r, dact in zip(subs, dacts):
            g = pg_ref[r, :].astype(F32)
            u = pu_ref[r, :].astype(F32)
            s = _sigmoid(g)
            sl = g * s
            act_ref[r, :] = (sl * u).astype(BF16)
            dg = (dact * u * (s * (1.0 + g * (1.0 - s)))).astype(BF16)
            du = (dact * sl).astype(BF16)
            dpg_ref[r, :] = dg
            dpu_ref[r, :] = du
            acc[r, :] += _dot_nt(dg, wg_ref[...]) + _dot_nt(du, wu_ref[...])

        @pl.when(c == N_FF_CHUNK - 1)
        def _():
            dx, dgn = _rms_bwd(acc[...], h_ref[...], g_ref[...])
            dhi_ref[...] = dh_ref[...] + dx
            dgain_ref[0:1, :] += dgn

    blk = pl.BlockSpec((None, tm, FF_SHARD), lambda i, c: (c, i, 0))
    row = pl.BlockSpec((tm, D), lambda i, c: (i, 0))
    return _call(
        body, name=name, grid=(nt, N_FF_CHUNK), side=side,
        in_specs=[
            row, row, pl.BlockSpec((1, D), lambda i, c: (0, 0)), blk, blk,
            pl.BlockSpec((None, D, FF_SHARD), lambda i, c: (c, 0, 0)),
            pl.BlockSpec((None, D, FF_SHARD), lambda i, c: (c + N_FF_CHUNK, 0, 0)),
            pl.BlockSpec((2, FF_SHARD // 2, D), lambda i, c: (c, 0, 0)),
        ],
        out_specs=[row, row, blk, blk, blk, pl.BlockSpec((8, D), lambda i, c: (0, 0))],
        out_shape=[S((t, D), F32), S((t, D), BF16)] + [S((N_FF_CHUNK, t, FF_SHARD), BF16)] * 3 + [S((8, D), F32)],
        scratch_shapes=[pltpu.VMEM((tm, D), F32)],
        operands=[dh, h, gain, pg, pu, win, win, wout])


def _ffn_dw_half(xn, dp, name, side=None):
    (out,), got = _call_dw_half(xn, dp, name, side)
    return out, got


def _call_dw_half(xn, dp, name, side):
    t = xn.shape[0]
    tk = _row_tile(t, DW_ROWS)
    nk = t // tk

    def body(a_ref, b_ref, o_ref, acc):
        k = pl.program_id(1)

        @pl.when(k == 0)
        def _():
            acc[...] = jnp.zeros_like(acc)

        acc[...] += _dot_tn(b_ref[...], a_ref[...])

        @pl.when(k == nk - 1)
        def _():
            o_ref[...] = acc[...].astype(BF16)

    return _call(
        body, name=name, grid=(N_FF_CHUNK, nk), side=side,
        in_specs=[pl.BlockSpec((tk, D), lambda c, k: (k, 0)), pl.BlockSpec((None, tk, FF_SHARD), lambda c, k: (c, k, 0))],
        out_specs=[pl.BlockSpec((None, FF_SHARD, D), lambda c, k: (c, 0, 0))],
        out_shape=[S((N_FF_CHUNK, FF_SHARD, D), BF16)],
        scratch_shapes=[pltpu.VMEM((FF_SHARD, D), F32)],
        operands=[xn, dp])


def _ffn_dw_in(xn, dpg, dpu, name, side=None):
    t = xn.shape[0]
    tk = _row_tile(t, DW_ROWS)
    nk = t // tk

    def body(a_ref, bg_ref, bu_ref, o_ref, acc):
        c, k = pl.program_id(0), pl.program_id(1)

        @pl.when(k == 0)
        def _():
            acc[...] = jnp.zeros_like(acc)

        @pl.when(c < N_FF_CHUNK)
        def _():
            acc[...] += _dot_tn(bg_ref[...], a_ref[...])

        @pl.when(c >= N_FF_CHUNK)
        def _():
            acc[...] += _dot_tn(bu_ref[...], a_ref[...])

        @pl.when(k == nk - 1)
        def _():
            o_ref[...] = acc[...].astype(BF16)

    return _call(
        body, name=name, grid=(2 * N_FF_CHUNK, nk), side=side,
        in_specs=[
            pl.BlockSpec((tk, D), lambda c, k: (k, 0)),
            pl.BlockSpec((None, tk, FF_SHARD), lambda c, k: (jnp.minimum(c, N_FF_CHUNK - 1), k, 0)),
            pl.BlockSpec((None, tk, FF_SHARD), lambda c, k: (jnp.maximum(c - N_FF_CHUNK, 0), k, 0)),
        ],
        out_specs=[pl.BlockSpec((None, FF_SHARD, D), lambda c, k: (c, 0, 0))],
        out_shape=[S((2 * N_FF_CHUNK, FF_SHARD, D), BF16)],
        scratch_shapes=[pltpu.VMEM((FF_SHARD, D), F32)],
        operands=[xn, dpg, dpu])


def _mm_tn(a, b, tn, name, tm=None, rows=DW_ROWS, shard_out=False):
    ca, t, m = a.shape
    cb, _, n = b.shape
    nc = max(ca, cb)
    tm = m if tm is None else tm
    tk = _row_tile(t, rows)
    nk = t // tk

    def body(a_ref, b_ref, o_ref, acc):
        k = pl.program_id(3)

        @pl.when(k == 0)
        def _():
            acc[...] = jnp.zeros_like(acc)

        acc[...] += _dot_tn(a_ref[...], b_ref[...])

        @pl.when(k == nk - 1)
        def _():
            o_ref[...] = acc[...].astype(BF16)

    if shard_out:
        out_spec = pl.BlockSpec((None, tm, tn), lambda c, i, j, k: (j, 0, 0))
        out_shape = S((n // tn, m, tn), BF16)
    else:
        out_spec = pl.BlockSpec((None, tm, tn), lambda c, i, j, k: (c, i, j))
        out_shape = S((nc, m, n), BF16)
    return pl.pallas_call(
        body, name=name, grid=(nc, m // tm, n // tn, nk),
        in_specs=[
            pl.BlockSpec((None, tk, tm), (lambda c, i, j, k: (c, k, i)) if ca > 1 else (lambda c, i, j, k: (0, k, i))),
            pl.BlockSpec((None, tk, tn), (lambda c, i, j, k: (c, k, j)) if cb > 1 else (lambda c, i, j, k: (0, k, j))),
        ],
        out_specs=out_spec, out_shape=out_shape,
        scratch_shapes=[pltpu.VMEM((tm, tn), F32)],
        compiler_params=_cp(dimension_semantics=("arbitrary",) * 4),
    )(a, b)


def _norm_mm(h, gain, w, tn, name, side=None):
    t = h.shape[0]
    n = w.shape[-1] if w.ndim == 2 else w.shape[0] * w.shape[2]
    tm = _row_tile(t, 704)
    kb = 1 if w.ndim == 2 else tn // w.shape[2]
    w_spec = (pl.BlockSpec((D, tn), lambda i, j: (0, j)) if w.ndim == 2
              else pl.BlockSpec((kb, D, tn // kb), lambda i, j: (j, 0, 0)))

    def body(h_ref, g_ref, w_ref, o_ref, xn_ref):
        @pl.when(pl.program_id(1) == 0)
        def _():
            x = h_ref[...]
            r = lax.rsqrt(jnp.mean(x * x, axis=-1, keepdims=True) + EPS)
            xn_ref[...] = (x * r * g_ref[...]).astype(BF16)

        if w.ndim == 2:
            o_ref[...] = _dot(xn_ref[...], w_ref[...]).astype(BF16)
        else:
            for b in range(kb):
                o_ref[:, b * (tn // kb):(b + 1) * (tn // kb)] = _dot(xn_ref[...], w_ref[b]).astype(BF16)

    return _call(
        body, name=name, grid=(t // tm, n // tn), side=side,
        in_specs=[pl.BlockSpec((tm, D), lambda i, j: (i, 0)), pl.BlockSpec((1, D), lambda i, j: (0, 0)), w_spec],
        out_specs=[pl.BlockSpec((tm, tn), lambda i, j: (i, j)), pl.BlockSpec((tm, D), lambda i, j: (i, 0))],
        out_shape=[S((t, n), BF16), S((t, D), BF16)], scratch_shapes=[],
        operands=[h, gain, w])


def _proj_bwd(dproj, w, dh, h, gain, tk, name, side=None):
    t, n = dproj.shape
    tm = _row_tile(t, 704)
    nk = n // tk
    kb = 1 if w.ndim == 2 else tk // w.shape[2]
    w_spec = (pl.BlockSpec((D, tk), lambda i, k: (0, k)) if w.ndim == 2
              else pl.BlockSpec((kb, D, tk // kb), lambda i, k: (k, 0, 0)))

    def body(dp_ref, w_ref, dh_ref, h_ref, g_ref, dhi_ref, dgain_ref, acc):
        i, k = pl.program_id(0), pl.program_id(1)

        @pl.when(k == 0)
        def _():
            acc[...] = jnp.zeros_like(acc)

        @pl.when((i == 0) & (k == 0))
        def _():
            dgain_ref[...] = jnp.zeros_like(dgain_ref)

        if w.ndim == 2:
            acc[...] += _dot_nt(dp_ref[...], w_ref[...])
        else:
            for b in range(kb):
                acc[...] += _dot_nt(dp_ref[:, b * (tk // kb):(b + 1) * (tk // kb)], w_ref[b])

        @pl.when(k == nk - 1)
        def _():
            dx, dgn = _rms_bwd(acc[...], h_ref[...], g_ref[...])
            dhi_ref[...] = dh_ref[...] + dx
            dgain_ref[0:1, :] += dgn

    row = pl.BlockSpec((tm, D), lambda i, k: (i, 0))
    return _call(
        body, name=name, grid=(t // tm, nk), side=side,
        in_specs=[pl.BlockSpec((tm, tk), lambda i, k: (i, k)), w_spec,
                  row, row, pl.BlockSpec((1, D), lambda i, k: (0, 0))],
        out_specs=[row, pl.BlockSpec((8, D), lambda i, k: (0, 0))],
        out_shape=[S((t, D), F32), S((8, D), F32)],
        scratch_shapes=[pltpu.VMEM((tm, D), F32)],
        operands=[dproj, w, dh, h, gain])


def _post_fwd(o, proj, hgain, wout, h, nh, dv, name, side=None):
    t = h.shape[0]
    w = nh * dv
    tm = _row_tile(t, 704)

    def body(o_ref, g_ref, hg_ref, wo_ref, h_ref, hn_ref, og_ref):
        for hd in range(nh):
            sl = slice(hd * dv, (hd + 1) * dv)
            oh = o_ref[:, sl].astype(F32)
            r = lax.rsqrt(jnp.mean(oh * oh, axis=-1, keepdims=True) + EPS)
            gg = g_ref[:, sl].astype(F32)
            og_ref[:, sl] = (oh * r * hg_ref[:, sl] * (gg * _sigmoid(gg))).astype(BF16)
        hn_ref[...] = h_ref[...] + _dot(og_ref[...], wo_ref[...])

    return _call(
        body, name=name, grid=(t // tm,), side=side,
        in_specs=[pl.BlockSpec((tm, w), lambda i: (i, 0)), pl.BlockSpec((tm, w), lambda i: (i, 2)),
                  pl.BlockSpec((1, w), lambda i: (0, 0)), pl.BlockSpec((w, D), lambda i: (0, 0)),
                  pl.BlockSpec((tm, D), lambda i: (i, 0))],
        out_specs=[pl.BlockSpec((tm, D), lambda i: (i, 0)), pl.BlockSpec((tm, w), lambda i: (i, 0))],
        out_shape=[S((t, D), F32), S((t, w), BF16)], scratch_shapes=[],
        operands=[o, proj, hgain, wout, h])


def _post_bwd(dh, o, proj, hgain, wout, nh, dv, nproj, name, side=None):
    t = dh.shape[0]
    w = nh * dv
    tm = _row_tile(t, 704)

    def body(dh_ref, o_ref, g_ref, hg_ref, wo_ref, do_ref, dg_ref, dhb_ref, dhg_ref):
        @pl.when(pl.program_id(0) == 0)
        def _():
            dhg_ref[...] = jnp.zeros_like(dhg_ref)

        dmix = dh_ref[...].astype(BF16)
        dhb_ref[...] = dmix
        dog = _dot_nt(dmix, wo_ref[...])
        for hd in range(nh):
            sl = slice(hd * dv, (hd + 1) * dv)
            oh = o_ref[:, sl].astype(F32)
            r = lax.rsqrt(jnp.mean(oh * oh, axis=-1, keepdims=True) + EPS)
            xh = oh * r
            gain = hg_ref[:, sl]
            gg = g_ref[:, sl].astype(F32)
            s = _sigmoid(gg)
            dogh = dog[:, sl]
            don = dogh * (gg * s)
            dg_ref[:, sl] = (dogh * (xh * gain) * (s * (1.0 + gg * (1.0 - s)))).astype(BF16)
            dxh = don * gain
            do_ref[:, sl] = (r * (dxh - xh * jnp.mean(dxh * xh, axis=-1, keepdims=True))).astype(BF16)
            dhg_ref[0:1, sl] += jnp.sum(don * xh, axis=0, keepdims=True)

    return _call(
        body, name=name, grid=(t // tm,), side=side,
        in_specs=[pl.BlockSpec((tm, D), lambda i: (i, 0)), pl.BlockSpec((tm, w), lambda i: (i, 0)),
                  pl.BlockSpec((tm, w), lambda i: (i, 2)), pl.BlockSpec((1, w), lambda i: (0, 0)),
                  pl.BlockSpec((w, D), lambda i: (0, 0))],
        out_specs=[pl.BlockSpec((tm, w), lambda i: (i, 0)), pl.BlockSpec((tm, w), lambda i: (i, 2)),
                   pl.BlockSpec((tm, D), lambda i: (i, 0)), pl.BlockSpec((8, w), lambda i: (0, 0))],
        out_shape=[S((t, w), BF16), S((t, nproj), BF16), S((t, D), BF16), S((8, w), F32)], scratch_shapes=[],
        operands=[dh, o, proj, hgain, wout])


def _ret_consts():
    lg = np.log1p(-np.exp2(-5.0 - np.arange(RET_H, dtype=np.float32))).astype(np.float32)
    return jnp.asarray(np.broadcast_to(lg[:, None, None], (RET_H, 1, 128)).copy())


def _rope_tables(t):
    half = RET_DK // 2
    inv = 1.0 / (ROPE_BASE ** jnp.linspace(0.0, 1.0, half, dtype=F32))
    base = (jnp.arange(t // CHUNK) * CHUNK - PAD).astype(F32)[:, None] * inv[None, :]
    off = jnp.arange(CHUNK).astype(F32)[:, None] * inv[None, :]
    ca, sa = jnp.cos(base)[:, None, :], jnp.sin(base)[:, None, :]
    cb, sb = jnp.cos(off)[None], jnp.sin(off)[None]
    return (ca * cb - sa * sb).reshape(t, half), (sa * cb + ca * sb).reshape(t, half)


def _ret_chunk(blk_ref, cos_ref, sin_ref, lg, h):
    c = RET_C
    half = RET_DK // 2
    oq, ok, ov = h * RET_DK, RET_H * RET_DK + h * RET_DK, 2 * RET_H * RET_DK + h * RET_DV
    cs, sn = cos_ref[...], sin_ref[...]
    q1, q2 = blk_ref[:, oq:oq + half].astype(F32), blk_ref[:, oq + half:oq + RET_DK].astype(F32)
    k1, k2 = blk_ref[:, ok:ok + half].astype(F32), blk_ref[:, ok + half:ok + RET_DK].astype(F32)
    qr = jnp.concatenate([q1 * cs - q2 * sn, q1 * sn + q2 * cs], axis=1)
    kr = jnp.concatenate([k1 * cs - k2 * sn, k1 * sn + k2 * cs], axis=1) * (RET_DK ** -0.5)
    v = blk_ref[:, ov:ov + RET_DV]
    ii = lax.broadcasted_iota(jnp.int32, (c, 1), 0).astype(F32)
    jj = lax.broadcasted_iota(jnp.int32, (1, c), 1).astype(F32)
    rel = ii - jj
    dmat = jnp.where(rel >= 0, jnp.exp(lg * jnp.maximum(rel, 0.0)), 0.0)
    dq = jnp.exp(lg * (ii + 1.0))
    dk = jnp.exp(lg * (c - 1.0 - ii))
    dchunk = jnp.exp(lg * float(c))
    return qr, kr, v, dmat, dq, dk, dchunk


def _ret_scan_fwd(proj, cos, sin, lgam, name, side=None):
    t = proj.shape[0]
    c = RET_C
    nc = t // c

    def body(blk_ref, cos_ref, sin_ref, lg_ref, o_ref, st_ref, state):
        @pl.when(pl.program_id(0) == 0)
        def _():
            state[...] = jnp.zeros_like(state)

        for h in range(RET_H):
            qr, kr, v, dmat, dq, dk, dchunk = _ret_chunk(blk_ref, cos_ref, sin_ref, lg_ref[h, :, 0:1], h)
            sp = state[h]
            st_ref[h] = sp.astype(BF16)
            scores = _dot_nt(qr.astype(BF16), kr.astype(BF16)) * dmat
            o = _dot(scores.astype(BF16), v) + _dot((qr * dq).astype(BF16), sp.astype(BF16))
            o_ref[:, h * RET_DV:(h + 1) * RET_DV] = o.astype(BF16)
            state[h] = sp * dchunk + _dot_tn((kr * dk).astype(BF16), v)

    return _call(
        body, name=name, grid=(nc,), side=side,
        in_specs=[pl.BlockSpec((c, RET_QKV), lambda n: (n, 0)), pl.BlockSpec((c, 128), lambda n: (n, 0)),
                  pl.BlockSpec((c, 128), lambda n: (n, 0)), pl.BlockSpec((RET_H, 1, 128), lambda n: (0, 0, 0))],
        out_specs=[pl.BlockSpec((c, RET_H * RET_DV), lambda n: (n, 0)),
                   pl.BlockSpec((RET_H, None, RET_DK, RET_DV), lambda n: (0, n, 0, 0))],
        out_shape=[S((t, RET_H * RET_DV), BF16), S((RET_H, nc, RET_DK, RET_DV), BF16)],
        scratch_shapes=[pltpu.VMEM((RET_H, RET_DK, RET_DV), F32)],
        operands=[proj, cos, sin, lgam])


def _ret_scan_bwd(proj, cos, sin, lgam, do, states, dproj, name, side=None):
    t = proj.shape[0]
    c = RET_C
    nc = t // c
    half = RET_DK // 2

    def body(blk_ref, cos_ref, sin_ref, lg_ref, do_ref, st_ref, dp_in, dp_ref, dstate):
        n = nc - 1 - pl.program_id(0)

        @pl.when(pl.program_id(0) == 0)
        def _():
            dstate[...] = jnp.zeros_like(dstate)

        cs, sn = cos_ref[...], sin_ref[...]
        rows = n * c + lax.broadcasted_iota(jnp.int32, (c, 1), 0)
        keep = rows >= PAD

        def unrot(d):
            d1, d2 = d[:, :half], d[:, half:]
            return jnp.concatenate([d1 * cs + d2 * sn, d2 * cs - d1 * sn], axis=1)

        for h in range(RET_H):
            qr, kr, v, dmat, dq, dk, dchunk = _ret_chunk(blk_ref, cos_ref, sin_ref, lg_ref[h, :, 0:1], h)
            qb, kb = qr.astype(BF16), kr.astype(BF16)
            dob = do_ref[:, h * RET_DV:(h + 1) * RET_DV]
            sp = st_ref[h]
            ds = dstate[h]
            dsb = ds.astype(BF16)
            p = (_dot_nt(qb, kb) * dmat).astype(BF16)
            dvv = _dot_tn(p, dob) + _dot((kr * dk).astype(BF16), dsb)
            dp = (_dot_nt(dob, v) * dmat).astype(BF16)
            dqr = _dot(dp, kb) + _dot_nt(dob, sp) * dq
            dkr = (_dot_tn(dp, qb) + _dot_nt(v, dsb) * dk) * (RET_DK ** -0.5)
            dstate[h] = ds * dchunk + _dot_tn((qr * dq).astype(BF16), dob)
            oq, ok, ov = h * RET_DK, RET_H * RET_DK + h * RET_DK, 2 * RET_H * RET_DK + h * RET_DV
            dp_ref[:, oq:oq + RET_DK] = jnp.where(keep, unrot(dqr), 0.0).astype(BF16)
            dp_ref[:, ok:ok + RET_DK] = jnp.where(keep, unrot(dkr), 0.0).astype(BF16)
            dp_ref[:, ov:ov + RET_DV] = jnp.where(keep, dvv, 0.0).astype(BF16)

    return _call(
        body, name=name, grid=(nc,), side=side, aliases={6: 0},
        in_specs=[pl.BlockSpec((c, RET_QKV), lambda n: (nc - 1 - n, 0)), pl.BlockSpec((c, 128), lambda n: (nc - 1 - n, 0)),
                  pl.BlockSpec((c, 128), lambda n: (nc - 1 - n, 0)), pl.BlockSpec((RET_H, 1, 128), lambda n: (0, 0, 0)),
                  pl.BlockSpec((c, RET_H * RET_DV), lambda n: (nc - 1 - n, 0)),
                  pl.BlockSpec((RET_H, None, RET_DK, RET_DV), lambda n: (0, nc - 1 - n, 0, 0)), ANY],
        out_specs=[pl.BlockSpec((c, RET_QKV), lambda n: (nc - 1 - n, 0))],
        out_shape=[S((t, dproj.shape[1]), BF16)],
        scratch_shapes=[pltpu.VMEM((RET_H, RET_DK, RET_DV), F32)],
        operands=[proj, cos, sin, lgam, do, states, dproj])


def _split3(x):
    hi = x.astype(BF16)
    r1 = x - hi.astype(F32)
    mid = r1.astype(BF16)
    lo = (r1 - mid.astype(F32)).astype(BF16)
    return hi, mid, lo


def _gla_chunk(blk_ref, z_ref, wg_ref, bg_ref, n, h, b_ref=None):
    c = CHUNK
    oq, ok, ov = h * GLA_DK, GLA_H * GLA_DK + h * GLA_DK, 2 * GLA_H * GLA_DK + h * GLA_DV
    q = blk_ref[:, oq:oq + GLA_DK].astype(F32) * (GLA_DK ** -0.5)
    k = blk_ref[:, ok:ok + GLA_DK].astype(F32)
    v = blk_ref[:, ov:ov + GLA_DV]
    hs = slice(h * GLA_DK, (h + 1) * GLA_DK)
    u = _dot(z_ref[...], wg_ref[:, hs]) + bg_ref[:, hs]
    rows = n * c + lax.broadcasted_iota(jnp.int32, (c, 1), 0)
    keep = rows >= PAD
    if b_ref is not None:
        return q, k, v, u, b_ref[:, hs], keep
    la = (jnp.minimum(u, 0.0) - jnp.log(1.0 + jnp.exp(-jnp.abs(u)))) * (1.0 / GLA_TAU)
    la = jnp.where(keep, la, 0.0)
    ii = lax.broadcasted_iota(jnp.int32, (c, c), 0)
    jj = lax.broadcasted_iota(jnp.int32, (c, c), 1)
    tril = (ii >= jj).astype(BF16)
    hi, mid, lo = _split3(la)
    b = _dot(tril, hi) + _dot(tril, mid) + _dot(tril, lo)
    return q, k, v, u, b, keep


def _gla_intra(qs, ks, bs, a_ref):
    c = CHUNK
    nh = len(qs)
    col = lax.broadcasted_iota(jnp.int32, (1, c), 1)
    rowi = lax.broadcasted_iota(jnp.int32, (SUB, 1), 0)
    for blk in range(c // SUB):
        r = slice(SUB * blk, SUB * (blk + 1))
        arows = []
        for h in range(nh):
            q, k, b = qs[h], ks[h], bs[h]
            if blk > 0:
                bprev = b[SUB * blk - 1:SUB * blk]
                qe = q[r] * jnp.exp(b[r] - bprev)
                kt = k * jnp.exp(jnp.minimum(bprev - b, 0.0))
                arows.append(jnp.where(col < SUB * blk, _dot_nt(qe.astype(BF16), kt.astype(BF16)), 0.0))
            else:
                arows.append(jnp.zeros((SUB, c), F32))
        half = SUB // 2
        lo = slice(SUB * blk + half, SUB * (blk + 1))
        tops = [a[:half] for a in arows]
        bots = [a[half:] for a in arows]
        for j in range(SUB):
            for h in range(nh):
                bj, kj = bs[h][SUB * blk + j:SUB * blk + j + 1], ks[h][SUB * blk + j:SUB * blk + j + 1]
                if j < half:
                    a = jnp.sum(qs[h][r] * kj * jnp.exp(bs[h][r] - bj), axis=1, keepdims=True)
                    tops[h] = jnp.where(col == SUB * blk + j, a[:half], tops[h])
                    bots[h] = jnp.where(col == SUB * blk + j, a[half:], bots[h])
                else:
                    a = jnp.sum(qs[h][lo] * kj * jnp.exp(bs[h][lo] - bj), axis=1, keepdims=True)
                    bots[h] = jnp.where(col == SUB * blk + j, a, bots[h])
        for h in range(nh):
            arow = jnp.concatenate([tops[h], bots[h]], axis=0)
            a_ref[h, r, :] = jnp.where(col - SUB * blk <= rowi, arow, 0.0)


def _gla_scan_fwd(proj, wgp, bg, name, side=None):
    t = proj.shape[0]
    c = CHUNK
    nc = t // c
    heads = range(GLA_H)

    def body(blk_ref, z_ref, wg_ref, bg_ref, o_ref, st_ref, am_ref, bs_ref, state, a_ref):
        n = pl.program_id(0)

        @pl.when(n == 0)
        def _():
            state[...] = jnp.zeros_like(state)

        qs, ks, vs, us, bs, keeps = zip(*[_gla_chunk(blk_ref, z_ref, wg_ref, bg_ref, n, h) for h in heads])
        _gla_intra(qs, ks, bs, a_ref)
        for h in heads:
            q, k, v, b = qs[h], ks[h], vs[h], bs[h]
            sp = state[h]
            st_ref[h] = sp.astype(BF16)
            ab = a_ref[h].astype(BF16)
            am_ref[:, h * c:(h + 1) * c] = ab
            bs_ref[:, h * GLA_DK:(h + 1) * GLA_DK] = b
            o = _dot(ab, v) + _dot_nt((q * jnp.exp(b)).astype(BF16), sp.astype(BF16))
            o_ref[:, h * GLA_DV:(h + 1) * GLA_DV] = o.astype(BF16)
            bc = b[c - 1:c]
            state[h] = sp * jnp.exp(bc) + _dot_tn(v, (k * jnp.exp(bc - b)).astype(BF16))

    return _call(
        body, name=name, grid=(nc,), side=side,
        in_specs=[pl.BlockSpec((c, GLA_QKV), lambda n: (n, 0)), pl.BlockSpec((c, 128), lambda n: (n, GLA_ZBLK)),
                  pl.BlockSpec((128, GLA_H * GLA_DK), lambda n: (0, 0)), pl.BlockSpec((1, GLA_H * GLA_DK), lambda n: (0, 0))],
        out_specs=[pl.BlockSpec((c, GLA_H * GLA_DV), lambda n: (n, 0)),
                   pl.BlockSpec((GLA_H, None, GLA_DV, GLA_DK), lambda n: (0, n, 0, 0)),
                   pl.BlockSpec((c, GLA_H * c), lambda n: (n, 0)),
                   pl.BlockSpec((c, GLA_H * GLA_DK), lambda n: (n, 0))],
        out_shape=[S((t, GLA_H * GLA_DV), BF16), S((GLA_H, nc, GLA_DV, GLA_DK), BF16), S((t, GLA_H * c), BF16),
                   S((t, GLA_H * GLA_DK), F32)],
        scratch_shapes=[pltpu.VMEM((GLA_H, GLA_DV, GLA_DK), F32), pltpu.VMEM((GLA_H, c, c), F32)],
        operands=[proj, proj, wgp, bg])


def _gla_scan_bwd(proj, wgp, bg, do, states, amat, bcum, dproj, name):
    t = proj.shape[0]
    c = CHUNK
    nc = t // c
    heads = range(GLA_H)

    def body(blk_ref, z_ref, wg_ref, bg_ref, do_ref, st_ref, am_ref, bs_ref, dp_in, dp_ref, du_ref, dstate, dq_ref, dkd_ref):
        n = nc - 1 - pl.program_id(0)

        @pl.when(pl.program_id(0) == 0)
        def _():
            dstate[...] = jnp.zeros_like(dstate)

        qs, ks, vs, us, bs, keeps = zip(*[_gla_chunk(blk_ref, z_ref, wg_ref, bg_ref, n, h, bs_ref) for h in heads])
        ii = lax.broadcasted_iota(jnp.int32, (c, c), 0)
        jj = lax.broadcasted_iota(jnp.int32, (c, c), 1)
        col = lax.broadcasted_iota(jnp.int32, (1, c), 1)
        rowi = lax.broadcasted_iota(jnp.int32, (SUB, 1), 0)
        rowc = lax.broadcasted_iota(jnp.int32, (c, 1), 0)
        das, dvs, dq_inters, dk_states, extras, dks = [], [], [], [], [], []
        for h in heads:
            q, k, v, b = qs[h], ks[h], vs[h], bs[h]
            ab = am_ref[:, h * c:(h + 1) * c]
            dob = do_ref[:, h * GLA_DV:(h + 1) * GLA_DV]
            sp = st_ref[h]
            ds = dstate[h]
            dsb = ds.astype(BF16)
            bc = b[c - 1:c]
            eb = jnp.exp(b)
            ebc = jnp.exp(bc - b)
            ec = jnp.exp(bc)
            qb = (q * eb).astype(BF16)
            kb = (k * ebc).astype(BF16)
            dvs.append(_dot_tn(ab, dob) + _dot_nt(kb, dsb))
            das.append(jnp.where(ii >= jj, _dot_nt(dob, v), 0.0))
            dq_inters.append(_dot(dob, sp) * eb)
            dk_state = _dot(v, dsb) * ebc
            dk_states.append(dk_state)
            extras.append(jnp.sum(k * dk_state, axis=0, keepdims=True)
                          + ec * jnp.sum(sp.astype(F32) * ds, axis=0, keepdims=True))
            dstate[h] = ds * ec + _dot_tn(dob, qb)
            dks.append(jnp.zeros((c, GLA_DK), F32))

        for blk in range(c // SUB):
            r = slice(SUB * blk, SUB * (blk + 1))
            dq_is, dkds = [], []
            for h in heads:
                q, k, b = qs[h], ks[h], bs[h]
                if blk > 0:
                    bprev = b[SUB * blk - 1:SUB * blk]
                    e_i = jnp.exp(b[r] - bprev)
                    ek = jnp.exp(jnp.minimum(bprev - b, 0.0))
                    daoff = jnp.where(col < SUB * blk, das[h][r], 0.0).astype(BF16)
                    dq_is.append(_dot(daoff, (k * ek).astype(BF16)) * e_i)
                    dks[h] = dks[h] + _dot_tn(daoff, (q[r] * e_i).astype(BF16)) * ek
                else:
                    dq_is.append(jnp.zeros((SUB, GLA_DK), F32))
                dkds.append(jnp.zeros((SUB, GLA_DK), F32))
            half = SUB // 2
            lo = slice(SUB * blk + half, SUB * (blk + 1))
            row8 = rowi[:half]
            dq_tops = [a[:half] for a in dq_is]
            dq_bots = [a[half:] for a in dq_is]
            for j in range(SUB):
                for h in heads:
                    bj, kj = bs[h][SUB * blk + j:SUB * blk + j + 1], ks[h][SUB * blk + j:SUB * blk + j + 1]
                    if j < half:
                        e = jnp.where(rowi >= j, jnp.exp(bs[h][r] - bj), 0.0)
                        dacol = jnp.sum(jnp.where(col == SUB * blk + j, das[h][r], 0.0), axis=1, keepdims=True)
                        tt = dacol * e
                        dq_tops[h] = dq_tops[h] + tt[:half] * kj
                        dq_bots[h] = dq_bots[h] + tt[half:] * kj
                        dkrow = jnp.sum(tt * qs[h][r], axis=0, keepdims=True)
                    else:
                        e = jnp.where(row8 + half >= j, jnp.exp(bs[h][lo] - bj), 0.0)
                        dacol = jnp.sum(jnp.where(col == SUB * blk + j, das[h][lo], 0.0), axis=1, keepdims=True)
                        tt = dacol * e
                        dq_bots[h] = dq_bots[h] + tt * kj
                        dkrow = jnp.sum(tt * qs[h][lo], axis=0, keepdims=True)
                    dkds[h] = jnp.where(rowi == j, dkrow, dkds[h])
            for h in heads:
                dq_ref[h, r, :] = jnp.concatenate([dq_tops[h], dq_bots[h]], axis=0)
                dkd_ref[h, r, :] = dkds[h]

        for h in heads:
            q, k, b, u, keep = qs[h], ks[h], bs[h], us[h], keeps[h]
            dq = dq_ref[h] + dq_inters[h]
            dk = dks[h] + dkd_ref[h] + dk_states[h]
            db = q * dq - k * dk + jnp.where(rowc == c - 1, extras[h], 0.0)
            triu = (ii <= jj).astype(BF16)
            hi, mid, lo = _split3(db)
            dla = _dot(triu, hi) + _dot(triu, mid) + _dot(triu, lo)
            du = jnp.where(keep, dla * (1.0 / GLA_TAU) / (1.0 + jnp.exp(u)), 0.0)
            du_ref[:, h * GLA_DK:(h + 1) * GLA_DK] = du.astype(BF16)
            oq, ok, ov = h * GLA_DK, GLA_H * GLA_DK + h * GLA_DK, 2 * GLA_H * GLA_DK + h * GLA_DV
            dp_ref[:, oq:oq + GLA_DK] = jnp.where(keep, dq * (GLA_DK ** -0.5), 0.0).astype(BF16)
            dp_ref[:, ok:ok + GLA_DK] = jnp.where(keep, dk, 0.0).astype(BF16)
            dp_ref[:, ov:ov + GLA_DV] = jnp.where(keep, dvs[h], 0.0).astype(BF16)

    nproj = dproj.shape[1]
    return pl.pallas_call(
        body, name=name, grid=(nc,),
        in_specs=[pl.BlockSpec((c, GLA_QKV), lambda n: (nc - 1 - n, 0)), pl.BlockSpec((c, 128), lambda n: (nc - 1 - n, GLA_ZBLK)),
                  pl.BlockSpec((128, GLA_H * GLA_DK), lambda n: (0, 0)), pl.BlockSpec((1, GLA_H * GLA_DK), lambda n: (0, 0)),
                  pl.BlockSpec((c, GLA_H * GLA_DV), lambda n: (nc - 1 - n, 0)),
                  pl.BlockSpec((GLA_H, None, GLA_DV, GLA_DK), lambda n: (0, nc - 1 - n, 0, 0)),
                  pl.BlockSpec((c, GLA_H * c), lambda n: (nc - 1 - n, 0)),
                  pl.BlockSpec((c, GLA_H * GLA_DK), lambda n: (nc - 1 - n, 0)), ANY],
        out_specs=[pl.BlockSpec((c, GLA_QKV), lambda n: (nc - 1 - n, 0)),
                   pl.BlockSpec((c, GLA_H * GLA_DK), lambda n: (nc - 1 - n, 0))],
        out_shape=[S((t, nproj), BF16), S((t, GLA_H * GLA_DK), BF16)],
        input_output_aliases={8: 0},
        scratch_shapes=[pltpu.VMEM((GLA_H, GLA_DV, GLA_DK), F32),
                        pltpu.VMEM((GLA_H, c, GLA_DK), F32), pltpu.VMEM((GLA_H, c, GLA_DK), F32)],
        compiler_params=_cp(dimension_semantics=("arbitrary",)),
    )(proj, proj, wgp, bg, do, states, amat, bcum, dproj)


def _gla_gate_bwd(du, proj, wgp, dproj, name):
    t = du.shape[0]
    tm = _row_tile(t, 704)
    w = GLA_H * GLA_DK

    def body(du_ref, z_ref, wg_ref, dp_in, dp_ref, dwg_ref, dbg_ref):
        @pl.when(pl.program_id(0) == 0)
        def _():
            dwg_ref[...] = jnp.zeros_like(dwg_ref)
            dbg_ref[...] = jnp.zeros_like(dbg_ref)

        d = du_ref[...]
        dp_ref[...] = _dot_nt(d, wg_ref[...]).astype(BF16)
        dwg_ref[...] += _dot_tn(z_ref[...], d)
        dbg_ref[0:1, :] += jnp.sum(d.astype(F32), axis=0, keepdims=True)

    return pl.pallas_call(
        body, name=name, grid=(t // tm,),
        in_specs=[pl.BlockSpec((tm, w), lambda i: (i, 0)), pl.BlockSpec((tm, 128), lambda i: (i, GLA_ZBLK)),
                  pl.BlockSpec((128, w), lambda i: (0, 0)), ANY],
        out_specs=[pl.BlockSpec((tm, 128), lambda i: (i, GLA_ZBLK)), pl.BlockSpec((128, w), lambda i: (0, 0)),
                   pl.BlockSpec((8, w), lambda i: (0, 0))],
        out_shape=[S(dproj.shape, BF16), S((128, w), F32), S((8, w), F32)],
        input_output_aliases={3: 0},
        compiler_params=_cp(dimension_semantics=("arbitrary",)),
    )(du, proj, wgp, dproj)


def _final_loss(h, gain, target, name):
    t = h.shape[0]
    tm = _row_tile(t, 704)

    def body(h_ref, g_ref, t_ref, dh_ref, dgain_ref, loss_ref):
        i = pl.program_id(0)

        @pl.when(i == 0)
        def _():
            dgain_ref[...] = jnp.zeros_like(dgain_ref)
            loss_ref[...] = jnp.zeros_like(loss_ref)

        x = h_ref[...]
        gain = g_ref[...]
        r = lax.rsqrt(jnp.mean(x * x, axis=-1, keepdims=True) + EPS)
        xh = x * r
        rows = i * tm + lax.broadcasted_iota(jnp.int32, (tm, 1), 0)
        e = jnp.where(rows >= CHUNK, xh * gain - t_ref[...], 0.0)
        loss_ref[...] += 0.5 * jnp.sum(jnp.mean(e * e, axis=-1, keepdims=True), axis=0, keepdims=True)
        dy = e * (1.0 / D)
        dgain_ref[0:1, :] += jnp.sum(dy * xh, axis=0, keepdims=True)
        dxh = dy * gain
        dh_ref[...] = r * (dxh - xh * jnp.mean(dxh * xh, axis=-1, keepdims=True))

    row = pl.BlockSpec((tm, D), lambda i: (i, 0))
    return pl.pallas_call(
        body, name=name, grid=(t // tm,),
        in_specs=[row, pl.BlockSpec((1, D), lambda i: (0, 0)), row],
        out_specs=[row, pl.BlockSpec((8, D), lambda i: (0, 0)), pl.BlockSpec((8, 128), lambda i: (0, 0))],
        out_shape=[S((t, D), F32), S((8, D), F32), S((8, 128), F32)],
        compiler_params=_cp(dimension_semantics=("arbitrary",)),
    )(h, gain, target)


def _adam_math(w, g, m, v):
    m2 = ADAM_B1 * m + (1.0 - ADAM_B1) * g
    v2 = ADAM_B2 * v + (1.0 - ADAM_B2) * (g * g)
    m_hat = m2 / (1.0 - ADAM_B1 ** ADAM_STEP)
    v_hat = v2 / (1.0 - ADAM_B2 ** ADAM_STEP)
    delta = -ADAM_LR * (m_hat / (jnp.sqrt(v_hat) + ADAM_EPS) + ADAM_WD * w)
    return delta, m2, v2


def _adamw_reduce(recvs, w, m, v, name):
    nl, r, wd = w.shape
    tr = _row_tile(r, 256) if r % 16 == 0 else r
    nr = r // tr

    def body(*refs):
        rv_refs = refs[:nl]
        w_ref, m_ref, v_ref, g_ref, d_ref, m2_ref, v2_ref = refs[nl:]
        layer = pl.program_id(0)

        def total(rv_ref):
            g = rv_ref[0].astype(F32)
            for s in range(1, N_DEV):
                g = g + rv_ref[s].astype(F32)
            return g

        g = total(rv_refs[0])
        for k in range(1, nl):
            g = jnp.where(layer == k, total(rv_refs[k]), g)
        g_ref[...] = g
        d_ref[...], m2_ref[...], v2_ref[...] = _adam_math(w_ref[...], g, m_ref[...], v_ref[...])

    def rv_spec(k):
        return pl.BlockSpec((N_DEV, tr, wd), lambda l, i: (0, jnp.where(l == k, i, jnp.where(l < k, 0, nr - 1)), 0))

    row = pl.BlockSpec((None, tr, wd), lambda l, i: (l, i, 0))
    return pl.pallas_call(
        body, name=name, grid=(nl, nr),
        in_specs=[rv_spec(k) for k in range(nl)] + [row, row, row],
        out_specs=[row] * 4, out_shape=[S((nl, r, wd), F32)] * 4,
        compiler_params=_cp(dimension_semantics=("arbitrary", "arbitrary")),
    )(*recvs, w, m, v)


def _small_reduce(parts, name):
    _, r, wd = parts.shape

    def body(p_ref, o_ref):
        g = p_ref[0]
        for s in range(1, N_DEV):
            g = g + p_ref[s]
        o_ref[...] = g

    return pl.pallas_call(body, name=name, out_shape=S((r, wd), F32), compiler_params=_cp())(parts)


def _adamw_small(w, g, m, v, name):
    def body(w_ref, g_ref, m_ref, v_ref, d_ref, m2_ref, v2_ref):
        d_ref[...], m2_ref[...], v2_ref[...] = _adam_math(w_ref[...], g_ref[...], m_ref[...], v_ref[...])

    return pl.pallas_call(body, name=name, out_shape=[S(w.shape, F32)] * 3, compiler_params=_cp())(w, g, m, v)


def _unshard_cols(g):
    return jnp.transpose(g, (1, 0, 2)).reshape(g.shape[1], N_DEV * g.shape[2])


def _my_cols(full, width):
    me = 4 * lax.axis_index("x") + 2 * lax.axis_index("y") + lax.axis_index("c")
    return lax.dynamic_slice_in_dim(full, me * width, width, axis=1)


def kernel(x, meta_tokens, norm_ffn1, ffn1_w_in, ffn1_w_out, norm_mix, norm_ffn2, ffn2_w_in, ffn2_w_out, ret_w_in, ret_head_norm, ret_w_out, gla_w_in, gla_w_gate, gla_b_gate, gla_head_norm, gla_w_out, final_norm, loss_target, m_meta_tokens, m_norm_ffn1, m_ffn1_w_in, m_ffn1_w_out, m_norm_mix, m_norm_ffn2, m_ffn2_w_in, m_ffn2_w_out, m_ret_w_in, m_ret_head_norm, m_ret_w_out, m_gla_w_in, m_gla_w_gate, m_gla_b_gate, m_gla_head_norm, m_gla_w_out, m_final_norm, v_meta_tokens, v_norm_ffn1, v_ffn1_w_in, v_ffn1_w_out, v_norm_mix, v_norm_ffn2, v_ffn2_w_in, v_ffn2_w_out, v_ret_w_in, v_ret_head_norm, v_ret_w_out, v_gla_w_in, v_gla_w_gate, v_gla_b_gate, v_gla_head_norm, v_gla_w_out, v_final_norm):
    seq = x.shape[1]
    t = seq + CHUNK
    xs = x[0]
    target = loss_target[0]

    def ffn_w(f):
        w_in, w_out = (ffn1_w_in, ffn1_w_out) if f < 2 else (ffn2_w_in, ffn2_w_out)
        return [w_in[f % 2].astype(BF16), w_out[f % 2].astype(BF16)]

    small = jnp.concatenate([meta_tokens.reshape(-1), ret_head_norm.reshape(-1), gla_w_gate.reshape(-1),
                             gla_b_gate.reshape(-1), gla_head_norm.reshape(-1)])
    n_small = small.shape[0]
    small = jnp.pad(small, (0, 32 * 128 - n_small)).reshape(32, 128)
    sg, win0, wout0 = _run_side(_Gather([small] + ffn_w(0)), "ag_first")
    sg = sg.reshape(N_DEV, 32 * 128)

    def small_cols(off, rows, width):
        return jnp.transpose(sg[:, off:off + rows * width].reshape(N_DEV, rows, width), (1, 0, 2)).reshape(rows, N_DEV * width)

    off = 0
    meta_full = small_cols(off, N_META, D // N_DEV); off += N_META * (D // N_DEV)
    ret_hn = small_cols(off, RET_H, RET_DV // N_DEV).reshape(1, RET_H * RET_DV); off += RET_H * RET_DV // N_DEV
    wgate = small_cols(off, GLA_RANK, GLA_H * GLA_DK // N_DEV); off += GLA_RANK * GLA_H * GLA_DK // N_DEV
    bgate = small_cols(off, 1, GLA_H * GLA_DK // N_DEV); off += GLA_H * GLA_DK // N_DEV
    gla_hn = small_cols(off, GLA_H, GLA_DV // N_DEV).reshape(1, GLA_H * GLA_DV)
    wgp = jnp.pad(wgate, ((0, 128 - GLA_RANK), (0, 0))).astype(BF16)

    cos, sin = _rope_tables(t)
    lgam = _ret_consts()

    h0 = jnp.concatenate([jnp.zeros((PAD, D), F32), meta_full, xs], axis=0)
    g1 = [norm_ffn1[i:i + 1] for i in range(2)]
    gm = [norm_mix[i:i + 1] for i in range(2)]
    g2 = [norm_ffn2[i:i + 1] for i in range(2)]

    (h1, xn_a0, pg_a0, pu_a0), (ret_win_g, ret_wout_g) = _ffn_fwd(
        h0, g1[0], win0, wout0, "ffn1_l0_fwd", side=_Gather([ret_w_in[0].astype(BF16), ret_w_out[0].astype(BF16)]))
    ret_win = ret_win_g
    ret_wout = ret_wout_g.reshape(RET_H * RET_DV, D)
    (rproj, rhn), (win2,) = _norm_mm(h1, gm[0], ret_win, 4 * ret_win.shape[2], "ret_proj_fwd", side=_Gather(ffn_w(2)[:1]))
    (ro, rstates), (wout2,) = _ret_scan_fwd(rproj, cos, sin, lgam, "ret_scan_fwd", side=_Gather(ffn_w(2)[1:]))
    (h2, rog), _ = _post_fwd(ro, rproj, ret_hn, ret_wout, h1, RET_H, RET_DV, "ret_post_fwd")
    (h3, xn_b0, pg_b0, pu_b0), (win1, wout1) = _ffn_fwd(h2, g2[0], win2, wout2, "ffn2_l0_fwd", side=_Gather(ffn_w(1)))
    (h4, xn_a1, pg_a1, pu_a1), (gla_win_g, gla_wout_g) = _ffn_fwd(
        h3, g1[1], win1, wout1, "ffn1_l1_fwd", side=_Gather([gla_w_in[0].astype(BF16), gla_w_out[0].astype(BF16)]))
    gla_win = _unshard_cols(gla_win_g)
    gla_win = jnp.pad(gla_win, ((0, 0), (0, GLA_N - gla_win.shape[1])))
    gla_wout = gla_wout_g.reshape(GLA_H * GLA_DV, D)
    (gproj, ghn), _ = _norm_mm(h4, gm[1], gla_win, GLA_N, "gla_proj_fwd")
    (go, gstates, gamat, gbcum), (win3, wout3) = _gla_scan_fwd(gproj, wgp, bgate, "gla_scan_fwd", side=_Gather(ffn_w(3)))
    (h5, gog), _ = _post_fwd(go, gproj, gla_hn, gla_wout, h4, GLA_H, GLA_DV, "gla_post_fwd")
    (h6, xn_b1, pg_b1, pu_b1), _ = _ffn_fwd(h5, g2[1], win3, wout3, "ffn2_l1_fwd")

    dh, dfinal, loss_blk = _final_loss(h6, final_norm.reshape(1, D), jnp.pad(target, ((CHUNK, 0), (0, 0))), "final_loss")
    loss = lax.psum(loss_blk[0, 0], ("x", "y", "c"))

    def ffn_back(dh, h_in, xn, gain, pg, pu, win, wout, tag, side=None, dw_side=None):
        (dh_in, dob, dpg, dpu, act, dgain), got = _ffn_bwd(dh, h_in, gain, pg, pu, win, wout, tag + "_bwd", side=side)
        dwout = _mm_tn(act, dob[None], D, tag + "_dw_out").reshape(N_DEV, FF_SHARD // 2, D)
        if dw_side == "last":
            gate, got_out = _ffn_dw_half(xn, dpg, tag + "_dw_gate", side=_Exchange([dwout]))
            up, (recv,) = _ffn_dw_half(xn, dpu, tag + "_dw_up", side=_ExchangeHalf([gate], 0))
            return dh_in, [up, recv], dgain[0], got, got_out
        (dwin,), dw_got = _ffn_dw_in(xn, dpg, dpu, tag + "_dw_in", side=dw_side)
        return dh_in, [dwin, dwout], dgain[0], got, dw_got

    dh, dw_b1, dg2_1, _, _ = ffn_back(dh, h5, xn_b1, g2[1], pg_b1, pu_b1, win3, wout3, "ffn2_l1")

    (gdo, gdproj, gdhb, dghn), _ = _post_bwd(dh, go, gproj, gla_hn, gla_wout, GLA_H, GLA_DV, GLA_N, "gla_post_bwd")
    d_gla_wout = _mm_tn(gog[None], gdhb[None], D, "gla_dw_out").reshape(N_DEV, GLA_H * GLA_DV // N_DEV, D)
    gdproj, gdu = _gla_scan_bwd(gproj, wgp, bgate, gdo, gstates, gamat, gbcum, gdproj, "gla_scan_bwd")
    gdproj, dwg, dbg = _gla_gate_bwd(gdu, gproj, wgp, gdproj, "gla_gate_bwd")
    d_gla_win = _mm_tn(gdproj[None], ghn[None], D, "gla_dw_in", tm=640)[0]
    (dh, dgm_1), _ = _proj_bwd(gdproj, gla_win, dh, h4, gm[1], GLA_N, "gla_proj_bwd")
    n_gla_in = 2 * GLA_H * GLA_DK + 2 * GLA_H * GLA_DV + GLA_RANK
    d_gla_win = d_gla_win[:n_gla_in].reshape(N_DEV, n_gla_in // N_DEV, D)

    dh, dw_a1, dg1_1, rv_b1, rv_gla = ffn_back(dh, h3, xn_a1, g1[1], pg_a1, pu_a1, win1, wout1, "ffn1_l1",
                                               side=_Exchange(dw_b1), dw_side=_Exchange([d_gla_win, d_gla_wout]))
    dh, dw_b0, dg2_0, rv_a1, _ = ffn_back(dh, h2, xn_b0, g2[0], pg_b0, pu_b0, win2, wout2, "ffn2_l0", side=_Exchange(dw_a1))

    (rdo, rdproj, rdhb, drhn), rv_b0_out = _post_bwd(dh, ro, rproj, ret_hn, ret_wout, RET_H, RET_DV, 6 * D, "ret_post_bwd",
                                                     side=_Exchange(dw_b0[1:]))
    d_ret_wout = _mm_tn(rog[None], rdhb[None], D, "ret_dw_out", rows=DW_ROWS // 2).reshape(N_DEV, RET_H * RET_DV // N_DEV, D)
    (rdproj,), rv_b0_in = _ret_scan_bwd(rproj, cos, sin, lgam, rdo, rstates, rdproj, "ret_scan_bwd", side=_Exchange(dw_b0[:1]))
    rv_b0 = rv_b0_in + rv_b0_out
    d_ret_win = _mm_tn(rhn[None], rdproj[None], ret_win.shape[2], "ret_dw_in", shard_out=True)
    (dh, dgm_0), rv_ret_out = _proj_bwd(rdproj, ret_win, dh, h1, gm[0], 4 * ret_win.shape[2], "ret_proj_bwd", side=_Exchange([d_ret_wout]))

    dh, (dw_up, recv_a0), dg1_0, rv_ret_in, rv_a0_out = ffn_back(dh, h0, xn_a0, g1[0], pg_a0, pu_a0, win0, wout0, "ffn1_l0",
                                                                 side=_Exchange([d_ret_win]), dw_side="last")
    rv_ret = rv_ret_in + rv_ret_out
    grad_x = dh[CHUNK:][None]

    dmeta = dh[PAD:CHUNK]
    parts = jnp.concatenate([
        dg1_0, dg1_1, dgm_0[0], dgm_1[0], dg2_0, dg2_1, dfinal[0], dmeta.reshape(-1), drhn[0], dwg[:GLA_RANK].reshape(-1),
        dbg[0], dghn[0]])
    n_parts = parts.shape[0]
    rows = -(-n_parts // D)
    rows = -(-rows // 8) * 8
    parts = jnp.pad(parts, (0, rows * D - n_parts)).reshape(rows, D)
    recv_a0, parts_all = _run_side(_Both(_ExchangeHalf([dw_up], 1, into=[recv_a0]), _Gather([parts])), "xchg_last")
    rv_a0 = [recv_a0] + rv_a0_out
    tot = _small_reduce(parts_all, "small_grad_sum").reshape(-1)

    def adam_t(recvs, w, m, v, tag):
        outs = _adamw_reduce(recvs, *(jnp.swapaxes(a, 1, 2) for a in (w, m, v)), tag)
        return [jnp.swapaxes(o, 1, 2) for o in outs]

    u_ffn1_in = adam_t([rv_a0[0], rv_a1[0]], ffn1_w_in, m_ffn1_w_in, v_ffn1_w_in, "adam_ffn1_w_in")
    u_ffn2_in = adam_t([rv_b0[0], rv_b1[0]], ffn2_w_in, m_ffn2_w_in, v_ffn2_w_in, "adam_ffn2_w_in")
    u_ffn1_out = _adamw_reduce([rv_a0[1], rv_a1[1]], ffn1_w_out, m_ffn1_w_out, v_ffn1_w_out, "adam_ffn1_w_out")
    u_ffn2_out = _adamw_reduce([rv_b0[1], rv_b1[1]], ffn2_w_out, m_ffn2_w_out, v_ffn2_w_out, "adam_ffn2_w_out")
    u_ret_in = _adamw_reduce([rv_ret[0]], ret_w_in, m_ret_w_in, v_ret_w_in, "adam_ret_w_in")
    u_ret_out = _adamw_reduce([rv_ret[1]], ret_w_out, m_ret_w_out, v_ret_w_out, "adam_ret_w_out")
    u_gla_in = adam_t([rv_gla[0]], gla_w_in, m_gla_w_in, v_gla_w_in, "adam_gla_w_in")
    u_gla_out = _adamw_reduce([rv_gla[1]], gla_w_out, m_gla_w_out, v_gla_w_out, "adam_gla_w_out")


    off = 0
    def take(nel):
        nonlocal off
        out = tot[off:off + nel]
        off += nel
        return out

    gr_norm_ffn1 = take(2 * D).reshape(2, D)
    gr_norm_mix = take(2 * D).reshape(2, D)
    gr_norm_ffn2 = take(2 * D).reshape(2, D)
    gr_final = take(D)
    gr_meta = _my_cols(take(N_META * D).reshape(N_META, D), D // N_DEV)
    gr_ret_hn = _my_cols(take(RET_H * RET_DV).reshape(RET_H, RET_DV), RET_DV // N_DEV)[None]
    gr_wgate = _my_cols(take(GLA_RANK * GLA_H * GLA_DK).reshape(GLA_RANK, GLA_H * GLA_DK), GLA_H * GLA_DK // N_DEV)[None]
    gr_bgate = _my_cols(take(GLA_H * GLA_DK).reshape(1, GLA_H * GLA_DK), GLA_H * GLA_DK // N_DEV)
    gr_gla_hn = _my_cols(take(GLA_H * GLA_DV).reshape(GLA_H, GLA_DV), GLA_DV // N_DEV)[None]

    small_w = [meta_tokens, norm_ffn1, norm_mix, norm_ffn2, ret_head_norm, gla_w_gate, gla_b_gate, gla_head_norm, final_norm]
    small_g = [gr_meta, gr_norm_ffn1, gr_norm_mix, gr_norm_ffn2, gr_ret_hn, gr_wgate, gr_bgate, gr_gla_hn, gr_final]
    small_m = [m_meta_tokens, m_norm_ffn1, m_norm_mix, m_norm_ffn2, m_ret_head_norm, m_gla_w_gate, m_gla_b_gate, m_gla_head_norm, m_final_norm]
    small_v = [v_meta_tokens, v_norm_ffn1, v_norm_mix, v_norm_ffn2, v_ret_head_norm, v_gla_w_gate, v_gla_b_gate, v_gla_head_norm, v_final_norm]

    def pack(arrs):
        flat = jnp.concatenate([a.reshape(-1) for a in arrs])
        n = flat.shape[0]
        r = -(-n // 128)
        r = -(-r // 8) * 8
        return jnp.pad(flat, (0, r * 128 - n), constant_values=1.0).reshape(r, 128)

    sd, sm, sv = _adamw_small(pack(small_w), pack(small_g), pack(small_m), pack(small_v), "adam_small")

    def unpack(buf):
        flat = buf.reshape(-1)
        outs, o = [], 0
        for a in small_w:
            outs.append(flat[o:o + a.size].reshape(a.shape))
            o += a.size
        return outs

    us_d, us_m, us_v = unpack(sd), unpack(sm), unpack(sv)

    def ordered(k, smalls):
        return (smalls[0], smalls[1], u_ffn1_in[k], u_ffn1_out[k], smalls[2], smalls[3], u_ffn2_in[k], u_ffn2_out[k],
                u_ret_in[k], smalls[4], u_ret_out[k], u_gla_in[k], smalls[5], smalls[6], smalls[7], u_gla_out[k], smalls[8])

    return (loss, grad_x, *ordered(0, small_g), *ordered(1, us_d), *ordered(2, us_m), *ordered(3, us_v))
```

```python
import functools
import math

import numpy as np
import jax
import jax.numpy as jnp
from jax import lax
from jax.experimental import pallas as pl
from jax.experimental.pallas import tpu as pltpu

F32 = jnp.float32
BF16 = jnp.bfloat16
S = jax.ShapeDtypeStruct
ANY = pl.BlockSpec(memory_space=pl.ANY)
MESH = pl.DeviceIdType.MESH

D = 1024
N_META = 16
CHUNK = 64
PAD = CHUNK - N_META
EPS = 1e-6
N_DEV = 8
FF_SHARD = 704
N_FF_CHUNK = 4
RET_H, RET_DK, RET_DV = 4, 256, 512
RET_QKV = RET_H * (2 * RET_DK + RET_DV)
RET_C = 192
GLA_H, GLA_DK, GLA_DV, GLA_RANK, GLA_TAU = 4, 128, 256, 16, 16.0
GLA_QKV = GLA_H * (2 * GLA_DK + GLA_DV)
GLA_N = 3200
GLA_ZBLK = 3072 // 128
SUB = 16
ROPE_BASE = 10000.0
ADAM_LR, ADAM_B1, ADAM_B2, ADAM_EPS, ADAM_WD, ADAM_STEP = 0.001, 0.9, 0.999, 1e-08, 0.01, 10
VMEM_LIMIT = 58 * 1024 * 1024
DW_ROWS = 2752


def _cp(**kw):
    return pltpu.CompilerParams(vmem_limit_bytes=VMEM_LIMIT, **kw)


def _row_tile(t, cap):
    best = 16
    for d in range(16, cap + 1, 16):
        if t % d == 0:
            best = d
    return best


def _sub_rows(tm, parts=2):
    units = tm // 16
    cuts = [16 * (units * p // parts) for p in range(parts + 1)]
    return [slice(a, b) for a, b in zip(cuts[:-1], cuts[1:]) if b > a]


def _dot(a, b):
    return jnp.dot(a, b, preferred_element_type=F32)


def _dot_nt(a, b):
    return lax.dot_general(a, b, (((1,), (1,)), ((), ())), preferred_element_type=F32)


def _dot_tn(a, b):
    return lax.dot_general(a, b, (((0,), (0,)), ((), ())), preferred_element_type=F32)


def _sigmoid(x):
    return pl.reciprocal(1.0 + jnp.exp(-x), approx=True)


def _rms_bwd(dxn, x, gain):
    r = lax.rsqrt(jnp.mean(x * x, axis=-1, keepdims=True) + EPS)
    xh = x * r
    dxh = dxn * gain
    dx = r * (dxh - xh * jnp.mean(dxh * xh, axis=-1, keepdims=True))
    return dx, jnp.sum(dxn * xh, axis=0, keepdims=True)


def _xyc():
    return lax.axis_index("x"), lax.axis_index("y"), lax.axis_index("c")


class _Gather:
    def __init__(self, xs):
        self.xs = list(xs)
        self.n = len(self.xs)

    def out_shape(self):
        return [S((N_DEV,) + a.shape, a.dtype) for a in self.xs]

    def scratch(self):
        return [pltpu.SemaphoreType.DMA((self.n, 7)), pltpu.SemaphoreType.DMA((self.n, 7)), pltpu.SemaphoreType.DMA((self.n,))]

    def phases(self, x_refs, out_refs, send_sems, recv_sems, local_sems):
        x, y, c = _xyc()
        me, sibling = (x, y, c), (x, y, 1 - c)
        chips = [(1 - x, y), (x, 1 - y), (1 - x, 1 - y)]

        def copy(t, k, block, to, src=None):
            px, py, pc = block
            dst = out_refs[t].at[4 * px + 2 * py + pc]
            return pltpu.make_async_remote_copy(
                src_ref=dst if src is None else src, dst_ref=dst,
                send_sem=send_sems.at[t, k], recv_sem=recv_sems.at[t, k], device_id=to, device_id_type=MESH)

        def own(t):
            return pltpu.make_async_copy(x_refs[t], out_refs[t].at[4 * x + 2 * y + c], local_sems.at[t])

        def first(t):
            return [copy(t, 0, me, sibling, src=x_refs[t])] + [
                copy(t, 1 + j, me, (*chip, c), src=x_refs[t]) for j, chip in enumerate(chips)]

        def passed(t):
            return [copy(t, 4 + j, (*chip, c), sibling) for j, chip in enumerate(chips)]

        def start():
            for t in range(self.n):
                own(t).start()
                for cp in first(t):
                    cp.start()

        def mid():
            for t in range(self.n):
                fw = passed(t)
                for j, chip in enumerate(chips):
                    copy(t, 1 + j, (*chip, c), me).wait_recv()
                    fw[j].start()

        def finish():
            for t in range(self.n):
                copy(t, 0, sibling, me).wait_recv()
                for j, chip in enumerate(chips):
                    copy(t, 4 + j, (*chip, 1 - c), me).wait_recv()
                for cp in first(t) + passed(t):
                    cp.wait_send()
                own(t).wait()

        return start, mid, finish


class _Exchange:
    def __init__(self, xs):
        self.xs = list(xs)
        self.n = len(self.xs)

    def out_shape(self):
        return [S(a.shape, a.dtype) for a in self.xs]

    def scratch(self):
        return [pltpu.SemaphoreType.DMA((self.n, 7)), pltpu.SemaphoreType.DMA((self.n, 7)), pltpu.SemaphoreType.DMA((self.n,))]

    def phases(self, g_refs, r_refs, send_sems, recv_sems, local_sems):
        x, y, c = _xyc()
        me = 4 * x + 2 * y + c

        def own(t):
            return pltpu.make_async_copy(g_refs[t].at[me], r_refs[t].at[me], local_sems.at[t])

        def send(t, m):
            px, py, pc = x ^ (m >> 2), y ^ ((m >> 1) & 1), c ^ (m & 1)
            return pltpu.make_async_remote_copy(
                src_ref=g_refs[t].at[4 * px + 2 * py + pc], dst_ref=r_refs[t].at[me],
                send_sem=send_sems.at[t, m - 1], recv_sem=recv_sems.at[t, m - 1],
                device_id=(px, py, pc), device_id_type=MESH)

        def arrival(t, m):
            peer = 4 * (x ^ (m >> 2)) + 2 * (y ^ ((m >> 1) & 1)) + (c ^ (m & 1))
            return pltpu.make_async_remote_copy(
                src_ref=g_refs[t].at[peer], dst_ref=r_refs[t].at[peer],
                send_sem=send_sems.at[t, m - 1], recv_sem=recv_sems.at[t, m - 1],
                device_id=(x, y, c), device_id_type=MESH)

        def start():
            for t in range(self.n):
                own(t).start()
            for m in range(1, N_DEV):
                for t in range(self.n):
                    send(t, m).start()

        def mid():
            pass

        def finish():
            for m in range(1, N_DEV):
                for t in range(self.n):
                    arrival(t, m).wait_recv()
            for m in range(1, N_DEV):
                for t in range(self.n):
                    send(t, m).wait_send()
            for t in range(self.n):
                own(t).wait()

        return start, mid, finish


class _Both:
    def __init__(self, a, b):
        self.a, self.b = a, b
        self.xs = a.xs + b.xs
        self.n = a.n + b.n

    def out_shape(self):
        return self.a.out_shape() + self.b.out_shape()

    def scratch(self):
        return self.a.scratch() + self.b.scratch()

    def phases(self, x_refs, out_refs, *sems):
        na = self.a.n
        pa = self.a.phases(x_refs[:na], out_refs[:na], *sems[:3])
        pb = self.b.phases(x_refs[na:], out_refs[na:], *sems[3:])
        return tuple((lambda f, g: (lambda: (f(), g())))(f, g) for f, g in zip(pa, pb))


def _run_side(side, name):
    n = side.n

    def body(*refs):
        start, mid, finish = side.phases(refs[:n], refs[n:2 * n], *refs[2 * n:])
        start()
        mid()
        finish()

    return list(pl.pallas_call(
        body, name=name, out_shape=side.out_shape(), in_specs=[ANY] * n, out_specs=[ANY] * n,
        scratch_shapes=side.scratch())(*side.xs))


def _grid_steps(grid):
    def ids():
        return [pl.program_id(a) for a in range(len(grid))]

    def first():
        return functools.reduce(jnp.logical_and, [i == 0 for i in ids()])

    def middle():
        i = ids()
        return functools.reduce(jnp.logical_and, [i[0] == (3 * grid[0]) // 4] + [j == 0 for j in i[1:]])

    def last():
        return functools.reduce(jnp.logical_and, [i == g - 1 for i, g in zip(ids(), grid)])

    return first, middle, last


def _call(body, *, name, grid, in_specs, out_specs, out_shape, scratch_shapes, operands, side=None, aliases=None):
    n_in, n_out, n_scr = len(in_specs), len(out_shape), len(scratch_shapes)
    full = body
    if side is not None:
        ns = side.n
        first, middle, last = _grid_steps(grid)

        def full(*refs):
            a = n_in
            ins, sins = refs[:a], refs[a:a + ns]
            a += ns
            outs, souts = refs[a:a + n_out], refs[a + n_out:a + n_out + ns]
            a += n_out + ns
            scr, sems = refs[a:a + n_scr], refs[a + n_scr:]
            start, mid, finish = side.phases(sins, souts, *sems)
            pl.when(first())(start)
            body(*ins, *outs, *scr)
            pl.when(middle())(mid)
            pl.when(last())(finish)

        in_specs = list(in_specs) + [ANY] * ns
        out_specs = list(out_specs) + [ANY] * ns
        out_shape = list(out_shape) + side.out_shape()
        scratch_shapes = list(scratch_shapes) + side.scratch()
        operands = list(operands) + side.xs
    outs = pl.pallas_call(
        full, name=name, grid=grid, in_specs=list(in_specs), out_specs=list(out_specs), out_shape=list(out_shape),
        scratch_shapes=list(scratch_shapes), input_output_aliases=aliases or {},
        compiler_params=_cp(dimension_semantics=("arbitrary",) * len(grid)),
    )(*operands)
    return list(outs[:n_out]), list(outs[n_out:])


def _ffn_fwd(h, gain, win, wout, name, side=None):
    t = h.shape[0]
    tm = _row_tile(t, 704)
    nt = t // tm

    def body(h_ref, g_ref, wg_ref, wu_ref, wo_ref, hn_ref, xn_ref, pg_ref, pu_ref, acc):
        c = pl.program_id(1)

        @pl.when(c == 0)
        def _():
            x = h_ref[...]
            r = lax.rsqrt(jnp.mean(x * x, axis=-1, keepdims=True) + EPS)
            xn_ref[...] = (x * r * g_ref[...]).astype(BF16)
            acc[...] = jnp.zeros_like(acc)

        wo = wo_ref[...].reshape(FF_SHARD, D)
        subs = _sub_rows(tm)
        gus = [(_dot(xn_ref[r, :], wg_ref[...]), _dot(xn_ref[r, :], wu_ref[...])) for r in subs]
        for r, (g, u) in zip(subs, gus):
            pg_ref[r, :] = g.astype(BF16)
            pu_ref[r, :] = u.astype(BF16)
            act = (g * _sigmoid(g) * u).astype(BF16)
            acc[r, :] += _dot(act, wo)

        @pl.when(c == N_FF_CHUNK - 1)
        def _():
            hn_ref[...] = h_ref[...] + 0.5 * acc[...]

    return _call(
        body, name=name, grid=(nt, N_FF_CHUNK), side=side,
        in_specs=[
            pl.BlockSpec((tm, D), lambda i, c: (i, 0)),
            pl.BlockSpec((1, D), lambda i, c: (0, 0)),
            pl.BlockSpec((None, D, FF_SHARD), lambda i, c: (c, 0, 0)),
            pl.BlockSpec((None, D, FF_SHARD), lambda i, c: (c + N_FF_CHUNK, 0, 0)),
            pl.BlockSpec((2, FF_SHARD // 2, D), lambda i, c: (c, 0, 0)),
        ],
        out_specs=[
            pl.BlockSpec((tm, D), lambda i, c: (i, 0)),
            pl.BlockSpec((tm, D), lambda i, c: (i, 0)),
            pl.BlockSpec((None, tm, FF_SHARD), lambda i, c: (c, i, 0)),
            pl.BlockSpec((None, tm, FF_SHARD), lambda i, c: (c, i, 0)),
        ],
        out_shape=[S((t, D), F32), S((t, D), BF16), S((N_FF_CHUNK, t, FF_SHARD), BF16), S((N_FF_CHUNK, t, FF_SHARD), BF16)],
        scratch_shapes=[pltpu.VMEM((tm, D), F32)],
        operands=[h, gain, win, win, wout])


def _ffn_bwd(dh, h, gain, pg, pu, win, wout, name, side=None):
    t = h.shape[0]
    tm = _row_tile(t, 704)
    nt = t // tm

    def body(dh_ref, h_ref, g_ref, pg_ref, pu_ref, wg_ref, wu_ref, wo_ref,
             dhi_ref, dob_ref, dpg_ref, dpu_ref, act_ref, dgain_ref, acc):
        i, c = pl.program_id(0), pl.program_id(1)

        @pl.when(c == 0)
        def _():
            dob_ref[...] = (0.5 * dh_ref[...]).astype(BF16)
            acc[...] = jnp.zeros_like(acc)

        @pl.when((i == 0) & (c == 0))
        def _():
            dgain_ref[...] = jnp.zeros_like(dgain_ref)

        wo = wo_ref[...].reshape(FF_SHARD, D)
        subs = _sub_rows(tm)
        dacts = [_dot_nt(dob_ref[r, :], wo) for r in subs]
        for r, dact in zip(subs, dacts):
            g = pg_ref[r, :].astype(F32)
            u = pu_ref[r, :].astype(F32)
            s = _sigmoid(g)
            sl = g * s
            act_ref[r, :] = (sl * u).astype(BF16)
            dg = (dact * u * (s * (1.0 + g * (1.0 - s)))).astype(BF16)
            du = (dact * sl).astype(BF16)
            dpg_ref[r, :] = dg
            dpu_ref[r, :] = du
            acc[r, :] += _dot_nt(dg, wg_ref[...]) + _dot_nt(du, wu_ref[...])

        @pl.when(c == N_FF_CHUNK - 1)
        def _():
            dx, dgn = _rms_bwd(acc[...], h_ref[...], g_ref[...])
            dhi_ref[...] = dh_ref[...] + dx
            dgain_ref[0:1, :] += dgn

    blk = pl.BlockSpec((None, tm, FF_SHARD), lambda i, c: (c, i, 0))
    row = pl.BlockSpec((tm, D), lambda i, c: (i, 0))
    return _call(
        body, name=name, grid=(nt, N_FF_CHUNK), side=side,
        in_specs=[
            row, row, pl.BlockSpec((1, D), lambda i, c: (0, 0)), blk, blk,
            pl.BlockSpec((None, D, FF_SHARD), lambda i, c: (c, 0, 0)),
            pl.BlockSpec((None, D, FF_SHARD), lambda i, c: (c + N_FF_CHUNK, 0, 0)),
            pl.BlockSpec((2, FF_SHARD // 2, D), lambda i, c: (c, 0, 0)),
        ],
        out_specs=[row, row, blk, blk, blk, pl.BlockSpec((8, D), lambda i, c: (0, 0))],
        out_shape=[S((t, D), F32), S((t, D), BF16)] + [S((N_FF_CHUNK, t, FF_SHARD), BF16)] * 3 + [S((8, D), F32)],
        scratch_shapes=[pltpu.VMEM((tm, D), F32)],
        operands=[dh, h, gain, pg, pu, win, win, wout])


def _ffn_dw_in(xn, dpg, dpu, name, side=None):
    t = xn.shape[0]
    tk = _row_tile(t, DW_ROWS)
    nk = t // tk

    def body(a_ref, bg_ref, bu_ref, o_ref, acc):
        c, k = pl.program_id(0), pl.program_id(1)

        @pl.when(k == 0)
        def _():
            acc[...] = jnp.zeros_like(acc)

        @pl.when(c < N_FF_CHUNK)
        def _():
            acc[...] += _dot_tn(bg_ref[...], a_ref[...])

        @pl.when(c >= N_FF_CHUNK)
        def _():
            acc[...] += _dot_tn(bu_ref[...], a_ref[...])

        @pl.when(k == nk - 1)
        def _():
            o_ref[...] = acc[...].astype(BF16)

    return _call(
        body, name=name, grid=(2 * N_FF_CHUNK, nk), side=side,
        in_specs=[
            pl.BlockSpec((tk, D), lambda c, k: (k, 0)),
            pl.BlockSpec((None, tk, FF_SHARD), lambda c, k: (jnp.minimum(c, N_FF_CHUNK - 1), k, 0)),
            pl.BlockSpec((None, tk, FF_SHARD), lambda c, k: (jnp.maximum(c - N_FF_CHUNK, 0), k, 0)),
        ],
        out_specs=[pl.BlockSpec((None, FF_SHARD, D), lambda c, k: (c, 0, 0))],
        out_shape=[S((2 * N_FF_CHUNK, FF_SHARD, D), BF16)],
        scratch_shapes=[pltpu.VMEM((FF_SHARD, D), F32)],
        operands=[xn, dpg, dpu])


def _mm_tn(a, b, tn, name, tm=None, rows=DW_ROWS, shard_out=False):
    ca, t, m = a.shape
    cb, _, n = b.shape
    nc = max(ca, cb)
    tm = m if tm is None else tm
    tk = _row_tile(t, rows)
    nk = t // tk

    def body(a_ref, b_ref, o_ref, acc):
        k = pl.program_id(3)

        @pl.when(k == 0)
        def _():
            acc[...] = jnp.zeros_like(acc)

        acc[...] += _dot_tn(a_ref[...], b_ref[...])

        @pl.when(k == nk - 1)
        def _():
            o_ref[...] = acc[...].astype(BF16)

    if shard_out:
        out_spec = pl.BlockSpec((None, tm, tn), lambda c, i, j, k: (j, 0, 0))
        out_shape = S((n // tn, m, tn), BF16)
    else:
        out_spec = pl.BlockSpec((None, tm, tn), lambda c, i, j, k: (c, i, j))
        out_shape = S((nc, m, n), BF16)
    return pl.pallas_call(
        body, name=name, grid=(nc, m // tm, n // tn, nk),
        in_specs=[
            pl.BlockSpec((None, tk, tm), (lambda c, i, j, k: (c, k, i)) if ca > 1 else (lambda c, i, j, k: (0, k, i))),
            pl.BlockSpec((None, tk, tn), (lambda c, i, j, k: (c, k, j)) if cb > 1 else (lambda c, i, j, k: (0, k, j))),
        ],
        out_specs=out_spec, out_shape=out_shape,
        scratch_shapes=[pltpu.VMEM((tm, tn), F32)],
        compiler_params=_cp(dimension_semantics=("arbitrary",) * 4),
    )(a, b)


def _norm_mm(h, gain, w, tn, name, side=None):
    t = h.shape[0]
    n = w.shape[-1] if w.ndim == 2 else w.shape[0] * w.shape[2]
    tm = _row_tile(t, 704)
    kb = 1 if w.ndim == 2 else tn // w.shape[2]
    w_spec = (pl.BlockSpec((D, tn), lambda i, j: (0, j)) if w.ndim == 2
              else pl.BlockSpec((kb, D, tn // kb), lambda i, j: (j, 0, 0)))

    def body(h_ref, g_ref, w_ref, o_ref, xn_ref):
        @pl.when(pl.program_id(1) == 0)
        def _():
            x = h_ref[...]
            r = lax.rsqrt(jnp.mean(x * x, axis=-1, keepdims=True) + EPS)
            xn_ref[...] = (x * r * g_ref[...]).astype(BF16)

        if w.ndim == 2:
            o_ref[...] = _dot(xn_ref[...], w_ref[...]).astype(BF16)
        else:
            for b in range(kb):
                o_ref[:, b * (tn // kb):(b + 1) * (tn // kb)] = _dot(xn_ref[...], w_ref[b]).astype(BF16)

    return _call(
        body, name=name, grid=(t // tm, n // tn), side=side,
        in_specs=[pl.BlockSpec((tm, D), lambda i, j: (i, 0)), pl.BlockSpec((1, D), lambda i, j: (0, 0)), w_spec],
        out_specs=[pl.BlockSpec((tm, tn), lambda i, j: (i, j)), pl.BlockSpec((tm, D), lambda i, j: (i, 0))],
        out_shape=[S((t, n), BF16), S((t, D), BF16)], scratch_shapes=[],
        operands=[h, gain, w])


def _proj_bwd(dproj, w, dh, h, gain, tk, name, side=None):
    t, n = dproj.shape
    tm = _row_tile(t, 704)
    nk = n // tk
    kb = 1 if w.ndim == 2 else tk // w.shape[2]
    w_spec = (pl.BlockSpec((D, tk), lambda i, k: (0, k)) if w.ndim == 2
              else pl.BlockSpec((kb, D, tk // kb), lambda i, k: (k, 0, 0)))

    def body(dp_ref, w_ref, dh_ref, h_ref, g_ref, dhi_ref, dgain_ref, acc):
        i, k = pl.program_id(0), pl.program_id(1)

        @pl.when(k == 0)
        def _():
            acc[...] = jnp.zeros_like(acc)

        @pl.when((i == 0) & (k == 0))
        def _():
            dgain_ref[...] = jnp.zeros_like(dgain_ref)

        if w.ndim == 2:
            acc[...] += _dot_nt(dp_ref[...], w_ref[...])
        else:
            for b in range(kb):
                acc[...] += _dot_nt(dp_ref[:, b * (tk // kb):(b + 1) * (tk // kb)], w_ref[b])

        @pl.when(k == nk - 1)
        def _():
            dx, dgn = _rms_bwd(acc[...], h_ref[...], g_ref[...])
            dhi_ref[...] = dh_ref[...] + dx
            dgain_ref[0:1, :] += dgn

    row = pl.BlockSpec((tm, D), lambda i, k: (i, 0))
    return _call(
        body, name=name, grid=(t // tm, nk), side=side,
        in_specs=[pl.BlockSpec((tm, tk), lambda i, k: (i, k)), w_spec,
                  row, row, pl.BlockSpec((1, D), lambda i, k: (0, 0))],
        out_specs=[row, pl.BlockSpec((8, D), lambda i, k: (0, 0))],
        out_shape=[S((t, D), F32), S((8, D), F32)],
        scratch_shapes=[pltpu.VMEM((tm, D), F32)],
        operands=[dproj, w, dh, h, gain])


def _post_fwd(o, proj, hgain, wout, h, nh, dv, name, side=None):
    t = h.shape[0]
    w = nh * dv
    tm = _row_tile(t, 704)

    def body(o_ref, g_ref, hg_ref, wo_ref, h_ref, hn_ref, og_ref):
        for hd in range(nh):
            sl = slice(hd * dv, (hd + 1) * dv)
            oh = o_ref[:, sl].astype(F32)
            r = lax.rsqrt(jnp.mean(oh * oh, axis=-1, keepdims=True) + EPS)
            gg = g_ref[:, sl].astype(F32)
            og_ref[:, sl] = (oh * r * hg_ref[:, sl] * (gg * _sigmoid(gg))).astype(BF16)
        hn_ref[...] = h_ref[...] + _dot(og_ref[...], wo_ref[...])

    return _call(
        body, name=name, grid=(t // tm,), side=side,
        in_specs=[pl.BlockSpec((tm, w), lambda i: (i, 0)), pl.BlockSpec((tm, w), lambda i: (i, 2)),
                  pl.BlockSpec((1, w), lambda i: (0, 0)), pl.BlockSpec((w, D), lambda i: (0, 0)),
                  pl.BlockSpec((tm, D), lambda i: (i, 0))],
        out_specs=[pl.BlockSpec((tm, D), lambda i: (i, 0)), pl.BlockSpec((tm, w), lambda i: (i, 0))],
        out_shape=[S((t, D), F32), S((t, w), BF16)], scratch_shapes=[],
        operands=[o, proj, hgain, wout, h])


def _post_bwd(dh, o, proj, hgain, wout, nh, dv, nproj, name, side=None):
    t = dh.shape[0]
    w = nh * dv
    tm = _row_tile(t, 704)

    def body(dh_ref, o_ref, g_ref, hg_ref, wo_ref, do_ref, dg_ref, dhb_ref, dhg_ref):
        @pl.when(pl.program_id(0) == 0)
        def _():
            dhg_ref[...] = jnp.zeros_like(dhg_ref)

        dmix = dh_ref[...].astype(BF16)
        dhb_ref[...] = dmix
        dog = _dot_nt(dmix, wo_ref[...])
        for hd in range(nh):
            sl = slice(hd * dv, (hd + 1) * dv)
            oh = o_ref[:, sl].astype(F32)
            r = lax.rsqrt(jnp.mean(oh * oh, axis=-1, keepdims=True) + EPS)
            xh = oh * r
            gain = hg_ref[:, sl]
            gg = g_ref[:, sl].astype(F32)
            s = _sigmoid(gg)
            dogh = dog[:, sl]
            don = dogh * (gg * s)
            dg_ref[:, sl] = (dogh * (xh * gain) * (s * (1.0 + gg * (1.0 - s)))).astype(BF16)
            dxh = don * gain
            do_ref[:, sl] = (r * (dxh - xh * jnp.mean(dxh * xh, axis=-1, keepdims=True))).astype(BF16)
            dhg_ref[0:1, sl] += jnp.sum(don * xh, axis=0, keepdims=True)

    return _call(
        body, name=name, grid=(t // tm,), side=side,
        in_specs=[pl.BlockSpec((tm, D), lambda i: (i, 0)), pl.BlockSpec((tm, w), lambda i: (i, 0)),
                  pl.BlockSpec((tm, w), lambda i: (i, 2)), pl.BlockSpec((1, w), lambda i: (0, 0)),
                  pl.BlockSpec((w, D), lambda i: (0, 0))],
        out_specs=[pl.BlockSpec((tm, w), lambda i: (i, 0)), pl.BlockSpec((tm, w), lambda i: (i, 2)),
                   pl.BlockSpec((tm, D), lambda i: (i, 0)), pl.BlockSpec((8, w), lambda i: (0, 0))],
        out_shape=[S((t, w), BF16), S((t, nproj), BF16), S((t, D), BF16), S((8, w), F32)], scratch_shapes=[],
        operands=[dh, o, proj, hgain, wout])


def _ret_consts():
    lg = np.log1p(-np.exp2(-5.0 - np.arange(RET_H, dtype=np.float32))).astype(np.float32)
    return jnp.asarray(np.broadcast_to(lg[:, None, None], (RET_H, 1, 128)).copy())


def _rope_tables(t):
    half = RET_DK // 2
    inv = 1.0 / (ROPE_BASE ** jnp.linspace(0.0, 1.0, half, dtype=F32))
    base = (jnp.arange(t // CHUNK) * CHUNK - PAD).astype(F32)[:, None] * inv[None, :]
    off = jnp.arange(CHUNK).astype(F32)[:, None] * inv[None, :]
    ca, sa = jnp.cos(base)[:, None, :], jnp.sin(base)[:, None, :]
    cb, sb = jnp.cos(off)[None], jnp.sin(off)[None]
    return (ca * cb - sa * sb).reshape(t, half), (sa * cb + ca * sb).reshape(t, half)


def _ret_chunk(blk_ref, cos_ref, sin_ref, lg, h):
    c = RET_C
    half = RET_DK // 2
    oq, ok, ov = h * RET_DK, RET_H * RET_DK + h * RET_DK, 2 * RET_H * RET_DK + h * RET_DV
    cs, sn = cos_ref[...], sin_ref[...]
    q1, q2 = blk_ref[:, oq:oq + half].astype(F32), blk_ref[:, oq + half:oq + RET_DK].astype(F32)
    k1, k2 = blk_ref[:, ok:ok + half].astype(F32), blk_ref[:, ok + half:ok + RET_DK].astype(F32)
    qr = jnp.concatenate([q1 * cs - q2 * sn, q1 * sn + q2 * cs], axis=1)
    kr = jnp.concatenate([k1 * cs - k2 * sn, k1 * sn + k2 * cs], axis=1) * (RET_DK ** -0.5)
    v = blk_ref[:, ov:ov + RET_DV]
    ii = lax.broadcasted_iota(jnp.int32, (c, 1), 0).astype(F32)
    jj = lax.broadcasted_iota(jnp.int32, (1, c), 1).astype(F32)
    rel = ii - jj
    dmat = jnp.where(rel >= 0, jnp.exp(lg * jnp.maximum(rel, 0.0)), 0.0)
    dq = jnp.exp(lg * (ii + 1.0))
    dk = jnp.exp(lg * (c - 1.0 - ii))
    dchunk = jnp.exp(lg * float(c))
    return qr, kr, v, dmat, dq, dk, dchunk


def _ret_scan_fwd(proj, cos, sin, lgam, name, side=None):
    t = proj.shape[0]
    c = RET_C
    nc = t // c

    def body(blk_ref, cos_ref, sin_ref, lg_ref, o_ref, st_ref, state):
        @pl.when(pl.program_id(0) == 0)
        def _():
            state[...] = jnp.zeros_like(state)

        for h in range(RET_H):
            qr, kr, v, dmat, dq, dk, dchunk = _ret_chunk(blk_ref, cos_ref, sin_ref, lg_ref[h, :, 0:1], h)
            sp = state[h]
            st_ref[h] = sp.astype(BF16)
            scores = _dot_nt(qr.astype(BF16), kr.astype(BF16)) * dmat
            o = _dot(scores.astype(BF16), v) + _dot((qr * dq).astype(BF16), sp.astype(BF16))
            o_ref[:, h * RET_DV:(h + 1) * RET_DV] = o.astype(BF16)
            state[h] = sp * dchunk + _dot_tn((kr * dk).astype(BF16), v)

    return _call(
        body, name=name, grid=(nc,), side=side,
        in_specs=[pl.BlockSpec((c, RET_QKV), lambda n: (n, 0)), pl.BlockSpec((c, 128), lambda n: (n, 0)),
                  pl.BlockSpec((c, 128), lambda n: (n, 0)), pl.BlockSpec((RET_H, 1, 128), lambda n: (0, 0, 0))],
        out_specs=[pl.BlockSpec((c, RET_H * RET_DV), lambda n: (n, 0)),
                   pl.BlockSpec((RET_H, None, RET_DK, RET_DV), lambda n: (0, n, 0, 0))],
        out_shape=[S((t, RET_H * RET_DV), BF16), S((RET_H, nc, RET_DK, RET_DV), BF16)],
        scratch_shapes=[pltpu.VMEM((RET_H, RET_DK, RET_DV), F32)],
        operands=[proj, cos, sin, lgam])


def _ret_scan_bwd(proj, cos, sin, lgam, do, states, dproj, name, side=None):
    t = proj.shape[0]
    c = RET_C
    nc = t // c
    half = RET_DK // 2

    def body(blk_ref, cos_ref, sin_ref, lg_ref, do_ref, st_ref, dp_in, dp_ref, dstate):
        n = nc - 1 - pl.program_id(0)

        @pl.when(pl.program_id(0) == 0)
        def _():
            dstate[...] = jnp.zeros_like(dstate)

        cs, sn = cos_ref[...], sin_ref[...]
        rows = n * c + lax.broadcasted_iota(jnp.int32, (c, 1), 0)
        keep = rows >= PAD

        def unrot(d):
            d1, d2 = d[:, :half], d[:, half:]
            return jnp.concatenate([d1 * cs + d2 * sn, d2 * cs - d1 * sn], axis=1)

        for h in range(RET_H):
            qr, kr, v, dmat, dq, dk, dchunk = _ret_chunk(blk_ref, cos_ref, sin_ref, lg_ref[h, :, 0:1], h)
            qb, kb = qr.astype(BF16), kr.astype(BF16)
            dob = do_ref[:, h * RET_DV:(h + 1) * RET_DV]
            sp = st_ref[h]
            ds = dstate[h]
            dsb = ds.astype(BF16)
            p = (_dot_nt(qb, kb) * dmat).astype(BF16)
            dvv = _dot_tn(p, dob) + _dot((kr * dk).astype(BF16), dsb)
            dp = (_dot_nt(dob, v) * dmat).astype(BF16)
            dqr = _dot(dp, kb) + _dot_nt(dob, sp) * dq
            dkr = (_dot_tn(dp, qb) + _dot_nt(v, dsb) * dk) * (RET_DK ** -0.5)
            dstate[h] = ds * dchunk + _dot_tn((qr * dq).astype(BF16), dob)
            oq, ok, ov = h * RET_DK, RET_H * RET_DK + h * RET_DK, 2 * RET_H * RET_DK + h * RET_DV
            dp_ref[:, oq:oq + RET_DK] = jnp.where(keep, unrot(dqr), 0.0).astype(BF16)
            dp_ref[:, ok:ok + RET_DK] = jnp.where(keep, unrot(dkr), 0.0).astype(BF16)
            dp_ref[:, ov:ov + RET_DV] = jnp.where(keep, dvv, 0.0).astype(BF16)

    return _call(
        body, name=name, grid=(nc,), side=side, aliases={6: 0},
        in_specs=[pl.BlockSpec((c, RET_QKV), lambda n: (nc - 1 - n, 0)), pl.BlockSpec((c, 128), lambda n: (nc - 1 - n, 0)),
                  pl.BlockSpec((c, 128), lambda n: (nc - 1 - n, 0)), pl.BlockSpec((RET_H, 1, 128), lambda n: (0, 0, 0)),
                  pl.BlockSpec((c, RET_H * RET_DV), lambda n: (nc - 1 - n, 0)),
                  pl.BlockSpec((RET_H, None, RET_DK, RET_DV), lambda n: (0, nc - 1 - n, 0, 0)), ANY],
        out_specs=[pl.BlockSpec((c, RET_QKV), lambda n: (nc - 1 - n, 0))],
        out_shape=[S((t, dproj.shape[1]), BF16)],
        scratch_shapes=[pltpu.VMEM((RET_H, RET_DK, RET_DV), F32)],
        operands=[proj, cos, sin, lgam, do, states, dproj])


def _split3(x):
    hi = x.astype(BF16)
    r1 = x - hi.astype(F32)
    mid = r1.astype(BF16)
    lo = (r1 - mid.astype(F32)).astype(BF16)
    return hi, mid, lo


def _gla_chunk(blk_ref, z_ref, wg_ref, bg_ref, n, h, b_ref=None):
    c = CHUNK
    oq, ok, ov = h * GLA_DK, GLA_H * GLA_DK + h * GLA_DK, 2 * GLA_H * GLA_DK + h * GLA_DV
    q = blk_ref[:, oq:oq + GLA_DK].astype(F32) * (GLA_DK ** -0.5)
    k = blk_ref[:, ok:ok + GLA_DK].astype(F32)
    v = blk_ref[:, ov:ov + GLA_DV]
    hs = slice(h * GLA_DK, (h + 1) * GLA_DK)
    u = _dot(z_ref[...], wg_ref[:, hs]) + bg_ref[:, hs]
    rows = n * c + lax.broadcasted_iota(jnp.int32, (c, 1), 0)
    keep = rows >= PAD
    if b_ref is not None:
        return q, k, v, u, b_ref[:, hs], keep
    la = (jnp.minimum(u, 0.0) - jnp.log(1.0 + jnp.exp(-jnp.abs(u)))) * (1.0 / GLA_TAU)
    la = jnp.where(keep, la, 0.0)
    ii = lax.broadcasted_iota(jnp.int32, (c, c), 0)
    jj = lax.broadcasted_iota(jnp.int32, (c, c), 1)
    tril = (ii >= jj).astype(BF16)
    hi, mid, lo = _split3(la)
    b = _dot(tril, hi) + _dot(tril, mid) + _dot(tril, lo)
    return q, k, v, u, b, keep


def _gla_intra(qs, ks, bs, a_ref):
    c = CHUNK
    nh = len(qs)
    col = lax.broadcasted_iota(jnp.int32, (1, c), 1)
    rowi = lax.broadcasted_iota(jnp.int32, (SUB, 1), 0)
    for blk in range(c // SUB):
        r = slice(SUB * blk, SUB * (blk + 1))
        arows = []
        for h in range(nh):
            q, k, b = qs[h], ks[h], bs[h]
            if blk > 0:
                bprev = b[SUB * blk - 1:SUB * blk]
                qe = q[r] * jnp.exp(b[r] - bprev)
                kt = k * jnp.exp(jnp.minimum(bprev - b, 0.0))
                arows.append(jnp.where(col < SUB * blk, _dot_nt(qe.astype(BF16), kt.astype(BF16)), 0.0))
            else:
                arows.append(jnp.zeros((SUB, c), F32))
        half = SUB // 2
        lo = slice(SUB * blk + half, SUB * (blk + 1))
        tops = [a[:half] for a in arows]
        bots = [a[half:] for a in arows]
        for j in range(SUB):
            for h in range(nh):
                bj, kj = bs[h][SUB * blk + j:SUB * blk + j + 1], ks[h][SUB * blk + j:SUB * blk + j + 1]
                if j < half:
                    a = jnp.sum(qs[h][r] * kj * jnp.exp(bs[h][r] - bj), axis=1, keepdims=True)
                    tops[h] = jnp.where(col == SUB * blk + j, a[:half], tops[h])
                    bots[h] = jnp.where(col == SUB * blk + j, a[half:], bots[h])
                else:
                    a = jnp.sum(qs[h][lo] * kj * jnp.exp(bs[h][lo] - bj), axis=1, keepdims=True)
                    bots[h] = jnp.where(col == SUB * blk + j, a, bots[h])
        for h in range(nh):
            arow = jnp.concatenate([tops[h], bots[h]], axis=0)
            a_ref[h, r, :] = jnp.where(col - SUB * blk <= rowi, arow, 0.0)


def _gla_scan_fwd(proj, wgp, bg, name, side=None):
    t = proj.shape[0]
    c = CHUNK
    nc = t // c
    heads = range(GLA_H)

    def body(blk_ref, z_ref, wg_ref, bg_ref, o_ref, st_ref, am_ref, bs_ref, state, a_ref):
        n = pl.program_id(0)

        @pl.when(n == 0)
        def _():
            state[...] = jnp.zeros_like(state)

        qs, ks, vs, us, bs, keeps = zip(*[_gla_chunk(blk_ref, z_ref, wg_ref, bg_ref, n, h) for h in heads])
        _gla_intra(qs, ks, bs, a_ref)
        for h in heads:
            q, k, v, b = qs[h], ks[h], vs[h], bs[h]
            sp = state[h]
            st_ref[h] = sp.astype(BF16)
            ab = a_ref[h].astype(BF16)
            am_ref[:, h * c:(h + 1) * c] = ab
            bs_ref[:, h * GLA_DK:(h + 1) * GLA_DK] = b
            o = _dot(ab, v) + _dot_nt((q * jnp.exp(b)).astype(BF16), sp.astype(BF16))
            o_ref[:, h * GLA_DV:(h + 1) * GLA_DV] = o.astype(BF16)
            bc = b[c - 1:c]
            state[h] = sp * jnp.exp(bc) + _dot_tn(v, (k * jnp.exp(bc - b)).astype(BF16))

    return _call(
        body, name=name, grid=(nc,), side=side,
        in_specs=[pl.BlockSpec((c, GLA_QKV), lambda n: (n, 0)), pl.BlockSpec((c, 128), lambda n: (n, GLA_ZBLK)),
                  pl.BlockSpec((128, GLA_H * GLA_DK), lambda n: (0, 0)), pl.BlockSpec((1, GLA_H * GLA_DK), lambda n: (0, 0))],
        out_specs=[pl.BlockSpec((c, GLA_H * GLA_DV), lambda n: (n, 0)),
                   pl.BlockSpec((GLA_H, None, GLA_DV, GLA_DK), lambda n: (0, n, 0, 0)),
                   pl.BlockSpec((c, GLA_H * c), lambda n: (n, 0)),
                   pl.BlockSpec((c, GLA_H * GLA_DK), lambda n: (n, 0))],
        out_shape=[S((t, GLA_H * GLA_DV), BF16), S((GLA_H, nc, GLA_DV, GLA_DK), BF16), S((t, GLA_H * c), BF16),
                   S((t, GLA_H * GLA_DK), F32)],
        scratch_shapes=[pltpu.VMEM((GLA_H, GLA_DV, GLA_DK), F32), pltpu.VMEM((GLA_H, c, c), F32)],
        operands=[proj, proj, wgp, bg])


def _gla_scan_bwd(proj, wgp, bg, do, states, amat, bcum, dproj, name, side=None):
    t = proj.shape[0]
    c = CHUNK
    nc = t // c
    heads = range(GLA_H)

    def body(blk_ref, z_ref, wg_ref, bg_ref, do_ref, st_ref, am_ref, bs_ref, dp_in, dp_ref, du_ref, dstate, dq_ref, dkd_ref):
        n = nc - 1 - pl.program_id(0)

        @pl.when(pl.program_id(0) == 0)
        def _():
            dstate[...] = jnp.zeros_like(dstate)

        qs, ks, vs, us, bs, keeps = zip(*[_gla_chunk(blk_ref, z_ref, wg_ref, bg_ref, n, h, bs_ref) for h in heads])
        ii = lax.broadcasted_iota(jnp.int32, (c, c), 0)
        jj = lax.broadcasted_iota(jnp.int32, (c, c), 1)
        col = lax.broadcasted_iota(jnp.int32, (1, c), 1)
        rowi = lax.broadcasted_iota(jnp.int32, (SUB, 1), 0)
        rowc = lax.broadcasted_iota(jnp.int32, (c, 1), 0)
        das, dvs, dq_inters, dk_states, extras, dks = [], [], [], [], [], []
        for h in heads:
            q, k, v, b = qs[h], ks[h], vs[h], bs[h]
            ab = am_ref[:, h * c:(h + 1) * c]
            dob = do_ref[:, h * GLA_DV:(h + 1) * GLA_DV]
            sp = st_ref[h]
            ds = dstate[h]
            dsb = ds.astype(BF16)
            bc = b[c - 1:c]
            eb = jnp.exp(b)
            ebc = jnp.exp(bc - b)
            ec = jnp.exp(bc)
            qb = (q * eb).astype(BF16)
            kb = (k * ebc).astype(BF16)
            dvs.append(_dot_tn(ab, dob) + _dot_nt(kb, dsb))
            das.append(jnp.where(ii >= jj, _dot_nt(dob, v), 0.0))
            dq_inters.append(_dot(dob, sp) * eb)
            dk_state = _dot(v, dsb) * ebc
            dk_states.append(dk_state)
            extras.append(jnp.sum(k * dk_state, axis=0, keepdims=True)
                          + ec * jnp.sum(sp.astype(F32) * ds, axis=0, keepdims=True))
            dstate[h] = ds * ec + _dot_tn(dob, qb)
            dks.append(jnp.zeros((c, GLA_DK), F32))

        for blk in range(c // SUB):
            r = slice(SUB * blk, SUB * (blk + 1))
            dq_is, dkds = [], []
            for h in heads:
                q, k, b = qs[h], ks[h], bs[h]
                if blk > 0:
                    bprev = b[SUB * blk - 1:SUB * blk]
                    e_i = jnp.exp(b[r] - bprev)
                    ek = jnp.exp(jnp.minimum(bprev - b, 0.0))
                    daoff = jnp.where(col < SUB * blk, das[h][r], 0.0).astype(BF16)
                    dq_is.append(_dot(daoff, (k * ek).astype(BF16)) * e_i)
                    dks[h] = dks[h] + _dot_tn(daoff, (q[r] * e_i).astype(BF16)) * ek
                else:
                    dq_is.append(jnp.zeros((SUB, GLA_DK), F32))
                dkds.append(jnp.zeros((SUB, GLA_DK), F32))
            half = SUB // 2
            lo = slice(SUB * blk + half, SUB * (blk + 1))
            row8 = rowi[:half]
            dq_tops = [a[:half] for a in dq_is]
            dq_bots = [a[half:] for a in dq_is]
            for j in range(SUB):
                for h in heads:
                    bj, kj = bs[h][SUB * blk + j:SUB * blk + j + 1], ks[h][SUB * blk + j:SUB * blk + j + 1]
                    if j < half:
                        e = jnp.where(rowi >= j, jnp.exp(bs[h][r] - bj), 0.0)
                        dacol = jnp.sum(jnp.where(col == SUB * blk + j, das[h][r], 0.0), axis=1, keepdims=True)
                        tt = dacol * e
                        dq_tops[h] = dq_tops[h] + tt[:half] * kj
                        dq_bots[h] = dq_bots[h] + tt[half:] * kj
                        dkrow = jnp.sum(tt * qs[h][r], axis=0, keepdims=True)
                    else:
                        e = jnp.where(row8 + half >= j, jnp.exp(bs[h][lo] - bj), 0.0)
                        dacol = jnp.sum(jnp.where(col == SUB * blk + j, das[h][lo], 0.0), axis=1, keepdims=True)
                        tt = dacol * e
                        dq_bots[h] = dq_bots[h] + tt * kj
                        dkrow = jnp.sum(tt * qs[h][lo], axis=0, keepdims=True)
                    dkds[h] = jnp.where(rowi == j, dkrow, dkds[h])
            for h in heads:
                dq_ref[h, r, :] = jnp.concatenate([dq_tops[h], dq_bots[h]], axis=0)
                dkd_ref[h, r, :] = dkds[h]

        for h in heads:
            q, k, b, u, keep = qs[h], ks[h], bs[h], us[h], keeps[h]
            dq = dq_ref[h] + dq_inters[h]
            dk = dks[h] + dkd_ref[h] + dk_states[h]
            db = q * dq - k * dk + jnp.where(rowc == c - 1, extras[h], 0.0)
            triu = (ii <= jj).astype(BF16)
            hi, mid, lo = _split3(db)
            dla = _dot(triu, hi) + _dot(triu, mid) + _dot(triu, lo)
            du = jnp.where(keep, dla * (1.0 / GLA_TAU) / (1.0 + jnp.exp(u)), 0.0)
            du_ref[:, h * GLA_DK:(h + 1) * GLA_DK] = du.astype(BF16)
            oq, ok, ov = h * GLA_DK, GLA_H * GLA_DK + h * GLA_DK, 2 * GLA_H * GLA_DK + h * GLA_DV
            dp_ref[:, oq:oq + GLA_DK] = jnp.where(keep, dq * (GLA_DK ** -0.5), 0.0).astype(BF16)
            dp_ref[:, ok:ok + GLA_DK] = jnp.where(keep, dk, 0.0).astype(BF16)
            dp_ref[:, ov:ov + GLA_DV] = jnp.where(keep, dvs[h], 0.0).astype(BF16)

    nproj = dproj.shape[1]
    return _call(
        body, name=name, grid=(nc,), side=side, aliases={8: 0},
        in_specs=[pl.BlockSpec((c, GLA_QKV), lambda n: (nc - 1 - n, 0)), pl.BlockSpec((c, 128), lambda n: (nc - 1 - n, GLA_ZBLK)),
                  pl.BlockSpec((128, GLA_H * GLA_DK), lambda n: (0, 0)), pl.BlockSpec((1, GLA_H * GLA_DK), lambda n: (0, 0)),
                  pl.BlockSpec((c, GLA_H * GLA_DV), lambda n: (nc - 1 - n, 0)),
                  pl.BlockSpec((GLA_H, None, GLA_DV, GLA_DK), lambda n: (0, nc - 1 - n, 0, 0)),
                  pl.BlockSpec((c, GLA_H * c), lambda n: (nc - 1 - n, 0)),
                  pl.BlockSpec((c, GLA_H * GLA_DK), lambda n: (nc - 1 - n, 0)), ANY],
        out_specs=[pl.BlockSpec((c, GLA_QKV), lambda n: (nc - 1 - n, 0)),
                   pl.BlockSpec((c, GLA_H * GLA_DK), lambda n: (nc - 1 - n, 0))],
        out_shape=[S((t, nproj), BF16), S((t, GLA_H * GLA_DK), BF16)],
        scratch_shapes=[pltpu.VMEM((GLA_H, GLA_DV, GLA_DK), F32),
                        pltpu.VMEM((GLA_H, c, GLA_DK), F32), pltpu.VMEM((GLA_H, c, GLA_DK), F32)],
        operands=[proj, proj, wgp, bg, do, states, amat, bcum, dproj])


def _gla_gate_bwd(du, proj, wgp, dproj, name):
    t = du.shape[0]
    tm = _row_tile(t, 704)
    w = GLA_H * GLA_DK

    def body(du_ref, z_ref, wg_ref, dp_in, dp_ref, dwg_ref, dbg_ref):
        @pl.when(pl.program_id(0) == 0)
        def _():
            dwg_ref[...] = jnp.zeros_like(dwg_ref)
            dbg_ref[...] = jnp.zeros_like(dbg_ref)

        d = du_ref[...]
        dp_ref[...] = _dot_nt(d, wg_ref[...]).astype(BF16)
        dwg_ref[...] += _dot_tn(z_ref[...], d)
        dbg_ref[0:1, :] += jnp.sum(d.astype(F32), axis=0, keepdims=True)

    return pl.pallas_call(
        body, name=name, grid=(t // tm,),
        in_specs=[pl.BlockSpec((tm, w), lambda i: (i, 0)), pl.BlockSpec((tm, 128), lambda i: (i, GLA_ZBLK)),
                  pl.BlockSpec((128, w), lambda i: (0, 0)), ANY],
        out_specs=[pl.BlockSpec((tm, 128), lambda i: (i, GLA_ZBLK)), pl.BlockSpec((128, w), lambda i: (0, 0)),
                   pl.BlockSpec((8, w), lambda i: (0, 0))],
        out_shape=[S(dproj.shape, BF16), S((128, w), F32), S((8, w), F32)],
        input_output_aliases={3: 0},
        compiler_params=_cp(dimension_semantics=("arbitrary",)),
    )(du, proj, wgp, dproj)


def _final_loss(h, gain, target, name):
    t = h.shape[0]
    tm = _row_tile(t, 704)

    def body(h_ref, g_ref, t_ref, dh_ref, dgain_ref, loss_ref):
        i = pl.program_id(0)

        @pl.when(i == 0)
        def _():
            dgain_ref[...] = jnp.zeros_like(dgain_ref)
            loss_ref[...] = jnp.zeros_like(loss_ref)

        x = h_ref[...]
        gain = g_ref[...]
        r = lax.rsqrt(jnp.mean(x * x, axis=-1, keepdims=True) + EPS)
        xh = x * r
        rows = i * tm + lax.broadcasted_iota(jnp.int32, (tm, 1), 0)
        e = jnp.where(rows >= CHUNK, xh * gain - t_ref[...], 0.0)
        loss_ref[...] += 0.5 * jnp.sum(jnp.mean(e * e, axis=-1, keepdims=True), axis=0, keepdims=True)
        dy = e * (1.0 / D)
        dgain_ref[0:1, :] += jnp.sum(dy * xh, axis=0, keepdims=True)
        dxh = dy * gain
        dh_ref[...] = r * (dxh - xh * jnp.mean(dxh * xh, axis=-1, keepdims=True))

    row = pl.BlockSpec((tm, D), lambda i: (i, 0))
    return pl.pallas_call(
        body, name=name, grid=(t // tm,),
        in_specs=[row, pl.BlockSpec((1, D), lambda i: (0, 0)), row],
        out_specs=[row, pl.BlockSpec((8, D), lambda i: (0, 0)), pl.BlockSpec((8, 128), lambda i: (0, 0))],
        out_shape=[S((t, D), F32), S((8, D), F32), S((8, 128), F32)],
        compiler_params=_cp(dimension_semantics=("arbitrary",)),
    )(h, gain, target)


def _adam_math(w, g, m, v):
    m2 = ADAM_B1 * m + (1.0 - ADAM_B1) * g
    v2 = ADAM_B2 * v + (1.0 - ADAM_B2) * (g * g)
    m_hat = m2 / (1.0 - ADAM_B1 ** ADAM_STEP)
    v_hat = v2 / (1.0 - ADAM_B2 ** ADAM_STEP)
    delta = -ADAM_LR * (m_hat / (jnp.sqrt(v_hat) + ADAM_EPS) + ADAM_WD * w)
    return delta, m2, v2


def _adamw_reduce(recvs, w, m, v, name):
    nl, r, wd = w.shape
    tr = _row_tile(r, 256) if r % 16 == 0 else r
    nr = r // tr

    def body(*refs):
        rv_refs = refs[:nl]
        w_ref, m_ref, v_ref, g_ref, d_ref, m2_ref, v2_ref = refs[nl:]
        layer = pl.program_id(0)

        def total(rv_ref):
            g = rv_ref[0].astype(F32)
            for s in range(1, N_DEV):
                g = g + rv_ref[s].astype(F32)
            return g

        g = total(rv_refs[0])
        for k in range(1, nl):
            g = jnp.where(layer == k, total(rv_refs[k]), g)
        g_ref[...] = g
        d_ref[...], m2_ref[...], v2_ref[...] = _adam_math(w_ref[...], g, m_ref[...], v_ref[...])

    def rv_spec(k):
        return pl.BlockSpec((N_DEV, tr, wd), lambda l, i: (0, jnp.where(l == k, i, jnp.where(l < k, 0, nr - 1)), 0))

    row = pl.BlockSpec((None, tr, wd), lambda l, i: (l, i, 0))
    return pl.pallas_call(
        body, name=name, grid=(nl, nr),
        in_specs=[rv_spec(k) for k in range(nl)] + [row, row, row],
        out_specs=[row] * 4, out_shape=[S((nl, r, wd), F32)] * 4,
        compiler_params=_cp(dimension_semantics=("arbitrary", "arbitrary")),
    )(*recvs, w, m, v)


def _small_reduce(parts, name):
    _, r, wd = parts.shape

    def body(p_ref, o_ref):
        g = p_ref[0]
        for s in range(1, N_DEV):
            g = g + p_ref[s]
        o_ref[...] = g

    return pl.pallas_call(body, name=name, out_shape=S((r, wd), F32), compiler_params=_cp())(parts)


def _adamw_small(w, g, m, v, name):
    def body(w_ref, g_ref, m_ref, v_ref, d_ref, m2_ref, v2_ref):
        d_ref[...], m2_ref[...], v2_ref[...] = _adam_math(w_ref[...], g_ref[...], m_ref[...], v_ref[...])

    return pl.pallas_call(body, name=name, out_shape=[S(w.shape, F32)] * 3, compiler_params=_cp())(w, g, m, v)


def _unshard_cols(g):
    return jnp.transpose(g, (1, 0, 2)).reshape(g.shape[1], N_DEV * g.shape[2])


def _my_cols(full, width):
    me = 4 * lax.axis_index("x") + 2 * lax.axis_index("y") + lax.axis_index("c")
    return lax.dynamic_slice_in_dim(full, me * width, width, axis=1)


def kernel(x, meta_tokens, norm_ffn1, ffn1_w_in, ffn1_w_out, norm_mix, norm_ffn2, ffn2_w_in, ffn2_w_out, ret_w_in, ret_head_norm, ret_w_out, gla_w_in, gla_w_gate, gla_b_gate, gla_head_norm, gla_w_out, final_norm, loss_target, m_meta_tokens, m_norm_ffn1, m_ffn1_w_in, m_ffn1_w_out, m_norm_mix, m_norm_ffn2, m_ffn2_w_in, m_ffn2_w_out, m_ret_w_in, m_ret_head_norm, m_ret_w_out, m_gla_w_in, m_gla_w_gate, m_gla_b_gate, m_gla_head_norm, m_gla_w_out, m_final_norm, v_meta_tokens, v_norm_ffn1, v_ffn1_w_in, v_ffn1_w_out, v_norm_mix, v_norm_ffn2, v_ffn2_w_in, v_ffn2_w_out, v_ret_w_in, v_ret_head_norm, v_ret_w_out, v_gla_w_in, v_gla_w_gate, v_gla_b_gate, v_gla_head_norm, v_gla_w_out, v_final_norm):
    seq = x.shape[1]
    t = seq + CHUNK
    xs = x[0]
    target = loss_target[0]

    def ffn_w(f):
        w_in, w_out = (ffn1_w_in, ffn1_w_out) if f < 2 else (ffn2_w_in, ffn2_w_out)
        return [w_in[f % 2].astype(BF16), w_out[f % 2].astype(BF16)]

    small = jnp.concatenate([meta_tokens.reshape(-1), ret_head_norm.reshape(-1), gla_w_gate.reshape(-1),
                             gla_b_gate.reshape(-1), gla_head_norm.reshape(-1)])
    n_small = small.shape[0]
    small = jnp.pad(small, (0, 32 * 128 - n_small)).reshape(32, 128)
    sg, win0, wout0 = _run_side(_Gather([small] + ffn_w(0)), "ag_first")
    sg = sg.reshape(N_DEV, 32 * 128)

    def small_cols(off, rows, width):
        return jnp.transpose(sg[:, off:off + rows * width].reshape(N_DEV, rows, width), (1, 0, 2)).reshape(rows, N_DEV * width)

    off = 0
    meta_full = small_cols(off, N_META, D // N_DEV); off += N_META * (D // N_DEV)
    ret_hn = small_cols(off, RET_H, RET_DV // N_DEV).reshape(1, RET_H * RET_DV); off += RET_H * RET_DV // N_DEV
    wgate = small_cols(off, GLA_RANK, GLA_H * GLA_DK // N_DEV); off += GLA_RANK * GLA_H * GLA_DK // N_DEV
    bgate = small_cols(off, 1, GLA_H * GLA_DK // N_DEV); off += GLA_H * GLA_DK // N_DEV
    gla_hn = small_cols(off, GLA_H, GLA_DV // N_DEV).reshape(1, GLA_H * GLA_DV)
    wgp = jnp.pad(wgate, ((0, 128 - GLA_RANK), (0, 0))).astype(BF16)

    cos, sin = _rope_tables(t)
    lgam = _ret_consts()

    h0 = jnp.concatenate([jnp.zeros((PAD, D), F32), meta_full, xs], axis=0)
    g1 = [norm_ffn1[i:i + 1] for i in range(2)]
    gm = [norm_mix[i:i + 1] for i in range(2)]
    g2 = [norm_ffn2[i:i + 1] for i in range(2)]

    (h1, xn_a0, pg_a0, pu_a0), (ret_win_g, ret_wout_g) = _ffn_fwd(
        h0, g1[0], win0, wout0, "ffn1_l0_fwd", side=_Gather([ret_w_in[0].astype(BF16), ret_w_out[0].astype(BF16)]))
    ret_win = ret_win_g
    ret_wout = ret_wout_g.reshape(RET_H * RET_DV, D)
    (rproj, rhn), (win2,) = _norm_mm(h1, gm[0], ret_win, 4 * ret_win.shape[2], "ret_proj_fwd", side=_Gather(ffn_w(2)[:1]))
    (ro, rstates), (wout2,) = _ret_scan_fwd(rproj, cos, sin, lgam, "ret_scan_fwd", side=_Gather(ffn_w(2)[1:]))
    (h2, rog), _ = _post_fwd(ro, rproj, ret_hn, ret_wout, h1, RET_H, RET_DV, "ret_post_fwd")
    (h3, xn_b0, pg_b0, pu_b0), (win1, wout1) = _ffn_fwd(h2, g2[0], win2, wout2, "ffn2_l0_fwd", side=_Gather(ffn_w(1)))
    (h4, xn_a1, pg_a1, pu_a1), (gla_win_g, gla_wout_g) = _ffn_fwd(
        h3, g1[1], win1, wout1, "ffn1_l1_fwd", side=_Gather([gla_w_in[0].astype(BF16), gla_w_out[0].astype(BF16)]))
    gla_win = _unshard_cols(gla_win_g)
    gla_win = jnp.pad(gla_win, ((0, 0), (0, GLA_N - gla_win.shape[1])))
    gla_wout = gla_wout_g.reshape(GLA_H * GLA_DV, D)
    (gproj, ghn), _ = _norm_mm(h4, gm[1], gla_win, GLA_N, "gla_proj_fwd")
    (go, gstates, gamat, gbcum), (win3, wout3) = _gla_scan_fwd(gproj, wgp, bgate, "gla_scan_fwd", side=_Gather(ffn_w(3)))
    (h5, gog), _ = _post_fwd(go, gproj, gla_hn, gla_wout, h4, GLA_H, GLA_DV, "gla_post_fwd")
    (h6, xn_b1, pg_b1, pu_b1), _ = _ffn_fwd(h5, g2[1], win3, wout3, "ffn2_l1_fwd")

    dh, dfinal, loss_blk = _final_loss(h6, final_norm.reshape(1, D), jnp.pad(target, ((CHUNK, 0), (0, 0))), "final_loss")
    loss = lax.psum(loss_blk[0, 0], ("x", "y", "c"))

    def ffn_back(dh, h_in, xn, gain, pg, pu, win, wout, tag, side=None, dw_side=None):
        (dh_in, dob, dpg, dpu, act, dgain), got = _ffn_bwd(dh, h_in, gain, pg, pu, win, wout, tag + "_bwd", side=side)
        dwout = _mm_tn(act, dob[None], D, tag + "_dw_out").reshape(N_DEV, FF_SHARD // 2, D)
        if dw_side == "own_dw_out":
            dw_side = _Exchange([dwout])
        (dwin,), dw_got = _ffn_dw_in(xn, dpg, dpu, tag + "_dw_in", side=dw_side)
        return dh_in, [dwin, dwout], dgain[0], got, dw_got

    dh, dw_b1, dg2_1, _, _ = ffn_back(dh, h5, xn_b1, g2[1], pg_b1, pu_b1, win3, wout3, "ffn2_l1")

    (gdo, gdproj, gdhb, dghn), _ = _post_bwd(dh, go, gproj, gla_hn, gla_wout, GLA_H, GLA_DV, GLA_N, "gla_post_bwd")
    d_gla_wout = _mm_tn(gog[None], gdhb[None], D, "gla_dw_out").reshape(N_DEV, GLA_H * GLA_DV // N_DEV, D)
    (gdproj, gdu), rv_b1_in = _gla_scan_bwd(gproj, wgp, bgate, gdo, gstates, gamat, gbcum, gdproj, "gla_scan_bwd",
                                            side=_Exchange(dw_b1[:1]))
    gdproj, dwg, dbg = _gla_gate_bwd(gdu, gproj, wgp, gdproj, "gla_gate_bwd")
    d_gla_win = _mm_tn(gdproj[None], ghn[None], D, "gla_dw_in", tm=640)[0]
    (dh, dgm_1), rv_b1_out = _proj_bwd(gdproj, gla_win, dh, h4, gm[1], GLA_N, "gla_proj_bwd", side=_Exchange(dw_b1[1:]))
    rv_b1 = rv_b1_in + rv_b1_out
    n_gla_in = 2 * GLA_H * GLA_DK + 2 * GLA_H * GLA_DV + GLA_RANK
    d_gla_win = d_gla_win[:n_gla_in].reshape(N_DEV, n_gla_in // N_DEV, D)

    dh, dw_a1, dg1_1, rv_gla, rv_a1_out = ffn_back(dh, h3, xn_a1, g1[1], pg_a1, pu_a1, win1, wout1, "ffn1_l1",
                                                   side=_Exchange([d_gla_win, d_gla_wout]), dw_side="own_dw_out")
    dh, dw_b0, dg2_0, rv_a1_in, _ = ffn_back(dh, h2, xn_b0, g2[0], pg_b0, pu_b0, win2, wout2, "ffn2_l0", side=_Exchange(dw_a1[:1]))
    rv_a1 = rv_a1_in + rv_a1_out

    (rdo, rdproj, rdhb, drhn), rv_b0_out = _post_bwd(dh, ro, rproj, ret_hn, ret_wout, RET_H, RET_DV, 6 * D, "ret_post_bwd",
                                                     side=_Exchange(dw_b0[1:]))
    d_ret_wout = _mm_tn(rog[None], rdhb[None], D, "ret_dw_out", rows=DW_ROWS // 2).reshape(N_DEV, RET_H * RET_DV // N_DEV, D)
    (rdproj,), rv_b0_in = _ret_scan_bwd(rproj, cos, sin, lgam, rdo, rstates, rdproj, "ret_scan_bwd", side=_Exchange(dw_b0[:1]))
    rv_b0 = rv_b0_in + rv_b0_out
    d_ret_win = _mm_tn(rhn[None], rdproj[None], ret_win.shape[2], "ret_dw_in", shard_out=True)
    (dh, dgm_0), rv_ret_out = _proj_bwd(rdproj, ret_win, dh, h1, gm[0], 4 * ret_win.shape[2], "ret_proj_bwd", side=_Exchange([d_ret_wout]))

    dh, dw_a0, dg1_0, rv_ret_in, rv_a0_out = ffn_back(dh, h0, xn_a0, g1[0], pg_a0, pu_a0, win0, wout0, "ffn1_l0",
                                                      side=_Exchange([d_ret_win]), dw_side="own_dw_out")
    rv_ret = rv_ret_in + rv_ret_out
    grad_x = dh[CHUNK:][None]

    dmeta = dh[PAD:CHUNK]
    parts = jnp.concatenate([
        dg1_0, dg1_1, dgm_0[0], dgm_1[0], dg2_0, dg2_1, dfinal[0], dmeta.reshape(-1), drhn[0], dwg[:GLA_RANK].reshape(-1),
        dbg[0], dghn[0]])
    n_parts = parts.shape[0]
    rows = -(-n_parts // D)
    rows = -(-rows // 8) * 8
    parts = jnp.pad(parts, (0, rows * D - n_parts)).reshape(rows, D)
    rv_a0_in, parts_all = _run_side(_Both(_Exchange(dw_a0[:1]), _Gather([parts])), "xchg_last")
    rv_a0 = [rv_a0_in] + rv_a0_out
    tot = _small_reduce(parts_all, "small_grad_sum").reshape(-1)

    def adam_t(recvs, w, m, v, tag):
        outs = _adamw_reduce(recvs, *(jnp.swapaxes(a, 1, 2) for a in (w, m, v)), tag)
        return [jnp.swapaxes(o, 1, 2) for o in outs]

    u_ffn1_in = adam_t([rv_a0[0], rv_a1[0]], ffn1_w_in, m_ffn1_w_in, v_ffn1_w_in, "adam_ffn1_w_in")
    u_ffn2_in = adam_t([rv_b0[0], rv_b1[0]], ffn2_w_in, m_ffn2_w_in, v_ffn2_w_in, "adam_ffn2_w_in")
    u_ffn1_out = _adamw_reduce([rv_a0[1], rv_a1[1]], ffn1_w_out, m_ffn1_w_out, v_ffn1_w_out, "adam_ffn1_w_out")
    u_ffn2_out = _adamw_reduce([rv_b0[1], rv_b1[1]], ffn2_w_out, m_ffn2_w_out, v_ffn2_w_out, "adam_ffn2_w_out")
    u_ret_in = _adamw_reduce([rv_ret[0]], ret_w_in, m_ret_w_in, v_ret_w_in, "adam_ret_w_in")
    u_ret_out = _adamw_reduce([rv_ret[1]], ret_w_out, m_ret_w_out, v_ret_w_out, "adam_ret_w_out")
    u_gla_in = adam_t([rv_gla[0]], gla_w_in, m_gla_w_in, v_gla_w_in, "adam_gla_w_in")
    u_gla_out = _adamw_reduce([rv_gla[1]], gla_w_out, m_gla_w_out, v_gla_w_out, "adam_gla_w_out")


    off = 0
    def take(nel):
        nonlocal off
        out = tot[off:off + nel]
        off += nel
        return out

    gr_norm_ffn1 = take(2 * D).reshape(2, D)
    gr_norm_mix = take(2 * D).reshape(2, D)
    gr_norm_ffn2 = take(2 * D).reshape(2, D)
    gr_final = take(D)
    gr_meta = _my_cols(take(N_META * D).reshape(N_META, D), D // N_DEV)
    gr_ret_hn = _my_cols(take(RET_H * RET_DV).reshape(RET_H, RET_DV), RET_DV // N_DEV)[None]
    gr_wgate = _my_cols(take(GLA_RANK * GLA_H * GLA_DK).reshape(GLA_RANK, GLA_H * GLA_DK), GLA_H * GLA_DK // N_DEV)[None]
    gr_bgate = _my_cols(take(GLA_H * GLA_DK).reshape(1, GLA_H * GLA_DK), GLA_H * GLA_DK // N_DEV)
    gr_gla_hn = _my_cols(take(GLA_H * GLA_DV).reshape(GLA_H, GLA_DV), GLA_DV // N_DEV)[None]

    small_w = [meta_tokens, norm_ffn1, norm_mix, norm_ffn2, ret_head_norm, gla_w_gate, gla_b_gate, gla_head_norm, final_norm]
    small_g = [gr_meta, gr_norm_ffn1, gr_norm_mix, gr_norm_ffn2, gr_ret_hn, gr_wgate, gr_bgate, gr_gla_hn, gr_final]
    small_m = [m_meta_tokens, m_norm_ffn1, m_norm_mix, m_norm_ffn2, m_ret_head_norm, m_gla_w_gate, m_gla_b_gate, m_gla_head_norm, m_final_norm]
    small_v = [v_meta_tokens, v_norm_ffn1, v_norm_mix, v_norm_ffn2, v_ret_head_norm, v_gla_w_gate, v_gla_b_gate, v_gla_head_norm, v_final_norm]

    def pack(arrs):
        flat = jnp.concatenate([a.reshape(-1) for a in arrs])
        n = flat.shape[0]
        r = -(-n // 128)
        r = -(-r // 8) * 8
        return jnp.pad(flat, (0, r * 128 - n), constant_values=1.0).reshape(r, 128)

    sd, sm, sv = _adamw_small(pack(small_w), pack(small_g), pack(small_m), pack(small_v), "adam_small")

    def unpack(buf):
        flat = buf.reshape(-1)
        outs, o = [], 0
        for a in small_w:
            outs.append(flat[o:o + a.size].reshape(a.shape))
            o += a.size
        return outs

    us_d, us_m, us_v = unpack(sd), unpack(sm), unpack(sv)

    def ordered(k, smalls):
        return (smalls[0], smalls[1], u_ffn1_in[k], u_ffn1_out[k], smalls[2], smalls[3], u_ffn2_in[k], u_ffn2_out[k],
                u_ret_in[k], smalls[4], u_ret_out[k], u_gla_in[k], smalls[5], smalls[6], smalls[7], u_gla_out[k], smalls[8])

    return (loss, grad_x, *ordered(0, small_g), *ordered(1, us_d), *ordered(2, us_m), *ordered(3, us_v))
```

```python
import functools

import numpy as np
import jax
import jax.numpy as jnp
from jax import lax
from jax.experimental import pallas as pl
from jax.experimental.pallas import tpu as pltpu

F32 = jnp.float32
BF16 = jnp.bfloat16
S = jax.ShapeDtypeStruct
ANY = pl.BlockSpec(memory_space=pl.ANY)
MESH = pl.DeviceIdType.MESH

D = 1024
N_META = 16
CHUNK = 64
PAD = CHUNK - N_META
EPS = 1e-6
N_DEV = 8
FF_SHARD = 704
N_FF_CHUNK = 4
RET_H, RET_DK, RET_DV = 4, 256, 512
RET_QKV = RET_H * (2 * RET_DK + RET_DV)
RET_C = 192
GLA_H, GLA_DK, GLA_DV, GLA_RANK, GLA_TAU = 4, 128, 256, 16, 16.0
GLA_QKV = GLA_H * (2 * GLA_DK + GLA_DV)
GLA_N = 3200
GLA_ZBLK = 3072 // 128
SUB = 16
ROPE_BASE = 10000.0
ADAM_LR, ADAM_B1, ADAM_B2, ADAM_EPS, ADAM_WD, ADAM_STEP = 0.001, 0.9, 0.999, 1e-08, 0.01, 10
VMEM_LIMIT = 58 * 1024 * 1024
DW_ROWS = 2752


def _cp(**kw):
    return pltpu.CompilerParams(vmem_limit_bytes=VMEM_LIMIT, **kw)


def _row_tile(t, cap):
    best = 16
    for d in range(16, cap + 1, 16):
        if t % d == 0:
            best = d
    return best


def _sub_rows(tm, parts=2):
    units = tm // 16
    cuts = [16 * (units * p // parts) for p in range(parts + 1)]
    return [slice(a, b) for a, b in zip(cuts[:-1], cuts[1:]) if b > a]


def _dot(a, b):
    return jnp.dot(a, b, preferred_element_type=F32)


def _dot_nt(a, b):
    return lax.dot_general(a, b, (((1,), (1,)), ((), ())), preferred_element_type=F32)


def _dot_tn(a, b):
    return lax.dot_general(a, b, (((0,), (0,)), ((), ())), preferred_element_type=F32)


def _sigmoid(x):
    return pl.reciprocal(1.0 + jnp.exp(-x), approx=True)


def _rms_bwd(dxn, x, gain):
    r = lax.rsqrt(jnp.mean(x * x, axis=-1, keepdims=True) + EPS)
    xh = x * r
    dxh = dxn * gain
    dx = r * (dxh - xh * jnp.mean(dxh * xh, axis=-1, keepdims=True))
    return dx, jnp.sum(dxn * xh, axis=0, keepdims=True)


def _xyc():
    return lax.axis_index("x"), lax.axis_index("y"), lax.axis_index("c")


class _Gather:
    def __init__(self, xs):
        self.xs = list(xs)
        self.n = len(self.xs)

    def out_shape(self):
        return [S((N_DEV,) + a.shape, a.dtype) for a in self.xs]

    def scratch(self):
        return [pltpu.SemaphoreType.DMA((self.n, 7)), pltpu.SemaphoreType.DMA((self.n, 7)), pltpu.SemaphoreType.DMA((self.n,))]

    def phases(self, x_refs, out_refs, send_sems, recv_sems, local_sems):
        x, y, c = _xyc()
        me, sibling = (x, y, c), (x, y, 1 - c)
        chips = [(1 - x, y), (x, 1 - y), (1 - x, 1 - y)]

        def copy(t, k, block, to, src=None):
            px, py, pc = block
            dst = out_refs[t].at[4 * px + 2 * py + pc]
            return pltpu.make_async_remote_copy(
                src_ref=dst if src is None else src, dst_ref=dst,
                send_sem=send_sems.at[t, k], recv_sem=recv_sems.at[t, k], device_id=to, device_id_type=MESH)

        def own(t):
            return pltpu.make_async_copy(x_refs[t], out_refs[t].at[4 * x + 2 * y + c], local_sems.at[t])

        def first(t):
            return [copy(t, 0, me, sibling, src=x_refs[t])] + [
                copy(t, 1 + j, me, (*chip, c), src=x_refs[t]) for j, chip in enumerate(chips)]

        def passed(t):
            return [copy(t, 4 + j, (*chip, c), sibling) for j, chip in enumerate(chips)]

        def start():
            for t in range(self.n):
                own(t).start()
                for cp in first(t):
                    cp.start()

        def mid():
            for t in range(self.n):
                fw = passed(t)
                for j, chip in enumerate(chips):
                    copy(t, 1 + j, (*chip, c), me).wait_recv()
                    fw[j].start()

        def finish():
            for t in range(self.n):
                copy(t, 0, sibling, me).wait_recv()
                for j, chip in enumerate(chips):
                    copy(t, 4 + j, (*chip, 1 - c), me).wait_recv()
                for cp in first(t) + passed(t):
                    cp.wait_send()
                own(t).wait()

        return start, mid, finish


class _Exchange:
    def __init__(self, xs):
        self.xs = list(xs)
        self.n = len(self.xs)

    def out_shape(self):
        return [S(a.shape, a.dtype) for a in self.xs]

    def scratch(self):
        return [pltpu.SemaphoreType.DMA((self.n, 7)), pltpu.SemaphoreType.DMA((self.n, 7)), pltpu.SemaphoreType.DMA((self.n,))]

    def phases(self, g_refs, r_refs, send_sems, recv_sems, local_sems):
        x, y, c = _xyc()
        me = 4 * x + 2 * y + c

        def own(t):
            return pltpu.make_async_copy(g_refs[t].at[me], r_refs[t].at[me], local_sems.at[t])

        def send(t, m):
            px, py, pc = x ^ (m >> 2), y ^ ((m >> 1) & 1), c ^ (m & 1)
            return pltpu.make_async_remote_copy(
                src_ref=g_refs[t].at[4 * px + 2 * py + pc], dst_ref=r_refs[t].at[me],
                send_sem=send_sems.at[t, m - 1], recv_sem=recv_sems.at[t, m - 1],
                device_id=(px, py, pc), device_id_type=MESH)

        def arrival(t, m):
            peer = 4 * (x ^ (m >> 2)) + 2 * (y ^ ((m >> 1) & 1)) + (c ^ (m & 1))
            return pltpu.make_async_remote_copy(
                src_ref=g_refs[t].at[peer], dst_ref=r_refs[t].at[peer],
                send_sem=send_sems.at[t, m - 1], recv_sem=recv_sems.at[t, m - 1],
                device_id=(x, y, c), device_id_type=MESH)

        def start():
            for t in range(self.n):
                own(t).start()
            for m in range(1, N_DEV):
                for t in range(self.n):
                    send(t, m).start()

        def mid():
            pass

        def finish():
            for m in range(1, N_DEV):
                for t in range(self.n):
                    arrival(t, m).wait_recv()
            for m in range(1, N_DEV):
                for t in range(self.n):
                    send(t, m).wait_send()
            for t in range(self.n):
                own(t).wait()

        return start, mid, finish


class _Both:
    def __init__(self, a, b):
        self.a, self.b = a, b
        self.xs = a.xs + b.xs
        self.n = a.n + b.n

    def out_shape(self):
        return self.a.out_shape() + self.b.out_shape()

    def scratch(self):
        return self.a.scratch() + self.b.scratch()

    def phases(self, x_refs, out_refs, *sems):
        na = self.a.n
        pa = self.a.phases(x_refs[:na], out_refs[:na], *sems[:3])
        pb = self.b.phases(x_refs[na:], out_refs[na:], *sems[3:])
        return tuple((lambda f, g: (lambda: (f(), g())))(f, g) for f, g in zip(pa, pb))


def _run_side(side, name):
    n = side.n

    def body(*refs):
        start, mid, finish = side.phases(refs[:n], refs[n:2 * n], *refs[2 * n:])
        start()
        mid()
        finish()

    return list(pl.pallas_call(
        body, name=name, out_shape=side.out_shape(), in_specs=[ANY] * n, out_specs=[ANY] * n,
        scratch_shapes=side.scratch())(*side.xs))


def _grid_steps(grid):
    def ids():
        return [pl.program_id(a) for a in range(len(grid))]

    def first():
        return functools.reduce(jnp.logical_and, [i == 0 for i in ids()])

    def middle():
        i = ids()
        return functools.reduce(jnp.logical_and, [i[0] == (3 * grid[0]) // 4] + [j == 0 for j in i[1:]])

    def last():
        return functools.reduce(jnp.logical_and, [i == g - 1 for i, g in zip(ids(), grid)])

    return first, middle, last


def _call(body, *, name, grid, in_specs, out_specs, out_shape, scratch_shapes, operands, side=None, aliases=None):
    n_in, n_out, n_scr = len(in_specs), len(out_shape), len(scratch_shapes)
    full = body
    if side is not None:
        ns = side.n
        first, middle, last = _grid_steps(grid)

        def full(*refs):
            a = n_in
            ins, sins = refs[:a], refs[a:a + ns]
            a += ns
            outs, souts = refs[a:a + n_out], refs[a + n_out:a + n_out + ns]
            a += n_out + ns
            scr, sems = refs[a:a + n_scr], refs[a + n_scr:]
            start, mid, finish = side.phases(sins, souts, *sems)
            pl.when(first())(start)
            body(*ins, *outs, *scr)
            pl.when(middle())(mid)
            pl.when(last())(finish)

        in_specs = list(in_specs) + [ANY] * ns
        out_specs = list(out_specs) + [ANY] * ns
        out_shape = list(out_shape) + side.out_shape()
        scratch_shapes = list(scratch_shapes) + side.scratch()
        operands = list(operands) + side.xs
    outs = pl.pallas_call(
        full, name=name, grid=grid, in_specs=list(in_specs), out_specs=list(out_specs), out_shape=list(out_shape),
        scratch_shapes=list(scratch_shapes), input_output_aliases=aliases or {},
        compiler_params=_cp(dimension_semantics=("arbitrary",) * len(grid)),
    )(*operands)
    return list(outs[:n_out]), list(outs[n_out:])


def _ffn_fwd(h, gain, win, wout, name, side=None):
    t = h.shape[0]
    tm = _row_tile(t, 704)
    nt = t // tm

    def body(h_ref, g_ref, wg_ref, wu_ref, wo_ref, hn_ref, xn_ref, pg_ref, pu_ref, acc):
        c = pl.program_id(1)

        @pl.when(c == 0)
        def _():
            x = h_ref[...]
            r = lax.rsqrt(jnp.mean(x * x, axis=-1, keepdims=True) + EPS)
            xn_ref[...] = (x * r * g_ref[...]).astype(BF16)
            acc[...] = jnp.zeros_like(acc)

        wo = wo_ref[...].reshape(FF_SHARD, D)
        subs = _sub_rows(tm)
        gus = [(_dot(xn_ref[r, :], wg_ref[...]), _dot(xn_ref[r, :], wu_ref[...])) for r in subs]
        for r, (g, u) in zip(subs, gus):
            pg_ref[r, :] = g.astype(BF16)
            pu_ref[r, :] = u.astype(BF16)
            act = (g * _sigmoid(g) * u).astype(BF16)
            acc[r, :] += _dot(act, wo)

        @pl.when(c == N_FF_CHUNK - 1)
        def _():
            hn_ref[...] = h_ref[...] + 0.5 * acc[...]

    return _call(
        body, name=name, grid=(nt, N_FF_CHUNK), side=side,
        in_specs=[
            pl.BlockSpec((tm, D), lambda i, c: (i, 0)),
            pl.BlockSpec((1, D), lambda i, c: (0, 0)),
            pl.BlockSpec((None, D, FF_SHARD), lambda i, c: (c, 0, 0)),
            pl.BlockSpec((None, D, FF_SHARD), lambda i, c: (c + N_FF_CHUNK, 0, 0)),
            pl.BlockSpec((2, FF_SHARD // 2, D), lambda i, c: (c, 0, 0)),
        ],
        out_specs=[
            pl.BlockSpec((tm, D), lambda i, c: (i, 0)),
            pl.BlockSpec((tm, D), lambda i, c: (i, 0)),
            pl.BlockSpec((None, tm, FF_SHARD), lambda i, c: (c, i, 0)),
            pl.BlockSpec((None, tm, FF_SHARD), lambda i, c: (c, i, 0)),
        ],
        out_shape=[S((t, D), F32), S((t, D), BF16), S((N_FF_CHUNK, t, FF_SHARD), BF16), S((N_FF_CHUNK, t, FF_SHARD), BF16)],
        scratch_shapes=[pltpu.VMEM((tm, D), F32)],
        operands=[h, gain, win, win, wout])


def _ffn_bwd(dh, h, gain, pg, pu, win, wout, name, side=None):
    t = h.shape[0]
    tm = _row_tile(t, 704)
    nt = t // tm

    def body(dh_ref, h_ref, g_ref, pg_ref, pu_ref, wg_ref, wu_ref, wo_ref,
             dhi_ref, dob_ref, dpg_ref, dpu_ref, act_ref, dgain_ref, acc):
        i, c = pl.program_id(0), pl.program_id(1)

        @pl.when(c == 0)
        def _():
            dob_ref[...] = (0.5 * dh_ref[...]).astype(BF16)
            acc[...] = jnp.zeros_like(acc)

        @pl.when((i == 0) & (c == 0))
        def _():
            dgain_ref[...] = jnp.zeros_like(dgain_ref)

        wo = wo_ref[...].reshape(FF_SHARD, D)
        subs = _sub_rows(tm)
        dacts = [_dot_nt(dob_ref[r, :], wo) for r in subs]
        for r, dact in zip(subs, dacts):
            g = pg_ref[r, :].astype(F32)
            u = pu_ref[r, :].astype(F32)
            s = _sigmoid(g)
            sl = g * s
            act_ref[r, :] = (sl * u).astype(BF16)
            dg = (dact * u * (s * (1.0 + g * (1.0 - s)))).astype(BF16)
            du = (dact * sl).astype(BF16)
            dpg_ref[r, :] = dg
            dpu_ref[r, :] = du
            acc[r, :] += _dot_nt(dg, wg_ref[...]) + _dot_nt(du, wu_ref[...])

        @pl.when(c == N_FF_CHUNK - 1)
        def _():
            dx, dgn = _rms_bwd(acc[...], h_ref[...], g_ref[...])
            dhi_ref[...] = dh_ref[...] + dx
            dgain_ref[0:1, :] += dgn

    blk = pl.BlockSpec((None, tm, FF_SHARD), lambda i, c: (c, i, 0))
    row = pl.BlockSpec((tm, D), lambda i, c: (i, 0))
    return _call(
        body, name=name, grid=(nt, N_FF_CHUNK), side=side,
        in_specs=[
            row, row, pl.BlockSpec((1, D), lambda i, c: (0, 0)), blk, blk,
            pl.BlockSpec((None, D, FF_SHARD), lambda i, c: (c, 0, 0)),
            pl.BlockSpec((None, D, FF_SHARD), lambda i, c: (c + N_FF_CHUNK, 0, 0)),
            pl.BlockSpec((2, FF_SHARD // 2, D), lambda i, c: (c, 0, 0)),
        ],
        out_specs=[row, row, blk, blk, blk, pl.BlockSpec((8, D), lambda i, c: (0, 0))],
        out_shape=[S((t, D), F32), S((t, D), BF16)] + [S((N_FF_CHUNK, t, FF_SHARD), BF16)] * 3 + [S((8, D), F32)],
        scratch_shapes=[pltpu.VMEM((tm, D), F32)],
        operands=[dh, h, gain, pg, pu, win, win, wout])


def _ffn_dw_in(xn, dpg, dpu, name, side=None):
    t = xn.shape[0]
    tk = _row_tile(t, DW_ROWS)
    nk = t // tk

    def body(a_ref, bg_ref, bu_ref, o_ref, acc):
        c, k = pl.program_id(0), pl.program_id(1)

        @pl.when(k == 0)
        def _():
            acc[...] = jnp.zeros_like(acc)

        @pl.when(c < N_FF_CHUNK)
        def _():
            acc[...] += _dot_tn(bg_ref[...], a_ref[...])

        @pl.when(c >= N_FF_CHUNK)
        def _():
            acc[...] += _dot_tn(bu_ref[...], a_ref[...])

        @pl.when(k == nk - 1)
        def _():
            o_ref[...] = acc[...].astype(BF16)

    return _call(
        body, name=name, grid=(2 * N_FF_CHUNK, nk), side=side,
        in_specs=[
            pl.BlockSpec((tk, D), lambda c, k: (k, 0)),
            pl.BlockSpec((None, tk, FF_SHARD), lambda c, k: (jnp.minimum(c, N_FF_CHUNK - 1), k, 0)),
            pl.BlockSpec((None, tk, FF_SHARD), lambda c, k: (jnp.maximum(c - N_FF_CHUNK, 0), k, 0)),
        ],
        out_specs=[pl.BlockSpec((None, FF_SHARD, D), lambda c, k: (c, 0, 0))],
        out_shape=[S((2 * N_FF_CHUNK, FF_SHARD, D), BF16)],
        scratch_shapes=[pltpu.VMEM((FF_SHARD, D), F32)],
        operands=[xn, dpg, dpu])


def _mm_tn(a, b, tn, name, tm=None, rows=DW_ROWS, shard_out=False):
    ca, t, m = a.shape
    cb, _, n = b.shape
    nc = max(ca, cb)
    tm = m if tm is None else tm
    tk = _row_tile(t, rows)
    nk = t // tk

    def body(a_ref, b_ref, o_ref, acc):
        k = pl.program_id(3)

        @pl.when(k == 0)
        def _():
            acc[...] = jnp.zeros_like(acc)

        acc[...] += _dot_tn(a_ref[...], b_ref[...])

        @pl.when(k == nk - 1)
        def _():
            o_ref[...] = acc[...].astype(BF16)

    if shard_out:
        out_spec = pl.BlockSpec((None, tm, tn), lambda c, i, j, k: (j, 0, 0))
        out_shape = S((n // tn, m, tn), BF16)
    else:
        out_spec = pl.BlockSpec((None, tm, tn), lambda c, i, j, k: (c, i, j))
        out_shape = S((nc, m, n), BF16)
    return pl.pallas_call(
        body, name=name, grid=(nc, m // tm, n // tn, nk),
        in_specs=[
            pl.BlockSpec((None, tk, tm), (lambda c, i, j, k: (c, k, i)) if ca > 1 else (lambda c, i, j, k: (0, k, i))),
            pl.BlockSpec((None, tk, tn), (lambda c, i, j, k: (c, k, j)) if cb > 1 else (lambda c, i, j, k: (0, k, j))),
        ],
        out_specs=out_spec, out_shape=out_shape,
        scratch_shapes=[pltpu.VMEM((tm, tn), F32)],
        compiler_params=_cp(dimension_semantics=("arbitrary",) * 4),
    )(a, b)


def _norm_mm(h, gain, w, tn, name, side=None):
    t = h.shape[0]
    n = w.shape[-1] if w.ndim == 2 else w.shape[0] * w.shape[2]
    tm = _row_tile(t, 704)
    kb = 1 if w.ndim == 2 else tn // w.shape[2]
    w_spec = (pl.BlockSpec((D, tn), lambda i, j: (0, j)) if w.ndim == 2
              else pl.BlockSpec((kb, D, tn // kb), lambda i, j: (j, 0, 0)))

    def body(h_ref, g_ref, w_ref, o_ref, xn_ref):
        @pl.when(pl.program_id(1) == 0)
        def _():
            x = h_ref[...]
            r = lax.rsqrt(jnp.mean(x * x, axis=-1, keepdims=True) + EPS)
            xn_ref[...] = (x * r * g_ref[...]).astype(BF16)

        if w.ndim == 2:
            o_ref[...] = _dot(xn_ref[...], w_ref[...]).astype(BF16)
        else:
            for b in range(kb):
                o_ref[:, b * (tn // kb):(b + 1) * (tn // kb)] = _dot(xn_ref[...], w_ref[b]).astype(BF16)

    return _call(
        body, name=name, grid=(t // tm, n // tn), side=side,
        in_specs=[pl.BlockSpec((tm, D), lambda i, j: (i, 0)), pl.BlockSpec((1, D), lambda i, j: (0, 0)), w_spec],
        out_specs=[pl.BlockSpec((tm, tn), lambda i, j: (i, j)), pl.BlockSpec((tm, D), lambda i, j: (i, 0))],
        out_shape=[S((t, n), BF16), S((t, D), BF16)], scratch_shapes=[],
        operands=[h, gain, w])


def _proj_bwd(dproj, w, dh, h, gain, tk, name, side=None):
    t, n = dproj.shape
    tm = _row_tile(t, 704)
    nk = n // tk
    kb = 1 if w.ndim == 2 else tk // w.shape[2]
    w_spec = (pl.BlockSpec((D, tk), lambda i, k: (0, k)) if w.ndim == 2
              else pl.BlockSpec((kb, D, tk // kb), lambda i, k: (k, 0, 0)))

    def body(dp_ref, w_ref, dh_ref, h_ref, g_ref, dhi_ref, dgain_ref, acc):
        i, k = pl.program_id(0), pl.program_id(1)

        @pl.when(k == 0)
        def _():
            acc[...] = jnp.zeros_like(acc)

        @pl.when((i == 0) & (k == 0))
        def _():
            dgain_ref[...] = jnp.zeros_like(dgain_ref)

        if w.ndim == 2:
            acc[...] += _dot_nt(dp_ref[...], w_ref[...])
        else:
            for b in range(kb):
                acc[...] += _dot_nt(dp_ref[:, b * (tk // kb):(b + 1) * (tk // kb)], w_ref[b])

        @pl.when(k == nk - 1)
        def _():
            dx, dgn = _rms_bwd(acc[...], h_ref[...], g_ref[...])
            dhi_ref[...] = dh_ref[...] + dx
            dgain_ref[0:1, :] += dgn

    row = pl.BlockSpec((tm, D), lambda i, k: (i, 0))
    return _call(
        body, name=name, grid=(t // tm, nk), side=side,
        in_specs=[pl.BlockSpec((tm, tk), lambda i, k: (i, k)), w_spec,
                  row, row, pl.BlockSpec((1, D), lambda i, k: (0, 0))],
        out_specs=[row, pl.BlockSpec((8, D), lambda i, k: (0, 0))],
        out_shape=[S((t, D), F32), S((8, D), F32)],
        scratch_shapes=[pltpu.VMEM((tm, D), F32)],
        operands=[dproj, w, dh, h, gain])


def _post_fwd(o, proj, hgain, wout, h, nh, dv, name, side=None):
    t = h.shape[0]
    w = nh * dv
    tm = _row_tile(t, 704)

    def body(o_ref, g_ref, hg_ref, wo_ref, h_ref, hn_ref, og_ref):
        for rows in _sub_rows(tm):
            for hd in range(nh):
                sl = slice(hd * dv, (hd + 1) * dv)
                oh = o_ref[rows, sl].astype(F32)
                r = lax.rsqrt(jnp.mean(oh * oh, axis=-1, keepdims=True) + EPS)
                gg = g_ref[rows, sl].astype(F32)
                og_ref[rows, sl] = (oh * r * hg_ref[:, sl] * (gg * _sigmoid(gg))).astype(BF16)
            hn_ref[rows, :] = h_ref[rows, :] + _dot(og_ref[rows, :], wo_ref[...])

    return _call(
        body, name=name, grid=(t // tm,), side=side,
        in_specs=[pl.BlockSpec((tm, w), lambda i: (i, 0)), pl.BlockSpec((tm, w), lambda i: (i, 2)),
                  pl.BlockSpec((1, w), lambda i: (0, 0)), pl.BlockSpec((w, D), lambda i: (0, 0)),
                  pl.BlockSpec((tm, D), lambda i: (i, 0))],
        out_specs=[pl.BlockSpec((tm, D), lambda i: (i, 0)), pl.BlockSpec((tm, w), lambda i: (i, 0))],
        out_shape=[S((t, D), F32), S((t, w), BF16)], scratch_shapes=[],
        operands=[o, proj, hgain, wout, h])


def _post_bwd(dh, o, proj, hgain, wout, nh, dv, nproj, name, side=None):
    t = dh.shape[0]
    w = nh * dv
    tm = _row_tile(t, 704)

    def body(dh_ref, o_ref, g_ref, hg_ref, wo_ref, do_ref, dg_ref, dhb_ref, dhg_ref):
        @pl.when(pl.program_id(0) == 0)
        def _():
            dhg_ref[...] = jnp.zeros_like(dhg_ref)

        dhb_ref[...] = dh_ref[...].astype(BF16)
        subs = _sub_rows(tm)
        dogs = [_dot_nt(dhb_ref[rows, :], wo_ref[...]) for rows in subs]
        for rows, dog in zip(subs, dogs):
            for hd in range(nh):
                sl = slice(hd * dv, (hd + 1) * dv)
                oh = o_ref[rows, sl].astype(F32)
                r = lax.rsqrt(jnp.mean(oh * oh, axis=-1, keepdims=True) + EPS)
                xh = oh * r
                gain = hg_ref[:, sl]
                gg = g_ref[rows, sl].astype(F32)
                s = _sigmoid(gg)
                dogh = dog[:, sl]
                don = dogh * (gg * s)
                dg_ref[rows, sl] = (dogh * (xh * gain) * (s * (1.0 + gg * (1.0 - s)))).astype(BF16)
                dxh = don * gain
                do_ref[rows, sl] = (r * (dxh - xh * jnp.mean(dxh * xh, axis=-1, keepdims=True))).astype(BF16)
                dhg_ref[0:1, sl] += jnp.sum(don * xh, axis=0, keepdims=True)

    return _call(
        body, name=name, grid=(t // tm,), side=side,
        in_specs=[pl.BlockSpec((tm, D), lambda i: (i, 0)), pl.BlockSpec((tm, w), lambda i: (i, 0)),
                  pl.BlockSpec((tm, w), lambda i: (i, 2)), pl.BlockSpec((1, w), lambda i: (0, 0)),
                  pl.BlockSpec((w, D), lambda i: (0, 0))],
        out_specs=[pl.BlockSpec((tm, w), lambda i: (i, 0)), pl.BlockSpec((tm, w), lambda i: (i, 2)),
                   pl.BlockSpec((tm, D), lambda i: (i, 0)), pl.BlockSpec((8, w), lambda i: (0, 0))],
        out_shape=[S((t, w), BF16), S((t, nproj), BF16), S((t, D), BF16), S((8, w), F32)], scratch_shapes=[],
        operands=[dh, o, proj, hgain, wout])


def _ret_consts():
    lg = np.log1p(-np.exp2(-5.0 - np.arange(RET_H, dtype=np.float32))).astype(np.float32)
    return jnp.asarray(np.broadcast_to(lg[:, None, None], (RET_H, 1, 128)).copy())


def _rope_tables(t):
    half = RET_DK // 2
    inv = 1.0 / (ROPE_BASE ** jnp.linspace(0.0, 1.0, half, dtype=F32))
    base = (jnp.arange(t // CHUNK) * CHUNK - PAD).astype(F32)[:, None] * inv[None, :]
    off = jnp.arange(CHUNK).astype(F32)[:, None] * inv[None, :]
    ca, sa = jnp.cos(base)[:, None, :], jnp.sin(base)[:, None, :]
    cb, sb = jnp.cos(off)[None], jnp.sin(off)[None]
    return (ca * cb - sa * sb).reshape(t, half), (sa * cb + ca * sb).reshape(t, half)


def _ret_chunk(blk_ref, cos_ref, sin_ref, lg, h):
    c = RET_C
    half = RET_DK // 2
    oq, ok, ov = h * RET_DK, RET_H * RET_DK + h * RET_DK, 2 * RET_H * RET_DK + h * RET_DV
    cs, sn = cos_ref[...], sin_ref[...]
    q1, q2 = blk_ref[:, oq:oq + half].astype(F32), blk_ref[:, oq + half:oq + RET_DK].astype(F32)
    k1, k2 = blk_ref[:, ok:ok + half].astype(F32), blk_ref[:, ok + half:ok + RET_DK].astype(F32)
    qr = jnp.concatenate([q1 * cs - q2 * sn, q1 * sn + q2 * cs], axis=1)
    kr = jnp.concatenate([k1 * cs - k2 * sn, k1 * sn + k2 * cs], axis=1) * (RET_DK ** -0.5)
    v = blk_ref[:, ov:ov + RET_DV]
    ii = lax.broadcasted_iota(jnp.int32, (c, 1), 0).astype(F32)
    jj = lax.broadcasted_iota(jnp.int32, (1, c), 1).astype(F32)
    rel = ii - jj
    dmat = jnp.where(rel >= 0, jnp.exp(lg * jnp.maximum(rel, 0.0)), 0.0)
    dq = jnp.exp(lg * (ii + 1.0))
    dk = jnp.exp(lg * (c - 1.0 - ii))
    dchunk = jnp.exp(lg * float(c))
    return qr, kr, v, dmat, dq, dk, dchunk


def _ret_scan_fwd(proj, cos, sin, lgam, name, side=None):
    t = proj.shape[0]
    c = RET_C
    nc = t // c

    def body(blk_ref, cos_ref, sin_ref, lg_ref, o_ref, st_ref, state):
        @pl.when(pl.program_id(0) == 0)
        def _():
            state[...] = jnp.zeros_like(state)

        for h in range(RET_H):
            qr, kr, v, dmat, dq, dk, dchunk = _ret_chunk(blk_ref, cos_ref, sin_ref, lg_ref[h, :, 0:1], h)
            sp = state[h]
            st_ref[h] = sp.astype(BF16)
            scores = _dot_nt(qr.astype(BF16), kr.astype(BF16)) * dmat
            o = _dot(scores.astype(BF16), v) + _dot((qr * dq).astype(BF16), sp.astype(BF16))
            o_ref[:, h * RET_DV:(h + 1) * RET_DV] = o.astype(BF16)
            state[h] = sp * dchunk + _dot_tn((kr * dk).astype(BF16), v)

    return _call(
        body, name=name, grid=(nc,), side=side,
        in_specs=[pl.BlockSpec((c, RET_QKV), lambda n: (n, 0)), pl.BlockSpec((c, 128), lambda n: (n, 0)),
                  pl.BlockSpec((c, 128), lambda n: (n, 0)), pl.BlockSpec((RET_H, 1, 128), lambda n: (0, 0, 0))],
        out_specs=[pl.BlockSpec((c, RET_H * RET_DV), lambda n: (n, 0)),
                   pl.BlockSpec((RET_H, None, RET_DK, RET_DV), lambda n: (0, n, 0, 0))],
        out_shape=[S((t, RET_H * RET_DV), BF16), S((RET_H, nc, RET_DK, RET_DV), BF16)],
        scratch_shapes=[pltpu.VMEM((RET_H, RET_DK, RET_DV), F32)],
        operands=[proj, cos, sin, lgam])


def _ret_scan_bwd(proj, cos, sin, lgam, do, states, dproj, name, side=None):
    t = proj.shape[0]
    c = RET_C
    nc = t // c
    half = RET_DK // 2

    def body(blk_ref, cos_ref, sin_ref, lg_ref, do_ref, st_ref, dp_in, dp_ref, dstate):
        n = nc - 1 - pl.program_id(0)

        @pl.when(pl.program_id(0) == 0)
        def _():
            dstate[...] = jnp.zeros_like(dstate)

        cs, sn = cos_ref[...], sin_ref[...]
        rows = n * c + lax.broadcasted_iota(jnp.int32, (c, 1), 0)
        keep = rows >= PAD

        def unrot(d):
            d1, d2 = d[:, :half], d[:, half:]
            return jnp.concatenate([d1 * cs + d2 * sn, d2 * cs - d1 * sn], axis=1)

        for h in range(RET_H):
            qr, kr, v, dmat, dq, dk, dchunk = _ret_chunk(blk_ref, cos_ref, sin_ref, lg_ref[h, :, 0:1], h)
            qb, kb = qr.astype(BF16), kr.astype(BF16)
            dob = do_ref[:, h * RET_DV:(h + 1) * RET_DV]
            sp = st_ref[h]
            ds = dstate[h]
            dsb = ds.astype(BF16)
            p = (_dot_nt(qb, kb) * dmat).astype(BF16)
            dvv = _dot_tn(p, dob) + _dot((kr * dk).astype(BF16), dsb)
            dp = (_dot_nt(dob, v) * dmat).astype(BF16)
            dqr = _dot(dp, kb) + _dot_nt(dob, sp) * dq
            dkr = (_dot_tn(dp, qb) + _dot_nt(v, dsb) * dk) * (RET_DK ** -0.5)
            dstate[h] = ds * dchunk + _dot_tn((qr * dq).astype(BF16), dob)
            oq, ok, ov = h * RET_DK, RET_H * RET_DK + h * RET_DK, 2 * RET_H * RET_DK + h * RET_DV
            dp_ref[:, oq:oq + RET_DK] = jnp.where(keep, unrot(dqr), 0.0).astype(BF16)
            dp_ref[:, ok:ok + RET_DK] = jnp.where(keep, unrot(dkr), 0.0).astype(BF16)
            dp_ref[:, ov:ov + RET_DV] = jnp.where(keep, dvv, 0.0).astype(BF16)

    return _call(
        body, name=name, grid=(nc,), side=side, aliases={6: 0},
        in_specs=[pl.BlockSpec((c, RET_QKV), lambda n: (nc - 1 - n, 0)), pl.BlockSpec((c, 128), lambda n: (nc - 1 - n, 0)),
                  pl.BlockSpec((c, 128), lambda n: (nc - 1 - n, 0)), pl.BlockSpec((RET_H, 1, 128), lambda n: (0, 0, 0)),
                  pl.BlockSpec((c, RET_H * RET_DV), lambda n: (nc - 1 - n, 0)),
                  pl.BlockSpec((RET_H, None, RET_DK, RET_DV), lambda n: (0, nc - 1 - n, 0, 0)), ANY],
        out_specs=[pl.BlockSpec((c, RET_QKV), lambda n: (nc - 1 - n, 0))],
        out_shape=[S((t, dproj.shape[1]), BF16)],
        scratch_shapes=[pltpu.VMEM((RET_H, RET_DK, RET_DV), F32)],
        operands=[proj, cos, sin, lgam, do, states, dproj])


def _split3(x):
    hi = x.astype(BF16)
    r1 = x - hi.astype(F32)
    mid = r1.astype(BF16)
    lo = (r1 - mid.astype(F32)).astype(BF16)
    return hi, mid, lo


def _gla_chunk(blk_ref, z_ref, wg_ref, bg_ref, n, h, b_ref=None):
    c = CHUNK
    oq, ok, ov = h * GLA_DK, GLA_H * GLA_DK + h * GLA_DK, 2 * GLA_H * GLA_DK + h * GLA_DV
    q = blk_ref[:, oq:oq + GLA_DK].astype(F32) * (GLA_DK ** -0.5)
    k = blk_ref[:, ok:ok + GLA_DK].astype(F32)
    v = blk_ref[:, ov:ov + GLA_DV]
    hs = slice(h * GLA_DK, (h + 1) * GLA_DK)
    u = _dot(z_ref[...], wg_ref[:, hs]) + bg_ref[:, hs]
    rows = n * c + lax.broadcasted_iota(jnp.int32, (c, 1), 0)
    keep = rows >= PAD
    if b_ref is not None:
        return q, k, v, u, b_ref[:, hs], keep
    la = (jnp.minimum(u, 0.0) - jnp.log(1.0 + jnp.exp(-jnp.abs(u)))) * (1.0 / GLA_TAU)
    la = jnp.where(keep, la, 0.0)
    ii = lax.broadcasted_iota(jnp.int32, (c, c), 0)
    jj = lax.broadcasted_iota(jnp.int32, (c, c), 1)
    tril = (ii >= jj).astype(BF16)
    hi, mid, lo = _split3(la)
    b = _dot(tril, hi) + _dot(tril, mid) + _dot(tril, lo)
    return q, k, v, u, b, keep


def _gla_intra(qs, ks, bs, a_ref):
    c = CHUNK
    nh = len(qs)
    col = lax.broadcasted_iota(jnp.int32, (1, c), 1)
    rowi = lax.broadcasted_iota(jnp.int32, (SUB, 1), 0)
    for blk in range(c // SUB):
        r = slice(SUB * blk, SUB * (blk + 1))
        arows = []
        for h in range(nh):
            q, k, b = qs[h], ks[h], bs[h]
            if blk > 0:
                bprev = b[SUB * blk - 1:SUB * blk]
                qe = q[r] * jnp.exp(b[r] - bprev)
                kt = k * jnp.exp(jnp.minimum(bprev - b, 0.0))
                arows.append(jnp.where(col < SUB * blk, _dot_nt(qe.astype(BF16), kt.astype(BF16)), 0.0))
            else:
                arows.append(jnp.zeros((SUB, c), F32))
        half = SUB // 2
        lo = slice(SUB * blk + half, SUB * (blk + 1))
        tops = [a[:half] for a in arows]
        bots = [a[half:] for a in arows]
        for j in range(SUB):
            for h in range(nh):
                bj, kj = bs[h][SUB * blk + j:SUB * blk + j + 1], ks[h][SUB * blk + j:SUB * blk + j + 1]
                if j < half:
                    a = jnp.sum(qs[h][r] * kj * jnp.exp(bs[h][r] - bj), axis=1, keepdims=True)
                    tops[h] = jnp.where(col == SUB * blk + j, a[:half], tops[h])
                    bots[h] = jnp.where(col == SUB * blk + j, a[half:], bots[h])
                else:
                    a = jnp.sum(qs[h][lo] * kj * jnp.exp(bs[h][lo] - bj), axis=1, keepdims=True)
                    bots[h] = jnp.where(col == SUB * blk + j, a, bots[h])
        for h in range(nh):
            arow = jnp.concatenate([tops[h], bots[h]], axis=0)
            a_ref[h, r, :] = jnp.where(col - SUB * blk <= rowi, arow, 0.0)


def _gla_scan_fwd(proj, wgp, bg, name, side=None):
    t = proj.shape[0]
    c = CHUNK
    nc = t // c
    heads = range(GLA_H)

    def body(blk_ref, z_ref, wg_ref, bg_ref, o_ref, st_ref, am_ref, bs_ref, state, a_ref):
        n = pl.program_id(0)

        @pl.when(n == 0)
        def _():
            state[...] = jnp.zeros_like(state)

        qs, ks, vs, us, bs, keeps = zip(*[_gla_chunk(blk_ref, z_ref, wg_ref, bg_ref, n, h) for h in heads])
        _gla_intra(qs, ks, bs, a_ref)
        for h in heads:
            q, k, v, b = qs[h], ks[h], vs[h], bs[h]
            sp = state[h]
            st_ref[h] = sp.astype(BF16)
            ab = a_ref[h].astype(BF16)
            am_ref[:, h * c:(h + 1) * c] = ab
            bs_ref[:, h * GLA_DK:(h + 1) * GLA_DK] = b
            o = _dot(ab, v) + _dot_nt((q * jnp.exp(b)).astype(BF16), sp.astype(BF16))
            o_ref[:, h * GLA_DV:(h + 1) * GLA_DV] = o.astype(BF16)
            bc = b[c - 1:c]
            state[h] = sp * jnp.exp(bc) + _dot_tn(v, (k * jnp.exp(bc - b)).astype(BF16))

    return _call(
        body, name=name, grid=(nc,), side=side,
        in_specs=[pl.BlockSpec((c, GLA_QKV), lambda n: (n, 0)), pl.BlockSpec((c, 128), lambda n: (n, GLA_ZBLK)),
                  pl.BlockSpec((128, GLA_H * GLA_DK), lambda n: (0, 0)), pl.BlockSpec((1, GLA_H * GLA_DK), lambda n: (0, 0))],
        out_specs=[pl.BlockSpec((c, GLA_H * GLA_DV), lambda n: (n, 0)),
                   pl.BlockSpec((GLA_H, None, GLA_DV, GLA_DK), lambda n: (0, n, 0, 0)),
                   pl.BlockSpec((c, GLA_H * c), lambda n: (n, 0)),
                   pl.BlockSpec((c, GLA_H * GLA_DK), lambda n: (n, 0))],
        out_shape=[S((t, GLA_H * GLA_DV), BF16), S((GLA_H, nc, GLA_DV, GLA_DK), BF16), S((t, GLA_H * c), BF16),
                   S((t, GLA_H * GLA_DK), F32)],
        scratch_shapes=[pltpu.VMEM((GLA_H, GLA_DV, GLA_DK), F32), pltpu.VMEM((GLA_H, c, c), F32)],
        operands=[proj, proj, wgp, bg])


def _gla_scan_bwd(proj, wgp, bg, do, states, amat, bcum, dproj, name):
    t = proj.shape[0]
    c = CHUNK
    nc = t // c
    heads = range(GLA_H)

    def body(blk_ref, z_ref, wg_ref, bg_ref, do_ref, st_ref, am_ref, bs_ref, dp_in, dp_ref, du_ref, dstate, dq_ref, dkd_ref):
        n = nc - 1 - pl.program_id(0)

        @pl.when(pl.program_id(0) == 0)
        def _():
            dstate[...] = jnp.zeros_like(dstate)

        qs, ks, vs, us, bs, keeps = zip(*[_gla_chunk(blk_ref, z_ref, wg_ref, bg_ref, n, h, bs_ref) for h in heads])
        ii = lax.broadcasted_iota(jnp.int32, (c, c), 0)
        jj = lax.broadcasted_iota(jnp.int32, (c, c), 1)
        col = lax.broadcasted_iota(jnp.int32, (1, c), 1)
        rowi = lax.broadcasted_iota(jnp.int32, (SUB, 1), 0)
        rowc = lax.broadcasted_iota(jnp.int32, (c, 1), 0)
        das, dvs, dq_inters, dk_states, extras, dks = [], [], [], [], [], []
        for h in heads:
            q, k, v, b = qs[h], ks[h], vs[h], bs[h]
            ab = am_ref[:, h * c:(h + 1) * c]
            dob = do_ref[:, h * GLA_DV:(h + 1) * GLA_DV]
            sp = st_ref[h]
            ds = dstate[h]
            dsb = ds.astype(BF16)
            bc = b[c - 1:c]
            eb = jnp.exp(b)
            ebc = jnp.exp(bc - b)
            ec = jnp.exp(bc)
            qb = (q * eb).astype(BF16)
            kb = (k * ebc).astype(BF16)
            dvs.append(_dot_tn(ab, dob) + _dot_nt(kb, dsb))
            das.append(jnp.where(ii >= jj, _dot_nt(dob, v), 0.0))
            dq_inters.append(_dot(dob, sp) * eb)
            dk_state = _dot(v, dsb) * ebc
            dk_states.append(dk_state)
            extras.append(jnp.sum(k * dk_state, axis=0, keepdims=True)
                          + ec * jnp.sum(sp.astype(F32) * ds, axis=0, keepdims=True))
            dstate[h] = ds * ec + _dot_tn(dob, qb)
            dks.append(jnp.zeros((c, GLA_DK), F32))

        for blk in range(c // SUB):
            r = slice(SUB * blk, SUB * (blk + 1))
            dq_is, dkds = [], []
            for h in heads:
                q, k, b = qs[h], ks[h], bs[h]
                if blk > 0:
                    bprev = b[SUB * blk - 1:SUB * blk]
                    e_i = jnp.exp(b[r] - bprev)
                    ek = jnp.exp(jnp.minimum(bprev - b, 0.0))
                    daoff = jnp.where(col < SUB * blk, das[h][r], 0.0).astype(BF16)
                    dq_is.append(_dot(daoff, (k * ek).astype(BF16)) * e_i)
                    dks[h] = dks[h] + _dot_tn(daoff, (q[r] * e_i).astype(BF16)) * ek
                else:
                    dq_is.append(jnp.zeros((SUB, GLA_DK), F32))
                dkds.append(jnp.zeros((SUB, GLA_DK), F32))
            half = SUB // 2
            lo = slice(SUB * blk + half, SUB * (blk + 1))
            row8 = rowi[:half]
            dq_tops = [a[:half] for a in dq_is]
            dq_bots = [a[half:] for a in dq_is]
            for j in range(SUB):
                for h in heads:
                    bj, kj = bs[h][SUB * blk + j:SUB * blk + j + 1], ks[h][SUB * blk + j:SUB * blk + j + 1]
                    if j < half:
                        e = jnp.where(rowi >= j, jnp.exp(bs[h][r] - bj), 0.0)
                        dacol = jnp.sum(jnp.where(col == SUB * blk + j, das[h][r], 0.0), axis=1, keepdims=True)
                        tt = dacol * e
                        dq_tops[h] = dq_tops[h] + tt[:half] * kj
                        dq_bots[h] = dq_bots[h] + tt[half:] * kj
                        dkrow = jnp.sum(tt * qs[h][r], axis=0, keepdims=True)
                    else:
                        e = jnp.where(row8 + half >= j, jnp.exp(bs[h][lo] - bj), 0.0)
                        dacol = jnp.sum(jnp.where(col == SUB * blk + j, das[h][lo], 0.0), axis=1, keepdims=True)
                        tt = dacol * e
                        dq_bots[h] = dq_bots[h] + tt * kj
                        dkrow = jnp.sum(tt * qs[h][lo], axis=0, keepdims=True)
                    dkds[h] = jnp.where(rowi == j, dkrow, dkds[h])
            for h in heads:
                dq_ref[h, r, :] = jnp.concatenate([dq_tops[h], dq_bots[h]], axis=0)
                dkd_ref[h, r, :] = dkds[h]

        for h in heads:
            q, k, b, u, keep = qs[h], ks[h], bs[h], us[h], keeps[h]
            dq = dq_ref[h] + dq_inters[h]
            dk = dks[h] + dkd_ref[h] + dk_states[h]
            db = q * dq - k * dk + jnp.where(rowc == c - 1, extras[h], 0.0)
            triu = (ii <= jj).astype(BF16)
            hi, mid, lo = _split3(db)
            dla = _dot(triu, hi) + _dot(triu, mid) + _dot(triu, lo)
            du = jnp.where(keep, dla * (1.0 / GLA_TAU) / (1.0 + jnp.exp(u)), 0.0)
            du_ref[:, h * GLA_DK:(h + 1) * GLA_DK] = du.astype(BF16)
            oq, ok, ov = h * GLA_DK, GLA_H * GLA_DK + h * GLA_DK, 2 * GLA_H * GLA_DK + h * GLA_DV
            dp_ref[:, oq:oq + GLA_DK] = jnp.where(keep, dq * (GLA_DK ** -0.5), 0.0).astype(BF16)
            dp_ref[:, ok:ok + GLA_DK] = jnp.where(keep, dk, 0.0).astype(BF16)
            dp_ref[:, ov:ov + GLA_DV] = jnp.where(keep, dvs[h], 0.0).astype(BF16)

    nproj = dproj.shape[1]
    return pl.pallas_call(
        body, name=name, grid=(nc,),
        in_specs=[pl.BlockSpec((c, GLA_QKV), lambda n: (nc - 1 - n, 0)), pl.BlockSpec((c, 128), lambda n: (nc - 1 - n, GLA_ZBLK)),
                  pl.BlockSpec((128, GLA_H * GLA_DK), lambda n: (0, 0)), pl.BlockSpec((1, GLA_H * GLA_DK), lambda n: (0, 0)),
                  pl.BlockSpec((c, GLA_H * GLA_DV), lambda n: (nc - 1 - n, 0)),
                  pl.BlockSpec((GLA_H, None, GLA_DV, GLA_DK), lambda n: (0, nc - 1 - n, 0, 0)),
                  pl.BlockSpec((c, GLA_H * c), lambda n: (nc - 1 - n, 0)),
                  pl.BlockSpec((c, GLA_H * GLA_DK), lambda n: (nc - 1 - n, 0)), ANY],
        out_specs=[pl.BlockSpec((c, GLA_QKV), lambda n: (nc - 1 - n, 0)),
                   pl.BlockSpec((c, GLA_H * GLA_DK), lambda n: (nc - 1 - n, 0))],
        out_shape=[S((t, nproj), BF16), S((t, GLA_H * GLA_DK), BF16)],
        input_output_aliases={8: 0},
        scratch_shapes=[pltpu.VMEM((GLA_H, GLA_DV, GLA_DK), F32),
                        pltpu.VMEM((GLA_H, c, GLA_DK), F32), pltpu.VMEM((GLA_H, c, GLA_DK), F32)],
        compiler_params=_cp(dimension_semantics=("arbitrary",)),
    )(proj, proj, wgp, bg, do, states, amat, bcum, dproj)


def _gla_gate_bwd(du, proj, wgp, dproj, name):
    t = du.shape[0]
    tm = _row_tile(t, 704)
    w = GLA_H * GLA_DK

    def body(du_ref, z_ref, wg_ref, dp_in, dp_ref, dwg_ref, dbg_ref):
        @pl.when(pl.program_id(0) == 0)
        def _():
            dwg_ref[...] = jnp.zeros_like(dwg_ref)
            dbg_ref[...] = jnp.zeros_like(dbg_ref)

        d = du_ref[...]
        dp_ref[...] = _dot_nt(d, wg_ref[...]).astype(BF16)
        dwg_ref[...] += _dot_tn(z_ref[...], d)
        dbg_ref[0:1, :] += jnp.sum(d.astype(F32), axis=0, keepdims=True)

    return pl.pallas_call(
        body, name=name, grid=(t // tm,),
        in_specs=[pl.BlockSpec((tm, w), lambda i: (i, 0)), pl.BlockSpec((tm, 128), lambda i: (i, GLA_ZBLK)),
                  pl.BlockSpec((128, w), lambda i: (0, 0)), ANY],
        out_specs=[pl.BlockSpec((tm, 128), lambda i: (i, GLA_ZBLK)), pl.BlockSpec((128, w), lambda i: (0, 0)),
                   pl.BlockSpec((8, w), lambda i: (0, 0))],
        out_shape=[S(dproj.shape, BF16), S((128, w), F32), S((8, w), F32)],
        input_output_aliases={3: 0},
        compiler_params=_cp(dimension_semantics=("arbitrary",)),
    )(du, proj, wgp, dproj)


def _final_loss(h, gain, target, name):
    t = h.shape[0]
    tm = _row_tile(t, 704)

    def body(h_ref, g_ref, t_ref, dh_ref, dgain_ref, loss_ref):
        i = pl.program_id(0)

        @pl.when(i == 0)
        def _():
            dgain_ref[...] = jnp.zeros_like(dgain_ref)
            loss_ref[...] = jnp.zeros_like(loss_ref)

        x = h_ref[...]
        gain = g_ref[...]
        r = lax.rsqrt(jnp.mean(x * x, axis=-1, keepdims=True) + EPS)
        xh = x * r
        rows = i * tm + lax.broadcasted_iota(jnp.int32, (tm, 1), 0)
        e = jnp.where(rows >= CHUNK, xh * gain - t_ref[...], 0.0)
        loss_ref[...] += 0.5 * jnp.sum(jnp.mean(e * e, axis=-1, keepdims=True), axis=0, keepdims=True)
        dy = e * (1.0 / D)
        dgain_ref[0:1, :] += jnp.sum(dy * xh, axis=0, keepdims=True)
        dxh = dy * gain
        dh_ref[...] = r * (dxh - xh * jnp.mean(dxh * xh, axis=-1, keepdims=True))

    row = pl.BlockSpec((tm, D), lambda i: (i, 0))
    return pl.pallas_call(
        body, name=name, grid=(t // tm,),
        in_specs=[row, pl.BlockSpec((1, D), lambda i: (0, 0)), row],
        out_specs=[row, pl.BlockSpec((8, D), lambda i: (0, 0)), pl.BlockSpec((8, 128), lambda i: (0, 0))],
        out_shape=[S((t, D), F32), S((8, D), F32), S((8, 128), F32)],
        compiler_params=_cp(dimension_semantics=("arbitrary",)),
    )(h, gain, target)


def _adam_math(w, g, m, v):
    m2 = ADAM_B1 * m + (1.0 - ADAM_B1) * g
    v2 = ADAM_B2 * v + (1.0 - ADAM_B2) * (g * g)
    m_hat = m2 / (1.0 - ADAM_B1 ** ADAM_STEP)
    v_hat = v2 / (1.0 - ADAM_B2 ** ADAM_STEP)
    delta = -ADAM_LR * (m_hat / (jnp.sqrt(v_hat) + ADAM_EPS) + ADAM_WD * w)
    return delta, m2, v2


def _adamw_reduce(recvs, w, m, v, name):
    nl, r, wd = w.shape
    tr = _row_tile(r, 256) if r % 16 == 0 else r
    nr = r // tr

    def body(*refs):
        rv_refs = refs[:nl]
        w_ref, m_ref, v_ref, g_ref, d_ref, m2_ref, v2_ref = refs[nl:]
        layer = pl.program_id(0)

        def total(rv_ref):
            g = rv_ref[0].astype(F32)
            for s in range(1, N_DEV):
                g = g + rv_ref[s].astype(F32)
            return g

        g = total(rv_refs[0])
        for k in range(1, nl):
            g = jnp.where(layer == k, total(rv_refs[k]), g)
        g_ref[...] = g
        d_ref[...], m2_ref[...], v2_ref[...] = _adam_math(w_ref[...], g, m_ref[...], v_ref[...])

    def rv_spec(k):
        return pl.BlockSpec((N_DEV, tr, wd), lambda l, i: (0, jnp.where(l == k, i, jnp.where(l < k, 0, nr - 1)), 0))

    row = pl.BlockSpec((None, tr, wd), lambda l, i: (l, i, 0))
    return pl.pallas_call(
        body, name=name, grid=(nl, nr),
        in_specs=[rv_spec(k) for k in range(nl)] + [row, row, row],
        out_specs=[row] * 4, out_shape=[S((nl, r, wd), F32)] * 4,
        compiler_params=_cp(dimension_semantics=("arbitrary", "arbitrary")),
    )(*recvs, w, m, v)


def _small_reduce(parts, name):
    _, r, wd = parts.shape

    def body(p_ref, o_ref):
        g = p_ref[0]
        for s in range(1, N_DEV):
            g = g + p_ref[s]
        o_ref[...] = g

    return pl.pallas_call(body, name=name, out_shape=S((r, wd), F32), compiler_params=_cp())(parts)


def _adamw_small(w, g, m, v, name):
    def body(w_ref, g_ref, m_ref, v_ref, d_ref, m2_ref, v2_ref):
        d_ref[...], m2_ref[...], v2_ref[...] = _adam_math(w_ref[...], g_ref[...], m_ref[...], v_ref[...])

    return pl.pallas_call(body, name=name, out_shape=[S(w.shape, F32)] * 3, compiler_params=_cp())(w, g, m, v)


def _unshard_cols(g):
    return jnp.transpose(g, (1, 0, 2)).reshape(g.shape[1], N_DEV * g.shape[2])


def _my_cols(full, width):
    me = 4 * lax.axis_index("x") + 2 * lax.axis_index("y") + lax.axis_index("c")
    return lax.dynamic_slice_in_dim(full, me * width, width, axis=1)


def kernel(x, meta_tokens, norm_ffn1, ffn1_w_in, ffn1_w_out, norm_mix, norm_ffn2, ffn2_w_in, ffn2_w_out, ret_w_in, ret_head_norm, ret_w_out, gla_w_in, gla_w_gate, gla_b_gate, gla_head_norm, gla_w_out, final_norm, loss_target, m_meta_tokens, m_norm_ffn1, m_ffn1_w_in, m_ffn1_w_out, m_norm_mix, m_norm_ffn2, m_ffn2_w_in, m_ffn2_w_out, m_ret_w_in, m_ret_head_norm, m_ret_w_out, m_gla_w_in, m_gla_w_gate, m_gla_b_gate, m_gla_head_norm, m_gla_w_out, m_final_norm, v_meta_tokens, v_norm_ffn1, v_ffn1_w_in, v_ffn1_w_out, v_norm_mix, v_norm_ffn2, v_ffn2_w_in, v_ffn2_w_out, v_ret_w_in, v_ret_head_norm, v_ret_w_out, v_gla_w_in, v_gla_w_gate, v_gla_b_gate, v_gla_head_norm, v_gla_w_out, v_final_norm):
    seq = x.shape[1]
    t = seq + CHUNK
    xs = x[0]
    target = loss_target[0]

    def ffn_w(f):
        w_in, w_out = (ffn1_w_in, ffn1_w_out) if f < 2 else (ffn2_w_in, ffn2_w_out)
        return [w_in[f % 2].astype(BF16), w_out[f % 2].astype(BF16)]

    small = jnp.concatenate([meta_tokens.reshape(-1), ret_head_norm.reshape(-1), gla_w_gate.reshape(-1),
                             gla_b_gate.reshape(-1), gla_head_norm.reshape(-1)])
    n_small = small.shape[0]
    small = jnp.pad(small, (0, 32 * 128 - n_small)).reshape(32, 128)
    sg, win0, wout0 = _run_side(_Gather([small] + ffn_w(0)), "ag_first")
    sg = sg.reshape(N_DEV, 32 * 128)

    def small_cols(off, rows, width):
        return jnp.transpose(sg[:, off:off + rows * width].reshape(N_DEV, rows, width), (1, 0, 2)).reshape(rows, N_DEV * width)

    off = 0
    meta_full = small_cols(off, N_META, D // N_DEV); off += N_META * (D // N_DEV)
    ret_hn = small_cols(off, RET_H, RET_DV // N_DEV).reshape(1, RET_H * RET_DV); off += RET_H * RET_DV // N_DEV
    wgate = small_cols(off, GLA_RANK, GLA_H * GLA_DK // N_DEV); off += GLA_RANK * GLA_H * GLA_DK // N_DEV
    bgate = small_cols(off, 1, GLA_H * GLA_DK // N_DEV); off += GLA_H * GLA_DK // N_DEV
    gla_hn = small_cols(off, GLA_H, GLA_DV // N_DEV).reshape(1, GLA_H * GLA_DV)
    wgp = jnp.pad(wgate, ((0, 128 - GLA_RANK), (0, 0))).astype(BF16)

    cos, sin = _rope_tables(t)
    lgam = _ret_consts()

    h0 = jnp.concatenate([jnp.zeros((PAD, D), F32), meta_full, xs], axis=0)
    g1 = [norm_ffn1[i:i + 1] for i in range(2)]
    gm = [norm_mix[i:i + 1] for i in range(2)]
    g2 = [norm_ffn2[i:i + 1] for i in range(2)]

    (h1, xn_a0, pg_a0, pu_a0), (ret_win_g, ret_wout_g) = _ffn_fwd(
        h0, g1[0], win0, wout0, "ffn1_l0_fwd", side=_Gather([ret_w_in[0].astype(BF16), ret_w_out[0].astype(BF16)]))
    ret_win = ret_win_g
    ret_wout = ret_wout_g.reshape(RET_H * RET_DV, D)
    (rproj, rhn), (win2,) = _norm_mm(h1, gm[0], ret_win, 4 * ret_win.shape[2], "ret_proj_fwd", side=_Gather(ffn_w(2)[:1]))
    (ro, rstates), (wout2,) = _ret_scan_fwd(rproj, cos, sin, lgam, "ret_scan_fwd", side=_Gather(ffn_w(2)[1:]))
    (h2, rog), _ = _post_fwd(ro, rproj, ret_hn, ret_wout, h1, RET_H, RET_DV, "ret_post_fwd")
    (h3, xn_b0, pg_b0, pu_b0), (win1, wout1) = _ffn_fwd(h2, g2[0], win2, wout2, "ffn2_l0_fwd", side=_Gather(ffn_w(1)))
    (h4, xn_a1, pg_a1, pu_a1), (gla_win_g, gla_wout_g) = _ffn_fwd(
        h3, g1[1], win1, wout1, "ffn1_l1_fwd", side=_Gather([gla_w_in[0].astype(BF16), gla_w_out[0].astype(BF16)]))
    gla_win = _unshard_cols(gla_win_g)
    gla_win = jnp.pad(gla_win, ((0, 0), (0, GLA_N - gla_win.shape[1])))
    gla_wout = gla_wout_g.reshape(GLA_H * GLA_DV, D)
    (gproj, ghn), _ = _norm_mm(h4, gm[1], gla_win, GLA_N, "gla_proj_fwd")
    (go, gstates, gamat, gbcum), (win3, wout3) = _gla_scan_fwd(gproj, wgp, bgate, "gla_scan_fwd", side=_Gather(ffn_w(3)))
    (h5, gog), _ = _post_fwd(go, gproj, gla_hn, gla_wout, h4, GLA_H, GLA_DV, "gla_post_fwd")
    (h6, xn_b1, pg_b1, pu_b1), _ = _ffn_fwd(h5, g2[1], win3, wout3, "ffn2_l1_fwd")

    dh, dfinal, loss_blk = _final_loss(h6, final_norm.reshape(1, D), jnp.pad(target, ((CHUNK, 0), (0, 0))), "final_loss")
    loss = lax.psum(loss_blk[0, 0], ("x", "y", "c"))

    def ffn_back(dh, h_in, xn, gain, pg, pu, win, wout, tag, side=None, dw_side=None):
        (dh_in, dob, dpg, dpu, act, dgain), got = _ffn_bwd(dh, h_in, gain, pg, pu, win, wout, tag + "_bwd", side=side)
        dwout = _mm_tn(act, dob[None], D, tag + "_dw_out").reshape(N_DEV, FF_SHARD // 2, D)
        if dw_side == "own_dw_out":
            dw_side = _Exchange([dwout])
        (dwin,), dw_got = _ffn_dw_in(xn, dpg, dpu, tag + "_dw_in", side=dw_side)
        return dh_in, [dwin, dwout], dgain[0], got, dw_got

    dh, dw_b1, dg2_1, _, _ = ffn_back(dh, h5, xn_b1, g2[1], pg_b1, pu_b1, win3, wout3, "ffn2_l1")

    (gdo, gdproj, gdhb, dghn), _ = _post_bwd(dh, go, gproj, gla_hn, gla_wout, GLA_H, GLA_DV, GLA_N, "gla_post_bwd")
    d_gla_wout = _mm_tn(gog[None], gdhb[None], D, "gla_dw_out").reshape(N_DEV, GLA_H * GLA_DV // N_DEV, D)
    gdproj, gdu = _gla_scan_bwd(gproj, wgp, bgate, gdo, gstates, gamat, gbcum, gdproj, "gla_scan_bwd")
    gdproj, dwg, dbg = _gla_gate_bwd(gdu, gproj, wgp, gdproj, "gla_gate_bwd")
    d_gla_win = _mm_tn(gdproj[None], ghn[None], D, "gla_dw_in", tm=640)[0]
    (dh, dgm_1), _ = _proj_bwd(gdproj, gla_win, dh, h4, gm[1], GLA_N, "gla_proj_bwd")
    n_gla_in = 2 * GLA_H * GLA_DK + 2 * GLA_H * GLA_DV + GLA_RANK
    d_gla_win = d_gla_win[:n_gla_in].reshape(N_DEV, n_gla_in // N_DEV, D)

    dh, dw_a1, dg1_1, rv_b1, rv_gla = ffn_back(dh, h3, xn_a1, g1[1], pg_a1, pu_a1, win1, wout1, "ffn1_l1",
                                               side=_Exchange(dw_b1), dw_side=_Exchange([d_gla_win, d_gla_wout]))
    dh, dw_b0, dg2_0, rv_a1, _ = ffn_back(dh, h2, xn_b0, g2[0], pg_b0, pu_b0, win2, wout2, "ffn2_l0", side=_Exchange(dw_a1))

    (rdo, rdproj, rdhb, drhn), rv_b0_out = _post_bwd(dh, ro, rproj, ret_hn, ret_wout, RET_H, RET_DV, 6 * D, "ret_post_bwd",
                                                     side=_Exchange(dw_b0[1:]))
    d_ret_wout = _mm_tn(rog[None], rdhb[None], D, "ret_dw_out", rows=DW_ROWS // 2).reshape(N_DEV, RET_H * RET_DV // N_DEV, D)
    (rdproj,), rv_b0_in = _ret_scan_bwd(rproj, cos, sin, lgam, rdo, rstates, rdproj, "ret_scan_bwd", side=_Exchange(dw_b0[:1]))
    rv_b0 = rv_b0_in + rv_b0_out
    d_ret_win = _mm_tn(rhn[None], rdproj[None], ret_win.shape[2], "ret_dw_in", shard_out=True)
    (dh, dgm_0), rv_ret_out = _proj_bwd(rdproj, ret_win, dh, h1, gm[0], 4 * ret_win.shape[2], "ret_proj_bwd", side=_Exchange([d_ret_wout]))

    dh, dw_a0, dg1_0, rv_ret_in, rv_a0_out = ffn_back(dh, h0, xn_a0, g1[0], pg_a0, pu_a0, win0, wout0, "ffn1_l0",
                                                      side=_Exchange([d_ret_win]), dw_side="own_dw_out")
    rv_ret = rv_ret_in + rv_ret_out
    grad_x = dh[CHUNK:][None]

    dmeta = dh[PAD:CHUNK]
    parts = jnp.concatenate([
        dg1_0, dg1_1, dgm_0[0], dgm_1[0], dg2_0, dg2_1, dfinal[0], dmeta.reshape(-1), drhn[0], dwg[:GLA_RANK].reshape(-1),
        dbg[0], dghn[0]])
    n_parts = parts.shape[0]
    rows = -(-n_parts // D)
    rows = -(-rows // 8) * 8
    parts = jnp.pad(parts, (0, rows * D - n_parts)).reshape(rows, D)
    rv_a0_in, parts_all = _run_side(_Both(_Exchange(dw_a0[:1]), _Gather([parts])), "xchg_last")
    rv_a0 = [rv_a0_in] + rv_a0_out
    tot = _small_reduce(parts_all, "small_grad_sum").reshape(-1)

    def adam_t(recvs, w, m, v, tag):
        outs = _adamw_reduce(recvs, *(jnp.swapaxes(a, 1, 2) for a in (w, m, v)), tag)
        return [jnp.swapaxes(o, 1, 2) for o in outs]

    u_ffn1_in = adam_t([rv_a0[0], rv_a1[0]], ffn1_w_in, m_ffn1_w_in, v_ffn1_w_in, "adam_ffn1_w_in")
    u_ffn2_in = adam_t([rv_b0[0], rv_b1[0]], ffn2_w_in, m_ffn2_w_in, v_ffn2_w_in, "adam_ffn2_w_in")
    u_ffn1_out = _adamw_reduce([rv_a0[1], rv_a1[1]], ffn1_w_out, m_ffn1_w_out, v_ffn1_w_out, "adam_ffn1_w_out")
    u_ffn2_out = _adamw_reduce([rv_b0[1], rv_b1[1]], ffn2_w_out, m_ffn2_w_out, v_ffn2_w_out, "adam_ffn2_w_out")
    u_ret_in = _adamw_reduce([rv_ret[0]], ret_w_in, m_ret_w_in, v_ret_w_in, "adam_ret_w_in")
    u_ret_out = _adamw_reduce([rv_ret[1]], ret_w_out, m_ret_w_out, v_ret_w_out, "adam_ret_w_out")
    u_gla_in = adam_t([rv_gla[0]], gla_w_in, m_gla_w_in, v_gla_w_in, "adam_gla_w_in")
    u_gla_out = _adamw_reduce([rv_gla[1]], gla_w_out, m_gla_w_out, v_gla_w_out, "adam_gla_w_out")


    off = 0
    def take(nel):
        nonlocal off
        out = tot[off:off + nel]
        off += nel
        return out

    gr_norm_ffn1 = take(2 * D).reshape(2, D)
    gr_norm_mix = take(2 * D).reshape(2, D)
    gr_norm_ffn2 = take(2 * D).reshape(2, D)
    gr_final = take(D)
    gr_meta = _my_cols(take(N_META * D).reshape(N_META, D), D // N_DEV)
    gr_ret_hn = _my_cols(take(RET_H * RET_DV).reshape(RET_H, RET_DV), RET_DV // N_DEV)[None]
    gr_wgate = _my_cols(take(GLA_RANK * GLA_H * GLA_DK).reshape(GLA_RANK, GLA_H * GLA_DK), GLA_H * GLA_DK // N_DEV)[None]
    gr_bgate = _my_cols(take(GLA_H * GLA_DK).reshape(1, GLA_H * GLA_DK), GLA_H * GLA_DK // N_DEV)
    gr_gla_hn = _my_cols(take(GLA_H * GLA_DV).reshape(GLA_H, GLA_DV), GLA_DV // N_DEV)[None]

    small_w = [meta_tokens, norm_ffn1, norm_mix, norm_ffn2, ret_head_norm, gla_w_gate, gla_b_gate, gla_head_norm, final_norm]
    small_g = [gr_meta, gr_norm_ffn1, gr_norm_mix, gr_norm_ffn2, gr_ret_hn, gr_wgate, gr_bgate, gr_gla_hn, gr_final]
    small_m = [m_meta_tokens, m_norm_ffn1, m_norm_mix, m_norm_ffn2, m_ret_head_norm, m_gla_w_gate, m_gla_b_gate, m_gla_head_norm, m_final_norm]
    small_v = [v_meta_tokens, v_norm_ffn1, v_norm_mix, v_norm_ffn2, v_ret_head_norm, v_gla_w_gate, v_gla_b_gate, v_gla_head_norm, v_final_norm]

    def pack(arrs):
        flat = jnp.concatenate([a.reshape(-1) for a in arrs])
        n = flat.shape[0]
        r = -(-n // 128)
        r = -(-r // 8) * 8
        return jnp.pad(flat, (0, r * 128 - n), constant_values=1.0).reshape(r, 128)

    sd, sm, sv = _adamw_small(pack(small_w), pack(small_g), pack(small_m), pack(small_v), "adam_small")

    def unpack(buf):
        flat = buf.reshape(-1)
        outs, o = [], 0
        for a in small_w:
            outs.append(flat[o:o + a.size].reshape(a.shape))
            o += a.size
        return outs

    us_d, us_m, us_v = unpack(sd), unpack(sm), unpack(sv)

    def ordered(k, smalls):
        return (smalls[0], smalls[1], u_ffn1_in[k], u_ffn1_out[k], smalls[2], smalls[3], u_ffn2_in[k], u_ffn2_out[k],
                u_ret_in[k], smalls[4], u_ret_out[k], u_gla_in[k], smalls[5], smalls[6], smalls[7], u_gla_out[k], smalls[8])

    return (loss, grad_x, *ordered(0, small_g), *ordered(1, us_d), *ordered(2, us_m), *ordered(3, us_v))
```

```python
import functools

import numpy as np
import jax
import jax.numpy as jnp
from jax import lax
from jax.experimental import pallas as pl
from jax.experimental.pallas import tpu as pltpu

F32 = jnp.float32
BF16 = jnp.bfloat16
S = jax.ShapeDtypeStruct
ANY = pl.BlockSpec(memory_space=pl.ANY)
MESH = pl.DeviceIdType.MESH

D = 1024
N_META = 16
CHUNK = 64
PAD = CHUNK - N_META
EPS = 1e-6
N_DEV = 8
FF_SHARD = 704
N_FF_CHUNK = 4
RET_H, RET_DK, RET_DV = 4, 256, 512
RET_QKV = RET_H * (2 * RET_DK + RET_DV)
RET_C = 192
GLA_H, GLA_DK, GLA_DV, GLA_RANK, GLA_TAU = 4, 128, 256, 16, 16.0
GLA_QKV = GLA_H * (2 * GLA_DK + GLA_DV)
GLA_N = 3200
GLA_ZBLK = 3072 // 128
SUB = 16
ROPE_BASE = 10000.0
ADAM_LR, ADAM_B1, ADAM_B2, ADAM_EPS, ADAM_WD, ADAM_STEP = 0.001, 0.9, 0.999, 1e-08, 0.01, 10
VMEM_LIMIT = 58 * 1024 * 1024
DW_ROWS = 2752


def _cp(**kw):
    return pltpu.CompilerParams(vmem_limit_bytes=VMEM_LIMIT, **kw)


def _row_tile(t, cap):
    best = 16
    for d in range(16, cap + 1, 16):
        if t % d == 0:
            best = d
    return best


def _sub_rows(tm, parts=2):
    units = tm // 16
    cuts = [16 * (units * p // parts) for p in range(parts + 1)]
    return [slice(a, b) for a, b in zip(cuts[:-1], cuts[1:]) if b > a]


def _dot(a, b):
    return jnp.dot(a, b, preferred_element_type=F32)


def _dot_nt(a, b):
    return lax.dot_general(a, b, (((1,), (1,)), ((), ())), preferred_element_type=F32)


def _dot_tn(a, b):
    return lax.dot_general(a, b, (((0,), (0,)), ((), ())), preferred_element_type=F32)


def _sigmoid(x):
    return pl.reciprocal(1.0 + jnp.exp(-x), approx=True)


def _rms_bwd(dxn, x, gain):
    r = lax.rsqrt(jnp.mean(x * x, axis=-1, keepdims=True) + EPS)
    xh = x * r
    dxh = dxn * gain
    dx = r * (dxh - xh * jnp.mean(dxh * xh, axis=-1, keepdims=True))
    return dx, jnp.sum(dxn * xh, axis=0, keepdims=True)


def _xyc():
    return lax.axis_index("x"), lax.axis_index("y"), lax.axis_index("c")


class _Gather:
    def __init__(self, xs):
        self.xs = list(xs)
        self.n = len(self.xs)

    def out_shape(self):
        return [S((N_DEV,) + a.shape, a.dtype) for a in self.xs]

    def scratch(self):
        return [pltpu.SemaphoreType.DMA((self.n, 7)), pltpu.SemaphoreType.DMA((self.n, 7)), pltpu.SemaphoreType.DMA((self.n,))]

    def phases(self, x_refs, out_refs, send_sems, recv_sems, local_sems):
        x, y, c = _xyc()
        me, sibling = (x, y, c), (x, y, 1 - c)
        chips = [(1 - x, y), (x, 1 - y), (1 - x, 1 - y)]

        def copy(t, k, block, to, src=None):
            px, py, pc = block
            dst = out_refs[t].at[4 * px + 2 * py + pc]
            return pltpu.make_async_remote_copy(
                src_ref=dst if src is None else src, dst_ref=dst,
                send_sem=send_sems.at[t, k], recv_sem=recv_sems.at[t, k], device_id=to, device_id_type=MESH)

        def own(t):
            return pltpu.make_async_copy(x_refs[t], out_refs[t].at[4 * x + 2 * y + c], local_sems.at[t])

        def first(t):
            return [copy(t, 0, me, sibling, src=x_refs[t])] + [
                copy(t, 1 + j, me, (*chip, c), src=x_refs[t]) for j, chip in enumerate(chips)]

        def passed(t):
            return [copy(t, 4 + j, (*chip, c), sibling) for j, chip in enumerate(chips)]

        def start():
            for t in range(self.n):
                own(t).start()
                for cp in first(t):
                    cp.start()

        def mid():
            for t in range(self.n):
                fw = passed(t)
                for j, chip in enumerate(chips):
                    copy(t, 1 + j, (*chip, c), me).wait_recv()
                    fw[j].start()

        def finish():
            for t in range(self.n):
                copy(t, 0, sibling, me).wait_recv()
                for j, chip in enumerate(chips):
                    copy(t, 4 + j, (*chip, 1 - c), me).wait_recv()
                for cp in first(t) + passed(t):
                    cp.wait_send()
                own(t).wait()

        return start, mid, finish


class _Exchange:
    def __init__(self, xs):
        self.xs = list(xs)
        self.n = len(self.xs)

    def out_shape(self):
        return [S(a.shape, a.dtype) for a in self.xs]

    def scratch(self):
        return [pltpu.SemaphoreType.DMA((self.n, 7)), pltpu.SemaphoreType.DMA((self.n, 7)), pltpu.SemaphoreType.DMA((self.n,))]

    def phases(self, g_refs, r_refs, send_sems, recv_sems, local_sems):
        x, y, c = _xyc()
        me = 4 * x + 2 * y + c

        def own(t):
            return pltpu.make_async_copy(g_refs[t].at[me], r_refs[t].at[me], local_sems.at[t])

        def send(t, m):
            px, py, pc = x ^ (m >> 2), y ^ ((m >> 1) & 1), c ^ (m & 1)
            return pltpu.make_async_remote_copy(
                src_ref=g_refs[t].at[4 * px + 2 * py + pc], dst_ref=r_refs[t].at[me],
                send_sem=send_sems.at[t, m - 1], recv_sem=recv_sems.at[t, m - 1],
                device_id=(px, py, pc), device_id_type=MESH)

        def arrival(t, m):
            peer = 4 * (x ^ (m >> 2)) + 2 * (y ^ ((m >> 1) & 1)) + (c ^ (m & 1))
            return pltpu.make_async_remote_copy(
                src_ref=g_refs[t].at[peer], dst_ref=r_refs[t].at[peer],
                send_sem=send_sems.at[t, m - 1], recv_sem=recv_sems.at[t, m - 1],
                device_id=(x, y, c), device_id_type=MESH)

        def start():
            for t in range(self.n):
                own(t).start()
            for m in range(1, N_DEV):
                for t in range(self.n):
                    send(t, m).start()

        def mid():
            pass

        def finish():
            for m in range(1, N_DEV):
                for t in range(self.n):
                    arrival(t, m).wait_recv()
            for m in range(1, N_DEV):
                for t in range(self.n):
                    send(t, m).wait_send()
            for t in range(self.n):
                own(t).wait()

        return start, mid, finish


class _Both:
    def __init__(self, a, b):
        self.a, self.b = a, b
        self.xs = a.xs + b.xs
        self.n = a.n + b.n

    def out_shape(self):
        return self.a.out_shape() + self.b.out_shape()

    def scratch(self):
        return self.a.scratch() + self.b.scratch()

    def phases(self, x_refs, out_refs, *sems):
        na = self.a.n
        pa = self.a.phases(x_refs[:na], out_refs[:na], *sems[:3])
        pb = self.b.phases(x_refs[na:], out_refs[na:], *sems[3:])
        return tuple((lambda f, g: (lambda: (f(), g())))(f, g) for f, g in zip(pa, pb))


def _run_side(side, name):
    n = side.n

    def body(*refs):
        start, mid, finish = side.phases(refs[:n], refs[n:2 * n], *refs[2 * n:])
        start()
        mid()
        finish()

    return list(pl.pallas_call(
        body, name=name, out_shape=side.out_shape(), in_specs=[ANY] * n, out_specs=[ANY] * n,
        scratch_shapes=side.scratch())(*side.xs))


def _grid_steps(grid):
    def ids():
        return [pl.program_id(a) for a in range(len(grid))]

    def first():
        return functools.reduce(jnp.logical_and, [i == 0 for i in ids()])

    def middle():
        i = ids()
        return functools.reduce(jnp.logical_and, [i[0] == (3 * grid[0]) // 4] + [j == 0 for j in i[1:]])

    def last():
        return functools.reduce(jnp.logical_and, [i == g - 1 for i, g in zip(ids(), grid)])

    return first, middle, last


def _call(body, *, name, grid, in_specs, out_specs, out_shape, scratch_shapes, operands, side=None, aliases=None):
    n_in, n_out, n_scr = len(in_specs), len(out_shape), len(scratch_shapes)
    full = body
    if side is not None:
        ns = side.n
        first, middle, last = _grid_steps(grid)

        def full(*refs):
            a = n_in
            ins, sins = refs[:a], refs[a:a + ns]
            a += ns
            outs, souts = refs[a:a + n_out], refs[a + n_out:a + n_out + ns]
            a += n_out + ns
            scr, sems = refs[a:a + n_scr], refs[a + n_scr:]
            start, mid, finish = side.phases(sins, souts, *sems)
            pl.when(first())(start)
            body(*ins, *outs, *scr)
            pl.when(middle())(mid)
            pl.when(last())(finish)

        in_specs = list(in_specs) + [ANY] * ns
        out_specs = list(out_specs) + [ANY] * ns
        out_shape = list(out_shape) + side.out_shape()
        scratch_shapes = list(scratch_shapes) + side.scratch()
        operands = list(operands) + side.xs
    outs = pl.pallas_call(
        full, name=name, grid=grid, in_specs=list(in_specs), out_specs=list(out_specs), out_shape=list(out_shape),
        scratch_shapes=list(scratch_shapes), input_output_aliases=aliases or {},
        compiler_params=_cp(dimension_semantics=("arbitrary",) * len(grid)),
    )(*operands)
    return list(outs[:n_out]), list(outs[n_out:])


def _ffn_fwd(h, gain, win, wout, name, side=None):
    t = h.shape[0]
    tm = _row_tile(t, 704)
    nt = t // tm

    def body(h_ref, g_ref, wg_ref, wu_ref, wo_ref, hn_ref, xn_ref, pg_ref, pu_ref, acc):
        c = pl.program_id(1)

        @pl.when(c == 0)
        def _():
            x = h_ref[...]
            r = lax.rsqrt(jnp.mean(x * x, axis=-1, keepdims=True) + EPS)
            xn_ref[...] = (x * r * g_ref[...]).astype(BF16)
            acc[...] = jnp.zeros_like(acc)

        wo = wo_ref[...].reshape(FF_SHARD, D)
        subs = _sub_rows(tm)
        gus = [(_dot(xn_ref[r, :], wg_ref[...]), _dot(xn_ref[r, :], wu_ref[...])) for r in subs]
        for r, (g, u) in zip(subs, gus):
            pg_ref[r, :] = g.astype(BF16)
            pu_ref[r, :] = u.astype(BF16)
            act = (g * _sigmoid(g) * u).astype(BF16)
            acc[r, :] += _dot(act, wo)

        @pl.when(c == N_FF_CHUNK - 1)
        def _():
            hn_ref[...] = h_ref[...] + 0.5 * acc[...]

    return _call(
        body, name=name, grid=(nt, N_FF_CHUNK), side=side,
        in_specs=[
            pl.BlockSpec((tm, D), lambda i, c: (i, 0)),
            pl.BlockSpec((1, D), lambda i, c: (0, 0)),
            pl.BlockSpec((None, D, FF_SHARD), lambda i, c: (c, 0, 0)),
            pl.BlockSpec((None, D, FF_SHARD), lambda i, c: (c + N_FF_CHUNK, 0, 0)),
            pl.BlockSpec((2, FF_SHARD // 2, D), lambda i, c: (c, 0, 0)),
        ],
        out_specs=[
            pl.BlockSpec((tm, D), lambda i, c: (i, 0)),
            pl.BlockSpec((tm, D), lambda i, c: (i, 0)),
            pl.BlockSpec((None, tm, FF_SHARD), lambda i, c: (c, i, 0)),
            pl.BlockSpec((None, tm, FF_SHARD), lambda i, c: (c, i, 0)),
        ],
        out_shape=[S((t, D), F32), S((t, D), BF16), S((N_FF_CHUNK, t, FF_SHARD), BF16), S((N_FF_CHUNK, t, FF_SHARD), BF16)],
        scratch_shapes=[pltpu.VMEM((tm, D), F32)],
        operands=[h, gain, win, win, wout])


def _ffn_bwd(dh, h, gain, pg, pu, win, wout, name, side=None):
    t = h.shape[0]
    tm = _row_tile(t, 704)
    nt = t // tm

    def body(dh_ref, h_ref, g_ref, pg_ref, pu_ref, wg_ref, wu_ref, wo_ref,
             dhi_ref, dob_ref, dpg_ref, dpu_ref, act_ref, dgain_ref, acc):
        i, c = pl.program_id(0), pl.program_id(1)

        @pl.when(c == 0)
        def _():
            dob_ref[...] = (0.5 * dh_ref[...]).astype(BF16)
            acc[...] = jnp.zeros_like(acc)

        @pl.when((i == 0) & (c == 0))
        def _():
            dgain_ref[...] = jnp.zeros_like(dgain_ref)

        wo = wo_ref[...].reshape(FF_SHARD, D)
        subs = _sub_rows(tm)
        dacts = [_dot_nt(dob_ref[r, :], wo) for r in subs]
        for r, dact in zip(subs, dacts):
            g = pg_ref[r, :].astype(F32)
            u = pu_ref[r, :].astype(F32)
            s = _sigmoid(g)
            sl = g * s
            act_ref[r, :] = (sl * u).astype(BF16)
            dg = (dact * u * (s * (1.0 + g * (1.0 - s)))).astype(BF16)
            du = (dact * sl).astype(BF16)
            dpg_ref[r, :] = dg
            dpu_ref[r, :] = du
            acc[r, :] += _dot_nt(dg, wg_ref[...]) + _dot_nt(du, wu_ref[...])

        @pl.when(c == N_FF_CHUNK - 1)
        def _():
            dx, dgn = _rms_bwd(acc[...], h_ref[...], g_ref[...])
            dhi_ref[...] = dh_ref[...] + dx
            dgain_ref[0:1, :] += dgn

    blk = pl.BlockSpec((None, tm, FF_SHARD), lambda i, c: (c, i, 0))
    row = pl.BlockSpec((tm, D), lambda i, c: (i, 0))
    return _call(
        body, name=name, grid=(nt, N_FF_CHUNK), side=side,
        in_specs=[
            row, row, pl.BlockSpec((1, D), lambda i, c: (0, 0)), blk, blk,
            pl.BlockSpec((None, D, FF_SHARD), lambda i, c: (c, 0, 0)),
            pl.BlockSpec((None, D, FF_SHARD), lambda i, c: (c + N_FF_CHUNK, 0, 0)),
            pl.BlockSpec((2, FF_SHARD // 2, D), lambda i, c: (c, 0, 0)),
        ],
        out_specs=[row, row, blk, blk, blk, pl.BlockSpec((8, D), lambda i, c: (0, 0))],
        out_shape=[S((t, D), F32), S((t, D), BF16)] + [S((N_FF_CHUNK, t, FF_SHARD), BF16)] * 3 + [S((8, D), F32)],
        scratch_shapes=[pltpu.VMEM((tm, D), F32)],
        operands=[dh, h, gain, pg, pu, win, win, wout])


def _ffn_dw_in(xn, dpg, dpu, name, side=None):
    t = xn.shape[0]
    tk = _row_tile(t, DW_ROWS)
    nk = t // tk

    def body(a_ref, bg_ref, bu_ref, o_ref, acc):
        c, k = pl.program_id(0), pl.program_id(1)

        @pl.when(k == 0)
        def _():
            acc[...] = jnp.zeros_like(acc)

        @pl.when(c < N_FF_CHUNK)
        def _():
            acc[...] += _dot_tn(bg_ref[...], a_ref[...])

        @pl.when(c >= N_FF_CHUNK)
        def _():
            acc[...] += _dot_tn(bu_ref[...], a_ref[...])

        @pl.when(k == nk - 1)
        def _():
            o_ref[...] = acc[...].astype(BF16)

    return _call(
        body, name=name, grid=(2 * N_FF_CHUNK, nk), side=side,
        in_specs=[
            pl.BlockSpec((tk, D), lambda c, k: (k, 0)),
            pl.BlockSpec((None, tk, FF_SHARD), lambda c, k: (jnp.minimum(c, N_FF_CHUNK - 1), k, 0)),
            pl.BlockSpec((None, tk, FF_SHARD), lambda c, k: (jnp.maximum(c - N_FF_CHUNK, 0), k, 0)),
        ],
        out_specs=[pl.BlockSpec((None, FF_SHARD, D), lambda c, k: (c, 0, 0))],
        out_shape=[S((2 * N_FF_CHUNK, FF_SHARD, D), BF16)],
        scratch_shapes=[pltpu.VMEM((FF_SHARD, D), F32)],
        operands=[xn, dpg, dpu])


def _mm_tn(a, b, tn, name, tm=None, rows=DW_ROWS, shard_out=False):
    ca, t, m = a.shape
    cb, _, n = b.shape
    nc = max(ca, cb)
    tm = m if tm is None else tm
    tk = _row_tile(t, rows)
    nk = t // tk

    def body(a_ref, b_ref, o_ref, acc):
        k = pl.program_id(3)

        @pl.when(k == 0)
        def _():
            acc[...] = jnp.zeros_like(acc)

        acc[...] += _dot_tn(a_ref[...], b_ref[...])

        @pl.when(k == nk - 1)
        def _():
            o_ref[...] = acc[...].astype(BF16)

    if shard_out:
        out_spec = pl.BlockSpec((None, tm, tn), lambda c, i, j, k: (j, 0, 0))
        out_shape = S((n // tn, m, tn), BF16)
    else:
        out_spec = pl.BlockSpec((None, tm, tn), lambda c, i, j, k: (c, i, j))
        out_shape = S((nc, m, n), BF16)
    return pl.pallas_call(
        body, name=name, grid=(nc, m // tm, n // tn, nk),
        in_specs=[
            pl.BlockSpec((None, tk, tm), (lambda c, i, j, k: (c, k, i)) if ca > 1 else (lambda c, i, j, k: (0, k, i))),
            pl.BlockSpec((None, tk, tn), (lambda c, i, j, k: (c, k, j)) if cb > 1 else (lambda c, i, j, k: (0, k, j))),
        ],
        out_specs=out_spec, out_shape=out_shape,
        scratch_shapes=[pltpu.VMEM((tm, tn), F32)],
        compiler_params=_cp(dimension_semantics=("arbitrary",) * 4),
    )(a, b)


def _norm_mm(h, gain, w, tn, name, side=None):
    t = h.shape[0]
    n = w.shape[-1] if w.ndim == 2 else w.shape[0] * w.shape[2]
    tm = _row_tile(t, 704)
    kb = 1 if w.ndim == 2 else tn // w.shape[2]
    w_spec = (pl.BlockSpec((D, tn), lambda i, j: (0, j)) if w.ndim == 2
              else pl.BlockSpec((kb, D, tn // kb), lambda i, j: (j, 0, 0)))

    def body(h_ref, g_ref, w_ref, o_ref, xn_ref):
        @pl.when(pl.program_id(1) == 0)
        def _():
            x = h_ref[...]
            r = lax.rsqrt(jnp.mean(x * x, axis=-1, keepdims=True) + EPS)
            xn_ref[...] = (x * r * g_ref[...]).astype(BF16)

        if w.ndim == 2:
            o_ref[...] = _dot(xn_ref[...], w_ref[...]).astype(BF16)
        else:
            for b in range(kb):
                o_ref[:, b * (tn // kb):(b + 1) * (tn // kb)] = _dot(xn_ref[...], w_ref[b]).astype(BF16)

    return _call(
        body, name=name, grid=(t // tm, n // tn), side=side,
        in_specs=[pl.BlockSpec((tm, D), lambda i, j: (i, 0)), pl.BlockSpec((1, D), lambda i, j: (0, 0)), w_spec],
        out_specs=[pl.BlockSpec((tm, tn), lambda i, j: (i, j)), pl.BlockSpec((tm, D), lambda i, j: (i, 0))],
        out_shape=[S((t, n), BF16), S((t, D), BF16)], scratch_shapes=[],
        operands=[h, gain, w])


def _proj_bwd(dproj, w, dh, h, gain, tk, name, side=None):
    t, n = dproj.shape
    tm = _row_tile(t, 704)
    nk = n // tk
    kb = 1 if w.ndim == 2 else tk // w.shape[2]
    w_spec = (pl.BlockSpec((D, tk), lambda i, k: (0, k)) if w.ndim == 2
              else pl.BlockSpec((kb, D, tk // kb), lambda i, k: (k, 0, 0)))

    def body(dp_ref, w_ref, dh_ref, h_ref, g_ref, dhi_ref, dgain_ref, acc):
        i, k = pl.program_id(0), pl.program_id(1)

        @pl.when(k == 0)
        def _():
            acc[...] = jnp.zeros_like(acc)

        @pl.when((i == 0) & (k == 0))
        def _():
            dgain_ref[...] = jnp.zeros_like(dgain_ref)

        if w.ndim == 2:
            acc[...] += _dot_nt(dp_ref[...], w_ref[...])
        else:
            for b in range(kb):
                acc[...] += _dot_nt(dp_ref[:, b * (tk // kb):(b + 1) * (tk // kb)], w_ref[b])

        @pl.when(k == nk - 1)
        def _():
            dx, dgn = _rms_bwd(acc[...], h_ref[...], g_ref[...])
            dhi_ref[...] = dh_ref[...] + dx
            dgain_ref[0:1, :] += dgn

    row = pl.BlockSpec((tm, D), lambda i, k: (i, 0))
    return _call(
        body, name=name, grid=(t // tm, nk), side=side,
        in_specs=[pl.BlockSpec((tm, tk), lambda i, k: (i, k)), w_spec,
                  row, row, pl.BlockSpec((1, D), lambda i, k: (0, 0))],
        out_specs=[row, pl.BlockSpec((8, D), lambda i, k: (0, 0))],
        out_shape=[S((t, D), F32), S((8, D), F32)],
        scratch_shapes=[pltpu.VMEM((tm, D), F32)],
        operands=[dproj, w, dh, h, gain])


def _post_fwd(o, proj, hgain, wout, h, nh, dv, name, side=None):
    t = h.shape[0]
    w = nh * dv
    tm = _row_tile(t, 704)

    def body(o_ref, g_ref, hg_ref, wo_ref, h_ref, hn_ref, og_ref):
        for rows in _sub_rows(tm):
            for hd in range(nh):
                sl = slice(hd * dv, (hd + 1) * dv)
                oh = o_ref[rows, sl].astype(F32)
                r = lax.rsqrt(jnp.mean(oh * oh, axis=-1, keepdims=True) + EPS)
                gg = g_ref[rows, sl].astype(F32)
                og_ref[rows, sl] = (oh * r * hg_ref[:, sl] * (gg * _sigmoid(gg))).astype(BF16)
            hn_ref[rows, :] = h_ref[rows, :] + _dot(og_ref[rows, :], wo_ref[...])

    return _call(
        body, name=name, grid=(t // tm,), side=side,
        in_specs=[pl.BlockSpec((tm, w), lambda i: (i, 0)), pl.BlockSpec((tm, w), lambda i: (i, 2)),
                  pl.BlockSpec((1, w), lambda i: (0, 0)), pl.BlockSpec((w, D), lambda i: (0, 0)),
                  pl.BlockSpec((tm, D), lambda i: (i, 0))],
        out_specs=[pl.BlockSpec((tm, D), lambda i: (i, 0)), pl.BlockSpec((tm, w), lambda i: (i, 0))],
        out_shape=[S((t, D), F32), S((t, w), BF16)], scratch_shapes=[],
        operands=[o, proj, hgain, wout, h])


def _post_bwd(dh, o, proj, hgain, wout, nh, dv, nproj, name, side=None):
    t = dh.shape[0]
    w = nh * dv
    tm = _row_tile(t, 704)

    def body(dh_ref, o_ref, g_ref, hg_ref, wo_ref, do_ref, dg_ref, dhb_ref, dhg_ref):
        @pl.when(pl.program_id(0) == 0)
        def _():
            dhg_ref[...] = jnp.zeros_like(dhg_ref)

        dhb_ref[...] = dh_ref[...].astype(BF16)
        subs = _sub_rows(tm)
        dogs = [_dot_nt(dhb_ref[rows, :], wo_ref[...]) for rows in subs]
        for rows, dog in zip(subs, dogs):
            for hd in range(nh):
                sl = slice(hd * dv, (hd + 1) * dv)
                oh = o_ref[rows, sl].astype(F32)
                r = lax.rsqrt(jnp.mean(oh * oh, axis=-1, keepdims=True) + EPS)
                xh = oh * r
                gain = hg_ref[:, sl]
                gg = g_ref[rows, sl].astype(F32)
                s = _sigmoid(gg)
                dogh = dog[:, sl]
                don = dogh * (gg * s)
                dg_ref[rows, sl] = (dogh * (xh * gain) * (s * (1.0 + gg * (1.0 - s)))).astype(BF16)
                dxh = don * gain
                do_ref[rows, sl] = (r * (dxh - xh * jnp.mean(dxh * xh, axis=-1, keepdims=True))).astype(BF16)
                dhg_ref[0:1, sl] += jnp.sum(don * xh, axis=0, keepdims=True)

    return _call(
        body, name=name, grid=(t // tm,), side=side,
        in_specs=[pl.BlockSpec((tm, D), lambda i: (i, 0)), pl.BlockSpec((tm, w), lambda i: (i, 0)),
                  pl.BlockSpec((tm, w), lambda i: (i, 2)), pl.BlockSpec((1, w), lambda i: (0, 0)),
                  pl.BlockSpec((w, D), lambda i: (0, 0))],
        out_specs=[pl.BlockSpec((tm, w), lambda i: (i, 0)), pl.BlockSpec((tm, w), lambda i: (i, 2)),
                   pl.BlockSpec((tm, D), lambda i: (i, 0)), pl.BlockSpec((8, w), lambda i: (0, 0))],
        out_shape=[S((t, w), BF16), S((t, nproj), BF16), S((t, D), BF16), S((8, w), F32)], scratch_shapes=[],
        operands=[dh, o, proj, hgain, wout])


def _ret_consts():
    lg = np.log1p(-np.exp2(-5.0 - np.arange(RET_H, dtype=np.float32))).astype(np.float32)
    return jnp.asarray(np.broadcast_to(lg[:, None, None], (RET_H, 1, 128)).copy())


def _rope_tables(t):
    half = RET_DK // 2
    inv = 1.0 / (ROPE_BASE ** jnp.linspace(0.0, 1.0, half, dtype=F32))
    base = (jnp.arange(t // CHUNK) * CHUNK - PAD).astype(F32)[:, None] * inv[None, :]
    off = jnp.arange(CHUNK).astype(F32)[:, None] * inv[None, :]
    ca, sa = jnp.cos(base)[:, None, :], jnp.sin(base)[:, None, :]
    cb, sb = jnp.cos(off)[None], jnp.sin(off)[None]
    return (ca * cb - sa * sb).reshape(t, half), (sa * cb + ca * sb).reshape(t, half)


def _ret_chunk(blk_ref, cos_ref, sin_ref, lg, h):
    c = RET_C
    half = RET_DK // 2
    oq, ok, ov = h * RET_DK, RET_H * RET_DK + h * RET_DK, 2 * RET_H * RET_DK + h * RET_DV
    cs, sn = cos_ref[...], sin_ref[...]
    q1, q2 = blk_ref[:, oq:oq + half].astype(F32), blk_ref[:, oq + half:oq + RET_DK].astype(F32)
    k1, k2 = blk_ref[:, ok:ok + half].astype(F32), blk_ref[:, ok + half:ok + RET_DK].astype(F32)
    qr = jnp.concatenate([q1 * cs - q2 * sn, q1 * sn + q2 * cs], axis=1)
    kr = jnp.concatenate([k1 * cs - k2 * sn, k1 * sn + k2 * cs], axis=1) * (RET_DK ** -0.5)
    v = blk_ref[:, ov:ov + RET_DV]
    ii = lax.broadcasted_iota(jnp.int32, (c, 1), 0).astype(F32)
    jj = lax.broadcasted_iota(jnp.int32, (1, c), 1).astype(F32)
    rel = ii - jj
    dmat = jnp.where(rel >= 0, jnp.exp(lg * jnp.maximum(rel, 0.0)), 0.0)
    dq = jnp.exp(lg * (ii + 1.0))
    dk = jnp.exp(lg * (c - 1.0 - ii))
    dchunk = jnp.exp(lg * float(c))
    return qr, kr, v, dmat, dq, dk, dchunk


def _ret_scan_fwd(proj, cos, sin, lgam, name, side=None):
    t = proj.shape[0]
    c = RET_C
    nc = t // c

    def body(blk_ref, cos_ref, sin_ref, lg_ref, o_ref, st_ref, state):
        @pl.when(pl.program_id(0) == 0)
        def _():
            state[...] = jnp.zeros_like(state)

        for h in range(RET_H):
            qr, kr, v, dmat, dq, dk, dchunk = _ret_chunk(blk_ref, cos_ref, sin_ref, lg_ref[h, :, 0:1], h)
            sp = state[h]
            st_ref[h] = sp.astype(BF16)
            scores = _dot_nt(qr.astype(BF16), kr.astype(BF16)) * dmat
            o = _dot(scores.astype(BF16), v) + _dot((qr * dq).astype(BF16), sp.astype(BF16))
            o_ref[:, h * RET_DV:(h + 1) * RET_DV] = o.astype(BF16)
            state[h] = sp * dchunk + _dot_tn((kr * dk).astype(BF16), v)

    return _call(
        body, name=name, grid=(nc,), side=side,
        in_specs=[pl.BlockSpec((c, RET_QKV), lambda n: (n, 0)), pl.BlockSpec((c, 128), lambda n: (n, 0)),
                  pl.BlockSpec((c, 128), lambda n: (n, 0)), pl.BlockSpec((RET_H, 1, 128), lambda n: (0, 0, 0))],
        out_specs=[pl.BlockSpec((c, RET_H * RET_DV), lambda n: (n, 0)),
                   pl.BlockSpec((RET_H, None, RET_DK, RET_DV), lambda n: (0, n, 0, 0))],
        out_shape=[S((t, RET_H * RET_DV), BF16), S((RET_H, nc, RET_DK, RET_DV), BF16)],
        scratch_shapes=[pltpu.VMEM((RET_H, RET_DK, RET_DV), F32)],
        operands=[proj, cos, sin, lgam])


def _ret_scan_bwd(proj, cos, sin, lgam, do, states, dproj, name, side=None):
    t = proj.shape[0]
    c = RET_C
    nc = t // c
    half = RET_DK // 2

    def body(blk_ref, cos_ref, sin_ref, lg_ref, do_ref, st_ref, dp_in, dp_ref, dstate):
        n = nc - 1 - pl.program_id(0)

        @pl.when(pl.program_id(0) == 0)
        def _():
            dstate[...] = jnp.zeros_like(dstate)

        cs, sn = cos_ref[...], sin_ref[...]
        rows = n * c + lax.broadcasted_iota(jnp.int32, (c, 1), 0)
        keep = rows >= PAD

        def unrot(d):
            d1, d2 = d[:, :half], d[:, half:]
            return jnp.concatenate([d1 * cs + d2 * sn, d2 * cs - d1 * sn], axis=1)

        for h in range(RET_H):
            qr, kr, v, dmat, dq, dk, dchunk = _ret_chunk(blk_ref, cos_ref, sin_ref, lg_ref[h, :, 0:1], h)
            qb, kb = qr.astype(BF16), kr.astype(BF16)
            dob = do_ref[:, h * RET_DV:(h + 1) * RET_DV]
            sp = st_ref[h]
            ds = dstate[h]
            dsb = ds.astype(BF16)
            p = (_dot_nt(qb, kb) * dmat).astype(BF16)
            dvv = _dot_tn(p, dob) + _dot((kr * dk).astype(BF16), dsb)
            dp = (_dot_nt(dob, v) * dmat).astype(BF16)
            dqr = _dot(dp, kb) + _dot_nt(dob, sp) * dq
            dkr = (_dot_tn(dp, qb) + _dot_nt(v, dsb) * dk) * (RET_DK ** -0.5)
            dstate[h] = ds * dchunk + _dot_tn((qr * dq).astype(BF16), dob)
            oq, ok, ov = h * RET_DK, RET_H * RET_DK + h * RET_DK, 2 * RET_H * RET_DK + h * RET_DV
            dp_ref[:, oq:oq + RET_DK] = jnp.where(keep, unrot(dqr), 0.0).astype(BF16)
            dp_ref[:, ok:ok + RET_DK] = jnp.where(keep, unrot(dkr), 0.0).astype(BF16)
            dp_ref[:, ov:ov + RET_DV] = jnp.where(keep, dvv, 0.0).astype(BF16)

    return _call(
        body, name=name, grid=(nc,), side=side, aliases={6: 0},
        in_specs=[pl.BlockSpec((c, RET_QKV), lambda n: (nc - 1 - n, 0)), pl.BlockSpec((c, 128), lambda n: (nc - 1 - n, 0)),
                  pl.BlockSpec((c, 128), lambda n: (nc - 1 - n, 0)), pl.BlockSpec((RET_H, 1, 128), lambda n: (0, 0, 0)),
                  pl.BlockSpec((c, RET_H * RET_DV), lambda n: (nc - 1 - n, 0)),
                  pl.BlockSpec((RET_H, None, RET_DK, RET_DV), lambda n: (0, nc - 1 - n, 0, 0)), ANY],
        out_specs=[pl.BlockSpec((c, RET_QKV), lambda n: (nc - 1 - n, 0))],
        out_shape=[S((t, dproj.shape[1]), BF16)],
        scratch_shapes=[pltpu.VMEM((RET_H, RET_DK, RET_DV), F32)],
        operands=[proj, cos, sin, lgam, do, states, dproj])


def _split3(x):
    hi = x.astype(BF16)
    r1 = x - hi.astype(F32)
    mid = r1.astype(BF16)
    lo = (r1 - mid.astype(F32)).astype(BF16)
    return hi, mid, lo


def _gla_chunk(blk_ref, z_ref, wg_ref, bg_ref, n, h, b_ref=None):
    c = CHUNK
    oq, ok, ov = h * GLA_DK, GLA_H * GLA_DK + h * GLA_DK, 2 * GLA_H * GLA_DK + h * GLA_DV
    q = blk_ref[:, oq:oq + GLA_DK].astype(F32) * (GLA_DK ** -0.5)
    k = blk_ref[:, ok:ok + GLA_DK].astype(F32)
    v = blk_ref[:, ov:ov + GLA_DV]
    hs = slice(h * GLA_DK, (h + 1) * GLA_DK)
    u = _dot(z_ref[...], wg_ref[:, hs]) + bg_ref[:, hs]
    rows = n * c + lax.broadcasted_iota(jnp.int32, (c, 1), 0)
    keep = rows >= PAD
    if b_ref is not None:
        return q, k, v, u, b_ref[:, hs], keep
    la = (jnp.minimum(u, 0.0) - jnp.log(1.0 + jnp.exp(-jnp.abs(u)))) * (1.0 / GLA_TAU)
    la = jnp.where(keep, la, 0.0)
    ii = lax.broadcasted_iota(jnp.int32, (c, c), 0)
    jj = lax.broadcasted_iota(jnp.int32, (c, c), 1)
    tril = (ii >= jj).astype(BF16)
    hi, mid, lo = _split3(la)
    b = _dot(tril, hi) + _dot(tril, mid) + _dot(tril, lo)
    return q, k, v, u, b, keep


def _gla_intra(qs, ks, bs, a_ref):
    c = CHUNK
    nh = len(qs)
    col = lax.broadcasted_iota(jnp.int32, (1, c), 1)
    rowi = lax.broadcasted_iota(jnp.int32, (SUB, 1), 0)
    for blk in range(c // SUB):
        r = slice(SUB * blk, SUB * (blk + 1))
        arows = []
        for h in range(nh):
            q, k, b = qs[h], ks[h], bs[h]
            if blk > 0:
                bprev = b[SUB * blk - 1:SUB * blk]
                qe = q[r] * jnp.exp(b[r] - bprev)
                kt = k * jnp.exp(jnp.minimum(bprev - b, 0.0))
                arows.append(jnp.where(col < SUB * blk, _dot_nt(qe.astype(BF16), kt.astype(BF16)), 0.0))
            else:
                arows.append(jnp.zeros((SUB, c), F32))
        half = SUB // 2
        lo = slice(SUB * blk + half, SUB * (blk + 1))
        tops = [a[:half] for a in arows]
        bots = [a[half:] for a in arows]
        for j in range(SUB):
            for h in range(nh):
                bj, kj = bs[h][SUB * blk + j:SUB * blk + j + 1], ks[h][SUB * blk + j:SUB * blk + j + 1]
                if j < half:
                    a = jnp.sum(qs[h][r] * kj * jnp.exp(bs[h][r] - bj), axis=1, keepdims=True)
                    tops[h] = jnp.where(col == SUB * blk + j, a[:half], tops[h])
                    bots[h] = jnp.where(col == SUB * blk + j, a[half:], bots[h])
                else:
                    a = jnp.sum(qs[h][lo] * kj * jnp.exp(bs[h][lo] - bj), axis=1, keepdims=True)
                    bots[h] = jnp.where(col == SUB * blk + j, a, bots[h])
        for h in range(nh):
            arow = jnp.concatenate([tops[h], bots[h]], axis=0)
            a_ref[h, r, :] = jnp.where(col - SUB * blk <= rowi, arow, 0.0)


def _gla_scan_fwd(proj, wgp, bg, name, side=None):
    t = proj.shape[0]
    c = CHUNK
    nc = t // c
    heads = range(GLA_H)

    def body(blk_ref, z_ref, wg_ref, bg_ref, o_ref, st_ref, am_ref, bs_ref, state, a_ref):
        n = pl.program_id(0)

        @pl.when(n == 0)
        def _():
            state[...] = jnp.zeros_like(state)

        qs, ks, vs, us, bs, keeps = zip(*[_gla_chunk(blk_ref, z_ref, wg_ref, bg_ref, n, h) for h in heads])
        _gla_intra(qs, ks, bs, a_ref)
        for h in heads:
            q, k, v, b = qs[h], ks[h], vs[h], bs[h]
            sp = state[h]
            st_ref[h] = sp.astype(BF16)
            ab = a_ref[h].astype(BF16)
            am_ref[:, h * c:(h + 1) * c] = ab
            bs_ref[:, h * GLA_DK:(h + 1) * GLA_DK] = b
            o = _dot(ab, v) + _dot_nt((q * jnp.exp(b)).astype(BF16), sp.astype(BF16))
            o_ref[:, h * GLA_DV:(h + 1) * GLA_DV] = o.astype(BF16)
            bc = b[c - 1:c]
            state[h] = sp * jnp.exp(bc) + _dot_tn(v, (k * jnp.exp(bc - b)).astype(BF16))

    return _call(
        body, name=name, grid=(nc,), side=side,
        in_specs=[pl.BlockSpec((c, GLA_QKV), lambda n: (n, 0)), pl.BlockSpec((c, 128), lambda n: (n, GLA_ZBLK)),
                  pl.BlockSpec((128, GLA_H * GLA_DK), lambda n: (0, 0)), pl.BlockSpec((1, GLA_H * GLA_DK), lambda n: (0, 0))],
        out_specs=[pl.BlockSpec((c, GLA_H * GLA_DV), lambda n: (n, 0)),
                   pl.BlockSpec((GLA_H, None, GLA_DV, GLA_DK), lambda n: (0, n, 0, 0)),
                   pl.BlockSpec((c, GLA_H * c), lambda n: (n, 0)),
                   pl.BlockSpec((c, GLA_H * GLA_DK), lambda n: (n, 0))],
        out_shape=[S((t, GLA_H * GLA_DV), BF16), S((GLA_H, nc, GLA_DV, GLA_DK), BF16), S((t, GLA_H * c), BF16),
                   S((t, GLA_H * GLA_DK), F32)],
        scratch_shapes=[pltpu.VMEM((GLA_H, GLA_DV, GLA_DK), F32), pltpu.VMEM((GLA_H, c, c), F32)],
        operands=[proj, proj, wgp, bg])


def _gla_scan_bwd(proj, wgp, bg, do, states, amat, bcum, dproj, name):
    t = proj.shape[0]
    c = CHUNK
    nc = t // c
    heads = range(GLA_H)

    def body(blk_ref, z_ref, wg_ref, bg_ref, do_ref, st_ref, am_ref, bs_ref, dp_in, dp_ref, du_ref, dstate, dq_ref, dkd_ref):
        n = nc - 1 - pl.program_id(0)

        @pl.when(pl.program_id(0) == 0)
        def _():
            dstate[...] = jnp.zeros_like(dstate)

        qs, ks, vs, us, bs, keeps = zip(*[_gla_chunk(blk_ref, z_ref, wg_ref, bg_ref, n, h, bs_ref) for h in heads])
        ii = lax.broadcasted_iota(jnp.int32, (c, c), 0)
        jj = lax.broadcasted_iota(jnp.int32, (c, c), 1)
        col = lax.broadcasted_iota(jnp.int32, (1, c), 1)
        rowi = lax.broadcasted_iota(jnp.int32, (SUB, 1), 0)
        rowc = lax.broadcasted_iota(jnp.int32, (c, 1), 0)
        das, dvs, dq_inters, dk_states, extras, dks = [], [], [], [], [], []
        for h in heads:
            q, k, v, b = qs[h], ks[h], vs[h], bs[h]
            ab = am_ref[:, h * c:(h + 1) * c]
            dob = do_ref[:, h * GLA_DV:(h + 1) * GLA_DV]
            sp = st_ref[h]
            ds = dstate[h]
            dsb = ds.astype(BF16)
            bc = b[c - 1:c]
            eb = jnp.exp(b)
            ebc = jnp.exp(bc - b)
            ec = jnp.exp(bc)
            qb = (q * eb).astype(BF16)
            kb = (k * ebc).astype(BF16)
            dvs.append(_dot_tn(ab, dob) + _dot_nt(kb, dsb))
            das.append(jnp.where(ii >= jj, _dot_nt(dob, v), 0.0))
            dq_inters.append(_dot(dob, sp) * eb)
            dk_state = _dot(v, dsb) * ebc
            dk_states.append(dk_state)
            extras.append(jnp.sum(k * dk_state, axis=0, keepdims=True)
                          + ec * jnp.sum(sp.astype(F32) * ds, axis=0, keepdims=True))
            dstate[h] = ds * ec + _dot_tn(dob, qb)
            dks.append(jnp.zeros((c, GLA_DK), F32))

        for blk in range(c // SUB):
            r = slice(SUB * blk, SUB * (blk + 1))
            dq_is, dkds = [], []
            for h in heads:
                q, k, b = qs[h], ks[h], bs[h]
                if blk > 0:
                    bprev = b[SUB * blk - 1:SUB * blk]
                    e_i = jnp.exp(b[r] - bprev)
                    ek = jnp.exp(jnp.minimum(bprev - b, 0.0))
                    daoff = jnp.where(col < SUB * blk, das[h][r], 0.0).astype(BF16)
                    dq_is.append(_dot(daoff, (k * ek).astype(BF16)) * e_i)
                    dks[h] = dks[h] + _dot_tn(daoff, (q[r] * e_i).astype(BF16)) * ek
                else:
                    dq_is.append(jnp.zeros((SUB, GLA_DK), F32))
                dkds.append(jnp.zeros((SUB, GLA_DK), F32))
            half = SUB // 2
            lo = slice(SUB * blk + half, SUB * (blk + 1))
            row8 = rowi[:half]
            dq_tops = [a[:half] for a in dq_is]
            dq_bots = [a[half:] for a in dq_is]
            for j in range(SUB):
                for h in heads:
                    bj, kj = bs[h][SUB * blk + j:SUB * blk + j + 1], ks[h][SUB * blk + j:SUB * blk + j + 1]
                    if j < half:
                        e = jnp.where(rowi >= j, jnp.exp(bs[h][r] - bj), 0.0)
                        dacol = jnp.sum(jnp.where(col == SUB * blk + j, das[h][r], 0.0), axis=1, keepdims=True)
                        tt = dacol * e
                        dq_tops[h] = dq_tops[h] + tt[:half] * kj
                        dq_bots[h] = dq_bots[h] + tt[half:] * kj
                        dkrow = jnp.sum(tt * qs[h][r], axis=0, keepdims=True)
                    else:
                        e = jnp.where(row8 + half >= j, jnp.exp(bs[h][lo] - bj), 0.0)
                        dacol = jnp.sum(jnp.where(col == SUB * blk + j, das[h][lo], 0.0), axis=1, keepdims=True)
                        tt = dacol * e
                        dq_bots[h] = dq_bots[h] + tt * kj
                        dkrow = jnp.sum(tt * qs[h][lo], axis=0, keepdims=True)
                    dkds[h] = jnp.where(rowi == j, dkrow, dkds[h])
            for h in heads:
                dq_ref[h, r, :] = jnp.concatenate([dq_tops[h], dq_bots[h]], axis=0)
                dkd_ref[h, r, :] = dkds[h]

        for h in heads:
            q, k, b, u, keep = qs[h], ks[h], bs[h], us[h], keeps[h]
            dq = dq_ref[h] + dq_inters[h]
            dk = dks[h] + dkd_ref[h] + dk_states[h]
            db = q * dq - k * dk + jnp.where(rowc == c - 1, extras[h], 0.0)
            triu = (ii <= jj).astype(BF16)
            hi, mid, lo = _split3(db)
            dla = _dot(triu, hi) + _dot(triu, mid) + _dot(triu, lo)
            du = jnp.where(keep, dla * (1.0 / GLA_TAU) / (1.0 + jnp.exp(u)), 0.0)
            du_ref[:, h * GLA_DK:(h + 1) * GLA_DK] = du.astype(BF16)
            oq, ok, ov = h * GLA_DK, GLA_H * GLA_DK + h * GLA_DK, 2 * GLA_H * GLA_DK + h * GLA_DV
            dp_ref[:, oq:oq + GLA_DK] = jnp.where(keep, dq * (GLA_DK ** -0.5), 0.0).astype(BF16)
            dp_ref[:, ok:ok + GLA_DK] = jnp.where(keep, dk, 0.0).astype(BF16)
            dp_ref[:, ov:ov + GLA_DV] = jnp.where(keep, dvs[h], 0.0).astype(BF16)

    nproj = dproj.shape[1]
    return pl.pallas_call(
        body, name=name, grid=(nc,),
        in_specs=[pl.BlockSpec((c, GLA_QKV), lambda n: (nc - 1 - n, 0)), pl.BlockSpec((c, 128), lambda n: (nc - 1 - n, GLA_ZBLK)),
                  pl.BlockSpec((128, GLA_H * GLA_DK), lambda n: (0, 0)), pl.BlockSpec((1, GLA_H * GLA_DK), lambda n: (0, 0)),
                  pl.BlockSpec((c, GLA_H * GLA_DV), lambda n: (nc - 1 - n, 0)),
                  pl.BlockSpec((GLA_H, None, GLA_DV, GLA_DK), lambda n: (0, nc - 1 - n, 0, 0)),
                  pl.BlockSpec((c, GLA_H * c), lambda n: (nc - 1 - n, 0)),
                  pl.BlockSpec((c, GLA_H * GLA_DK), lambda n: (nc - 1 - n, 0)), ANY],
        out_specs=[pl.BlockSpec((c, GLA_QKV), lambda n: (nc - 1 - n, 0)),
                   pl.BlockSpec((c, GLA_H * GLA_DK), lambda n: (nc - 1 - n, 0))],
        out_shape=[S((t, nproj), BF16), S((t, GLA_H * GLA_DK), BF16)],
        input_output_aliases={8: 0},
        scratch_shapes=[pltpu.VMEM((GLA_H, GLA_DV, GLA_DK), F32),
                        pltpu.VMEM((GLA_H, c, GLA_DK), F32), pltpu.VMEM((GLA_H, c, GLA_DK), F32)],
        compiler_params=_cp(dimension_semantics=("arbitrary",)),
    )(proj, proj, wgp, bg, do, states, amat, bcum, dproj)


def _gla_gate_bwd(du, proj, wgp, dproj, name):
    t = du.shape[0]
    tm = _row_tile(t, 704)
    w = GLA_H * GLA_DK

    def body(du_ref, z_ref, wg_ref, dp_in, dp_ref, dwg_ref, dbg_ref):
        @pl.when(pl.program_id(0) == 0)
        def _():
            dwg_ref[...] = jnp.zeros_like(dwg_ref)
            dbg_ref[...] = jnp.zeros_like(dbg_ref)

        d = du_ref[...]
        dp_ref[...] = _dot_nt(d, wg_ref[...]).astype(BF16)
        dwg_ref[...] += _dot_tn(z_ref[...], d)
        dbg_ref[0:1, :] += jnp.sum(d.astype(F32), axis=0, keepdims=True)

    return pl.pallas_call(
        body, name=name, grid=(t // tm,),
        in_specs=[pl.BlockSpec((tm, w), lambda i: (i, 0)), pl.BlockSpec((tm, 128), lambda i: (i, GLA_ZBLK)),
                  pl.BlockSpec((128, w), lambda i: (0, 0)), ANY],
        out_specs=[pl.BlockSpec((tm, 128), lambda i: (i, GLA_ZBLK)), pl.BlockSpec((128, w), lambda i: (0, 0)),
                   pl.BlockSpec((8, w), lambda i: (0, 0))],
        out_shape=[S(dproj.shape, BF16), S((128, w), F32), S((8, w), F32)],
        input_output_aliases={3: 0},
        compiler_params=_cp(dimension_semantics=("arbitrary",)),
    )(du, proj, wgp, dproj)


def _final_loss(h, gain, target, name):
    t = h.shape[0]
    tm = _row_tile(t, 704)
    nt = t // tm

    def body(h_ref, g_ref, t_hbm, dh_ref, dgain_ref, loss_ref, tbuf, sem):
        i = pl.program_id(0)
        slot = i % 2

        def fetch(step, to):
            return pltpu.make_async_copy(t_hbm.at[pl.ds(tm * step - CHUNK, tm)], tbuf.at[to], sem.at[to])

        head = pltpu.make_async_copy(t_hbm.at[pl.ds(0, tm - CHUNK)], tbuf.at[0, pl.ds(CHUNK, tm - CHUNK)], sem.at[0])

        @pl.when(i == 0)
        def _():
            dgain_ref[...] = jnp.zeros_like(dgain_ref)
            loss_ref[...] = jnp.zeros_like(loss_ref)
            tbuf[0, 0:CHUNK, :] = jnp.zeros((CHUNK, D), F32)
            head.start()

        @pl.when(i + 1 < nt)
        def _():
            fetch(i + 1, 1 - slot).start()

        @pl.when(i == 0)
        def _():
            head.wait()

        @pl.when(i > 0)
        def _():
            fetch(i, slot).wait()

        x = h_ref[...]
        gain = g_ref[...]
        r = lax.rsqrt(jnp.mean(x * x, axis=-1, keepdims=True) + EPS)
        xh = x * r
        rows = i * tm + lax.broadcasted_iota(jnp.int32, (tm, 1), 0)
        e = jnp.where(rows >= CHUNK, xh * gain - tbuf[slot], 0.0)
        loss_ref[...] += 0.5 * jnp.sum(jnp.mean(e * e, axis=-1, keepdims=True), axis=0, keepdims=True)
        dy = e * (1.0 / D)
        dgain_ref[0:1, :] += jnp.sum(dy * xh, axis=0, keepdims=True)
        dxh = dy * gain
        dh_ref[...] = r * (dxh - xh * jnp.mean(dxh * xh, axis=-1, keepdims=True))

    row = pl.BlockSpec((tm, D), lambda i: (i, 0))
    return pl.pallas_call(
        body, name=name, grid=(nt,),
        in_specs=[row, pl.BlockSpec((1, D), lambda i: (0, 0)), ANY],
        out_specs=[row, pl.BlockSpec((8, D), lambda i: (0, 0)), pl.BlockSpec((8, 128), lambda i: (0, 0))],
        out_shape=[S((t, D), F32), S((8, D), F32), S((8, 128), F32)],
        scratch_shapes=[pltpu.VMEM((2, tm, D), F32), pltpu.SemaphoreType.DMA((2,))],
        compiler_params=_cp(dimension_semantics=("arbitrary",)),
    )(h, gain, target)


def _adam_math(w, g, m, v):
    m2 = ADAM_B1 * m + (1.0 - ADAM_B1) * g
    v2 = ADAM_B2 * v + (1.0 - ADAM_B2) * (g * g)
    m_hat = m2 / (1.0 - ADAM_B1 ** ADAM_STEP)
    v_hat = v2 / (1.0 - ADAM_B2 ** ADAM_STEP)
    delta = -ADAM_LR * (m_hat / (jnp.sqrt(v_hat) + ADAM_EPS) + ADAM_WD * w)
    return delta, m2, v2


def _adamw_reduce(recvs, w, m, v, name):
    nl, r, wd = w.shape
    tr = _row_tile(r, 256) if r % 16 == 0 else r
    nr = r // tr

    def body(*refs):
        rv_refs = refs[:nl]
        w_ref, m_ref, v_ref, g_ref, d_ref, m2_ref, v2_ref = refs[nl:]
        layer = pl.program_id(0)

        def total(rv_ref):
            g = rv_ref[0].astype(F32)
            for s in range(1, N_DEV):
                g = g + rv_ref[s].astype(F32)
            return g

        g = total(rv_refs[0])
        for k in range(1, nl):
            g = jnp.where(layer == k, total(rv_refs[k]), g)
        g_ref[...] = g
        d_ref[...], m2_ref[...], v2_ref[...] = _adam_math(w_ref[...], g, m_ref[...], v_ref[...])

    def rv_spec(k):
        return pl.BlockSpec((N_DEV, tr, wd), lambda l, i: (0, jnp.where(l == k, i, jnp.where(l < k, 0, nr - 1)), 0))

    row = pl.BlockSpec((None, tr, wd), lambda l, i: (l, i, 0))
    return pl.pallas_call(
        body, name=name, grid=(nl, nr),
        in_specs=[rv_spec(k) for k in range(nl)] + [row, row, row],
        out_specs=[row] * 4, out_shape=[S((nl, r, wd), F32)] * 4,
        compiler_params=_cp(dimension_semantics=("arbitrary", "arbitrary")),
    )(*recvs, w, m, v)


def _small_reduce(parts, name):
    _, r, wd = parts.shape

    def body(p_ref, o_ref):
        g = p_ref[0]
        for s in range(1, N_DEV):
            g = g + p_ref[s]
        o_ref[...] = g

    return pl.pallas_call(body, name=name, out_shape=S((r, wd), F32), compiler_params=_cp())(parts)


def _adamw_small(w, g, m, v, name):
    def body(w_ref, g_ref, m_ref, v_ref, d_ref, m2_ref, v2_ref):
        d_ref[...], m2_ref[...], v2_ref[...] = _adam_math(w_ref[...], g_ref[...], m_ref[...], v_ref[...])

    return pl.pallas_call(body, name=name, out_shape=[S(w.shape, F32)] * 3, compiler_params=_cp())(w, g, m, v)


def _unshard_cols(g):
    return jnp.transpose(g, (1, 0, 2)).reshape(g.shape[1], N_DEV * g.shape[2])


def _my_cols(full, width):
    me = 4 * lax.axis_index("x") + 2 * lax.axis_index("y") + lax.axis_index("c")
    return lax.dynamic_slice_in_dim(full, me * width, width, axis=1)


def kernel(x, meta_tokens, norm_ffn1, ffn1_w_in, ffn1_w_out, norm_mix, norm_ffn2, ffn2_w_in, ffn2_w_out, ret_w_in, ret_head_norm, ret_w_out, gla_w_in, gla_w_gate, gla_b_gate, gla_head_norm, gla_w_out, final_norm, loss_target, m_meta_tokens, m_norm_ffn1, m_ffn1_w_in, m_ffn1_w_out, m_norm_mix, m_norm_ffn2, m_ffn2_w_in, m_ffn2_w_out, m_ret_w_in, m_ret_head_norm, m_ret_w_out, m_gla_w_in, m_gla_w_gate, m_gla_b_gate, m_gla_head_norm, m_gla_w_out, m_final_norm, v_meta_tokens, v_norm_ffn1, v_ffn1_w_in, v_ffn1_w_out, v_norm_mix, v_norm_ffn2, v_ffn2_w_in, v_ffn2_w_out, v_ret_w_in, v_ret_head_norm, v_ret_w_out, v_gla_w_in, v_gla_w_gate, v_gla_b_gate, v_gla_head_norm, v_gla_w_out, v_final_norm):
    seq = x.shape[1]
    t = seq + CHUNK
    xs = x[0]
    target = loss_target[0]

    def ffn_w(f):
        w_in, w_out = (ffn1_w_in, ffn1_w_out) if f < 2 else (ffn2_w_in, ffn2_w_out)
        return [w_in[f % 2].astype(BF16), w_out[f % 2].astype(BF16)]

    small = jnp.concatenate([meta_tokens.reshape(-1), ret_head_norm.reshape(-1), gla_w_gate.reshape(-1),
                             gla_b_gate.reshape(-1), gla_head_norm.reshape(-1)])
    n_small = small.shape[0]
    small = jnp.pad(small, (0, 32 * 128 - n_small)).reshape(32, 128)
    sg, win0, wout0 = _run_side(_Gather([small] + ffn_w(0)), "ag_first")
    sg = sg.reshape(N_DEV, 32 * 128)

    def small_cols(off, rows, width):
        return jnp.transpose(sg[:, off:off + rows * width].reshape(N_DEV, rows, width), (1, 0, 2)).reshape(rows, N_DEV * width)

    off = 0
    meta_full = small_cols(off, N_META, D // N_DEV); off += N_META * (D // N_DEV)
    ret_hn = small_cols(off, RET_H, RET_DV // N_DEV).reshape(1, RET_H * RET_DV); off += RET_H * RET_DV // N_DEV
    wgate = small_cols(off, GLA_RANK, GLA_H * GLA_DK // N_DEV); off += GLA_RANK * GLA_H * GLA_DK // N_DEV
    bgate = small_cols(off, 1, GLA_H * GLA_DK // N_DEV); off += GLA_H * GLA_DK // N_DEV
    gla_hn = small_cols(off, GLA_H, GLA_DV // N_DEV).reshape(1, GLA_H * GLA_DV)
    wgp = jnp.pad(wgate, ((0, 128 - GLA_RANK), (0, 0))).astype(BF16)

    cos, sin = _rope_tables(t)
    lgam = _ret_consts()

    h0 = jnp.concatenate([jnp.zeros((PAD, D), F32), meta_full, xs], axis=0)
    g1 = [norm_ffn1[i:i + 1] for i in range(2)]
    gm = [norm_mix[i:i + 1] for i in range(2)]
    g2 = [norm_ffn2[i:i + 1] for i in range(2)]

    (h1, xn_a0, pg_a0, pu_a0), (ret_win_g, ret_wout_g) = _ffn_fwd(
        h0, g1[0], win0, wout0, "ffn1_l0_fwd", side=_Gather([ret_w_in[0].astype(BF16), ret_w_out[0].astype(BF16)]))
    ret_win = ret_win_g
    ret_wout = ret_wout_g.reshape(RET_H * RET_DV, D)
    (rproj, rhn), (win2,) = _norm_mm(h1, gm[0], ret_win, 4 * ret_win.shape[2], "ret_proj_fwd", side=_Gather(ffn_w(2)[:1]))
    (ro, rstates), (wout2,) = _ret_scan_fwd(rproj, cos, sin, lgam, "ret_scan_fwd", side=_Gather(ffn_w(2)[1:]))
    (h2, rog), _ = _post_fwd(ro, rproj, ret_hn, ret_wout, h1, RET_H, RET_DV, "ret_post_fwd")
    (h3, xn_b0, pg_b0, pu_b0), (win1, wout1) = _ffn_fwd(h2, g2[0], win2, wout2, "ffn2_l0_fwd", side=_Gather(ffn_w(1)))
    (h4, xn_a1, pg_a1, pu_a1), (gla_win_g, gla_wout_g) = _ffn_fwd(
        h3, g1[1], win1, wout1, "ffn1_l1_fwd", side=_Gather([gla_w_in[0].astype(BF16), gla_w_out[0].astype(BF16)]))
    gla_win = _unshard_cols(gla_win_g)
    gla_win = jnp.pad(gla_win, ((0, 0), (0, GLA_N - gla_win.shape[1])))
    gla_wout = gla_wout_g.reshape(GLA_H * GLA_DV, D)
    (gproj, ghn), _ = _norm_mm(h4, gm[1], gla_win, GLA_N, "gla_proj_fwd")
    (go, gstates, gamat, gbcum), (win3, wout3) = _gla_scan_fwd(gproj, wgp, bgate, "gla_scan_fwd", side=_Gather(ffn_w(3)))
    (h5, gog), _ = _post_fwd(go, gproj, gla_hn, gla_wout, h4, GLA_H, GLA_DV, "gla_post_fwd")
    (h6, xn_b1, pg_b1, pu_b1), _ = _ffn_fwd(h5, g2[1], win3, wout3, "ffn2_l1_fwd")

    dh, dfinal, loss_blk = _final_loss(h6, final_norm.reshape(1, D), target, "final_loss")
    loss = lax.psum(loss_blk[0, 0], ("x", "y", "c"))

    def ffn_back(dh, h_in, xn, gain, pg, pu, win, wout, tag, side=None, dw_side=None):
        (dh_in, dob, dpg, dpu, act, dgain), got = _ffn_bwd(dh, h_in, gain, pg, pu, win, wout, tag + "_bwd", side=side)
        dwout = _mm_tn(act, dob[None], D, tag + "_dw_out").reshape(N_DEV, FF_SHARD // 2, D)
        if dw_side == "own_dw_out":
            dw_side = _Exchange([dwout])
        (dwin,), dw_got = _ffn_dw_in(xn, dpg, dpu, tag + "_dw_in", side=dw_side)
        return dh_in, [dwin, dwout], dgain[0], got, dw_got

    dh, dw_b1, dg2_1, _, _ = ffn_back(dh, h5, xn_b1, g2[1], pg_b1, pu_b1, win3, wout3, "ffn2_l1")

    (gdo, gdproj, gdhb, dghn), _ = _post_bwd(dh, go, gproj, gla_hn, gla_wout, GLA_H, GLA_DV, GLA_N, "gla_post_bwd")
    d_gla_wout = _mm_tn(gog[None], gdhb[None], D, "gla_dw_out").reshape(N_DEV, GLA_H * GLA_DV // N_DEV, D)
    gdproj, gdu = _gla_scan_bwd(gproj, wgp, bgate, gdo, gstates, gamat, gbcum, gdproj, "gla_scan_bwd")
    gdproj, dwg, dbg = _gla_gate_bwd(gdu, gproj, wgp, gdproj, "gla_gate_bwd")
    d_gla_win = _mm_tn(gdproj[None], ghn[None], D, "gla_dw_in", tm=640)[0]
    (dh, dgm_1), _ = _proj_bwd(gdproj, gla_win, dh, h4, gm[1], GLA_N, "gla_proj_bwd")
    n_gla_in = 2 * GLA_H * GLA_DK + 2 * GLA_H * GLA_DV + GLA_RANK
    d_gla_win = d_gla_win[:n_gla_in].reshape(N_DEV, n_gla_in // N_DEV, D)

    dh, dw_a1, dg1_1, rv_b1, rv_gla = ffn_back(dh, h3, xn_a1, g1[1], pg_a1, pu_a1, win1, wout1, "ffn1_l1",
                                               side=_Exchange(dw_b1), dw_side=_Exchange([d_gla_win, d_gla_wout]))
    dh, dw_b0, dg2_0, rv_a1, _ = ffn_back(dh, h2, xn_b0, g2[0], pg_b0, pu_b0, win2, wout2, "ffn2_l0", side=_Exchange(dw_a1))

    (rdo, rdproj, rdhb, drhn), rv_b0_out = _post_bwd(dh, ro, rproj, ret_hn, ret_wout, RET_H, RET_DV, 6 * D, "ret_post_bwd",
                                                     side=_Exchange(dw_b0[1:]))
    d_ret_wout = _mm_tn(rog[None], rdhb[None], D, "ret_dw_out", rows=DW_ROWS // 2).reshape(N_DEV, RET_H * RET_DV // N_DEV, D)
    (rdproj,), rv_b0_in = _ret_scan_bwd(rproj, cos, sin, lgam, rdo, rstates, rdproj, "ret_scan_bwd", side=_Exchange(dw_b0[:1]))
    rv_b0 = rv_b0_in + rv_b0_out
    d_ret_win = _mm_tn(rhn[None], rdproj[None], ret_win.shape[2], "ret_dw_in", shard_out=True)
    (dh, dgm_0), rv_ret_out = _proj_bwd(rdproj, ret_win, dh, h1, gm[0], 4 * ret_win.shape[2], "ret_proj_bwd", side=_Exchange([d_ret_wout]))

    dh, dw_a0, dg1_0, rv_ret_in, rv_a0_out = ffn_back(dh, h0, xn_a0, g1[0], pg_a0, pu_a0, win0, wout0, "ffn1_l0",
                                                      side=_Exchange([d_ret_win]), dw_side="own_dw_out")
    rv_ret = rv_ret_in + rv_ret_out
    grad_x = dh[CHUNK:][None]

    dmeta = dh[PAD:CHUNK]
    parts = jnp.concatenate([
        dg1_0, dg1_1, dgm_0[0], dgm_1[0], dg2_0, dg2_1, dfinal[0], dmeta.reshape(-1), drhn[0], dwg[:GLA_RANK].reshape(-1),
        dbg[0], dghn[0]])
    n_parts = parts.shape[0]
    rows = -(-n_parts // D)
    rows = -(-rows // 8) * 8
    parts = jnp.pad(parts, (0, rows * D - n_parts)).reshape(rows, D)
    rv_a0_in, parts_all = _run_side(_Both(_Exchange(dw_a0[:1]), _Gather([parts])), "xchg_last")
    rv_a0 = [rv_a0_in] + rv_a0_out
    tot = _small_reduce(parts_all, "small_grad_sum").reshape(-1)

    def adam_t(recvs, w, m, v, tag):
        outs = _adamw_reduce(recvs, *(jnp.swapaxes(a, 1, 2) for a in (w, m, v)), tag)
        return [jnp.swapaxes(o, 1, 2) for o in outs]

    u_ffn1_in = adam_t([rv_a0[0], rv_a1[0]], ffn1_w_in, m_ffn1_w_in, v_ffn1_w_in, "adam_ffn1_w_in")
    u_ffn2_in = adam_t([rv_b0[0], rv_b1[0]], ffn2_w_in, m_ffn2_w_in, v_ffn2_w_in, "adam_ffn2_w_in")
    u_ffn1_out = _adamw_reduce([rv_a0[1], rv_a1[1]], ffn1_w_out, m_ffn1_w_out, v_ffn1_w_out, "adam_ffn1_w_out")
    u_ffn2_out = _adamw_reduce([rv_b0[1], rv_b1[1]], ffn2_w_out, m_ffn2_w_out, v_ffn2_w_out, "adam_ffn2_w_out")
    u_ret_in = _adamw_reduce([rv_ret[0]], ret_w_in, m_ret_w_in, v_ret_w_in, "adam_ret_w_in")
    u_ret_out = _adamw_reduce([rv_ret[1]], ret_w_out, m_ret_w_out, v_ret_w_out, "adam_ret_w_out")
    u_gla_in = adam_t([rv_gla[0]], gla_w_in, m_gla_w_in, v_gla_w_in, "adam_gla_w_in")
    u_gla_out = _adamw_reduce([rv_gla[1]], gla_w_out, m_gla_w_out, v_gla_w_out, "adam_gla_w_out")


    off = 0
    def take(nel):
        nonlocal off
        out = tot[off:off + nel]
        off += nel
        return out

    gr_norm_ffn1 = take(2 * D).reshape(2, D)
    gr_norm_mix = take(2 * D).reshape(2, D)
    gr_norm_ffn2 = take(2 * D).reshape(2, D)
    gr_final = take(D)
    gr_meta = _my_cols(take(N_META * D).reshape(N_META, D), D // N_DEV)
    gr_ret_hn = _my_cols(take(RET_H * RET_DV).reshape(RET_H, RET_DV), RET_DV // N_DEV)[None]
    gr_wgate = _my_cols(take(GLA_RANK * GLA_H * GLA_DK).reshape(GLA_RANK, GLA_H * GLA_DK), GLA_H * GLA_DK // N_DEV)[None]
    gr_bgate = _my_cols(take(GLA_H * GLA_DK).reshape(1, GLA_H * GLA_DK), GLA_H * GLA_DK // N_DEV)
    gr_gla_hn = _my_cols(take(GLA_H * GLA_DV).reshape(GLA_H, GLA_DV), GLA_DV // N_DEV)[None]

    small_w = [meta_tokens, norm_ffn1, norm_mix, norm_ffn2, ret_head_norm, gla_w_gate, gla_b_gate, gla_head_norm, final_norm]
    small_g = [gr_meta, gr_norm_ffn1, gr_norm_mix, gr_norm_ffn2, gr_ret_hn, gr_wgate, gr_bgate, gr_gla_hn, gr_final]
    small_m = [m_meta_tokens, m_norm_ffn1, m_norm_mix, m_norm_ffn2, m_ret_head_norm, m_gla_w_gate, m_gla_b_gate, m_gla_head_norm, m_final_norm]
    small_v = [v_meta_tokens, v_norm_ffn1, v_norm_mix, v_norm_ffn2, v_ret_head_norm, v_gla_w_gate, v_gla_b_gate, v_gla_head_norm, v_final_norm]

    def pack(arrs):
        flat = jnp.concatenate([a.reshape(-1) for a in arrs])
        n = flat.shape[0]
        r = -(-n // 128)
        r = -(-r // 8) * 8
        return jnp.pad(flat, (0, r * 128 - n), constant_values=1.0).reshape(r, 128)

    sd, sm, sv = _adamw_small(pack(small_w), pack(small_g), pack(small_m), pack(small_v), "adam_small")

    def unpack(buf):
        flat = buf.reshape(-1)
        outs, o = [], 0
        for a in small_w:
            outs.append(flat[o:o + a.size].reshape(a.shape))
            o += a.size
        return outs

    us_d, us_m, us_v = unpack(sd), unpack(sm), unpack(sv)

    def ordered(k, smalls):
        return (smalls[0], smalls[1], u_ffn1_in[k], u_ffn1_out[k], smalls[2], smalls[3], u_ffn2_in[k], u_ffn2_out[k],
                u_ret_in[k], smalls[4], u_ret_out[k], u_gla_in[k], smalls[5], smalls[6], smalls[7], u_gla_out[k], smalls[8])

    return (loss, grad_x, *ordered(0, small_g), *ordered(1, us_d), *ordered(2, us_m), *ordered(3, us_v))
```

```python
import functools

import numpy as np
import jax
import jax.numpy as jnp
from jax import lax
from jax.experimental import pallas as pl
from jax.experimental.pallas import tpu as pltpu

F32 = jnp.float32
BF16 = jnp.bfloat16
S = jax.ShapeDtypeStruct
ANY = pl.BlockSpec(memory_space=pl.ANY)
MESH = pl.DeviceIdType.MESH

D = 1024
N_META = 16
CHUNK = 64
PAD = CHUNK - N_META
EPS = 1e-6
N_DEV = 8
FF_SHARD = 704
N_FF_CHUNK = 4
RET_H, RET_DK, RET_DV = 4, 256, 512
RET_QKV = RET_H * (2 * RET_DK + RET_DV)
RET_C = 192
GLA_H, GLA_DK, GLA_DV, GLA_RANK, GLA_TAU = 4, 128, 256, 16, 16.0
GLA_QKV = GLA_H * (2 * GLA_DK + GLA_DV)
GLA_N = 3200
GLA_ZBLK = 3072 // 128
SUB = 16
ROPE_BASE = 10000.0
ADAM_LR, ADAM_B1, ADAM_B2, ADAM_EPS, ADAM_WD, ADAM_STEP = 0.001, 0.9, 0.999, 1e-08, 0.01, 10
VMEM_LIMIT = 58 * 1024 * 1024
DW_ROWS = 2752


def _cp(**kw):
    return pltpu.CompilerParams(vmem_limit_bytes=VMEM_LIMIT, **kw)


def _row_tile(t, cap):
    best = 16
    for d in range(16, cap + 1, 16):
        if t % d == 0:
            best = d
    return best


def _sub_rows(tm, parts=2):
    units = tm // 16
    cuts = [16 * (units * p // parts) for p in range(parts + 1)]
    return [slice(a, b) for a, b in zip(cuts[:-1], cuts[1:]) if b > a]


def _dot(a, b):
    return jnp.dot(a, b, preferred_element_type=F32)


def _dot_nt(a, b):
    return lax.dot_general(a, b, (((1,), (1,)), ((), ())), preferred_element_type=F32)


def _dot_tn(a, b):
    return lax.dot_general(a, b, (((0,), (0,)), ((), ())), preferred_element_type=F32)


def _sigmoid(x):
    return pl.reciprocal(1.0 + jnp.exp(-x), approx=True)


def _rms_bwd(dxn, x, gain):
    r = lax.rsqrt(jnp.mean(x * x, axis=-1, keepdims=True) + EPS)
    xh = x * r
    dxh = dxn * gain
    dx = r * (dxh - xh * jnp.mean(dxh * xh, axis=-1, keepdims=True))
    return dx, jnp.sum(dxn * xh, axis=0, keepdims=True)


def _xyc():
    return lax.axis_index("x"), lax.axis_index("y"), lax.axis_index("c")


class _Gather:
    def __init__(self, xs):
        self.xs = list(xs)
        self.n = len(self.xs)

    def out_shape(self):
        return [S((N_DEV,) + a.shape, a.dtype) for a in self.xs]

    def scratch(self):
        return [pltpu.SemaphoreType.DMA((self.n, 7)), pltpu.SemaphoreType.DMA((self.n, 7)), pltpu.SemaphoreType.DMA((self.n,))]

    def phases(self, x_refs, out_refs, send_sems, recv_sems, local_sems):
        x, y, c = _xyc()
        me, sibling = (x, y, c), (x, y, 1 - c)
        chips = [(1 - x, y), (x, 1 - y), (1 - x, 1 - y)]

        def copy(t, k, block, to, src=None):
            px, py, pc = block
            dst = out_refs[t].at[4 * px + 2 * py + pc]
            return pltpu.make_async_remote_copy(
                src_ref=dst if src is None else src, dst_ref=dst,
                send_sem=send_sems.at[t, k], recv_sem=recv_sems.at[t, k], device_id=to, device_id_type=MESH)

        def own(t):
            return pltpu.make_async_copy(x_refs[t], out_refs[t].at[4 * x + 2 * y + c], local_sems.at[t])

        def first(t):
            return [copy(t, 0, me, sibling, src=x_refs[t])] + [
                copy(t, 1 + j, me, (*chip, c), src=x_refs[t]) for j, chip in enumerate(chips)]

        def passed(t):
            return [copy(t, 4 + j, (*chip, c), sibling) for j, chip in enumerate(chips)]

        def start():
            for t in range(self.n):
                own(t).start()
                for cp in first(t):
                    cp.start()

        def mid():
            for t in range(self.n):
                fw = passed(t)
                for j, chip in enumerate(chips):
                    copy(t, 1 + j, (*chip, c), me).wait_recv()
                    fw[j].start()

        def finish():
            for t in range(self.n):
                copy(t, 0, sibling, me).wait_recv()
                for j, chip in enumerate(chips):
                    copy(t, 4 + j, (*chip, 1 - c), me).wait_recv()
                for cp in first(t) + passed(t):
                    cp.wait_send()
                own(t).wait()

        return start, mid, finish


class _Exchange:
    def __init__(self, xs):
        self.xs = list(xs)
        self.n = len(self.xs)

    def out_shape(self):
        return [S(a.shape, a.dtype) for a in self.xs]

    def scratch(self):
        return [pltpu.SemaphoreType.DMA((self.n, 7)), pltpu.SemaphoreType.DMA((self.n, 7)), pltpu.SemaphoreType.DMA((self.n,))]

    def phases(self, g_refs, r_refs, send_sems, recv_sems, local_sems):
        x, y, c = _xyc()
        me = 4 * x + 2 * y + c

        def own(t):
            return pltpu.make_async_copy(g_refs[t].at[me], r_refs[t].at[me], local_sems.at[t])

        def send(t, m):
            px, py, pc = x ^ (m >> 2), y ^ ((m >> 1) & 1), c ^ (m & 1)
            return pltpu.make_async_remote_copy(
                src_ref=g_refs[t].at[4 * px + 2 * py + pc], dst_ref=r_refs[t].at[me],
                send_sem=send_sems.at[t, m - 1], recv_sem=recv_sems.at[t, m - 1],
                device_id=(px, py, pc), device_id_type=MESH)

        def arrival(t, m):
            peer = 4 * (x ^ (m >> 2)) + 2 * (y ^ ((m >> 1) & 1)) + (c ^ (m & 1))
            return pltpu.make_async_remote_copy(
                src_ref=g_refs[t].at[peer], dst_ref=r_refs[t].at[peer],
                send_sem=send_sems.at[t, m - 1], recv_sem=recv_sems.at[t, m - 1],
                device_id=(x, y, c), device_id_type=MESH)

        def start():
            for t in range(self.n):
                own(t).start()
            for m in range(1, N_DEV):
                for t in range(self.n):
                    send(t, m).start()

        def mid():
            pass

        def finish():
            for m in range(1, N_DEV):
                for t in range(self.n):
                    arrival(t, m).wait_recv()
            for m in range(1, N_DEV):
                for t in range(self.n):
                    send(t, m).wait_send()
            for t in range(self.n):
                own(t).wait()

        return start, mid, finish


class _Both:
    def __init__(self, a, b):
        self.a, self.b = a, b
        self.xs = a.xs + b.xs
        self.n = a.n + b.n

    def out_shape(self):
        return self.a.out_shape() + self.b.out_shape()

    def scratch(self):
        return self.a.scratch() + self.b.scratch()

    def phases(self, x_refs, out_refs, *sems):
        na = self.a.n
        pa = self.a.phases(x_refs[:na], out_refs[:na], *sems[:3])
        pb = self.b.phases(x_refs[na:], out_refs[na:], *sems[3:])
        return tuple((lambda f, g: (lambda: (f(), g())))(f, g) for f, g in zip(pa, pb))


def _run_side(side, name):
    n = side.n

    def body(*refs):
        start, mid, finish = side.phases(refs[:n], refs[n:2 * n], *refs[2 * n:])
        start()
        mid()
        finish()

    return list(pl.pallas_call(
        body, name=name, out_shape=side.out_shape(), in_specs=[ANY] * n, out_specs=[ANY] * n,
        scratch_shapes=side.scratch())(*side.xs))


def _grid_steps(grid):
    def ids():
        return [pl.program_id(a) for a in range(len(grid))]

    def first():
        return functools.reduce(jnp.logical_and, [i == 0 for i in ids()])

    def middle():
        i = ids()
        return functools.reduce(jnp.logical_and, [i[0] == (3 * grid[0]) // 4] + [j == 0 for j in i[1:]])

    def last():
        return functools.reduce(jnp.logical_and, [i == g - 1 for i, g in zip(ids(), grid)])

    return first, middle, last


def _call(body, *, name, grid, in_specs, out_specs, out_shape, scratch_shapes, operands, side=None, aliases=None):
    n_in, n_out, n_scr = len(in_specs), len(out_shape), len(scratch_shapes)
    full = body
    if side is not None:
        ns = side.n
        first, middle, last = _grid_steps(grid)

        def full(*refs):
            a = n_in
            ins, sins = refs[:a], refs[a:a + ns]
            a += ns
            outs, souts = refs[a:a + n_out], refs[a + n_out:a + n_out + ns]
            a += n_out + ns
            scr, sems = refs[a:a + n_scr], refs[a + n_scr:]
            start, mid, finish = side.phases(sins, souts, *sems)
            pl.when(first())(start)
            body(*ins, *outs, *scr)
            pl.when(middle())(mid)
            pl.when(last())(finish)

        in_specs = list(in_specs) + [ANY] * ns
        out_specs = list(out_specs) + [ANY] * ns
        out_shape = list(out_shape) + side.out_shape()
        scratch_shapes = list(scratch_shapes) + side.scratch()
        operands = list(operands) + side.xs
    outs = pl.pallas_call(
        full, name=name, grid=grid, in_specs=list(in_specs), out_specs=list(out_specs), out_shape=list(out_shape),
        scratch_shapes=list(scratch_shapes), input_output_aliases=aliases or {},
        compiler_params=_cp(dimension_semantics=("arbitrary",) * len(grid)),
    )(*operands)
    return list(outs[:n_out]), list(outs[n_out:])


def _ffn_fwd(h, gain, win, wout, name, side=None):
    t = h.shape[0]
    tm = _row_tile(t, 704)
    nt = t // tm

    def body(h_ref, g_ref, wg_ref, wu_ref, wo_ref, hn_ref, xn_ref, pg_ref, pu_ref, acc):
        c = pl.program_id(1)

        @pl.when(c == 0)
        def _():
            x = h_ref[...]
            r = lax.rsqrt(jnp.mean(x * x, axis=-1, keepdims=True) + EPS)
            xn_ref[...] = (x * r * g_ref[...]).astype(BF16)
            acc[...] = jnp.zeros_like(acc)

        wo = wo_ref[...].reshape(FF_SHARD, D)
        subs = _sub_rows(tm)
        gus = [(_dot(xn_ref[r, :], wg_ref[...]), _dot(xn_ref[r, :], wu_ref[...])) for r in subs]
        for r, (g, u) in zip(subs, gus):
            pg_ref[r, :] = g.astype(BF16)
            pu_ref[r, :] = u.astype(BF16)
            act = (g * _sigmoid(g) * u).astype(BF16)
            acc[r, :] += _dot(act, wo)

        @pl.when(c == N_FF_CHUNK - 1)
        def _():
            hn_ref[...] = h_ref[...] + 0.5 * acc[...]

    return _call(
        body, name=name, grid=(nt, N_FF_CHUNK), side=side,
        in_specs=[
            pl.BlockSpec((tm, D), lambda i, c: (i, 0)),
            pl.BlockSpec((1, D), lambda i, c: (0, 0)),
            pl.BlockSpec((None, D, FF_SHARD), lambda i, c: (c, 0, 0)),
            pl.BlockSpec((None, D, FF_SHARD), lambda i, c: (c + N_FF_CHUNK, 0, 0)),
            pl.BlockSpec((2, FF_SHARD // 2, D), lambda i, c: (c, 0, 0)),
        ],
        out_specs=[
            pl.BlockSpec((tm, D), lambda i, c: (i, 0)),
            pl.BlockSpec((tm, D), lambda i, c: (i, 0)),
            pl.BlockSpec((None, tm, FF_SHARD), lambda i, c: (c, i, 0)),
            pl.BlockSpec((None, tm, FF_SHARD), lambda i, c: (c, i, 0)),
        ],
        out_shape=[S((t, D), F32), S((t, D), BF16), S((N_FF_CHUNK, t, FF_SHARD), BF16), S((N_FF_CHUNK, t, FF_SHARD), BF16)],
        scratch_shapes=[pltpu.VMEM((tm, D), F32)],
        operands=[h, gain, win, win, wout])


def _ffn_bwd(dh, h, gain, pg, pu, win, wout, name, side=None):
    t = h.shape[0]
    tm = _row_tile(t, 704)
    nt = t // tm

    def body(dh_ref, h_ref, g_ref, pg_ref, pu_ref, wg_ref, wu_ref, wo_ref,
             dhi_ref, dob_ref, dpg_ref, dpu_ref, act_ref, dgain_ref, acc):
        i, c = pl.program_id(0), pl.program_id(1)

        @pl.when(c == 0)
        def _():
            dob_ref[...] = (0.5 * dh_ref[...]).astype(BF16)
            acc[...] = jnp.zeros_like(acc)

        @pl.when((i == 0) & (c == 0))
        def _():
            dgain_ref[...] = jnp.zeros_like(dgain_ref)

        wo = wo_ref[...].reshape(FF_SHARD, D)
        subs = _sub_rows(tm)
        dacts = [_dot_nt(dob_ref[r, :], wo) for r in subs]
        for r, dact in zip(subs, dacts):
            g = pg_ref[r, :].astype(F32)
            u = pu_ref[r, :].astype(F32)
            s = _sigmoid(g)
            sl = g * s
            act_ref[r, :] = (sl * u).astype(BF16)
            dg = (dact * u * (s * (1.0 + g * (1.0 - s)))).astype(BF16)
            du = (dact * sl).astype(BF16)
            dpg_ref[r, :] = dg
            dpu_ref[r, :] = du
            acc[r, :] += _dot_nt(dg, wg_ref[...]) + _dot_nt(du, wu_ref[...])

        @pl.when(c == N_FF_CHUNK - 1)
        def _():
            dx, dgn = _rms_bwd(acc[...], h_ref[...], g_ref[...])
            dhi_ref[...] = dh_ref[...] + dx
            dgain_ref[0:1, :] += dgn

    blk = pl.BlockSpec((None, tm, FF_SHARD), lambda i, c: (c, i, 0))
    row = pl.BlockSpec((tm, D), lambda i, c: (i, 0))
    return _call(
        body, name=name, grid=(nt, N_FF_CHUNK), side=side,
        in_specs=[
            row, row, pl.BlockSpec((1, D), lambda i, c: (0, 0)), blk, blk,
            pl.BlockSpec((None, D, FF_SHARD), lambda i, c: (c, 0, 0)),
            pl.BlockSpec((None, D, FF_SHARD), lambda i, c: (c + N_FF_CHUNK, 0, 0)),
            pl.BlockSpec((2, FF_SHARD // 2, D), lambda i, c: (c, 0, 0)),
        ],
        out_specs=[row, row, blk, blk, blk, pl.BlockSpec((8, D), lambda i, c: (0, 0))],
        out_shape=[S((t, D), F32), S((t, D), BF16)] + [S((N_FF_CHUNK, t, FF_SHARD), BF16)] * 3 + [S((8, D), F32)],
        scratch_shapes=[pltpu.VMEM((tm, D), F32)],
        operands=[dh, h, gain, pg, pu, win, win, wout])


def _ffn_dw_in(xn, dpg, dpu, name, side=None):
    t = xn.shape[0]
    tk = _row_tile(t, DW_ROWS)
    nk = t // tk

    def body(a_ref, bg_ref, bu_ref, o_ref, acc):
        c, k = pl.program_id(0), pl.program_id(1)

        @pl.when(k == 0)
        def _():
            acc[...] = jnp.zeros_like(acc)

        @pl.when(c < N_FF_CHUNK)
        def _():
            acc[...] += _dot_tn(bg_ref[...], a_ref[...])

        @pl.when(c >= N_FF_CHUNK)
        def _():
            acc[...] += _dot_tn(bu_ref[...], a_ref[...])

        @pl.when(k == nk - 1)
        def _():
            o_ref[...] = acc[...].astype(BF16)

    return _call(
        body, name=name, grid=(2 * N_FF_CHUNK, nk), side=side,
        in_specs=[
            pl.BlockSpec((tk, D), lambda c, k: (k, 0)),
            pl.BlockSpec((None, tk, FF_SHARD), lambda c, k: (jnp.minimum(c, N_FF_CHUNK - 1), k, 0)),
            pl.BlockSpec((None, tk, FF_SHARD), lambda c, k: (jnp.maximum(c - N_FF_CHUNK, 0), k, 0)),
        ],
        out_specs=[pl.BlockSpec((None, FF_SHARD, D), lambda c, k: (c, 0, 0))],
        out_shape=[S((2 * N_FF_CHUNK, FF_SHARD, D), BF16)],
        scratch_shapes=[pltpu.VMEM((FF_SHARD, D), F32)],
        operands=[xn, dpg, dpu])


def _mm_tn(a, b, tn, name, tm=None, rows=DW_ROWS, shard_out=False):
    ca, t, m = a.shape
    cb, _, n = b.shape
    nc = max(ca, cb)
    tm = m if tm is None else tm
    tk = _row_tile(t, rows)
    nk = t // tk

    def body(a_ref, b_ref, o_ref, acc):
        k = pl.program_id(3)

        @pl.when(k == 0)
        def _():
            acc[...] = jnp.zeros_like(acc)

        acc[...] += _dot_tn(a_ref[...], b_ref[...])

        @pl.when(k == nk - 1)
        def _():
            o_ref[...] = acc[...].astype(BF16)

    if shard_out:
        out_spec = pl.BlockSpec((None, tm, tn), lambda c, i, j, k: (j, 0, 0))
        out_shape = S((n // tn, m, tn), BF16)
    else:
        out_spec = pl.BlockSpec((None, tm, tn), lambda c, i, j, k: (c, i, j))
        out_shape = S((nc, m, n), BF16)
    return pl.pallas_call(
        body, name=name, grid=(nc, m // tm, n // tn, nk),
        in_specs=[
            pl.BlockSpec((None, tk, tm), (lambda c, i, j, k: (c, k, i)) if ca > 1 else (lambda c, i, j, k: (0, k, i))),
            pl.BlockSpec((None, tk, tn), (lambda c, i, j, k: (c, k, j)) if cb > 1 else (lambda c, i, j, k: (0, k, j))),
        ],
        out_specs=out_spec, out_shape=out_shape,
        scratch_shapes=[pltpu.VMEM((tm, tn), F32)],
        compiler_params=_cp(dimension_semantics=("arbitrary",) * 4),
    )(a, b)


def _norm_mm(h, gain, w, tn, name, side=None):
    t = h.shape[0]
    n = w.shape[-1] if w.ndim == 2 else w.shape[0] * w.shape[2]
    tm = _row_tile(t, 704)
    kb = 1 if w.ndim == 2 else tn // w.shape[2]
    w_spec = (pl.BlockSpec((D, tn), lambda i, j: (0, j)) if w.ndim == 2
              else pl.BlockSpec((kb, D, tn // kb), lambda i, j: (j, 0, 0)))

    def body(h_ref, g_ref, w_ref, o_ref, xn_ref):
        @pl.when(pl.program_id(1) == 0)
        def _():
            x = h_ref[...]
            r = lax.rsqrt(jnp.mean(x * x, axis=-1, keepdims=True) + EPS)
            xn_ref[...] = (x * r * g_ref[...]).astype(BF16)

        if w.ndim == 2:
            o_ref[...] = _dot(xn_ref[...], w_ref[...]).astype(BF16)
        else:
            for b in range(kb):
                o_ref[:, b * (tn // kb):(b + 1) * (tn // kb)] = _dot(xn_ref[...], w_ref[b]).astype(BF16)

    return _call(
        body, name=name, grid=(t // tm, n // tn), side=side,
        in_specs=[pl.BlockSpec((tm, D), lambda i, j: (i, 0)), pl.BlockSpec((1, D), lambda i, j: (0, 0)), w_spec],
        out_specs=[pl.BlockSpec((tm, tn), lambda i, j: (i, j)), pl.BlockSpec((tm, D), lambda i, j: (i, 0))],
        out_shape=[S((t, n), BF16), S((t, D), BF16)], scratch_shapes=[],
        operands=[h, gain, w])


def _proj_bwd(dproj, w, dh, h, gain, tk, name, side=None):
    t, n = dproj.shape
    tm = _row_tile(t, 704)
    nk = n // tk
    kb = 1 if w.ndim == 2 else tk // w.shape[2]
    w_spec = (pl.BlockSpec((D, tk), lambda i, k: (0, k)) if w.ndim == 2
              else pl.BlockSpec((kb, D, tk // kb), lambda i, k: (k, 0, 0)))

    def body(dp_ref, w_ref, dh_ref, h_ref, g_ref, dhi_ref, dgain_ref, acc):
        i, k = pl.program_id(0), pl.program_id(1)

        @pl.when(k == 0)
        def _():
            acc[...] = jnp.zeros_like(acc)

        @pl.when((i == 0) & (k == 0))
        def _():
            dgain_ref[...] = jnp.zeros_like(dgain_ref)

        if w.ndim == 2:
            acc[...] += _dot_nt(dp_ref[...], w_ref[...])
        else:
            for b in range(kb):
                acc[...] += _dot_nt(dp_ref[:, b * (tk // kb):(b + 1) * (tk // kb)], w_ref[b])

        @pl.when(k == nk - 1)
        def _():
            dx, dgn = _rms_bwd(acc[...], h_ref[...], g_ref[...])
            dhi_ref[...] = dh_ref[...] + dx
            dgain_ref[0:1, :] += dgn

    row = pl.BlockSpec((tm, D), lambda i, k: (i, 0))
    return _call(
        body, name=name, grid=(t // tm, nk), side=side,
        in_specs=[pl.BlockSpec((tm, tk), lambda i, k: (i, k)), w_spec,
                  row, row, pl.BlockSpec((1, D), lambda i, k: (0, 0))],
        out_specs=[row, pl.BlockSpec((8, D), lambda i, k: (0, 0))],
        out_shape=[S((t, D), F32), S((8, D), F32)],
        scratch_shapes=[pltpu.VMEM((tm, D), F32)],
        operands=[dproj, w, dh, h, gain])


def _post_fwd(o, proj, hgain, wout, h, nh, dv, name, side=None):
    t = h.shape[0]
    w = nh * dv
    tm = _row_tile(t, 704)

    def body(o_ref, g_ref, hg_ref, wo_ref, h_ref, hn_ref, og_ref):
        for rows in _sub_rows(tm):
            for hd in range(nh):
                sl = slice(hd * dv, (hd + 1) * dv)
                oh = o_ref[rows, sl].astype(F32)
                r = lax.rsqrt(jnp.mean(oh * oh, axis=-1, keepdims=True) + EPS)
                gg = g_ref[rows, sl].astype(F32)
                og_ref[rows, sl] = (oh * r * hg_ref[:, sl] * (gg * _sigmoid(gg))).astype(BF16)
            hn_ref[rows, :] = h_ref[rows, :] + _dot(og_ref[rows, :], wo_ref[...])

    return _call(
        body, name=name, grid=(t // tm,), side=side,
        in_specs=[pl.BlockSpec((tm, w), lambda i: (i, 0)), pl.BlockSpec((tm, w), lambda i: (i, 2)),
                  pl.BlockSpec((1, w), lambda i: (0, 0)), pl.BlockSpec((w, D), lambda i: (0, 0)),
                  pl.BlockSpec((tm, D), lambda i: (i, 0))],
        out_specs=[pl.BlockSpec((tm, D), lambda i: (i, 0)), pl.BlockSpec((tm, w), lambda i: (i, 0))],
        out_shape=[S((t, D), F32), S((t, w), BF16)], scratch_shapes=[],
        operands=[o, proj, hgain, wout, h])


def _post_bwd(dh, o, proj, hgain, wout, nh, dv, nproj, name, side=None):
    t = dh.shape[0]
    w = nh * dv
    tm = _row_tile(t, 704)

    def body(dh_ref, o_ref, g_ref, hg_ref, wo_ref, do_ref, dg_ref, dhb_ref, dhg_ref):
        @pl.when(pl.program_id(0) == 0)
        def _():
            dhg_ref[...] = jnp.zeros_like(dhg_ref)

        dhb_ref[...] = dh_ref[...].astype(BF16)
        subs = _sub_rows(tm)
        dogs = [_dot_nt(dhb_ref[rows, :], wo_ref[...]) for rows in subs]
        for rows, dog in zip(subs, dogs):
            for hd in range(nh):
                sl = slice(hd * dv, (hd + 1) * dv)
                oh = o_ref[rows, sl].astype(F32)
                r = lax.rsqrt(jnp.mean(oh * oh, axis=-1, keepdims=True) + EPS)
                xh = oh * r
                gain = hg_ref[:, sl]
                gg = g_ref[rows, sl].astype(F32)
                s = _sigmoid(gg)
                dogh = dog[:, sl]
                don = dogh * (gg * s)
                dg_ref[rows, sl] = (dogh * (xh * gain) * (s * (1.0 + gg * (1.0 - s)))).astype(BF16)
                dxh = don * gain
                do_ref[rows, sl] = (r * (dxh - xh * jnp.mean(dxh * xh, axis=-1, keepdims=True))).astype(BF16)
                dhg_ref[0:1, sl] += jnp.sum(don * xh, axis=0, keepdims=True)

    return _call(
        body, name=name, grid=(t // tm,), side=side,
        in_specs=[pl.BlockSpec((tm, D), lambda i: (i, 0)), pl.BlockSpec((tm, w), lambda i: (i, 0)),
                  pl.BlockSpec((tm, w), lambda i: (i, 2)), pl.BlockSpec((1, w), lambda i: (0, 0)),
                  pl.BlockSpec((w, D), lambda i: (0, 0))],
        out_specs=[pl.BlockSpec((tm, w), lambda i: (i, 0)), pl.BlockSpec((tm, w), lambda i: (i, 2)),
                   pl.BlockSpec((tm, D), lambda i: (i, 0)), pl.BlockSpec((8, w), lambda i: (0, 0))],
        out_shape=[S((t, w), BF16), S((t, nproj), BF16), S((t, D), BF16), S((8, w), F32)], scratch_shapes=[],
        operands=[dh, o, proj, hgain, wout])


def _ret_consts():
    lg = np.log1p(-np.exp2(-5.0 - np.arange(RET_H, dtype=np.float32))).astype(np.float32)
    return jnp.asarray(np.broadcast_to(lg[:, None, None], (RET_H, 1, 128)).copy())


def _rope_tables(t):
    half = RET_DK // 2
    inv = 1.0 / (ROPE_BASE ** jnp.linspace(0.0, 1.0, half, dtype=F32))
    base = (jnp.arange(t // CHUNK) * CHUNK - PAD).astype(F32)[:, None] * inv[None, :]
    off = jnp.arange(CHUNK).astype(F32)[:, None] * inv[None, :]
    ca, sa = jnp.cos(base)[:, None, :], jnp.sin(base)[:, None, :]
    cb, sb = jnp.cos(off)[None], jnp.sin(off)[None]
    return (ca * cb - sa * sb).reshape(t, half), (sa * cb + ca * sb).reshape(t, half)


def _ret_chunk(blk_ref, cos_ref, sin_ref, lg, h):
    c = RET_C
    half = RET_DK // 2
    oq, ok, ov = h * RET_DK, RET_H * RET_DK + h * RET_DK, 2 * RET_H * RET_DK + h * RET_DV
    cs, sn = cos_ref[...], sin_ref[...]
    q1, q2 = blk_ref[:, oq:oq + half].astype(F32), blk_ref[:, oq + half:oq + RET_DK].astype(F32)
    k1, k2 = blk_ref[:, ok:ok + half].astype(F32), blk_ref[:, ok + half:ok + RET_DK].astype(F32)
    qr = jnp.concatenate([q1 * cs - q2 * sn, q1 * sn + q2 * cs], axis=1)
    kr = jnp.concatenate([k1 * cs - k2 * sn, k1 * sn + k2 * cs], axis=1) * (RET_DK ** -0.5)
    v = blk_ref[:, ov:ov + RET_DV]
    ii = lax.broadcasted_iota(jnp.int32, (c, 1), 0).astype(F32)
    jj = lax.broadcasted_iota(jnp.int32, (1, c), 1).astype(F32)
    rel = ii - jj
    dmat = jnp.where(rel >= 0, jnp.exp(lg * jnp.maximum(rel, 0.0)), 0.0)
    dq = jnp.exp(lg * (ii + 1.0))
    dk = jnp.exp(lg * (c - 1.0 - ii))
    dchunk = jnp.exp(lg * float(c))
    return qr, kr, v, dmat, dq, dk, dchunk


def _ret_scan_fwd(proj, cos, sin, lgam, name, side=None):
    t = proj.shape[0]
    c = RET_C
    nc = t // c

    def body(blk_ref, cos_ref, sin_ref, lg_ref, o_ref, st_ref, state):
        @pl.when(pl.program_id(0) == 0)
        def _():
            state[...] = jnp.zeros_like(state)

        for h in range(RET_H):
            qr, kr, v, dmat, dq, dk, dchunk = _ret_chunk(blk_ref, cos_ref, sin_ref, lg_ref[h, :, 0:1], h)
            sp = state[h]
            st_ref[h] = sp.astype(BF16)
            scores = _dot_nt(qr.astype(BF16), kr.astype(BF16)) * dmat
            o = _dot(scores.astype(BF16), v) + _dot((qr * dq).astype(BF16), sp.astype(BF16))
            o_ref[:, h * RET_DV:(h + 1) * RET_DV] = o.astype(BF16)
            state[h] = sp * dchunk + _dot_tn((kr * dk).astype(BF16), v)

    return _call(
        body, name=name, grid=(nc,), side=side,
        in_specs=[pl.BlockSpec((c, RET_QKV), lambda n: (n, 0)), pl.BlockSpec((c, 128), lambda n: (n, 0)),
                  pl.BlockSpec((c, 128), lambda n: (n, 0)), pl.BlockSpec((RET_H, 1, 128), lambda n: (0, 0, 0))],
        out_specs=[pl.BlockSpec((c, RET_H * RET_DV), lambda n: (n, 0)),
                   pl.BlockSpec((RET_H, None, RET_DK, RET_DV), lambda n: (0, n, 0, 0))],
        out_shape=[S((t, RET_H * RET_DV), BF16), S((RET_H, nc, RET_DK, RET_DV), BF16)],
        scratch_shapes=[pltpu.VMEM((RET_H, RET_DK, RET_DV), F32)],
        operands=[proj, cos, sin, lgam])


def _ret_scan_bwd(proj, cos, sin, lgam, do, states, dproj, name, side=None):
    t = proj.shape[0]
    c = RET_C
    nc = t // c
    half = RET_DK // 2

    def body(blk_ref, cos_ref, sin_ref, lg_ref, do_ref, st_ref, dp_in, dp_ref, dstate):
        n = nc - 1 - pl.program_id(0)

        @pl.when(pl.program_id(0) == 0)
        def _():
            dstate[...] = jnp.zeros_like(dstate)

        cs, sn = cos_ref[...], sin_ref[...]
        rows = n * c + lax.broadcasted_iota(jnp.int32, (c, 1), 0)
        keep = rows >= PAD

        def unrot(d):
            d1, d2 = d[:, :half], d[:, half:]
            return jnp.concatenate([d1 * cs + d2 * sn, d2 * cs - d1 * sn], axis=1)

        for h in range(RET_H):
            qr, kr, v, dmat, dq, dk, dchunk = _ret_chunk(blk_ref, cos_ref, sin_ref, lg_ref[h, :, 0:1], h)
            qb, kb = qr.astype(BF16), kr.astype(BF16)
            dob = do_ref[:, h * RET_DV:(h + 1) * RET_DV]
            sp = st_ref[h]
            ds = dstate[h]
            dsb = ds.astype(BF16)
            p = (_dot_nt(qb, kb) * dmat).astype(BF16)
            dvv = _dot_tn(p, dob) + _dot((kr * dk).astype(BF16), dsb)
            dp = (_dot_nt(dob, v) * dmat).astype(BF16)
            dqr = _dot(dp, kb) + _dot_nt(dob, sp) * dq
            dkr = (_dot_tn(dp, qb) + _dot_nt(v, dsb) * dk) * (RET_DK ** -0.5)
            dstate[h] = ds * dchunk + _dot_tn((qr * dq).astype(BF16), dob)
            oq, ok, ov = h * RET_DK, RET_H * RET_DK + h * RET_DK, 2 * RET_H * RET_DK + h * RET_DV
            dp_ref[:, oq:oq + RET_DK] = jnp.where(keep, unrot(dqr), 0.0).astype(BF16)
            dp_ref[:, ok:ok + RET_DK] = jnp.where(keep, unrot(dkr), 0.0).astype(BF16)
            dp_ref[:, ov:ov + RET_DV] = jnp.where(keep, dvv, 0.0).astype(BF16)

    return _call(
        body, name=name, grid=(nc,), side=side, aliases={6: 0},
        in_specs=[pl.BlockSpec((c, RET_QKV), lambda n: (nc - 1 - n, 0)), pl.BlockSpec((c, 128), lambda n: (nc - 1 - n, 0)),
                  pl.BlockSpec((c, 128), lambda n: (nc - 1 - n, 0)), pl.BlockSpec((RET_H, 1, 128), lambda n: (0, 0, 0)),
                  pl.BlockSpec((c, RET_H * RET_DV), lambda n: (nc - 1 - n, 0)),
                  pl.BlockSpec((RET_H, None, RET_DK, RET_DV), lambda n: (0, nc - 1 - n, 0, 0)), ANY],
        out_specs=[pl.BlockSpec((c, RET_QKV), lambda n: (nc - 1 - n, 0))],
        out_shape=[S((t, dproj.shape[1]), BF16)],
        scratch_shapes=[pltpu.VMEM((RET_H, RET_DK, RET_DV), F32)],
        operands=[proj, cos, sin, lgam, do, states, dproj])


def _split3(x):
    hi = x.astype(BF16)
    r1 = x - hi.astype(F32)
    mid = r1.astype(BF16)
    lo = (r1 - mid.astype(F32)).astype(BF16)
    return hi, mid, lo


def _gla_chunk(blk_ref, z_ref, wg_ref, bg_ref, n, h, b_ref=None):
    c = CHUNK
    oq, ok, ov = h * GLA_DK, GLA_H * GLA_DK + h * GLA_DK, 2 * GLA_H * GLA_DK + h * GLA_DV
    q = blk_ref[:, oq:oq + GLA_DK].astype(F32) * (GLA_DK ** -0.5)
    k = blk_ref[:, ok:ok + GLA_DK].astype(F32)
    v = blk_ref[:, ov:ov + GLA_DV]
    hs = slice(h * GLA_DK, (h + 1) * GLA_DK)
    u = _dot(z_ref[...], wg_ref[:, hs]) + bg_ref[:, hs]
    rows = n * c + lax.broadcasted_iota(jnp.int32, (c, 1), 0)
    keep = rows >= PAD
    if b_ref is not None:
        return q, k, v, u, b_ref[:, hs], keep
    la = (jnp.minimum(u, 0.0) - jnp.log(1.0 + jnp.exp(-jnp.abs(u)))) * (1.0 / GLA_TAU)
    la = jnp.where(keep, la, 0.0)
    ii = lax.broadcasted_iota(jnp.int32, (c, c), 0)
    jj = lax.broadcasted_iota(jnp.int32, (c, c), 1)
    tril = (ii >= jj).astype(BF16)
    hi, mid, lo = _split3(la)
    b = _dot(tril, hi) + _dot(tril, mid) + _dot(tril, lo)
    return q, k, v, u, b, keep


def _gla_intra(qs, ks, bs, a_ref):
    c = CHUNK
    nh = len(qs)
    col = lax.broadcasted_iota(jnp.int32, (1, c), 1)
    rowi = lax.broadcasted_iota(jnp.int32, (SUB, 1), 0)
    for blk in range(c // SUB):
        r = slice(SUB * blk, SUB * (blk + 1))
        arows = []
        for h in range(nh):
            q, k, b = qs[h], ks[h], bs[h]
            if blk > 0:
                bprev = b[SUB * blk - 1:SUB * blk]
                qe = q[r] * jnp.exp(b[r] - bprev)
                kt = k * jnp.exp(jnp.minimum(bprev - b, 0.0))
                arows.append(jnp.where(col < SUB * blk, _dot_nt(qe.astype(BF16), kt.astype(BF16)), 0.0))
            else:
                arows.append(jnp.zeros((SUB, c), F32))
        half = SUB // 2
        lo = slice(SUB * blk + half, SUB * (blk + 1))
        tops = [a[:half] for a in arows]
        bots = [a[half:] for a in arows]
        for j in range(SUB):
            for h in range(nh):
                bj, kj = bs[h][SUB * blk + j:SUB * blk + j + 1], ks[h][SUB * blk + j:SUB * blk + j + 1]
                if j < half:
                    a = jnp.sum(qs[h][r] * kj * jnp.exp(bs[h][r] - bj), axis=1, keepdims=True)
                    tops[h] = jnp.where(col == SUB * blk + j, a[:half], tops[h])
                    bots[h] = jnp.where(col == SUB * blk + j, a[half:], bots[h])
                else:
                    a = jnp.sum(qs[h][lo] * kj * jnp.exp(bs[h][lo] - bj), axis=1, keepdims=True)
                    bots[h] = jnp.where(col == SUB * blk + j, a, bots[h])
        for h in range(nh):
            arow = jnp.concatenate([tops[h], bots[h]], axis=0)
            a_ref[h, r, :] = jnp.where(col - SUB * blk <= rowi, arow, 0.0)


def _gla_scan_fwd(proj, wgp, bg, name, side=None):
    t = proj.shape[0]
    c = CHUNK
    nc = t // c
    heads = range(GLA_H)

    def body(blk_ref, z_ref, wg_ref, bg_ref, o_ref, st_ref, am_ref, bs_ref, state, a_ref):
        n = pl.program_id(0)

        @pl.when(n == 0)
        def _():
            state[...] = jnp.zeros_like(state)

        qs, ks, vs, us, bs, keeps = zip(*[_gla_chunk(blk_ref, z_ref, wg_ref, bg_ref, n, h) for h in heads])
        _gla_intra(qs, ks, bs, a_ref)
        for h in heads:
            q, k, v, b = qs[h], ks[h], vs[h], bs[h]
            sp = state[h]
            st_ref[h] = sp.astype(BF16)
            ab = a_ref[h].astype(BF16)
            am_ref[:, h * c:(h + 1) * c] = ab
            bs_ref[:, h * GLA_DK:(h + 1) * GLA_DK] = b
            o = _dot(ab, v) + _dot_nt((q * jnp.exp(b)).astype(BF16), sp.astype(BF16))
            o_ref[:, h * GLA_DV:(h + 1) * GLA_DV] = o.astype(BF16)
            bc = b[c - 1:c]
            state[h] = sp * jnp.exp(bc) + _dot_tn(v, (k * jnp.exp(bc - b)).astype(BF16))

    return _call(
        body, name=name, grid=(nc,), side=side,
        in_specs=[pl.BlockSpec((c, GLA_QKV), lambda n: (n, 0)), pl.BlockSpec((c, 128), lambda n: (n, GLA_ZBLK)),
                  pl.BlockSpec((128, GLA_H * GLA_DK), lambda n: (0, 0)), pl.BlockSpec((1, GLA_H * GLA_DK), lambda n: (0, 0))],
        out_specs=[pl.BlockSpec((c, GLA_H * GLA_DV), lambda n: (n, 0)),
                   pl.BlockSpec((GLA_H, None, GLA_DV, GLA_DK), lambda n: (0, n, 0, 0)),
                   pl.BlockSpec((c, GLA_H * c), lambda n: (n, 0)),
                   pl.BlockSpec((c, GLA_H * GLA_DK), lambda n: (n, 0))],
        out_shape=[S((t, GLA_H * GLA_DV), BF16), S((GLA_H, nc, GLA_DV, GLA_DK), BF16), S((t, GLA_H * c), BF16),
                   S((t, GLA_H * GLA_DK), F32)],
        scratch_shapes=[pltpu.VMEM((GLA_H, GLA_DV, GLA_DK), F32), pltpu.VMEM((GLA_H, c, c), F32)],
        operands=[proj, proj, wgp, bg])


def _gla_scan_bwd(proj, wgp, bg, do, states, amat, bcum, dproj, name):
    t = proj.shape[0]
    c = CHUNK
    nc = t // c
    heads = range(GLA_H)

    def body(blk_ref, z_ref, wg_ref, bg_ref, do_ref, st_ref, am_ref, bs_ref, dp_in, dp_ref, du_ref, dstate, dq_ref, dkd_ref):
        n = nc - 1 - pl.program_id(0)

        @pl.when(pl.program_id(0) == 0)
        def _():
            dstate[...] = jnp.zeros_like(dstate)

        qs, ks, vs, us, bs, keeps = zip(*[_gla_chunk(blk_ref, z_ref, wg_ref, bg_ref, n, h, bs_ref) for h in heads])
        ii = lax.broadcasted_iota(jnp.int32, (c, c), 0)
        jj = lax.broadcasted_iota(jnp.int32, (c, c), 1)
        col = lax.broadcasted_iota(jnp.int32, (1, c), 1)
        rowi = lax.broadcasted_iota(jnp.int32, (SUB, 1), 0)
        rowc = lax.broadcasted_iota(jnp.int32, (c, 1), 0)
        das, dvs, dq_inters, dk_states, extras, dks = [], [], [], [], [], []
        for h in heads:
            q, k, v, b = qs[h], ks[h], vs[h], bs[h]
            ab = am_ref[:, h * c:(h + 1) * c]
            dob = do_ref[:, h * GLA_DV:(h + 1) * GLA_DV]
            sp = st_ref[h]
            ds = dstate[h]
            dsb = ds.astype(BF16)
            bc = b[c - 1:c]
            eb = jnp.exp(b)
            ebc = jnp.exp(bc - b)
            ec = jnp.exp(bc)
            qb = (q * eb).astype(BF16)
            kb = (k * ebc).astype(BF16)
            dvs.append(_dot_tn(ab, dob) + _dot_nt(kb, dsb))
            das.append(jnp.where(ii >= jj, _dot_nt(dob, v), 0.0))
            dq_inters.append(_dot(dob, sp) * eb)
            dk_state = _dot(v, dsb) * ebc
            dk_states.append(dk_state)
            extras.append(jnp.sum(k * dk_state, axis=0, keepdims=True)
                          + ec * jnp.sum(sp.astype(F32) * ds, axis=0, keepdims=True))
            dstate[h] = ds * ec + _dot_tn(dob, qb)
            dks.append(jnp.zeros((c, GLA_DK), F32))

        for blk in range(c // SUB):
            r = slice(SUB * blk, SUB * (blk + 1))
            dq_is, dkds = [], []
            for h in heads:
                q, k, b = qs[h], ks[h], bs[h]
                if blk > 0:
                    bprev = b[SUB * blk - 1:SUB * blk]
                    e_i = jnp.exp(b[r] - bprev)
                    ek = jnp.exp(jnp.minimum(bprev - b, 0.0))
                    daoff = jnp.where(col < SUB * blk, das[h][r], 0.0).astype(BF16)
                    dq_is.append(_dot(daoff, (k * ek).astype(BF16)) * e_i)
                    dks[h] = dks[h] + _dot_tn(daoff, (q[r] * e_i).astype(BF16)) * ek
                else:
                    dq_is.append(jnp.zeros((SUB, GLA_DK), F32))
                dkds.append(jnp.zeros((SUB, GLA_DK), F32))
            half = SUB // 2
            lo = slice(SUB * blk + half, SUB * (blk + 1))
            row8 = rowi[:half]
            dq_tops = [a[:half] for a in dq_is]
            dq_bots = [a[half:] for a in dq_is]
            for j in range(SUB):
                for h in heads:
                    bj, kj = bs[h][SUB * blk + j:SUB * blk + j + 1], ks[h][SUB * blk + j:SUB * blk + j + 1]
                    if j < half:
                        e = jnp.where(rowi >= j, jnp.exp(bs[h][r] - bj), 0.0)
                        dacol = jnp.sum(jnp.where(col == SUB * blk + j, das[h][r], 0.0), axis=1, keepdims=True)
                        tt = dacol * e
                        dq_tops[h] = dq_tops[h] + tt[:half] * kj
                        dq_bots[h] = dq_bots[h] + tt[half:] * kj
                        dkrow = jnp.sum(tt * qs[h][r], axis=0, keepdims=True)
                    else:
                        e = jnp.where(row8 + half >= j, jnp.exp(bs[h][lo] - bj), 0.0)
                        dacol = jnp.sum(jnp.where(col == SUB * blk + j, das[h][lo], 0.0), axis=1, keepdims=True)
                        tt = dacol * e
                        dq_bots[h] = dq_bots[h] + tt * kj
                        dkrow = jnp.sum(tt * qs[h][lo], axis=0, keepdims=True)
                    dkds[h] = jnp.where(rowi == j, dkrow, dkds[h])
            for h in heads:
                dq_ref[h, r, :] = jnp.concatenate([dq_tops[h], dq_bots[h]], axis=0)
                dkd_ref[h, r, :] = dkds[h]

        for h in heads:
            q, k, b, u, keep = qs[h], ks[h], bs[h], us[h], keeps[h]
            dq = dq_ref[h] + dq_inters[h]
            dk = dks[h] + dkd_ref[h] + dk_states[h]
            db = q * dq - k * dk + jnp.where(rowc == c - 1, extras[h], 0.0)
            triu = (ii <= jj).astype(BF16)
            hi, mid, lo = _split3(db)
            dla = _dot(triu, hi) + _dot(triu, mid) + _dot(triu, lo)
            du = jnp.where(keep, dla * (1.0 / GLA_TAU) / (1.0 + jnp.exp(u)), 0.0)
            du_ref[:, h * GLA_DK:(h + 1) * GLA_DK] = du.astype(BF16)
            oq, ok, ov = h * GLA_DK, GLA_H * GLA_DK + h * GLA_DK, 2 * GLA_H * GLA_DK + h * GLA_DV
            dp_ref[:, oq:oq + GLA_DK] = jnp.where(keep, dq * (GLA_DK ** -0.5), 0.0).astype(BF16)
            dp_ref[:, ok:ok + GLA_DK] = jnp.where(keep, dk, 0.0).astype(BF16)
            dp_ref[:, ov:ov + GLA_DV] = jnp.where(keep, dvs[h], 0.0).astype(BF16)

    nproj = dproj.shape[1]
    return pl.pallas_call(
        body, name=name, grid=(nc,),
        in_specs=[pl.BlockSpec((c, GLA_QKV), lambda n: (nc - 1 - n, 0)), pl.BlockSpec((c, 128), lambda n: (nc - 1 - n, GLA_ZBLK)),
                  pl.BlockSpec((128, GLA_H * GLA_DK), lambda n: (0, 0)), pl.BlockSpec((1, GLA_H * GLA_DK), lambda n: (0, 0)),
                  pl.BlockSpec((c, GLA_H * GLA_DV), lambda n: (nc - 1 - n, 0)),
                  pl.BlockSpec((GLA_H, None, GLA_DV, GLA_DK), lambda n: (0, nc - 1 - n, 0, 0)),
                  pl.BlockSpec((c, GLA_H * c), lambda n: (nc - 1 - n, 0)),
                  pl.BlockSpec((c, GLA_H * GLA_DK), lambda n: (nc - 1 - n, 0)), ANY],
        out_specs=[pl.BlockSpec((c, GLA_QKV), lambda n: (nc - 1 - n, 0)),
                   pl.BlockSpec((c, GLA_H * GLA_DK), lambda n: (nc - 1 - n, 0))],
        out_shape=[S((t, nproj), BF16), S((t, GLA_H * GLA_DK), BF16)],
        input_output_aliases={8: 0},
        scratch_shapes=[pltpu.VMEM((GLA_H, GLA_DV, GLA_DK), F32),
                        pltpu.VMEM((GLA_H, c, GLA_DK), F32), pltpu.VMEM((GLA_H, c, GLA_DK), F32)],
        compiler_params=_cp(dimension_semantics=("arbitrary",)),
    )(proj, proj, wgp, bg, do, states, amat, bcum, dproj)


def _gla_gate_bwd(du, proj, wgp, dproj, name):
    t = du.shape[0]
    tm = _row_tile(t, 704)
    w = GLA_H * GLA_DK

    def body(du_ref, z_ref, wg_ref, dp_in, dp_ref, dwg_ref, dbg_ref):
        @pl.when(pl.program_id(0) == 0)
        def _():
            dwg_ref[...] = jnp.zeros_like(dwg_ref)
            dbg_ref[...] = jnp.zeros_like(dbg_ref)

        d = du_ref[...]
        dp_ref[...] = _dot_nt(d, wg_ref[...]).astype(BF16)
        dwg_ref[...] += _dot_tn(z_ref[...], d)
        dbg_ref[0:1, :] += jnp.sum(d.astype(F32), axis=0, keepdims=True)

    return pl.pallas_call(
        body, name=name, grid=(t // tm,),
        in_specs=[pl.BlockSpec((tm, w), lambda i: (i, 0)), pl.BlockSpec((tm, 128), lambda i: (i, GLA_ZBLK)),
                  pl.BlockSpec((128, w), lambda i: (0, 0)), ANY],
        out_specs=[pl.BlockSpec((tm, 128), lambda i: (i, GLA_ZBLK)), pl.BlockSpec((128, w), lambda i: (0, 0)),
                   pl.BlockSpec((8, w), lambda i: (0, 0))],
        out_shape=[S(dproj.shape, BF16), S((128, w), F32), S((8, w), F32)],
        input_output_aliases={3: 0},
        compiler_params=_cp(dimension_semantics=("arbitrary",)),
    )(du, proj, wgp, dproj)


def _final_loss(h, gain, target, name):
    t = h.shape[0]
    tm = _row_tile(t, 704)
    nt = t // tm

    def body(h_ref, g_ref, t_hbm, dh_ref, dgain_ref, loss_ref, tbuf, sem):
        i = pl.program_id(0)
        slot = i % 2

        def fetch(step, to):
            return pltpu.make_async_copy(t_hbm.at[pl.ds(tm * step - CHUNK, tm)], tbuf.at[to], sem.at[to])

        head = pltpu.make_async_copy(t_hbm.at[pl.ds(0, tm - CHUNK)], tbuf.at[0, pl.ds(CHUNK, tm - CHUNK)], sem.at[0])

        @pl.when(i == 0)
        def _():
            dgain_ref[...] = jnp.zeros_like(dgain_ref)
            loss_ref[...] = jnp.zeros_like(loss_ref)
            tbuf[0, 0:CHUNK, :] = jnp.zeros((CHUNK, D), F32)
            head.start()

        @pl.when(i + 1 < nt)
        def _():
            fetch(i + 1, 1 - slot).start()

        @pl.when(i == 0)
        def _():
            head.wait()

        @pl.when(i > 0)
        def _():
            fetch(i, slot).wait()

        x = h_ref[...]
        gain = g_ref[...]
        r = lax.rsqrt(jnp.mean(x * x, axis=-1, keepdims=True) + EPS)
        xh = x * r
        rows = i * tm + lax.broadcasted_iota(jnp.int32, (tm, 1), 0)
        e = jnp.where(rows >= CHUNK, xh * gain - tbuf[slot], 0.0)
        loss_ref[...] += 0.5 * jnp.sum(jnp.mean(e * e, axis=-1, keepdims=True), axis=0, keepdims=True)
        dy = e * (1.0 / D)
        dgain_ref[0:1, :] += jnp.sum(dy * xh, axis=0, keepdims=True)
        dxh = dy * gain
        dh_ref[...] = r * (dxh - xh * jnp.mean(dxh * xh, axis=-1, keepdims=True))

    row = pl.BlockSpec((tm, D), lambda i: (i, 0))
    return pl.pallas_call(
        body, name=name, grid=(nt,),
        in_specs=[row, pl.BlockSpec((1, D), lambda i: (0, 0)), ANY],
        out_specs=[row, pl.BlockSpec((8, D), lambda i: (0, 0)), pl.BlockSpec((8, 128), lambda i: (0, 0))],
        out_shape=[S((t, D), F32), S((8, D), F32), S((8, 128), F32)],
        scratch_shapes=[pltpu.VMEM((2, tm, D), F32), pltpu.SemaphoreType.DMA((2,))],
        compiler_params=_cp(dimension_semantics=("arbitrary",)),
    )(h, gain, target)


def _adam_math(w, g, m, v):
    m2 = ADAM_B1 * m + (1.0 - ADAM_B1) * g
    v2 = ADAM_B2 * v + (1.0 - ADAM_B2) * (g * g)
    m_hat = m2 / (1.0 - ADAM_B1 ** ADAM_STEP)
    v_hat = v2 / (1.0 - ADAM_B2 ** ADAM_STEP)
    delta = -ADAM_LR * (m_hat / (jnp.sqrt(v_hat) + ADAM_EPS) + ADAM_WD * w)
    return delta, m2, v2


def _adamw_reduce(recvs, w, m, v, name):
    nl, r, wd = w.shape
    tr = _row_tile(r, 512) if r % 16 == 0 else r
    nr = r // tr

    def body(*refs):
        rv_refs = refs[:nl]
        w_ref, m_ref, v_ref, g_ref, d_ref, m2_ref, v2_ref = refs[nl:]
        layer = pl.program_id(0)

        def total(rv_ref):
            g = rv_ref[0].astype(F32)
            for s in range(1, N_DEV):
                g = g + rv_ref[s].astype(F32)
            return g

        g = total(rv_refs[0])
        for k in range(1, nl):
            g = jnp.where(layer == k, total(rv_refs[k]), g)
        g_ref[...] = g
        d_ref[...], m2_ref[...], v2_ref[...] = _adam_math(w_ref[...], g, m_ref[...], v_ref[...])

    def rv_spec(k):
        return pl.BlockSpec((N_DEV, tr, wd), lambda l, i: (0, jnp.where(l == k, i, jnp.where(l < k, 0, nr - 1)), 0))

    row = pl.BlockSpec((None, tr, wd), lambda l, i: (l, i, 0))
    return pl.pallas_call(
        body, name=name, grid=(nl, nr),
        in_specs=[rv_spec(k) for k in range(nl)] + [row, row, row],
        out_specs=[row] * 4, out_shape=[S((nl, r, wd), F32)] * 4,
        compiler_params=_cp(dimension_semantics=("arbitrary", "arbitrary")),
    )(*recvs, w, m, v)


def _small_reduce(parts, name):
    _, r, wd = parts.shape

    def body(p_ref, o_ref):
        g = p_ref[0]
        for s in range(1, N_DEV):
            g = g + p_ref[s]
        o_ref[...] = g

    return pl.pallas_call(body, name=name, out_shape=S((r, wd), F32), compiler_params=_cp())(parts)


def _adamw_small(w, g, m, v, name):
    def body(w_ref, g_ref, m_ref, v_ref, d_ref, m2_ref, v2_ref):
        d_ref[...], m2_ref[...], v2_ref[...] = _adam_math(w_ref[...], g_ref[...], m_ref[...], v_ref[...])

    return pl.pallas_call(body, name=name, out_shape=[S(w.shape, F32)] * 3, compiler_params=_cp())(w, g, m, v)


def _unshard_cols(g):
    return jnp.transpose(g, (1, 0, 2)).reshape(g.shape[1], N_DEV * g.shape[2])


def _my_cols(full, width):
    me = 4 * lax.axis_index("x") + 2 * lax.axis_index("y") + lax.axis_index("c")
    return lax.dynamic_slice_in_dim(full, me * width, width, axis=1)


def kernel(x, meta_tokens, norm_ffn1, ffn1_w_in, ffn1_w_out, norm_mix, norm_ffn2, ffn2_w_in, ffn2_w_out, ret_w_in, ret_head_norm, ret_w_out, gla_w_in, gla_w_gate, gla_b_gate, gla_head_norm, gla_w_out, final_norm, loss_target, m_meta_tokens, m_norm_ffn1, m_ffn1_w_in, m_ffn1_w_out, m_norm_mix, m_norm_ffn2, m_ffn2_w_in, m_ffn2_w_out, m_ret_w_in, m_ret_head_norm, m_ret_w_out, m_gla_w_in, m_gla_w_gate, m_gla_b_gate, m_gla_head_norm, m_gla_w_out, m_final_norm, v_meta_tokens, v_norm_ffn1, v_ffn1_w_in, v_ffn1_w_out, v_norm_mix, v_norm_ffn2, v_ffn2_w_in, v_ffn2_w_out, v_ret_w_in, v_ret_head_norm, v_ret_w_out, v_gla_w_in, v_gla_w_gate, v_gla_b_gate, v_gla_head_norm, v_gla_w_out, v_final_norm):
    seq = x.shape[1]
    t = seq + CHUNK
    xs = x[0]
    target = loss_target[0]

    def ffn_w(f):
        w_in, w_out = (ffn1_w_in, ffn1_w_out) if f < 2 else (ffn2_w_in, ffn2_w_out)
        return [w_in[f % 2].astype(BF16), w_out[f % 2].astype(BF16)]

    small = jnp.concatenate([meta_tokens.reshape(-1), ret_head_norm.reshape(-1), gla_w_gate.reshape(-1),
                             gla_b_gate.reshape(-1), gla_head_norm.reshape(-1)])
    n_small = small.shape[0]
    small = jnp.pad(small, (0, 32 * 128 - n_small)).reshape(32, 128)
    sg, win0, wout0 = _run_side(_Gather([small] + ffn_w(0)), "ag_first")
    sg = sg.reshape(N_DEV, 32 * 128)

    def small_cols(off, rows, width):
        return jnp.transpose(sg[:, off:off + rows * width].reshape(N_DEV, rows, width), (1, 0, 2)).reshape(rows, N_DEV * width)

    off = 0
    meta_full = small_cols(off, N_META, D // N_DEV); off += N_META * (D // N_DEV)
    ret_hn = small_cols(off, RET_H, RET_DV // N_DEV).reshape(1, RET_H * RET_DV); off += RET_H * RET_DV // N_DEV
    wgate = small_cols(off, GLA_RANK, GLA_H * GLA_DK // N_DEV); off += GLA_RANK * GLA_H * GLA_DK // N_DEV
    bgate = small_cols(off, 1, GLA_H * GLA_DK // N_DEV); off += GLA_H * GLA_DK // N_DEV
    gla_hn = small_cols(off, GLA_H, GLA_DV // N_DEV).reshape(1, GLA_H * GLA_DV)
    wgp = jnp.pad(wgate, ((0, 128 - GLA_RANK), (0, 0))).astype(BF16)

    cos, sin = _rope_tables(t)
    lgam = _ret_consts()

    h0 = jnp.concatenate([jnp.zeros((PAD, D), F32), meta_full, xs], axis=0)
    g1 = [norm_ffn1[i:i + 1] for i in range(2)]
    gm = [norm_mix[i:i + 1] for i in range(2)]
    g2 = [norm_ffn2[i:i + 1] for i in range(2)]

    (h1, xn_a0, pg_a0, pu_a0), (ret_win_g, ret_wout_g) = _ffn_fwd(
        h0, g1[0], win0, wout0, "ffn1_l0_fwd", side=_Gather([ret_w_in[0].astype(BF16), ret_w_out[0].astype(BF16)]))
    ret_win = ret_win_g
    ret_wout = ret_wout_g.reshape(RET_H * RET_DV, D)
    (rproj, rhn), (win2,) = _norm_mm(h1, gm[0], ret_win, 4 * ret_win.shape[2], "ret_proj_fwd", side=_Gather(ffn_w(2)[:1]))
    (ro, rstates), (wout2,) = _ret_scan_fwd(rproj, cos, sin, lgam, "ret_scan_fwd", side=_Gather(ffn_w(2)[1:]))
    (h2, rog), _ = _post_fwd(ro, rproj, ret_hn, ret_wout, h1, RET_H, RET_DV, "ret_post_fwd")
    (h3, xn_b0, pg_b0, pu_b0), (win1, wout1) = _ffn_fwd(h2, g2[0], win2, wout2, "ffn2_l0_fwd", side=_Gather(ffn_w(1)))
    (h4, xn_a1, pg_a1, pu_a1), (gla_win_g, gla_wout_g) = _ffn_fwd(
        h3, g1[1], win1, wout1, "ffn1_l1_fwd", side=_Gather([gla_w_in[0].astype(BF16), gla_w_out[0].astype(BF16)]))
    gla_win = _unshard_cols(gla_win_g)
    gla_win = jnp.pad(gla_win, ((0, 0), (0, GLA_N - gla_win.shape[1])))
    gla_wout = gla_wout_g.reshape(GLA_H * GLA_DV, D)
    (gproj, ghn), _ = _norm_mm(h4, gm[1], gla_win, GLA_N, "gla_proj_fwd")
    (go, gstates, gamat, gbcum), (win3, wout3) = _gla_scan_fwd(gproj, wgp, bgate, "gla_scan_fwd", side=_Gather(ffn_w(3)))
    (h5, gog), _ = _post_fwd(go, gproj, gla_hn, gla_wout, h4, GLA_H, GLA_DV, "gla_post_fwd")
    (h6, xn_b1, pg_b1, pu_b1), _ = _ffn_fwd(h5, g2[1], win3, wout3, "ffn2_l1_fwd")

    dh, dfinal, loss_blk = _final_loss(h6, final_norm.reshape(1, D), target, "final_loss")
    loss = lax.psum(loss_blk[0, 0], ("x", "y", "c"))

    def ffn_back(dh, h_in, xn, gain, pg, pu, win, wout, tag, side=None, dw_side=None):
        (dh_in, dob, dpg, dpu, act, dgain), got = _ffn_bwd(dh, h_in, gain, pg, pu, win, wout, tag + "_bwd", side=side)
        dwout = _mm_tn(act, dob[None], D, tag + "_dw_out").reshape(N_DEV, FF_SHARD // 2, D)
        if dw_side == "own_dw_out":
            dw_side = _Exchange([dwout])
        (dwin,), dw_got = _ffn_dw_in(xn, dpg, dpu, tag + "_dw_in", side=dw_side)
        return dh_in, [dwin, dwout], dgain[0], got, dw_got

    dh, dw_b1, dg2_1, _, _ = ffn_back(dh, h5, xn_b1, g2[1], pg_b1, pu_b1, win3, wout3, "ffn2_l1")

    (gdo, gdproj, gdhb, dghn), _ = _post_bwd(dh, go, gproj, gla_hn, gla_wout, GLA_H, GLA_DV, GLA_N, "gla_post_bwd")
    d_gla_wout = _mm_tn(gog[None], gdhb[None], D, "gla_dw_out").reshape(N_DEV, GLA_H * GLA_DV // N_DEV, D)
    gdproj, gdu = _gla_scan_bwd(gproj, wgp, bgate, gdo, gstates, gamat, gbcum, gdproj, "gla_scan_bwd")
    gdproj, dwg, dbg = _gla_gate_bwd(gdu, gproj, wgp, gdproj, "gla_gate_bwd")
    d_gla_win = _mm_tn(gdproj[None], ghn[None], D, "gla_dw_in", tm=640)[0]
    (dh, dgm_1), _ = _proj_bwd(gdproj, gla_win, dh, h4, gm[1], GLA_N, "gla_proj_bwd")
    n_gla_in = 2 * GLA_H * GLA_DK + 2 * GLA_H * GLA_DV + GLA_RANK
    d_gla_win = d_gla_win[:n_gla_in].reshape(N_DEV, n_gla_in // N_DEV, D)

    dh, dw_a1, dg1_1, rv_b1, rv_gla = ffn_back(dh, h3, xn_a1, g1[1], pg_a1, pu_a1, win1, wout1, "ffn1_l1",
                                               side=_Exchange(dw_b1), dw_side=_Exchange([d_gla_win, d_gla_wout]))
    dh, dw_b0, dg2_0, rv_a1, _ = ffn_back(dh, h2, xn_b0, g2[0], pg_b0, pu_b0, win2, wout2, "ffn2_l0", side=_Exchange(dw_a1))

    (rdo, rdproj, rdhb, drhn), rv_b0_out = _post_bwd(dh, ro, rproj, ret_hn, ret_wout, RET_H, RET_DV, 6 * D, "ret_post_bwd",
                                                     side=_Exchange(dw_b0[1:]))
    d_ret_wout = _mm_tn(rog[None], rdhb[None], D, "ret_dw_out", rows=DW_ROWS // 2).reshape(N_DEV, RET_H * RET_DV // N_DEV, D)
    (rdproj,), rv_b0_in = _ret_scan_bwd(rproj, cos, sin, lgam, rdo, rstates, rdproj, "ret_scan_bwd", side=_Exchange(dw_b0[:1]))
    rv_b0 = rv_b0_in + rv_b0_out
    d_ret_win = _mm_tn(rhn[None], rdproj[None], ret_win.shape[2], "ret_dw_in", shard_out=True)
    (dh, dgm_0), rv_ret_out = _proj_bwd(rdproj, ret_win, dh, h1, gm[0], 4 * ret_win.shape[2], "ret_proj_bwd", side=_Exchange([d_ret_wout]))

    dh, dw_a0, dg1_0, rv_ret_in, rv_a0_out = ffn_back(dh, h0, xn_a0, g1[0], pg_a0, pu_a0, win0, wout0, "ffn1_l0",
                                                      side=_Exchange([d_ret_win]), dw_side="own_dw_out")
    rv_ret = rv_ret_in + rv_ret_out
    grad_x = dh[CHUNK:][None]

    dmeta = dh[PAD:CHUNK]
    parts = jnp.concatenate([
        dg1_0, dg1_1, dgm_0[0], dgm_1[0], dg2_0, dg2_1, dfinal[0], dmeta.reshape(-1), drhn[0], dwg[:GLA_RANK].reshape(-1),
        dbg[0], dghn[0]])
    n_parts = parts.shape[0]
    rows = -(-n_parts // D)
    rows = -(-rows // 8) * 8
    parts = jnp.pad(parts, (0, rows * D - n_parts)).reshape(rows, D)
    rv_a0_in, parts_all = _run_side(_Both(_Exchange(dw_a0[:1]), _Gather([parts])), "xchg_last")
    rv_a0 = [rv_a0_in] + rv_a0_out
    tot = _small_reduce(parts_all, "small_grad_sum").reshape(-1)

    def adam_t(recvs, w, m, v, tag):
        outs = _adamw_reduce(recvs, *(jnp.swapaxes(a, 1, 2) for a in (w, m, v)), tag)
        return [jnp.swapaxes(o, 1, 2) for o in outs]

    u_ffn1_in = adam_t([rv_a0[0], rv_a1[0]], ffn1_w_in, m_ffn1_w_in, v_ffn1_w_in, "adam_ffn1_w_in")
    u_ffn2_in = adam_t([rv_b0[0], rv_b1[0]], ffn2_w_in, m_ffn2_w_in, v_ffn2_w_in, "adam_ffn2_w_in")
    u_ffn1_out = _adamw_reduce([rv_a0[1], rv_a1[1]], ffn1_w_out, m_ffn1_w_out, v_ffn1_w_out, "adam_ffn1_w_out")
    u_ffn2_out = _adamw_reduce([rv_b0[1], rv_b1[1]], ffn2_w_out, m_ffn2_w_out, v_ffn2_w_out, "adam_ffn2_w_out")
    u_ret_in = _adamw_reduce([rv_ret[0]], ret_w_in, m_ret_w_in, v_ret_w_in, "adam_ret_w_in")
    u_ret_out = _adamw_reduce([rv_ret[1]], ret_w_out, m_ret_w_out, v_ret_w_out, "adam_ret_w_out")
    u_gla_in = adam_t([rv_gla[0]], gla_w_in, m_gla_w_in, v_gla_w_in, "adam_gla_w_in")
    u_gla_out = _adamw_reduce([rv_gla[1]], gla_w_out, m_gla_w_out, v_gla_w_out, "adam_gla_w_out")


    off = 0
    def take(nel):
        nonlocal off
        out = tot[off:off + nel]
        off += nel
        return out

    gr_norm_ffn1 = take(2 * D).reshape(2, D)
    gr_norm_mix = take(2 * D).reshape(2, D)
    gr_norm_ffn2 = take(2 * D).reshape(2, D)
    gr_final = take(D)
    gr_meta = _my_cols(take(N_META * D).reshape(N_META, D), D // N_DEV)
    gr_ret_hn = _my_cols(take(RET_H * RET_DV).reshape(RET_H, RET_DV), RET_DV // N_DEV)[None]
    gr_wgate = _my_cols(take(GLA_RANK * GLA_H * GLA_DK).reshape(GLA_RANK, GLA_H * GLA_DK), GLA_H * GLA_DK // N_DEV)[None]
    gr_bgate = _my_cols(take(GLA_H * GLA_DK).reshape(1, GLA_H * GLA_DK), GLA_H * GLA_DK // N_DEV)
    gr_gla_hn = _my_cols(take(GLA_H * GLA_DV).reshape(GLA_H, GLA_DV), GLA_DV // N_DEV)[None]

    small_w = [meta_tokens, norm_ffn1, norm_mix, norm_ffn2, ret_head_norm, gla_w_gate, gla_b_gate, gla_head_norm, final_norm]
    small_g = [gr_meta, gr_norm_ffn1, gr_norm_mix, gr_norm_ffn2, gr_ret_hn, gr_wgate, gr_bgate, gr_gla_hn, gr_final]
    small_m = [m_meta_tokens, m_norm_ffn1, m_norm_mix, m_norm_ffn2, m_ret_head_norm, m_gla_w_gate, m_gla_b_gate, m_gla_head_norm, m_final_norm]
    small_v = [v_meta_tokens, v_norm_ffn1, v_norm_mix, v_norm_ffn2, v_ret_head_norm, v_gla_w_gate, v_gla_b_gate, v_gla_head_norm, v_final_norm]

    def pack(arrs):
        flat = jnp.concatenate([a.reshape(-1) for a in arrs])
        n = flat.shape[0]
        r = -(-n // 128)
        r = -(-r // 8) * 8
        return jnp.pad(flat, (0, r * 128 - n), constant_values=1.0).reshape(r, 128)

    sd, sm, sv = _adamw_small(pack(small_w), pack(small_g), pack(small_m), pack(small_v), "adam_small")

    def unpack(buf):
        flat = buf.reshape(-1)
        outs, o = [], 0
        for a in small_w:
            outs.append(flat[o:o + a.size].reshape(a.shape))
            o += a.size
        return outs

    us_d, us_m, us_v = unpack(sd), unpack(sm), unpack(sv)

    def ordered(k, smalls):
        return (smalls[0], smalls[1], u_ffn1_in[k], u_ffn1_out[k], smalls[2], smalls[3], u_ffn2_in[k], u_ffn2_out[k],
                u_ret_in[k], smalls[4], u_ret_out[k], u_gla_in[k], smalls[5], smalls[6], smalls[7], u_gla_out[k], smalls[8])

    return (loss, grad_x, *ordered(0, small_g), *ordered(1, us_d), *ordered(2, us_m), *ordered(3, us_v))
```
